```python
import jax, jax.numpy as jnp
from jax import lax
import numpy as np

D_MODEL = 2048
BATCH = 16
SEQ = 2048
DEPTH = 2

D_CONV = D_MODEL // 2
CONV_WIDTH = 31
D_POOL = D_MODEL // 2
POOL_WINDOWS = (2, 4, 8, 16)
POOL_GROUP = D_POOL // len(POOL_WINDOWS)
D_SHORT = D_MODEL
SHORT_WIDTH = 3
D_FF = -(-8 * D_MODEL // (3 * 256)) * 256
N_EVEN = (DEPTH + 1) // 2
N_ODD = DEPTH // 2
EPS = 1e-6

kernel_name = 'hybrid_conformer_pool_shortconv_block'


def rms_norm(x, g):
    xf = x.astype(jnp.float32)
    y = xf * lax.rsqrt(jnp.mean(xf * xf, axis=-1, keepdims=True) + EPS)
    return (y * g.astype(jnp.float32)).astype(x.dtype)


def layer_norm(x, g, b):
    xf = x.astype(jnp.float32)
    mu = jnp.mean(xf, axis=-1, keepdims=True)
    xc = xf - mu
    var = jnp.mean(xc * xc, axis=-1, keepdims=True)
    y = xc * lax.rsqrt(var + EPS) * g.astype(jnp.float32) + b.astype(jnp.float32)
    return y.astype(x.dtype)


def causal_depthwise_conv(x, w):
    k, c = w.shape
    return lax.conv_general_dilated(
        x, w[:, None, :].astype(x.dtype), window_strides=(1,),
        padding=[(k - 1, 0)], dimension_numbers=('NWC', 'WIO', 'NWC'),
        feature_group_count=c)


def multiscale_pool(v, w_pool, scale):
    b, s, _ = v.shape
    vf = v.astype(jnp.float32)
    cnt_pos = jnp.arange(s) + 1
    outs = []
    for g, w in enumerate(POOL_WINDOWS):
        xg = vf[..., g * POOL_GROUP:(g + 1) * POOL_GROUP]
        cs = jnp.cumsum(xg, axis=1)
        lag = jnp.pad(cs, ((0, 0), (w, 0), (0, 0)))[:, :s]
        cnt = jnp.minimum(cnt_pos, w).astype(jnp.float32)[None, :, None]
        outs.append((cs - lag) / cnt - xg)
    p = jnp.stack(outs, axis=2).astype(v.dtype)
    p = jnp.einsum('bsgc,gcd->bsgd', p, w_pool).reshape(b, s, D_POOL)
    return p * scale


def conv_pool_mixer(x, norm_g, w_in, conv_w, conv_b, ln_g, ln_b, w_pool, pool_scale, w_out):
    h = rms_norm(x, norm_g)
    u = h @ w_in
    a_val = u[..., :D_CONV]
    a_gate = u[..., D_CONV:2 * D_CONV]
    b_in = u[..., 2 * D_CONV:]
    a = a_val * jax.nn.sigmoid(a_gate)
    a = causal_depthwise_conv(a, conv_w) + conv_b
    a = jax.nn.silu(layer_norm(a, ln_g, ln_b))
    p = multiscale_pool(b_in, w_pool, pool_scale)
    return jnp.concatenate([a, p], axis=-1) @ w_out


def short_conv_mixer(x, norm_g, w_in, conv_w, w_out):
    h = rms_norm(x, norm_g)
    u = h @ w_in
    gate_b = u[..., :D_SHORT]
    gate_c = u[..., D_SHORT:2 * D_SHORT]
    v = u[..., 2 * D_SHORT:]
    y = gate_b * causal_depthwise_conv(gate_c * v, conv_w)
    return y @ w_out


def swiglu(h, w_gate, w_up, w_down):
    return (jax.nn.silu(h @ w_gate) * (h @ w_up)) @ w_down


def _normal(k, shape, fan_in):
    return jax.random.normal(k, shape, jnp.float32) * (fan_in ** -0.5)


def _fwd_setup_inputs(seed: int = 0) -> dict:
    key = jax.random.key(seed)
    ks = jax.random.split(key, 20)
    d = D_MODEL
    x = jax.random.normal(ks[0], (BATCH, SEQ, d), jnp.float32)
    mix_norm_e = 1.0 + 0.02 * jax.random.normal(ks[1], (N_EVEN, d), jnp.float32)
    w_in_e = _normal(ks[2], (N_EVEN, d, 2 * D_CONV + D_POOL), d)
    conv_w_e = _normal(ks[3], (N_EVEN, CONV_WIDTH, D_CONV), CONV_WIDTH)
    conv_b_e = 0.02 * jax.random.normal(ks[4], (N_EVEN, D_CONV), jnp.float32)
    ln_g_e = 1.0 + 0.02 * jax.random.normal(ks[5], (N_EVEN, D_CONV), jnp.float32)
    ln_b_e = 0.02 * jax.random.normal(ks[6], (N_EVEN, D_CONV), jnp.float32)
    w_pool_e = _normal(ks[7], (N_EVEN, len(POOL_WINDOWS), POOL_GROUP, POOL_GROUP), POOL_GROUP)
    pool_scale_e = 1.0 + 0.02 * jax.random.normal(ks[8], (N_EVEN, D_POOL), jnp.float32)
    w_out_e = _normal(ks[9], (N_EVEN, D_CONV + D_POOL, d), D_CONV + D_POOL)
    mix_norm_o = 1.0 + 0.02 * jax.random.normal(ks[10], (N_ODD, d), jnp.float32)
    w_in_o = _normal(ks[11], (N_ODD, d, 3 * D_SHORT), d)
    conv_w_o = _normal(ks[12], (N_ODD, SHORT_WIDTH, D_SHORT), SHORT_WIDTH)
    w_out_o = _normal(ks[13], (N_ODD, D_SHORT, d), D_SHORT)
    ffn_norm = 1.0 + 0.02 * jax.random.normal(ks[14], (DEPTH, d), jnp.float32)
    w_gate = _normal(ks[15], (DEPTH, d, D_FF), d)
    w_up = _normal(ks[16], (DEPTH, d, D_FF), d)
    w_down = _normal(ks[17], (DEPTH, D_FF, d), D_FF)
    final_norm = 1.0 + 0.02 * jax.random.normal(ks[18], (d,), jnp.float32)
    return {'x': x, 'mix_norm_e': mix_norm_e, 'w_in_e': w_in_e, 'conv_w_e': conv_w_e,
            'conv_b_e': conv_b_e, 'ln_g_e': ln_g_e, 'ln_b_e': ln_b_e, 'w_pool_e': w_pool_e,
            'pool_scale_e': pool_scale_e, 'w_out_e': w_out_e, 'mix_norm_o': mix_norm_o,
            'w_in_o': w_in_o, 'conv_w_o': conv_w_o, 'w_out_o': w_out_o, 'ffn_norm': ffn_norm,
            'w_gate': w_gate, 'w_up': w_up, 'w_down': w_down, 'final_norm': final_norm}


def _fwd_reference(x, mix_norm_e, w_in_e, conv_w_e, conv_b_e, ln_g_e, ln_b_e, w_pool_e,
              pool_scale_e, w_out_e, mix_norm_o, w_in_o, conv_w_o, w_out_o, ffn_norm,
              w_gate, w_up, w_down, final_norm):
    h = x
    for i in range(DEPTH):
        j = i // 2
        if i % 2 == 0:
            h = h + conv_pool_mixer(h, mix_norm_e[j], w_in_e[j], conv_w_e[j], conv_b_e[j],
                                    ln_g_e[j], ln_b_e[j], w_pool_e[j], pool_scale_e[j],
                                    w_out_e[j])
        else:
            h = h + short_conv_mixer(h, mix_norm_o[j], w_in_o[j], conv_w_o[j], w_out_o[j])
        h = h + swiglu(rms_norm(h, ffn_norm[i]), w_gate[i], w_up[i], w_down[i])
    return rms_norm(h, final_norm)


import jax as _jax
import jax.numpy as _jnp

TWIN_FORMAT = 'train_step'
FWD_PARAMS = ['x', 'mix_norm_e', 'w_in_e', 'conv_w_e', 'conv_b_e', 'ln_g_e', 'ln_b_e', 'w_pool_e', 'pool_scale_e', 'w_out_e', 'mix_norm_o', 'w_in_o', 'conv_w_o', 'w_out_o', 'ffn_norm', 'w_gate', 'w_up', 'w_down', 'final_norm']
TWIN_WEIGHTS = ['mix_norm_e', 'w_in_e', 'conv_w_e', 'conv_b_e', 'ln_g_e', 'ln_b_e', 'w_pool_e', 'pool_scale_e', 'w_out_e', 'mix_norm_o', 'w_in_o', 'conv_w_o', 'w_out_o', 'ffn_norm', 'w_gate', 'w_up', 'w_down', 'final_norm']
TWIN_DIFF_INPUT = 'x'
TWIN_INPUTS = ['x', 'mix_norm_e', 'w_in_e', 'conv_w_e', 'conv_b_e', 'ln_g_e', 'ln_b_e', 'w_pool_e', 'pool_scale_e', 'w_out_e', 'mix_norm_o', 'w_in_o', 'conv_w_o', 'w_out_o', 'ffn_norm', 'w_gate', 'w_up', 'w_down', 'final_norm', 'loss_target', 'm_mix_norm_e', 'm_w_in_e', 'm_conv_w_e', 'm_conv_b_e', 'm_ln_g_e', 'm_ln_b_e', 'm_w_pool_e', 'm_pool_scale_e', 'm_w_out_e', 'm_mix_norm_o', 'm_w_in_o', 'm_conv_w_o', 'm_w_out_o', 'm_ffn_norm', 'm_w_gate', 'm_w_up', 'm_w_down', 'm_final_norm', 'v_mix_norm_e', 'v_w_in_e', 'v_conv_w_e', 'v_conv_b_e', 'v_ln_g_e', 'v_ln_b_e', 'v_w_pool_e', 'v_pool_scale_e', 'v_w_out_e', 'v_mix_norm_o', 'v_w_in_o', 'v_conv_w_o', 'v_w_out_o', 'v_ffn_norm', 'v_w_gate', 'v_w_up', 'v_w_down', 'v_final_norm']
TWIN_OUTPUTS = ['loss', 'grad_x', 'grad_mix_norm_e', 'grad_w_in_e', 'grad_conv_w_e', 'grad_conv_b_e', 'grad_ln_g_e', 'grad_ln_b_e', 'grad_w_pool_e', 'grad_pool_scale_e', 'grad_w_out_e', 'grad_mix_norm_o', 'grad_w_in_o', 'grad_conv_w_o', 'grad_w_out_o', 'grad_ffn_norm', 'grad_w_gate', 'grad_w_up', 'grad_w_down', 'grad_final_norm', 'delta_mix_norm_e', 'delta_w_in_e', 'delta_conv_w_e', 'delta_conv_b_e', 'delta_ln_g_e', 'delta_ln_b_e', 'delta_w_pool_e', 'delta_pool_scale_e', 'delta_w_out_e', 'delta_mix_norm_o', 'delta_w_in_o', 'delta_conv_w_o', 'delta_w_out_o', 'delta_ffn_norm', 'delta_w_gate', 'delta_w_up', 'delta_w_down', 'delta_final_norm', 'new_m_mix_norm_e', 'new_m_w_in_e', 'new_m_conv_w_e', 'new_m_conv_b_e', 'new_m_ln_g_e', 'new_m_ln_b_e', 'new_m_w_pool_e', 'new_m_pool_scale_e', 'new_m_w_out_e', 'new_m_mix_norm_o', 'new_m_w_in_o', 'new_m_conv_w_o', 'new_m_w_out_o', 'new_m_ffn_norm', 'new_m_w_gate', 'new_m_w_up', 'new_m_w_down', 'new_m_final_norm', 'new_v_mix_norm_e', 'new_v_w_in_e', 'new_v_conv_w_e', 'new_v_conv_b_e', 'new_v_ln_g_e', 'new_v_ln_b_e', 'new_v_w_pool_e', 'new_v_pool_scale_e', 'new_v_w_out_e', 'new_v_mix_norm_o', 'new_v_w_in_o', 'new_v_conv_w_o', 'new_v_w_out_o', 'new_v_ffn_norm', 'new_v_w_gate', 'new_v_w_up', 'new_v_w_down', 'new_v_final_norm']
TWIN_LEAF_KINDS = {'loss': 'loss', 'grad_x': 'grad_x', 'grad_mix_norm_e': 'grad_w', 'grad_w_in_e': 'grad_w', 'grad_conv_w_e': 'grad_w', 'grad_conv_b_e': 'grad_w', 'grad_ln_g_e': 'grad_w', 'grad_ln_b_e': 'grad_w', 'grad_w_pool_e': 'grad_w', 'grad_pool_scale_e': 'grad_w', 'grad_w_out_e': 'grad_w', 'grad_mix_norm_o': 'grad_w', 'grad_w_in_o': 'grad_w', 'grad_conv_w_o': 'grad_w', 'grad_w_out_o': 'grad_w', 'grad_ffn_norm': 'grad_w', 'grad_w_gate': 'grad_w', 'grad_w_up': 'grad_w', 'grad_w_down': 'grad_w', 'grad_final_norm': 'grad_w', 'delta_mix_norm_e': 'delta_w', 'delta_w_in_e': 'delta_w', 'delta_conv_w_e': 'delta_w', 'delta_conv_b_e': 'delta_w', 'delta_ln_g_e': 'delta_w', 'delta_ln_b_e': 'delta_w', 'delta_w_pool_e': 'delta_w', 'delta_pool_scale_e': 'delta_w', 'delta_w_out_e': 'delta_w', 'delta_mix_norm_o': 'delta_w', 'delta_w_in_o': 'delta_w', 'delta_conv_w_o': 'delta_w', 'delta_w_out_o': 'delta_w', 'delta_ffn_norm': 'delta_w', 'delta_w_gate': 'delta_w', 'delta_w_up': 'delta_w', 'delta_w_down': 'delta_w', 'delta_final_norm': 'delta_w', 'new_m_mix_norm_e': 'new_m', 'new_m_w_in_e': 'new_m', 'new_m_conv_w_e': 'new_m', 'new_m_conv_b_e': 'new_m', 'new_m_ln_g_e': 'new_m', 'new_m_ln_b_e': 'new_m', 'new_m_w_pool_e': 'new_m', 'new_m_pool_scale_e': 'new_m', 'new_m_w_out_e': 'new_m', 'new_m_mix_norm_o': 'new_m', 'new_m_w_in_o': 'new_m', 'new_m_conv_w_o': 'new_m', 'new_m_w_out_o': 'new_m', 'new_m_ffn_norm': 'new_m', 'new_m_w_gate': 'new_m', 'new_m_w_up': 'new_m', 'new_m_w_down': 'new_m', 'new_m_final_norm': 'new_m', 'new_v_mix_norm_e': 'new_v', 'new_v_w_in_e': 'new_v', 'new_v_conv_w_e': 'new_v', 'new_v_conv_b_e': 'new_v', 'new_v_ln_g_e': 'new_v', 'new_v_ln_b_e': 'new_v', 'new_v_w_pool_e': 'new_v', 'new_v_pool_scale_e': 'new_v', 'new_v_w_out_e': 'new_v', 'new_v_mix_norm_o': 'new_v', 'new_v_w_in_o': 'new_v', 'new_v_conv_w_o': 'new_v', 'new_v_w_out_o': 'new_v', 'new_v_ffn_norm': 'new_v', 'new_v_w_gate': 'new_v', 'new_v_w_up': 'new_v', 'new_v_w_down': 'new_v', 'new_v_final_norm': 'new_v'}


def _forward(args):
    return _fwd_reference(*[args[k] for k in FWD_PARAMS])


def _output_shape():
    out = _jax.eval_shape(lambda: _forward(_fwd_setup_inputs(0)))
    return out.shape, out.dtype

N_MICROBATCH = 1
ADAM_LR = 0.001
ADAM_B1 = 0.9
ADAM_B2 = 0.999
ADAM_EPS = 1e-08
ADAM_WD = 0.01
ADAM_STEP = 10
PER_EXAMPLE_BATCH_AXIS = {'x': 0, 'loss_target': 0}
SHARED_INPUTS = []
_WEIGHT_DTYPES = {'mix_norm_e': _jnp.float32, 'w_in_e': _jnp.float32, 'conv_w_e': _jnp.float32, 'conv_b_e': _jnp.float32, 'ln_g_e': _jnp.float32, 'ln_b_e': _jnp.float32, 'w_pool_e': _jnp.float32, 'pool_scale_e': _jnp.float32, 'w_out_e': _jnp.float32, 'mix_norm_o': _jnp.float32, 'w_in_o': _jnp.float32, 'conv_w_o': _jnp.float32, 'w_out_o': _jnp.float32, 'ffn_norm': _jnp.float32, 'w_gate': _jnp.float32, 'w_up': _jnp.float32, 'w_down': _jnp.float32, 'final_norm': _jnp.float32}
MOMENT_SCALE = {'mix_norm_e': 8.644336e-02, 'w_in_e': 6.766393e-02, 'conv_w_e': 6.580863e-02, 'conv_b_e': 1.466231e-01, 'ln_g_e': 8.110635e-02, 'ln_b_e': 6.602005e-02, 'w_pool_e': 9.342980e-02, 'pool_scale_e': 9.468483e-02, 'w_out_e': 7.977283e-02, 'mix_norm_o': 9.631109e-02, 'w_in_o': 5.449370e-02, 'conv_w_o': 5.483155e-02, 'w_out_o': 5.452458e-02, 'ffn_norm': 6.137619e-02, 'w_gate': 2.633920e-02, 'w_up': 2.553570e-02, 'w_down': 4.228887e-02, 'final_norm': 1.599322e+01}


def _to_microbatches(a, axis):
    t = _jnp.moveaxis(a, axis, 0)
    t = t.reshape((N_MICROBATCH, t.shape[0] // N_MICROBATCH) + t.shape[1:])
    return _jnp.moveaxis(t, 1, axis + 1)


def setup_inputs(seed: int = 0) -> dict:
    inp = _fwd_setup_inputs(seed)
    key = _jax.random.fold_in(_jax.random.key(seed), 7919)
    shape, _ = _output_shape()
    out = dict(inp)
    out["loss_target"] = _jax.random.normal(_jax.random.fold_in(key, 0), shape, _jnp.float32)
    for i, name in enumerate(TWIN_WEIGHTS):
        w = inp[name].astype(_jnp.float32)
        if MOMENT_SCALE is None:
            s = _jnp.sqrt(_jnp.mean(_jnp.square(w)) + 1e-30)
        else:
            s = MOMENT_SCALE[name]
        km, kv = _jax.random.split(_jax.random.fold_in(key, i + 1))
        out[name] = w
        out["m_" + name] = s * _jax.random.normal(km, w.shape, _jnp.float32)
        out["v_" + name] = (s * s) * _jax.random.uniform(kv, w.shape, _jnp.float32, 0.5, 1.5)
    if N_MICROBATCH > 1:
        for name, axis in PER_EXAMPLE_BATCH_AXIS.items():
            out[name] = _to_microbatches(out[name], axis)
    return {'x': out['x'], 'mix_norm_e': out['mix_norm_e'], 'w_in_e': out['w_in_e'], 'conv_w_e': out['conv_w_e'], 'conv_b_e': out['conv_b_e'], 'ln_g_e': out['ln_g_e'], 'ln_b_e': out['ln_b_e'], 'w_pool_e': out['w_pool_e'], 'pool_scale_e': out['pool_scale_e'], 'w_out_e': out['w_out_e'], 'mix_norm_o': out['mix_norm_o'], 'w_in_o': out['w_in_o'], 'conv_w_o': out['conv_w_o'], 'w_out_o': out['w_out_o'], 'ffn_norm': out['ffn_norm'], 'w_gate': out['w_gate'], 'w_up': out['w_up'], 'w_down': out['w_down'], 'final_norm': out['final_norm'], 'loss_target': out['loss_target'], 'm_mix_norm_e': out['m_mix_norm_e'], 'm_w_in_e': out['m_w_in_e'], 'm_conv_w_e': out['m_conv_w_e'], 'm_conv_b_e': out['m_conv_b_e'], 'm_ln_g_e': out['m_ln_g_e'], 'm_ln_b_e': out['m_ln_b_e'], 'm_w_pool_e': out['m_w_pool_e'], 'm_pool_scale_e': out['m_pool_scale_e'], 'm_w_out_e': out['m_w_out_e'], 'm_mix_norm_o': out['m_mix_norm_o'], 'm_w_in_o': out['m_w_in_o'], 'm_conv_w_o': out['m_conv_w_o'], 'm_w_out_o': out['m_w_out_o'], 'm_ffn_norm': out['m_ffn_norm'], 'm_w_gate': out['m_w_gate'], 'm_w_up': out['m_w_up'], 'm_w_down': out['m_w_down'], 'm_final_norm': out['m_final_norm'], 'v_mix_norm_e': out['v_mix_norm_e'], 'v_w_in_e': out['v_w_in_e'], 'v_conv_w_e': out['v_conv_w_e'], 'v_conv_b_e': out['v_conv_b_e'], 'v_ln_g_e': out['v_ln_g_e'], 'v_ln_b_e': out['v_ln_b_e'], 'v_w_pool_e': out['v_w_pool_e'], 'v_pool_scale_e': out['v_pool_scale_e'], 'v_w_out_e': out['v_w_out_e'], 'v_mix_norm_o': out['v_mix_norm_o'], 'v_w_in_o': out['v_w_in_o'], 'v_conv_w_o': out['v_conv_w_o'], 'v_w_out_o': out['v_w_out_o'], 'v_ffn_norm': out['v_ffn_norm'], 'v_w_gate': out['v_w_gate'], 'v_w_up': out['v_w_up'], 'v_w_down': out['v_w_down'], 'v_final_norm': out['v_final_norm']}


def _loss(weights, diff, rest, loss_target):
    with _jax.named_scope("forward"):
        args = {**rest, TWIN_DIFF_INPUT: diff, **{k: w.astype(_WEIGHT_DTYPES[k]) for k, w in weights.items()}}
        y = _forward(args)
    with _jax.named_scope("loss_head"):
        err = _jnp.square(y.astype(_jnp.float32) - loss_target)
        return 0.5 * _jnp.sum(_jnp.mean(err, axis=-1)) if err.ndim else 0.5 * err


def _adamw(w, g, m, v):
    m = ADAM_B1 * m + (1.0 - ADAM_B1) * g
    v = ADAM_B2 * v + (1.0 - ADAM_B2) * _jnp.square(g)
    m_hat = m / (1.0 - ADAM_B1 ** ADAM_STEP)
    v_hat = v / (1.0 - ADAM_B2 ** ADAM_STEP)
    delta = -ADAM_LR * (m_hat / (_jnp.sqrt(v_hat) + ADAM_EPS) + ADAM_WD * w)
    return delta, m, v


def reference(x, mix_norm_e, w_in_e, conv_w_e, conv_b_e, ln_g_e, ln_b_e, w_pool_e, pool_scale_e, w_out_e, mix_norm_o, w_in_o, conv_w_o, w_out_o, ffn_norm, w_gate, w_up, w_down, final_norm, loss_target, m_mix_norm_e, m_w_in_e, m_conv_w_e, m_conv_b_e, m_ln_g_e, m_ln_b_e, m_w_pool_e, m_pool_scale_e, m_w_out_e, m_mix_norm_o, m_w_in_o, m_conv_w_o, m_w_out_o, m_ffn_norm, m_w_gate, m_w_up, m_w_down, m_final_norm, v_mix_norm_e, v_w_in_e, v_conv_w_e, v_conv_b_e, v_ln_g_e, v_ln_b_e, v_w_pool_e, v_pool_scale_e, v_w_out_e, v_mix_norm_o, v_w_in_o, v_conv_w_o, v_w_out_o, v_ffn_norm, v_w_gate, v_w_up, v_w_down, v_final_norm):
    given = dict(x=x, mix_norm_e=mix_norm_e, w_in_e=w_in_e, conv_w_e=conv_w_e, conv_b_e=conv_b_e, ln_g_e=ln_g_e, ln_b_e=ln_b_e, w_pool_e=w_pool_e, pool_scale_e=pool_scale_e, w_out_e=w_out_e, mix_norm_o=mix_norm_o, w_in_o=w_in_o, conv_w_o=conv_w_o, w_out_o=w_out_o, ffn_norm=ffn_norm, w_gate=w_gate, w_up=w_up, w_down=w_down, final_norm=final_norm, loss_target=loss_target, m_mix_norm_e=m_mix_norm_e, m_w_in_e=m_w_in_e, m_conv_w_e=m_conv_w_e, m_conv_b_e=m_conv_b_e, m_ln_g_e=m_ln_g_e, m_ln_b_e=m_ln_b_e, m_w_pool_e=m_w_pool_e, m_pool_scale_e=m_pool_scale_e, m_w_out_e=m_w_out_e, m_mix_norm_o=m_mix_norm_o, m_w_in_o=m_w_in_o, m_conv_w_o=m_conv_w_o, m_w_out_o=m_w_out_o, m_ffn_norm=m_ffn_norm, m_w_gate=m_w_gate, m_w_up=m_w_up, m_w_down=m_w_down, m_final_norm=m_final_norm, v_mix_norm_e=v_mix_norm_e, v_w_in_e=v_w_in_e, v_conv_w_e=v_conv_w_e, v_conv_b_e=v_conv_b_e, v_ln_g_e=v_ln_g_e, v_ln_b_e=v_ln_b_e, v_w_pool_e=v_w_pool_e, v_pool_scale_e=v_pool_scale_e, v_w_out_e=v_w_out_e, v_mix_norm_o=v_mix_norm_o, v_w_in_o=v_w_in_o, v_conv_w_o=v_conv_w_o, v_w_out_o=v_w_out_o, v_ffn_norm=v_ffn_norm, v_w_gate=v_w_gate, v_w_up=v_w_up, v_w_down=v_w_down, v_final_norm=v_final_norm)
    weights = {n: given[n] for n in TWIN_WEIGHTS}
    shared = {n: given[n] for n in SHARED_INPUTS}
    per_example = {n: given[n] for n in ['x']}
    grad_fn = _jax.value_and_grad(_loss, argnums=(0, 1))

    def one_microbatch(ex, loss_target):
        ex = dict(ex)
        diff = ex.pop(TWIN_DIFF_INPUT)
        return grad_fn(weights, diff, {**shared, **ex}, loss_target)

    if N_MICROBATCH == 1:
        loss, (grad_w, grad_x) = one_microbatch(per_example, given["loss_target"])
    else:
        def body(carry, xs):
            loss_sum, grad_sum = carry
            l_k, (gw_k, gx_k) = one_microbatch(xs[0], xs[1])
            with _jax.named_scope("update"):
                return (loss_sum + l_k, _jax.tree.map(_jnp.add, grad_sum, gw_k)), gx_k

        init = (_jnp.zeros((), _jnp.float32), _jax.tree.map(_jnp.zeros_like, weights))
        (loss, grad_w), grad_x = _jax.lax.scan(body, init, (per_example, given["loss_target"]))
    with _jax.named_scope("update"):
        delta_w, new_m, new_v = {}, {}, {}
        for n in TWIN_WEIGHTS:
            delta_w[n], new_m[n], new_v[n] = _adamw(weights[n], grad_w[n], given["m_" + n], given["v_" + n])
    return (loss, grad_x, *[grad_w[n] for n in TWIN_WEIGHTS], *[delta_w[n] for n in TWIN_WEIGHTS],
            *[new_m[n] for n in TWIN_WEIGHTS], *[new_v[n] for n in TWIN_WEIGHTS])
```

```python
import functools

import jax
import jax.numpy as jnp
from jax import lax
from jax.experimental import pallas as pl
from jax.experimental.pallas import tpu as pltpu

F32 = jnp.float32
BF16 = jnp.bfloat16
NDEV = 8
MESH_AXES = ("x", "y", "c")
EPS = 1e-6
POOL_WINDOWS = (2, 4, 8, 16)
CONV_WIDTH = 31
SHORT_WIDTH = 3
ADAM_LR = 0.001
ADAM_B1 = 0.9
ADAM_B2 = 0.999
ADAM_EPS = 1e-08
ADAM_WD = 0.01
ADAM_STEP = 10
LANES = 128
SUBLANES = 8
VMEM_LIMIT = 56 * 1024 * 1024
MESH = pl.DeviceIdType.MESH
ANY = pl.BlockSpec(memory_space=pl.ANY)


def _cp(*sem):
    return pltpu.CompilerParams(dimension_semantics=sem, vmem_limit_bytes=VMEM_LIMIT)


def _tile(n, pref, unit=LANES):
    if n <= pref:
        return n
    t = (pref // unit) * unit
    while t > unit and n % t:
        t -= unit
    assert n % t == 0, (n, pref)
    return t


def _sigmoid(v):
    return 1.0 / (1.0 + jnp.exp(-v))


def _mm(name, pairs, a_specs, b_specs, dims, out_shape, o_spec, grid, acc_shape,
        res=None, res_spec=None):
    np_ = len(pairs)
    nk = grid[2]
    has_res = res is not None

    def body(*refs):
        a_refs = refs[:np_]
        b_refs = refs[np_:2 * np_]
        r_ref = refs[2 * np_] if has_res else None
        o_ref = refs[2 * np_ + (1 if has_res else 0)]
        acc = refs[-1]

        def part():
            s = None
            for a_ref, b_ref in zip(a_refs, b_refs):
                d = lax.dot_general(a_ref[...], b_ref[...], dims, preferred_element_type=F32)
                s = d if s is None else s + d
            return s

        def finish(v):
            if has_res:
                v = v + r_ref[...]
            o_ref[...] = v.astype(o_ref.dtype)

        if nk == 1:
            finish(part())
        else:
            k = pl.program_id(2)

            @pl.when(k == 0)
            def _():
                acc[...] = part()

            @pl.when(k > 0)
            def _():
                acc[...] += part()

            @pl.when(k == nk - 1)
            def _():
                finish(acc[...])

    ins = [p[0] for p in pairs] + [p[1] for p in pairs]
    specs = list(a_specs) + list(b_specs)
    if has_res:
        ins.append(res)
        specs.append(res_spec)
    return pl.pallas_call(
        body, name=name, grid=grid, in_specs=specs, out_specs=o_spec, out_shape=out_shape,
        scratch_shapes=[pltpu.VMEM(acc_shape if nk > 1 else (SUBLANES, LANES), F32)],
        compiler_params=_cp("parallel", "parallel", "arbitrary"))(*ins)


NN = (((1,), (0,)), ((), ()))
NT = (((1,), (1,)), ((), ()))
TN = (((0,), (0,)), ((), ()))


def _mm_nn(name, a, b, out_dtype, res=None):
    m, kk = a.shape
    n = b.shape[1]
    tm, tn, tk = _tile(m, 1024), _tile(n, 1024), _tile(kk, 1024)
    return _mm(name, [(a, b)],
               [pl.BlockSpec((tm, tk), lambda i, j, k: (i, k))],
               [pl.BlockSpec((tk, tn), lambda i, j, k: (k, j))], NN,
               jax.ShapeDtypeStruct((m, n), out_dtype),
               pl.BlockSpec((tm, tn), lambda i, j, k: (i, j)),
               (m // tm, n // tn, kk // tk), (tm, tn), res,
               pl.BlockSpec((tm, tn), lambda i, j, k: (i, j)))


def _mm_nt(name, a, b, out_dtype):
    m, n = a.shape
    kk = b.shape[0]
    tm, tn, tk = _tile(m, 1024), _tile(kk, 1024), _tile(n, 1024)
    return _mm(name, [(a, b)],
               [pl.BlockSpec((tm, tk), lambda i, j, k: (i, k))],
               [pl.BlockSpec((tn, tk), lambda i, j, k: (j, k))], NT,
               jax.ShapeDtypeStruct((m, kk), out_dtype),
               pl.BlockSpec((tm, tn), lambda i, j, k: (i, j)),
               (m // tm, kk // tn, n // tk), (tm, tn))


def _mm_tn(name, a, b, out_dtype):
    t, m = a.shape
    n = b.shape[1]
    tm, tn, tk = _tile(m, 1024), _tile(n, 1024), _tile(t, 1024)
    return _mm(name, [(a, b)],
               [pl.BlockSpec((tk, tm), lambda i, j, k: (k, i))],
               [pl.BlockSpec((tk, tn), lambda i, j, k: (k, j))], TN,
               jax.ShapeDtypeStruct((m, n), out_dtype),
               pl.BlockSpec((tm, tn), lambda i, j, k: (i, j)),
               (m // tm, n // tn, t // tk), (tm, tn))


def _mm_down(name, act, wd, res):
    nb, t, f = act.shape
    d = wd.shape[2]
    tm, tn = _tile(t, 1024), _tile(d, 1024)
    return _mm(name, [(act, wd)],
               [pl.BlockSpec((None, tm, f), lambda i, j, k: (k, i, 0))],
               [pl.BlockSpec((None, f, tn), lambda i, j, k: (k, 0, j))], NN,
               jax.ShapeDtypeStruct((t, d), F32),
               pl.BlockSpec((tm, tn), lambda i, j, k: (i, j)),
               (t // tm, d // tn, nb), (tm, tn), res,
               pl.BlockSpec((tm, tn), lambda i, j, k: (i, j)))


def _mm_ffn_dn(name, dg, wg, dup, wu):
    nb, t, f = dg.shape
    d = wg.shape[1]
    tm, tn = _tile(t, 1024), _tile(d, 1024)
    a_spec = pl.BlockSpec((None, tm, f), lambda i, j, k: (k, i, 0))
    b_spec = pl.BlockSpec((None, tn, f), lambda i, j, k: (k, j, 0))
    return _mm(name, [(dg, wg), (dup, wu)], [a_spec, a_spec], [b_spec, b_spec], NT,
               jax.ShapeDtypeStruct((t, d), BF16),
               pl.BlockSpec((tm, tn), lambda i, j, k: (i, j)),
               (t // tm, d // tn, nb), (tm, tn))


def _mm_dwd(name, act, dh):
    nb, t, f = act.shape
    d = dh.shape[1]
    tn, tk = _tile(d, 1024), _tile(t, 1024)
    return _mm(name, [(act, dh)],
               [pl.BlockSpec((None, tk, f), lambda i, j, k: (i, k, 0))],
               [pl.BlockSpec((tk, tn), lambda i, j, k: (k, j))], TN,
               jax.ShapeDtypeStruct((nb, f, d), BF16),
               pl.BlockSpec((None, f, tn), lambda i, j, k: (i, 0, j)),
               (nb, d // tn, t // tk), (f, tn))


def _mm_dwg(name, n, dg):
    nb, t, f = dg.shape
    d = n.shape[1]
    tm, tk = _tile(d, 1024), _tile(t, 1024)
    return _mm(name, [(n, dg)],
               [pl.BlockSpec((tk, tm), lambda i, j, k: (k, i))],
               [pl.BlockSpec((None, tk, f), lambda i, j, k: (j, k, 0))], TN,
               jax.ShapeDtypeStruct((nb, d, f), BF16),
               pl.BlockSpec((None, tm, f), lambda i, j, k: (j, i, 0)),
               (d // tm, nb, t // tk), (tm, f))


def _ffn_fwd(name, n, wg, wu):
    nb, d, f = wg.shape
    t = n.shape[0]
    tm = _tile(t, 512)

    def body(n_ref, wg_ref, wu_ref, g_ref, up_ref, act_ref):
        nv = n_ref[...]
        g = jnp.dot(nv, wg_ref[...], preferred_element_type=F32)
        up = jnp.dot(nv, wu_ref[...], preferred_element_type=F32)
        g_ref[...] = g.astype(BF16)
        up_ref[...] = up.astype(BF16)
        act_ref[...] = (g * _sigmoid(g) * up).astype(BF16)

    w_spec = pl.BlockSpec((None, d, f), lambda j, i: (j, 0, 0))
    o_spec = pl.BlockSpec((None, tm, f), lambda j, i: (j, i, 0))
    shp = jax.ShapeDtypeStruct((nb, t, f), BF16)
    return pl.pallas_call(
        body, name=name, grid=(nb, t // tm),
        in_specs=[pl.BlockSpec((tm, d), lambda j, i: (i, 0)), w_spec, w_spec],
        out_specs=[o_spec, o_spec, o_spec], out_shape=[shp, shp, shp],
        compiler_params=_cp("parallel", "parallel"))(n, wg, wu)


def _ffn_bwd_act(name, dh, wd, g, up):
    nb, f, d = wd.shape
    t = dh.shape[0]
    tm = _tile(t, 512)

    def body(dh_ref, wd_ref, g_ref, up_ref, dg_ref, dup_ref):
        da = lax.dot_general(dh_ref[...], wd_ref[...], NT, preferred_element_type=F32)
        gv = g_ref[...].astype(F32)
        uv = up_ref[...].astype(F32)
        sg = _sigmoid(gv)
        dg_ref[...] = (da * uv * (sg * (1.0 + gv * (1.0 - sg)))).astype(BF16)
        dup_ref[...] = (da * gv * sg).astype(BF16)

    o_spec = pl.BlockSpec((None, tm, f), lambda j, i: (j, i, 0))
    shp = jax.ShapeDtypeStruct((nb, t, f), BF16)
    return pl.pallas_call(
        body, name=name, grid=(nb, t // tm),
        in_specs=[pl.BlockSpec((tm, d), lambda j, i: (i, 0)),
                  pl.BlockSpec((None, f, d), lambda j, i: (j, 0, 0)), o_spec, o_spec],
        out_specs=[o_spec, o_spec], out_shape=[shp, shp],
        compiler_params=_cp("parallel", "parallel"))(dh, wd, g, up)


def _rms_fwd(name, h, gain):
    t, d = h.shape
    tr = _tile(t, 512, SUBLANES)

    def body(h_ref, g_ref, n_ref):
        hv = h_ref[...]
        r = lax.rsqrt(jnp.mean(hv * hv, axis=-1, keepdims=True) + EPS)
        n_ref[...] = (hv * r * g_ref[...]).astype(BF16)

    return pl.pallas_call(
        body, name=name, grid=(t // tr,),
        in_specs=[pl.BlockSpec((tr, d), lambda i: (i, 0)), pl.BlockSpec((1, d), lambda i: (0, 0))],
        out_specs=pl.BlockSpec((tr, d), lambda i: (i, 0)),
        out_shape=jax.ShapeDtypeStruct((t, d), BF16),
        compiler_params=_cp("parallel"))(h, gain)


def _rms_bwd_math(hv, gain, dn):
    d = hv.shape[-1]
    r = lax.rsqrt(jnp.mean(hv * hv, axis=-1, keepdims=True) + EPS)
    xhat = hv * r
    dxh = dn * gain
    dh = r * (dxh - xhat * (jnp.sum(dxh * xhat, axis=-1, keepdims=True) / d))
    dgain = jnp.sum(dn * xhat, axis=0, keepdims=True)
    return dh, dgain


def _rms_bwd(name, h, gain, dn, dres):
    t, d = h.shape
    tr = _tile(t, 256, SUBLANES)

    def body(h_ref, g_ref, dn_ref, dr_ref, dh_ref, dhb_ref, dg_ref):
        dh, dgain = _rms_bwd_math(h_ref[...], g_ref[...], dn_ref[...].astype(F32))
        dh = dh + dr_ref[...]
        dh_ref[...] = dh
        dhb_ref[...] = dh.astype(BF16)

        @pl.when(pl.program_id(0) == 0)
        def _():
            dg_ref[...] = dgain

        @pl.when(pl.program_id(0) > 0)
        def _():
            dg_ref[...] += dgain

    row = pl.BlockSpec((tr, d), lambda i: (i, 0))
    vec = pl.BlockSpec((1, d), lambda i: (0, 0))
    return pl.pallas_call(
        body, name=name, grid=(t // tr,), in_specs=[row, vec, row, row],
        out_specs=[row, row, vec],
        out_shape=[jax.ShapeDtypeStruct((t, d), F32), jax.ShapeDtypeStruct((t, d), BF16),
                   jax.ShapeDtypeStruct((1, d), F32)],
        compiler_params=_cp("arbitrary"))(h, gain, dn, dres)


def _loss_head(name, h, gain, tgt):
    t, d = h.shape
    tr = _tile(t, 256, SUBLANES)

    def body(h_ref, g_ref, t_ref, dh_ref, dhb_ref, dg_ref, ls_ref):
        hv = h_ref[...]
        gv = g_ref[...]
        r = lax.rsqrt(jnp.mean(hv * hv, axis=-1, keepdims=True) + EPS)
        err = hv * r * gv - t_ref[...]
        lsum = 0.5 * jnp.sum(err * err, axis=0, keepdims=True) / d
        dh, dgain = _rms_bwd_math(hv, gv, err / d)
        dh_ref[...] = dh
        dhb_ref[...] = dh.astype(BF16)

        @pl.when(pl.program_id(0) == 0)
        def _():
            dg_ref[...] = dgain
            ls_ref[...] = lsum

        @pl.when(pl.program_id(0) > 0)
        def _():
            dg_ref[...] += dgain
            ls_ref[...] += lsum

    row = pl.BlockSpec((tr, d), lambda i: (i, 0))
    vec = pl.BlockSpec((1, d), lambda i: (0, 0))
    return pl.pallas_call(
        body, name=name, grid=(t // tr,), in_specs=[row, vec, row],
        out_specs=[row, row, vec, vec],
        out_shape=[jax.ShapeDtypeStruct((t, d), F32), jax.ShapeDtypeStruct((t, d), BF16),
                   jax.ShapeDtypeStruct((1, d), F32), jax.ShapeDtypeStruct((1, d), F32)],
        compiler_params=_cp("arbitrary"))(h, gain, tgt)


def _conv_geom(t, seq, c, k):
    halo = 32 if k - 1 > SUBLANES else SUBLANES
    assert k - 1 <= halo
    tm = min(256, seq // 2)
    tc = min(512, c)
    assert seq % tm == 0 and tm % halo == 0 and c % tc == 0 and t % seq == 0
    return halo, tm, tc, min(128, tm), min(LANES, tc)


def _pre(kind, a, b):
    if kind == "glu":
        return a * _sigmoid(b)
    if kind == "mul":
        return a * b
    return a


def _taps(k):
    return [(s % SUBLANES, s // SUBLANES, s) for s in range(k)]


def _conv_fwd(name, seq, c, w, x1, c1, x2=None, c2=0, pre=None, bias=None, post=None, cpost=0):
    t = x1.shape[0]
    k = w.shape[0]
    halo, tm, tc, sr, sl = _conv_geom(t, seq, c, k)
    nb, cps = tm // halo, seq // tm
    two = x2 is not None
    has_bias, has_post = bias is not None, post is not None

    def body(*refs):
        it = iter(refs)
        x1c, x1h = next(it), next(it)
        x2c, x2h = (next(it), next(it)) if two else (None, None)
        w_ref = next(it)
        b_ref = next(it) if has_bias else None
        p_ref = next(it) if has_post else None
        o_ref = next(it)
        y_ref = next(it) if has_post else None
        xs = next(it)
        first = (pl.program_id(1) % cps) == 0
        hv = _pre(pre, x1h[...].astype(F32), x2h[...].astype(F32) if two else None)
        xs[0:halo, :] = jnp.where(first, 0.0, hv)
        xs[halo:halo + tm, :] = _pre(pre, x1c[...].astype(F32), x2c[...].astype(F32) if two else None)
        for l0 in range(0, tc, sl):
            ls = slice(l0, l0 + sl)
            for r0 in range(0, tm, sr):
                win = xs[r0:r0 + sr + halo, ls]
                acc = jnp.zeros((sr, sl), F32)
                rolled = {}
                for r, q, s in _taps(k):
                    if r not in rolled:
                        rolled[r] = win if r == 0 else pltpu.roll(win, r, 0)
                    lo = halo - SUBLANES * q
                    acc = acc + w_ref[k - 1 - s:k - s, ls] * rolled[r][lo:lo + sr]
                if has_bias:
                    acc = acc + b_ref[:, ls]
                o_ref[r0:r0 + sr, ls] = acc.astype(o_ref.dtype)
                if has_post:
                    y_ref[r0:r0 + sr, ls] = (acc * p_ref[r0:r0 + sr, ls].astype(F32)).astype(y_ref.dtype)

    def cur(off):
        return pl.BlockSpec((tm, tc), lambda j, i: (i, off // tc + j))

    def prev(off):
        return pl.BlockSpec((halo, tc), lambda j, i: (jnp.maximum(i * nb - 1, 0), off // tc + j))

    ins, specs = [x1, x1], [cur(c1), prev(c1)]
    if two:
        ins += [x2, x2]
        specs += [cur(c2), prev(c2)]
    ins.append(w)
    specs.append(pl.BlockSpec((k, tc), lambda j, i: (0, j)))
    if has_bias:
        ins.append(bias)
        specs.append(pl.BlockSpec((1, tc), lambda j, i: (0, j)))
    if has_post:
        ins.append(post)
        specs.append(cur(cpost))
    o_spec = pl.BlockSpec((tm, tc), lambda j, i: (i, j))
    shp = jax.ShapeDtypeStruct((t, c), BF16)
    return pl.pallas_call(
        body, name=name, grid=(c // tc, t // tm), in_specs=specs,
        out_specs=[o_spec, o_spec] if has_post else o_spec,
        out_shape=[shp, shp] if has_post else shp,
        scratch_shapes=[pltpu.VMEM((halo + tm, tc), F32)],
        compiler_params=_cp("parallel", "parallel"))(*ins)


def _conv_bwd(name, seq, c, w, d1, cd1, d2=None, cd2=0, dpre=None,
              x1=None, c1=0, x2=None, c2=0, pre=None):
    t = d1.shape[0]
    k = w.shape[0]
    halo, tm, tc, sr, sl = _conv_geom(t, seq, c, k)
    nb, cps = tm // halo, seq // tm
    nchunks = t // tm
    dtwo, xtwo, has_x = d2 is not None, x2 is not None, x1 is not None

    def body(*refs):
        it = iter(refs)
        d1c, d1n = next(it), next(it)
        d2c, d2n = (next(it), next(it)) if dtwo else (None, None)
        x1c, x1h = (next(it), next(it)) if has_x else (None, None)
        x2c, x2h = (next(it), next(it)) if xtwo else (None, None)
        w_ref = next(it)
        dx_ref = next(it)
        dw_ref = next(it) if has_x else None
        ds = next(it)
        xs = next(it) if has_x else None
        i = pl.program_id(1)
        last = (i % cps) == cps - 1
        ds[0:tm, :] = _pre(dpre, d1c[...].astype(F32), d2c[...].astype(F32) if dtwo else None)
        nv = _pre(dpre, d1n[...].astype(F32), d2n[...].astype(F32) if dtwo else None)
        ds[tm:tm + halo, :] = jnp.where(last, 0.0, nv)
        if has_x:
            first = (i % cps) == 0
            hv = _pre(pre, x1h[...].astype(F32), x2h[...].astype(F32) if xtwo else None)
            xs[0:halo, :] = jnp.where(first, 0.0, hv)
            xs[halo:halo + tm, :] = _pre(pre, x1c[...].astype(F32), x2c[...].astype(F32) if xtwo else None)

            @pl.when(i == 0)
            def _():
                dw_ref[...] = jnp.zeros_like(dw_ref)

        for l0 in range(0, tc, sl):
            ls = slice(l0, l0 + sl)
            for r0 in range(0, tm, sr):
                win = ds[r0:r0 + sr + halo, ls]
                nrow = sr + halo
                acc = jnp.zeros((sr, sl), F32)
                rolled = {}
                for r, q, s in _taps(k):
                    if r not in rolled:
                        rolled[r] = win if r == 0 else pltpu.roll(win, nrow - r, 0)
                    lo = SUBLANES * q
                    acc = acc + w_ref[k - 1 - s:k - s, ls] * rolled[r][lo:lo + sr]
                dx_ref[r0:r0 + sr, ls] = acc.astype(dx_ref.dtype)
                if has_x:
                    dcur = win[0:sr]
                    xwin = xs[r0:r0 + sr + halo, ls]
                    xrolled = {}
                    for r, q, s in _taps(k):
                        if r not in xrolled:
                            xrolled[r] = xwin if r == 0 else pltpu.roll(xwin, r, 0)
                        lo = halo - SUBLANES * q
                        part = jnp.sum(dcur * xrolled[r][lo:lo + sr], axis=0, keepdims=True)
                        dw_ref[k - 1 - s:k - s, ls] += part

    def cur(off):
        return pl.BlockSpec((tm, tc), lambda j, i: (i, off // tc + j))

    def prev(off):
        return pl.BlockSpec((halo, tc), lambda j, i: (jnp.maximum(i * nb - 1, 0), off // tc + j))

    def nxt(off):
        return pl.BlockSpec((halo, tc),
                            lambda j, i: (jnp.minimum((i + 1) * nb, nchunks * nb - 1), off // tc + j))

    ins, specs = [d1, d1], [cur(cd1), nxt(cd1)]
    if dtwo:
        ins += [d2, d2]
        specs += [cur(cd2), nxt(cd2)]
    if has_x:
        ins += [x1, x1]
        specs += [cur(c1), prev(c1)]
    if xtwo:
        ins += [x2, x2]
        specs += [cur(c2), prev(c2)]
    ins.append(w)
    specs.append(pl.BlockSpec((k, tc), lambda j, i: (0, j)))
    o_specs = [pl.BlockSpec((tm, tc), lambda j, i: (i, j))]
    o_shapes = [jax.ShapeDtypeStruct((t, c), BF16)]
    scratch = [pltpu.VMEM((tm + halo, tc), F32)]
    if has_x:
        o_specs.append(pl.BlockSpec((k, tc), lambda j, i: (0, j)))
        o_shapes.append(jax.ShapeDtypeStruct((k, c), F32))
        scratch.append(pltpu.VMEM((halo + tm, tc), F32))
    out = pl.pallas_call(
        body, name=name, grid=(c // tc, t // tm), in_specs=specs, out_specs=o_specs,
        out_shape=o_shapes, scratch_shapes=scratch,
        compiler_params=_cp("parallel", "arbitrary"))(*ins)
    return out if has_x else out[0]


def _pool_taps(c):
    kmax = max(POOL_WINDOWS)
    grp = c // len(POOL_WINDOWS)
    cols = []
    for wdw in POOL_WINDOWS:
        col = jnp.concatenate([jnp.zeros((kmax - wdw,), F32), jnp.ones((wdw,), F32)])
        cols.append(jnp.tile(col[:, None], (1, grp)))
    return jnp.concatenate(cols, axis=1)


def _counts(i, tr, seq, grp):
    pos = (i * tr + lax.broadcasted_iota(jnp.int32, (tr, 1), 0)) % seq + 1
    return [1.0 / jnp.minimum(pos, wdw).astype(F32) for wdw in POOL_WINDOWS]


def _ln_stats(a2):
    mu = jnp.mean(a2, axis=-1, keepdims=True)
    xc = a2 - mu
    rstd = lax.rsqrt(jnp.mean(xc * xc, axis=-1, keepdims=True) + EPS)
    return xc * rstd, rstd


def _even_fwd(name, seq, a2, ws, u, ln_g, ln_b, w_pool, scale):
    t, c = a2.shape
    ng = len(POOL_WINDOWS)
    grp = c // ng
    tr = _tile(t, 256, SUBLANES)

    def body(a_ref, ws_ref, b_ref, g_ref, bb_ref, wp_ref, sc_ref, z_ref, pm_ref):
        xhat, _ = _ln_stats(a_ref[...].astype(F32))
        l = xhat * g_ref[...] + bb_ref[...]
        z_ref[:, 0:c] = (l * _sigmoid(l)).astype(BF16)
        inv = _counts(pl.program_id(0), tr, seq, grp)
        for g in range(ng):
            gs = slice(g * grp, (g + 1) * grp)
            pm = (ws_ref[:, gs].astype(F32) * inv[g] - b_ref[:, gs].astype(F32)).astype(BF16)
            pm_ref[:, gs] = pm
            q = jnp.dot(pm, wp_ref[g], preferred_element_type=F32)
            z_ref[:, c + g * grp:c + (g + 1) * grp] = (q * sc_ref[:, gs]).astype(BF16)

    row = pl.BlockSpec((tr, c), lambda i: (i, 0))
    vec = pl.BlockSpec((1, c), lambda i: (0, 0))
    return pl.pallas_call(
        body, name=name, grid=(t // tr,),
        in_specs=[row, row, pl.BlockSpec((tr, c), lambda i: (i, 2)), vec, vec,
                  pl.BlockSpec((ng, grp, grp), lambda i: (0, 0, 0)), vec],
        out_specs=[pl.BlockSpec((tr, 2 * c), lambda i: (i, 0)), row],
        out_shape=[jax.ShapeDtypeStruct((t, 2 * c), BF16), jax.ShapeDtypeStruct((t, c), BF16)],
        compiler_params=_cp("parallel"))(a2, ws, u, ln_g, ln_b, w_pool, scale)


def _even_bwd(name, seq, dz, a2, pm, ln_g, ln_b, w_pool, scale):
    t, c = a2.shape
    ng = len(POOL_WINDOWS)
    grp = c // ng
    tr = _tile(t, 256, SUBLANES)

    def body(dz_ref, a_ref, pm_ref, g_ref, bb_ref, wp_ref, sc_ref,
             da_ref, dws_ref, dpm_ref, vec_ref, dwp_ref):
        i = pl.program_id(0)

        @pl.when(i == 0)
        def _():
            vec_ref[...] = jnp.zeros_like(vec_ref)
            dwp_ref[...] = jnp.zeros_like(dwp_ref)

        xhat, rstd = _ln_stats(a_ref[...].astype(F32))
        gv = g_ref[...]
        l = xhat * gv + bb_ref[...]
        sg = _sigmoid(l)
        dl = dz_ref[:, 0:c].astype(F32) * (sg * (1.0 + l * (1.0 - sg)))
        dxh = dl * gv
        da2 = rstd * (dxh - jnp.mean(dxh, axis=-1, keepdims=True)
                      - xhat * jnp.mean(dxh * xhat, axis=-1, keepdims=True))
        da_ref[...] = da2.astype(BF16)
        vec_ref[0:1, :] += jnp.sum(dl * xhat, axis=0, keepdims=True)
        vec_ref[1:2, :] += jnp.sum(dl, axis=0, keepdims=True)
        vec_ref[2:3, :] += jnp.sum(da2, axis=0, keepdims=True)
        inv = _counts(i, tr, seq, grp)
        for g in range(ng):
            gs = slice(g * grp, (g + 1) * grp)
            pmv = pm_ref[:, gs]
            wp = wp_ref[g]
            dp = dz_ref[:, c + g * grp:c + (g + 1) * grp].astype(F32)
            q = jnp.dot(pmv, wp, preferred_element_type=F32)
            vec_ref[3:4, gs] += jnp.sum(dp * q, axis=0, keepdims=True)
            dq = (dp * sc_ref[:, gs]).astype(BF16)
            dpm = lax.dot_general(dq, wp, NT, preferred_element_type=F32)
            dwp_ref[g] += lax.dot_general(pmv, dq, TN, preferred_element_type=F32)
            dpm_ref[:, gs] = dpm.astype(BF16)
            dws_ref[:, gs] = (dpm * inv[g]).astype(BF16)

    row = pl.BlockSpec((tr, c), lambda i: (i, 0))
    vec = pl.BlockSpec((1, c), lambda i: (0, 0))
    rshape = jax.ShapeDtypeStruct((t, c), BF16)
    return pl.pallas_call(
        body, name=name, grid=(t // tr,),
        in_specs=[pl.BlockSpec((tr, 2 * c), lambda i: (i, 0)), row, row, vec, vec,
                  pl.BlockSpec((ng, grp, grp), lambda i: (0, 0, 0)), vec],
        out_specs=[row, row, row, pl.BlockSpec((SUBLANES, c), lambda i: (0, 0)),
                   pl.BlockSpec((ng, grp, grp), lambda i: (0, 0, 0))],
        out_shape=[rshape, rshape, rshape, jax.ShapeDtypeStruct((SUBLANES, c), F32),
                   jax.ShapeDtypeStruct((ng, grp, grp), F32)],
        compiler_params=_cp("arbitrary"))(dz, a2, pm, ln_g, ln_b, w_pool, scale)


def _even_du(name, u, da1, dbp, dpm):
    t, c = da1.shape
    tr = _tile(t, 256, SUBLANES)

    def body(u_ref, da_ref, dbp_ref, dpm_ref, du_ref):
        val = u_ref[:, 0:c].astype(F32)
        sg = _sigmoid(u_ref[:, c:2 * c].astype(F32))
        da = da_ref[...].astype(F32)
        du_ref[:, 0:c] = (da * sg).astype(BF16)
        du_ref[:, c:2 * c] = (da * val * sg * (1.0 - sg)).astype(BF16)
        du_ref[:, 2 * c:3 * c] = (dbp_ref[...].astype(F32) - dpm_ref[...].astype(F32)).astype(BF16)

    row = pl.BlockSpec((tr, c), lambda i: (i, 0))
    wide = pl.BlockSpec((tr, 3 * c), lambda i: (i, 0))
    return pl.pallas_call(
        body, name=name, grid=(t // tr,), in_specs=[wide, row, row, row], out_specs=wide,
        out_shape=jax.ShapeDtypeStruct((t, 3 * c), BF16),
        compiler_params=_cp("parallel"))(u, da1, dbp, dpm)


def _odd_du(name, u, dy, co, dxc):
    t, c = dy.shape
    tr = _tile(t, 256, SUBLANES)

    def body(u_ref, dy_ref, co_ref, dx_ref, du_ref):
        dx = dx_ref[...].astype(F32)
        du_ref[:, 0:c] = (dy_ref[...].astype(F32) * co_ref[...].astype(F32)).astype(BF16)
        du_ref[:, c:2 * c] = (dx * u_ref[:, 2 * c:3 * c].astype(F32)).astype(BF16)
        du_ref[:, 2 * c:3 * c] = (dx * u_ref[:, c:2 * c].astype(F32)).astype(BF16)

    row = pl.BlockSpec((tr, c), lambda i: (i, 0))
    wide = pl.BlockSpec((tr, 3 * c), lambda i: (i, 0))
    return pl.pallas_call(
        body, name=name, grid=(t // tr,), in_specs=[wide, row, row, row], out_specs=wide,
        out_shape=jax.ShapeDtypeStruct((t, 3 * c), BF16),
        compiler_params=_cp("parallel"))(u, dy, co, dxc)


def _local_step(x, tgt, seq, big, small):
    t, d = x.shape
    c = d // 2
    cw_e, cw_o = small["conv_w_e"], small["conv_w_o"]
    wp = small["w_pool_e"].astype(BF16)
    ptaps = _pool_taps(c)
    row = lambda v: v.reshape(1, -1)

    n0 = _rms_fwd("rms_fwd_mix0", x, row(small["mix_norm_e"]))
    u0 = _mm_nn("mm_in_e", n0, big["w_in_e"], BF16)
    a2 = _conv_fwd("conv_e_fwd", seq, c, cw_e, u0, 0, u0, c, "glu", bias=row(small["conv_b_e"]))
    ws = _conv_fwd("pool_fwd", seq, c, ptaps, u0, 2 * c)
    z0, pm = _even_fwd("even_fwd", seq, a2, ws, u0, row(small["ln_g_e"]), row(small["ln_b_e"]),
                       wp, row(small["pool_scale_e"]))
    h1 = _mm_nn("mm_out_e", z0, big["w_out_e"], F32, res=x)
    n1 = _rms_fwd("rms_fwd_ffn0", h1, row(small["ffn_norm"][0]))
    g0, up0, act0 = _ffn_fwd("ffn0_fwd", n1, big["w_gate"][0], big["w_up"][0])
    h2 = _mm_down("mm_down0", act0, big["w_down"][0], h1)
    n2 = _rms_fwd("rms_fwd_mix1", h2, row(small["mix_norm_o"]))
    u1 = _mm_nn("mm_in_o", n2, big["w_in_o"], BF16)
    co, y1 = _conv_fwd("conv_o_fwd", seq, d, cw_o, u1, d, u1, 2 * d, "mul", post=u1, cpost=0)
    h3 = _mm_nn("mm_out_o", y1, big["w_out_o"], F32, res=h2)
    n3 = _rms_fwd("rms_fwd_ffn1", h3, row(small["ffn_norm"][1]))
    g1, up1, act1 = _ffn_fwd("ffn1_fwd", n3, big["w_gate"][1], big["w_up"][1])
    h4 = _mm_down("mm_down1", act1, big["w_down"][1], h3)

    dh4, dh4b, d_final, lsum = _loss_head("loss_head", h4, row(small["final_norm"]), tgt)

    def ffn_bwd(tag, dh, dhb, h_in, gain, n, g, up, act, wg, wu, wd):
        dg, dup = _ffn_bwd_act("ffn%s_bwd_act" % tag, dhb, wd, g, up)
        dwd = _mm_dwd("mm_dwd%s" % tag, act, dhb)
        dwg = _mm_dwg("mm_dwg%s" % tag, n, dg)
        dwu = _mm_dwg("mm_dwu%s" % tag, n, dup)
        dn = _mm_ffn_dn("mm_ffn_dn%s" % tag, dg, wg, dup, wu)
        dh_in, dhb_in, dgain = _rms_bwd("rms_bwd_ffn%s" % tag, h_in, gain, dn, dh)
        return dh_in, dhb_in, dgain, dwg, dwu, dwd

    dh3, dh3b, d_ffn1, dwg1, dwu1, dwd1 = ffn_bwd(
        "1", dh4, dh4b, h3, row(small["ffn_norm"][1]), n3, g1, up1, act1,
        big["w_gate"][1], big["w_up"][1], big["w_down"][1])
    dy1 = _mm_nt("mm_dy_o", dh3b, big["w_out_o"], BF16)
    dw_out_o = _mm_tn("mm_dw_out_o", y1, dh3b, BF16)
    dxc, dcw_o = _conv_bwd("conv_o_bwd", seq, d, cw_o, dy1, 0, u1, 0, "mul",
                           x1=u1, c1=d, x2=u1, c2=2 * d, pre="mul")
    du1 = _odd_du("odd_du", u1, dy1, co, dxc)
    dw_in_o = _mm_tn("mm_dw_in_o", n2, du1, BF16)
    dn2 = _mm_nt("mm_dn_o", du1, big["w_in_o"], BF16)
    dh2, dh2b, d_mix_o = _rms_bwd("rms_bwd_mix1", h2, row(small["mix_norm_o"]), dn2, dh3)

    dh1, dh1b, d_ffn0, dwg0, dwu0, dwd0 = ffn_bwd(
        "0", dh2, dh2b, h1, row(small["ffn_norm"][0]), n1, g0, up0, act0,
        big["w_gate"][0], big["w_up"][0], big["w_down"][0])
    dz0 = _mm_nt("mm_dz_e", dh1b, big["w_out_e"], BF16)
    dw_out_e = _mm_tn("mm_dw_out_e", z0, dh1b, BF16)
    da2, dws, dpm, vecs, dwp = _even_bwd("even_bwd", seq, dz0, a2, pm, row(small["ln_g_e"]),
                                         row(small["ln_b_e"]), wp, row(small["pool_scale_e"]))
    da1, dcw_e = _conv_bwd("conv_e_bwd", seq, c, cw_e, da2, 0, x1=u0, c1=0, x2=u0, c2=c, pre="glu")
    dbp = _conv_bwd("pool_bwd", seq, c, ptaps, dws, 0)
    du0 = _even_du("even_du", u0, da1, dbp, dpm)
    dw_in_e = _mm_tn("mm_dw_in_e", n0, du0, BF16)
    dn0 = _mm_nt("mm_dn_e", du0, big["w_in_e"], BF16)
    dx, _, d_mix_e = _rms_bwd("rms_bwd_mix0", x, row(small["mix_norm_e"]), dn0, dh1)

    gbig = {"w_in_e": dw_in_e, "w_out_e": dw_out_e, "w_in_o": dw_in_o, "w_out_o": dw_out_o,
            "w_gate": (dwg0, dwg1), "w_up": (dwu0, dwu1), "w_down": (dwd0, dwd1)}
    gsmall = {"mix_norm_e": d_mix_e[0], "conv_w_e": dcw_e, "conv_b_e": vecs[2], "ln_g_e": vecs[0],
              "ln_b_e": vecs[1], "w_pool_e": dwp, "pool_scale_e": vecs[3], "mix_norm_o": d_mix_o[0],
              "conv_w_o": dcw_o, "ffn_norm": jnp.concatenate([d_ffn0, d_ffn1], axis=0),
              "final_norm": d_final[0]}
    return lsum, dx, gbig, gsmall


def _place():
    x, y, c = (lax.axis_index(a) for a in MESH_AXES)
    return x, y, c


def _index(p):
    return 4 * p[0] + 2 * p[1] + p[2]


def _slab(ref, kind, d, n):
    if kind == "blk":
        return ref.at[d]
    return ref.at[:, pl.ds(pl.multiple_of(d * n, LANES), n)]


def _all_gather(name, shards, kinds):
    na = len(shards)

    def body(*refs):
        x_refs, o_refs = refs[:na], refs[na:2 * na]
        send_sems, recv_sems, local_sems = refs[2 * na:]
        x, y, c = _place()
        me, sib = (x, y, c), (x, y, 1 - c)
        chips = [(1 - x, y), (x, 1 - y), (1 - x, 1 - y)]

        def slot(a, p):
            return _slab(o_refs[a], kinds[a], _index(p), shards[a].shape[1])

        def copy(a, k, block, to, src=None):
            return pltpu.make_async_remote_copy(
                src_ref=slot(a, block) if src is None else src, dst_ref=slot(a, block),
                send_sem=send_sems.at[a, k], recv_sem=recv_sems.at[a, k],
                device_id=to, device_id_type=MESH)

        mine = [pltpu.make_async_copy(x_refs[a], slot(a, me), local_sems.at[a]) for a in range(na)]
        for cp in mine:
            cp.start()
        first = []
        for a in range(na):
            first.append(copy(a, 0, me, sib, src=x_refs[a]))
            first += [copy(a, 1 + j, me, (*chip, c), src=x_refs[a]) for j, chip in enumerate(chips)]
        for cp in first:
            cp.start()
        passed = []
        for j, chip in enumerate(chips):
            for a in range(na):
                copy(a, 1 + j, (*chip, c), me).wait_recv()
                fwd = copy(a, 4 + j, (*chip, c), sib)
                fwd.start()
                passed.append(fwd)
        for a in range(na):
            copy(a, 0, sib, me).wait_recv()
            for j, chip in enumerate(chips):
                copy(a, 4 + j, (*chip, 1 - c), me).wait_recv()
        for cp in first + passed:
            cp.wait_send()
        for cp in mine:
            cp.wait()

    shapes = []
    for s, kind in zip(shards, kinds):
        m, n = s.shape
        shapes.append(jax.ShapeDtypeStruct((NDEV, m, n) if kind == "blk" else (m, NDEV * n), s.dtype))
    return pl.pallas_call(
        body, name=name, in_specs=[ANY] * na, out_specs=[ANY] * na, out_shape=shapes,
        scratch_shapes=[pltpu.SemaphoreType.DMA((na, 7)), pltpu.SemaphoreType.DMA((na, 7)),
                        pltpu.SemaphoreType.DMA((na,))])(*shards)


def _scatter_blocks(name, fulls, kinds, nloc):
    na = len(fulls)
    rels = [(dx, dy, dc) for dx in (0, 1) for dy in (0, 1) for dc in (0, 1)][1:]

    def body(*refs):
        g_refs, o_refs = refs[:na], refs[na:2 * na]
        send_sems, recv_sems, local_sems = refs[2 * na:]
        x, y, c = _place()
        me = (x, y, c)
        peers = [(x ^ dx, y ^ dy, c ^ dc) for dx, dy, dc in rels]

        def copy(a, k, to):
            return pltpu.make_async_remote_copy(
                src_ref=_slab(g_refs[a], kinds[a], _index(to), nloc[a]),
                dst_ref=o_refs[a].at[_index(me)],
                send_sem=send_sems.at[a, k], recv_sem=recv_sems.at[a, k],
                device_id=to, device_id_type=MESH)

        def arrival(a, k, frm):
            return pltpu.make_async_remote_copy(
                src_ref=_slab(g_refs[a], kinds[a], _index(frm), nloc[a]),
                dst_ref=o_refs[a].at[_index(frm)],
                send_sem=send_sems.at[a, k], recv_sem=recv_sems.at[a, k],
                device_id=frm, device_id_type=MESH)

        mine = [pltpu.make_async_copy(_slab(g_refs[a], kinds[a], _index(me), nloc[a]),
                                      o_refs[a].at[_index(me)], local_sems.at[a]) for a in range(na)]
        for cp in mine:
            cp.start()
        sent = [copy(a, k, p) for a in range(na) for k, p in enumerate(peers)]
        for cp in sent:
            cp.start()
        for a in range(na):
            for k, p in enumerate(peers):
                arrival(a, k, p).wait_recv()
        for cp in sent:
            cp.wait_send()
        for cp in mine:
            cp.wait()

    shapes = []
    for g, kind, n in zip(fulls, kinds, nloc):
        m = g.shape[1] if kind == "blk" else g.shape[0]
        shapes.append(jax.ShapeDtypeStruct((NDEV, m, n), g.dtype))
    return pl.pallas_call(
        body, name=name, in_specs=[ANY] * na, out_specs=[ANY] * na, out_shape=shapes,
        scratch_shapes=[pltpu.SemaphoreType.DMA((na, 7)), pltpu.SemaphoreType.DMA((na, 7)),
                        pltpu.SemaphoreType.DMA((na,))])(*fulls)


def _adam_math(w, g, m, v):
    m = ADAM_B1 * m + (1.0 - ADAM_B1) * g
    v = ADAM_B2 * v + (1.0 - ADAM_B2) * (g * g)
    m_hat = m / (1.0 - ADAM_B1 ** ADAM_STEP)
    v_hat = v / (1.0 - ADAM_B2 ** ADAM_STEP)
    delta = -ADAM_LR * (m_hat / (jnp.sqrt(v_hat) + ADAM_EPS) + ADAM_WD * w)
    return delta, m, v


def _sum_adamw(name, parts, w, m, v):
    r, c = w.shape
    tr = _tile(r, max(SUBLANES, 262144 // c), SUBLANES)

    def body(p_ref, w_ref, m_ref, v_ref, g_ref, d_ref, mo_ref, vo_ref):
        g = p_ref[0].astype(F32)
        for s in range(1, NDEV):
            g = g + p_ref[s].astype(F32)
        delta, mn, vn = _adam_math(w_ref[...], g, m_ref[...], v_ref[...])
        g_ref[...] = g
        d_ref[...] = delta
        mo_ref[...] = mn
        vo_ref[...] = vn

    row = pl.BlockSpec((tr, c), lambda i: (i, 0))
    shp = jax.ShapeDtypeStruct((r, c), F32)
    return pl.pallas_call(
        body, name=name, grid=(r // tr,),
        in_specs=[pl.BlockSpec((NDEV, tr, c), lambda i: (0, i, 0)), row, row, row],
        out_specs=[row, row, row, row], out_shape=[shp, shp, shp, shp],
        compiler_params=_cp("parallel"))(parts, w, m, v)


def _sum_parts(name, parts):
    _, r, c = parts.shape

    def body(p_ref, o_ref):
        g = p_ref[0]
        for s in range(1, NDEV):
            g = g + p_ref[s]
        o_ref[...] = g

    return pl.pallas_call(
        body, name=name, grid=(1,),
        in_specs=[pl.BlockSpec((NDEV, r, c), lambda i: (0, 0, 0))],
        out_specs=pl.BlockSpec((r, c), lambda i: (0, 0)),
        out_shape=jax.ShapeDtypeStruct((r, c), F32), compiler_params=_cp("arbitrary"))(parts)


def _adamw(name, w, g, m, v):
    r, c = w.shape

    def body(w_ref, g_ref, m_ref, v_ref, d_ref, mo_ref, vo_ref):
        delta, mn, vn = _adam_math(w_ref[...], g_ref[...], m_ref[...], v_ref[...])
        d_ref[...] = delta
        mo_ref[...] = mn
        vo_ref[...] = vn

    full = pl.BlockSpec((r, c), lambda i: (0, 0))
    shp = jax.ShapeDtypeStruct((r, c), F32)
    return pl.pallas_call(
        body, name=name, grid=(1,), in_specs=[full] * 4, out_specs=[full] * 3,
        out_shape=[shp] * 3, compiler_params=_cp("arbitrary"))(w, g, m, v)


def _pack(arrays):
    flat = jnp.concatenate([a.reshape(-1) for a in arrays])
    unit = SUBLANES * LANES
    pad = (-flat.shape[0]) % unit
    return jnp.pad(flat, (0, pad)).reshape(-1, LANES)


def _unpack(buf, shapes):
    flat = buf.reshape(-1)
    out, off = [], 0
    for shp in shapes:
        size = 1
        for s in shp:
            size *= s
        out.append(flat[off:off + size].reshape(shp))
        off += size
    return out


WEIGHTS = ["mix_norm_e", "w_in_e", "conv_w_e", "conv_b_e", "ln_g_e", "ln_b_e", "w_pool_e",
           "pool_scale_e", "w_out_e", "mix_norm_o", "w_in_o", "conv_w_o", "w_out_o", "ffn_norm",
           "w_gate", "w_up", "w_down", "final_norm"]
BIG = ["w_in_e", "w_out_e", "w_in_o", "w_out_o", "w_gate", "w_up", "w_down"]
SHARDED_SMALL = {"conv_w_e": 1, "w_pool_e": 1, "mix_norm_o": 0, "conv_w_o": 1}
SMALL = [n for n in WEIGHTS if n not in BIG]


def kernel(x, mix_norm_e, w_in_e, conv_w_e, conv_b_e, ln_g_e, ln_b_e, w_pool_e, pool_scale_e, w_out_e, mix_norm_o, w_in_o, conv_w_o, w_out_o, ffn_norm, w_gate, w_up, w_down, final_norm, loss_target, m_mix_norm_e, m_w_in_e, m_conv_w_e, m_conv_b_e, m_ln_g_e, m_ln_b_e, m_w_pool_e, m_pool_scale_e, m_w_out_e, m_mix_norm_o, m_w_in_o, m_conv_w_o, m_w_out_o, m_ffn_norm, m_w_gate, m_w_up, m_w_down, m_final_norm, v_mix_norm_e, v_w_in_e, v_conv_w_e, v_conv_b_e, v_ln_g_e, v_ln_b_e, v_w_pool_e, v_pool_scale_e, v_w_out_e, v_mix_norm_o, v_w_in_o, v_conv_w_o, v_w_out_o, v_ffn_norm, v_w_gate, v_w_up, v_w_down, v_final_norm):
    wts = dict(zip(WEIGHTS, [mix_norm_e, w_in_e, conv_w_e, conv_b_e, ln_g_e, ln_b_e, w_pool_e, pool_scale_e, w_out_e, mix_norm_o, w_in_o, conv_w_o, w_out_o, ffn_norm, w_gate, w_up, w_down, final_norm]))
    mom = dict(zip(WEIGHTS, [m_mix_norm_e, m_w_in_e, m_conv_w_e, m_conv_b_e, m_ln_g_e, m_ln_b_e, m_w_pool_e, m_pool_scale_e, m_w_out_e, m_mix_norm_o, m_w_in_o, m_conv_w_o, m_w_out_o, m_ffn_norm, m_w_gate, m_w_up, m_w_down, m_final_norm]))
    var = dict(zip(WEIGHTS, [v_mix_norm_e, v_w_in_e, v_conv_w_e, v_conv_b_e, v_ln_g_e, v_ln_b_e, v_w_pool_e, v_pool_scale_e, v_w_out_e, v_mix_norm_o, v_w_in_o, v_conv_w_o, v_w_out_o, v_ffn_norm, v_w_gate, v_w_up, v_w_down, v_final_norm]))
    bsz, seq, d = x.shape
    t = bsz * seq
    me = _index(_place())

    bf = lambda a: a.astype(BF16)
    e_sh = [bf(w_in_e[0]), bf(w_out_e[0])]
    o_sh = [bf(w_in_o[0]), bf(w_out_o[0])]
    f_sh = [[bf(w_gate[l]), bf(w_up[l]), bf(w_down[l])] for l in range(2)]
    g_in_e, g_out_e = _all_gather("gather_mix_e", e_sh, ["col", "blk"])
    g_in_o, g_out_o = _all_gather("gather_mix_o", o_sh, ["col", "blk"])
    g_f = [_all_gather("gather_ffn%d" % l, f_sh[l], ["blk", "blk", "blk"]) for l in range(2)]
    big = {"w_in_e": g_in_e, "w_out_e": g_out_e.reshape(-1, d),
           "w_in_o": g_in_o, "w_out_o": g_out_o.reshape(-1, d),
           "w_gate": (g_f[0][0], g_f[1][0]), "w_up": (g_f[0][1], g_f[1][1]),
           "w_down": (g_f[0][2], g_f[1][2])}
    sh_names = list(SHARDED_SMALL)
    sh_local = [wts[n][0] for n in sh_names]
    packed = _pack(sh_local)
    gathered, = _all_gather("gather_small", [packed], ["blk"])
    small = {n: wts[n][0] for n in SMALL if n not in SHARDED_SMALL and n not in ("ffn_norm", "final_norm")}
    small["ffn_norm"], small["final_norm"] = ffn_norm, final_norm
    per_dev = [_unpack(gathered[s], [a.shape for a in sh_local]) for s in range(NDEV)]
    for i, n in enumerate(sh_names):
        small[n] = jnp.concatenate([per_dev[s][i] for s in range(NDEV)], axis=SHARDED_SMALL[n])

    lsum, dx, gbig, gsmall = _local_step(x.reshape(t, d), loss_target.reshape(t, d), seq, big, small)
    loss = lax.psum(jnp.sum(lsum), MESH_AXES)

    out_g, out_d, out_m, out_v = {}, {}, {}, {}

    def finish(n, layer, parts):
        pick = (lambda a: a[0]) if layer is None else (lambda a: a[layer])
        return _sum_adamw("adamw_%s%s" % (n, "" if layer is None else layer), parts,
                          pick(wts[n]), pick(mom[n]), pick(var[n]))

    ce = w_in_e.shape[2]
    co = w_in_o.shape[2]
    rows = w_out_e.shape[1]
    l_in_e, l_out_e = _scatter_blocks(
        "scatter_mix_e", [gbig["w_in_e"], gbig["w_out_e"].reshape(NDEV, rows, d)], ["col", "blk"], [ce, d])
    l_in_o, l_out_o = _scatter_blocks(
        "scatter_mix_o", [gbig["w_in_o"], gbig["w_out_o"].reshape(NDEV, rows, d)], ["col", "blk"], [co, d])
    res = {"w_in_e": finish("w_in_e", None, l_in_e), "w_out_e": finish("w_out_e", None, l_out_e),
           "w_in_o": finish("w_in_o", None, l_in_o), "w_out_o": finish("w_out_o", None, l_out_o)}
    f = w_gate.shape[2]
    ffn_res = {"w_gate": [], "w_up": [], "w_down": []}
    for l in range(2):
        lg, lu, ld = _scatter_blocks(
            "scatter_ffn%d" % l, [gbig["w_gate"][l], gbig["w_up"][l], gbig["w_down"][l]],
            ["blk", "blk", "blk"], [f, f, d])
        ffn_res["w_gate"].append(finish("w_gate", l, lg))
        ffn_res["w_up"].append(finish("w_up", l, lu))
        ffn_res["w_down"].append(finish("w_down", l, ld))
    for n in ("w_in_e", "w_out_e", "w_in_o", "w_out_o"):
        out_g[n], out_d[n], out_m[n], out_v[n] = (a[None] for a in res[n])
    for n in ("w_gate", "w_up", "w_down"):
        out_g[n], out_d[n], out_m[n], out_v[n] = (
            jnp.stack([ffn_res[n][0][i], ffn_res[n][1][i]]) for i in range(4))

    gs_list = [gsmall[n] for n in SMALL]
    gs_all, = _all_gather("gather_small_grads", [_pack(gs_list)], ["blk"])
    gs_sum = _unpack(_sum_parts("sum_small_grads", gs_all), [a.shape for a in gs_list])
    local_g = []
    for n, g in zip(SMALL, gs_sum):
        if n in SHARDED_SMALL:
            ax = SHARDED_SMALL[n]
            size = wts[n].shape[ax + 1]
            g = lax.dynamic_slice_in_dim(g, me * size, size, axis=ax)
        local_g.append(g.reshape(wts[n].shape))
    shapes = [wts[n].shape for n in SMALL]
    upd = _adamw("adamw_small", _pack([wts[n] for n in SMALL]), _pack(local_g),
                 _pack([mom[n] for n in SMALL]), _pack([var[n] for n in SMALL]))
    for i, outd in enumerate((out_d, out_m, out_v)):
        for n, a in zip(SMALL, _unpack(upd[i], shapes)):
            outd[n] = a
    for n, g in zip(SMALL, local_g):
        out_g[n] = g

    return (loss, dx.reshape(bsz, seq, d), *[out_g[n] for n in WEIGHTS], *[out_d[n] for n in WEIGHTS],
            *[out_m[n] for n in WEIGHTS], *[out_v[n] for n in WEIGHTS])
```

```python
import functools

import jax
import jax.numpy as jnp
from jax import lax
from jax.experimental import pallas as pl
from jax.experimental.pallas import tpu as pltpu

F32 = jnp.float32
BF16 = jnp.bfloat16
NDEV = 8
MESH_AXES = ("x", "y", "c")
EPS = 1e-6
POOL_WINDOWS = (2, 4, 8, 16)
CONV_WIDTH = 31
SHORT_WIDTH = 3
ADAM_LR = 0.001
ADAM_B1 = 0.9
ADAM_B2 = 0.999
ADAM_EPS = 1e-08
ADAM_WD = 0.01
ADAM_STEP = 10
LANES = 128
SUBLANES = 8
VMEM_LIMIT = 56 * 1024 * 1024
MESH = pl.DeviceIdType.MESH
ANY = pl.BlockSpec(memory_space=pl.ANY)


def _cp(*sem):
    return pltpu.CompilerParams(dimension_semantics=sem, vmem_limit_bytes=VMEM_LIMIT)


def _tile(n, pref, unit=LANES):
    if n <= pref:
        return n
    t = (pref // unit) * unit
    while t > unit and n % t:
        t -= unit
    assert n % t == 0, (n, pref)
    return t


def _sigmoid(v):
    return 1.0 / (1.0 + jnp.exp(-v))


def _mm(name, pairs, a_specs, b_specs, dims, out_shape, o_spec, grid, acc_shape,
        res=None, res_spec=None, dep=None):
    np_ = len(pairs)
    nk = grid[2]
    has_res = res is not None
    n_in = 2 * np_ + (1 if has_res else 0) + (0 if dep is None else 1)

    def body(*refs):
        a_refs = refs[:np_]
        b_refs = refs[np_:2 * np_]
        r_ref = refs[2 * np_] if has_res else None
        o_ref = refs[n_in]
        acc = refs[-1]

        def part():
            s = None
            for a_ref, b_ref in zip(a_refs, b_refs):
                d = lax.dot_general(a_ref[...], b_ref[...], dims, preferred_element_type=F32)
                s = d if s is None else s + d
            return s

        def finish(v):
            if has_res:
                v = v + r_ref[...]
            o_ref[...] = v.astype(o_ref.dtype)

        if nk == 1:
            finish(part())
        else:
            k = pl.program_id(2)

            @pl.when(k == 0)
            def _():
                acc[...] = part()

            @pl.when(k > 0)
            def _():
                acc[...] += part()

            @pl.when(k == nk - 1)
            def _():
                finish(acc[...])

    ins = [p[0] for p in pairs] + [p[1] for p in pairs]
    specs = list(a_specs) + list(b_specs)
    if has_res:
        ins.append(res)
        specs.append(res_spec)
    if dep is not None:
        ins.append(dep)
        specs.append(ANY)
    return pl.pallas_call(
        body, name=name, grid=grid, in_specs=specs, out_specs=o_spec, out_shape=out_shape,
        scratch_shapes=[pltpu.VMEM(acc_shape if nk > 1 else (SUBLANES, LANES), F32)],
        compiler_params=_cp("parallel", "parallel", "arbitrary"))(*ins)


NN = (((1,), (0,)), ((), ()))
NT = (((1,), (1,)), ((), ()))
TN = (((0,), (0,)), ((), ()))


def _mm_nn(name, a, b, out_dtype, res=None):
    m, kk = a.shape
    n = b.shape[1]
    tm, tn, tk = _tile(m, 1024), _tile(n, 1024), _tile(kk, 1024)
    return _mm(name, [(a, b)],
               [pl.BlockSpec((tm, tk), lambda i, j, k: (i, k))],
               [pl.BlockSpec((tk, tn), lambda i, j, k: (k, j))], NN,
               jax.ShapeDtypeStruct((m, n), out_dtype),
               pl.BlockSpec((tm, tn), lambda i, j, k: (i, j)),
               (m // tm, n // tn, kk // tk), (tm, tn), res,
               pl.BlockSpec((tm, tn), lambda i, j, k: (i, j)))


def _mm_nt(name, a, b, out_dtype, dep=None):
    m, n = a.shape
    kk = b.shape[0]
    tm, tn, tk = _tile(m, 1024), _tile(kk, 1024), _tile(n, 1024)
    return _mm(name, [(a, b)],
               [pl.BlockSpec((tm, tk), lambda i, j, k: (i, k))],
               [pl.BlockSpec((tn, tk), lambda i, j, k: (j, k))], NT,
               jax.ShapeDtypeStruct((m, kk), out_dtype),
               pl.BlockSpec((tm, tn), lambda i, j, k: (i, j)),
               (m // tm, kk // tn, n // tk), (tm, tn), dep=dep)


def _mm_tn(name, a, b, out_dtype):
    t, m = a.shape
    n = b.shape[1]
    tm, tn, tk = _tile(m, 1024), _tile(n, 1024), _tile(t, 1024)
    return _mm(name, [(a, b)],
               [pl.BlockSpec((tk, tm), lambda i, j, k: (k, i))],
               [pl.BlockSpec((tk, tn), lambda i, j, k: (k, j))], TN,
               jax.ShapeDtypeStruct((m, n), out_dtype),
               pl.BlockSpec((tm, tn), lambda i, j, k: (i, j)),
               (m // tm, n // tn, t // tk), (tm, tn))


def _mm_down(name, act, wd, res):
    nb, t, f = act.shape
    d = wd.shape[2]
    tm, tn = _tile(t, 1024), _tile(d, 1024)
    return _mm(name, [(act, wd)],
               [pl.BlockSpec((None, tm, f), lambda i, j, k: (k, i, 0))],
               [pl.BlockSpec((None, f, tn), lambda i, j, k: (k, 0, j))], NN,
               jax.ShapeDtypeStruct((t, d), F32),
               pl.BlockSpec((tm, tn), lambda i, j, k: (i, j)),
               (t // tm, d // tn, nb), (tm, tn), res,
               pl.BlockSpec((tm, tn), lambda i, j, k: (i, j)))


def _mm_ffn_dn(name, dg, wg, dup, wu):
    nb, t, f = dg.shape
    d = wg.shape[1]
    tm, tn = _tile(t, 1024), _tile(d, 1024)
    a_spec = pl.BlockSpec((None, tm, f), lambda i, j, k: (k, i, 0))
    b_spec = pl.BlockSpec((None, tn, f), lambda i, j, k: (k, j, 0))
    return _mm(name, [(dg, wg), (dup, wu)], [a_spec, a_spec], [b_spec, b_spec], NT,
               jax.ShapeDtypeStruct((t, d), BF16),
               pl.BlockSpec((tm, tn), lambda i, j, k: (i, j)),
               (t // tm, d // tn, nb), (tm, tn))


def _mm_dwd(name, act, dh):
    nb, t, f = act.shape
    d = dh.shape[1]
    tn, tk = _tile(d, 1024), _tile(t, 1024)
    return _mm(name, [(act, dh)],
               [pl.BlockSpec((None, tk, f), lambda i, j, k: (i, k, 0))],
               [pl.BlockSpec((tk, tn), lambda i, j, k: (k, j))], TN,
               jax.ShapeDtypeStruct((nb, f, d), BF16),
               pl.BlockSpec((None, f, tn), lambda i, j, k: (i, 0, j)),
               (nb, d // tn, t // tk), (f, tn))


def _mm_dwg(name, n, dg):
    nb, t, f = dg.shape
    d = n.shape[1]
    tm, tk = _tile(d, 1024), _tile(t, 1024)
    return _mm(name, [(n, dg)],
               [pl.BlockSpec((tk, tm), lambda i, j, k: (k, i))],
               [pl.BlockSpec((None, tk, f), lambda i, j, k: (j, k, 0))], TN,
               jax.ShapeDtypeStruct((nb, d, f), BF16),
               pl.BlockSpec((None, tm, f), lambda i, j, k: (j, i, 0)),
               (d // tm, nb, t // tk), (tm, f))


def _ffn_fwd(name, n, wg, wu):
    nb, d, f = wg.shape
    t = n.shape[0]
    tm = _tile(t, 512)

    def body(n_ref, wg_ref, wu_ref, g_ref, up_ref, act_ref):
        nv = n_ref[...]
        g = jnp.dot(nv, wg_ref[...], preferred_element_type=F32)
        up = jnp.dot(nv, wu_ref[...], preferred_element_type=F32)
        g_ref[...] = g.astype(BF16)
        up_ref[...] = up.astype(BF16)
        act_ref[...] = (g * _sigmoid(g) * up).astype(BF16)

    w_spec = pl.BlockSpec((None, d, f), lambda j, i: (j, 0, 0))
    o_spec = pl.BlockSpec((None, tm, f), lambda j, i: (j, i, 0))
    shp = jax.ShapeDtypeStruct((nb, t, f), BF16)
    return pl.pallas_call(
        body, name=name, grid=(nb, t // tm),
        in_specs=[pl.BlockSpec((tm, d), lambda j, i: (i, 0)), w_spec, w_spec],
        out_specs=[o_spec, o_spec, o_spec], out_shape=[shp, shp, shp],
        compiler_params=_cp("parallel", "parallel"))(n, wg, wu)


def _ffn_bwd_act(name, dh, wd, g, up, dep=None):
    nb, f, d = wd.shape
    t = dh.shape[0]
    tm = _tile(t, 512)

    def body(dh_ref, wd_ref, g_ref, up_ref, *rest):
        dg_ref, dup_ref = rest[-2:]
        da = lax.dot_general(dh_ref[...], wd_ref[...], NT, preferred_element_type=F32)
        gv = g_ref[...].astype(F32)
        uv = up_ref[...].astype(F32)
        sg = _sigmoid(gv)
        dg_ref[...] = (da * uv * (sg * (1.0 + gv * (1.0 - sg)))).astype(BF16)
        dup_ref[...] = (da * gv * sg).astype(BF16)

    o_spec = pl.BlockSpec((None, tm, f), lambda j, i: (j, i, 0))
    shp = jax.ShapeDtypeStruct((nb, t, f), BF16)
    return pl.pallas_call(
        body, name=name, grid=(nb, t // tm),
        in_specs=[pl.BlockSpec((tm, d), lambda j, i: (i, 0)),
                  pl.BlockSpec((None, f, d), lambda j, i: (j, 0, 0)), o_spec, o_spec]
        + ([] if dep is None else [ANY]),
        out_specs=[o_spec, o_spec], out_shape=[shp, shp],
        compiler_params=_cp("parallel", "parallel"))(dh, wd, g, up, *([] if dep is None else [dep]))


def _rms_fwd(name, h, gain):
    t, d = h.shape
    tr = _tile(t, 512, SUBLANES)

    def body(h_ref, g_ref, n_ref):
        hv = h_ref[...]
        r = lax.rsqrt(jnp.mean(hv * hv, axis=-1, keepdims=True) + EPS)
        n_ref[...] = (hv * r * g_ref[...]).astype(BF16)

    return pl.pallas_call(
        body, name=name, grid=(t // tr,),
        in_specs=[pl.BlockSpec((tr, d), lambda i: (i, 0)), pl.BlockSpec((1, d), lambda i: (0, 0))],
        out_specs=pl.BlockSpec((tr, d), lambda i: (i, 0)),
        out_shape=jax.ShapeDtypeStruct((t, d), BF16),
        compiler_params=_cp("parallel"))(h, gain)


def _rms_bwd_math(hv, gain, dn):
    d = hv.shape[-1]
    r = lax.rsqrt(jnp.mean(hv * hv, axis=-1, keepdims=True) + EPS)
    xhat = hv * r
    dxh = dn * gain
    dh = r * (dxh - xhat * (jnp.sum(dxh * xhat, axis=-1, keepdims=True) / d))
    dgain = jnp.sum(dn * xhat, axis=0, keepdims=True)
    return dh, dgain


def _rms_bwd(name, h, gain, dn, dres):
    t, d = h.shape
    tr = _tile(t, 256, SUBLANES)

    def body(h_ref, g_ref, dn_ref, dr_ref, dh_ref, dhb_ref, dg_ref):
        dh, dgain = _rms_bwd_math(h_ref[...], g_ref[...], dn_ref[...].astype(F32))
        dh = dh + dr_ref[...]
        dh_ref[...] = dh
        dhb_ref[...] = dh.astype(BF16)

        @pl.when(pl.program_id(0) == 0)
        def _():
            dg_ref[...] = dgain

        @pl.when(pl.program_id(0) > 0)
        def _():
            dg_ref[...] += dgain

    row = pl.BlockSpec((tr, d), lambda i: (i, 0))
    vec = pl.BlockSpec((1, d), lambda i: (0, 0))
    return pl.pallas_call(
        body, name=name, grid=(t // tr,), in_specs=[row, vec, row, row],
        out_specs=[row, row, vec],
        out_shape=[jax.ShapeDtypeStruct((t, d), F32), jax.ShapeDtypeStruct((t, d), BF16),
                   jax.ShapeDtypeStruct((1, d), F32)],
        compiler_params=_cp("arbitrary"))(h, gain, dn, dres)


def _loss_head(name, h, gain, tgt):
    t, d = h.shape
    tr = _tile(t, 256, SUBLANES)

    def body(h_ref, g_ref, t_ref, dh_ref, dhb_ref, dg_ref, ls_ref):
        hv = h_ref[...]
        gv = g_ref[...]
        r = lax.rsqrt(jnp.mean(hv * hv, axis=-1, keepdims=True) + EPS)
        err = hv * r * gv - t_ref[...]
        lsum = 0.5 * jnp.sum(err * err, axis=0, keepdims=True) / d
        dh, dgain = _rms_bwd_math(hv, gv, err / d)
        dh_ref[...] = dh
        dhb_ref[...] = dh.astype(BF16)

        @pl.when(pl.program_id(0) == 0)
        def _():
            dg_ref[...] = dgain
            ls_ref[...] = lsum

        @pl.when(pl.program_id(0) > 0)
        def _():
            dg_ref[...] += dgain
            ls_ref[...] += lsum

    row = pl.BlockSpec((tr, d), lambda i: (i, 0))
    vec = pl.BlockSpec((1, d), lambda i: (0, 0))
    return pl.pallas_call(
        body, name=name, grid=(t // tr,), in_specs=[row, vec, row],
        out_specs=[row, row, vec, vec],
        out_shape=[jax.ShapeDtypeStruct((t, d), F32), jax.ShapeDtypeStruct((t, d), BF16),
                   jax.ShapeDtypeStruct((1, d), F32), jax.ShapeDtypeStruct((1, d), F32)],
        compiler_params=_cp("arbitrary"))(h, gain, tgt)


def _conv_geom(t, seq, c, k):
    halo = 32 if k - 1 > SUBLANES else SUBLANES
    assert k - 1 <= halo
    tm = min(256, seq // 2)
    tc = min(512, c)
    assert seq % tm == 0 and tm % halo == 0 and c % tc == 0 and t % seq == 0
    return halo, tm, tc, min(128, tm), min(LANES, tc)


def _pre(kind, a, b):
    if kind == "glu":
        return a * _sigmoid(b)
    if kind == "mul":
        return a * b
    return a


def _taps(k):
    return [(s % SUBLANES, s // SUBLANES, s) for s in range(k)]


def _conv_fwd(name, seq, c, w, x1, c1, x2=None, c2=0, pre=None, bias=None, post=None, cpost=0):
    t = x1.shape[0]
    k = w.shape[0]
    halo, tm, tc, sr, sl = _conv_geom(t, seq, c, k)
    nb, cps = tm // halo, seq // tm
    two = x2 is not None
    has_bias, has_post = bias is not None, post is not None

    def body(*refs):
        it = iter(refs)
        x1c, x1h = next(it), next(it)
        x2c, x2h = (next(it), next(it)) if two else (None, None)
        w_ref = next(it)
        b_ref = next(it) if has_bias else None
        p_ref = next(it) if has_post else None
        o_ref = next(it)
        y_ref = next(it) if has_post else None
        xs = next(it)
        first = (pl.program_id(1) % cps) == 0
        hv = _pre(pre, x1h[...].astype(F32), x2h[...].astype(F32) if two else None)
        xs[0:halo, :] = jnp.where(first, 0.0, hv)
        xs[halo:halo + tm, :] = _pre(pre, x1c[...].astype(F32), x2c[...].astype(F32) if two else None)
        for l0 in range(0, tc, sl):
            ls = slice(l0, l0 + sl)
            for r0 in range(0, tm, sr):
                win = xs[r0:r0 + sr + halo, ls]
                acc = jnp.zeros((sr, sl), F32)
                rolled = {}
                for r, q, s in _taps(k):
                    if r not in rolled:
                        rolled[r] = win if r == 0 else pltpu.roll(win, r, 0)
                    lo = halo - SUBLANES * q
                    acc = acc + w_ref[k - 1 - s:k - s, ls] * rolled[r][lo:lo + sr]
                if has_bias:
                    acc = acc + b_ref[:, ls]
                o_ref[r0:r0 + sr, ls] = acc.astype(o_ref.dtype)
                if has_post:
                    y_ref[r0:r0 + sr, ls] = (acc * p_ref[r0:r0 + sr, ls].astype(F32)).astype(y_ref.dtype)

    def cur(off):
        return pl.BlockSpec((tm, tc), lambda j, i: (i, off // tc + j))

    def prev(off):
        return pl.BlockSpec((halo, tc), lambda j, i: (jnp.maximum(i * nb - 1, 0), off // tc + j))

    ins, specs = [x1, x1], [cur(c1), prev(c1)]
    if two:
        ins += [x2, x2]
        specs += [cur(c2), prev(c2)]
    ins.append(w)
    specs.append(pl.BlockSpec((k, tc), lambda j, i: (0, j)))
    if has_bias:
        ins.append(bias)
        specs.append(pl.BlockSpec((1, tc), lambda j, i: (0, j)))
    if has_post:
        ins.append(post)
        specs.append(cur(cpost))
    o_spec = pl.BlockSpec((tm, tc), lambda j, i: (i, j))
    shp = jax.ShapeDtypeStruct((t, c), BF16)
    return pl.pallas_call(
        body, name=name, grid=(c // tc, t // tm), in_specs=specs,
        out_specs=[o_spec, o_spec] if has_post else o_spec,
        out_shape=[shp, shp] if has_post else shp,
        scratch_shapes=[pltpu.VMEM((halo + tm, tc), F32)],
        compiler_params=_cp("parallel", "parallel"))(*ins)


def _conv_bwd(name, seq, c, w, d1, cd1, d2=None, cd2=0, dpre=None,
              x1=None, c1=0, x2=None, c2=0, pre=None):
    t = d1.shape[0]
    k = w.shape[0]
    halo, tm, tc, sr, sl = _conv_geom(t, seq, c, k)
    nb, cps = tm // halo, seq // tm
    nchunks = t // tm
    dtwo, xtwo, has_x = d2 is not None, x2 is not None, x1 is not None

    def body(*refs):
        it = iter(refs)
        d1c, d1n = next(it), next(it)
        d2c, d2n = (next(it), next(it)) if dtwo else (None, None)
        x1c, x1h = (next(it), next(it)) if has_x else (None, None)
        x2c, x2h = (next(it), next(it)) if xtwo else (None, None)
        w_ref = next(it)
        dx_ref = next(it)
        dw_ref = next(it) if has_x else None
        ds = next(it)
        xs = next(it) if has_x else None
        i = pl.program_id(1)
        last = (i % cps) == cps - 1
        ds[0:tm, :] = _pre(dpre, d1c[...].astype(F32), d2c[...].astype(F32) if dtwo else None)
        nv = _pre(dpre, d1n[...].astype(F32), d2n[...].astype(F32) if dtwo else None)
        ds[tm:tm + halo, :] = jnp.where(last, 0.0, nv)
        if has_x:
            first = (i % cps) == 0
            hv = _pre(pre, x1h[...].astype(F32), x2h[...].astype(F32) if xtwo else None)
            xs[0:halo, :] = jnp.where(first, 0.0, hv)
            xs[halo:halo + tm, :] = _pre(pre, x1c[...].astype(F32), x2c[...].astype(F32) if xtwo else None)

            @pl.when(i == 0)
            def _():
                dw_ref[...] = jnp.zeros_like(dw_ref)

        for l0 in range(0, tc, sl):
            ls = slice(l0, l0 + sl)
            for r0 in range(0, tm, sr):
                win = ds[r0:r0 + sr + halo, ls]
                nrow = sr + halo
                acc = jnp.zeros((sr, sl), F32)
                rolled = {}
                for r, q, s in _taps(k):
                    if r not in rolled:
                        rolled[r] = win if r == 0 else pltpu.roll(win, nrow - r, 0)
                    lo = SUBLANES * q
                    acc = acc + w_ref[k - 1 - s:k - s, ls] * rolled[r][lo:lo + sr]
                dx_ref[r0:r0 + sr, ls] = acc.astype(dx_ref.dtype)
                if has_x:
                    dcur = win[0:sr]
                    xwin = xs[r0:r0 + sr + halo, ls]
                    xrolled = {}
                    for r, q, s in _taps(k):
                        if r not in xrolled:
                            xrolled[r] = xwin if r == 0 else pltpu.roll(xwin, r, 0)
                        lo = halo - SUBLANES * q
                        part = jnp.sum(dcur * xrolled[r][lo:lo + sr], axis=0, keepdims=True)
                        dw_ref[k - 1 - s:k - s, ls] += part

    def cur(off):
        return pl.BlockSpec((tm, tc), lambda j, i: (i, off // tc + j))

    def prev(off):
        return pl.BlockSpec((halo, tc), lambda j, i: (jnp.maximum(i * nb - 1, 0), off // tc + j))

    def nxt(off):
        return pl.BlockSpec((halo, tc),
                            lambda j, i: (jnp.minimum((i + 1) * nb, nchunks * nb - 1), off // tc + j))

    ins, specs = [d1, d1], [cur(cd1), nxt(cd1)]
    if dtwo:
        ins += [d2, d2]
        specs += [cur(cd2), nxt(cd2)]
    if has_x:
        ins += [x1, x1]
        specs += [cur(c1), prev(c1)]
    if xtwo:
        ins += [x2, x2]
        specs += [cur(c2), prev(c2)]
    ins.append(w)
    specs.append(pl.BlockSpec((k, tc), lambda j, i: (0, j)))
    o_specs = [pl.BlockSpec((tm, tc), lambda j, i: (i, j))]
    o_shapes = [jax.ShapeDtypeStruct((t, c), BF16)]
    scratch = [pltpu.VMEM((tm + halo, tc), F32)]
    if has_x:
        o_specs.append(pl.BlockSpec((k, tc), lambda j, i: (0, j)))
        o_shapes.append(jax.ShapeDtypeStruct((k, c), F32))
        scratch.append(pltpu.VMEM((halo + tm, tc), F32))
    out = pl.pallas_call(
        body, name=name, grid=(c // tc, t // tm), in_specs=specs, out_specs=o_specs,
        out_shape=o_shapes, scratch_shapes=scratch,
        compiler_params=_cp("parallel", "arbitrary"))(*ins)
    return out if has_x else out[0]


def _pool_taps(c):
    kmax = max(POOL_WINDOWS)
    grp = c // len(POOL_WINDOWS)
    cols = []
    for wdw in POOL_WINDOWS:
        col = jnp.concatenate([jnp.zeros((kmax - wdw,), F32), jnp.ones((wdw,), F32)])
        cols.append(jnp.tile(col[:, None], (1, grp)))
    return jnp.concatenate(cols, axis=1)


def _counts(i, tr, seq, grp):
    pos = (i * tr + lax.broadcasted_iota(jnp.int32, (tr, 1), 0)) % seq + 1
    return [1.0 / jnp.minimum(pos, wdw).astype(F32) for wdw in POOL_WINDOWS]


def _ln_stats(a2):
    mu = jnp.mean(a2, axis=-1, keepdims=True)
    xc = a2 - mu
    rstd = lax.rsqrt(jnp.mean(xc * xc, axis=-1, keepdims=True) + EPS)
    return xc * rstd, rstd


def _even_fwd(name, seq, a2, ws, u, ln_g, ln_b, w_pool, scale):
    t, c = a2.shape
    ng = len(POOL_WINDOWS)
    grp = c // ng
    tr = _tile(t, 256, SUBLANES)

    def body(a_ref, ws_ref, b_ref, g_ref, bb_ref, wp_ref, sc_ref, z_ref, pm_ref):
        xhat, _ = _ln_stats(a_ref[...].astype(F32))
        l = xhat * g_ref[...] + bb_ref[...]
        z_ref[:, 0:c] = (l * _sigmoid(l)).astype(BF16)
        inv = _counts(pl.program_id(0), tr, seq, grp)
        for g in range(ng):
            gs = slice(g * grp, (g + 1) * grp)
            pm = (ws_ref[:, gs].astype(F32) * inv[g] - b_ref[:, gs].astype(F32)).astype(BF16)
            pm_ref[:, gs] = pm
            q = jnp.dot(pm, wp_ref[g], preferred_element_type=F32)
            z_ref[:, c + g * grp:c + (g + 1) * grp] = (q * sc_ref[:, gs]).astype(BF16)

    row = pl.BlockSpec((tr, c), lambda i: (i, 0))
    vec = pl.BlockSpec((1, c), lambda i: (0, 0))
    return pl.pallas_call(
        body, name=name, grid=(t // tr,),
        in_specs=[row, row, pl.BlockSpec((tr, c), lambda i: (i, 2)), vec, vec,
                  pl.BlockSpec((ng, grp, grp), lambda i: (0, 0, 0)), vec],
        out_specs=[pl.BlockSpec((tr, 2 * c), lambda i: (i, 0)), row],
        out_shape=[jax.ShapeDtypeStruct((t, 2 * c), BF16), jax.ShapeDtypeStruct((t, c), BF16)],
        compiler_params=_cp("parallel"))(a2, ws, u, ln_g, ln_b, w_pool, scale)


def _even_bwd(name, seq, dz, a2, pm, ln_g, ln_b, w_pool, scale):
    t, c = a2.shape
    ng = len(POOL_WINDOWS)
    grp = c // ng
    tr = _tile(t, 256, SUBLANES)

    def body(dz_ref, a_ref, pm_ref, g_ref, bb_ref, wp_ref, sc_ref,
             da_ref, dws_ref, dpm_ref, vec_ref, dwp_ref):
        i = pl.program_id(0)

        @pl.when(i == 0)
        def _():
            vec_ref[...] = jnp.zeros_like(vec_ref)
            dwp_ref[...] = jnp.zeros_like(dwp_ref)

        xhat, rstd = _ln_stats(a_ref[...].astype(F32))
        gv = g_ref[...]
        l = xhat * gv + bb_ref[...]
        sg = _sigmoid(l)
        dl = dz_ref[:, 0:c].astype(F32) * (sg * (1.0 + l * (1.0 - sg)))
        dxh = dl * gv
        da2 = rstd * (dxh - jnp.mean(dxh, axis=-1, keepdims=True)
                      - xhat * jnp.mean(dxh * xhat, axis=-1, keepdims=True))
        da_ref[...] = da2.astype(BF16)
        vec_ref[0:1, :] += jnp.sum(dl * xhat, axis=0, keepdims=True)
        vec_ref[1:2, :] += jnp.sum(dl, axis=0, keepdims=True)
        vec_ref[2:3, :] += jnp.sum(da2, axis=0, keepdims=True)
        inv = _counts(i, tr, seq, grp)
        for g in range(ng):
            gs = slice(g * grp, (g + 1) * grp)
            pmv = pm_ref[:, gs]
            wp = wp_ref[g]
            dp = dz_ref[:, c + g * grp:c + (g + 1) * grp].astype(F32)
            q = jnp.dot(pmv, wp, preferred_element_type=F32)
            vec_ref[3:4, gs] += jnp.sum(dp * q, axis=0, keepdims=True)
            dq = (dp * sc_ref[:, gs]).astype(BF16)
            dpm = lax.dot_general(dq, wp, NT, preferred_element_type=F32)
            dwp_ref[g] += lax.dot_general(pmv, dq, TN, preferred_element_type=F32)
            dpm_ref[:, gs] = dpm.astype(BF16)
            dws_ref[:, gs] = (dpm * inv[g]).astype(BF16)

    row = pl.BlockSpec((tr, c), lambda i: (i, 0))
    vec = pl.BlockSpec((1, c), lambda i: (0, 0))
    rshape = jax.ShapeDtypeStruct((t, c), BF16)
    return pl.pallas_call(
        body, name=name, grid=(t // tr,),
        in_specs=[pl.BlockSpec((tr, 2 * c), lambda i: (i, 0)), row, row, vec, vec,
                  pl.BlockSpec((ng, grp, grp), lambda i: (0, 0, 0)), vec],
        out_specs=[row, row, row, pl.BlockSpec((SUBLANES, c), lambda i: (0, 0)),
                   pl.BlockSpec((ng, grp, grp), lambda i: (0, 0, 0))],
        out_shape=[rshape, rshape, rshape, jax.ShapeDtypeStruct((SUBLANES, c), F32),
                   jax.ShapeDtypeStruct((ng, grp, grp), F32)],
        compiler_params=_cp("arbitrary"))(dz, a2, pm, ln_g, ln_b, w_pool, scale)


def _even_du(name, u, da1, dbp, dpm):
    t, c = da1.shape
    tr = _tile(t, 256, SUBLANES)

    def body(u_ref, da_ref, dbp_ref, dpm_ref, du_ref):
        val = u_ref[:, 0:c].astype(F32)
        sg = _sigmoid(u_ref[:, c:2 * c].astype(F32))
        da = da_ref[...].astype(F32)
        du_ref[:, 0:c] = (da * sg).astype(BF16)
        du_ref[:, c:2 * c] = (da * val * sg * (1.0 - sg)).astype(BF16)
        du_ref[:, 2 * c:3 * c] = (dbp_ref[...].astype(F32) - dpm_ref[...].astype(F32)).astype(BF16)

    row = pl.BlockSpec((tr, c), lambda i: (i, 0))
    wide = pl.BlockSpec((tr, 3 * c), lambda i: (i, 0))
    return pl.pallas_call(
        body, name=name, grid=(t // tr,), in_specs=[wide, row, row, row], out_specs=wide,
        out_shape=jax.ShapeDtypeStruct((t, 3 * c), BF16),
        compiler_params=_cp("parallel"))(u, da1, dbp, dpm)


def _odd_du(name, u, dy, co, dxc):
    t, c = dy.shape
    tr = _tile(t, 256, SUBLANES)

    def body(u_ref, dy_ref, co_ref, dx_ref, du_ref):
        dx = dx_ref[...].astype(F32)
        du_ref[:, 0:c] = (dy_ref[...].astype(F32) * co_ref[...].astype(F32)).astype(BF16)
        du_ref[:, c:2 * c] = (dx * u_ref[:, 2 * c:3 * c].astype(F32)).astype(BF16)
        du_ref[:, 2 * c:3 * c] = (dx * u_ref[:, c:2 * c].astype(F32)).astype(BF16)

    row = pl.BlockSpec((tr, c), lambda i: (i, 0))
    wide = pl.BlockSpec((tr, 3 * c), lambda i: (i, 0))
    return pl.pallas_call(
        body, name=name, grid=(t // tr,), in_specs=[wide, row, row, row], out_specs=wide,
        out_shape=jax.ShapeDtypeStruct((t, 3 * c), BF16),
        compiler_params=_cp("parallel"))(u, dy, co, dxc)


def _local_step(x, tgt, seq, small, get_w, put_g):
    t, d = x.shape
    c = d // 2
    cw_e, cw_o = small["conv_w_e"], small["conv_w_o"]
    wp = small["w_pool_e"].astype(BF16)
    ptaps = _pool_taps(c)
    row = lambda v: v.reshape(1, -1)

    we = get_w("mix_e", x)
    n0 = _rms_fwd("rms_fwd_mix0", x, row(small["mix_norm_e"]))
    u0 = _mm_nn("mm_in_e", n0, we["w_in"], BF16)
    a2 = _conv_fwd("conv_e_fwd", seq, c, cw_e, u0, 0, u0, c, "glu", bias=row(small["conv_b_e"]))
    ws = _conv_fwd("pool_fwd", seq, c, ptaps, u0, 2 * c)
    z0, pm = _even_fwd("even_fwd", seq, a2, ws, u0, row(small["ln_g_e"]), row(small["ln_b_e"]),
                       wp, row(small["pool_scale_e"]))
    h1 = _mm_nn("mm_out_e", z0, we["w_out"], F32, res=x)
    wf0 = get_w("ffn0", h1)
    n1 = _rms_fwd("rms_fwd_ffn0", h1, row(small["ffn_norm"][0]))
    g0, up0, act0 = _ffn_fwd("ffn0_fwd", n1, wf0["w_gate"], wf0["w_up"])
    h2 = _mm_down("mm_down0", act0, wf0["w_down"], h1)
    wo = get_w("mix_o", h2)
    n2 = _rms_fwd("rms_fwd_mix1", h2, row(small["mix_norm_o"]))
    u1 = _mm_nn("mm_in_o", n2, wo["w_in"], BF16)
    co, y1 = _conv_fwd("conv_o_fwd", seq, d, cw_o, u1, d, u1, 2 * d, "mul", post=u1, cpost=0)
    h3 = _mm_nn("mm_out_o", y1, wo["w_out"], F32, res=h2)
    wf1 = get_w("ffn1", h3)
    n3 = _rms_fwd("rms_fwd_ffn1", h3, row(small["ffn_norm"][1]))
    g1, up1, act1 = _ffn_fwd("ffn1_fwd", n3, wf1["w_gate"], wf1["w_up"])
    h4 = _mm_down("mm_down1", act1, wf1["w_down"], h3)

    dh4, dh4b, d_final, lsum = _loss_head("loss_head", h4, row(small["final_norm"]), tgt)

    def ffn_bwd(tag, dh, dhb, h_in, gain, n, g, up, act, w, dep):
        dg, dup = _ffn_bwd_act("ffn%s_bwd_act" % tag, dhb, w["w_down"], g, up, dep=dep)
        dwd = _mm_dwd("mm_dwd%s" % tag, act, dhb)
        dwg = _mm_dwg("mm_dwg%s" % tag, n, dg)
        dwu = _mm_dwg("mm_dwu%s" % tag, n, dup)
        dn = _mm_ffn_dn("mm_ffn_dn%s" % tag, dg, w["w_gate"], dup, w["w_up"])
        dh_in, dhb_in, dgain = _rms_bwd("rms_bwd_ffn%s" % tag, h_in, gain, dn, dh)
        dep = put_g("ffn" + tag, {"w_gate": dwg, "w_up": dwu, "w_down": dwd})
        return dh_in, dhb_in, dgain, dep

    dh3, dh3b, d_ffn1, dep = ffn_bwd("1", dh4, dh4b, h3, row(small["ffn_norm"][1]), n3, g1, up1,
                                     act1, wf1, None)
    dy1 = _mm_nt("mm_dy_o", dh3b, wo["w_out"], BF16, dep=dep)
    dw_out_o = _mm_tn("mm_dw_out_o", y1, dh3b, BF16)
    dxc, dcw_o = _conv_bwd("conv_o_bwd", seq, d, cw_o, dy1, 0, u1, 0, "mul",
                           x1=u1, c1=d, x2=u1, c2=2 * d, pre="mul")
    du1 = _odd_du("odd_du", u1, dy1, co, dxc)
    dw_in_o = _mm_tn("mm_dw_in_o", n2, du1, BF16)
    dn2 = _mm_nt("mm_dn_o", du1, wo["w_in"], BF16)
    dh2, dh2b, d_mix_o = _rms_bwd("rms_bwd_mix1", h2, row(small["mix_norm_o"]), dn2, dh3)
    dep = put_g("mix_o", {"w_in": dw_in_o, "w_out": dw_out_o})

    dh1, dh1b, d_ffn0, dep = ffn_bwd("0", dh2, dh2b, h1, row(small["ffn_norm"][0]), n1, g0, up0,
                                     act0, wf0, dep)
    dz0 = _mm_nt("mm_dz_e", dh1b, we["w_out"], BF16, dep=dep)
    dw_out_e = _mm_tn("mm_dw_out_e", z0, dh1b, BF16)
    da2, dws, dpm, vecs, dwp = _even_bwd("even_bwd", seq, dz0, a2, pm, row(small["ln_g_e"]),
                                         row(small["ln_b_e"]), wp, row(small["pool_scale_e"]))
    da1, dcw_e = _conv_bwd("conv_e_bwd", seq, c, cw_e, da2, 0, x1=u0, c1=0, x2=u0, c2=c, pre="glu")
    dbp = _conv_bwd("pool_bwd", seq, c, ptaps, dws, 0)
    du0 = _even_du("even_du", u0, da1, dbp, dpm)
    dw_in_e = _mm_tn("mm_dw_in_e", n0, du0, BF16)
    dn0 = _mm_nt("mm_dn_e", du0, we["w_in"], BF16)
    dx, _, d_mix_e = _rms_bwd("rms_bwd_mix0", x, row(small["mix_norm_e"]), dn0, dh1)
    put_g("mix_e", {"w_in": dw_in_e, "w_out": dw_out_e})

    gsmall = {"mix_norm_e": d_mix_e[0], "conv_w_e": dcw_e, "conv_b_e": vecs[2], "ln_g_e": vecs[0],
              "ln_b_e": vecs[1], "w_pool_e": dwp, "pool_scale_e": vecs[3], "mix_norm_o": d_mix_o[0],
              "conv_w_o": dcw_o, "ffn_norm": jnp.concatenate([d_ffn0, d_ffn1], axis=0),
              "final_norm": d_final[0]}
    return lsum, dx, gsmall


def _place():
    x, y, c = (lax.axis_index(a) for a in MESH_AXES)
    return x, y, c


def _index(p):
    return 4 * p[0] + 2 * p[1] + p[2]


def _slab(ref, kind, d, n):
    if kind == "blk":
        return ref.at[d]
    return ref.at[:, pl.ds(pl.multiple_of(d * n, LANES), n)]


def _all_gather(name, shards, kinds):
    na = len(shards)

    def body(*refs):
        x_refs, o_refs = refs[:na], refs[na:2 * na]
        send_sems, recv_sems, local_sems = refs[2 * na:]
        x, y, c = _place()
        me, sib = (x, y, c), (x, y, 1 - c)
        chips = [(1 - x, y), (x, 1 - y), (1 - x, 1 - y)]

        def slot(a, p):
            return _slab(o_refs[a], kinds[a], _index(p), shards[a].shape[1])

        def copy(a, k, block, to, src=None):
            return pltpu.make_async_remote_copy(
                src_ref=slot(a, block) if src is None else src, dst_ref=slot(a, block),
                send_sem=send_sems.at[a, k], recv_sem=recv_sems.at[a, k],
                device_id=to, device_id_type=MESH)

        mine = [pltpu.make_async_copy(x_refs[a], slot(a, me), local_sems.at[a]) for a in range(na)]
        for cp in mine:
            cp.start()
        first = []
        for a in range(na):
            first.append(copy(a, 0, me, sib, src=x_refs[a]))
            first += [copy(a, 1 + j, me, (*chip, c), src=x_refs[a]) for j, chip in enumerate(chips)]
        for cp in first:
            cp.start()
        passed = []
        for j, chip in enumerate(chips):
            for a in range(na):
                copy(a, 1 + j, (*chip, c), me).wait_recv()
                fwd = copy(a, 4 + j, (*chip, c), sib)
                fwd.start()
                passed.append(fwd)
        for a in range(na):
            copy(a, 0, sib, me).wait_recv()
            for j, chip in enumerate(chips):
                copy(a, 4 + j, (*chip, 1 - c), me).wait_recv()
        for cp in first + passed:
            cp.wait_send()
        for cp in mine:
            cp.wait()

    shapes = []
    for s, kind in zip(shards, kinds):
        m, n = s.shape
        shapes.append(jax.ShapeDtypeStruct((NDEV, m, n) if kind == "blk" else (m, NDEV * n), s.dtype))
    return pl.pallas_call(
        body, name=name, in_specs=[ANY] * na, out_specs=[ANY] * na, out_shape=shapes,
        scratch_shapes=[pltpu.SemaphoreType.DMA((na, 7)), pltpu.SemaphoreType.DMA((na, 7)),
                        pltpu.SemaphoreType.DMA((na,))])(*shards)


HBM = pl.BlockSpec(memory_space=pltpu.HBM)
SEM = pl.BlockSpec(memory_space=pltpu.SEMAPHORE)
EFFECT = pltpu.SideEffectType.DATAFLOW_SIDE_EFFECTING
NCHIPS = 4


def _in_hbm(a):
    return pltpu.with_memory_space_constraint(a, pltpu.HBM)


def _gathered_shape(s, kind):
    m, n = s.shape
    return (NDEV, m, n) if kind == "blk" else (m, NDEV * n)


def _first_targets():
    x, y, c = _place()
    return [(x, y, 1 - c), (1 - x, y, c), (x, 1 - y, c), (1 - x, 1 - y, c)]


def _gather_start(name, shards, kinds, after):
    na = len(shards)

    def body(*refs):
        x_refs, land_refs = refs[:na], refs[na:2 * na]
        send_sems, recv_sems = refs[2 * na + 1], refs[2 * na + 2]
        token = refs[-1]
        me = _index(_place())
        for a in range(na):
            for k, to in enumerate(_first_targets()):
                pltpu.make_async_remote_copy(
                    src_ref=x_refs[a], dst_ref=_slab(land_refs[a], kinds[a], me, shards[a].shape[1]),
                    send_sem=send_sems.at[4 * a + k], recv_sem=recv_sems.at[4 * a + k],
                    device_id=to, device_id_type=MESH).start()
        token[...] = jnp.zeros_like(token)

    lands = [lax.empty(_gathered_shape(s, k), s.dtype) for s, k in zip(shards, kinds)]
    outs = pl.pallas_call(
        body, name=name,
        out_shape=(pltpu.SemaphoreType.DMA((4 * na,)), pltpu.SemaphoreType.DMA((4 * na,)),
                   *[pltpu.HBM(s.shape, s.dtype) for s in shards],
                   *[pltpu.HBM(l.shape, l.dtype) for l in lands],
                   jax.ShapeDtypeStruct((SUBLANES, LANES), F32)),
        in_specs=[HBM] * (2 * na) + [ANY],
        out_specs=(SEM, SEM, *[HBM] * (2 * na), pl.BlockSpec(memory_space=pltpu.VMEM)),
        input_output_aliases={i: 2 + i for i in range(2 * na)},
        compiler_params=pltpu.CompilerParams(has_side_effects=EFFECT),
    )(*[_in_hbm(s) for s in shards], *[_in_hbm(l) for l in lands], after)
    return outs[0], outs[1], outs[2:2 + na], outs[2 + na:2 + 2 * na], outs[-1]


def _gather_wait(name, started, kinds, after):
    send_sems, recv_sems, shards, lands, _ = started
    na = len(shards)

    def body(*refs):
        x_refs, land_refs = refs[:na], refs[na:2 * na]
        s_sems, r_sems = refs[2 * na], refs[2 * na + 1]
        for a in range(na):
            for k, frm in enumerate(_first_targets()):
                cp = pltpu.make_async_remote_copy(
                    src_ref=x_refs[a],
                    dst_ref=_slab(land_refs[a], kinds[a], _index(frm), shards[a].shape[1]),
                    send_sem=s_sems.at[4 * a + k], recv_sem=r_sems.at[4 * a + k],
                    device_id=frm, device_id_type=MESH)
                cp.wait_send()
                cp.wait_recv()

    outs = pl.pallas_call(
        body, name=name,
        out_shape=(*[pltpu.HBM(s.shape, s.dtype) for s in shards],
                   *[pltpu.HBM(l.shape, l.dtype) for l in lands]),
        in_specs=[HBM] * (2 * na) + [SEM, SEM, ANY], out_specs=[HBM] * (2 * na),
        input_output_aliases={i: i for i in range(2 * na)},
        compiler_params=pltpu.CompilerParams(has_side_effects=EFFECT),
    )(*shards, *lands, send_sems, recv_sems, after)
    return outs[:na], outs[na:]


def _gather_finish(name, shards, lands, kinds):
    na = len(shards)

    def body(*refs):
        x_refs, land_refs = refs[:na], refs[2 * na:3 * na]
        send_sems, recv_sems, local_sems = refs[3 * na:]
        x, y, c = _place()
        chips = [(1 - x, y), (x, 1 - y), (1 - x, 1 - y)]

        def slot(a, p):
            return _slab(land_refs[a], kinds[a], _index(p), shards[a].shape[1])

        def copy(a, j, block):
            return pltpu.make_async_remote_copy(
                src_ref=slot(a, block), dst_ref=slot(a, block),
                send_sem=send_sems.at[a, j], recv_sem=recv_sems.at[a, j],
                device_id=(x, y, 1 - c), device_id_type=MESH)

        mine = [pltpu.make_async_copy(x_refs[a], slot(a, (x, y, c)), local_sems.at[a]) for a in range(na)]
        sent = [copy(a, j, (*chip, c)) for a in range(na) for j, chip in enumerate(chips)]
        for cp in mine + sent:
            cp.start()
        for a in range(na):
            for j, chip in enumerate(chips):
                copy(a, j, (*chip, 1 - c)).wait_recv()
        for cp in sent:
            cp.wait_send()
        for cp in mine:
            cp.wait()

    return pl.pallas_call(
        body, name=name, in_specs=[ANY] * (2 * na), out_specs=[ANY] * na,
        out_shape=[jax.ShapeDtypeStruct(l.shape, l.dtype) for l in lands],
        input_output_aliases={na + i: i for i in range(na)},
        scratch_shapes=[pltpu.SemaphoreType.DMA((na, 3)), pltpu.SemaphoreType.DMA((na, 3)),
                        pltpu.SemaphoreType.DMA((na,))])(*shards, *lands)


def _pair_exchange(name, fulls, kinds, nloc):
    na = len(fulls)

    def body(*refs):
        g_refs, o_refs = refs[:na], refs[na:2 * na]
        send_sems, recv_sems = refs[2 * na:]
        x, y, c = _place()

        def copy(a, j):
            return pltpu.make_async_remote_copy(
                src_ref=_slab(g_refs[a], kinds[a], 2 * j + (1 - c), nloc[a]), dst_ref=o_refs[a].at[j],
                send_sem=send_sems.at[a, j], recv_sem=recv_sems.at[a, j],
                device_id=(x, y, 1 - c), device_id_type=MESH)

        sent = [copy(a, j) for a in range(na) for j in range(NCHIPS)]
        for cp in sent:
            cp.start()
        for cp in sent:
            cp.wait_recv()
        for cp in sent:
            cp.wait_send()

    shapes = []
    for g, kind, n in zip(fulls, kinds, nloc):
        m = g.shape[1] if kind == "blk" else g.shape[0]
        shapes.append(jax.ShapeDtypeStruct((NCHIPS, m, n), g.dtype))
    return pl.pallas_call(
        body, name=name, in_specs=[ANY] * na, out_specs=[ANY] * na, out_shape=shapes,
        scratch_shapes=[pltpu.SemaphoreType.DMA((na, NCHIPS)), pltpu.SemaphoreType.DMA((na, NCHIPS))])(*fulls)


def _chip_sum(name, full, kind, n, from_sib, place):
    _, m, _ = from_sib.shape
    tr = _tile(m, max(SUBLANES, 262144 // n), SUBLANES)

    def body(s_ref, mine_ref, sib_ref, csum_ref, land_ref):
        v = (mine_ref[...].astype(F32) + sib_ref[...].astype(F32)).astype(csum_ref.dtype)
        csum_ref[...] = v

        @pl.when(pl.program_id(1) == s_ref[1])
        def _():
            land_ref[...] = v

    if kind == "blk":
        mine_spec = pl.BlockSpec((None, tr, n), lambda i, j, s: (2 * j + s[0], i, 0))
    else:
        mine_spec = pl.BlockSpec((tr, n), lambda i, j, s: (i, 2 * j + s[0]))
    slot = pl.BlockSpec((None, tr, n), lambda i, j, s: (j, i, 0))
    shp = jax.ShapeDtypeStruct((NCHIPS, m, n), from_sib.dtype)
    return pl.pallas_call(
        body, name=name,
        grid_spec=pltpu.PrefetchScalarGridSpec(
            num_scalar_prefetch=1, grid=(m // tr, NCHIPS), in_specs=[mine_spec, slot],
            out_specs=[slot, pl.BlockSpec((None, tr, n), lambda i, j, s: (s[1], i, 0))]),
        out_shape=[shp, shp], compiler_params=_cp("parallel", "arbitrary"))(place, full, from_sib)


def _other_chips():
    x, y, c = _place()
    return [(1 - x, y, c), (x, 1 - y, c), (1 - x, 1 - y, c)]


def _scatter_start(name, csums, lands, after):
    na = len(csums)

    def body(*refs):
        c_refs, land_refs = refs[:na], refs[na:2 * na]
        send_sems, recv_sems = refs[2 * na + 1], refs[2 * na + 2]
        token = refs[-1]
        x, y, _ = _place()
        for a in range(na):
            for k, to in enumerate(_other_chips()):
                pltpu.make_async_remote_copy(
                    src_ref=c_refs[a].at[2 * to[0] + to[1]], dst_ref=land_refs[a].at[2 * x + y],
                    send_sem=send_sems.at[3 * a + k], recv_sem=recv_sems.at[3 * a + k],
                    device_id=to, device_id_type=MESH).start()
        token[...] = jnp.zeros_like(token)

    outs = pl.pallas_call(
        body, name=name,
        out_shape=(pltpu.SemaphoreType.DMA((3 * na,)), pltpu.SemaphoreType.DMA((3 * na,)),
                   *[pltpu.HBM(s.shape, s.dtype) for s in csums],
                   *[pltpu.HBM(l.shape, l.dtype) for l in lands],
                   jax.ShapeDtypeStruct((SUBLANES, LANES), F32)),
        in_specs=[HBM] * (2 * na) + [ANY],
        out_specs=(SEM, SEM, *[HBM] * (2 * na), pl.BlockSpec(memory_space=pltpu.VMEM)),
        input_output_aliases={i: 2 + i for i in range(2 * na)},
        compiler_params=pltpu.CompilerParams(has_side_effects=EFFECT),
    )(*[_in_hbm(s) for s in csums], *[_in_hbm(l) for l in lands], after)
    return outs[0], outs[1], outs[2:2 + na], outs[2 + na:2 + 2 * na], outs[-1]


def _scatter_wait(name, started, after):
    send_sems, recv_sems, csums, lands, _ = started
    na = len(csums)

    def body(*refs):
        c_refs, land_refs = refs[:na], refs[na:2 * na]
        s_sems, r_sems = refs[2 * na], refs[2 * na + 1]
        for a in range(na):
            for k, frm in enumerate(_other_chips()):
                cp = pltpu.make_async_remote_copy(
                    src_ref=c_refs[a].at[2 * frm[0] + frm[1]], dst_ref=land_refs[a].at[2 * frm[0] + frm[1]],
                    send_sem=s_sems.at[3 * a + k], recv_sem=r_sems.at[3 * a + k],
                    device_id=frm, device_id_type=MESH)
                cp.wait_send()
                cp.wait_recv()

    outs = pl.pallas_call(
        body, name=name,
        out_shape=(*[pltpu.HBM(s.shape, s.dtype) for s in csums],
                   *[pltpu.HBM(l.shape, l.dtype) for l in lands]),
        in_specs=[HBM] * (2 * na) + [SEM, SEM, ANY], out_specs=[HBM] * (2 * na),
        input_output_aliases={i: i for i in range(2 * na)},
        compiler_params=pltpu.CompilerParams(has_side_effects=EFFECT),
    )(*csums, *lands, send_sems, recv_sems, after)
    return outs[na:]


def _adam_math(w, g, m, v):
    m = ADAM_B1 * m + (1.0 - ADAM_B1) * g
    v = ADAM_B2 * v + (1.0 - ADAM_B2) * (g * g)
    m_hat = m / (1.0 - ADAM_B1 ** ADAM_STEP)
    v_hat = v / (1.0 - ADAM_B2 ** ADAM_STEP)
    delta = -ADAM_LR * (m_hat / (jnp.sqrt(v_hat) + ADAM_EPS) + ADAM_WD * w)
    return delta, m, v


def _sum_adamw(name, parts, w, m, v):
    r, c = w.shape
    nparts = parts.shape[0]
    tr = _tile(r, max(SUBLANES, 262144 // c), SUBLANES)

    def body(p_ref, w_ref, m_ref, v_ref, g_ref, d_ref, mo_ref, vo_ref):
        g = p_ref[0].astype(F32)
        for s in range(1, nparts):
            g = g + p_ref[s].astype(F32)
        delta, mn, vn = _adam_math(w_ref[...], g, m_ref[...], v_ref[...])
        g_ref[...] = g
        d_ref[...] = delta
        mo_ref[...] = mn
        vo_ref[...] = vn

    row = pl.BlockSpec((tr, c), lambda i: (i, 0))
    shp = jax.ShapeDtypeStruct((r, c), F32)
    return pl.pallas_call(
        body, name=name, grid=(r // tr,),
        in_specs=[pl.BlockSpec((nparts, tr, c), lambda i: (0, i, 0)), row, row, row],
        out_specs=[row, row, row, row], out_shape=[shp, shp, shp, shp],
        compiler_params=_cp("parallel"))(parts, w, m, v)


def _sum_parts(name, parts):
    _, r, c = parts.shape

    def body(p_ref, o_ref):
        g = p_ref[0]
        for s in range(1, NDEV):
            g = g + p_ref[s]
        o_ref[...] = g

    return pl.pallas_call(
        body, name=name, grid=(1,),
        in_specs=[pl.BlockSpec((NDEV, r, c), lambda i: (0, 0, 0))],
        out_specs=pl.BlockSpec((r, c), lambda i: (0, 0)),
        out_shape=jax.ShapeDtypeStruct((r, c), F32), compiler_params=_cp("arbitrary"))(parts)


def _adamw(name, w, g, m, v):
    r, c = w.shape

    def body(w_ref, g_ref, m_ref, v_ref, d_ref, mo_ref, vo_ref):
        delta, mn, vn = _adam_math(w_ref[...], g_ref[...], m_ref[...], v_ref[...])
        d_ref[...] = delta
        mo_ref[...] = mn
        vo_ref[...] = vn

    full = pl.BlockSpec((r, c), lambda i: (0, 0))
    shp = jax.ShapeDtypeStruct((r, c), F32)
    return pl.pallas_call(
        body, name=name, grid=(1,), in_specs=[full] * 4, out_specs=[full] * 3,
        out_shape=[shp] * 3, compiler_params=_cp("arbitrary"))(w, g, m, v)


def _pack(arrays):
    flat = jnp.concatenate([a.reshape(-1) for a in arrays])
    unit = SUBLANES * LANES
    pad = (-flat.shape[0]) % unit
    return jnp.pad(flat, (0, pad)).reshape(-1, LANES)


def _unpack(buf, shapes):
    flat = buf.reshape(-1)
    out, off = [], 0
    for shp in shapes:
        size = 1
        for s in shp:
            size *= s
        out.append(flat[off:off + size].reshape(shp))
        off += size
    return out


WEIGHTS = ["mix_norm_e", "w_in_e", "conv_w_e", "conv_b_e", "ln_g_e", "ln_b_e", "w_pool_e",
           "pool_scale_e", "w_out_e", "mix_norm_o", "w_in_o", "conv_w_o", "w_out_o", "ffn_norm",
           "w_gate", "w_up", "w_down", "final_norm"]
BIG = ["w_in_e", "w_out_e", "w_in_o", "w_out_o", "w_gate", "w_up", "w_down"]
SHARDED_SMALL = {"conv_w_e": 1, "w_pool_e": 1, "mix_norm_o": 0, "conv_w_o": 1}
SMALL = [n for n in WEIGHTS if n not in BIG]


def kernel(x, mix_norm_e, w_in_e, conv_w_e, conv_b_e, ln_g_e, ln_b_e, w_pool_e, pool_scale_e, w_out_e, mix_norm_o, w_in_o, conv_w_o, w_out_o, ffn_norm, w_gate, w_up, w_down, final_norm, loss_target, m_mix_norm_e, m_w_in_e, m_conv_w_e, m_conv_b_e, m_ln_g_e, m_ln_b_e, m_w_pool_e, m_pool_scale_e, m_w_out_e, m_mix_norm_o, m_w_in_o, m_conv_w_o, m_w_out_o, m_ffn_norm, m_w_gate, m_w_up, m_w_down, m_final_norm, v_mix_norm_e, v_w_in_e, v_conv_w_e, v_conv_b_e, v_ln_g_e, v_ln_b_e, v_w_pool_e, v_pool_scale_e, v_w_out_e, v_mix_norm_o, v_w_in_o, v_conv_w_o, v_w_out_o, v_ffn_norm, v_w_gate, v_w_up, v_w_down, v_final_norm):
    wts = dict(zip(WEIGHTS, [mix_norm_e, w_in_e, conv_w_e, conv_b_e, ln_g_e, ln_b_e, w_pool_e, pool_scale_e, w_out_e, mix_norm_o, w_in_o, conv_w_o, w_out_o, ffn_norm, w_gate, w_up, w_down, final_norm]))
    mom = dict(zip(WEIGHTS, [m_mix_norm_e, m_w_in_e, m_conv_w_e, m_conv_b_e, m_ln_g_e, m_ln_b_e, m_w_pool_e, m_pool_scale_e, m_w_out_e, m_mix_norm_o, m_w_in_o, m_conv_w_o, m_w_out_o, m_ffn_norm, m_w_gate, m_w_up, m_w_down, m_final_norm]))
    var = dict(zip(WEIGHTS, [v_mix_norm_e, v_w_in_e, v_conv_w_e, v_conv_b_e, v_ln_g_e, v_ln_b_e, v_w_pool_e, v_pool_scale_e, v_w_out_e, v_mix_norm_o, v_w_in_o, v_conv_w_o, v_w_out_o, v_ffn_norm, v_w_gate, v_w_up, v_w_down, v_final_norm]))
    bsz, seq, d = x.shape
    t = bsz * seq
    me = _index(_place())

    bf = lambda a: a.astype(BF16)
    mix_kinds, ffn_kinds = ["col", "blk"], ["blk", "blk", "blk"]
    groups = {
        "mix_e": ([bf(w_in_e[0]), bf(w_out_e[0])], mix_kinds, [("w_in_e", 0), ("w_out_e", 0)]),
        "ffn0": ([bf(w_gate[0]), bf(w_up[0]), bf(w_down[0])], ffn_kinds,
                 [("w_gate", 0), ("w_up", 0), ("w_down", 0)]),
        "mix_o": ([bf(w_in_o[0]), bf(w_out_o[0])], mix_kinds, [("w_in_o", 0), ("w_out_o", 0)]),
        "ffn1": ([bf(w_gate[1]), bf(w_up[1]), bf(w_down[1])], ffn_kinds,
                 [("w_gate", 1), ("w_up", 1), ("w_down", 1)]),
    }
    started, prev = {}, x
    for grp, (shards, kinds, _) in groups.items():
        started[grp] = _gather_start("gather_start_" + grp, shards, kinds, prev)
        prev = started[grp][4]
    all_started = prev[0, 0:1]

    def get_w(grp, after):
        _, kinds, _ = groups[grp]
        shards, lands = _gather_wait("gather_wait_" + grp, started[grp], kinds, after)
        full = _gather_finish("gather_finish_" + grp, shards, lands, kinds)
        if len(full) == 2:
            return {"w_in": full[0], "w_out": full[1].reshape(-1, d)}
        return {"w_gate": full[0], "w_up": full[1], "w_down": full[2]}

    cx, cy, cc = _place()
    place = jnp.stack([cc, 2 * cx + cy]).astype(jnp.int32)
    bwd_order = ["ffn1", "mix_o", "ffn0", "mix_e"]
    pending, results = {}, {}

    def finish(grp, after):
        lands = _scatter_wait("scatter_wait_" + grp, pending[grp], after)
        for (n, l), parts in zip(groups[grp][2], lands):
            results[(n, l)] = _sum_adamw("adamw_%s%d" % (n, l), parts, wts[n][l], mom[n][l], var[n][l])

    def put_g(grp, grads):
        shards, kinds, _ = groups[grp]
        nloc = [s.shape[1] for s in shards]
        if len(shards) == 2:
            fulls = [grads["w_in"], grads["w_out"].reshape(NDEV, -1, d)]
        else:
            fulls = [grads["w_gate"], grads["w_up"], grads["w_down"]]
        from_sib = _pair_exchange("pair_" + grp, fulls, kinds, nloc)
        sums = [_chip_sum("chip_sum_%s%d" % (grp, a), fulls[a], kinds[a], nloc[a], from_sib[a], place)
                for a in range(len(fulls))]
        pending[grp] = _scatter_start("scatter_start_" + grp, [s[0] for s in sums], [s[1] for s in sums],
                                      from_sib[0])
        token = pending[grp][4]
        i = bwd_order.index(grp)
        if i > 0:
            finish(bwd_order[i - 1], token)
        return token

    sh_names = list(SHARDED_SMALL)
    sh_local = [wts[n][0] for n in sh_names]
    packed = _pack(sh_local)
    gathered, = _all_gather("gather_small", [packed], ["blk"])
    small = {n: wts[n][0] for n in SMALL if n not in SHARDED_SMALL and n not in ("ffn_norm", "final_norm")}
    small["ffn_norm"], small["final_norm"] = ffn_norm, final_norm
    per_dev = [_unpack(gathered[s], [a.shape for a in sh_local]) for s in range(NDEV)]
    for i, n in enumerate(sh_names):
        small[n] = jnp.concatenate([per_dev[s][i] for s in range(NDEV)], axis=SHARDED_SMALL[n])

    small["mix_norm_e"] = small["mix_norm_e"] + all_started
    lsum, dx, gsmall = _local_step(x.reshape(t, d), loss_target.reshape(t, d), seq, small, get_w, put_g)
    loss = lax.psum(jnp.sum(lsum), MESH_AXES)
    finish(bwd_order[-1], dx)

    out_g, out_d, out_m, out_v = {}, {}, {}, {}
    for n in ("w_in_e", "w_out_e", "w_in_o", "w_out_o"):
        out_g[n], out_d[n], out_m[n], out_v[n] = (a[None] for a in results[(n, 0)])
    for n in ("w_gate", "w_up", "w_down"):
        out_g[n], out_d[n], out_m[n], out_v[n] = (
            jnp.stack([results[(n, 0)][i], results[(n, 1)][i]]) for i in range(4))

    gs_list = [gsmall[n] for n in SMALL]
    gs_all, = _all_gather("gather_small_grads", [_pack(gs_list)], ["blk"])
    gs_sum = _unpack(_sum_parts("sum_small_grads", gs_all), [a.shape for a in gs_list])
    local_g = []
    for n, g in zip(SMALL, gs_sum):
        if n in SHARDED_SMALL:
            ax = SHARDED_SMALL[n]
            size = wts[n].shape[ax + 1]
            g = lax.dynamic_slice_in_dim(g, me * size, size, axis=ax)
        local_g.append(g.reshape(wts[n].shape))
    shapes = [wts[n].shape for n in SMALL]
    upd = _adamw("adamw_small", _pack([wts[n] for n in SMALL]), _pack(local_g),
                 _pack([mom[n] for n in SMALL]), _pack([var[n] for n in SMALL]))
    for i, outd in enumerate((out_d, out_m, out_v)):
        for n, a in zip(SMALL, _unpack(upd[i], shapes)):
            outd[n] = a
    for n, g in zip(SMALL, local_g):
        out_g[n] = g

    return (loss, dx.reshape(bsz, seq, d), *[out_g[n] for n in WEIGHTS], *[out_d[n] for n in WEIGHTS],
            *[out_m[n] for n in WEIGHTS], *[out_v[n] for n in WEIGHTS])
```

```python
import functools

import jax
import jax.numpy as jnp
from jax import lax
from jax.experimental import pallas as pl
from jax.experimental.pallas import tpu as pltpu

F32 = jnp.float32
BF16 = jnp.bfloat16
NDEV = 8
MESH_AXES = ("x", "y", "c")
EPS = 1e-6
POOL_WINDOWS = (2, 4, 8, 16)
CONV_WIDTH = 31
SHORT_WIDTH = 3
ADAM_LR = 0.001
ADAM_B1 = 0.9
ADAM_B2 = 0.999
ADAM_EPS = 1e-08
ADAM_WD = 0.01
ADAM_STEP = 10
LANES = 128
SUBLANES = 8
VMEM_LIMIT = 56 * 1024 * 1024
MESH = pl.DeviceIdType.MESH
ANY = pl.BlockSpec(memory_space=pl.ANY)


def _cp(*sem):
    return pltpu.CompilerParams(dimension_semantics=sem, vmem_limit_bytes=VMEM_LIMIT)


def _tile(n, pref, unit=LANES):
    if n <= pref:
        return n
    t = (pref // unit) * unit
    while t > unit and n % t:
        t -= unit
    assert n % t == 0, (n, pref)
    return t


def _sigmoid(v):
    return 1.0 / (1.0 + jnp.exp(-v))


def _mm(name, pairs, a_specs, b_specs, dims, out_shape, o_spec, grid, acc_shape,
        res=None, res_spec=None, dep=None):
    np_ = len(pairs)
    nk = grid[2]
    has_res = res is not None
    n_in = 2 * np_ + (1 if has_res else 0) + (0 if dep is None else 1)

    def body(*refs):
        a_refs = refs[:np_]
        b_refs = refs[np_:2 * np_]
        r_ref = refs[2 * np_] if has_res else None
        o_ref = refs[n_in]
        acc = refs[-1]

        def part():
            s = None
            for a_ref, b_ref in zip(a_refs, b_refs):
                d = lax.dot_general(a_ref[...], b_ref[...], dims, preferred_element_type=F32)
                s = d if s is None else s + d
            return s

        def finish(v):
            if has_res:
                v = v + r_ref[...]
            o_ref[...] = v.astype(o_ref.dtype)

        if nk == 1:
            finish(part())
        else:
            k = pl.program_id(2)

            @pl.when(k == 0)
            def _():
                acc[...] = part()

            @pl.when(k > 0)
            def _():
                acc[...] += part()

            @pl.when(k == nk - 1)
            def _():
                finish(acc[...])

    ins = [p[0] for p in pairs] + [p[1] for p in pairs]
    specs = list(a_specs) + list(b_specs)
    if has_res:
        ins.append(res)
        specs.append(res_spec)
    if dep is not None:
        ins.append(dep)
        specs.append(ANY)
    return pl.pallas_call(
        body, name=name, grid=grid, in_specs=specs, out_specs=o_spec, out_shape=out_shape,
        scratch_shapes=[pltpu.VMEM(acc_shape if nk > 1 else (SUBLANES, LANES), F32)],
        compiler_params=_cp("parallel", "parallel", "arbitrary"))(*ins)


NN = (((1,), (0,)), ((), ()))
NT = (((1,), (1,)), ((), ()))
TN = (((0,), (0,)), ((), ()))


def _mm_nn(name, a, b, out_dtype, res=None, dep=None):
    m, kk = a.shape
    n = b.shape[1]
    tm, tn, tk = _tile(m, 1024), _tile(n, 1024), _tile(kk, 1024)
    return _mm(name, [(a, b)],
               [pl.BlockSpec((tm, tk), lambda i, j, k: (i, k))],
               [pl.BlockSpec((tk, tn), lambda i, j, k: (k, j))], NN,
               jax.ShapeDtypeStruct((m, n), out_dtype),
               pl.BlockSpec((tm, tn), lambda i, j, k: (i, j)),
               (m // tm, n // tn, kk // tk), (tm, tn), res,
               pl.BlockSpec((tm, tn), lambda i, j, k: (i, j)), dep=dep)


def _mm_nt(name, a, b, out_dtype, dep=None):
    m, n = a.shape
    kk = b.shape[0]
    tm, tn, tk = _tile(m, 1024), _tile(kk, 1024), _tile(n, 1024)
    return _mm(name, [(a, b)],
               [pl.BlockSpec((tm, tk), lambda i, j, k: (i, k))],
               [pl.BlockSpec((tn, tk), lambda i, j, k: (j, k))], NT,
               jax.ShapeDtypeStruct((m, kk), out_dtype),
               pl.BlockSpec((tm, tn), lambda i, j, k: (i, j)),
               (m // tm, kk // tn, n // tk), (tm, tn), dep=dep)


def _mm_tn(name, a, b, out_dtype, dep=None):
    t, m = a.shape
    n = b.shape[1]
    tm, tn, tk = _tile(m, 1024), _tile(n, 1024), _tile(t, 1024)
    return _mm(name, [(a, b)],
               [pl.BlockSpec((tk, tm), lambda i, j, k: (k, i))],
               [pl.BlockSpec((tk, tn), lambda i, j, k: (k, j))], TN,
               jax.ShapeDtypeStruct((m, n), out_dtype),
               pl.BlockSpec((tm, tn), lambda i, j, k: (i, j)),
               (m // tm, n // tn, t // tk), (tm, tn), dep=dep)


def _mm_down(name, act, wd, res, dep=None):
    nb, t, f = act.shape
    d = wd.shape[2]
    tm, tn = _tile(t, 1024), _tile(d, 1024)
    return _mm(name, [(act, wd)],
               [pl.BlockSpec((None, tm, f), lambda i, j, k: (k, i, 0))],
               [pl.BlockSpec((None, f, tn), lambda i, j, k: (k, 0, j))], NN,
               jax.ShapeDtypeStruct((t, d), F32),
               pl.BlockSpec((tm, tn), lambda i, j, k: (i, j)),
               (t // tm, d // tn, nb), (tm, tn), res,
               pl.BlockSpec((tm, tn), lambda i, j, k: (i, j)), dep=dep)


def _mm_ffn_dn(name, dg, wg, dup, wu):
    nb, t, f = dg.shape
    d = wg.shape[1]
    tm, tn = _tile(t, 1024), _tile(d, 1024)
    a_spec = pl.BlockSpec((None, tm, f), lambda i, j, k: (k, i, 0))
    b_spec = pl.BlockSpec((None, tn, f), lambda i, j, k: (k, j, 0))
    return _mm(name, [(dg, wg), (dup, wu)], [a_spec, a_spec], [b_spec, b_spec], NT,
               jax.ShapeDtypeStruct((t, d), BF16),
               pl.BlockSpec((tm, tn), lambda i, j, k: (i, j)),
               (t // tm, d // tn, nb), (tm, tn))


def _mm_dwd(name, act, dh):
    nb, t, f = act.shape
    d = dh.shape[1]
    tn, tk = _tile(d, 1024), _tile(t, 1024)
    return _mm(name, [(act, dh)],
               [pl.BlockSpec((None, tk, f), lambda i, j, k: (i, k, 0))],
               [pl.BlockSpec((tk, tn), lambda i, j, k: (k, j))], TN,
               jax.ShapeDtypeStruct((nb, f, d), BF16),
               pl.BlockSpec((None, f, tn), lambda i, j, k: (i, 0, j)),
               (nb, d // tn, t // tk), (f, tn))


def _mm_dwg(name, n, dg, dep=None):
    nb, t, f = dg.shape
    d = n.shape[1]
    tm, tk = _tile(d, 1024), _tile(t, 1024)
    return _mm(name, [(n, dg)],
               [pl.BlockSpec((tk, tm), lambda i, j, k: (k, i))],
               [pl.BlockSpec((None, tk, f), lambda i, j, k: (j, k, 0))], TN,
               jax.ShapeDtypeStruct((nb, d, f), BF16),
               pl.BlockSpec((None, tm, f), lambda i, j, k: (j, i, 0)),
               (d // tm, nb, t // tk), (tm, f), dep=dep)


def _ffn_fwd(name, n, wg, wu):
    nb, d, f = wg.shape
    t = n.shape[0]
    tm = _tile(t, 512)

    def body(n_ref, wg_ref, wu_ref, g_ref, up_ref, act_ref):
        nv = n_ref[...]
        g = jnp.dot(nv, wg_ref[...], preferred_element_type=F32)
        up = jnp.dot(nv, wu_ref[...], preferred_element_type=F32)
        g_ref[...] = g.astype(BF16)
        up_ref[...] = up.astype(BF16)
        act_ref[...] = (g * _sigmoid(g) * up).astype(BF16)

    w_spec = pl.BlockSpec((None, d, f), lambda j, i: (j, 0, 0))
    o_spec = pl.BlockSpec((None, tm, f), lambda j, i: (j, i, 0))
    shp = jax.ShapeDtypeStruct((nb, t, f), BF16)
    return pl.pallas_call(
        body, name=name, grid=(nb, t // tm),
        in_specs=[pl.BlockSpec((tm, d), lambda j, i: (i, 0)), w_spec, w_spec],
        out_specs=[o_spec, o_spec, o_spec], out_shape=[shp, shp, shp],
        compiler_params=_cp("parallel", "parallel"))(n, wg, wu)


def _ffn_bwd_act(name, dh, wd, g, up, dep=None):
    nb, f, d = wd.shape
    t = dh.shape[0]
    tm = _tile(t, 512)

    def body(dh_ref, wd_ref, g_ref, up_ref, *rest):
        dg_ref, dup_ref = rest[-2:]
        da = lax.dot_general(dh_ref[...], wd_ref[...], NT, preferred_element_type=F32)
        gv = g_ref[...].astype(F32)
        uv = up_ref[...].astype(F32)
        sg = _sigmoid(gv)
        dg_ref[...] = (da * uv * (sg * (1.0 + gv * (1.0 - sg)))).astype(BF16)
        dup_ref[...] = (da * gv * sg).astype(BF16)

    o_spec = pl.BlockSpec((None, tm, f), lambda j, i: (j, i, 0))
    shp = jax.ShapeDtypeStruct((nb, t, f), BF16)
    return pl.pallas_call(
        body, name=name, grid=(nb, t // tm),
        in_specs=[pl.BlockSpec((tm, d), lambda j, i: (i, 0)),
                  pl.BlockSpec((None, f, d), lambda j, i: (j, 0, 0)), o_spec, o_spec]
        + ([] if dep is None else [ANY]),
        out_specs=[o_spec, o_spec], out_shape=[shp, shp],
        compiler_params=_cp("parallel", "parallel"))(dh, wd, g, up, *([] if dep is None else [dep]))


def _rms_fwd(name, h, gain):
    t, d = h.shape
    tr = _tile(t, 512, SUBLANES)

    def body(h_ref, g_ref, n_ref):
        hv = h_ref[...]
        r = lax.rsqrt(jnp.mean(hv * hv, axis=-1, keepdims=True) + EPS)
        n_ref[...] = (hv * r * g_ref[...]).astype(BF16)

    return pl.pallas_call(
        body, name=name, grid=(t // tr,),
        in_specs=[pl.BlockSpec((tr, d), lambda i: (i, 0)), pl.BlockSpec((1, d), lambda i: (0, 0))],
        out_specs=pl.BlockSpec((tr, d), lambda i: (i, 0)),
        out_shape=jax.ShapeDtypeStruct((t, d), BF16),
        compiler_params=_cp("parallel"))(h, gain)


def _rms_bwd_math(hv, gain, dn):
    d = hv.shape[-1]
    r = lax.rsqrt(jnp.mean(hv * hv, axis=-1, keepdims=True) + EPS)
    xhat = hv * r
    dxh = dn * gain
    dh = r * (dxh - xhat * (jnp.sum(dxh * xhat, axis=-1, keepdims=True) / d))
    dgain = jnp.sum(dn * xhat, axis=0, keepdims=True)
    return dh, dgain


def _rms_bwd(name, h, gain, dn, dres):
    t, d = h.shape
    tr = _tile(t, 256, SUBLANES)

    def body(h_ref, g_ref, dn_ref, dr_ref, dh_ref, dhb_ref, dg_ref):
        dh, dgain = _rms_bwd_math(h_ref[...], g_ref[...], dn_ref[...].astype(F32))
        dh = dh + dr_ref[...]
        dh_ref[...] = dh
        dhb_ref[...] = dh.astype(BF16)

        @pl.when(pl.program_id(0) == 0)
        def _():
            dg_ref[...] = dgain

        @pl.when(pl.program_id(0) > 0)
        def _():
            dg_ref[...] += dgain

    row = pl.BlockSpec((tr, d), lambda i: (i, 0))
    vec = pl.BlockSpec((1, d), lambda i: (0, 0))
    return pl.pallas_call(
        body, name=name, grid=(t // tr,), in_specs=[row, vec, row, row],
        out_specs=[row, row, vec],
        out_shape=[jax.ShapeDtypeStruct((t, d), F32), jax.ShapeDtypeStruct((t, d), BF16),
                   jax.ShapeDtypeStruct((1, d), F32)],
        compiler_params=_cp("arbitrary"))(h, gain, dn, dres)


def _loss_head(name, h, gain, tgt):
    t, d = h.shape
    tr = _tile(t, 256, SUBLANES)

    def body(h_ref, g_ref, t_ref, dh_ref, dhb_ref, dg_ref, ls_ref):
        hv = h_ref[...]
        gv = g_ref[...]
        r = lax.rsqrt(jnp.mean(hv * hv, axis=-1, keepdims=True) + EPS)
        err = hv * r * gv - t_ref[...]
        lsum = 0.5 * jnp.sum(err * err, axis=0, keepdims=True) / d
        dh, dgain = _rms_bwd_math(hv, gv, err / d)
        dh_ref[...] = dh
        dhb_ref[...] = dh.astype(BF16)

        @pl.when(pl.program_id(0) == 0)
        def _():
            dg_ref[...] = dgain
            ls_ref[...] = lsum

        @pl.when(pl.program_id(0) > 0)
        def _():
            dg_ref[...] += dgain
            ls_ref[...] += lsum

    row = pl.BlockSpec((tr, d), lambda i: (i, 0))
    vec = pl.BlockSpec((1, d), lambda i: (0, 0))
    return pl.pallas_call(
        body, name=name, grid=(t // tr,), in_specs=[row, vec, row],
        out_specs=[row, row, vec, vec],
        out_shape=[jax.ShapeDtypeStruct((t, d), F32), jax.ShapeDtypeStruct((t, d), BF16),
                   jax.ShapeDtypeStruct((1, d), F32), jax.ShapeDtypeStruct((1, d), F32)],
        compiler_params=_cp("arbitrary"))(h, gain, tgt)


def _conv_geom(t, seq, c, k):
    halo = 32 if k - 1 > SUBLANES else SUBLANES
    assert k - 1 <= halo
    tm = min(256, seq // 2)
    tc = min(512, c)
    assert seq % tm == 0 and tm % halo == 0 and c % tc == 0 and t % seq == 0
    return halo, tm, tc, min(128, tm), min(LANES, tc)


def _pre(kind, a, b):
    if kind == "glu":
        return a * _sigmoid(b)
    if kind == "mul":
        return a * b
    return a


def _taps(k):
    return [(s % SUBLANES, s // SUBLANES, s) for s in range(k)]


def _conv_fwd(name, seq, c, w, x1, c1, x2=None, c2=0, pre=None, bias=None, post=None, cpost=0):
    t = x1.shape[0]
    k = w.shape[0]
    halo, tm, tc, sr, sl = _conv_geom(t, seq, c, k)
    nb, cps = tm // halo, seq // tm
    two = x2 is not None
    has_bias, has_post = bias is not None, post is not None

    def body(*refs):
        it = iter(refs)
        x1c, x1h = next(it), next(it)
        x2c, x2h = (next(it), next(it)) if two else (None, None)
        w_ref = next(it)
        b_ref = next(it) if has_bias else None
        p_ref = next(it) if has_post else None
        o_ref = next(it)
        y_ref = next(it) if has_post else None
        xs = next(it)
        first = (pl.program_id(1) % cps) == 0
        hv = _pre(pre, x1h[...].astype(F32), x2h[...].astype(F32) if two else None)
        xs[0:halo, :] = jnp.where(first, 0.0, hv)
        xs[halo:halo + tm, :] = _pre(pre, x1c[...].astype(F32), x2c[...].astype(F32) if two else None)
        for l0 in range(0, tc, sl):
            ls = slice(l0, l0 + sl)
            for r0 in range(0, tm, sr):
                win = xs[r0:r0 + sr + halo, ls]
                acc = jnp.zeros((sr, sl), F32)
                rolled = {}
                for r, q, s in _taps(k):
                    if r not in rolled:
                        rolled[r] = win if r == 0 else pltpu.roll(win, r, 0)
                    lo = halo - SUBLANES * q
                    acc = acc + w_ref[k - 1 - s:k - s, ls] * rolled[r][lo:lo + sr]
                if has_bias:
                    acc = acc + b_ref[:, ls]
                o_ref[r0:r0 + sr, ls] = acc.astype(o_ref.dtype)
                if has_post:
                    y_ref[r0:r0 + sr, ls] = (acc * p_ref[r0:r0 + sr, ls].astype(F32)).astype(y_ref.dtype)

    def cur(off):
        return pl.BlockSpec((tm, tc), lambda j, i: (i, off // tc + j))

    def prev(off):
        return pl.BlockSpec((halo, tc), lambda j, i: (jnp.maximum(i * nb - 1, 0), off // tc + j))

    ins, specs = [x1, x1], [cur(c1), prev(c1)]
    if two:
        ins += [x2, x2]
        specs += [cur(c2), prev(c2)]
    ins.append(w)
    specs.append(pl.BlockSpec((k, tc), lambda j, i: (0, j)))
    if has_bias:
        ins.append(bias)
        specs.append(pl.BlockSpec((1, tc), lambda j, i: (0, j)))
    if has_post:
        ins.append(post)
        specs.append(cur(cpost))
    o_spec = pl.BlockSpec((tm, tc), lambda j, i: (i, j))
    shp = jax.ShapeDtypeStruct((t, c), BF16)
    return pl.pallas_call(
        body, name=name, grid=(c // tc, t // tm), in_specs=specs,
        out_specs=[o_spec, o_spec] if has_post else o_spec,
        out_shape=[shp, shp] if has_post else shp,
        scratch_shapes=[pltpu.VMEM((halo + tm, tc), F32)],
        compiler_params=_cp("parallel", "parallel"))(*ins)


def _conv_bwd(name, seq, c, w, d1, cd1, d2=None, cd2=0, dpre=None,
              x1=None, c1=0, x2=None, c2=0, pre=None):
    t = d1.shape[0]
    k = w.shape[0]
    halo, tm, tc, sr, sl = _conv_geom(t, seq, c, k)
    nb, cps = tm // halo, seq // tm
    nchunks = t // tm
    dtwo, xtwo, has_x = d2 is not None, x2 is not None, x1 is not None

    def body(*refs):
        it = iter(refs)
        d1c, d1n = next(it), next(it)
        d2c, d2n = (next(it), next(it)) if dtwo else (None, None)
        x1c, x1h = (next(it), next(it)) if has_x else (None, None)
        x2c, x2h = (next(it), next(it)) if xtwo else (None, None)
        w_ref = next(it)
        dx_ref = next(it)
        dw_ref = next(it) if has_x else None
        ds = next(it)
        xs = next(it) if has_x else None
        i = pl.program_id(1)
        last = (i % cps) == cps - 1
        ds[0:tm, :] = _pre(dpre, d1c[...].astype(F32), d2c[...].astype(F32) if dtwo else None)
        nv = _pre(dpre, d1n[...].astype(F32), d2n[...].astype(F32) if dtwo else None)
        ds[tm:tm + halo, :] = jnp.where(last, 0.0, nv)
        if has_x:
            first = (i % cps) == 0
            hv = _pre(pre, x1h[...].astype(F32), x2h[...].astype(F32) if xtwo else None)
            xs[0:halo, :] = jnp.where(first, 0.0, hv)
            xs[halo:halo + tm, :] = _pre(pre, x1c[...].astype(F32), x2c[...].astype(F32) if xtwo else None)

            @pl.when(i == 0)
            def _():
                dw_ref[...] = jnp.zeros_like(dw_ref)

        for l0 in range(0, tc, sl):
            ls = slice(l0, l0 + sl)
            for r0 in range(0, tm, sr):
                win = ds[r0:r0 + sr + halo, ls]
                nrow = sr + halo
                acc = jnp.zeros((sr, sl), F32)
                rolled = {}
                for r, q, s in _taps(k):
                    if r not in rolled:
                        rolled[r] = win if r == 0 else pltpu.roll(win, nrow - r, 0)
                    lo = SUBLANES * q
                    acc = acc + w_ref[k - 1 - s:k - s, ls] * rolled[r][lo:lo + sr]
                dx_ref[r0:r0 + sr, ls] = acc.astype(dx_ref.dtype)
                if has_x:
                    dcur = win[0:sr]
                    xwin = xs[r0:r0 + sr + halo, ls]
                    xrolled = {}
                    for r, q, s in _taps(k):
                        if r not in xrolled:
                            xrolled[r] = xwin if r == 0 else pltpu.roll(xwin, r, 0)
                        lo = halo - SUBLANES * q
                        part = jnp.sum(dcur * xrolled[r][lo:lo + sr], axis=0, keepdims=True)
                        dw_ref[k - 1 - s:k - s, ls] += part

    def cur(off):
        return pl.BlockSpec((tm, tc), lambda j, i: (i, off // tc + j))

    def prev(off):
        return pl.BlockSpec((halo, tc), lambda j, i: (jnp.maximum(i * nb - 1, 0), off // tc + j))

    def nxt(off):
        return pl.BlockSpec((halo, tc),
                            lambda j, i: (jnp.minimum((i + 1) * nb, nchunks * nb - 1), off // tc + j))

    ins, specs = [d1, d1], [cur(cd1), nxt(cd1)]
    if dtwo:
        ins += [d2, d2]
        specs += [cur(cd2), nxt(cd2)]
    if has_x:
        ins += [x1, x1]
        specs += [cur(c1), prev(c1)]
    if xtwo:
        ins += [x2, x2]
        specs += [cur(c2), prev(c2)]
    ins.append(w)
    specs.append(pl.BlockSpec((k, tc), lambda j, i: (0, j)))
    o_specs = [pl.BlockSpec((tm, tc), lambda j, i: (i, j))]
    o_shapes = [jax.ShapeDtypeStruct((t, c), BF16)]
    scratch = [pltpu.VMEM((tm + halo, tc), F32)]
    if has_x:
        o_specs.append(pl.BlockSpec((k, tc), lambda j, i: (0, j)))
        o_shapes.append(jax.ShapeDtypeStruct((k, c), F32))
        scratch.append(pltpu.VMEM((halo + tm, tc), F32))
    out = pl.pallas_call(
        body, name=name, grid=(c // tc, t // tm), in_specs=specs, out_specs=o_specs,
        out_shape=o_shapes, scratch_shapes=scratch,
        compiler_params=_cp("parallel", "arbitrary"))(*ins)
    return out if has_x else out[0]


def _pool_taps(c):
    kmax = max(POOL_WINDOWS)
    grp = c // len(POOL_WINDOWS)
    cols = []
    for wdw in POOL_WINDOWS:
        col = jnp.concatenate([jnp.zeros((kmax - wdw,), F32), jnp.ones((wdw,), F32)])
        cols.append(jnp.tile(col[:, None], (1, grp)))
    return jnp.concatenate(cols, axis=1)


def _counts(i, tr, seq, grp):
    pos = (i * tr + lax.broadcasted_iota(jnp.int32, (tr, 1), 0)) % seq + 1
    return [1.0 / jnp.minimum(pos, wdw).astype(F32) for wdw in POOL_WINDOWS]


def _ln_stats(a2):
    mu = jnp.mean(a2, axis=-1, keepdims=True)
    xc = a2 - mu
    rstd = lax.rsqrt(jnp.mean(xc * xc, axis=-1, keepdims=True) + EPS)
    return xc * rstd, rstd


def _even_fwd(name, seq, a2, ws, u, ln_g, ln_b, w_pool, scale):
    t, c = a2.shape
    ng = len(POOL_WINDOWS)
    grp = c // ng
    tr = _tile(t, 256, SUBLANES)

    def body(a_ref, ws_ref, b_ref, g_ref, bb_ref, wp_ref, sc_ref, z_ref, pm_ref):
        xhat, _ = _ln_stats(a_ref[...].astype(F32))
        l = xhat * g_ref[...] + bb_ref[...]
        z_ref[:, 0:c] = (l * _sigmoid(l)).astype(BF16)
        inv = _counts(pl.program_id(0), tr, seq, grp)
        for g in range(ng):
            gs = slice(g * grp, (g + 1) * grp)
            pm = (ws_ref[:, gs].astype(F32) * inv[g] - b_ref[:, gs].astype(F32)).astype(BF16)
            pm_ref[:, gs] = pm
            q = jnp.dot(pm, wp_ref[g], preferred_element_type=F32)
            z_ref[:, c + g * grp:c + (g + 1) * grp] = (q * sc_ref[:, gs]).astype(BF16)

    row = pl.BlockSpec((tr, c), lambda i: (i, 0))
    vec = pl.BlockSpec((1, c), lambda i: (0, 0))
    return pl.pallas_call(
        body, name=name, grid=(t // tr,),
        in_specs=[row, row, pl.BlockSpec((tr, c), lambda i: (i, 2)), vec, vec,
                  pl.BlockSpec((ng, grp, grp), lambda i: (0, 0, 0)), vec],
        out_specs=[pl.BlockSpec((tr, 2 * c), lambda i: (i, 0)), row],
        out_shape=[jax.ShapeDtypeStruct((t, 2 * c), BF16), jax.ShapeDtypeStruct((t, c), BF16)],
        compiler_params=_cp("parallel"))(a2, ws, u, ln_g, ln_b, w_pool, scale)


def _even_bwd(name, seq, dz, a2, pm, ln_g, ln_b, w_pool, scale):
    t, c = a2.shape
    ng = len(POOL_WINDOWS)
    grp = c // ng
    tr = _tile(t, 256, SUBLANES)

    def body(dz_ref, a_ref, pm_ref, g_ref, bb_ref, wp_ref, sc_ref,
             da_ref, dws_ref, dpm_ref, vec_ref, dwp_ref):
        i = pl.program_id(0)

        @pl.when(i == 0)
        def _():
            vec_ref[...] = jnp.zeros_like(vec_ref)
            dwp_ref[...] = jnp.zeros_like(dwp_ref)

        xhat, rstd = _ln_stats(a_ref[...].astype(F32))
        gv = g_ref[...]
        l = xhat * gv + bb_ref[...]
        sg = _sigmoid(l)
        dl = dz_ref[:, 0:c].astype(F32) * (sg * (1.0 + l * (1.0 - sg)))
        dxh = dl * gv
        da2 = rstd * (dxh - jnp.mean(dxh, axis=-1, keepdims=True)
                      - xhat * jnp.mean(dxh * xhat, axis=-1, keepdims=True))
        da_ref[...] = da2.astype(BF16)
        vec_ref[0:1, :] += jnp.sum(dl * xhat, axis=0, keepdims=True)
        vec_ref[1:2, :] += jnp.sum(dl, axis=0, keepdims=True)
        vec_ref[2:3, :] += jnp.sum(da2, axis=0, keepdims=True)
        inv = _counts(i, tr, seq, grp)
        for g in range(ng):
            gs = slice(g * grp, (g + 1) * grp)
            pmv = pm_ref[:, gs]
            wp = wp_ref[g]
            dp = dz_ref[:, c + g * grp:c + (g + 1) * grp].astype(F32)
            q = jnp.dot(pmv, wp, preferred_element_type=F32)
            vec_ref[3:4, gs] += jnp.sum(dp * q, axis=0, keepdims=True)
            dq = (dp * sc_ref[:, gs]).astype(BF16)
            dpm = lax.dot_general(dq, wp, NT, preferred_element_type=F32)
            dwp_ref[g] += lax.dot_general(pmv, dq, TN, preferred_element_type=F32)
            dpm_ref[:, gs] = dpm.astype(BF16)
            dws_ref[:, gs] = (dpm * inv[g]).astype(BF16)

    row = pl.BlockSpec((tr, c), lambda i: (i, 0))
    vec = pl.BlockSpec((1, c), lambda i: (0, 0))
    rshape = jax.ShapeDtypeStruct((t, c), BF16)
    return pl.pallas_call(
        body, name=name, grid=(t // tr,),
        in_specs=[pl.BlockSpec((tr, 2 * c), lambda i: (i, 0)), row, row, vec, vec,
                  pl.BlockSpec((ng, grp, grp), lambda i: (0, 0, 0)), vec],
        out_specs=[row, row, row, pl.BlockSpec((SUBLANES, c), lambda i: (0, 0)),
                   pl.BlockSpec((ng, grp, grp), lambda i: (0, 0, 0))],
        out_shape=[rshape, rshape, rshape, jax.ShapeDtypeStruct((SUBLANES, c), F32),
                   jax.ShapeDtypeStruct((ng, grp, grp), F32)],
        compiler_params=_cp("arbitrary"))(dz, a2, pm, ln_g, ln_b, w_pool, scale)


def _even_du(name, u, da1, dbp, dpm):
    t, c = da1.shape
    tr = _tile(t, 256, SUBLANES)

    def body(u_ref, da_ref, dbp_ref, dpm_ref, du_ref):
        val = u_ref[:, 0:c].astype(F32)
        sg = _sigmoid(u_ref[:, c:2 * c].astype(F32))
        da = da_ref[...].astype(F32)
        du_ref[:, 0:c] = (da * sg).astype(BF16)
        du_ref[:, c:2 * c] = (da * val * sg * (1.0 - sg)).astype(BF16)
        du_ref[:, 2 * c:3 * c] = (dbp_ref[...].astype(F32) - dpm_ref[...].astype(F32)).astype(BF16)

    row = pl.BlockSpec((tr, c), lambda i: (i, 0))
    wide = pl.BlockSpec((tr, 3 * c), lambda i: (i, 0))
    return pl.pallas_call(
        body, name=name, grid=(t // tr,), in_specs=[wide, row, row, row], out_specs=wide,
        out_shape=jax.ShapeDtypeStruct((t, 3 * c), BF16),
        compiler_params=_cp("parallel"))(u, da1, dbp, dpm)


def _odd_du(name, u, dy, co, dxc):
    t, c = dy.shape
    tr = _tile(t, 256, SUBLANES)

    def body(u_ref, dy_ref, co_ref, dx_ref, du_ref):
        dx = dx_ref[...].astype(F32)
        du_ref[:, 0:c] = (dy_ref[...].astype(F32) * co_ref[...].astype(F32)).astype(BF16)
        du_ref[:, c:2 * c] = (dx * u_ref[:, 2 * c:3 * c].astype(F32)).astype(BF16)
        du_ref[:, 2 * c:3 * c] = (dx * u_ref[:, c:2 * c].astype(F32)).astype(BF16)

    row = pl.BlockSpec((tr, c), lambda i: (i, 0))
    wide = pl.BlockSpec((tr, 3 * c), lambda i: (i, 0))
    return pl.pallas_call(
        body, name=name, grid=(t // tr,), in_specs=[wide, row, row, row], out_specs=wide,
        out_shape=jax.ShapeDtypeStruct((t, 3 * c), BF16),
        compiler_params=_cp("parallel"))(u, dy, co, dxc)


def _local_step(x, tgt, seq, small, get_w, put_g, sync):
    t, d = x.shape
    c = d // 2
    cw_e, cw_o = small["conv_w_e"], small["conv_w_o"]
    wp = small["w_pool_e"].astype(BF16)
    ptaps = _pool_taps(c)
    row = lambda v: v.reshape(1, -1)

    we = get_w("mix_e", x)
    n0 = _rms_fwd("rms_fwd_mix0", x, row(small["mix_norm_e"]))
    u0 = _mm_nn("mm_in_e", n0, we["w_in"], BF16)
    a2 = _conv_fwd("conv_e_fwd", seq, c, cw_e, u0, 0, u0, c, "glu", bias=row(small["conv_b_e"]))
    ws = _conv_fwd("pool_fwd", seq, c, ptaps, u0, 2 * c)
    z0, pm = _even_fwd("even_fwd", seq, a2, ws, u0, row(small["ln_g_e"]), row(small["ln_b_e"]),
                       wp, row(small["pool_scale_e"]))
    h1 = _mm_nn("mm_out_e", z0, we["w_out"], F32, res=x, dep=sync("fwd_mix_e", z0))
    wf0 = get_w("ffn0", h1)
    n1 = _rms_fwd("rms_fwd_ffn0", h1, row(small["ffn_norm"][0]))
    g0, up0, act0 = _ffn_fwd("ffn0_fwd", n1, wf0["w_gate"], wf0["w_up"])
    h2 = _mm_down("mm_down0", act0, wf0["w_down"], h1, dep=sync("fwd_ffn0", act0))
    wo = get_w("mix_o", h2)
    n2 = _rms_fwd("rms_fwd_mix1", h2, row(small["mix_norm_o"]))
    u1 = _mm_nn("mm_in_o", n2, wo["w_in"], BF16)
    co, y1 = _conv_fwd("conv_o_fwd", seq, d, cw_o, u1, d, u1, 2 * d, "mul", post=u1, cpost=0)
    h3 = _mm_nn("mm_out_o", y1, wo["w_out"], F32, res=h2, dep=sync("fwd_mix_o", u1))
    wf1 = get_w("ffn1", h3)
    n3 = _rms_fwd("rms_fwd_ffn1", h3, row(small["ffn_norm"][1]))
    g1, up1, act1 = _ffn_fwd("ffn1_fwd", n3, wf1["w_gate"], wf1["w_up"])
    h4 = _mm_down("mm_down1", act1, wf1["w_down"], h3)

    dh4, dh4b, d_final, lsum = _loss_head("loss_head", h4, row(small["final_norm"]), tgt)

    def ffn_bwd(tag, dh, dhb, h_in, gain, n, g, up, act, w, dep):
        dg, dup = _ffn_bwd_act("ffn%s_bwd_act" % tag, dhb, w["w_down"], g, up, dep=dep)
        dwd = _mm_dwd("mm_dwd%s" % tag, act, dhb)
        dwg = _mm_dwg("mm_dwg%s" % tag, n, dg, dep=sync("bwd_ffn" + tag, dwd))
        dwu = _mm_dwg("mm_dwu%s" % tag, n, dup)
        dn = _mm_ffn_dn("mm_ffn_dn%s" % tag, dg, w["w_gate"], dup, w["w_up"])
        dh_in, dhb_in, dgain = _rms_bwd("rms_bwd_ffn%s" % tag, h_in, gain, dn, dh)
        dep = put_g("ffn" + tag, {"w_gate": dwg, "w_up": dwu, "w_down": dwd})
        return dh_in, dhb_in, dgain, dep

    dh3, dh3b, d_ffn1, dep = ffn_bwd("1", dh4, dh4b, h3, row(small["ffn_norm"][1]), n3, g1, up1,
                                     act1, wf1, None)
    dw_out_o = _mm_tn("mm_dw_out_o", y1, dh3b, BF16, dep=dep)
    dy1 = _mm_nt("mm_dy_o", dh3b, wo["w_out"], BF16, dep=sync("bwd_mix_o", dw_out_o))
    dxc, dcw_o = _conv_bwd("conv_o_bwd", seq, d, cw_o, dy1, 0, u1, 0, "mul",
                           x1=u1, c1=d, x2=u1, c2=2 * d, pre="mul")
    du1 = _odd_du("odd_du", u1, dy1, co, dxc)
    dw_in_o = _mm_tn("mm_dw_in_o", n2, du1, BF16)
    dn2 = _mm_nt("mm_dn_o", du1, wo["w_in"], BF16)
    dh2, dh2b, d_mix_o = _rms_bwd("rms_bwd_mix1", h2, row(small["mix_norm_o"]), dn2, dh3)
    dep = put_g("mix_o", {"w_in": dw_in_o, "w_out": dw_out_o})

    dh1, dh1b, d_ffn0, dep = ffn_bwd("0", dh2, dh2b, h1, row(small["ffn_norm"][0]), n1, g0, up0,
                                     act0, wf0, dep)
    dw_out_e = _mm_tn("mm_dw_out_e", z0, dh1b, BF16, dep=dep)
    dz0 = _mm_nt("mm_dz_e", dh1b, we["w_out"], BF16, dep=sync("bwd_mix_e", dw_out_e))
    da2, dws, dpm, vecs, dwp = _even_bwd("even_bwd", seq, dz0, a2, pm, row(small["ln_g_e"]),
                                         row(small["ln_b_e"]), wp, row(small["pool_scale_e"]))
    da1, dcw_e = _conv_bwd("conv_e_bwd", seq, c, cw_e, da2, 0, x1=u0, c1=0, x2=u0, c2=c, pre="glu")
    dbp = _conv_bwd("pool_bwd", seq, c, ptaps, dws, 0)
    du0 = _even_du("even_du", u0, da1, dbp, dpm)
    dw_in_e = _mm_tn("mm_dw_in_e", n0, du0, BF16)
    dep = put_g("mix_e", {"w_in": dw_in_e, "w_out": dw_out_e})
    dn0 = _mm_nt("mm_dn_e", du0, we["w_in"], BF16, dep=dep)
    dx, _, d_mix_e = _rms_bwd("rms_bwd_mix0", x, row(small["mix_norm_e"]), dn0, dh1)

    gsmall = {"mix_norm_e": d_mix_e[0], "conv_w_e": dcw_e, "conv_b_e": vecs[2], "ln_g_e": vecs[0],
              "ln_b_e": vecs[1], "w_pool_e": dwp, "pool_scale_e": vecs[3], "mix_norm_o": d_mix_o[0],
              "conv_w_o": dcw_o, "ffn_norm": jnp.concatenate([d_ffn0, d_ffn1], axis=0),
              "final_norm": d_final[0]}
    return lsum, dx, gsmall


def _place():
    x, y, c = (lax.axis_index(a) for a in MESH_AXES)
    return x, y, c


def _index(p):
    return 4 * p[0] + 2 * p[1] + p[2]


def _slab(ref, kind, d, n):
    if kind == "blk":
        return ref.at[d]
    return ref.at[:, pl.ds(pl.multiple_of(d * n, LANES), n)]


def _all_gather(name, shards, kinds):
    na = len(shards)

    def body(*refs):
        x_refs, o_refs = refs[:na], refs[na:2 * na]
        send_sems, recv_sems, local_sems = refs[2 * na:]
        x, y, c = _place()
        me, sib = (x, y, c), (x, y, 1 - c)
        chips = [(1 - x, y), (x, 1 - y), (1 - x, 1 - y)]

        def slot(a, p):
            return _slab(o_refs[a], kinds[a], _index(p), shards[a].shape[1])

        def copy(a, k, block, to, src=None):
            return pltpu.make_async_remote_copy(
                src_ref=slot(a, block) if src is None else src, dst_ref=slot(a, block),
                send_sem=send_sems.at[a, k], recv_sem=recv_sems.at[a, k],
                device_id=to, device_id_type=MESH)

        mine = [pltpu.make_async_copy(x_refs[a], slot(a, me), local_sems.at[a]) for a in range(na)]
        for cp in mine:
            cp.start()
        first = []
        for a in range(na):
            first.append(copy(a, 0, me, sib, src=x_refs[a]))
            first += [copy(a, 1 + j, me, (*chip, c), src=x_refs[a]) for j, chip in enumerate(chips)]
        for cp in first:
            cp.start()
        passed = []
        for j, chip in enumerate(chips):
            for a in range(na):
                copy(a, 1 + j, (*chip, c), me).wait_recv()
                fwd = copy(a, 4 + j, (*chip, c), sib)
                fwd.start()
                passed.append(fwd)
        for a in range(na):
            copy(a, 0, sib, me).wait_recv()
            for j, chip in enumerate(chips):
                copy(a, 4 + j, (*chip, 1 - c), me).wait_recv()
        for cp in first + passed:
            cp.wait_send()
        for cp in mine:
            cp.wait()

    shapes = []
    for s, kind in zip(shards, kinds):
        m, n = s.shape
        shapes.append(jax.ShapeDtypeStruct((NDEV, m, n) if kind == "blk" else (m, NDEV * n), s.dtype))
    return pl.pallas_call(
        body, name=name, in_specs=[ANY] * na, out_specs=[ANY] * na, out_shape=shapes,
        scratch_shapes=[pltpu.SemaphoreType.DMA((na, 7)), pltpu.SemaphoreType.DMA((na, 7)),
                        pltpu.SemaphoreType.DMA((na,))])(*shards)


HBM = pl.BlockSpec(memory_space=pltpu.HBM)
SEM = pl.BlockSpec(memory_space=pltpu.SEMAPHORE)
EFFECT = pltpu.SideEffectType.DATAFLOW_SIDE_EFFECTING
NCHIPS = 4


def _in_hbm(a):
    return pltpu.with_memory_space_constraint(a, pltpu.HBM)


def _gathered_shape(s, kind):
    m, n = s.shape
    return (NDEV, m, n) if kind == "blk" else (m, NDEV * n)


def _first_targets():
    x, y, c = _place()
    return [(x, y, 1 - c), (1 - x, y, c), (x, 1 - y, c), (1 - x, 1 - y, c)]


def _gather_start(name, shards, kinds, after):
    na = len(shards)

    def body(*refs):
        x_refs, land_refs = refs[:na], refs[na:2 * na]
        send_sems, recv_sems = refs[2 * na + 1], refs[2 * na + 2]
        token = refs[-1]
        me = _index(_place())
        for a in range(na):
            for k, to in enumerate(_first_targets()):
                pltpu.make_async_remote_copy(
                    src_ref=x_refs[a], dst_ref=_slab(land_refs[a], kinds[a], me, shards[a].shape[1]),
                    send_sem=send_sems.at[4 * a + k], recv_sem=recv_sems.at[4 * a + k],
                    device_id=to, device_id_type=MESH).start()
        token[...] = jnp.zeros_like(token)

    lands = [lax.empty(_gathered_shape(s, k), s.dtype) for s, k in zip(shards, kinds)]
    outs = pl.pallas_call(
        body, name=name,
        out_shape=(pltpu.SemaphoreType.DMA((4 * na,)), pltpu.SemaphoreType.DMA((4 * na,)),
                   *[pltpu.HBM(s.shape, s.dtype) for s in shards],
                   *[pltpu.HBM(l.shape, l.dtype) for l in lands],
                   jax.ShapeDtypeStruct((SUBLANES, LANES), F32)),
        in_specs=[HBM] * (2 * na) + [ANY],
        out_specs=(SEM, SEM, *[HBM] * (2 * na), pl.BlockSpec(memory_space=pltpu.VMEM)),
        input_output_aliases={i: 2 + i for i in range(2 * na)},
        compiler_params=pltpu.CompilerParams(has_side_effects=EFFECT),
    )(*[_in_hbm(s) for s in shards], *[_in_hbm(l) for l in lands], after)
    return outs[0], outs[1], outs[2:2 + na], outs[2 + na:2 + 2 * na], outs[-1]


def _gather_wait(name, started, kinds, after):
    send_sems, recv_sems, shards, lands, _ = started
    na = len(shards)

    def body(*refs):
        x_refs, land_refs = refs[:na], refs[na:2 * na]
        s_sems, r_sems = refs[2 * na], refs[2 * na + 1]
        for a in range(na):
            for k, frm in enumerate(_first_targets()):
                cp = pltpu.make_async_remote_copy(
                    src_ref=x_refs[a],
                    dst_ref=_slab(land_refs[a], kinds[a], _index(frm), shards[a].shape[1]),
                    send_sem=s_sems.at[4 * a + k], recv_sem=r_sems.at[4 * a + k],
                    device_id=frm, device_id_type=MESH)
                cp.wait_send()
                cp.wait_recv()

    outs = pl.pallas_call(
        body, name=name,
        out_shape=(*[pltpu.HBM(s.shape, s.dtype) for s in shards],
                   *[pltpu.HBM(l.shape, l.dtype) for l in lands]),
        in_specs=[HBM] * (2 * na) + [SEM, SEM, ANY], out_specs=[HBM] * (2 * na),
        input_output_aliases={i: i for i in range(2 * na)},
        compiler_params=pltpu.CompilerParams(has_side_effects=EFFECT),
    )(*shards, *lands, send_sems, recv_sems, after)
    return outs[:na], outs[na:]


def _split_start(name, bufs, ncopies, plan, after):
    nb = len(bufs)

    def body(*refs):
        send_sems, recv_sems, token = refs[nb + 1], refs[nb + 2], refs[-1]
        for k, (src, dst, to, _) in enumerate(plan(refs[:nb])):
            pltpu.make_async_remote_copy(src_ref=src, dst_ref=dst, send_sem=send_sems.at[k],
                                         recv_sem=recv_sems.at[k], device_id=to, device_id_type=MESH).start()
        token[...] = jnp.zeros_like(token)

    outs = pl.pallas_call(
        body, name=name,
        out_shape=(pltpu.SemaphoreType.DMA((ncopies,)), pltpu.SemaphoreType.DMA((ncopies,)),
                   *[pltpu.HBM(b.shape, b.dtype) for b in bufs],
                   jax.ShapeDtypeStruct((SUBLANES, LANES), F32)),
        in_specs=[HBM] * nb + [ANY],
        out_specs=(SEM, SEM, *[HBM] * nb, pl.BlockSpec(memory_space=pltpu.VMEM)),
        input_output_aliases={i: 2 + i for i in range(nb)},
        compiler_params=pltpu.CompilerParams(has_side_effects=EFFECT),
    )(*[_in_hbm(b) for b in bufs], after)
    return outs[0], outs[1], list(outs[2:2 + nb]), outs[-1]


def _split_wait(name, started, plan, after):
    send_sems, recv_sems, bufs, _ = started
    nb = len(bufs)

    def body(*refs):
        s_sems, r_sems = refs[nb], refs[nb + 1]
        for k, (src, _, to, landed) in enumerate(plan(refs[:nb])):
            cp = pltpu.make_async_remote_copy(src_ref=src, dst_ref=landed, send_sem=s_sems.at[k],
                                              recv_sem=r_sems.at[k], device_id=to, device_id_type=MESH)
            cp.wait_send()
            cp.wait_recv()

    outs = pl.pallas_call(
        body, name=name, out_shape=tuple(pltpu.HBM(b.shape, b.dtype) for b in bufs),
        in_specs=[HBM] * nb + [SEM, SEM, ANY], out_specs=[HBM] * nb,
        input_output_aliases={i: i for i in range(nb)},
        compiler_params=pltpu.CompilerParams(has_side_effects=EFFECT),
    )(*bufs, send_sems, recv_sems, after)
    return list(outs)


def _forward_plan(kinds, nloc):
    def plan(lands):
        x, y, c = _place()
        out = []
        for a, land in enumerate(lands):
            for chip in [(1 - x, y), (x, 1 - y), (1 - x, 1 - y)]:
                mine = _slab(land, kinds[a], _index((*chip, c)), nloc[a])
                out.append((mine, mine, (x, y, 1 - c), _slab(land, kinds[a], _index((*chip, 1 - c)), nloc[a])))
        return out
    return plan


def _own_copy(name, shards, lands, kinds):
    na = len(shards)

    def body(*refs):
        x_refs, land_refs, sems = refs[:na], refs[2 * na:3 * na], refs[3 * na]
        me = _index(_place())
        mine = [pltpu.make_async_copy(x_refs[a], _slab(land_refs[a], kinds[a], me, shards[a].shape[1]),
                                      sems.at[a]) for a in range(na)]
        for cp in mine:
            cp.start()
        for cp in mine:
            cp.wait()

    return pl.pallas_call(
        body, name=name, in_specs=[ANY] * (2 * na), out_specs=[ANY] * na,
        out_shape=[jax.ShapeDtypeStruct(l.shape, l.dtype) for l in lands],
        input_output_aliases={na + i: i for i in range(na)},
        scratch_shapes=[pltpu.SemaphoreType.DMA((na,))])(*shards, *lands)


def _pair_plan(kinds, nloc):
    na = len(kinds)

    def plan(refs):
        x, y, c = _place()
        out = []
        for a in range(na):
            for j in range(NCHIPS):
                dst = refs[na + a].at[j]
                out.append((_slab(refs[a], kinds[a], 2 * j + (1 - c), nloc[a]), dst, (x, y, 1 - c), dst))
        return out
    return plan


def _chip_sum(name, full, kind, n, from_sib, place):
    _, m, _ = from_sib.shape
    tr = _tile(m, max(SUBLANES, 262144 // n), SUBLANES)

    def body(s_ref, mine_ref, sib_ref, csum_ref, land_ref):
        v = (mine_ref[...].astype(F32) + sib_ref[...].astype(F32)).astype(csum_ref.dtype)
        csum_ref[...] = v

        @pl.when(pl.program_id(1) == s_ref[1])
        def _():
            land_ref[...] = v

    if kind == "blk":
        mine_spec = pl.BlockSpec((None, tr, n), lambda i, j, s: (2 * j + s[0], i, 0))
    else:
        mine_spec = pl.BlockSpec((tr, n), lambda i, j, s: (i, 2 * j + s[0]))
    slot = pl.BlockSpec((None, tr, n), lambda i, j, s: (j, i, 0))
    shp = jax.ShapeDtypeStruct((NCHIPS, m, n), from_sib.dtype)
    return pl.pallas_call(
        body, name=name,
        grid_spec=pltpu.PrefetchScalarGridSpec(
            num_scalar_prefetch=1, grid=(m // tr, NCHIPS), in_specs=[mine_spec, slot],
            out_specs=[slot, pl.BlockSpec((None, tr, n), lambda i, j, s: (s[1], i, 0))]),
        out_shape=[shp, shp], compiler_params=_cp("parallel", "arbitrary"))(place, full, from_sib)


def _other_chips():
    x, y, c = _place()
    return [(1 - x, y, c), (x, 1 - y, c), (1 - x, 1 - y, c)]


def _scatter_start(name, csums, lands, after):
    na = len(csums)

    def body(*refs):
        c_refs, land_refs = refs[:na], refs[na:2 * na]
        send_sems, recv_sems = refs[2 * na + 1], refs[2 * na + 2]
        token = refs[-1]
        x, y, _ = _place()
        for a in range(na):
            for k, to in enumerate(_other_chips()):
                pltpu.make_async_remote_copy(
                    src_ref=c_refs[a].at[2 * to[0] + to[1]], dst_ref=land_refs[a].at[2 * x + y],
                    send_sem=send_sems.at[3 * a + k], recv_sem=recv_sems.at[3 * a + k],
                    device_id=to, device_id_type=MESH).start()
        token[...] = jnp.zeros_like(token)

    outs = pl.pallas_call(
        body, name=name,
        out_shape=(pltpu.SemaphoreType.DMA((3 * na,)), pltpu.SemaphoreType.DMA((3 * na,)),
                   *[pltpu.HBM(s.shape, s.dtype) for s in csums],
                   *[pltpu.HBM(l.shape, l.dtype) for l in lands],
                   jax.ShapeDtypeStruct((SUBLANES, LANES), F32)),
        in_specs=[HBM] * (2 * na) + [ANY],
        out_specs=(SEM, SEM, *[HBM] * (2 * na), pl.BlockSpec(memory_space=pltpu.VMEM)),
        input_output_aliases={i: 2 + i for i in range(2 * na)},
        compiler_params=pltpu.CompilerParams(has_side_effects=EFFECT),
    )(*[_in_hbm(s) for s in csums], *[_in_hbm(l) for l in lands], after)
    return outs[0], outs[1], outs[2:2 + na], outs[2 + na:2 + 2 * na], outs[-1]


def _scatter_wait(name, started, after):
    send_sems, recv_sems, csums, lands, _ = started
    na = len(csums)

    def body(*refs):
        c_refs, land_refs = refs[:na], refs[na:2 * na]
        s_sems, r_sems = refs[2 * na], refs[2 * na + 1]
        for a in range(na):
            for k, frm in enumerate(_other_chips()):
                cp = pltpu.make_async_remote_copy(
                    src_ref=c_refs[a].at[2 * frm[0] + frm[1]], dst_ref=land_refs[a].at[2 * frm[0] + frm[1]],
                    send_sem=s_sems.at[3 * a + k], recv_sem=r_sems.at[3 * a + k],
                    device_id=frm, device_id_type=MESH)
                cp.wait_send()
                cp.wait_recv()

    outs = pl.pallas_call(
        body, name=name,
        out_shape=(*[pltpu.HBM(s.shape, s.dtype) for s in csums],
                   *[pltpu.HBM(l.shape, l.dtype) for l in lands]),
        in_specs=[HBM] * (2 * na) + [SEM, SEM, ANY], out_specs=[HBM] * (2 * na),
        input_output_aliases={i: i for i in range(2 * na)},
        compiler_params=pltpu.CompilerParams(has_side_effects=EFFECT),
    )(*csums, *lands, send_sems, recv_sems, after)
    return outs[na:]


def _adam_math(w, g, m, v):
    m = ADAM_B1 * m + (1.0 - ADAM_B1) * g
    v = ADAM_B2 * v + (1.0 - ADAM_B2) * (g * g)
    m_hat = m / (1.0 - ADAM_B1 ** ADAM_STEP)
    v_hat = v / (1.0 - ADAM_B2 ** ADAM_STEP)
    delta = -ADAM_LR * (m_hat / (jnp.sqrt(v_hat) + ADAM_EPS) + ADAM_WD * w)
    return delta, m, v


def _sum_adamw(name, parts, w, m, v, layer, prev=None):
    nl, r, c = w.shape
    nparts = parts.shape[0]
    tr = _tile(r, max(SUBLANES, 262144 // c), SUBLANES)

    def body(p_ref, w_ref, m_ref, v_ref, *rest):
        g_ref, d_ref, mo_ref, vo_ref = rest[-4:]
        g = p_ref[0].astype(F32)
        for s in range(1, nparts):
            g = g + p_ref[s].astype(F32)
        delta, mn, vn = _adam_math(w_ref[...], g, m_ref[...], v_ref[...])
        g_ref[...] = g
        d_ref[...] = delta
        mo_ref[...] = mn
        vo_ref[...] = vn

    row = pl.BlockSpec((None, tr, c), lambda i: (layer, i, 0))
    shp = jax.ShapeDtypeStruct((nl, r, c), F32)
    kept = [] if prev is None else list(prev)
    return pl.pallas_call(
        body, name=name, grid=(r // tr,),
        in_specs=[pl.BlockSpec((nparts, tr, c), lambda i: (0, i, 0)), row, row, row] + [ANY] * len(kept),
        out_specs=[row, row, row, row], out_shape=[shp, shp, shp, shp],
        input_output_aliases={4 + i: i for i in range(len(kept))},
        compiler_params=_cp("parallel"))(parts, w, m, v, *kept)


def _sum_parts(name, parts):
    _, r, c = parts.shape

    def body(p_ref, o_ref):
        g = p_ref[0]
        for s in range(1, NDEV):
            g = g + p_ref[s]
        o_ref[...] = g

    return pl.pallas_call(
        body, name=name, grid=(1,),
        in_specs=[pl.BlockSpec((NDEV, r, c), lambda i: (0, 0, 0))],
        out_specs=pl.BlockSpec((r, c), lambda i: (0, 0)),
        out_shape=jax.ShapeDtypeStruct((r, c), F32), compiler_params=_cp("arbitrary"))(parts)


def _adamw(name, w, g, m, v):
    r, c = w.shape

    def body(w_ref, g_ref, m_ref, v_ref, d_ref, mo_ref, vo_ref):
        delta, mn, vn = _adam_math(w_ref[...], g_ref[...], m_ref[...], v_ref[...])
        d_ref[...] = delta
        mo_ref[...] = mn
        vo_ref[...] = vn

    full = pl.BlockSpec((r, c), lambda i: (0, 0))
    shp = jax.ShapeDtypeStruct((r, c), F32)
    return pl.pallas_call(
        body, name=name, grid=(1,), in_specs=[full] * 4, out_specs=[full] * 3,
        out_shape=[shp] * 3, compiler_params=_cp("arbitrary"))(w, g, m, v)


def _pack(arrays):
    flat = jnp.concatenate([a.reshape(-1) for a in arrays])
    unit = SUBLANES * LANES
    pad = (-flat.shape[0]) % unit
    return jnp.pad(flat, (0, pad)).reshape(-1, LANES)


def _unpack(buf, shapes):
    flat = buf.reshape(-1)
    out, off = [], 0
    for shp in shapes:
        size = 1
        for s in shp:
            size *= s
        out.append(flat[off:off + size].reshape(shp))
        off += size
    return out


WEIGHTS = ["mix_norm_e", "w_in_e", "conv_w_e", "conv_b_e", "ln_g_e", "ln_b_e", "w_pool_e",
           "pool_scale_e", "w_out_e", "mix_norm_o", "w_in_o", "conv_w_o", "w_out_o", "ffn_norm",
           "w_gate", "w_up", "w_down", "final_norm"]
BIG = ["w_in_e", "w_out_e", "w_in_o", "w_out_o", "w_gate", "w_up", "w_down"]
SHARDED_SMALL = {"conv_w_e": 1, "w_pool_e": 1, "mix_norm_o": 0, "conv_w_o": 1}
SMALL = [n for n in WEIGHTS if n not in BIG]


def kernel(x, mix_norm_e, w_in_e, conv_w_e, conv_b_e, ln_g_e, ln_b_e, w_pool_e, pool_scale_e, w_out_e, mix_norm_o, w_in_o, conv_w_o, w_out_o, ffn_norm, w_gate, w_up, w_down, final_norm, loss_target, m_mix_norm_e, m_w_in_e, m_conv_w_e, m_conv_b_e, m_ln_g_e, m_ln_b_e, m_w_pool_e, m_pool_scale_e, m_w_out_e, m_mix_norm_o, m_w_in_o, m_conv_w_o, m_w_out_o, m_ffn_norm, m_w_gate, m_w_up, m_w_down, m_final_norm, v_mix_norm_e, v_w_in_e, v_conv_w_e, v_conv_b_e, v_ln_g_e, v_ln_b_e, v_w_pool_e, v_pool_scale_e, v_w_out_e, v_mix_norm_o, v_w_in_o, v_conv_w_o, v_w_out_o, v_ffn_norm, v_w_gate, v_w_up, v_w_down, v_final_norm):
    wts = dict(zip(WEIGHTS, [mix_norm_e, w_in_e, conv_w_e, conv_b_e, ln_g_e, ln_b_e, w_pool_e, pool_scale_e, w_out_e, mix_norm_o, w_in_o, conv_w_o, w_out_o, ffn_norm, w_gate, w_up, w_down, final_norm]))
    mom = dict(zip(WEIGHTS, [m_mix_norm_e, m_w_in_e, m_conv_w_e, m_conv_b_e, m_ln_g_e, m_ln_b_e, m_w_pool_e, m_pool_scale_e, m_w_out_e, m_mix_norm_o, m_w_in_o, m_conv_w_o, m_w_out_o, m_ffn_norm, m_w_gate, m_w_up, m_w_down, m_final_norm]))
    var = dict(zip(WEIGHTS, [v_mix_norm_e, v_w_in_e, v_conv_w_e, v_conv_b_e, v_ln_g_e, v_ln_b_e, v_w_pool_e, v_pool_scale_e, v_w_out_e, v_mix_norm_o, v_w_in_o, v_conv_w_o, v_w_out_o, v_ffn_norm, v_w_gate, v_w_up, v_w_down, v_final_norm]))
    bsz, seq, d = x.shape
    t = bsz * seq
    me = _index(_place())

    bf = lambda a: a.astype(BF16)
    mix_kinds, ffn_kinds = ["col", "blk"], ["blk", "blk", "blk"]
    groups = {
        "mix_e": ([bf(w_in_e[0]), bf(w_out_e[0])], mix_kinds, [("w_in_e", 0), ("w_out_e", 0)]),
        "ffn0": ([bf(w_gate[0]), bf(w_up[0]), bf(w_down[0])], ffn_kinds,
                 [("w_gate", 0), ("w_up", 0), ("w_down", 0)]),
        "mix_o": ([bf(w_in_o[0]), bf(w_out_o[0])], mix_kinds, [("w_in_o", 0), ("w_out_o", 0)]),
        "ffn1": ([bf(w_gate[1]), bf(w_up[1]), bf(w_down[1])], ffn_kinds,
                 [("w_gate", 1), ("w_up", 1), ("w_down", 1)]),
    }
    started, prev = {}, x
    for grp, (shards, kinds, _) in groups.items():
        started[grp] = _gather_start("gather_start_" + grp, shards, kinds, prev)
        prev = started[grp][4]
    all_started = prev[0, 0:1]

    fwd_order = list(groups)
    passing, shards_of = {}, {}

    def pass_on(grp, after):
        shards, kinds, _ = groups[grp]
        shards_of[grp], lands = _gather_wait("gather_wait_" + grp, started[grp], kinds, after)
        plan = _forward_plan(kinds, [s.shape[1] for s in shards])
        passing[grp] = (_split_start("forward_start_" + grp, lands, 3 * len(lands), plan, after), plan)
        return passing[grp][0][3]

    def get_w(grp, after):
        if grp not in passing:
            after = pass_on(grp, after)
        st, plan = passing[grp]
        lands = _split_wait("forward_wait_" + grp, st, plan, after)
        full = _own_copy("own_copy_" + grp, shards_of[grp], lands, groups[grp][1])
        if len(full) == 2:
            return {"w_in": full[0], "w_out": full[1].reshape(-1, d)}
        return {"w_gate": full[0], "w_up": full[1], "w_down": full[2]}

    cx, cy, cc = _place()
    place = jnp.stack([cc, 2 * cx + cy]).astype(jnp.int32)
    bwd_order = ["ffn1", "mix_o", "ffn0", "mix_e"]
    pairing, pending, results = {}, {}, {}

    def put_g(grp, grads):
        shards, kinds, _ = groups[grp]
        nloc = [s.shape[1] for s in shards]
        if len(shards) == 2:
            fulls = [grads["w_in"], grads["w_out"].reshape(NDEV, -1, d)]
        else:
            fulls = [grads["w_gate"], grads["w_up"], grads["w_down"]]
        empties = []
        for g, kind, n in zip(fulls, kinds, nloc):
            empties.append(lax.empty((NCHIPS, g.shape[1] if kind == "blk" else g.shape[0], n), g.dtype))
        plan = _pair_plan(kinds, nloc)
        pairing[grp] = (_split_start("pair_start_" + grp, fulls + empties, NCHIPS * len(fulls), plan, fulls[0]),
                        plan, kinds, nloc)
        token = pairing[grp][0][3]
        return send_sums(grp, token) if grp == bwd_order[-1] else token

    def send_sums(grp, after):
        st, plan, kinds, nloc = pairing[grp]
        bufs = _split_wait("pair_wait_" + grp, st, plan, after)
        na = len(kinds)
        sums = [_chip_sum("chip_sum_%s%d" % (grp, a), bufs[a], kinds[a], nloc[a], bufs[na + a], place)
                for a in range(na)]
        pending[grp] = _scatter_start("scatter_start_" + grp, [s[0] for s in sums], [s[1] for s in sums], after)
        return pending[grp][4]

    def finish(grp, after):
        lands = _scatter_wait("scatter_wait_" + grp, pending[grp], after)
        for (n, l), parts in zip(groups[grp][2], lands):
            results[n] = _sum_adamw("adamw_%s%d" % (n, l), parts, wts[n], mom[n], var[n], l, results.get(n))
        return results[groups[grp][2][-1][0]][1]

    def sync(tag, after):
        if tag == "fwd_mix_e":
            return pass_on("ffn0", after)
        if tag == "fwd_ffn0":
            return pass_on("mix_o", after)
        if tag == "fwd_mix_o":
            return pass_on("ffn1", after)
        if tag == "bwd_mix_o":
            return send_sums("ffn1", after)
        if tag == "bwd_ffn0":
            return finish("ffn1", send_sums("mix_o", after))
        if tag == "bwd_mix_e":
            return finish("mix_o", send_sums("ffn0", after))
        return None

    sh_names = list(SHARDED_SMALL)
    sh_local = [wts[n][0] for n in sh_names]
    packed = _pack(sh_local)
    gathered, = _all_gather("gather_small", [packed], ["blk"])
    small = {n: wts[n][0] for n in SMALL if n not in SHARDED_SMALL and n not in ("ffn_norm", "final_norm")}
    small["ffn_norm"], small["final_norm"] = ffn_norm, final_norm
    per_dev = [_unpack(gathered[s], [a.shape for a in sh_local]) for s in range(NDEV)]
    for i, n in enumerate(sh_names):
        small[n] = jnp.concatenate([per_dev[s][i] for s in range(NDEV)], axis=SHARDED_SMALL[n])

    small["mix_norm_e"] = small["mix_norm_e"] + all_started
    lsum, dx, gsmall = _local_step(x.reshape(t, d), loss_target.reshape(t, d), seq, small, get_w, put_g, sync)
    loss = lax.psum(jnp.sum(lsum), MESH_AXES)
    finish("ffn0", dx)

    out_g, out_d, out_m, out_v = {}, {}, {}, {}

    gs_list = [gsmall[n] for n in SMALL]
    gs_all, = _all_gather("gather_small_grads", [_pack(gs_list)], ["blk"])
    gs_sum = _unpack(_sum_parts("sum_small_grads", gs_all), [a.shape for a in gs_list])
    local_g = []
    for n, g in zip(SMALL, gs_sum):
        if n in SHARDED_SMALL:
            ax = SHARDED_SMALL[n]
            size = wts[n].shape[ax + 1]
            g = lax.dynamic_slice_in_dim(g, me * size, size, axis=ax)
        local_g.append(g.reshape(wts[n].shape))
    shapes = [wts[n].shape for n in SMALL]
    upd = _adamw("adamw_small", _pack([wts[n] for n in SMALL]), _pack(local_g),
                 _pack([mom[n] for n in SMALL]), _pack([var[n] for n in SMALL]))
    for i, outd in enumerate((out_d, out_m, out_v)):
        for n, a in zip(SMALL, _unpack(upd[i], shapes)):
            outd[n] = a
    for n, g in zip(SMALL, local_g):
        out_g[n] = g

    finish("mix_e", upd[0])
    for n in BIG:
        out_g[n], out_d[n], out_m[n], out_v[n] = results[n]

    return (loss, dx.reshape(bsz, seq, d), *[out_g[n] for n in WEIGHTS], *[out_d[n] for n in WEIGHTS],
            *[out_m[n] for n in WEIGHTS], *[out_v[n] for n in WEIGHTS])
```

```python
import functools

import jax
import jax.numpy as jnp
from jax import lax
from jax.experimental import pallas as pl
from jax.experimental.pallas import tpu as pltpu

F32 = jnp.float32
BF16 = jnp.bfloat16
NDEV = 8
MESH_AXES = ("x", "y", "c")
EPS = 1e-6
POOL_WINDOWS = (2, 4, 8, 16)
CONV_WIDTH = 31
SHORT_WIDTH = 3
ADAM_LR = 0.001
ADAM_B1 = 0.9
ADAM_B2 = 0.999
ADAM_EPS = 1e-08
ADAM_WD = 0.01
ADAM_STEP = 10
LANES = 128
SUBLANES = 8
VMEM_LIMIT = 56 * 1024 * 1024
MESH = pl.DeviceIdType.MESH
ANY = pl.BlockSpec(memory_space=pl.ANY)


def _cp(*sem):
    return pltpu.CompilerParams(dimension_semantics=sem, vmem_limit_bytes=VMEM_LIMIT)


def _tile(n, pref, unit=LANES):
    if n <= pref:
        return n
    t = (pref // unit) * unit
    while t > unit and n % t:
        t -= unit
    assert n % t == 0, (n, pref)
    return t


def _sigmoid(v):
    return 1.0 / (1.0 + jnp.exp(-v))


def _mm(name, pairs, a_specs, b_specs, dims, out_shape, o_spec, grid, acc_shape,
        res=None, res_spec=None, dep=None):
    np_ = len(pairs)
    nk = grid[2]
    has_res = res is not None
    n_in = 2 * np_ + (1 if has_res else 0) + (0 if dep is None else 1)

    def body(*refs):
        a_refs = refs[:np_]
        b_refs = refs[np_:2 * np_]
        r_ref = refs[2 * np_] if has_res else None
        o_ref = refs[n_in]
        acc = refs[-1]

        def part():
            s = None
            for a_ref, b_ref in zip(a_refs, b_refs):
                d = lax.dot_general(a_ref[...], b_ref[...], dims, preferred_element_type=F32)
                s = d if s is None else s + d
            return s

        def finish(v):
            if has_res:
                v = v + r_ref[...]
            o_ref[...] = v.astype(o_ref.dtype)

        if nk == 1:
            finish(part())
        else:
            k = pl.program_id(2)

            @pl.when(k == 0)
            def _():
                acc[...] = part()

            @pl.when(k > 0)
            def _():
                acc[...] += part()

            @pl.when(k == nk - 1)
            def _():
                finish(acc[...])

    ins = [p[0] for p in pairs] + [p[1] for p in pairs]
    specs = list(a_specs) + list(b_specs)
    if has_res:
        ins.append(res)
        specs.append(res_spec)
    if dep is not None:
        ins.append(dep)
        specs.append(ANY)
    return pl.pallas_call(
        body, name=name, grid=grid, in_specs=specs, out_specs=o_spec, out_shape=out_shape,
        scratch_shapes=[pltpu.VMEM(acc_shape if nk > 1 else (SUBLANES, LANES), F32)],
        compiler_params=_cp("parallel", "parallel", "arbitrary"))(*ins)


NN = (((1,), (0,)), ((), ()))
NT = (((1,), (1,)), ((), ()))
TN = (((0,), (0,)), ((), ()))


def _mm_nn(name, a, b, out_dtype, res=None, dep=None):
    m, kk = a.shape
    n = b.shape[1]
    tm, tn, tk = _tile(m, 1024), _tile(n, 1024), _tile(kk, 1024)
    return _mm(name, [(a, b)],
               [pl.BlockSpec((tm, tk), lambda i, j, k: (i, k))],
               [pl.BlockSpec((tk, tn), lambda i, j, k: (k, j))], NN,
               jax.ShapeDtypeStruct((m, n), out_dtype),
               pl.BlockSpec((tm, tn), lambda i, j, k: (i, j)),
               (m // tm, n // tn, kk // tk), (tm, tn), res,
               pl.BlockSpec((tm, tn), lambda i, j, k: (i, j)), dep=dep)


def _mm_nt(name, a, b, out_dtype, dep=None):
    m, n = a.shape
    kk = b.shape[0]
    tm, tn, tk = _tile(m, 1024), _tile(kk, 1024), _tile(n, 1024)
    return _mm(name, [(a, b)],
               [pl.BlockSpec((tm, tk), lambda i, j, k: (i, k))],
               [pl.BlockSpec((tn, tk), lambda i, j, k: (j, k))], NT,
               jax.ShapeDtypeStruct((m, kk), out_dtype),
               pl.BlockSpec((tm, tn), lambda i, j, k: (i, j)),
               (m // tm, kk // tn, n // tk), (tm, tn), dep=dep)


def _mm_tn(name, a, b, out_dtype, dep=None):
    t, m = a.shape
    n = b.shape[1]
    tm, tn, tk = _tile(m, 1024), _tile(n, 1024), _tile(t, 1024)
    return _mm(name, [(a, b)],
               [pl.BlockSpec((tk, tm), lambda i, j, k: (k, i))],
               [pl.BlockSpec((tk, tn), lambda i, j, k: (k, j))], TN,
               jax.ShapeDtypeStruct((m, n), out_dtype),
               pl.BlockSpec((tm, tn), lambda i, j, k: (i, j)),
               (m // tm, n // tn, t // tk), (tm, tn), dep=dep)


def _mm_down(name, act, wd, res, dep=None):
    nb, t, f = act.shape
    d = wd.shape[2]
    tm, tn = _tile(t, 1024), _tile(d, 1024)
    return _mm(name, [(act, wd)],
               [pl.BlockSpec((None, tm, f), lambda i, j, k: (k, i, 0))],
               [pl.BlockSpec((None, f, tn), lambda i, j, k: (k, 0, j))], NN,
               jax.ShapeDtypeStruct((t, d), F32),
               pl.BlockSpec((tm, tn), lambda i, j, k: (i, j)),
               (t // tm, d // tn, nb), (tm, tn), res,
               pl.BlockSpec((tm, tn), lambda i, j, k: (i, j)), dep=dep)


def _mm_ffn_dn(name, dg, wg, dup, wu):
    nb, t, f = dg.shape
    d = wg.shape[2]
    tm, tn = _tile(t, 1024), _tile(d, 1024)
    a_spec = pl.BlockSpec((None, tm, f), lambda i, j, k: (k, i, 0))
    b_spec = pl.BlockSpec((None, f, tn), lambda i, j, k: (k, 0, j))
    return _mm(name, [(dg, wg), (dup, wu)], [a_spec, a_spec], [b_spec, b_spec], NN,
               jax.ShapeDtypeStruct((t, d), BF16),
               pl.BlockSpec((tm, tn), lambda i, j, k: (i, j)),
               (t // tm, d // tn, nb), (tm, tn))


def _mm_dwd(name, act, dh, dep=None):
    nb, t, f = act.shape
    d = dh.shape[1]
    tn, tk = _tile(d, 1024), _tile(t, 1024)
    return _mm(name, [(act, dh)],
               [pl.BlockSpec((None, tk, f), lambda i, j, k: (i, k, 0))],
               [pl.BlockSpec((tk, tn), lambda i, j, k: (k, j))], TN,
               jax.ShapeDtypeStruct((nb, f, d), BF16),
               pl.BlockSpec((None, f, tn), lambda i, j, k: (i, 0, j)),
               (nb, d // tn, t // tk), (f, tn), dep=dep)


def _ffn_fwd(name, n, wg, wu):
    nb, f, d = wg.shape
    t = n.shape[0]
    tm = _tile(t, 512)

    def body(n_ref, wg_ref, wu_ref, g_ref, up_ref, act_ref):
        nv = n_ref[...]
        g = lax.dot_general(nv, wg_ref[...], NT, preferred_element_type=F32)
        up = lax.dot_general(nv, wu_ref[...], NT, preferred_element_type=F32)
        g_ref[...] = g.astype(BF16)
        up_ref[...] = up.astype(BF16)
        act_ref[...] = (g * _sigmoid(g) * up).astype(BF16)

    w_spec = pl.BlockSpec((None, f, d), lambda j, i: (j, 0, 0))
    o_spec = pl.BlockSpec((None, tm, f), lambda j, i: (j, i, 0))
    shp = jax.ShapeDtypeStruct((nb, t, f), BF16)
    return pl.pallas_call(
        body, name=name, grid=(nb, t // tm),
        in_specs=[pl.BlockSpec((tm, d), lambda j, i: (i, 0)), w_spec, w_spec],
        out_specs=[o_spec, o_spec, o_spec], out_shape=[shp, shp, shp],
        compiler_params=_cp("parallel", "parallel"))(n, wg, wu)


def _ffn_bwd_act(name, dh, wd, g, up, dep=None):
    nb, f, d = wd.shape
    t = dh.shape[0]
    tm = _tile(t, 512)

    def body(dh_ref, wd_ref, g_ref, up_ref, *rest):
        dg_ref, dup_ref = rest[-2:]
        da = lax.dot_general(dh_ref[...], wd_ref[...], NT, preferred_element_type=F32)
        gv = g_ref[...].astype(F32)
        uv = up_ref[...].astype(F32)
        sg = _sigmoid(gv)
        dg_ref[...] = (da * uv * (sg * (1.0 + gv * (1.0 - sg)))).astype(BF16)
        dup_ref[...] = (da * gv * sg).astype(BF16)

    o_spec = pl.BlockSpec((None, tm, f), lambda j, i: (j, i, 0))
    shp = jax.ShapeDtypeStruct((nb, t, f), BF16)
    return pl.pallas_call(
        body, name=name, grid=(nb, t // tm),
        in_specs=[pl.BlockSpec((tm, d), lambda j, i: (i, 0)),
                  pl.BlockSpec((None, f, d), lambda j, i: (j, 0, 0)), o_spec, o_spec]
        + ([] if dep is None else [ANY]),
        out_specs=[o_spec, o_spec], out_shape=[shp, shp],
        compiler_params=_cp("parallel", "parallel"))(dh, wd, g, up, *([] if dep is None else [dep]))


def _rms_fwd(name, h, gain):
    t, d = h.shape
    tr = _tile(t, 512, SUBLANES)

    def body(h_ref, g_ref, n_ref):
        hv = h_ref[...]
        r = lax.rsqrt(jnp.mean(hv * hv, axis=-1, keepdims=True) + EPS)
        n_ref[...] = (hv * r * g_ref[...]).astype(BF16)

    return pl.pallas_call(
        body, name=name, grid=(t // tr,),
        in_specs=[pl.BlockSpec((tr, d), lambda i: (i, 0)), pl.BlockSpec((1, d), lambda i: (0, 0))],
        out_specs=pl.BlockSpec((tr, d), lambda i: (i, 0)),
        out_shape=jax.ShapeDtypeStruct((t, d), BF16),
        compiler_params=_cp("parallel"))(h, gain)


def _rms_bwd_math(hv, gain, dn):
    d = hv.shape[-1]
    r = lax.rsqrt(jnp.mean(hv * hv, axis=-1, keepdims=True) + EPS)
    xhat = hv * r
    dxh = dn * gain
    dh = r * (dxh - xhat * (jnp.sum(dxh * xhat, axis=-1, keepdims=True) / d))
    dgain = jnp.sum(dn * xhat, axis=0, keepdims=True)
    return dh, dgain


def _rms_bwd(name, h, gain, dn, dres):
    t, d = h.shape
    tr = _tile(t, 256, SUBLANES)

    def body(h_ref, g_ref, dn_ref, dr_ref, dh_ref, dhb_ref, dg_ref):
        dh, dgain = _rms_bwd_math(h_ref[...], g_ref[...], dn_ref[...].astype(F32))
        dh = dh + dr_ref[...]
        dh_ref[...] = dh
        dhb_ref[...] = dh.astype(BF16)

        @pl.when(pl.program_id(0) == 0)
        def _():
            dg_ref[...] = dgain

        @pl.when(pl.program_id(0) > 0)
        def _():
            dg_ref[...] += dgain

    row = pl.BlockSpec((tr, d), lambda i: (i, 0))
    vec = pl.BlockSpec((1, d), lambda i: (0, 0))
    return pl.pallas_call(
        body, name=name, grid=(t // tr,), in_specs=[row, vec, row, row],
        out_specs=[row, row, vec],
        out_shape=[jax.ShapeDtypeStruct((t, d), F32), jax.ShapeDtypeStruct((t, d), BF16),
                   jax.ShapeDtypeStruct((1, d), F32)],
        compiler_params=_cp("arbitrary"))(h, gain, dn, dres)


def _loss_head(name, h, gain, tgt):
    t, d = h.shape
    tr = _tile(t, 256, SUBLANES)

    def body(h_ref, g_ref, t_ref, dh_ref, dhb_ref, dg_ref, ls_ref):
        hv = h_ref[...]
        gv = g_ref[...]
        r = lax.rsqrt(jnp.mean(hv * hv, axis=-1, keepdims=True) + EPS)
        err = hv * r * gv - t_ref[...]
        lsum = 0.5 * jnp.sum(err * err, axis=0, keepdims=True) / d
        dh, dgain = _rms_bwd_math(hv, gv, err / d)
        dh_ref[...] = dh
        dhb_ref[...] = dh.astype(BF16)

        @pl.when(pl.program_id(0) == 0)
        def _():
            dg_ref[...] = dgain
            ls_ref[...] = lsum

        @pl.when(pl.program_id(0) > 0)
        def _():
            dg_ref[...] += dgain
            ls_ref[...] += lsum

    row = pl.BlockSpec((tr, d), lambda i: (i, 0))
    vec = pl.BlockSpec((1, d), lambda i: (0, 0))
    return pl.pallas_call(
        body, name=name, grid=(t // tr,), in_specs=[row, vec, row],
        out_specs=[row, row, vec, vec],
        out_shape=[jax.ShapeDtypeStruct((t, d), F32), jax.ShapeDtypeStruct((t, d), BF16),
                   jax.ShapeDtypeStruct((1, d), F32), jax.ShapeDtypeStruct((1, d), F32)],
        compiler_params=_cp("arbitrary"))(h, gain, tgt)


def _conv_geom(t, seq, c, k):
    halo = 32 if k - 1 > SUBLANES else SUBLANES
    assert k - 1 <= halo
    tm = min(256, seq // 2)
    tc = min(512, c)
    assert seq % tm == 0 and tm % halo == 0 and c % tc == 0 and t % seq == 0
    return halo, tm, tc, min(128, tm), min(LANES, tc)


def _pre(kind, a, b):
    if kind == "glu":
        return a * _sigmoid(b)
    if kind == "mul":
        return a * b
    return a


def _taps(k):
    return [(s % SUBLANES, s // SUBLANES, s) for s in range(k)]


def _conv_fwd(name, seq, c, w, x1, c1, x2=None, c2=0, pre=None, bias=None, post=None, cpost=0):
    t = x1.shape[0]
    k = w.shape[0]
    halo, tm, tc, sr, sl = _conv_geom(t, seq, c, k)
    nb, cps = tm // halo, seq // tm
    two = x2 is not None
    has_bias, has_post = bias is not None, post is not None

    def body(*refs):
        it = iter(refs)
        x1c, x1h = next(it), next(it)
        x2c, x2h = (next(it), next(it)) if two else (None, None)
        w_ref = next(it)
        b_ref = next(it) if has_bias else None
        p_ref = next(it) if has_post else None
        o_ref = next(it)
        y_ref = next(it) if has_post else None
        xs = next(it)
        first = (pl.program_id(1) % cps) == 0
        hv = _pre(pre, x1h[...].astype(F32), x2h[...].astype(F32) if two else None)
        xs[0:halo, :] = jnp.where(first, 0.0, hv)
        xs[halo:halo + tm, :] = _pre(pre, x1c[...].astype(F32), x2c[...].astype(F32) if two else None)
        for l0 in range(0, tc, sl):
            ls = slice(l0, l0 + sl)
            for r0 in range(0, tm, sr):
                win = xs[r0:r0 + sr + halo, ls]
                acc = jnp.zeros((sr, sl), F32)
                rolled = {}
                for r, q, s in _taps(k):
                    if r not in rolled:
                        rolled[r] = win if r == 0 else pltpu.roll(win, r, 0)
                    lo = halo - SUBLANES * q
                    acc = acc + w_ref[k - 1 - s:k - s, ls] * rolled[r][lo:lo + sr]
                if has_bias:
                    acc = acc + b_ref[:, ls]
                o_ref[r0:r0 + sr, ls] = acc.astype(o_ref.dtype)
                if has_post:
                    y_ref[r0:r0 + sr, ls] = (acc * p_ref[r0:r0 + sr, ls].astype(F32)).astype(y_ref.dtype)

    def cur(off):
        return pl.BlockSpec((tm, tc), lambda j, i: (i, off // tc + j))

    def prev(off):
        return pl.BlockSpec((halo, tc), lambda j, i: (jnp.maximum(i * nb - 1, 0), off // tc + j))

    ins, specs = [x1, x1], [cur(c1), prev(c1)]
    if two:
        ins += [x2, x2]
        specs += [cur(c2), prev(c2)]
    ins.append(w)
    specs.append(pl.BlockSpec((k, tc), lambda j, i: (0, j)))
    if has_bias:
        ins.append(bias)
        specs.append(pl.BlockSpec((1, tc), lambda j, i: (0, j)))
    if has_post:
        ins.append(post)
        specs.append(cur(cpost))
    o_spec = pl.BlockSpec((tm, tc), lambda j, i: (i, j))
    shp = jax.ShapeDtypeStruct((t, c), BF16)
    return pl.pallas_call(
        body, name=name, grid=(c // tc, t // tm), in_specs=specs,
        out_specs=[o_spec, o_spec] if has_post else o_spec,
        out_shape=[shp, shp] if has_post else shp,
        scratch_shapes=[pltpu.VMEM((halo + tm, tc), F32)],
        compiler_params=_cp("parallel", "parallel"))(*ins)


def _conv_bwd(name, seq, c, w, d1, cd1, d2=None, cd2=0, dpre=None,
              x1=None, c1=0, x2=None, c2=0, pre=None):
    t = d1.shape[0]
    k = w.shape[0]
    halo, tm, tc, sr, sl = _conv_geom(t, seq, c, k)
    nb, cps = tm // halo, seq // tm
    nchunks = t // tm
    dtwo, xtwo, has_x = d2 is not None, x2 is not None, x1 is not None

    def body(*refs):
        it = iter(refs)
        d1c, d1n = next(it), next(it)
        d2c, d2n = (next(it), next(it)) if dtwo else (None, None)
        x1c, x1h = (next(it), next(it)) if has_x else (None, None)
        x2c, x2h = (next(it), next(it)) if xtwo else (None, None)
        w_ref = next(it)
        dx_ref = next(it)
        dw_ref = next(it) if has_x else None
        ds = next(it)
        xs = next(it) if has_x else None
        i = pl.program_id(1)
        last = (i % cps) == cps - 1
        ds[0:tm, :] = _pre(dpre, d1c[...].astype(F32), d2c[...].astype(F32) if dtwo else None)
        nv = _pre(dpre, d1n[...].astype(F32), d2n[...].astype(F32) if dtwo else None)
        ds[tm:tm + halo, :] = jnp.where(last, 0.0, nv)
        if has_x:
            first = (i % cps) == 0
            hv = _pre(pre, x1h[...].astype(F32), x2h[...].astype(F32) if xtwo else None)
            xs[0:halo, :] = jnp.where(first, 0.0, hv)
            xs[halo:halo + tm, :] = _pre(pre, x1c[...].astype(F32), x2c[...].astype(F32) if xtwo else None)

            @pl.when(i == 0)
            def _():
                dw_ref[...] = jnp.zeros_like(dw_ref)

        for l0 in range(0, tc, sl):
            ls = slice(l0, l0 + sl)
            for r0 in range(0, tm, sr):
                win = ds[r0:r0 + sr + halo, ls]
                nrow = sr + halo
                acc = jnp.zeros((sr, sl), F32)
                rolled = {}
                for r, q, s in _taps(k):
                    if r not in rolled:
                        rolled[r] = win if r == 0 else pltpu.roll(win, nrow - r, 0)
                    lo = SUBLANES * q
                    acc = acc + w_ref[k - 1 - s:k - s, ls] * rolled[r][lo:lo + sr]
                dx_ref[r0:r0 + sr, ls] = acc.astype(dx_ref.dtype)
                if has_x:
                    dcur = win[0:sr]
                    xwin = xs[r0:r0 + sr + halo, ls]
                    xrolled = {}
                    for r, q, s in _taps(k):
                        if r not in xrolled:
                            xrolled[r] = xwin if r == 0 else pltpu.roll(xwin, r, 0)
                        lo = halo - SUBLANES * q
                        part = jnp.sum(dcur * xrolled[r][lo:lo + sr], axis=0, keepdims=True)
                        dw_ref[k - 1 - s:k - s, ls] += part

    def cur(off):
        return pl.BlockSpec((tm, tc), lambda j, i: (i, off // tc + j))

    def prev(off):
        return pl.BlockSpec((halo, tc), lambda j, i: (jnp.maximum(i * nb - 1, 0), off // tc + j))

    def nxt(off):
        return pl.BlockSpec((halo, tc),
                            lambda j, i: (jnp.minimum((i + 1) * nb, nchunks * nb - 1), off // tc + j))

    ins, specs = [d1, d1], [cur(cd1), nxt(cd1)]
    if dtwo:
        ins += [d2, d2]
        specs += [cur(cd2), nxt(cd2)]
    if has_x:
        ins += [x1, x1]
        specs += [cur(c1), prev(c1)]
    if xtwo:
        ins += [x2, x2]
        specs += [cur(c2), prev(c2)]
    ins.append(w)
    specs.append(pl.BlockSpec((k, tc), lambda j, i: (0, j)))
    o_specs = [pl.BlockSpec((tm, tc), lambda j, i: (i, j))]
    o_shapes = [jax.ShapeDtypeStruct((t, c), BF16)]
    scratch = [pltpu.VMEM((tm + halo, tc), F32)]
    if has_x:
        o_specs.append(pl.BlockSpec((k, tc), lambda j, i: (0, j)))
        o_shapes.append(jax.ShapeDtypeStruct((k, c), F32))
        scratch.append(pltpu.VMEM((halo + tm, tc), F32))
    out = pl.pallas_call(
        body, name=name, grid=(c // tc, t // tm), in_specs=specs, out_specs=o_specs,
        out_shape=o_shapes, scratch_shapes=scratch,
        compiler_params=_cp("parallel", "arbitrary"))(*ins)
    return out if has_x else out[0]


def _pool_taps(c):
    kmax = max(POOL_WINDOWS)
    grp = c // len(POOL_WINDOWS)
    cols = []
    for wdw in POOL_WINDOWS:
        col = jnp.concatenate([jnp.zeros((kmax - wdw,), F32), jnp.ones((wdw,), F32)])
        cols.append(jnp.tile(col[:, None], (1, grp)))
    return jnp.concatenate(cols, axis=1)


def _counts(i, tr, seq, grp):
    pos = (i * tr + lax.broadcasted_iota(jnp.int32, (tr, 1), 0)) % seq + 1
    return [1.0 / jnp.minimum(pos, wdw).astype(F32) for wdw in POOL_WINDOWS]


def _ln_stats(a2):
    mu = jnp.mean(a2, axis=-1, keepdims=True)
    xc = a2 - mu
    rstd = lax.rsqrt(jnp.mean(xc * xc, axis=-1, keepdims=True) + EPS)
    return xc * rstd, rstd


def _even_fwd(name, seq, a2, ws, u, ln_g, ln_b, w_pool, scale):
    t, c = a2.shape
    ng = len(POOL_WINDOWS)
    grp = c // ng
    tr = _tile(t, 256, SUBLANES)

    def body(a_ref, ws_ref, b_ref, g_ref, bb_ref, wp_ref, sc_ref, z_ref, pm_ref):
        xhat, _ = _ln_stats(a_ref[...].astype(F32))
        l = xhat * g_ref[...] + bb_ref[...]
        z_ref[:, 0:c] = (l * _sigmoid(l)).astype(BF16)
        inv = _counts(pl.program_id(0), tr, seq, grp)
        for g in range(ng):
            gs = slice(g * grp, (g + 1) * grp)
            pm = (ws_ref[:, gs].astype(F32) * inv[g] - b_ref[:, gs].astype(F32)).astype(BF16)
            pm_ref[:, gs] = pm
            q = jnp.dot(pm, wp_ref[g], preferred_element_type=F32)
            z_ref[:, c + g * grp:c + (g + 1) * grp] = (q * sc_ref[:, gs]).astype(BF16)

    row = pl.BlockSpec((tr, c), lambda i: (i, 0))
    vec = pl.BlockSpec((1, c), lambda i: (0, 0))
    return pl.pallas_call(
        body, name=name, grid=(t // tr,),
        in_specs=[row, row, pl.BlockSpec((tr, c), lambda i: (i, 2)), vec, vec,
                  pl.BlockSpec((ng, grp, grp), lambda i: (0, 0, 0)), vec],
        out_specs=[pl.BlockSpec((tr, 2 * c), lambda i: (i, 0)), row],
        out_shape=[jax.ShapeDtypeStruct((t, 2 * c), BF16), jax.ShapeDtypeStruct((t, c), BF16)],
        compiler_params=_cp("parallel"))(a2, ws, u, ln_g, ln_b, w_pool, scale)


def _even_bwd(name, seq, dz, a2, pm, ln_g, ln_b, w_pool, scale):
    t, c = a2.shape
    ng = len(POOL_WINDOWS)
    grp = c // ng
    tr = _tile(t, 256, SUBLANES)

    def body(dz_ref, a_ref, pm_ref, g_ref, bb_ref, wp_ref, sc_ref,
             da_ref, dws_ref, dpm_ref, vec_ref, dwp_ref):
        i = pl.program_id(0)

        @pl.when(i == 0)
        def _():
            vec_ref[...] = jnp.zeros_like(vec_ref)
            dwp_ref[...] = jnp.zeros_like(dwp_ref)

        xhat, rstd = _ln_stats(a_ref[...].astype(F32))
        gv = g_ref[...]
        l = xhat * gv + bb_ref[...]
        sg = _sigmoid(l)
        dl = dz_ref[:, 0:c].astype(F32) * (sg * (1.0 + l * (1.0 - sg)))
        dxh = dl * gv
        da2 = rstd * (dxh - jnp.mean(dxh, axis=-1, keepdims=True)
                      - xhat * jnp.mean(dxh * xhat, axis=-1, keepdims=True))
        da_ref[...] = da2.astype(BF16)
        vec_ref[0:1, :] += jnp.sum(dl * xhat, axis=0, keepdims=True)
        vec_ref[1:2, :] += jnp.sum(dl, axis=0, keepdims=True)
        vec_ref[2:3, :] += jnp.sum(da2, axis=0, keepdims=True)
        inv = _counts(i, tr, seq, grp)
        for g in range(ng):
            gs = slice(g * grp, (g + 1) * grp)
            pmv = pm_ref[:, gs]
            wp = wp_ref[g]
            dp = dz_ref[:, c + g * grp:c + (g + 1) * grp].astype(F32)
            q = jnp.dot(pmv, wp, preferred_element_type=F32)
            vec_ref[3:4, gs] += jnp.sum(dp * q, axis=0, keepdims=True)
            dq = (dp * sc_ref[:, gs]).astype(BF16)
            dpm = lax.dot_general(dq, wp, NT, preferred_element_type=F32)
            dwp_ref[g] += lax.dot_general(pmv, dq, TN, preferred_element_type=F32)
            dpm_ref[:, gs] = dpm.astype(BF16)
            dws_ref[:, gs] = (dpm * inv[g]).astype(BF16)

    row = pl.BlockSpec((tr, c), lambda i: (i, 0))
    vec = pl.BlockSpec((1, c), lambda i: (0, 0))
    rshape = jax.ShapeDtypeStruct((t, c), BF16)
    return pl.pallas_call(
        body, name=name, grid=(t // tr,),
        in_specs=[pl.BlockSpec((tr, 2 * c), lambda i: (i, 0)), row, row, vec, vec,
                  pl.BlockSpec((ng, grp, grp), lambda i: (0, 0, 0)), vec],
        out_specs=[row, row, row, pl.BlockSpec((SUBLANES, c), lambda i: (0, 0)),
                   pl.BlockSpec((ng, grp, grp), lambda i: (0, 0, 0))],
        out_shape=[rshape, rshape, rshape, jax.ShapeDtypeStruct((SUBLANES, c), F32),
                   jax.ShapeDtypeStruct((ng, grp, grp), F32)],
        compiler_params=_cp("arbitrary"))(dz, a2, pm, ln_g, ln_b, w_pool, scale)


def _even_du(name, u, da1, dbp, dpm):
    t, c = da1.shape
    tr = _tile(t, 256, SUBLANES)

    def body(u_ref, da_ref, dbp_ref, dpm_ref, du_ref):
        val = u_ref[:, 0:c].astype(F32)
        sg = _sigmoid(u_ref[:, c:2 * c].astype(F32))
        da = da_ref[...].astype(F32)
        du_ref[:, 0:c] = (da * sg).astype(BF16)
        du_ref[:, c:2 * c] = (da * val * sg * (1.0 - sg)).astype(BF16)
        du_ref[:, 2 * c:3 * c] = (dbp_ref[...].astype(F32) - dpm_ref[...].astype(F32)).astype(BF16)

    row = pl.BlockSpec((tr, c), lambda i: (i, 0))
    wide = pl.BlockSpec((tr, 3 * c), lambda i: (i, 0))
    return pl.pallas_call(
        body, name=name, grid=(t // tr,), in_specs=[wide, row, row, row], out_specs=wide,
        out_shape=jax.ShapeDtypeStruct((t, 3 * c), BF16),
        compiler_params=_cp("parallel"))(u, da1, dbp, dpm)


def _odd_du(name, u, dy, co, dxc):
    t, c = dy.shape
    tr = _tile(t, 256, SUBLANES)

    def body(u_ref, dy_ref, co_ref, dx_ref, du_ref):
        dx = dx_ref[...].astype(F32)
        du_ref[:, 0:c] = (dy_ref[...].astype(F32) * co_ref[...].astype(F32)).astype(BF16)
        du_ref[:, c:2 * c] = (dx * u_ref[:, 2 * c:3 * c].astype(F32)).astype(BF16)
        du_ref[:, 2 * c:3 * c] = (dx * u_ref[:, c:2 * c].astype(F32)).astype(BF16)

    row = pl.BlockSpec((tr, c), lambda i: (i, 0))
    wide = pl.BlockSpec((tr, 3 * c), lambda i: (i, 0))
    return pl.pallas_call(
        body, name=name, grid=(t // tr,), in_specs=[wide, row, row, row], out_specs=wide,
        out_shape=jax.ShapeDtypeStruct((t, 3 * c), BF16),
        compiler_params=_cp("parallel"))(u, dy, co, dxc)


def _local_step(x, tgt, seq, small, get_w, put_g, sync):
    t, d = x.shape
    c = d // 2
    cw_e, cw_o = small["conv_w_e"], small["conv_w_o"]
    wp = small["w_pool_e"].astype(BF16)
    ptaps = _pool_taps(c)
    row = lambda v: v.reshape(1, -1)

    we = get_w("mix_e", x)
    n0 = _rms_fwd("rms_fwd_mix0", x, row(small["mix_norm_e"]))
    u0 = _mm_nn("mm_in_e", n0, we["w_in"], BF16)
    a2 = _conv_fwd("conv_e_fwd", seq, c, cw_e, u0, 0, u0, c, "glu", bias=row(small["conv_b_e"]))
    ws = _conv_fwd("pool_fwd", seq, c, ptaps, u0, 2 * c)
    z0, pm = _even_fwd("even_fwd", seq, a2, ws, u0, row(small["ln_g_e"]), row(small["ln_b_e"]),
                       wp, row(small["pool_scale_e"]))
    h1 = _mm_nn("mm_out_e", z0, we["w_out"], F32, res=x, dep=sync("fwd_mix_e", z0))
    wf0 = get_w("ffn0", h1)
    n1 = _rms_fwd("rms_fwd_ffn0", h1, row(small["ffn_norm"][0]))
    g0, up0, act0 = _ffn_fwd("ffn0_fwd", n1, wf0["w_gate"], wf0["w_up"])
    h2 = _mm_down("mm_down0", act0, wf0["w_down"], h1, dep=sync("fwd_ffn0", act0))
    wo = get_w("mix_o", h2)
    n2 = _rms_fwd("rms_fwd_mix1", h2, row(small["mix_norm_o"]))
    u1 = _mm_nn("mm_in_o", n2, wo["w_in"], BF16)
    co, y1 = _conv_fwd("conv_o_fwd", seq, d, cw_o, u1, d, u1, 2 * d, "mul", post=u1, cpost=0)
    h3 = _mm_nn("mm_out_o", y1, wo["w_out"], F32, res=h2, dep=sync("fwd_mix_o", u1))
    wf1 = get_w("ffn1", h3)
    n3 = _rms_fwd("rms_fwd_ffn1", h3, row(small["ffn_norm"][1]))
    g1, up1, act1 = _ffn_fwd("ffn1_fwd", n3, wf1["w_gate"], wf1["w_up"])
    h4 = _mm_down("mm_down1", act1, wf1["w_down"], h3)

    dh4, dh4b, d_final, lsum = _loss_head("loss_head", h4, row(small["final_norm"]), tgt)

    def ffn_bwd(tag, dh, dhb, h_in, gain, n, g, up, act, w, dep):
        dg, dup = _ffn_bwd_act("ffn%s_bwd_act" % tag, dhb, w["w_down"], g, up, dep=dep)
        dwd = _mm_dwd("mm_dwd%s" % tag, act, dhb)
        dwg = _mm_dwd("mm_dwg%s" % tag, dg, n, dep=sync("bwd_ffn" + tag, dwd))
        dwu = _mm_dwd("mm_dwu%s" % tag, dup, n)
        dn = _mm_ffn_dn("mm_ffn_dn%s" % tag, dg, w["w_gate"], dup, w["w_up"])
        dh_in, dhb_in, dgain = _rms_bwd("rms_bwd_ffn%s" % tag, h_in, gain, dn, dh)
        dep = put_g("ffn" + tag, {"w_gate": dwg, "w_up": dwu, "w_down": dwd})
        return dh_in, dhb_in, dgain, dep

    dh3, dh3b, d_ffn1, dep = ffn_bwd("1", dh4, dh4b, h3, row(small["ffn_norm"][1]), n3, g1, up1,
                                     act1, wf1, None)
    dw_out_o = _mm_tn("mm_dw_out_o", y1, dh3b, BF16, dep=dep)
    dy1 = _mm_nt("mm_dy_o", dh3b, wo["w_out"], BF16, dep=sync("bwd_mix_o", dw_out_o))
    dxc, dcw_o = _conv_bwd("conv_o_bwd", seq, d, cw_o, dy1, 0, u1, 0, "mul",
                           x1=u1, c1=d, x2=u1, c2=2 * d, pre="mul")
    du1 = _odd_du("odd_du", u1, dy1, co, dxc)
    dw_in_o = _mm_tn("mm_dw_in_o", n2, du1, BF16)
    dn2 = _mm_nt("mm_dn_o", du1, wo["w_in"], BF16)
    dh2, dh2b, d_mix_o = _rms_bwd("rms_bwd_mix1", h2, row(small["mix_norm_o"]), dn2, dh3)
    dep = put_g("mix_o", {"w_in": dw_in_o, "w_out": dw_out_o})

    dh1, dh1b, d_ffn0, dep = ffn_bwd("0", dh2, dh2b, h1, row(small["ffn_norm"][0]), n1, g0, up0,
                                     act0, wf0, dep)
    dw_out_e = _mm_tn("mm_dw_out_e", z0, dh1b, BF16, dep=dep)
    dz0 = _mm_nt("mm_dz_e", dh1b, we["w_out"], BF16, dep=sync("bwd_mix_e", dw_out_e))
    da2, dws, dpm, vecs, dwp = _even_bwd("even_bwd", seq, dz0, a2, pm, row(small["ln_g_e"]),
                                         row(small["ln_b_e"]), wp, row(small["pool_scale_e"]))
    da1, dcw_e = _conv_bwd("conv_e_bwd", seq, c, cw_e, da2, 0, x1=u0, c1=0, x2=u0, c2=c, pre="glu")
    dbp = _conv_bwd("pool_bwd", seq, c, ptaps, dws, 0)
    du0 = _even_du("even_du", u0, da1, dbp, dpm)
    dw_in_e = _mm_tn("mm_dw_in_e", n0, du0, BF16)
    dep = put_g("mix_e", {"w_in": dw_in_e, "w_out": dw_out_e})
    dn0 = _mm_nt("mm_dn_e", du0, we["w_in"], BF16, dep=dep)
    dx, _, d_mix_e = _rms_bwd("rms_bwd_mix0", x, row(small["mix_norm_e"]), dn0, dh1)

    gsmall = {"mix_norm_e": d_mix_e[0], "conv_w_e": dcw_e, "conv_b_e": vecs[2], "ln_g_e": vecs[0],
              "ln_b_e": vecs[1], "w_pool_e": dwp, "pool_scale_e": vecs[3], "mix_norm_o": d_mix_o[0],
              "conv_w_o": dcw_o, "ffn_norm": jnp.concatenate([d_ffn0, d_ffn1], axis=0),
              "final_norm": d_final[0]}
    return lsum, dx, gsmall


def _place():
    x, y, c = (lax.axis_index(a) for a in MESH_AXES)
    return x, y, c


def _index(p):
    return 4 * p[0] + 2 * p[1] + p[2]


def _slab(ref, kind, d, n):
    if kind == "blk":
        return ref.at[d]
    return ref.at[:, pl.ds(pl.multiple_of(d * n, LANES), n)]


def _all_gather(name, shards, kinds):
    na = len(shards)

    def body(*refs):
        x_refs, o_refs = refs[:na], refs[na:2 * na]
        send_sems, recv_sems, local_sems = refs[2 * na:]
        x, y, c = _place()
        me, sib = (x, y, c), (x, y, 1 - c)
        chips = [(1 - x, y), (x, 1 - y), (1 - x, 1 - y)]

        def slot(a, p):
            return _slab(o_refs[a], kinds[a], _index(p), shards[a].shape[1])

        def copy(a, k, block, to, src=None):
            return pltpu.make_async_remote_copy(
                src_ref=slot(a, block) if src is None else src, dst_ref=slot(a, block),
                send_sem=send_sems.at[a, k], recv_sem=recv_sems.at[a, k],
                device_id=to, device_id_type=MESH)

        mine = [pltpu.make_async_copy(x_refs[a], slot(a, me), local_sems.at[a]) for a in range(na)]
        for cp in mine:
            cp.start()
        first = []
        for a in range(na):
            first.append(copy(a, 0, me, sib, src=x_refs[a]))
            first += [copy(a, 1 + j, me, (*chip, c), src=x_refs[a]) for j, chip in enumerate(chips)]
        for cp in first:
            cp.start()
        passed = []
        for j, chip in enumerate(chips):
            for a in range(na):
                copy(a, 1 + j, (*chip, c), me).wait_recv()
                fwd = copy(a, 4 + j, (*chip, c), sib)
                fwd.start()
                passed.append(fwd)
        for a in range(na):
            copy(a, 0, sib, me).wait_recv()
            for j, chip in enumerate(chips):
                copy(a, 4 + j, (*chip, 1 - c), me).wait_recv()
        for cp in first + passed:
            cp.wait_send()
        for cp in mine:
            cp.wait()

    shapes = []
    for s, kind in zip(shards, kinds):
        m, n = s.shape
        shapes.append(jax.ShapeDtypeStruct((NDEV, m, n) if kind == "blk" else (m, NDEV * n), s.dtype))
    return pl.pallas_call(
        body, name=name, in_specs=[ANY] * na, out_specs=[ANY] * na, out_shape=shapes,
        scratch_shapes=[pltpu.SemaphoreType.DMA((na, 7)), pltpu.SemaphoreType.DMA((na, 7)),
                        pltpu.SemaphoreType.DMA((na,))])(*shards)


HBM = pl.BlockSpec(memory_space=pltpu.HBM)
SEM = pl.BlockSpec(memory_space=pltpu.SEMAPHORE)
EFFECT = pltpu.SideEffectType.DATAFLOW_SIDE_EFFECTING
NCHIPS = 4


def _in_hbm(a):
    return pltpu.with_memory_space_constraint(a, pltpu.HBM)


def _gathered_shape(s, kind):
    m, n = s.shape
    return (NDEV, m, n) if kind == "blk" else (m, NDEV * n)


def _first_targets():
    x, y, c = _place()
    return [(x, y, 1 - c), (1 - x, y, c), (x, 1 - y, c), (1 - x, 1 - y, c)]


def _gather_start(name, shards, kinds, after):
    na = len(shards)

    def body(*refs):
        x_refs, land_refs = refs[:na], refs[na:2 * na]
        send_sems, recv_sems = refs[2 * na + 1], refs[2 * na + 2]
        token = refs[-1]
        me = _index(_place())
        for a in range(na):
            for k, to in enumerate(_first_targets()):
                pltpu.make_async_remote_copy(
                    src_ref=x_refs[a], dst_ref=_slab(land_refs[a], kinds[a], me, shards[a].shape[1]),
                    send_sem=send_sems.at[4 * a + k], recv_sem=recv_sems.at[4 * a + k],
                    device_id=to, device_id_type=MESH).start()
        token[...] = jnp.zeros_like(token)

    lands = [lax.empty(_gathered_shape(s, k), s.dtype) for s, k in zip(shards, kinds)]
    outs = pl.pallas_call(
        body, name=name,
        out_shape=(pltpu.SemaphoreType.DMA((4 * na,)), pltpu.SemaphoreType.DMA((4 * na,)),
                   *[pltpu.HBM(s.shape, s.dtype) for s in shards],
                   *[pltpu.HBM(l.shape, l.dtype) for l in lands],
                   jax.ShapeDtypeStruct((SUBLANES, LANES), F32)),
        in_specs=[HBM] * (2 * na) + [ANY],
        out_specs=(SEM, SEM, *[HBM] * (2 * na), pl.BlockSpec(memory_space=pltpu.VMEM)),
        input_output_aliases={i: 2 + i for i in range(2 * na)},
        compiler_params=pltpu.CompilerParams(has_side_effects=EFFECT),
    )(*[_in_hbm(s) for s in shards], *[_in_hbm(l) for l in lands], after)
    return outs[0], outs[1], outs[2:2 + na], outs[2 + na:2 + 2 * na], outs[-1]


def _gather_wait(name, started, kinds, after):
    send_sems, recv_sems, shards, lands, _ = started
    na = len(shards)

    def body(*refs):
        x_refs, land_refs = refs[:na], refs[na:2 * na]
        s_sems, r_sems = refs[2 * na], refs[2 * na + 1]
        for a in range(na):
            for k, frm in enumerate(_first_targets()):
                cp = pltpu.make_async_remote_copy(
                    src_ref=x_refs[a],
                    dst_ref=_slab(land_refs[a], kinds[a], _index(frm), shards[a].shape[1]),
                    send_sem=s_sems.at[4 * a + k], recv_sem=r_sems.at[4 * a + k],
                    device_id=frm, device_id_type=MESH)
                cp.wait_send()
                cp.wait_recv()

    outs = pl.pallas_call(
        body, name=name,
        out_shape=(*[pltpu.HBM(s.shape, s.dtype) for s in shards],
                   *[pltpu.HBM(l.shape, l.dtype) for l in lands]),
        in_specs=[HBM] * (2 * na) + [SEM, SEM, ANY], out_specs=[HBM] * (2 * na),
        input_output_aliases={i: i for i in range(2 * na)},
        compiler_params=pltpu.CompilerParams(has_side_effects=EFFECT),
    )(*shards, *lands, send_sems, recv_sems, after)
    return outs[:na], outs[na:]


def _split_start(name, bufs, ncopies, plan, after):
    nb = len(bufs)

    def body(*refs):
        send_sems, recv_sems, token = refs[nb + 1], refs[nb + 2], refs[-1]
        for k, (src, dst, to, _) in enumerate(plan(refs[:nb])):
            pltpu.make_async_remote_copy(src_ref=src, dst_ref=dst, send_sem=send_sems.at[k],
                                         recv_sem=recv_sems.at[k], device_id=to, device_id_type=MESH).start()
        token[...] = jnp.zeros_like(token)

    outs = pl.pallas_call(
        body, name=name,
        out_shape=(pltpu.SemaphoreType.DMA((ncopies,)), pltpu.SemaphoreType.DMA((ncopies,)),
                   *[pltpu.HBM(b.shape, b.dtype) for b in bufs],
                   jax.ShapeDtypeStruct((SUBLANES, LANES), F32)),
        in_specs=[HBM] * nb + [ANY],
        out_specs=(SEM, SEM, *[HBM] * nb, pl.BlockSpec(memory_space=pltpu.VMEM)),
        input_output_aliases={i: 2 + i for i in range(nb)},
        compiler_params=pltpu.CompilerParams(has_side_effects=EFFECT),
    )(*[_in_hbm(b) for b in bufs], after)
    return outs[0], outs[1], list(outs[2:2 + nb]), outs[-1]


def _split_wait(name, started, plan, after):
    send_sems, recv_sems, bufs, _ = started
    nb = len(bufs)

    def body(*refs):
        s_sems, r_sems = refs[nb], refs[nb + 1]
        for k, (src, _, to, landed) in enumerate(plan(refs[:nb])):
            cp = pltpu.make_async_remote_copy(src_ref=src, dst_ref=landed, send_sem=s_sems.at[k],
                                              recv_sem=r_sems.at[k], device_id=to, device_id_type=MESH)
            cp.wait_send()
            cp.wait_recv()

    outs = pl.pallas_call(
        body, name=name, out_shape=tuple(pltpu.HBM(b.shape, b.dtype) for b in bufs),
        in_specs=[HBM] * nb + [SEM, SEM, ANY], out_specs=[HBM] * nb,
        input_output_aliases={i: i for i in range(nb)},
        compiler_params=pltpu.CompilerParams(has_side_effects=EFFECT),
    )(*bufs, send_sems, recv_sems, after)
    return list(outs)


def _forward_plan(kinds, nloc):
    def plan(lands):
        x, y, c = _place()
        out = []
        for a, land in enumerate(lands):
            for chip in [(1 - x, y), (x, 1 - y), (1 - x, 1 - y)]:
                mine = _slab(land, kinds[a], _index((*chip, c)), nloc[a])
                out.append((mine, mine, (x, y, 1 - c), _slab(land, kinds[a], _index((*chip, 1 - c)), nloc[a])))
        return out
    return plan


def _own_copy(name, shard, land, kind, me):
    m, n = shard.shape
    tr = _tile(m, max(SUBLANES, 1048576 // n), SUBLANES)

    def body(s_ref, x_ref, land_ref, o_ref):
        o_ref[...] = x_ref[...]

    if kind == "blk":
        o_spec = pl.BlockSpec((None, tr, n), lambda i, s: (s[0], i, 0))
    else:
        o_spec = pl.BlockSpec((tr, n), lambda i, s: (i, s[0]))
    return pl.pallas_call(
        body, name=name,
        grid_spec=pltpu.PrefetchScalarGridSpec(
            num_scalar_prefetch=1, grid=(m // tr,),
            in_specs=[pl.BlockSpec((tr, n), lambda i, s: (i, 0)), ANY], out_specs=o_spec),
        out_shape=jax.ShapeDtypeStruct(land.shape, land.dtype),
        input_output_aliases={2: 0}, compiler_params=_cp("parallel"))(me, shard, land)


def _pair_plan(kinds, nloc):
    na = len(kinds)

    def plan(refs):
        x, y, c = _place()
        out = []
        for a in range(na):
            for j in range(NCHIPS):
                dst = refs[na + a].at[j]
                out.append((_slab(refs[a], kinds[a], 2 * j + (1 - c), nloc[a]), dst, (x, y, 1 - c), dst))
        return out
    return plan


def _chip_sum(name, full, kind, n, from_sib, place):
    _, m, _ = from_sib.shape
    tr = _tile(m, max(SUBLANES, 262144 // n), SUBLANES)

    def body(s_ref, mine_ref, sib_ref, csum_ref, land_ref):
        v = (mine_ref[...].astype(F32) + sib_ref[...].astype(F32)).astype(csum_ref.dtype)
        csum_ref[...] = v

        @pl.when(pl.program_id(1) == s_ref[1])
        def _():
            land_ref[...] = v

    if kind == "blk":
        mine_spec = pl.BlockSpec((None, tr, n), lambda i, j, s: (2 * j + s[0], i, 0))
    else:
        mine_spec = pl.BlockSpec((tr, n), lambda i, j, s: (i, 2 * j + s[0]))
    slot = pl.BlockSpec((None, tr, n), lambda i, j, s: (j, i, 0))
    shp = jax.ShapeDtypeStruct((NCHIPS, m, n), from_sib.dtype)
    return pl.pallas_call(
        body, name=name,
        grid_spec=pltpu.PrefetchScalarGridSpec(
            num_scalar_prefetch=1, grid=(m // tr, NCHIPS), in_specs=[mine_spec, slot],
            out_specs=[slot, pl.BlockSpec((None, tr, n), lambda i, j, s: (s[1], i, 0))]),
        out_shape=[shp, shp], compiler_params=_cp("parallel", "arbitrary"))(place, full, from_sib)


def _other_chips():
    x, y, c = _place()
    return [(1 - x, y, c), (x, 1 - y, c), (1 - x, 1 - y, c)]


def _scatter_start(name, csums, lands, after):
    na = len(csums)

    def body(*refs):
        c_refs, land_refs = refs[:na], refs[na:2 * na]
        send_sems, recv_sems = refs[2 * na + 1], refs[2 * na + 2]
        token = refs[-1]
        x, y, _ = _place()
        for a in range(na):
            for k, to in enumerate(_other_chips()):
                pltpu.make_async_remote_copy(
                    src_ref=c_refs[a].at[2 * to[0] + to[1]], dst_ref=land_refs[a].at[2 * x + y],
                    send_sem=send_sems.at[3 * a + k], recv_sem=recv_sems.at[3 * a + k],
                    device_id=to, device_id_type=MESH).start()
        token[...] = jnp.zeros_like(token)

    outs = pl.pallas_call(
        body, name=name,
        out_shape=(pltpu.SemaphoreType.DMA((3 * na,)), pltpu.SemaphoreType.DMA((3 * na,)),
                   *[pltpu.HBM(s.shape, s.dtype) for s in csums],
                   *[pltpu.HBM(l.shape, l.dtype) for l in lands],
                   jax.ShapeDtypeStruct((SUBLANES, LANES), F32)),
        in_specs=[HBM] * (2 * na) + [ANY],
        out_specs=(SEM, SEM, *[HBM] * (2 * na), pl.BlockSpec(memory_space=pltpu.VMEM)),
        input_output_aliases={i: 2 + i for i in range(2 * na)},
        compiler_params=pltpu.CompilerParams(has_side_effects=EFFECT),
    )(*[_in_hbm(s) for s in csums], *[_in_hbm(l) for l in lands], after)
    return outs[0], outs[1], outs[2:2 + na], outs[2 + na:2 + 2 * na], outs[-1]


def _scatter_wait(name, started, after):
    send_sems, recv_sems, csums, lands, _ = started
    na = len(csums)

    def body(*refs):
        c_refs, land_refs = refs[:na], refs[na:2 * na]
        s_sems, r_sems = refs[2 * na], refs[2 * na + 1]
        for a in range(na):
            for k, frm in enumerate(_other_chips()):
                cp = pltpu.make_async_remote_copy(
                    src_ref=c_refs[a].at[2 * frm[0] + frm[1]], dst_ref=land_refs[a].at[2 * frm[0] + frm[1]],
                    send_sem=s_sems.at[3 * a + k], recv_sem=r_sems.at[3 * a + k],
                    device_id=frm, device_id_type=MESH)
                cp.wait_send()
                cp.wait_recv()

    outs = pl.pallas_call(
        body, name=name,
        out_shape=(*[pltpu.HBM(s.shape, s.dtype) for s in csums],
                   *[pltpu.HBM(l.shape, l.dtype) for l in lands]),
        in_specs=[HBM] * (2 * na) + [SEM, SEM, ANY], out_specs=[HBM] * (2 * na),
        input_output_aliases={i: i for i in range(2 * na)},
        compiler_params=pltpu.CompilerParams(has_side_effects=EFFECT),
    )(*csums, *lands, send_sems, recv_sems, after)
    return outs[na:]


def _adam_math(w, g, m, v):
    m = ADAM_B1 * m + (1.0 - ADAM_B1) * g
    v = ADAM_B2 * v + (1.0 - ADAM_B2) * (g * g)
    m_hat = m / (1.0 - ADAM_B1 ** ADAM_STEP)
    v_hat = v / (1.0 - ADAM_B2 ** ADAM_STEP)
    delta = -ADAM_LR * (m_hat / (jnp.sqrt(v_hat) + ADAM_EPS) + ADAM_WD * w)
    return delta, m, v


def _sum_adamw(name, parts, w, m, v, layer, prev=None):
    nl, r, c = w.shape
    nparts = parts.shape[0]
    tr = _tile(r, max(SUBLANES, 262144 // c), SUBLANES)

    def body(p_ref, w_ref, m_ref, v_ref, *rest):
        g_ref, d_ref, mo_ref, vo_ref = rest[-4:]
        g = p_ref[0].astype(F32)
        for s in range(1, nparts):
            g = g + p_ref[s].astype(F32)
        delta, mn, vn = _adam_math(w_ref[...], g, m_ref[...], v_ref[...])
        g_ref[...] = g
        d_ref[...] = delta
        mo_ref[...] = mn
        vo_ref[...] = vn

    row = pl.BlockSpec((None, tr, c), lambda i: (layer, i, 0))
    shp = jax.ShapeDtypeStruct((nl, r, c), F32)
    kept = [] if prev is None else list(prev)
    return pl.pallas_call(
        body, name=name, grid=(r // tr,),
        in_specs=[pl.BlockSpec((nparts, tr, c), lambda i: (0, i, 0)), row, row, row] + [ANY] * len(kept),
        out_specs=[row, row, row, row], out_shape=[shp, shp, shp, shp],
        input_output_aliases={4 + i: i for i in range(len(kept))},
        compiler_params=_cp("parallel"))(parts, w, m, v, *kept)


def _sum_parts(name, parts):
    _, r, c = parts.shape

    def body(p_ref, o_ref):
        g = p_ref[0]
        for s in range(1, NDEV):
            g = g + p_ref[s]
        o_ref[...] = g

    return pl.pallas_call(
        body, name=name, grid=(1,),
        in_specs=[pl.BlockSpec((NDEV, r, c), lambda i: (0, 0, 0))],
        out_specs=pl.BlockSpec((r, c), lambda i: (0, 0)),
        out_shape=jax.ShapeDtypeStruct((r, c), F32), compiler_params=_cp("arbitrary"))(parts)


def _adamw(name, w, g, m, v):
    r, c = w.shape

    def body(w_ref, g_ref, m_ref, v_ref, d_ref, mo_ref, vo_ref):
        delta, mn, vn = _adam_math(w_ref[...], g_ref[...], m_ref[...], v_ref[...])
        d_ref[...] = delta
        mo_ref[...] = mn
        vo_ref[...] = vn

    full = pl.BlockSpec((r, c), lambda i: (0, 0))
    shp = jax.ShapeDtypeStruct((r, c), F32)
    return pl.pallas_call(
        body, name=name, grid=(1,), in_specs=[full] * 4, out_specs=[full] * 3,
        out_shape=[shp] * 3, compiler_params=_cp("arbitrary"))(w, g, m, v)


def _pack(arrays):
    flat = jnp.concatenate([a.reshape(-1) for a in arrays])
    unit = SUBLANES * LANES
    pad = (-flat.shape[0]) % unit
    return jnp.pad(flat, (0, pad)).reshape(-1, LANES)


def _unpack(buf, shapes):
    flat = buf.reshape(-1)
    out, off = [], 0
    for shp in shapes:
        size = 1
        for s in shp:
            size *= s
        out.append(flat[off:off + size].reshape(shp))
        off += size
    return out


WEIGHTS = ["mix_norm_e", "w_in_e", "conv_w_e", "conv_b_e", "ln_g_e", "ln_b_e", "w_pool_e",
           "pool_scale_e", "w_out_e", "mix_norm_o", "w_in_o", "conv_w_o", "w_out_o", "ffn_norm",
           "w_gate", "w_up", "w_down", "final_norm"]
BIG = ["w_in_e", "w_out_e", "w_in_o", "w_out_o", "w_gate", "w_up", "w_down"]
SHARDED_SMALL = {"conv_w_e": 1, "w_pool_e": 1, "mix_norm_o": 0, "conv_w_o": 1}
SMALL = [n for n in WEIGHTS if n not in BIG]


def kernel(x, mix_norm_e, w_in_e, conv_w_e, conv_b_e, ln_g_e, ln_b_e, w_pool_e, pool_scale_e, w_out_e, mix_norm_o, w_in_o, conv_w_o, w_out_o, ffn_norm, w_gate, w_up, w_down, final_norm, loss_target, m_mix_norm_e, m_w_in_e, m_conv_w_e, m_conv_b_e, m_ln_g_e, m_ln_b_e, m_w_pool_e, m_pool_scale_e, m_w_out_e, m_mix_norm_o, m_w_in_o, m_conv_w_o, m_w_out_o, m_ffn_norm, m_w_gate, m_w_up, m_w_down, m_final_norm, v_mix_norm_e, v_w_in_e, v_conv_w_e, v_conv_b_e, v_ln_g_e, v_ln_b_e, v_w_pool_e, v_pool_scale_e, v_w_out_e, v_mix_norm_o, v_w_in_o, v_conv_w_o, v_w_out_o, v_ffn_norm, v_w_gate, v_w_up, v_w_down, v_final_norm):
    wts = dict(zip(WEIGHTS, [mix_norm_e, w_in_e, conv_w_e, conv_b_e, ln_g_e, ln_b_e, w_pool_e, pool_scale_e, w_out_e, mix_norm_o, w_in_o, conv_w_o, w_out_o, ffn_norm, w_gate, w_up, w_down, final_norm]))
    mom = dict(zip(WEIGHTS, [m_mix_norm_e, m_w_in_e, m_conv_w_e, m_conv_b_e, m_ln_g_e, m_ln_b_e, m_w_pool_e, m_pool_scale_e, m_w_out_e, m_mix_norm_o, m_w_in_o, m_conv_w_o, m_w_out_o, m_ffn_norm, m_w_gate, m_w_up, m_w_down, m_final_norm]))
    var = dict(zip(WEIGHTS, [v_mix_norm_e, v_w_in_e, v_conv_w_e, v_conv_b_e, v_ln_g_e, v_ln_b_e, v_w_pool_e, v_pool_scale_e, v_w_out_e, v_mix_norm_o, v_w_in_o, v_conv_w_o, v_w_out_o, v_ffn_norm, v_w_gate, v_w_up, v_w_down, v_final_norm]))
    bsz, seq, d = x.shape
    t = bsz * seq
    me = _index(_place())
    me_arr = jnp.reshape(me, (1,)).astype(jnp.int32)

    sh_names = list(SHARDED_SMALL)
    sh_local = [wts[n][0] for n in sh_names]
    packed = _pack(sh_local)
    gathered, = _all_gather("gather_small", [packed], ["blk"])
    small = {n: wts[n][0] for n in SMALL if n not in SHARDED_SMALL and n not in ("ffn_norm", "final_norm")}
    small["ffn_norm"], small["final_norm"] = ffn_norm, final_norm
    per_dev = [_unpack(gathered[s], [a.shape for a in sh_local]) for s in range(NDEV)]
    for i, n in enumerate(sh_names):
        small[n] = jnp.concatenate([per_dev[s][i] for s in range(NDEV)], axis=SHARDED_SMALL[n])

    for state in (wts, mom, var):
        for n in ("w_gate", "w_up"):
            state[n] = jnp.swapaxes(state[n], 1, 2)
    bf = lambda a: a.astype(BF16)
    mix_kinds, ffn_kinds = ["col", "blk"], ["blk", "blk", "blk"]
    ffn_names = ("w_gate", "w_up", "w_down")
    groups = {
        "mix_e": ([bf(w_in_e[0]), bf(w_out_e[0])], mix_kinds, [("w_in_e", 0), ("w_out_e", 0)]),
        "ffn0": ([bf(wts[n][0]) for n in ffn_names], ffn_kinds, [(n, 0) for n in ffn_names]),
        "mix_o": ([bf(w_in_o[0]), bf(w_out_o[0])], mix_kinds, [("w_in_o", 0), ("w_out_o", 0)]),
        "ffn1": ([bf(wts[n][1]) for n in ffn_names], ffn_kinds, [(n, 1) for n in ffn_names]),
    }
    started, prev = {}, gathered
    for grp, (shards, kinds, _) in groups.items():
        started[grp] = _gather_start("gather_start_" + grp, shards, kinds, prev)
        prev = started[grp][4]
    all_started = prev[0, 0:1]

    fwd_order = list(groups)
    passing, shards_of = {}, {}

    def pass_on(grp, after):
        shards, kinds, _ = groups[grp]
        shards_of[grp], lands = _gather_wait("gather_wait_" + grp, started[grp], kinds, after)
        plan = _forward_plan(kinds, [s.shape[1] for s in shards])
        passing[grp] = (_split_start("forward_start_" + grp, lands, 3 * len(lands), plan, after), plan)
        return passing[grp][0][3]

    def get_w(grp, after):
        if grp not in passing:
            after = pass_on(grp, after)
        st, plan = passing[grp]
        lands = _split_wait("forward_wait_" + grp, st, plan, after)
        full = [_own_copy("own_copy_%s%d" % (grp, a), shards_of[grp][a], lands[a], groups[grp][1][a], me_arr)
                for a in range(len(lands))]
        if len(full) == 2:
            return {"w_in": full[0], "w_out": full[1].reshape(-1, d)}
        return {"w_gate": full[0], "w_up": full[1], "w_down": full[2]}

    cx, cy, cc = _place()
    place = jnp.stack([cc, 2 * cx + cy]).astype(jnp.int32)
    bwd_order = ["ffn1", "mix_o", "ffn0", "mix_e"]
    pairing, pending, results = {}, {}, {}

    def put_g(grp, grads):
        shards, kinds, _ = groups[grp]
        nloc = [s.shape[1] for s in shards]
        if len(shards) == 2:
            fulls = [grads["w_in"], grads["w_out"].reshape(NDEV, -1, d)]
        else:
            fulls = [grads["w_gate"], grads["w_up"], grads["w_down"]]
        empties = []
        for g, kind, n in zip(fulls, kinds, nloc):
            empties.append(lax.empty((NCHIPS, g.shape[1] if kind == "blk" else g.shape[0], n), g.dtype))
        plan = _pair_plan(kinds, nloc)
        pairing[grp] = (_split_start("pair_start_" + grp, fulls + empties, NCHIPS * len(fulls), plan, place),
                        plan, kinds, nloc)
        token = pairing[grp][0][3]
        return send_sums(grp, token) if grp == bwd_order[-1] else token

    def send_sums(grp, after):
        st, plan, kinds, nloc = pairing[grp]
        bufs = _split_wait("pair_wait_" + grp, st, plan, after)
        na = len(kinds)
        sums = [_chip_sum("chip_sum_%s%d" % (grp, a), bufs[a], kinds[a], nloc[a], bufs[na + a], place)
                for a in range(na)]
        pending[grp] = _scatter_start("scatter_start_" + grp, [s[0] for s in sums], [s[1] for s in sums], after)
        return pending[grp][4]

    def finish(grp, after):
        lands = _scatter_wait("scatter_wait_" + grp, pending[grp], after)
        for (n, l), parts in zip(groups[grp][2], lands):
            results[n] = _sum_adamw("adamw_%s%d" % (n, l), parts, wts[n], mom[n], var[n], l, results.get(n))
        return results[groups[grp][2][-1][0]][1]

    def sync(tag, after):
        if tag == "fwd_mix_e":
            return pass_on("ffn0", after)
        if tag == "fwd_ffn0":
            return pass_on("mix_o", after)
        if tag == "fwd_mix_o":
            return pass_on("ffn1", after)
        if tag == "bwd_mix_o":
            return send_sums("ffn1", after)
        if tag == "bwd_ffn0":
            return finish("ffn1", send_sums("mix_o", after))
        if tag == "bwd_mix_e":
            return finish("mix_o", send_sums("ffn0", after))
        return None

    small["mix_norm_e"] = small["mix_norm_e"] + all_started
    lsum, dx, gsmall = _local_step(x.reshape(t, d), loss_target.reshape(t, d), seq, small, get_w, put_g, sync)
    loss = lax.psum(jnp.sum(lsum), MESH_AXES)
    finish("ffn0", dx)

    out_g, out_d, out_m, out_v = {}, {}, {}, {}

    gs_list = [gsmall[n] for n in SMALL]
    gs_all, = _all_gather("gather_small_grads", [_pack(gs_list)], ["blk"])
    gs_sum = _unpack(_sum_parts("sum_small_grads", gs_all), [a.shape for a in gs_list])
    local_g = []
    for n, g in zip(SMALL, gs_sum):
        if n in SHARDED_SMALL:
            ax = SHARDED_SMALL[n]
            size = wts[n].shape[ax + 1]
            g = lax.dynamic_slice_in_dim(g, me * size, size, axis=ax)
        local_g.append(g.reshape(wts[n].shape))
    shapes = [wts[n].shape for n in SMALL]
    upd = _adamw("adamw_small", _pack([wts[n] for n in SMALL]), _pack(local_g),
                 _pack([mom[n] for n in SMALL]), _pack([var[n] for n in SMALL]))
    for i, outd in enumerate((out_d, out_m, out_v)):
        for n, a in zip(SMALL, _unpack(upd[i], shapes)):
            outd[n] = a
    for n, g in zip(SMALL, local_g):
        out_g[n] = g

    finish("mix_e", upd[0])
    for n in BIG:
        res = [jnp.swapaxes(a, 1, 2) for a in results[n]] if n in ("w_gate", "w_up") else results[n]
        out_g[n], out_d[n], out_m[n], out_v[n] = res

    return (loss, dx.reshape(bsz, seq, d), *[out_g[n] for n in WEIGHTS], *[out_d[n] for n in WEIGHTS],
            *[out_m[n] for n in WEIGHTS], *[out_v[n] for n in WEIGHTS])
```

```python
import functools

import jax
import jax.numpy as jnp
from jax import lax
from jax.experimental import pallas as pl
from jax.experimental.pallas import tpu as pltpu

F32 = jnp.float32
BF16 = jnp.bfloat16
NDEV = 8
MESH_AXES = ("x", "y", "c")
EPS = 1e-6
POOL_WINDOWS = (2, 4, 8, 16)
CONV_WIDTH = 31
SHORT_WIDTH = 3
ADAM_LR = 0.001
ADAM_B1 = 0.9
ADAM_B2 = 0.999
ADAM_EPS = 1e-08
ADAM_WD = 0.01
ADAM_STEP = 10
LANES = 128
SUBLANES = 8
VMEM_LIMIT = 56 * 1024 * 1024
MM_TK = 2048
MESH = pl.DeviceIdType.MESH
ANY = pl.BlockSpec(memory_space=pl.ANY)


def _cp(*sem):
    return pltpu.CompilerParams(dimension_semantics=sem, vmem_limit_bytes=VMEM_LIMIT)


def _tile(n, pref, unit=LANES):
    if n <= pref:
        return n
    t = (pref // unit) * unit
    while t > unit and n % t:
        t -= unit
    assert n % t == 0, (n, pref)
    return t


def _sigmoid(v):
    return 1.0 / (1.0 + jnp.exp(-v))


def _mm(name, pairs, a_specs, b_specs, dims, out_shape, o_spec, grid, acc_shape,
        res=None, res_spec=None, dep=None):
    np_ = len(pairs)
    nk = grid[2]
    has_res = res is not None
    n_in = 2 * np_ + (1 if has_res else 0) + (0 if dep is None else 1)

    def body(*refs):
        a_refs = refs[:np_]
        b_refs = refs[np_:2 * np_]
        r_ref = refs[2 * np_] if has_res else None
        o_ref = refs[n_in]
        acc = refs[-1]

        def part():
            s = None
            for a_ref, b_ref in zip(a_refs, b_refs):
                blocks = [(a_ref[...], b_ref[...])] if len(a_ref.shape) == 2 else [
                    (a_ref[q], b_ref[q]) for q in range(a_ref.shape[0])]
                for av, bv in blocks:
                    d = lax.dot_general(av, bv, dims, preferred_element_type=F32)
                    s = d if s is None else s + d
            return s

        def finish(v):
            if has_res:
                v = v + r_ref[...]
            o_ref[...] = v.astype(o_ref.dtype)

        if nk == 1:
            finish(part())
        else:
            k = pl.program_id(2)

            @pl.when(k == 0)
            def _():
                acc[...] = part()

            @pl.when((k > 0) & (k < nk - 1))
            def _():
                acc[...] += part()

            @pl.when(k == nk - 1)
            def _():
                finish(acc[...] + part())

    ins = [p[0] for p in pairs] + [p[1] for p in pairs]
    specs = list(a_specs) + list(b_specs)
    if has_res:
        ins.append(res)
        specs.append(res_spec)
    if dep is not None:
        ins.append(dep)
        specs.append(ANY)
    return pl.pallas_call(
        body, name=name, grid=grid, in_specs=specs, out_specs=o_spec, out_shape=out_shape,
        scratch_shapes=[pltpu.VMEM(acc_shape if nk > 1 else (SUBLANES, LANES), F32)],
        compiler_params=_cp("parallel", "parallel", "arbitrary"))(*ins)


NN = (((1,), (0,)), ((), ()))
NT = (((1,), (1,)), ((), ()))
TN = (((0,), (0,)), ((), ()))


def _mm_nn(name, a, b, out_dtype, res=None, dep=None):
    m, kk = a.shape
    n = b.shape[1]
    tm, tn, tk = _tile(m, 1024), _tile(n, 1024), _tile(kk, MM_TK)
    return _mm(name, [(a, b)],
               [pl.BlockSpec((tm, tk), lambda i, j, k: (i, k))],
               [pl.BlockSpec((tk, tn), lambda i, j, k: (k, j))], NN,
               jax.ShapeDtypeStruct((m, n), out_dtype),
               pl.BlockSpec((tm, tn), lambda i, j, k: (i, j)),
               (m // tm, n // tn, kk // tk), (tm, tn), res,
               pl.BlockSpec((tm, tn), lambda i, j, k: (i, j)), dep=dep)


def _mm_nt(name, a, b, out_dtype, dep=None):
    m, n = a.shape
    kk = b.shape[0]
    tm, tn, tk = _tile(m, 1024), _tile(kk, 1024), _tile(n, MM_TK)
    return _mm(name, [(a, b)],
               [pl.BlockSpec((tm, tk), lambda i, j, k: (i, k))],
               [pl.BlockSpec((tn, tk), lambda i, j, k: (j, k))], NT,
               jax.ShapeDtypeStruct((m, kk), out_dtype),
               pl.BlockSpec((tm, tn), lambda i, j, k: (i, j)),
               (m // tm, kk // tn, n // tk), (tm, tn), dep=dep)


def _mm_tn(name, a, b, out_dtype, dep=None):
    t, m = a.shape
    n = b.shape[1]
    tm, tn, tk = _tile(m, 1024), _tile(n, 1024), _tile(t, MM_TK)
    return _mm(name, [(a, b)],
               [pl.BlockSpec((tk, tm), lambda i, j, k: (k, i))],
               [pl.BlockSpec((tk, tn), lambda i, j, k: (k, j))], TN,
               jax.ShapeDtypeStruct((m, n), out_dtype),
               pl.BlockSpec((tm, tn), lambda i, j, k: (i, j)),
               (m // tm, n // tn, t // tk), (tm, tn), dep=dep)


def _mm_down(name, act, wd, res, dep=None):
    nb, t, f = act.shape
    d = wd.shape[2]
    tm, tn = _tile(t, 1024), _tile(d, 1024)
    kb = 2 if nb % 2 == 0 else 1
    return _mm(name, [(act, wd)],
               [pl.BlockSpec((kb, tm, f), lambda i, j, k: (k, i, 0))],
               [pl.BlockSpec((kb, f, tn), lambda i, j, k: (k, 0, j))], NN,
               jax.ShapeDtypeStruct((t, d), F32),
               pl.BlockSpec((tm, tn), lambda i, j, k: (i, j)),
               (t // tm, d // tn, nb // kb), (tm, tn), res,
               pl.BlockSpec((tm, tn), lambda i, j, k: (i, j)), dep=dep)


def _mm_ffn_dn(name, dg, wg, dup, wu):
    nb, t, f = dg.shape
    d = wg.shape[2]
    tm, tn = _tile(t, 1024), _tile(d, 1024)
    kb = 2 if nb % 2 == 0 else 1
    a_spec = pl.BlockSpec((kb, tm, f), lambda i, j, k: (k, i, 0))
    b_spec = pl.BlockSpec((kb, f, tn), lambda i, j, k: (k, 0, j))
    return _mm(name, [(dg, wg), (dup, wu)], [a_spec, a_spec], [b_spec, b_spec], NN,
               jax.ShapeDtypeStruct((t, d), BF16),
               pl.BlockSpec((tm, tn), lambda i, j, k: (i, j)),
               (t // tm, d // tn, nb // kb), (tm, tn))


def _mm_dwd(name, act, dh, dep=None):
    nb, t, f = act.shape
    d = dh.shape[1]
    tn, tk = _tile(d, 1024), _tile(t, MM_TK)
    return _mm(name, [(act, dh)],
               [pl.BlockSpec((None, tk, f), lambda i, j, k: (i, k, 0))],
               [pl.BlockSpec((tk, tn), lambda i, j, k: (k, j))], TN,
               jax.ShapeDtypeStruct((nb, f, d), BF16),
               pl.BlockSpec((None, f, tn), lambda i, j, k: (i, 0, j)),
               (nb, d // tn, t // tk), (f, tn), dep=dep)


def _ffn_fwd(name, n, wg, wu):
    nb, f, d = wg.shape
    t = n.shape[0]
    tm = _tile(t, 512)

    def body(n_ref, wg_ref, wu_ref, g_ref, up_ref, act_ref):
        nv = n_ref[...]
        g = lax.dot_general(nv, wg_ref[...], NT, preferred_element_type=F32)
        up = lax.dot_general(nv, wu_ref[...], NT, preferred_element_type=F32)
        g_ref[...] = g.astype(BF16)
        up_ref[...] = up.astype(BF16)
        act_ref[...] = (g * _sigmoid(g) * up).astype(BF16)

    w_spec = pl.BlockSpec((None, f, d), lambda j, i: (j, 0, 0))
    o_spec = pl.BlockSpec((None, tm, f), lambda j, i: (j, i, 0))
    shp = jax.ShapeDtypeStruct((nb, t, f), BF16)
    return pl.pallas_call(
        body, name=name, grid=(nb, t // tm),
        in_specs=[pl.BlockSpec((tm, d), lambda j, i: (i, 0)), w_spec, w_spec],
        out_specs=[o_spec, o_spec, o_spec], out_shape=[shp, shp, shp],
        compiler_params=_cp("parallel", "parallel"))(n, wg, wu)


def _ffn_bwd_act(name, dh, wd, g, up, dep=None):
    nb, f, d = wd.shape
    t = dh.shape[0]
    tm = _tile(t, 512)

    def body(dh_ref, wd_ref, g_ref, up_ref, *rest):
        dg_ref, dup_ref = rest[-2:]
        da = lax.dot_general(dh_ref[...], wd_ref[...], NT, preferred_element_type=F32)
        gv = g_ref[...].astype(F32)
        uv = up_ref[...].astype(F32)
        sg = _sigmoid(gv)
        dg_ref[...] = (da * uv * (sg * (1.0 + gv * (1.0 - sg)))).astype(BF16)
        dup_ref[...] = (da * gv * sg).astype(BF16)

    o_spec = pl.BlockSpec((None, tm, f), lambda j, i: (j, i, 0))
    shp = jax.ShapeDtypeStruct((nb, t, f), BF16)
    return pl.pallas_call(
        body, name=name, grid=(nb, t // tm),
        in_specs=[pl.BlockSpec((tm, d), lambda j, i: (i, 0)),
                  pl.BlockSpec((None, f, d), lambda j, i: (j, 0, 0)), o_spec, o_spec]
        + ([] if dep is None else [ANY]),
        out_specs=[o_spec, o_spec], out_shape=[shp, shp],
        compiler_params=_cp("parallel", "parallel"))(dh, wd, g, up, *([] if dep is None else [dep]))


def _rms_fwd(name, h, gain):
    t, d = h.shape
    tr = _tile(t, 512, SUBLANES)

    def body(h_ref, g_ref, n_ref):
        hv = h_ref[...]
        r = lax.rsqrt(jnp.mean(hv * hv, axis=-1, keepdims=True) + EPS)
        n_ref[...] = (hv * r * g_ref[...]).astype(BF16)

    return pl.pallas_call(
        body, name=name, grid=(t // tr,),
        in_specs=[pl.BlockSpec((tr, d), lambda i: (i, 0)), pl.BlockSpec((1, d), lambda i: (0, 0))],
        out_specs=pl.BlockSpec((tr, d), lambda i: (i, 0)),
        out_shape=jax.ShapeDtypeStruct((t, d), BF16),
        compiler_params=_cp("parallel"))(h, gain)


def _rms_bwd_math(hv, gain, dn):
    d = hv.shape[-1]
    r = lax.rsqrt(jnp.mean(hv * hv, axis=-1, keepdims=True) + EPS)
    xhat = hv * r
    dxh = dn * gain
    dh = r * (dxh - xhat * (jnp.sum(dxh * xhat, axis=-1, keepdims=True) / d))
    dgain = jnp.sum(dn * xhat, axis=0, keepdims=True)
    return dh, dgain


def _rms_bwd(name, h, gain, dn, dres):
    t, d = h.shape
    tr = _tile(t, 256, SUBLANES)

    def body(h_ref, g_ref, dn_ref, dr_ref, dh_ref, dhb_ref, dg_ref):
        dh, dgain = _rms_bwd_math(h_ref[...], g_ref[...], dn_ref[...].astype(F32))
        dh = dh + dr_ref[...]
        dh_ref[...] = dh
        dhb_ref[...] = dh.astype(BF16)

        @pl.when(pl.program_id(0) == 0)
        def _():
            dg_ref[...] = dgain

        @pl.when(pl.program_id(0) > 0)
        def _():
            dg_ref[...] += dgain

    row = pl.BlockSpec((tr, d), lambda i: (i, 0))
    vec = pl.BlockSpec((1, d), lambda i: (0, 0))
    return pl.pallas_call(
        body, name=name, grid=(t // tr,), in_specs=[row, vec, row, row],
        out_specs=[row, row, vec],
        out_shape=[jax.ShapeDtypeStruct((t, d), F32), jax.ShapeDtypeStruct((t, d), BF16),
                   jax.ShapeDtypeStruct((1, d), F32)],
        compiler_params=_cp("arbitrary"))(h, gain, dn, dres)


def _loss_head(name, h, gain, tgt):
    t, d = h.shape
    tr = _tile(t, 256, SUBLANES)

    def body(h_ref, g_ref, t_ref, dh_ref, dhb_ref, dg_ref, ls_ref):
        hv = h_ref[...]
        gv = g_ref[...]
        r = lax.rsqrt(jnp.mean(hv * hv, axis=-1, keepdims=True) + EPS)
        err = hv * r * gv - t_ref[...]
        lsum = 0.5 * jnp.sum(err * err, axis=0, keepdims=True) / d
        dh, dgain = _rms_bwd_math(hv, gv, err / d)
        dh_ref[...] = dh
        dhb_ref[...] = dh.astype(BF16)

        @pl.when(pl.program_id(0) == 0)
        def _():
            dg_ref[...] = dgain
            ls_ref[...] = lsum

        @pl.when(pl.program_id(0) > 0)
        def _():
            dg_ref[...] += dgain
            ls_ref[...] += lsum

    row = pl.BlockSpec((tr, d), lambda i: (i, 0))
    vec = pl.BlockSpec((1, d), lambda i: (0, 0))
    return pl.pallas_call(
        body, name=name, grid=(t // tr,), in_specs=[row, vec, row],
        out_specs=[row, row, vec, vec],
        out_shape=[jax.ShapeDtypeStruct((t, d), F32), jax.ShapeDtypeStruct((t, d), BF16),
                   jax.ShapeDtypeStruct((1, d), F32), jax.ShapeDtypeStruct((1, d), F32)],
        compiler_params=_cp("arbitrary"))(h, gain, tgt)


def _conv_geom(t, seq, c, k):
    halo = 32 if k - 1 > SUBLANES else SUBLANES
    assert k - 1 <= halo
    tm = min(256, seq // 2)
    tc = min(512, c)
    assert seq % tm == 0 and tm % halo == 0 and c % tc == 0 and t % seq == 0
    return halo, tm, tc, min(128, tm), min(LANES, tc)


def _pre(kind, a, b):
    if kind == "glu":
        return a * _sigmoid(b)
    if kind == "mul":
        return a * b
    return a


def _taps(k):
    return [(s % SUBLANES, s // SUBLANES, s) for s in range(k)]


def _conv_fwd(name, seq, c, w, x1, c1, x2=None, c2=0, pre=None, bias=None, post=None, cpost=0):
    t = x1.shape[0]
    k = w.shape[0]
    halo, tm, tc, sr, sl = _conv_geom(t, seq, c, k)
    nb, cps = tm // halo, seq // tm
    two = x2 is not None
    has_bias, has_post = bias is not None, post is not None

    def body(*refs):
        it = iter(refs)
        x1c, x1h = next(it), next(it)
        x2c, x2h = (next(it), next(it)) if two else (None, None)
        w_ref = next(it)
        b_ref = next(it) if has_bias else None
        p_ref = next(it) if has_post else None
        o_ref = next(it)
        y_ref = next(it) if has_post else None
        xs = next(it)
        first = (pl.program_id(1) % cps) == 0
        hv = _pre(pre, x1h[...].astype(F32), x2h[...].astype(F32) if two else None)
        xs[0:halo, :] = jnp.where(first, 0.0, hv)
        xs[halo:halo + tm, :] = _pre(pre, x1c[...].astype(F32), x2c[...].astype(F32) if two else None)
        for l0 in range(0, tc, sl):
            ls = slice(l0, l0 + sl)
            for r0 in range(0, tm, sr):
                win = xs[r0:r0 + sr + halo, ls]
                acc = jnp.zeros((sr, sl), F32)
                rolled = {}
                for r, q, s in _taps(k):
                    if r not in rolled:
                        rolled[r] = win if r == 0 else pltpu.roll(win, r, 0)
                    lo = halo - SUBLANES * q
                    acc = acc + w_ref[k - 1 - s:k - s, ls] * rolled[r][lo:lo + sr]
                if has_bias:
                    acc = acc + b_ref[:, ls]
                o_ref[r0:r0 + sr, ls] = acc.astype(o_ref.dtype)
                if has_post:
                    y_ref[r0:r0 + sr, ls] = (acc * p_ref[r0:r0 + sr, ls].astype(F32)).astype(y_ref.dtype)

    def cur(off):
        return pl.BlockSpec((tm, tc), lambda j, i: (i, off // tc + j))

    def prev(off):
        return pl.BlockSpec((halo, tc), lambda j, i: (jnp.maximum(i * nb - 1, 0), off // tc + j))

    ins, specs = [x1, x1], [cur(c1), prev(c1)]
    if two:
        ins += [x2, x2]
        specs += [cur(c2), prev(c2)]
    ins.append(w)
    specs.append(pl.BlockSpec((k, tc), lambda j, i: (0, j)))
    if has_bias:
        ins.append(bias)
        specs.append(pl.BlockSpec((1, tc), lambda j, i: (0, j)))
    if has_post:
        ins.append(post)
        specs.append(cur(cpost))
    o_spec = pl.BlockSpec((tm, tc), lambda j, i: (i, j))
    shp = jax.ShapeDtypeStruct((t, c), BF16)
    return pl.pallas_call(
        body, name=name, grid=(c // tc, t // tm), in_specs=specs,
        out_specs=[o_spec, o_spec] if has_post else o_spec,
        out_shape=[shp, shp] if has_post else shp,
        scratch_shapes=[pltpu.VMEM((halo + tm, tc), F32)],
        compiler_params=_cp("parallel", "parallel"))(*ins)


def _conv_bwd(name, seq, c, w, d1, cd1, d2=None, cd2=0, dpre=None,
              x1=None, c1=0, x2=None, c2=0, pre=None):
    t = d1.shape[0]
    k = w.shape[0]
    halo, tm, tc, sr, sl = _conv_geom(t, seq, c, k)
    nb, cps = tm // halo, seq // tm
    nchunks = t // tm
    dtwo, xtwo, has_x = d2 is not None, x2 is not None, x1 is not None

    def body(*refs):
        it = iter(refs)
        d1c, d1n = next(it), next(it)
        d2c, d2n = (next(it), next(it)) if dtwo else (None, None)
        x1c, x1h = (next(it), next(it)) if has_x else (None, None)
        x2c, x2h = (next(it), next(it)) if xtwo else (None, None)
        w_ref = next(it)
        dx_ref = next(it)
        dw_ref = next(it) if has_x else None
        ds = next(it)
        xs = next(it) if has_x else None
        i = pl.program_id(1)
        last = (i % cps) == cps - 1
        ds[0:tm, :] = _pre(dpre, d1c[...].astype(F32), d2c[...].astype(F32) if dtwo else None)
        nv = _pre(dpre, d1n[...].astype(F32), d2n[...].astype(F32) if dtwo else None)
        ds[tm:tm + halo, :] = jnp.where(last, 0.0, nv)
        if has_x:
            first = (i % cps) == 0
            hv = _pre(pre, x1h[...].astype(F32), x2h[...].astype(F32) if xtwo else None)
            xs[0:halo, :] = jnp.where(first, 0.0, hv)
            xs[halo:halo + tm, :] = _pre(pre, x1c[...].astype(F32), x2c[...].astype(F32) if xtwo else None)

            @pl.when(i == 0)
            def _():
                dw_ref[...] = jnp.zeros_like(dw_ref)

        for l0 in range(0, tc, sl):
            ls = slice(l0, l0 + sl)
            for r0 in range(0, tm, sr):
                win = ds[r0:r0 + sr + halo, ls]
                nrow = sr + halo
                acc = jnp.zeros((sr, sl), F32)
                rolled = {}
                for r, q, s in _taps(k):
                    if r not in rolled:
                        rolled[r] = win if r == 0 else pltpu.roll(win, nrow - r, 0)
                    lo = SUBLANES * q
                    acc = acc + w_ref[k - 1 - s:k - s, ls] * rolled[r][lo:lo + sr]
                dx_ref[r0:r0 + sr, ls] = acc.astype(dx_ref.dtype)
                if has_x:
                    dcur = win[0:sr]
                    xwin = xs[r0:r0 + sr + halo, ls]
                    xrolled = {}
                    for r, q, s in _taps(k):
                        if r not in xrolled:
                            xrolled[r] = xwin if r == 0 else pltpu.roll(xwin, r, 0)
                        lo = halo - SUBLANES * q
                        part = jnp.sum(dcur * xrolled[r][lo:lo + sr], axis=0, keepdims=True)
                        dw_ref[k - 1 - s:k - s, ls] += part

    def cur(off):
        return pl.BlockSpec((tm, tc), lambda j, i: (i, off // tc + j))

    def prev(off):
        return pl.BlockSpec((halo, tc), lambda j, i: (jnp.maximum(i * nb - 1, 0), off // tc + j))

    def nxt(off):
        return pl.BlockSpec((halo, tc),
                            lambda j, i: (jnp.minimum((i + 1) * nb, nchunks * nb - 1), off // tc + j))

    ins, specs = [d1, d1], [cur(cd1), nxt(cd1)]
    if dtwo:
        ins += [d2, d2]
        specs += [cur(cd2), nxt(cd2)]
    if has_x:
        ins += [x1, x1]
        specs += [cur(c1), prev(c1)]
    if xtwo:
        ins += [x2, x2]
        specs += [cur(c2), prev(c2)]
    ins.append(w)
    specs.append(pl.BlockSpec((k, tc), lambda j, i: (0, j)))
    o_specs = [pl.BlockSpec((tm, tc), lambda j, i: (i, j))]
    o_shapes = [jax.ShapeDtypeStruct((t, c), BF16)]
    scratch = [pltpu.VMEM((tm + halo, tc), F32)]
    if has_x:
        o_specs.append(pl.BlockSpec((k, tc), lambda j, i: (0, j)))
        o_shapes.append(jax.ShapeDtypeStruct((k, c), F32))
        scratch.append(pltpu.VMEM((halo + tm, tc), F32))
    out = pl.pallas_call(
        body, name=name, grid=(c // tc, t // tm), in_specs=specs, out_specs=o_specs,
        out_shape=o_shapes, scratch_shapes=scratch,
        compiler_params=_cp("parallel", "arbitrary"))(*ins)
    return out if has_x else out[0]


def _pool_taps(c):
    kmax = max(POOL_WINDOWS)
    grp = c // len(POOL_WINDOWS)
    cols = []
    for wdw in POOL_WINDOWS:
        col = jnp.concatenate([jnp.zeros((kmax - wdw,), F32), jnp.ones((wdw,), F32)])
        cols.append(jnp.tile(col[:, None], (1, grp)))
    return jnp.concatenate(cols, axis=1)


def _counts(i, tr, seq, grp):
    pos = (i * tr + lax.broadcasted_iota(jnp.int32, (tr, 1), 0)) % seq + 1
    return [1.0 / jnp.minimum(pos, wdw).astype(F32) for wdw in POOL_WINDOWS]


def _ln_stats(a2):
    mu = jnp.mean(a2, axis=-1, keepdims=True)
    xc = a2 - mu
    rstd = lax.rsqrt(jnp.mean(xc * xc, axis=-1, keepdims=True) + EPS)
    return xc * rstd, rstd


def _even_fwd(name, seq, a2, ws, u, ln_g, ln_b, w_pool, scale):
    t, c = a2.shape
    ng = len(POOL_WINDOWS)
    grp = c // ng
    tr = _tile(t, 256, SUBLANES)

    def body(a_ref, ws_ref, b_ref, g_ref, bb_ref, wp_ref, sc_ref, z_ref, pm_ref):
        xhat, _ = _ln_stats(a_ref[...].astype(F32))
        l = xhat * g_ref[...] + bb_ref[...]
        z_ref[:, 0:c] = (l * _sigmoid(l)).astype(BF16)
        inv = _counts(pl.program_id(0), tr, seq, grp)
        for g in range(ng):
            gs = slice(g * grp, (g + 1) * grp)
            pm = (ws_ref[:, gs].astype(F32) * inv[g] - b_ref[:, gs].astype(F32)).astype(BF16)
            pm_ref[:, gs] = pm
            q = jnp.dot(pm, wp_ref[g], preferred_element_type=F32)
            z_ref[:, c + g * grp:c + (g + 1) * grp] = (q * sc_ref[:, gs]).astype(BF16)

    row = pl.BlockSpec((tr, c), lambda i: (i, 0))
    vec = pl.BlockSpec((1, c), lambda i: (0, 0))
    return pl.pallas_call(
        body, name=name, grid=(t // tr,),
        in_specs=[row, row, pl.BlockSpec((tr, c), lambda i: (i, 2)), vec, vec,
                  pl.BlockSpec((ng, grp, grp), lambda i: (0, 0, 0)), vec],
        out_specs=[pl.BlockSpec((tr, 2 * c), lambda i: (i, 0)), row],
        out_shape=[jax.ShapeDtypeStruct((t, 2 * c), BF16), jax.ShapeDtypeStruct((t, c), BF16)],
        compiler_params=_cp("parallel"))(a2, ws, u, ln_g, ln_b, w_pool, scale)


def _even_bwd(name, seq, dz, a2, pm, ln_g, ln_b, w_pool, scale):
    t, c = a2.shape
    ng = len(POOL_WINDOWS)
    grp = c // ng
    tr = _tile(t, 256, SUBLANES)

    def body(dz_ref, a_ref, pm_ref, g_ref, bb_ref, wp_ref, sc_ref,
             da_ref, dws_ref, dpm_ref, vec_ref, dwp_ref):
        i = pl.program_id(0)

        @pl.when(i == 0)
        def _():
            vec_ref[...] = jnp.zeros_like(vec_ref)
            dwp_ref[...] = jnp.zeros_like(dwp_ref)

        xhat, rstd = _ln_stats(a_ref[...].astype(F32))
        gv = g_ref[...]
        l = xhat * gv + bb_ref[...]
        sg = _sigmoid(l)
        dl = dz_ref[:, 0:c].astype(F32) * (sg * (1.0 + l * (1.0 - sg)))
        dxh = dl * gv
        da2 = rstd * (dxh - jnp.mean(dxh, axis=-1, keepdims=True)
                      - xhat * jnp.mean(dxh * xhat, axis=-1, keepdims=True))
        da_ref[...] = da2.astype(BF16)
        vec_ref[0:1, :] += jnp.sum(dl * xhat, axis=0, keepdims=True)
        vec_ref[1:2, :] += jnp.sum(dl, axis=0, keepdims=True)
        vec_ref[2:3, :] += jnp.sum(da2, axis=0, keepdims=True)
        inv = _counts(i, tr, seq, grp)
        for g in range(ng):
            gs = slice(g * grp, (g + 1) * grp)
            pmv = pm_ref[:, gs]
            wp = wp_ref[g]
            dp = dz_ref[:, c + g * grp:c + (g + 1) * grp].astype(F32)
            q = jnp.dot(pmv, wp, preferred_element_type=F32)
            vec_ref[3:4, gs] += jnp.sum(dp * q, axis=0, keepdims=True)
            dq = (dp * sc_ref[:, gs]).astype(BF16)
            dpm = lax.dot_general(dq, wp, NT, preferred_element_type=F32)
            dwp_ref[g] += lax.dot_general(pmv, dq, TN, preferred_element_type=F32)
            dpm_ref[:, gs] = dpm.astype(BF16)
            dws_ref[:, gs] = (dpm * inv[g]).astype(BF16)

    row = pl.BlockSpec((tr, c), lambda i: (i, 0))
    vec = pl.BlockSpec((1, c), lambda i: (0, 0))
    rshape = jax.ShapeDtypeStruct((t, c), BF16)
    return pl.pallas_call(
        body, name=name, grid=(t // tr,),
        in_specs=[pl.BlockSpec((tr, 2 * c), lambda i: (i, 0)), row, row, vec, vec,
                  pl.BlockSpec((ng, grp, grp), lambda i: (0, 0, 0)), vec],
        out_specs=[row, row, row, pl.BlockSpec((SUBLANES, c), lambda i: (0, 0)),
                   pl.BlockSpec((ng, grp, grp), lambda i: (0, 0, 0))],
        out_shape=[rshape, rshape, rshape, jax.ShapeDtypeStruct((SUBLANES, c), F32),
                   jax.ShapeDtypeStruct((ng, grp, grp), F32)],
        compiler_params=_cp("arbitrary"))(dz, a2, pm, ln_g, ln_b, w_pool, scale)


def _even_du(name, u, da1, dbp, dpm):
    t, c = da1.shape
    tr = _tile(t, 256, SUBLANES)

    def body(u_ref, da_ref, dbp_ref, dpm_ref, du_ref):
        val = u_ref[:, 0:c].astype(F32)
        sg = _sigmoid(u_ref[:, c:2 * c].astype(F32))
        da = da_ref[...].astype(F32)
        du_ref[:, 0:c] = (da * sg).astype(BF16)
        du_ref[:, c:2 * c] = (da * val * sg * (1.0 - sg)).astype(BF16)
        du_ref[:, 2 * c:3 * c] = (dbp_ref[...].astype(F32) - dpm_ref[...].astype(F32)).astype(BF16)

    row = pl.BlockSpec((tr, c), lambda i: (i, 0))
    wide = pl.BlockSpec((tr, 3 * c), lambda i: (i, 0))
    return pl.pallas_call(
        body, name=name, grid=(t // tr,), in_specs=[wide, row, row, row], out_specs=wide,
        out_shape=jax.ShapeDtypeStruct((t, 3 * c), BF16),
        compiler_params=_cp("parallel"))(u, da1, dbp, dpm)


def _odd_du(name, u, dy, co, dxc):
    t, c = dy.shape
    tr = _tile(t, 256, SUBLANES)

    def body(u_ref, dy_ref, co_ref, dx_ref, du_ref):
        dx = dx_ref[...].astype(F32)
        du_ref[:, 0:c] = (dy_ref[...].astype(F32) * co_ref[...].astype(F32)).astype(BF16)
        du_ref[:, c:2 * c] = (dx * u_ref[:, 2 * c:3 * c].astype(F32)).astype(BF16)
        du_ref[:, 2 * c:3 * c] = (dx * u_ref[:, c:2 * c].astype(F32)).astype(BF16)

    row = pl.BlockSpec((tr, c), lambda i: (i, 0))
    wide = pl.BlockSpec((tr, 3 * c), lambda i: (i, 0))
    return pl.pallas_call(
        body, name=name, grid=(t // tr,), in_specs=[wide, row, row, row], out_specs=wide,
        out_shape=jax.ShapeDtypeStruct((t, 3 * c), BF16),
        compiler_params=_cp("parallel"))(u, dy, co, dxc)


def _local_step(x, tgt, seq, small, get_w, put_g, sync):
    t, d = x.shape
    c = d // 2
    cw_e, cw_o = small["conv_w_e"], small["conv_w_o"]
    wp = small["w_pool_e"].astype(BF16)
    ptaps = _pool_taps(c)
    row = lambda v: v.reshape(1, -1)

    we = {"w_in": get_w("in_e", x)[0]}
    n0 = _rms_fwd("rms_fwd_mix0", x, row(small["mix_norm_e"]))
    u0 = _mm_nn("mm_in_e", n0, we["w_in"], BF16)
    sync("fwd_a", u0)
    a2 = _conv_fwd("conv_e_fwd", seq, c, cw_e, u0, 0, u0, c, "glu", bias=row(small["conv_b_e"]))
    ws = _conv_fwd("pool_fwd", seq, c, ptaps, u0, 2 * c)
    z0, pm = _even_fwd("even_fwd", seq, a2, ws, u0, row(small["ln_g_e"]), row(small["ln_b_e"]),
                       wp, row(small["pool_scale_e"]))
    we["w_out"] = get_w("out_e", z0)[0]
    h1 = _mm_nn("mm_out_e", z0, we["w_out"], F32, res=x)
    sync("fwd_b", h1)
    n1 = _rms_fwd("rms_fwd_ffn0", h1, row(small["ffn_norm"][0]))
    wf0 = dict(zip(("w_gate", "w_up"), get_w("gu0", n1)))
    g0, up0, act0 = _ffn_fwd("ffn0_fwd", n1, wf0["w_gate"], wf0["w_up"])
    dep = sync("fwd_c", act0)
    wf0["w_down"] = get_w("down0", act0)[0]
    h2 = _mm_down("mm_down0", act0, wf0["w_down"], h1, dep=dep)
    dep = sync("fwd_d", h2)
    n2 = _rms_fwd("rms_fwd_mix1", h2, row(small["mix_norm_o"]))
    wo = {"w_in": get_w("in_o", n2)[0]}
    u1 = _mm_nn("mm_in_o", n2, wo["w_in"], BF16, dep=dep)
    co, y1 = _conv_fwd("conv_o_fwd", seq, d, cw_o, u1, d, u1, 2 * d, "mul", post=u1, cpost=0)
    dep = sync("fwd_e", y1)
    wo["w_out"] = get_w("out_o", y1)[0]
    h3 = _mm_nn("mm_out_o", y1, wo["w_out"], F32, res=h2, dep=dep)
    sync("fwd_f", h3)
    n3 = _rms_fwd("rms_fwd_ffn1", h3, row(small["ffn_norm"][1]))
    wf1 = dict(zip(("w_gate", "w_up"), get_w("gu1", n3)))
    g1, up1, act1 = _ffn_fwd("ffn1_fwd", n3, wf1["w_gate"], wf1["w_up"])
    wf1["w_down"] = get_w("down1", act1)[0]
    h4 = _mm_down("mm_down1", act1, wf1["w_down"], h3)

    dh4, dh4b, d_final, lsum = _loss_head("loss_head", h4, row(small["final_norm"]), tgt)

    def ffn_bwd(tag, dh, dhb, h_in, gain, n, g, up, act, w, dep):
        dg, dup = _ffn_bwd_act("ffn%s_bwd_act" % tag, dhb, w["w_down"], g, up, dep=dep)
        dwd = _mm_dwd("mm_dwd%s" % tag, act, dhb)
        dwg = _mm_dwd("mm_dwg%s" % tag, dg, n, dep=sync("bwd_ffn" + tag, dwd))
        dwu = _mm_dwd("mm_dwu%s" % tag, dup, n)
        dn = _mm_ffn_dn("mm_ffn_dn%s" % tag, dg, w["w_gate"], dup, w["w_up"])
        dh_in, dhb_in, dgain = _rms_bwd("rms_bwd_ffn%s" % tag, h_in, gain, dn, dh)
        dep = put_g("ffn" + tag, {"w_gate": dwg, "w_up": dwu, "w_down": dwd})
        return dh_in, dhb_in, dgain, dep

    dh3, dh3b, d_ffn1, dep = ffn_bwd("1", dh4, dh4b, h3, row(small["ffn_norm"][1]), n3, g1, up1,
                                     act1, wf1, None)
    dw_out_o = _mm_tn("mm_dw_out_o", y1, dh3b, BF16, dep=dep)
    dy1 = _mm_nt("mm_dy_o", dh3b, wo["w_out"], BF16, dep=sync("bwd_mix_o", dw_out_o))
    dxc, dcw_o = _conv_bwd("conv_o_bwd", seq, d, cw_o, dy1, 0, u1, 0, "mul",
                           x1=u1, c1=d, x2=u1, c2=2 * d, pre="mul")
    du1 = _odd_du("odd_du", u1, dy1, co, dxc)
    dw_in_o = _mm_tn("mm_dw_in_o", n2, du1, BF16)
    dn2 = _mm_nt("mm_dn_o", du1, wo["w_in"], BF16)
    dh2, dh2b, d_mix_o = _rms_bwd("rms_bwd_mix1", h2, row(small["mix_norm_o"]), dn2, dh3)
    dep = put_g("mix_o", {"w_in": dw_in_o, "w_out": dw_out_o})

    dh1, dh1b, d_ffn0, dep = ffn_bwd("0", dh2, dh2b, h1, row(small["ffn_norm"][0]), n1, g0, up0,
                                     act0, wf0, dep)
    dw_out_e = _mm_tn("mm_dw_out_e", z0, dh1b, BF16, dep=dep)
    dz0 = _mm_nt("mm_dz_e", dh1b, we["w_out"], BF16, dep=sync("bwd_mix_e", dw_out_e))
    da2, dws, dpm, vecs, dwp = _even_bwd("even_bwd", seq, dz0, a2, pm, row(small["ln_g_e"]),
                                         row(small["ln_b_e"]), wp, row(small["pool_scale_e"]))
    da1, dcw_e = _conv_bwd("conv_e_bwd", seq, c, cw_e, da2, 0, x1=u0, c1=0, x2=u0, c2=c, pre="glu")
    dbp = _conv_bwd("pool_bwd", seq, c, ptaps, dws, 0)
    du0 = _even_du("even_du", u0, da1, dbp, dpm)
    dw_in_e = _mm_tn("mm_dw_in_e", n0, du0, BF16)
    dep = put_g("mix_e", {"w_in": dw_in_e, "w_out": dw_out_e})
    dn0 = _mm_nt("mm_dn_e", du0, we["w_in"], BF16, dep=dep)
    dx, _, d_mix_e = _rms_bwd("rms_bwd_mix0", x, row(small["mix_norm_e"]), dn0, dh1)

    gsmall = {"mix_norm_e": d_mix_e[0], "conv_w_e": dcw_e, "conv_b_e": vecs[2], "ln_g_e": vecs[0],
              "ln_b_e": vecs[1], "w_pool_e": dwp, "pool_scale_e": vecs[3], "mix_norm_o": d_mix_o[0],
              "conv_w_o": dcw_o, "ffn_norm": jnp.concatenate([d_ffn0, d_ffn1], axis=0),
              "final_norm": d_final[0]}
    return lsum, dx, gsmall


def _place():
    x, y, c = (lax.axis_index(a) for a in MESH_AXES)
    return x, y, c


def _index(p):
    return 4 * p[0] + 2 * p[1] + p[2]


def _slab(ref, kind, d, n):
    if kind == "blk":
        return ref.at[d]
    return ref.at[:, pl.ds(pl.multiple_of(d * n, LANES), n)]


def _all_gather(name, shards, kinds):
    na = len(shards)

    def body(*refs):
        x_refs, o_refs = refs[:na], refs[na:2 * na]
        send_sems, recv_sems, local_sems = refs[2 * na:]
        x, y, c = _place()
        me, sib = (x, y, c), (x, y, 1 - c)
        chips = [(1 - x, y), (x, 1 - y), (1 - x, 1 - y)]

        def slot(a, p):
            return _slab(o_refs[a], kinds[a], _index(p), shards[a].shape[1])

        def copy(a, k, block, to, src=None):
            return pltpu.make_async_remote_copy(
                src_ref=slot(a, block) if src is None else src, dst_ref=slot(a, block),
                send_sem=send_sems.at[a, k], recv_sem=recv_sems.at[a, k],
                device_id=to, device_id_type=MESH)

        mine = [pltpu.make_async_copy(x_refs[a], slot(a, me), local_sems.at[a]) for a in range(na)]
        for cp in mine:
            cp.start()
        first = []
        for a in range(na):
            first.append(copy(a, 0, me, sib, src=x_refs[a]))
            first += [copy(a, 1 + j, me, (*chip, c), src=x_refs[a]) for j, chip in enumerate(chips)]
        for cp in first:
            cp.start()
        passed = []
        for j, chip in enumerate(chips):
            for a in range(na):
                copy(a, 1 + j, (*chip, c), me).wait_recv()
                fwd = copy(a, 4 + j, (*chip, c), sib)
                fwd.start()
                passed.append(fwd)
        for a in range(na):
            copy(a, 0, sib, me).wait_recv()
            for j, chip in enumerate(chips):
                copy(a, 4 + j, (*chip, 1 - c), me).wait_recv()
        for cp in first + passed:
            cp.wait_send()
        for cp in mine:
            cp.wait()

    shapes = []
    for s, kind in zip(shards, kinds):
        m, n = s.shape
        shapes.append(jax.ShapeDtypeStruct((NDEV, m, n) if kind == "blk" else (m, NDEV * n), s.dtype))
    return pl.pallas_call(
        body, name=name, in_specs=[ANY] * na, out_specs=[ANY] * na, out_shape=shapes,
        scratch_shapes=[pltpu.SemaphoreType.DMA((na, 7)), pltpu.SemaphoreType.DMA((na, 7)),
                        pltpu.SemaphoreType.DMA((na,))])(*shards)


HBM = pl.BlockSpec(memory_space=pltpu.HBM)
SEM = pl.BlockSpec(memory_space=pltpu.SEMAPHORE)
EFFECT = pltpu.SideEffectType.DATAFLOW_SIDE_EFFECTING
NCHIPS = 4


def _in_hbm(a):
    return pltpu.with_memory_space_constraint(a, pltpu.HBM)


def _gathered_shape(s, kind):
    m, n = s.shape
    return (NDEV, m, n) if kind == "blk" else (m, NDEV * n)


def _first_targets():
    x, y, c = _place()
    return [(x, y, 1 - c), (1 - x, y, c), (x, 1 - y, c), (1 - x, 1 - y, c)]


def _gather_start(name, shards, kinds, after):
    na = len(shards)

    def body(*refs):
        x_refs, land_refs = refs[:na], refs[na:2 * na]
        send_sems, recv_sems = refs[2 * na + 1], refs[2 * na + 2]
        token = refs[-1]
        me = _index(_place())
        for a in range(na):
            for k, to in enumerate(_first_targets()):
                pltpu.make_async_remote_copy(
                    src_ref=x_refs[a], dst_ref=_slab(land_refs[a], kinds[a], me, shards[a].shape[1]),
                    send_sem=send_sems.at[4 * a + k], recv_sem=recv_sems.at[4 * a + k],
                    device_id=to, device_id_type=MESH).start()
        token[...] = jnp.zeros_like(token)

    lands = [lax.empty(_gathered_shape(s, k), s.dtype) for s, k in zip(shards, kinds)]
    outs = pl.pallas_call(
        body, name=name,
        out_shape=(pltpu.SemaphoreType.DMA((4 * na,)), pltpu.SemaphoreType.DMA((4 * na,)),
                   *[pltpu.HBM(s.shape, s.dtype) for s in shards],
                   *[pltpu.HBM(l.shape, l.dtype) for l in lands],
                   jax.ShapeDtypeStruct((SUBLANES, LANES), F32)),
        in_specs=[HBM] * (2 * na) + [ANY],
        out_specs=(SEM, SEM, *[HBM] * (2 * na), pl.BlockSpec(memory_space=pltpu.VMEM)),
        input_output_aliases={i: 2 + i for i in range(2 * na)},
        compiler_params=pltpu.CompilerParams(has_side_effects=EFFECT),
    )(*[_in_hbm(s) for s in shards], *[_in_hbm(l) for l in lands], after)
    return outs[0], outs[1], outs[2:2 + na], outs[2 + na:2 + 2 * na], outs[-1]


def _gather_wait(name, started, kinds, after):
    send_sems, recv_sems, shards, lands, _ = started
    na = len(shards)

    def body(*refs):
        x_refs, land_refs = refs[:na], refs[na:2 * na]
        s_sems, r_sems = refs[2 * na], refs[2 * na + 1]
        for a in range(na):
            for k, frm in enumerate(_first_targets()):
                cp = pltpu.make_async_remote_copy(
                    src_ref=x_refs[a],
                    dst_ref=_slab(land_refs[a], kinds[a], _index(frm), shards[a].shape[1]),
                    send_sem=s_sems.at[4 * a + k], recv_sem=r_sems.at[4 * a + k],
                    device_id=frm, device_id_type=MESH)
                cp.wait_send()
                cp.wait_recv()

    outs = pl.pallas_call(
        body, name=name,
        out_shape=(*[pltpu.HBM(s.shape, s.dtype) for s in shards],
                   *[pltpu.HBM(l.shape, l.dtype) for l in lands]),
        in_specs=[HBM] * (2 * na) + [SEM, SEM, ANY], out_specs=[HBM] * (2 * na),
        input_output_aliases={i: i for i in range(2 * na)},
        compiler_params=pltpu.CompilerParams(has_side_effects=EFFECT),
    )(*shards, *lands, send_sems, recv_sems, after)
    return outs[:na], outs[na:]


def _split_start(name, bufs, ncopies, plan, after):
    nb = len(bufs)

    def body(*refs):
        send_sems, recv_sems, token = refs[nb + 1], refs[nb + 2], refs[-1]
        for k, (src, dst, to, _) in enumerate(plan(refs[:nb])):
            pltpu.make_async_remote_copy(src_ref=src, dst_ref=dst, send_sem=send_sems.at[k],
                                         recv_sem=recv_sems.at[k], device_id=to, device_id_type=MESH).start()
        token[...] = jnp.zeros_like(token)

    outs = pl.pallas_call(
        body, name=name,
        out_shape=(pltpu.SemaphoreType.DMA((ncopies,)), pltpu.SemaphoreType.DMA((ncopies,)),
                   *[pltpu.HBM(b.shape, b.dtype) for b in bufs],
                   jax.ShapeDtypeStruct((SUBLANES, LANES), F32)),
        in_specs=[HBM] * nb + [ANY],
        out_specs=(SEM, SEM, *[HBM] * nb, pl.BlockSpec(memory_space=pltpu.VMEM)),
        input_output_aliases={i: 2 + i for i in range(nb)},
        compiler_params=pltpu.CompilerParams(has_side_effects=EFFECT),
    )(*[_in_hbm(b) for b in bufs], after)
    return outs[0], outs[1], list(outs[2:2 + nb]), outs[-1]


def _split_wait(name, started, plan, after):
    send_sems, recv_sems, bufs, _ = started
    nb = len(bufs)

    def body(*refs):
        s_sems, r_sems = refs[nb], refs[nb + 1]
        for k, (src, _, to, landed) in enumerate(plan(refs[:nb])):
            cp = pltpu.make_async_remote_copy(src_ref=src, dst_ref=landed, send_sem=s_sems.at[k],
                                              recv_sem=r_sems.at[k], device_id=to, device_id_type=MESH)
            cp.wait_send()
            cp.wait_recv()

    outs = pl.pallas_call(
        body, name=name, out_shape=tuple(pltpu.HBM(b.shape, b.dtype) for b in bufs),
        in_specs=[HBM] * nb + [SEM, SEM, ANY], out_specs=[HBM] * nb,
        input_output_aliases={i: i for i in range(nb)},
        compiler_params=pltpu.CompilerParams(has_side_effects=EFFECT),
    )(*bufs, send_sems, recv_sems, after)
    return list(outs)


def _forward_plan(kinds, nloc):
    def plan(lands):
        x, y, c = _place()
        out = []
        for a, land in enumerate(lands):
            for chip in [(1 - x, y), (x, 1 - y), (1 - x, 1 - y)]:
                mine = _slab(land, kinds[a], _index((*chip, c)), nloc[a])
                out.append((mine, mine, (x, y, 1 - c), _slab(land, kinds[a], _index((*chip, 1 - c)), nloc[a])))
        return out
    return plan


def _own_copy(name, shard, land, kind, me):
    m, n = shard.shape
    tr = _tile(m, max(SUBLANES, 1048576 // n), SUBLANES)

    def body(s_ref, x_ref, land_ref, o_ref):
        o_ref[...] = x_ref[...]

    if kind == "blk":
        o_spec = pl.BlockSpec((None, tr, n), lambda i, s: (s[0], i, 0))
    else:
        o_spec = pl.BlockSpec((tr, n), lambda i, s: (i, s[0]))
    return pl.pallas_call(
        body, name=name,
        grid_spec=pltpu.PrefetchScalarGridSpec(
            num_scalar_prefetch=1, grid=(m // tr,),
            in_specs=[pl.BlockSpec((tr, n), lambda i, s: (i, 0)), ANY], out_specs=o_spec),
        out_shape=jax.ShapeDtypeStruct(land.shape, land.dtype),
        input_output_aliases={2: 0}, compiler_params=_cp("parallel"))(me, shard, land)


def _pair_plan(kinds, nloc):
    na = len(kinds)

    def plan(refs):
        x, y, c = _place()
        out = []
        for a in range(na):
            for j in range(NCHIPS):
                dst = refs[na + a].at[j]
                out.append((_slab(refs[a], kinds[a], 2 * j + (1 - c), nloc[a]), dst, (x, y, 1 - c), dst))
        return out
    return plan


def _chip_sum(name, full, kind, n, from_sib, place):
    _, m, _ = from_sib.shape
    tr = _tile(m, max(SUBLANES, 1048576 // n), SUBLANES)

    def body(s_ref, mine_ref, sib_ref, csum_ref, land_ref):
        v = (mine_ref[...].astype(F32) + sib_ref[...].astype(F32)).astype(csum_ref.dtype)
        csum_ref[...] = v

        @pl.when(pl.program_id(1) == s_ref[1])
        def _():
            land_ref[...] = v

    if kind == "blk":
        mine_spec = pl.BlockSpec((None, tr, n), lambda i, j, s: (2 * j + s[0], i, 0))
    else:
        mine_spec = pl.BlockSpec((tr, n), lambda i, j, s: (i, 2 * j + s[0]))
    slot = pl.BlockSpec((None, tr, n), lambda i, j, s: (j, i, 0))
    shp = jax.ShapeDtypeStruct((NCHIPS, m, n), from_sib.dtype)
    return pl.pallas_call(
        body, name=name,
        grid_spec=pltpu.PrefetchScalarGridSpec(
            num_scalar_prefetch=1, grid=(m // tr, NCHIPS), in_specs=[mine_spec, slot],
            out_specs=[slot, pl.BlockSpec((None, tr, n), lambda i, j, s: (s[1], i, 0))]),
        out_shape=[shp, shp], compiler_params=_cp("parallel", "arbitrary"))(place, full, from_sib)


def _other_chips():
    x, y, c = _place()
    return [(1 - x, y, c), (x, 1 - y, c), (1 - x, 1 - y, c)]


def _scatter_start(name, csums, lands, after):
    na = len(csums)

    def body(*refs):
        c_refs, land_refs = refs[:na], refs[na:2 * na]
        send_sems, recv_sems = refs[2 * na + 1], refs[2 * na + 2]
        token = refs[-1]
        x, y, _ = _place()
        for a in range(na):
            for k, to in enumerate(_other_chips()):
                pltpu.make_async_remote_copy(
                    src_ref=c_refs[a].at[2 * to[0] + to[1]], dst_ref=land_refs[a].at[2 * x + y],
                    send_sem=send_sems.at[3 * a + k], recv_sem=recv_sems.at[3 * a + k],
                    device_id=to, device_id_type=MESH).start()
        token[...] = jnp.zeros_like(token)

    outs = pl.pallas_call(
        body, name=name,
        out_shape=(pltpu.SemaphoreType.DMA((3 * na,)), pltpu.SemaphoreType.DMA((3 * na,)),
                   *[pltpu.HBM(s.shape, s.dtype) for s in csums],
                   *[pltpu.HBM(l.shape, l.dtype) for l in lands],
                   jax.ShapeDtypeStruct((SUBLANES, LANES), F32)),
        in_specs=[HBM] * (2 * na) + [ANY],
        out_specs=(SEM, SEM, *[HBM] * (2 * na), pl.BlockSpec(memory_space=pltpu.VMEM)),
        input_output_aliases={i: 2 + i for i in range(2 * na)},
        compiler_params=pltpu.CompilerParams(has_side_effects=EFFECT),
    )(*[_in_hbm(s) for s in csums], *[_in_hbm(l) for l in lands], after)
    return outs[0], outs[1], outs[2:2 + na], outs[2 + na:2 + 2 * na], outs[-1]


def _scatter_wait(name, started, after):
    send_sems, recv_sems, csums, lands, _ = started
    na = len(csums)

    def body(*refs):
        c_refs, land_refs = refs[:na], refs[na:2 * na]
        s_sems, r_sems = refs[2 * na], refs[2 * na + 1]
        for a in range(na):
            for k, frm in enumerate(_other_chips()):
                cp = pltpu.make_async_remote_copy(
                    src_ref=c_refs[a].at[2 * frm[0] + frm[1]], dst_ref=land_refs[a].at[2 * frm[0] + frm[1]],
                    send_sem=s_sems.at[3 * a + k], recv_sem=r_sems.at[3 * a + k],
                    device_id=frm, device_id_type=MESH)
                cp.wait_send()
                cp.wait_recv()

    outs = pl.pallas_call(
        body, name=name,
        out_shape=(*[pltpu.HBM(s.shape, s.dtype) for s in csums],
                   *[pltpu.HBM(l.shape, l.dtype) for l in lands]),
        in_specs=[HBM] * (2 * na) + [SEM, SEM, ANY], out_specs=[HBM] * (2 * na),
        input_output_aliases={i: i for i in range(2 * na)},
        compiler_params=pltpu.CompilerParams(has_side_effects=EFFECT),
    )(*csums, *lands, send_sems, recv_sems, after)
    return outs[na:]


def _adam_math(w, g, m, v):
    m = ADAM_B1 * m + (1.0 - ADAM_B1) * g
    v = ADAM_B2 * v + (1.0 - ADAM_B2) * (g * g)
    m_hat = m / (1.0 - ADAM_B1 ** ADAM_STEP)
    v_hat = v / (1.0 - ADAM_B2 ** ADAM_STEP)
    delta = -ADAM_LR * (m_hat / (jnp.sqrt(v_hat) + ADAM_EPS) + ADAM_WD * w)
    return delta, m, v


def _sum_adamw(name, parts, w, m, v, layer, prev=None):
    nl, r, c = w.shape
    nparts = parts.shape[0]
    tr = _tile(r, max(SUBLANES, 262144 // c), SUBLANES)

    def body(p_ref, w_ref, m_ref, v_ref, *rest):
        g_ref, d_ref, mo_ref, vo_ref = rest[-4:]
        g = p_ref[0].astype(F32)
        for s in range(1, nparts):
            g = g + p_ref[s].astype(F32)
        delta, mn, vn = _adam_math(w_ref[...], g, m_ref[...], v_ref[...])
        g_ref[...] = g
        d_ref[...] = delta
        mo_ref[...] = mn
        vo_ref[...] = vn

    row = pl.BlockSpec((None, tr, c), lambda i: (layer, i, 0))
    shp = jax.ShapeDtypeStruct((nl, r, c), F32)
    kept = [] if prev is None else list(prev)
    return pl.pallas_call(
        body, name=name, grid=(r // tr,),
        in_specs=[pl.BlockSpec((nparts, tr, c), lambda i: (0, i, 0)), row, row, row] + [ANY] * len(kept),
        out_specs=[row, row, row, row], out_shape=[shp, shp, shp, shp],
        input_output_aliases={4 + i: i for i in range(len(kept))},
        compiler_params=_cp("parallel"))(parts, w, m, v, *kept)


def _sum_parts(name, parts):
    _, r, c = parts.shape

    def body(p_ref, o_ref):
        g = p_ref[0]
        for s in range(1, NDEV):
            g = g + p_ref[s]
        o_ref[...] = g

    return pl.pallas_call(
        body, name=name, grid=(1,),
        in_specs=[pl.BlockSpec((NDEV, r, c), lambda i: (0, 0, 0))],
        out_specs=pl.BlockSpec((r, c), lambda i: (0, 0)),
        out_shape=jax.ShapeDtypeStruct((r, c), F32), compiler_params=_cp("arbitrary"))(parts)


def _adamw(name, w, g, m, v):
    r, c = w.shape

    def body(w_ref, g_ref, m_ref, v_ref, d_ref, mo_ref, vo_ref):
        delta, mn, vn = _adam_math(w_ref[...], g_ref[...], m_ref[...], v_ref[...])
        d_ref[...] = delta
        mo_ref[...] = mn
        vo_ref[...] = vn

    full = pl.BlockSpec((r, c), lambda i: (0, 0))
    shp = jax.ShapeDtypeStruct((r, c), F32)
    return pl.pallas_call(
        body, name=name, grid=(1,), in_specs=[full] * 4, out_specs=[full] * 3,
        out_shape=[shp] * 3, compiler_params=_cp("arbitrary"))(w, g, m, v)


def _pack(arrays):
    flat = jnp.concatenate([a.reshape(-1) for a in arrays])
    unit = SUBLANES * LANES
    pad = (-flat.shape[0]) % unit
    return jnp.pad(flat, (0, pad)).reshape(-1, LANES)


def _unpack(buf, shapes):
    flat = buf.reshape(-1)
    out, off = [], 0
    for shp in shapes:
        size = 1
        for s in shp:
            size *= s
        out.append(flat[off:off + size].reshape(shp))
        off += size
    return out


WEIGHTS = ["mix_norm_e", "w_in_e", "conv_w_e", "conv_b_e", "ln_g_e", "ln_b_e", "w_pool_e",
           "pool_scale_e", "w_out_e", "mix_norm_o", "w_in_o", "conv_w_o", "w_out_o", "ffn_norm",
           "w_gate", "w_up", "w_down", "final_norm"]
BIG = ["w_in_e", "w_out_e", "w_in_o", "w_out_o", "w_gate", "w_up", "w_down"]
SHARDED_SMALL = {"conv_w_e": 1, "w_pool_e": 1, "mix_norm_o": 0, "conv_w_o": 1}
SMALL = [n for n in WEIGHTS if n not in BIG]


def kernel(x, mix_norm_e, w_in_e, conv_w_e, conv_b_e, ln_g_e, ln_b_e, w_pool_e, pool_scale_e, w_out_e, mix_norm_o, w_in_o, conv_w_o, w_out_o, ffn_norm, w_gate, w_up, w_down, final_norm, loss_target, m_mix_norm_e, m_w_in_e, m_conv_w_e, m_conv_b_e, m_ln_g_e, m_ln_b_e, m_w_pool_e, m_pool_scale_e, m_w_out_e, m_mix_norm_o, m_w_in_o, m_conv_w_o, m_w_out_o, m_ffn_norm, m_w_gate, m_w_up, m_w_down, m_final_norm, v_mix_norm_e, v_w_in_e, v_conv_w_e, v_conv_b_e, v_ln_g_e, v_ln_b_e, v_w_pool_e, v_pool_scale_e, v_w_out_e, v_mix_norm_o, v_w_in_o, v_conv_w_o, v_w_out_o, v_ffn_norm, v_w_gate, v_w_up, v_w_down, v_final_norm):
    wts = dict(zip(WEIGHTS, [mix_norm_e, w_in_e, conv_w_e, conv_b_e, ln_g_e, ln_b_e, w_pool_e, pool_scale_e, w_out_e, mix_norm_o, w_in_o, conv_w_o, w_out_o, ffn_norm, w_gate, w_up, w_down, final_norm]))
    mom = dict(zip(WEIGHTS, [m_mix_norm_e, m_w_in_e, m_conv_w_e, m_conv_b_e, m_ln_g_e, m_ln_b_e, m_w_pool_e, m_pool_scale_e, m_w_out_e, m_mix_norm_o, m_w_in_o, m_conv_w_o, m_w_out_o, m_ffn_norm, m_w_gate, m_w_up, m_w_down, m_final_norm]))
    var = dict(zip(WEIGHTS, [v_mix_norm_e, v_w_in_e, v_conv_w_e, v_conv_b_e, v_ln_g_e, v_ln_b_e, v_w_pool_e, v_pool_scale_e, v_w_out_e, v_mix_norm_o, v_w_in_o, v_conv_w_o, v_w_out_o, v_ffn_norm, v_w_gate, v_w_up, v_w_down, v_final_norm]))
    bsz, seq, d = x.shape
    t = bsz * seq
    me = _index(_place())
    me_arr = jnp.reshape(me, (1,)).astype(jnp.int32)

    sh_names = list(SHARDED_SMALL)
    sh_local = [wts[n][0] for n in sh_names]
    packed = _pack(sh_local)
    gathered, = _all_gather("gather_small", [packed], ["blk"])
    small = {n: wts[n][0] for n in SMALL if n not in SHARDED_SMALL and n not in ("ffn_norm", "final_norm")}
    small["ffn_norm"], small["final_norm"] = ffn_norm, final_norm
    per_dev = [_unpack(gathered[s], [a.shape for a in sh_local]) for s in range(NDEV)]
    for i, n in enumerate(sh_names):
        small[n] = jnp.concatenate([per_dev[s][i] for s in range(NDEV)], axis=SHARDED_SMALL[n])

    for state in (wts, mom, var):
        for n in ("w_gate", "w_up"):
            state[n] = jnp.swapaxes(state[n], 1, 2)
    bf = lambda a: a.astype(BF16)
    mix_kinds, ffn_kinds = ["col", "blk"], ["blk", "blk", "blk"]
    ffn_names = ("w_gate", "w_up", "w_down")
    groups = {
        "mix_e": ([w_in_e.shape[2], d], mix_kinds, [("w_in_e", 0), ("w_out_e", 0)]),
        "ffn0": ([d, d, d], ffn_kinds, [(n, 0) for n in ffn_names]),
        "mix_o": ([w_in_o.shape[2], d], mix_kinds, [("w_in_o", 0), ("w_out_o", 0)]),
        "ffn1": ([d, d, d], ffn_kinds, [(n, 1) for n in ffn_names]),
    }
    gathers = {
        "in_e": ([bf(w_in_e[0])], ["col"]), "out_e": ([bf(w_out_e[0])], ["blk"]),
        "gu0": ([bf(wts["w_gate"][0]), bf(wts["w_up"][0])], ["blk", "blk"]), "down0": ([bf(w_down[0])], ["blk"]),
        "in_o": ([bf(w_in_o[0])], ["col"]), "out_o": ([bf(w_out_o[0])], ["blk"]),
        "gu1": ([bf(wts["w_gate"][1]), bf(wts["w_up"][1])], ["blk", "blk"]), "down1": ([bf(w_down[1])], ["blk"]),
    }
    started, prev = {}, gathered
    for grp, (shards, kinds) in gathers.items():
        started[grp] = _gather_start("gather_start_" + grp, shards, kinds, prev)
        prev = started[grp][4]
    all_started = prev[0, 0:1]

    passing, shards_of = {}, {}

    def pass_on(grp, after):
        shards, kinds = gathers[grp]
        shards_of[grp], lands = _gather_wait("gather_wait_" + grp, started[grp], kinds, after)
        plan = _forward_plan(kinds, [s.shape[1] for s in shards])
        passing[grp] = (_split_start("forward_start_" + grp, lands, 3 * len(lands), plan, after), plan)
        return passing[grp][0][3]

    def get_w(grp, after):
        if grp not in passing:
            after = pass_on(grp, after)
        st, plan = passing[grp]
        lands = _split_wait("forward_wait_" + grp, st, plan, after)
        full = [_own_copy("own_copy_%s%d" % (grp, a), shards_of[grp][a], lands[a], gathers[grp][1][a], me_arr)
                for a in range(len(lands))]
        return [full[0].reshape(-1, d)] if grp.startswith("out_") else full

    cx, cy, cc = _place()
    place = jnp.stack([cc, 2 * cx + cy]).astype(jnp.int32)
    bwd_order = ["ffn1", "mix_o", "ffn0", "mix_e"]
    pairing, pending, results = {}, {}, {}

    def put_g(grp, grads):
        nloc, kinds, _ = groups[grp]
        if len(kinds) == 2:
            fulls = [grads["w_in"], grads["w_out"].reshape(NDEV, -1, d)]
        else:
            fulls = [grads["w_gate"], grads["w_up"], grads["w_down"]]
        empties = []
        for g, kind, n in zip(fulls, kinds, nloc):
            empties.append(lax.empty((NCHIPS, g.shape[1] if kind == "blk" else g.shape[0], n), g.dtype))
        plan = _pair_plan(kinds, nloc)
        pairing[grp] = (_split_start("pair_start_" + grp, fulls + empties, NCHIPS * len(fulls), plan, place),
                        plan, kinds, nloc)
        token = pairing[grp][0][3]
        return send_sums(grp, token) if grp == bwd_order[-1] else token

    def send_sums(grp, after):
        st, plan, kinds, nloc = pairing[grp]
        bufs = _split_wait("pair_wait_" + grp, st, plan, after)
        na = len(kinds)
        sums = [_chip_sum("chip_sum_%s%d" % (grp, a), bufs[a], kinds[a], nloc[a], bufs[na + a], place)
                for a in range(na)]
        pending[grp] = _scatter_start("scatter_start_" + grp, [s[0] for s in sums], [s[1] for s in sums], after)
        return pending[grp][4]

    def finish(grp, after):
        lands = _scatter_wait("scatter_wait_" + grp, pending[grp], after)
        for (n, l), parts in zip(groups[grp][2], lands):
            results[n] = _sum_adamw("adamw_%s%d" % (n, l), parts, wts[n], mom[n], var[n], l, results.get(n))
        return results[groups[grp][2][-1][0]][1]

    fwd_sync = {"fwd_a": ["out_e"], "fwd_b": ["gu0"], "fwd_c": ["down0", "in_o"], "fwd_d": ["out_o"],
                "fwd_e": ["gu1"], "fwd_f": ["down1"]}

    def sync(tag, after):
        if tag in fwd_sync:
            for grp in fwd_sync[tag]:
                after = pass_on(grp, after)
            return after
        if tag == "bwd_mix_o":
            return send_sums("ffn1", after)
        if tag == "bwd_ffn0":
            return finish("ffn1", send_sums("mix_o", after))
        if tag == "bwd_mix_e":
            return finish("mix_o", send_sums("ffn0", after))
        return None

    small["mix_norm_e"] = small["mix_norm_e"] + all_started
    lsum, dx, gsmall = _local_step(x.reshape(t, d), loss_target.reshape(t, d), seq, small, get_w, put_g, sync)
    loss = lax.psum(jnp.sum(lsum), MESH_AXES)
    finish("ffn0", dx)

    out_g, out_d, out_m, out_v = {}, {}, {}, {}

    gs_list = [gsmall[n] for n in SMALL]
    gs_all, = _all_gather("gather_small_grads", [_pack(gs_list)], ["blk"])
    gs_sum = _unpack(_sum_parts("sum_small_grads", gs_all), [a.shape for a in gs_list])
    local_g = []
    for n, g in zip(SMALL, gs_sum):
        if n in SHARDED_SMALL:
            ax = SHARDED_SMALL[n]
            size = wts[n].shape[ax + 1]
            g = lax.dynamic_slice_in_dim(g, me * size, size, axis=ax)
        local_g.append(g.reshape(wts[n].shape))
    shapes = [wts[n].shape for n in SMALL]
    upd = _adamw("adamw_small", _pack([wts[n] for n in SMALL]), _pack(local_g),
                 _pack([mom[n] for n in SMALL]), _pack([var[n] for n in SMALL]))
    for i, outd in enumerate((out_d, out_m, out_v)):
        for n, a in zip(SMALL, _unpack(upd[i], shapes)):
            outd[n] = a
    for n, g in zip(SMALL, local_g):
        out_g[n] = g

    finish("mix_e", upd[0])
    for n in BIG:
        res = [jnp.swapaxes(a, 1, 2) for a in results[n]] if n in ("w_gate", "w_up") else results[n]
        out_g[n], out_d[n], out_m[n], out_v[n] = res

    return (loss, dx.reshape(bsz, seq, d), *[out_g[n] for n in WEIGHTS], *[out_d[n] for n in WEIGHTS],
            *[out_m[n] for n in WEIGHTS], *[out_v[n] for n in WEIGHTS])
```

```python
import functools

import jax
import jax.numpy as jnp
from jax import lax
from jax.experimental import pallas as pl
from jax.experimental.pallas import tpu as pltpu

F32 = jnp.float32
BF16 = jnp.bfloat16
NDEV = 8
MESH_AXES = ("x", "y", "c")
EPS = 1e-6
POOL_WINDOWS = (2, 4, 8, 16)
CONV_WIDTH = 31
SHORT_WIDTH = 3
ADAM_LR = 0.001
ADAM_B1 = 0.9
ADAM_B2 = 0.999
ADAM_EPS = 1e-08
ADAM_WD = 0.01
ADAM_STEP = 10
LANES = 128
SUBLANES = 8
VMEM_LIMIT = 56 * 1024 * 1024
MM_TK = 2048
MESH = pl.DeviceIdType.MESH
ANY = pl.BlockSpec(memory_space=pl.ANY)


def _cp(*sem):
    return pltpu.CompilerParams(dimension_semantics=sem, vmem_limit_bytes=VMEM_LIMIT)


def _tile(n, pref, unit=LANES):
    if n <= pref:
        return n
    t = (pref // unit) * unit
    while t > unit and n % t:
        t -= unit
    assert n % t == 0, (n, pref)
    return t


def _sigmoid(v):
    return 0.5 * jnp.tanh(0.5 * v) + 0.5


def _col_chunks(f, width=512):
    return [(c0, min(c0 + width, f)) for c0 in range(0, f, width)]


def _mm(name, pairs, a_specs, b_specs, dims, out_shape, o_spec, grid, acc_shape,
        res=None, res_spec=None, dep=None):
    np_ = len(pairs)
    nk = grid[2]
    has_res = res is not None
    n_in = 2 * np_ + (1 if has_res else 0) + (0 if dep is None else 1)

    def body(*refs):
        a_refs = refs[:np_]
        b_refs = refs[np_:2 * np_]
        r_ref = refs[2 * np_] if has_res else None
        o_ref = refs[n_in]
        acc = refs[-1]

        def part():
            s = None
            for a_ref, b_ref in zip(a_refs, b_refs):
                blocks = [(a_ref[...], b_ref[...])] if len(a_ref.shape) == 2 else [
                    (a_ref[q], b_ref[q]) for q in range(a_ref.shape[0])]
                for av, bv in blocks:
                    d = lax.dot_general(av, bv, dims, preferred_element_type=F32)
                    s = d if s is None else s + d
            return s

        def finish(v):
            if has_res:
                v = v + r_ref[...]
            o_ref[...] = v.astype(o_ref.dtype)

        if nk == 1:
            finish(part())
        else:
            k = pl.program_id(2)

            @pl.when(k == 0)
            def _():
                acc[...] = part()

            @pl.when((k > 0) & (k < nk - 1))
            def _():
                acc[...] += part()

            @pl.when(k == nk - 1)
            def _():
                finish(acc[...] + part())

    ins = [p[0] for p in pairs] + [p[1] for p in pairs]
    specs = list(a_specs) + list(b_specs)
    if has_res:
        ins.append(res)
        specs.append(res_spec)
    if dep is not None:
        ins.append(dep)
        specs.append(ANY)
    return pl.pallas_call(
        body, name=name, grid=grid, in_specs=specs, out_specs=o_spec, out_shape=out_shape,
        scratch_shapes=[pltpu.VMEM(acc_shape if nk > 1 else (SUBLANES, LANES), F32)],
        compiler_params=_cp("parallel", "parallel", "arbitrary"))(*ins)


NN = (((1,), (0,)), ((), ()))
NT = (((1,), (1,)), ((), ()))
TN = (((0,), (0,)), ((), ()))


def _mm_nn(name, a, b, out_dtype, res=None, dep=None):
    m, kk = a.shape
    n = b.shape[1]
    tm, tn, tk = _tile(m, 1024), _tile(n, 1024), _tile(kk, MM_TK)
    return _mm(name, [(a, b)],
               [pl.BlockSpec((tm, tk), lambda i, j, k: (i, k))],
               [pl.BlockSpec((tk, tn), lambda i, j, k: (k, j))], NN,
               jax.ShapeDtypeStruct((m, n), out_dtype),
               pl.BlockSpec((tm, tn), lambda i, j, k: (i, j)),
               (m // tm, n // tn, kk // tk), (tm, tn), res,
               pl.BlockSpec((tm, tn), lambda i, j, k: (i, j)), dep=dep)


def _mm_nt(name, a, b, out_dtype, dep=None):
    m, n = a.shape
    kk = b.shape[0]
    tm, tn, tk = _tile(m, 1024), _tile(kk, 1024), _tile(n, MM_TK)
    return _mm(name, [(a, b)],
               [pl.BlockSpec((tm, tk), lambda i, j, k: (i, k))],
               [pl.BlockSpec((tn, tk), lambda i, j, k: (j, k))], NT,
               jax.ShapeDtypeStruct((m, kk), out_dtype),
               pl.BlockSpec((tm, tn), lambda i, j, k: (i, j)),
               (m // tm, kk // tn, n // tk), (tm, tn), dep=dep)


def _mm_tn(name, a, b, out_dtype, dep=None):
    t, m = a.shape
    n = b.shape[1]
    tm, tn, tk = _tile(m, 1024), _tile(n, 1024), _tile(t, MM_TK)
    return _mm(name, [(a, b)],
               [pl.BlockSpec((tk, tm), lambda i, j, k: (k, i))],
               [pl.BlockSpec((tk, tn), lambda i, j, k: (k, j))], TN,
               jax.ShapeDtypeStruct((m, n), out_dtype),
               pl.BlockSpec((tm, tn), lambda i, j, k: (i, j)),
               (m // tm, n // tn, t // tk), (tm, tn), dep=dep)


def _mm_down(name, act, wd, res, dep=None):
    nb, t, f = act.shape
    d = wd.shape[2]
    tm, tn = _tile(t, 1024), _tile(d, 1024)
    kb = 2 if nb % 2 == 0 else 1
    return _mm(name, [(act, wd)],
               [pl.BlockSpec((kb, tm, f), lambda i, j, k: (k, i, 0))],
               [pl.BlockSpec((kb, f, tn), lambda i, j, k: (k, 0, j))], NN,
               jax.ShapeDtypeStruct((t, d), F32),
               pl.BlockSpec((tm, tn), lambda i, j, k: (i, j)),
               (t // tm, d // tn, nb // kb), (tm, tn), res,
               pl.BlockSpec((tm, tn), lambda i, j, k: (i, j)), dep=dep)


def _mm_ffn_dn(name, dg, wg, dup, wu):
    nb, t, f = dg.shape
    d = wg.shape[2]
    tm, tn = _tile(t, 1024), _tile(d, 1024)
    kb = 2 if nb % 2 == 0 else 1
    a_spec = pl.BlockSpec((kb, tm, f), lambda i, j, k: (k, i, 0))
    b_spec = pl.BlockSpec((kb, f, tn), lambda i, j, k: (k, 0, j))
    return _mm(name, [(dg, wg), (dup, wu)], [a_spec, a_spec], [b_spec, b_spec], NN,
               jax.ShapeDtypeStruct((t, d), BF16),
               pl.BlockSpec((tm, tn), lambda i, j, k: (i, j)),
               (t // tm, d // tn, nb // kb), (tm, tn))


def _mm_dwd(name, act, dh, dep=None):
    nb, t, f = act.shape
    d = dh.shape[1]
    tn, tk = _tile(d, 1024), _tile(t, MM_TK)
    return _mm(name, [(act, dh)],
               [pl.BlockSpec((None, tk, f), lambda i, j, k: (i, k, 0))],
               [pl.BlockSpec((tk, tn), lambda i, j, k: (k, j))], TN,
               jax.ShapeDtypeStruct((nb, f, d), BF16),
               pl.BlockSpec((None, f, tn), lambda i, j, k: (i, 0, j)),
               (nb, d // tn, t // tk), (f, tn), dep=dep)


def _ffn_fwd(name, n, wg, wu):
    nb, f, d = wg.shape
    t = n.shape[0]
    tm = _tile(t, 512)

    def body(n_ref, wg_ref, wu_ref, g_ref, up_ref, act_ref):
        nv = n_ref[...]
        for c0, c1 in _col_chunks(f):
            g = lax.dot_general(nv, wg_ref[c0:c1, :], NT, preferred_element_type=F32)
            up = lax.dot_general(nv, wu_ref[c0:c1, :], NT, preferred_element_type=F32)
            g_ref[:, c0:c1] = g.astype(BF16)
            up_ref[:, c0:c1] = up.astype(BF16)
            act_ref[:, c0:c1] = (g * _sigmoid(g) * up).astype(BF16)

    w_spec = pl.BlockSpec((None, f, d), lambda j, i: (j, 0, 0))
    o_spec = pl.BlockSpec((None, tm, f), lambda j, i: (j, i, 0))
    shp = jax.ShapeDtypeStruct((nb, t, f), BF16)
    return pl.pallas_call(
        body, name=name, grid=(nb, t // tm),
        in_specs=[pl.BlockSpec((tm, d), lambda j, i: (i, 0)), w_spec, w_spec],
        out_specs=[o_spec, o_spec, o_spec], out_shape=[shp, shp, shp],
        compiler_params=_cp("parallel", "parallel"))(n, wg, wu)


def _ffn_bwd_act(name, dh, wd, g, up, dep=None):
    nb, f, d = wd.shape
    t = dh.shape[0]
    tm = _tile(t, 512)

    def body(dh_ref, wd_ref, g_ref, up_ref, *rest):
        dg_ref, dup_ref = rest[-2:]
        dhv = dh_ref[...]
        for c0, c1 in _col_chunks(f):
            da = lax.dot_general(dhv, wd_ref[c0:c1, :], NT, preferred_element_type=F32)
            gv = g_ref[:, c0:c1].astype(F32)
            uv = up_ref[:, c0:c1].astype(F32)
            sg = _sigmoid(gv)
            dg_ref[:, c0:c1] = (da * uv * (sg * (1.0 + gv * (1.0 - sg)))).astype(BF16)
            dup_ref[:, c0:c1] = (da * gv * sg).astype(BF16)

    o_spec = pl.BlockSpec((None, tm, f), lambda j, i: (j, i, 0))
    shp = jax.ShapeDtypeStruct((nb, t, f), BF16)
    return pl.pallas_call(
        body, name=name, grid=(nb, t // tm),
        in_specs=[pl.BlockSpec((tm, d), lambda j, i: (i, 0)),
                  pl.BlockSpec((None, f, d), lambda j, i: (j, 0, 0)), o_spec, o_spec]
        + ([] if dep is None else [ANY]),
        out_specs=[o_spec, o_spec], out_shape=[shp, shp],
        compiler_params=_cp("parallel", "parallel"))(dh, wd, g, up, *([] if dep is None else [dep]))


def _rms_fwd(name, h, gain):
    t, d = h.shape
    tr = _tile(t, 512, SUBLANES)

    def body(h_ref, g_ref, n_ref):
        hv = h_ref[...]
        r = lax.rsqrt(jnp.mean(hv * hv, axis=-1, keepdims=True) + EPS)
        n_ref[...] = (hv * r * g_ref[...]).astype(BF16)

    return pl.pallas_call(
        body, name=name, grid=(t // tr,),
        in_specs=[pl.BlockSpec((tr, d), lambda i: (i, 0)), pl.BlockSpec((1, d), lambda i: (0, 0))],
        out_specs=pl.BlockSpec((tr, d), lambda i: (i, 0)),
        out_shape=jax.ShapeDtypeStruct((t, d), BF16),
        compiler_params=_cp("parallel"))(h, gain)


def _rms_bwd_math(hv, gain, dn):
    d = hv.shape[-1]
    r = lax.rsqrt(jnp.mean(hv * hv, axis=-1, keepdims=True) + EPS)
    xhat = hv * r
    dxh = dn * gain
    dh = r * (dxh - xhat * (jnp.sum(dxh * xhat, axis=-1, keepdims=True) / d))
    dgain = jnp.sum(dn * xhat, axis=0, keepdims=True)
    return dh, dgain


def _rms_bwd(name, h, gain, dn, dres):
    t, d = h.shape
    tr = _tile(t, 256, SUBLANES)

    def body(h_ref, g_ref, dn_ref, dr_ref, dh_ref, dhb_ref, dg_ref):
        dh, dgain = _rms_bwd_math(h_ref[...], g_ref[...], dn_ref[...].astype(F32))
        dh = dh + dr_ref[...]
        dh_ref[...] = dh
        dhb_ref[...] = dh.astype(BF16)

        @pl.when(pl.program_id(0) == 0)
        def _():
            dg_ref[...] = dgain

        @pl.when(pl.program_id(0) > 0)
        def _():
            dg_ref[...] += dgain

    row = pl.BlockSpec((tr, d), lambda i: (i, 0))
    vec = pl.BlockSpec((1, d), lambda i: (0, 0))
    return pl.pallas_call(
        body, name=name, grid=(t // tr,), in_specs=[row, vec, row, row],
        out_specs=[row, row, vec],
        out_shape=[jax.ShapeDtypeStruct((t, d), F32), jax.ShapeDtypeStruct((t, d), BF16),
                   jax.ShapeDtypeStruct((1, d), F32)],
        compiler_params=_cp("arbitrary"))(h, gain, dn, dres)


def _loss_head(name, h, gain, tgt):
    t, d = h.shape
    tr = _tile(t, 256, SUBLANES)

    def body(h_ref, g_ref, t_ref, dh_ref, dhb_ref, dg_ref, ls_ref):
        hv = h_ref[...]
        gv = g_ref[...]
        r = lax.rsqrt(jnp.mean(hv * hv, axis=-1, keepdims=True) + EPS)
        err = hv * r * gv - t_ref[...]
        lsum = 0.5 * jnp.sum(err * err, axis=0, keepdims=True) / d
        dh, dgain = _rms_bwd_math(hv, gv, err / d)
        dh_ref[...] = dh
        dhb_ref[...] = dh.astype(BF16)

        @pl.when(pl.program_id(0) == 0)
        def _():
            dg_ref[...] = dgain
            ls_ref[...] = lsum

        @pl.when(pl.program_id(0) > 0)
        def _():
            dg_ref[...] += dgain
            ls_ref[...] += lsum

    row = pl.BlockSpec((tr, d), lambda i: (i, 0))
    vec = pl.BlockSpec((1, d), lambda i: (0, 0))
    return pl.pallas_call(
        body, name=name, grid=(t // tr,), in_specs=[row, vec, row],
        out_specs=[row, row, vec, vec],
        out_shape=[jax.ShapeDtypeStruct((t, d), F32), jax.ShapeDtypeStruct((t, d), BF16),
                   jax.ShapeDtypeStruct((1, d), F32), jax.ShapeDtypeStruct((1, d), F32)],
        compiler_params=_cp("arbitrary"))(h, gain, tgt)


def _conv_geom(t, seq, c, k):
    halo = 32 if k - 1 > SUBLANES else SUBLANES
    assert k - 1 <= halo
    tm = min(256, seq // 2)
    tc = min(512, c)
    assert seq % tm == 0 and tm % halo == 0 and c % tc == 0 and t % seq == 0
    return halo, tm, tc, min(128, tm), min(LANES, tc)


def _pre(kind, a, b):
    if kind == "glu":
        return a * _sigmoid(b)
    if kind == "mul":
        return a * b
    return a


def _taps(k):
    return sorted((s % SUBLANES, s // SUBLANES, s) for s in range(k))


def _conv_fwd(name, seq, c, w, x1, c1, x2=None, c2=0, pre=None, bias=None, post=None, cpost=0):
    t = x1.shape[0]
    k = w.shape[0]
    halo, tm, tc, sr, sl = _conv_geom(t, seq, c, k)
    nb, cps = tm // halo, seq // tm
    two = x2 is not None
    has_bias, has_post = bias is not None, post is not None

    def body(*refs):
        it = iter(refs)
        x1c, x1h = next(it), next(it)
        x2c, x2h = (next(it), next(it)) if two else (None, None)
        w_ref = next(it)
        b_ref = next(it) if has_bias else None
        p_ref = next(it) if has_post else None
        o_ref = next(it)
        y_ref = next(it) if has_post else None
        xs = next(it)
        first = (pl.program_id(1) % cps) == 0
        hv = _pre(pre, x1h[...].astype(F32), x2h[...].astype(F32) if two else None)
        xs[0:halo, :] = jnp.where(first, 0.0, hv)
        xs[halo:halo + tm, :] = _pre(pre, x1c[...].astype(F32), x2c[...].astype(F32) if two else None)
        for l0 in range(0, tc, sl):
            ls = slice(l0, l0 + sl)
            for r0 in range(0, tm, sr):
                win = xs[r0:r0 + sr + halo, ls]
                acc = jnp.zeros((sr, sl), F32)
                rolled = {}
                for r, q, s in _taps(k):
                    if r not in rolled:
                        rolled[r] = win if r == 0 else pltpu.roll(win, r, 0)
                    lo = halo - SUBLANES * q
                    acc = acc + w_ref[k - 1 - s:k - s, ls] * rolled[r][lo:lo + sr]
                if has_bias:
                    acc = acc + b_ref[:, ls]
                o_ref[r0:r0 + sr, ls] = acc.astype(o_ref.dtype)
                if has_post:
                    y_ref[r0:r0 + sr, ls] = (acc * p_ref[r0:r0 + sr, ls].astype(F32)).astype(y_ref.dtype)

    def cur(off):
        return pl.BlockSpec((tm, tc), lambda j, i: (i, off // tc + j))

    def prev(off):
        return pl.BlockSpec((halo, tc), lambda j, i: (jnp.maximum(i * nb - 1, 0), off // tc + j))

    ins, specs = [x1, x1], [cur(c1), prev(c1)]
    if two:
        ins += [x2, x2]
        specs += [cur(c2), prev(c2)]
    ins.append(w)
    specs.append(pl.BlockSpec((k, tc), lambda j, i: (0, j)))
    if has_bias:
        ins.append(bias)
        specs.append(pl.BlockSpec((1, tc), lambda j, i: (0, j)))
    if has_post:
        ins.append(post)
        specs.append(cur(cpost))
    o_spec = pl.BlockSpec((tm, tc), lambda j, i: (i, j))
    shp = jax.ShapeDtypeStruct((t, c), BF16)
    return pl.pallas_call(
        body, name=name, grid=(c // tc, t // tm), in_specs=specs,
        out_specs=[o_spec, o_spec] if has_post else o_spec,
        out_shape=[shp, shp] if has_post else shp,
        scratch_shapes=[pltpu.VMEM((halo + tm, tc), F32)],
        compiler_params=_cp("parallel", "parallel"))(*ins)


def _conv_bwd(name, seq, c, w, d1, cd1, d2=None, cd2=0, dpre=None,
              x1=None, c1=0, x2=None, c2=0, pre=None):
    t = d1.shape[0]
    k = w.shape[0]
    halo, tm, tc, sr, sl = _conv_geom(t, seq, c, k)
    nb, cps = tm // halo, seq // tm
    nchunks = t // tm
    dtwo, xtwo, has_x = d2 is not None, x2 is not None, x1 is not None

    def body(*refs):
        it = iter(refs)
        d1c, d1n = next(it), next(it)
        d2c, d2n = (next(it), next(it)) if dtwo else (None, None)
        x1c, x1h = (next(it), next(it)) if has_x else (None, None)
        x2c, x2h = (next(it), next(it)) if xtwo else (None, None)
        w_ref = next(it)
        dx_ref = next(it)
        dw_ref = next(it) if has_x else None
        ds = next(it)
        xs = next(it) if has_x else None
        i = pl.program_id(1)
        last = (i % cps) == cps - 1
        ds[0:tm, :] = _pre(dpre, d1c[...].astype(F32), d2c[...].astype(F32) if dtwo else None)
        nv = _pre(dpre, d1n[...].astype(F32), d2n[...].astype(F32) if dtwo else None)
        ds[tm:tm + halo, :] = jnp.where(last, 0.0, nv)
        if has_x:
            first = (i % cps) == 0
            hv = _pre(pre, x1h[...].astype(F32), x2h[...].astype(F32) if xtwo else None)
            xs[0:halo, :] = jnp.where(first, 0.0, hv)
            xs[halo:halo + tm, :] = _pre(pre, x1c[...].astype(F32), x2c[...].astype(F32) if xtwo else None)

            @pl.when(i == 0)
            def _():
                dw_ref[...] = jnp.zeros_like(dw_ref)

        for l0 in range(0, tc, sl):
            ls = slice(l0, l0 + sl)
            for r0 in range(0, tm, sr):
                win = ds[r0:r0 + sr + halo, ls]
                nrow = sr + halo
                acc = jnp.zeros((sr, sl), F32)
                rolled = {}
                for r, q, s in _taps(k):
                    if r not in rolled:
                        rolled[r] = win if r == 0 else pltpu.roll(win, nrow - r, 0)
                    lo = SUBLANES * q
                    acc = acc + w_ref[k - 1 - s:k - s, ls] * rolled[r][lo:lo + sr]
                dx_ref[r0:r0 + sr, ls] = acc.astype(dx_ref.dtype)
                if has_x:
                    dcur = win[0:sr]
                    xwin = xs[r0:r0 + sr + halo, ls]
                    xrolled = {}
                    for r, q, s in _taps(k):
                        if r not in xrolled:
                            xrolled[r] = xwin if r == 0 else pltpu.roll(xwin, r, 0)
                        lo = halo - SUBLANES * q
                        part = jnp.sum(dcur * xrolled[r][lo:lo + sr], axis=0, keepdims=True)
                        dw_ref[k - 1 - s:k - s, ls] += part

    def cur(off):
        return pl.BlockSpec((tm, tc), lambda j, i: (i, off // tc + j))

    def prev(off):
        return pl.BlockSpec((halo, tc), lambda j, i: (jnp.maximum(i * nb - 1, 0), off // tc + j))

    def nxt(off):
        return pl.BlockSpec((halo, tc),
                            lambda j, i: (jnp.minimum((i + 1) * nb, nchunks * nb - 1), off // tc + j))

    ins, specs = [d1, d1], [cur(cd1), nxt(cd1)]
    if dtwo:
        ins += [d2, d2]
        specs += [cur(cd2), nxt(cd2)]
    if has_x:
        ins += [x1, x1]
        specs += [cur(c1), prev(c1)]
    if xtwo:
        ins += [x2, x2]
        specs += [cur(c2), prev(c2)]
    ins.append(w)
    specs.append(pl.BlockSpec((k, tc), lambda j, i: (0, j)))
    o_specs = [pl.BlockSpec((tm, tc), lambda j, i: (i, j))]
    o_shapes = [jax.ShapeDtypeStruct((t, c), BF16)]
    scratch = [pltpu.VMEM((tm + halo, tc), F32)]
    if has_x:
        o_specs.append(pl.BlockSpec((k, tc), lambda j, i: (0, j)))
        o_shapes.append(jax.ShapeDtypeStruct((k, c), F32))
        scratch.append(pltpu.VMEM((halo + tm, tc), F32))
    out = pl.pallas_call(
        body, name=name, grid=(c // tc, t // tm), in_specs=specs, out_specs=o_specs,
        out_shape=o_shapes, scratch_shapes=scratch,
        compiler_params=_cp("parallel", "arbitrary"))(*ins)
    return out if has_x else out[0]


def _pool_taps(c):
    kmax = max(POOL_WINDOWS)
    grp = c // len(POOL_WINDOWS)
    cols = []
    for wdw in POOL_WINDOWS:
        col = jnp.concatenate([jnp.zeros((kmax - wdw,), F32), jnp.ones((wdw,), F32)])
        cols.append(jnp.tile(col[:, None], (1, grp)))
    return jnp.concatenate(cols, axis=1)


def _counts(i, tr, seq, grp):
    pos = (i * tr + lax.broadcasted_iota(jnp.int32, (tr, 1), 0)) % seq + 1
    return [1.0 / jnp.minimum(pos, wdw).astype(F32) for wdw in POOL_WINDOWS]


def _ln_stats(a2):
    mu = jnp.mean(a2, axis=-1, keepdims=True)
    xc = a2 - mu
    rstd = lax.rsqrt(jnp.mean(xc * xc, axis=-1, keepdims=True) + EPS)
    return xc * rstd, rstd


def _even_fwd(name, seq, a2, ws, u, ln_g, ln_b, w_pool, scale):
    t, c = a2.shape
    ng = len(POOL_WINDOWS)
    grp = c // ng
    tr = _tile(t, 256, SUBLANES)

    def body(a_ref, ws_ref, b_ref, g_ref, bb_ref, wp_ref, sc_ref, z_ref, pm_ref):
        xhat, _ = _ln_stats(a_ref[...].astype(F32))
        l = xhat * g_ref[...] + bb_ref[...]
        z_ref[:, 0:c] = (l * _sigmoid(l)).astype(BF16)
        inv = _counts(pl.program_id(0), tr, seq, grp)
        for g in range(ng):
            gs = slice(g * grp, (g + 1) * grp)
            pm = (ws_ref[:, gs].astype(F32) * inv[g] - b_ref[:, gs].astype(F32)).astype(BF16)
            pm_ref[:, gs] = pm
            q = jnp.dot(pm, wp_ref[g], preferred_element_type=F32)
            z_ref[:, c + g * grp:c + (g + 1) * grp] = (q * sc_ref[:, gs]).astype(BF16)

    row = pl.BlockSpec((tr, c), lambda i: (i, 0))
    vec = pl.BlockSpec((1, c), lambda i: (0, 0))
    return pl.pallas_call(
        body, name=name, grid=(t // tr,),
        in_specs=[row, row, pl.BlockSpec((tr, c), lambda i: (i, 2)), vec, vec,
                  pl.BlockSpec((ng, grp, grp), lambda i: (0, 0, 0)), vec],
        out_specs=[pl.BlockSpec((tr, 2 * c), lambda i: (i, 0)), row],
        out_shape=[jax.ShapeDtypeStruct((t, 2 * c), BF16), jax.ShapeDtypeStruct((t, c), BF16)],
        compiler_params=_cp("parallel"))(a2, ws, u, ln_g, ln_b, w_pool, scale)


def _even_bwd(name, seq, dz, a2, pm, ln_g, ln_b, w_pool, scale):
    t, c = a2.shape
    ng = len(POOL_WINDOWS)
    grp = c // ng
    tr = _tile(t, 256, SUBLANES)

    def body(dz_ref, a_ref, pm_ref, g_ref, bb_ref, wp_ref, sc_ref,
             da_ref, dws_ref, dpm_ref, vec_ref, dwp_ref):
        i = pl.program_id(0)

        @pl.when(i == 0)
        def _():
            vec_ref[...] = jnp.zeros_like(vec_ref)
            dwp_ref[...] = jnp.zeros_like(dwp_ref)

        xhat, rstd = _ln_stats(a_ref[...].astype(F32))
        gv = g_ref[...]
        l = xhat * gv + bb_ref[...]
        sg = _sigmoid(l)
        dl = dz_ref[:, 0:c].astype(F32) * (sg * (1.0 + l * (1.0 - sg)))
        dxh = dl * gv
        da2 = rstd * (dxh - jnp.mean(dxh, axis=-1, keepdims=True)
                      - xhat * jnp.mean(dxh * xhat, axis=-1, keepdims=True))
        da_ref[...] = da2.astype(BF16)
        vec_ref[0:1, :] += jnp.sum(dl * xhat, axis=0, keepdims=True)
        vec_ref[1:2, :] += jnp.sum(dl, axis=0, keepdims=True)
        vec_ref[2:3, :] += jnp.sum(da2, axis=0, keepdims=True)
        inv = _counts(i, tr, seq, grp)
        for g in range(ng):
            gs = slice(g * grp, (g + 1) * grp)
            pmv = pm_ref[:, gs]
            wp = wp_ref[g]
            dp = dz_ref[:, c + g * grp:c + (g + 1) * grp].astype(F32)
            q = jnp.dot(pmv, wp, preferred_element_type=F32)
            vec_ref[3:4, gs] += jnp.sum(dp * q, axis=0, keepdims=True)
            dq = (dp * sc_ref[:, gs]).astype(BF16)
            dpm = lax.dot_general(dq, wp, NT, preferred_element_type=F32)
            dwp_ref[g] += lax.dot_general(pmv, dq, TN, preferred_element_type=F32)
            dpm_ref[:, gs] = dpm.astype(BF16)
            dws_ref[:, gs] = (dpm * inv[g]).astype(BF16)

    row = pl.BlockSpec((tr, c), lambda i: (i, 0))
    vec = pl.BlockSpec((1, c), lambda i: (0, 0))
    rshape = jax.ShapeDtypeStruct((t, c), BF16)
    return pl.pallas_call(
        body, name=name, grid=(t // tr,),
        in_specs=[pl.BlockSpec((tr, 2 * c), lambda i: (i, 0)), row, row, vec, vec,
                  pl.BlockSpec((ng, grp, grp), lambda i: (0, 0, 0)), vec],
        out_specs=[row, row, row, pl.BlockSpec((SUBLANES, c), lambda i: (0, 0)),
                   pl.BlockSpec((ng, grp, grp), lambda i: (0, 0, 0))],
        out_shape=[rshape, rshape, rshape, jax.ShapeDtypeStruct((SUBLANES, c), F32),
                   jax.ShapeDtypeStruct((ng, grp, grp), F32)],
        compiler_params=_cp("arbitrary"))(dz, a2, pm, ln_g, ln_b, w_pool, scale)


def _even_du(name, u, da1, dbp, dpm):
    t, c = da1.shape
    tr = _tile(t, 256, SUBLANES)

    def body(u_ref, da_ref, dbp_ref, dpm_ref, du_ref):
        val = u_ref[:, 0:c].astype(F32)
        sg = _sigmoid(u_ref[:, c:2 * c].astype(F32))
        da = da_ref[...].astype(F32)
        du_ref[:, 0:c] = (da * sg).astype(BF16)
        du_ref[:, c:2 * c] = (da * val * sg * (1.0 - sg)).astype(BF16)
        du_ref[:, 2 * c:3 * c] = (dbp_ref[...].astype(F32) - dpm_ref[...].astype(F32)).astype(BF16)

    row = pl.BlockSpec((tr, c), lambda i: (i, 0))
    wide = pl.BlockSpec((tr, 3 * c), lambda i: (i, 0))
    return pl.pallas_call(
        body, name=name, grid=(t // tr,), in_specs=[wide, row, row, row], out_specs=wide,
        out_shape=jax.ShapeDtypeStruct((t, 3 * c), BF16),
        compiler_params=_cp("parallel"))(u, da1, dbp, dpm)


def _odd_du(name, u, dy, co, dxc):
    t, c = dy.shape
    tr = _tile(t, 256, SUBLANES)

    def body(u_ref, dy_ref, co_ref, dx_ref, du_ref):
        dx = dx_ref[...].astype(F32)
        du_ref[:, 0:c] = (dy_ref[...].astype(F32) * co_ref[...].astype(F32)).astype(BF16)
        du_ref[:, c:2 * c] = (dx * u_ref[:, 2 * c:3 * c].astype(F32)).astype(BF16)
        du_ref[:, 2 * c:3 * c] = (dx * u_ref[:, c:2 * c].astype(F32)).astype(BF16)

    row = pl.BlockSpec((tr, c), lambda i: (i, 0))
    wide = pl.BlockSpec((tr, 3 * c), lambda i: (i, 0))
    return pl.pallas_call(
        body, name=name, grid=(t // tr,), in_specs=[wide, row, row, row], out_specs=wide,
        out_shape=jax.ShapeDtypeStruct((t, 3 * c), BF16),
        compiler_params=_cp("parallel"))(u, dy, co, dxc)


def _local_step(x, tgt, seq, small, get_w, put_g, sync):
    t, d = x.shape
    c = d // 2
    cw_e, cw_o = small["conv_w_e"], small["conv_w_o"]
    wp = small["w_pool_e"].astype(BF16)
    ptaps = _pool_taps(c)
    row = lambda v: v.reshape(1, -1)

    we = {"w_in": get_w("in_e", x)[0]}
    n0 = _rms_fwd("rms_fwd_mix0", x, row(small["mix_norm_e"]))
    u0 = _mm_nn("mm_in_e", n0, we["w_in"], BF16)
    sync("fwd_a", u0)
    a2 = _conv_fwd("conv_e_fwd", seq, c, cw_e, u0, 0, u0, c, "glu", bias=row(small["conv_b_e"]))
    ws = _conv_fwd("pool_fwd", seq, c, ptaps, u0, 2 * c)
    z0, pm = _even_fwd("even_fwd", seq, a2, ws, u0, row(small["ln_g_e"]), row(small["ln_b_e"]),
                       wp, row(small["pool_scale_e"]))
    we["w_out"] = get_w("out_e", z0)[0]
    h1 = _mm_nn("mm_out_e", z0, we["w_out"], F32, res=x)
    sync("fwd_b", h1)
    n1 = _rms_fwd("rms_fwd_ffn0", h1, row(small["ffn_norm"][0]))
    wf0 = dict(zip(("w_gate", "w_up"), get_w("gu0", n1)))
    g0, up0, act0 = _ffn_fwd("ffn0_fwd", n1, wf0["w_gate"], wf0["w_up"])
    dep = sync("fwd_c", act0)
    wf0["w_down"] = get_w("down0", act0)[0]
    h2 = _mm_down("mm_down0", act0, wf0["w_down"], h1, dep=dep)
    dep = sync("fwd_d", h2)
    n2 = _rms_fwd("rms_fwd_mix1", h2, row(small["mix_norm_o"]))
    wo = {"w_in": get_w("in_o", n2)[0]}
    u1 = _mm_nn("mm_in_o", n2, wo["w_in"], BF16, dep=dep)
    co, y1 = _conv_fwd("conv_o_fwd", seq, d, cw_o, u1, d, u1, 2 * d, "mul", post=u1, cpost=0)
    dep = sync("fwd_e", y1)
    wo["w_out"] = get_w("out_o", y1)[0]
    h3 = _mm_nn("mm_out_o", y1, wo["w_out"], F32, res=h2, dep=dep)
    sync("fwd_f", h3)
    n3 = _rms_fwd("rms_fwd_ffn1", h3, row(small["ffn_norm"][1]))
    wf1 = dict(zip(("w_gate", "w_up"), get_w("gu1", n3)))
    g1, up1, act1 = _ffn_fwd("ffn1_fwd", n3, wf1["w_gate"], wf1["w_up"])
    wf1["w_down"] = get_w("down1", act1)[0]
    h4 = _mm_down("mm_down1", act1, wf1["w_down"], h3)

    dh4, dh4b, d_final, lsum = _loss_head("loss_head", h4, row(small["final_norm"]), tgt)

    def ffn_bwd(tag, dh, dhb, h_in, gain, n, g, up, act, w, dep):
        dg, dup = _ffn_bwd_act("ffn%s_bwd_act" % tag, dhb, w["w_down"], g, up, dep=dep)
        dwd = _mm_dwd("mm_dwd%s" % tag, act, dhb)
        dwg = _mm_dwd("mm_dwg%s" % tag, dg, n, dep=sync("bwd_ffn" + tag, dwd))
        dwu = _mm_dwd("mm_dwu%s" % tag, dup, n)
        dn = _mm_ffn_dn("mm_ffn_dn%s" % tag, dg, w["w_gate"], dup, w["w_up"])
        dh_in, dhb_in, dgain = _rms_bwd("rms_bwd_ffn%s" % tag, h_in, gain, dn, dh)
        dep = put_g("ffn" + tag, {"w_gate": dwg, "w_up": dwu, "w_down": dwd})
        return dh_in, dhb_in, dgain, dep

    dh3, dh3b, d_ffn1, dep = ffn_bwd("1", dh4, dh4b, h3, row(small["ffn_norm"][1]), n3, g1, up1,
                                     act1, wf1, None)
    dw_out_o = _mm_tn("mm_dw_out_o", y1, dh3b, BF16, dep=dep)
    dy1 = _mm_nt("mm_dy_o", dh3b, wo["w_out"], BF16, dep=sync("bwd_mix_o", dw_out_o))
    dxc, dcw_o = _conv_bwd("conv_o_bwd", seq, d, cw_o, dy1, 0, u1, 0, "mul",
                           x1=u1, c1=d, x2=u1, c2=2 * d, pre="mul")
    du1 = _odd_du("odd_du", u1, dy1, co, dxc)
    dw_in_o = _mm_tn("mm_dw_in_o", n2, du1, BF16)
    dn2 = _mm_nt("mm_dn_o", du1, wo["w_in"], BF16)
    dh2, dh2b, d_mix_o = _rms_bwd("rms_bwd_mix1", h2, row(small["mix_norm_o"]), dn2, dh3)
    dep = put_g("mix_o", {"w_in": dw_in_o, "w_out": dw_out_o})

    dh1, dh1b, d_ffn0, dep = ffn_bwd("0", dh2, dh2b, h1, row(small["ffn_norm"][0]), n1, g0, up0,
                                     act0, wf0, dep)
    dw_out_e = _mm_tn("mm_dw_out_e", z0, dh1b, BF16, dep=dep)
    dz0 = _mm_nt("mm_dz_e", dh1b, we["w_out"], BF16, dep=sync("bwd_mix_e", dw_out_e))
    da2, dws, dpm, vecs, dwp = _even_bwd("even_bwd", seq, dz0, a2, pm, row(small["ln_g_e"]),
                                         row(small["ln_b_e"]), wp, row(small["pool_scale_e"]))
    da1, dcw_e = _conv_bwd("conv_e_bwd", seq, c, cw_e, da2, 0, x1=u0, c1=0, x2=u0, c2=c, pre="glu")
    dbp = _conv_bwd("pool_bwd", seq, c, ptaps, dws, 0)
    du0 = _even_du("even_du", u0, da1, dbp, dpm)
    dw_in_e = _mm_tn("mm_dw_in_e", n0, du0, BF16)
    dep = put_g("mix_e", {"w_in": dw_in_e, "w_out": dw_out_e})
    dn0 = _mm_nt("mm_dn_e", du0, we["w_in"], BF16, dep=dep)
    dx, _, d_mix_e = _rms_bwd("rms_bwd_mix0", x, row(small["mix_norm_e"]), dn0, dh1)

    gsmall = {"mix_norm_e": d_mix_e[0], "conv_w_e": dcw_e, "conv_b_e": vecs[2], "ln_g_e": vecs[0],
              "ln_b_e": vecs[1], "w_pool_e": dwp, "pool_scale_e": vecs[3], "mix_norm_o": d_mix_o[0],
              "conv_w_o": dcw_o, "ffn_norm": jnp.concatenate([d_ffn0, d_ffn1], axis=0),
              "final_norm": d_final[0]}
    return lsum, dx, gsmall


def _place():
    x, y, c = (lax.axis_index(a) for a in MESH_AXES)
    return x, y, c


def _index(p):
    return 4 * p[0] + 2 * p[1] + p[2]


def _slab(ref, kind, d, n):
    if kind == "blk":
        return ref.at[d]
    return ref.at[:, pl.ds(pl.multiple_of(d * n, LANES), n)]


def _all_gather(name, shards, kinds):
    na = len(shards)

    def body(*refs):
        x_refs, o_refs = refs[:na], refs[na:2 * na]
        send_sems, recv_sems, local_sems = refs[2 * na:]
        x, y, c = _place()
        me, sib = (x, y, c), (x, y, 1 - c)
        chips = [(1 - x, y), (x, 1 - y), (1 - x, 1 - y)]

        def slot(a, p):
            return _slab(o_refs[a], kinds[a], _index(p), shards[a].shape[1])

        def copy(a, k, block, to, src=None):
            return pltpu.make_async_remote_copy(
                src_ref=slot(a, block) if src is None else src, dst_ref=slot(a, block),
                send_sem=send_sems.at[a, k], recv_sem=recv_sems.at[a, k],
                device_id=to, device_id_type=MESH)

        mine = [pltpu.make_async_copy(x_refs[a], slot(a, me), local_sems.at[a]) for a in range(na)]
        for cp in mine:
            cp.start()
        first = []
        for a in range(na):
            first.append(copy(a, 0, me, sib, src=x_refs[a]))
            first += [copy(a, 1 + j, me, (*chip, c), src=x_refs[a]) for j, chip in enumerate(chips)]
        for cp in first:
            cp.start()
        passed = []
        for j, chip in enumerate(chips):
            for a in range(na):
                copy(a, 1 + j, (*chip, c), me).wait_recv()
                fwd = copy(a, 4 + j, (*chip, c), sib)
                fwd.start()
                passed.append(fwd)
        for a in range(na):
            copy(a, 0, sib, me).wait_recv()
            for j, chip in enumerate(chips):
                copy(a, 4 + j, (*chip, 1 - c), me).wait_recv()
        for cp in first + passed:
            cp.wait_send()
        for cp in mine:
            cp.wait()

    shapes = []
    for s, kind in zip(shards, kinds):
        m, n = s.shape
        shapes.append(jax.ShapeDtypeStruct((NDEV, m, n) if kind == "blk" else (m, NDEV * n), s.dtype))
    return pl.pallas_call(
        body, name=name, in_specs=[ANY] * na, out_specs=[ANY] * na, out_shape=shapes,
        scratch_shapes=[pltpu.SemaphoreType.DMA((na, 7)), pltpu.SemaphoreType.DMA((na, 7)),
                        pltpu.SemaphoreType.DMA((na,))])(*shards)


HBM = pl.BlockSpec(memory_space=pltpu.HBM)
SEM = pl.BlockSpec(memory_space=pltpu.SEMAPHORE)
EFFECT = pltpu.SideEffectType.DATAFLOW_SIDE_EFFECTING
NCHIPS = 4


def _in_hbm(a):
    return pltpu.with_memory_space_constraint(a, pltpu.HBM)


def _gathered_shape(s, kind):
    m, n = s.shape
    return (NDEV, m, n) if kind == "blk" else (m, NDEV * n)


def _first_targets():
    x, y, c = _place()
    return [(x, y, 1 - c), (1 - x, y, c), (x, 1 - y, c), (1 - x, 1 - y, c)]


def _gather_start(name, shards, kinds, after):
    na = len(shards)

    def body(*refs):
        x_refs, land_refs = refs[:na], refs[na:2 * na]
        send_sems, recv_sems = refs[2 * na + 1], refs[2 * na + 2]
        token = refs[-1]
        me = _index(_place())
        for a in range(na):
            for k, to in enumerate(_first_targets()):
                pltpu.make_async_remote_copy(
                    src_ref=x_refs[a], dst_ref=_slab(land_refs[a], kinds[a], me, shards[a].shape[1]),
                    send_sem=send_sems.at[4 * a + k], recv_sem=recv_sems.at[4 * a + k],
                    device_id=to, device_id_type=MESH).start()
        token[...] = jnp.zeros_like(token)

    lands = [lax.empty(_gathered_shape(s, k), s.dtype) for s, k in zip(shards, kinds)]
    outs = pl.pallas_call(
        body, name=name,
        out_shape=(pltpu.SemaphoreType.DMA((4 * na,)), pltpu.SemaphoreType.DMA((4 * na,)),
                   *[pltpu.HBM(s.shape, s.dtype) for s in shards],
                   *[pltpu.HBM(l.shape, l.dtype) for l in lands],
                   jax.ShapeDtypeStruct((SUBLANES, LANES), F32)),
        in_specs=[HBM] * (2 * na) + [ANY],
        out_specs=(SEM, SEM, *[HBM] * (2 * na), pl.BlockSpec(memory_space=pltpu.VMEM)),
        input_output_aliases={i: 2 + i for i in range(2 * na)},
        compiler_params=pltpu.CompilerParams(has_side_effects=EFFECT),
    )(*[_in_hbm(s) for s in shards], *[_in_hbm(l) for l in lands], after)
    return outs[0], outs[1], outs[2:2 + na], outs[2 + na:2 + 2 * na], outs[-1]


def _gather_wait(name, started, kinds, after):
    send_sems, recv_sems, shards, lands, _ = started
    na = len(shards)

    def body(*refs):
        x_refs, land_refs = refs[:na], refs[na:2 * na]
        s_sems, r_sems = refs[2 * na], refs[2 * na + 1]
        for a in range(na):
            for k, frm in enumerate(_first_targets()):
                cp = pltpu.make_async_remote_copy(
                    src_ref=x_refs[a],
                    dst_ref=_slab(land_refs[a], kinds[a], _index(frm), shards[a].shape[1]),
                    send_sem=s_sems.at[4 * a + k], recv_sem=r_sems.at[4 * a + k],
                    device_id=frm, device_id_type=MESH)
                cp.wait_send()
                cp.wait_recv()

    outs = pl.pallas_call(
        body, name=name,
        out_shape=(*[pltpu.HBM(s.shape, s.dtype) for s in shards],
                   *[pltpu.HBM(l.shape, l.dtype) for l in lands]),
        in_specs=[HBM] * (2 * na) + [SEM, SEM, ANY], out_specs=[HBM] * (2 * na),
        input_output_aliases={i: i for i in range(2 * na)},
        compiler_params=pltpu.CompilerParams(has_side_effects=EFFECT),
    )(*shards, *lands, send_sems, recv_sems, after)
    return outs[:na], outs[na:]


def _split_start(name, bufs, ncopies, plan, after):
    nb = len(bufs)

    def body(*refs):
        send_sems, recv_sems, token = refs[nb + 1], refs[nb + 2], refs[-1]
        for k, (src, dst, to, _) in enumerate(plan(refs[:nb])):
            pltpu.make_async_remote_copy(src_ref=src, dst_ref=dst, send_sem=send_sems.at[k],
                                         recv_sem=recv_sems.at[k], device_id=to, device_id_type=MESH).start()
        token[...] = jnp.zeros_like(token)

    outs = pl.pallas_call(
        body, name=name,
        out_shape=(pltpu.SemaphoreType.DMA((ncopies,)), pltpu.SemaphoreType.DMA((ncopies,)),
                   *[pltpu.HBM(b.shape, b.dtype) for b in bufs],
                   jax.ShapeDtypeStruct((SUBLANES, LANES), F32)),
        in_specs=[HBM] * nb + [ANY],
        out_specs=(SEM, SEM, *[HBM] * nb, pl.BlockSpec(memory_space=pltpu.VMEM)),
        input_output_aliases={i: 2 + i for i in range(nb)},
        compiler_params=pltpu.CompilerParams(has_side_effects=EFFECT),
    )(*[_in_hbm(b) for b in bufs], after)
    return outs[0], outs[1], list(outs[2:2 + nb]), outs[-1]


def _split_wait(name, started, plan, after):
    send_sems, recv_sems, bufs, _ = started
    nb = len(bufs)

    def body(*refs):
        s_sems, r_sems = refs[nb], refs[nb + 1]
        for k, (src, _, to, landed) in enumerate(plan(refs[:nb])):
            cp = pltpu.make_async_remote_copy(src_ref=src, dst_ref=landed, send_sem=s_sems.at[k],
                                              recv_sem=r_sems.at[k], device_id=to, device_id_type=MESH)
            cp.wait_send()
            cp.wait_recv()

    outs = pl.pallas_call(
        body, name=name, out_shape=tuple(pltpu.HBM(b.shape, b.dtype) for b in bufs),
        in_specs=[HBM] * nb + [SEM, SEM, ANY], out_specs=[HBM] * nb,
        input_output_aliases={i: i for i in range(nb)},
        compiler_params=pltpu.CompilerParams(has_side_effects=EFFECT),
    )(*bufs, send_sems, recv_sems, after)
    return list(outs)


def _forward_plan(kinds, nloc):
    def plan(lands):
        x, y, c = _place()
        out = []
        for a, land in enumerate(lands):
            for chip in [(1 - x, y), (x, 1 - y), (1 - x, 1 - y)]:
                mine = _slab(land, kinds[a], _index((*chip, c)), nloc[a])
                out.append((mine, mine, (x, y, 1 - c), _slab(land, kinds[a], _index((*chip, 1 - c)), nloc[a])))
        return out
    return plan


def _own_copy(name, shard, land, kind, me):
    m, n = shard.shape
    tr = _tile(m, max(SUBLANES, 1048576 // n), SUBLANES)

    def body(s_ref, x_ref, land_ref, o_ref):
        o_ref[...] = x_ref[...]

    if kind == "blk":
        o_spec = pl.BlockSpec((None, tr, n), lambda i, s: (s[0], i, 0))
    else:
        o_spec = pl.BlockSpec((tr, n), lambda i, s: (i, s[0]))
    return pl.pallas_call(
        body, name=name,
        grid_spec=pltpu.PrefetchScalarGridSpec(
            num_scalar_prefetch=1, grid=(m // tr,),
            in_specs=[pl.BlockSpec((tr, n), lambda i, s: (i, 0)), ANY], out_specs=o_spec),
        out_shape=jax.ShapeDtypeStruct(land.shape, land.dtype),
        input_output_aliases={2: 0}, compiler_params=_cp("parallel"))(me, shard, land)


def _everyone_plan(refs):
    x, y, c = _place()
    out = []
    for dx, dy, dc in [(a, b, e) for a in (0, 1) for b in (0, 1) for e in (0, 1)][1:]:
        peer = (x ^ dx, y ^ dy, c ^ dc)
        out.append((refs[0], refs[1].at[_index((x, y, c))], peer, refs[1].at[_index(peer)]))
    return out


def _pair_plan(kinds, nloc):
    na = len(kinds)

    def plan(refs):
        x, y, c = _place()
        out = []
        for a in range(na):
            for j in range(NCHIPS):
                dst = refs[na + a].at[j]
                out.append((_slab(refs[a], kinds[a], 2 * j + (1 - c), nloc[a]), dst, (x, y, 1 - c), dst))
        return out
    return plan


def _chip_sum(name, full, kind, n, from_sib, place):
    _, m, _ = from_sib.shape
    tr = _tile(m, max(SUBLANES, 1048576 // n), SUBLANES)

    def body(s_ref, mine_ref, sib_ref, csum_ref, land_ref):
        v = (mine_ref[...].astype(F32) + sib_ref[...].astype(F32)).astype(csum_ref.dtype)
        csum_ref[...] = v

        @pl.when(pl.program_id(1) == s_ref[1])
        def _():
            land_ref[...] = v

    if kind == "blk":
        mine_spec = pl.BlockSpec((None, tr, n), lambda i, j, s: (2 * j + s[0], i, 0))
    else:
        mine_spec = pl.BlockSpec((tr, n), lambda i, j, s: (i, 2 * j + s[0]))
    slot = pl.BlockSpec((None, tr, n), lambda i, j, s: (j, i, 0))
    shp = jax.ShapeDtypeStruct((NCHIPS, m, n), from_sib.dtype)
    return pl.pallas_call(
        body, name=name,
        grid_spec=pltpu.PrefetchScalarGridSpec(
            num_scalar_prefetch=1, grid=(m // tr, NCHIPS), in_specs=[mine_spec, slot],
            out_specs=[slot, pl.BlockSpec((None, tr, n), lambda i, j, s: (s[1], i, 0))]),
        out_shape=[shp, shp], compiler_params=_cp("parallel", "arbitrary"))(place, full, from_sib)


def _other_chips():
    x, y, c = _place()
    return [(1 - x, y, c), (x, 1 - y, c), (1 - x, 1 - y, c)]


def _scatter_start(name, csums, lands, after):
    na = len(csums)

    def body(*refs):
        c_refs, land_refs = refs[:na], refs[na:2 * na]
        send_sems, recv_sems = refs[2 * na + 1], refs[2 * na + 2]
        token = refs[-1]
        x, y, _ = _place()
        for a in range(na):
            for k, to in enumerate(_other_chips()):
                pltpu.make_async_remote_copy(
                    src_ref=c_refs[a].at[2 * to[0] + to[1]], dst_ref=land_refs[a].at[2 * x + y],
                    send_sem=send_sems.at[3 * a + k], recv_sem=recv_sems.at[3 * a + k],
                    device_id=to, device_id_type=MESH).start()
        token[...] = jnp.zeros_like(token)

    outs = pl.pallas_call(
        body, name=name,
        out_shape=(pltpu.SemaphoreType.DMA((3 * na,)), pltpu.SemaphoreType.DMA((3 * na,)),
                   *[pltpu.HBM(s.shape, s.dtype) for s in csums],
                   *[pltpu.HBM(l.shape, l.dtype) for l in lands],
                   jax.ShapeDtypeStruct((SUBLANES, LANES), F32)),
        in_specs=[HBM] * (2 * na) + [ANY],
        out_specs=(SEM, SEM, *[HBM] * (2 * na), pl.BlockSpec(memory_space=pltpu.VMEM)),
        input_output_aliases={i: 2 + i for i in range(2 * na)},
        compiler_params=pltpu.CompilerParams(has_side_effects=EFFECT),
    )(*[_in_hbm(s) for s in csums], *[_in_hbm(l) for l in lands], after)
    return outs[0], outs[1], outs[2:2 + na], outs[2 + na:2 + 2 * na], outs[-1]


def _scatter_wait(name, started, after):
    send_sems, recv_sems, csums, lands, _ = started
    na = len(csums)

    def body(*refs):
        c_refs, land_refs = refs[:na], refs[na:2 * na]
        s_sems, r_sems = refs[2 * na], refs[2 * na + 1]
        for a in range(na):
            for k, frm in enumerate(_other_chips()):
                cp = pltpu.make_async_remote_copy(
                    src_ref=c_refs[a].at[2 * frm[0] + frm[1]], dst_ref=land_refs[a].at[2 * frm[0] + frm[1]],
                    send_sem=s_sems.at[3 * a + k], recv_sem=r_sems.at[3 * a + k],
                    device_id=frm, device_id_type=MESH)
                cp.wait_send()
                cp.wait_recv()

    outs = pl.pallas_call(
        body, name=name,
        out_shape=(*[pltpu.HBM(s.shape, s.dtype) for s in csums],
                   *[pltpu.HBM(l.shape, l.dtype) for l in lands]),
        in_specs=[HBM] * (2 * na) + [SEM, SEM, ANY], out_specs=[HBM] * (2 * na),
        input_output_aliases={i: i for i in range(2 * na)},
        compiler_params=pltpu.CompilerParams(has_side_effects=EFFECT),
    )(*csums, *lands, send_sems, recv_sems, after)
    return outs[na:]


def _adam_math(w, g, m, v):
    m = ADAM_B1 * m + (1.0 - ADAM_B1) * g
    v = ADAM_B2 * v + (1.0 - ADAM_B2) * (g * g)
    m_hat = m / (1.0 - ADAM_B1 ** ADAM_STEP)
    v_hat = v / (1.0 - ADAM_B2 ** ADAM_STEP)
    delta = -ADAM_LR * (m_hat / (jnp.sqrt(v_hat) + ADAM_EPS) + ADAM_WD * w)
    return delta, m, v


def _sum_adamw(name, parts, w, m, v, layer, prev=None, dep=None):
    nl, r, c = w.shape
    nparts = parts.shape[0]
    tr = _tile(r, max(SUBLANES, 262144 // c), SUBLANES)

    def body(p_ref, w_ref, m_ref, v_ref, *rest):
        g_ref, d_ref, mo_ref, vo_ref = rest[-4:]
        g = p_ref[0].astype(F32)
        for s in range(1, nparts):
            g = g + p_ref[s].astype(F32)
        delta, mn, vn = _adam_math(w_ref[...], g, m_ref[...], v_ref[...])
        g_ref[...] = g
        d_ref[...] = delta
        mo_ref[...] = mn
        vo_ref[...] = vn

    row = pl.BlockSpec((None, tr, c), lambda i: (layer, i, 0))
    shp = jax.ShapeDtypeStruct((nl, r, c), F32)
    extra = ([] if prev is None else list(prev)) + ([] if dep is None else [dep])
    return pl.pallas_call(
        body, name=name, grid=(r // tr,),
        in_specs=[pl.BlockSpec((nparts, tr, c), lambda i: (0, i, 0)), row, row, row] + [ANY] * len(extra),
        out_specs=[row, row, row, row], out_shape=[shp, shp, shp, shp],
        input_output_aliases={} if prev is None else {4 + i: i for i in range(4)},
        compiler_params=_cp("parallel"))(parts, w, m, v, *extra)


def _sum_parts(name, parts):
    _, r, c = parts.shape

    def body(p_ref, o_ref):
        g = p_ref[0]
        for s in range(1, NDEV):
            g = g + p_ref[s]
        o_ref[...] = g

    return pl.pallas_call(
        body, name=name, grid=(1,),
        in_specs=[pl.BlockSpec((NDEV, r, c), lambda i: (0, 0, 0))],
        out_specs=pl.BlockSpec((r, c), lambda i: (0, 0)),
        out_shape=jax.ShapeDtypeStruct((r, c), F32), compiler_params=_cp("arbitrary"))(parts)


def _adamw(name, w, g, m, v):
    r, c = w.shape

    def body(w_ref, g_ref, m_ref, v_ref, d_ref, mo_ref, vo_ref):
        delta, mn, vn = _adam_math(w_ref[...], g_ref[...], m_ref[...], v_ref[...])
        d_ref[...] = delta
        mo_ref[...] = mn
        vo_ref[...] = vn

    full = pl.BlockSpec((r, c), lambda i: (0, 0))
    shp = jax.ShapeDtypeStruct((r, c), F32)
    return pl.pallas_call(
        body, name=name, grid=(1,), in_specs=[full] * 4, out_specs=[full] * 3,
        out_shape=[shp] * 3, compiler_params=_cp("arbitrary"))(w, g, m, v)


def _pack(arrays):
    flat = jnp.concatenate([a.reshape(-1) for a in arrays])
    unit = SUBLANES * LANES
    pad = (-flat.shape[0]) % unit
    return jnp.pad(flat, (0, pad)).reshape(-1, LANES)


def _unpack(buf, shapes):
    flat = buf.reshape(-1)
    out, off = [], 0
    for shp in shapes:
        size = 1
        for s in shp:
            size *= s
        out.append(flat[off:off + size].reshape(shp))
        off += size
    return out


WEIGHTS = ["mix_norm_e", "w_in_e", "conv_w_e", "conv_b_e", "ln_g_e", "ln_b_e", "w_pool_e",
           "pool_scale_e", "w_out_e", "mix_norm_o", "w_in_o", "conv_w_o", "w_out_o", "ffn_norm",
           "w_gate", "w_up", "w_down", "final_norm"]
BIG = ["w_in_e", "w_out_e", "w_in_o", "w_out_o", "w_gate", "w_up", "w_down"]
SHARDED_SMALL = {"conv_w_e": 1, "w_pool_e": 1, "mix_norm_o": 0, "conv_w_o": 1}
SMALL = [n for n in WEIGHTS if n not in BIG]


def kernel(x, mix_norm_e, w_in_e, conv_w_e, conv_b_e, ln_g_e, ln_b_e, w_pool_e, pool_scale_e, w_out_e, mix_norm_o, w_in_o, conv_w_o, w_out_o, ffn_norm, w_gate, w_up, w_down, final_norm, loss_target, m_mix_norm_e, m_w_in_e, m_conv_w_e, m_conv_b_e, m_ln_g_e, m_ln_b_e, m_w_pool_e, m_pool_scale_e, m_w_out_e, m_mix_norm_o, m_w_in_o, m_conv_w_o, m_w_out_o, m_ffn_norm, m_w_gate, m_w_up, m_w_down, m_final_norm, v_mix_norm_e, v_w_in_e, v_conv_w_e, v_conv_b_e, v_ln_g_e, v_ln_b_e, v_w_pool_e, v_pool_scale_e, v_w_out_e, v_mix_norm_o, v_w_in_o, v_conv_w_o, v_w_out_o, v_ffn_norm, v_w_gate, v_w_up, v_w_down, v_final_norm):
    wts = dict(zip(WEIGHTS, [mix_norm_e, w_in_e, conv_w_e, conv_b_e, ln_g_e, ln_b_e, w_pool_e, pool_scale_e, w_out_e, mix_norm_o, w_in_o, conv_w_o, w_out_o, ffn_norm, w_gate, w_up, w_down, final_norm]))
    mom = dict(zip(WEIGHTS, [m_mix_norm_e, m_w_in_e, m_conv_w_e, m_conv_b_e, m_ln_g_e, m_ln_b_e, m_w_pool_e, m_pool_scale_e, m_w_out_e, m_mix_norm_o, m_w_in_o, m_conv_w_o, m_w_out_o, m_ffn_norm, m_w_gate, m_w_up, m_w_down, m_final_norm]))
    var = dict(zip(WEIGHTS, [v_mix_norm_e, v_w_in_e, v_conv_w_e, v_conv_b_e, v_ln_g_e, v_ln_b_e, v_w_pool_e, v_pool_scale_e, v_w_out_e, v_mix_norm_o, v_w_in_o, v_conv_w_o, v_w_out_o, v_ffn_norm, v_w_gate, v_w_up, v_w_down, v_final_norm]))
    bsz, seq, d = x.shape
    t = bsz * seq
    me = _index(_place())
    me_arr = jnp.reshape(me, (1,)).astype(jnp.int32)

    sh_names = list(SHARDED_SMALL)
    sh_local = [wts[n][0] for n in sh_names]
    packed = _pack(sh_local)
    gathered, = _all_gather("gather_small", [packed], ["blk"])
    small = {n: wts[n][0] for n in SMALL if n not in SHARDED_SMALL and n not in ("ffn_norm", "final_norm")}
    small["ffn_norm"], small["final_norm"] = ffn_norm, final_norm
    per_dev = [_unpack(gathered[s], [a.shape for a in sh_local]) for s in range(NDEV)]
    for i, n in enumerate(sh_names):
        small[n] = jnp.concatenate([per_dev[s][i] for s in range(NDEV)], axis=SHARDED_SMALL[n])

    for state in (wts, mom, var):
        for n in ("w_gate", "w_up"):
            state[n] = jnp.swapaxes(state[n], 1, 2)
    bf = lambda a: a.astype(BF16)
    mix_kinds, ffn_kinds = ["col", "blk"], ["blk", "blk", "blk"]
    ffn_names = ("w_gate", "w_up", "w_down")
    groups = {
        "mix_e": ([w_in_e.shape[2], d], mix_kinds, [("w_in_e", 0), ("w_out_e", 0)]),
        "ffn0": ([d, d, d], ffn_kinds, [(n, 0) for n in ffn_names]),
        "mix_o": ([w_in_o.shape[2], d], mix_kinds, [("w_in_o", 0), ("w_out_o", 0)]),
        "ffn1": ([d, d, d], ffn_kinds, [(n, 1) for n in ffn_names]),
    }
    gathers = {
        "in_e": ([bf(w_in_e[0])], ["col"]), "out_e": ([bf(w_out_e[0])], ["blk"]),
        "gu0": ([bf(wts["w_gate"][0]), bf(wts["w_up"][0])], ["blk", "blk"]), "down0": ([bf(w_down[0])], ["blk"]),
        "in_o": ([bf(w_in_o[0])], ["col"]), "out_o": ([bf(w_out_o[0])], ["blk"]),
        "gu1": ([bf(wts["w_gate"][1]), bf(wts["w_up"][1])], ["blk", "blk"]), "down1": ([bf(w_down[1])], ["blk"]),
    }
    started, prev = {}, gathered
    for grp, (shards, kinds) in gathers.items():
        started[grp] = _gather_start("gather_start_" + grp, shards, kinds, prev)
        prev = started[grp][4]
    all_started = prev[0, 0:1]

    passing, shards_of = {}, {}

    def pass_on(grp, after):
        shards, kinds = gathers[grp]
        shards_of[grp], lands = _gather_wait("gather_wait_" + grp, started[grp], kinds, after)
        plan = _forward_plan(kinds, [s.shape[1] for s in shards])
        passing[grp] = (_split_start("forward_start_" + grp, lands, 3 * len(lands), plan, after), plan)
        return passing[grp][0][3]

    def get_w(grp, after):
        if grp not in passing:
            after = pass_on(grp, after)
        st, plan = passing[grp]
        lands = _split_wait("forward_wait_" + grp, st, plan, after)
        full = [_own_copy("own_copy_%s%d" % (grp, a), shards_of[grp][a], lands[a], gathers[grp][1][a], me_arr)
                for a in range(len(lands))]
        return [full[0].reshape(-1, d)] if grp.startswith("out_") else full

    cx, cy, cc = _place()
    place = jnp.stack([cc, 2 * cx + cy]).astype(jnp.int32)
    bwd_order = ["ffn1", "mix_o", "ffn0", "mix_e"]
    pairing, pending, results = {}, {}, {}

    def put_g(grp, grads):
        nloc, kinds, _ = groups[grp]
        if len(kinds) == 2:
            fulls = [grads["w_in"], grads["w_out"].reshape(NDEV, -1, d)]
        else:
            fulls = [grads["w_gate"], grads["w_up"], grads["w_down"]]
        empties = []
        for g, kind, n in zip(fulls, kinds, nloc):
            empties.append(lax.empty((NCHIPS, g.shape[1] if kind == "blk" else g.shape[0], n), g.dtype))
        plan = _pair_plan(kinds, nloc)
        pairing[grp] = (_split_start("pair_start_" + grp, fulls + empties, NCHIPS * len(fulls), plan, place),
                        plan, kinds, nloc)
        token = pairing[grp][0][3]
        return send_sums(grp, token) if grp == bwd_order[-1] else token

    def send_sums(grp, after):
        st, plan, kinds, nloc = pairing[grp]
        bufs = _split_wait("pair_wait_" + grp, st, plan, after)
        na = len(kinds)
        sums = [_chip_sum("chip_sum_%s%d" % (grp, a), bufs[a], kinds[a], nloc[a], bufs[na + a], place)
                for a in range(na)]
        pending[grp] = _scatter_start("scatter_start_" + grp, [s[0] for s in sums], [s[1] for s in sums], after)
        return pending[grp][4]

    def finish(grp, after):
        lands = _scatter_wait("scatter_wait_" + grp, pending[grp], after)
        dep = None
        for (n, l), parts in zip(groups[grp][2], lands):
            results[n] = _sum_adamw("adamw_%s%d" % (n, l), parts, wts[n], mom[n], var[n], l, results.get(n), dep)
            dep = results[n][1]
        return dep

    fwd_sync = {"fwd_a": ["out_e"], "fwd_b": ["gu0"], "fwd_c": ["down0", "in_o"], "fwd_d": ["out_o"],
                "fwd_e": ["gu1"], "fwd_f": ["down1"]}

    def sync(tag, after):
        if tag in fwd_sync:
            for grp in fwd_sync[tag]:
                after = pass_on(grp, after)
            return after
        if tag == "bwd_mix_o":
            return send_sums("ffn1", after)
        if tag == "bwd_ffn0":
            return finish("ffn1", send_sums("mix_o", after))
        if tag == "bwd_mix_e":
            return finish("mix_o", send_sums("ffn0", after))
        return None

    small["mix_norm_e"] = small["mix_norm_e"] + all_started
    lsum, dx, gsmall = _local_step(x.reshape(t, d), loss_target.reshape(t, d), seq, small, get_w, put_g, sync)
    loss = lax.psum(jnp.sum(lsum), MESH_AXES)

    out_g, out_d, out_m, out_v = {}, {}, {}, {}

    gs_list = [gsmall[n] for n in SMALL]
    gs_mine = _pack(gs_list)
    small_st = _split_start("small_grads_start", [gs_mine, lax.empty((NDEV,) + gs_mine.shape, F32)], NDEV - 1,
                            _everyone_plan, dx)
    bufs = _split_wait("small_grads_wait", small_st, _everyone_plan, finish("ffn0", small_st[3]))
    gs_all = _own_copy("small_grads_own", bufs[0], bufs[1], "blk", me_arr)
    gs_sum = _unpack(_sum_parts("sum_small_grads", gs_all), [a.shape for a in gs_list])
    local_g = []
    for n, g in zip(SMALL, gs_sum):
        if n in SHARDED_SMALL:
            ax = SHARDED_SMALL[n]
            size = wts[n].shape[ax + 1]
            g = lax.dynamic_slice_in_dim(g, me * size, size, axis=ax)
        local_g.append(g.reshape(wts[n].shape))
    shapes = [wts[n].shape for n in SMALL]
    upd = _adamw("adamw_small", _pack([wts[n] for n in SMALL]), _pack(local_g),
                 _pack([mom[n] for n in SMALL]), _pack([var[n] for n in SMALL]))
    for i, outd in enumerate((out_d, out_m, out_v)):
        for n, a in zip(SMALL, _unpack(upd[i], shapes)):
            outd[n] = a
    for n, g in zip(SMALL, local_g):
        out_g[n] = g

    finish("mix_e", upd[0])
    for n in BIG:
        res = [jnp.swapaxes(a, 1, 2) for a in results[n]] if n in ("w_gate", "w_up") else results[n]
        out_g[n], out_d[n], out_m[n], out_v[n] = res

    return (loss, dx.reshape(bsz, seq, d), *[out_g[n] for n in WEIGHTS], *[out_d[n] for n in WEIGHTS],
            *[out_m[n] for n in WEIGHTS], *[out_v[n] for n in WEIGHTS])
```

```python
import functools

import jax
import jax.numpy as jnp
from jax import lax
from jax.experimental import pallas as pl
from jax.experimental.pallas import tpu as pltpu

F32 = jnp.float32
BF16 = jnp.bfloat16
NDEV = 8
MESH_AXES = ("x", "y", "c")
EPS = 1e-6
POOL_WINDOWS = (2, 4, 8, 16)
CONV_WIDTH = 31
SHORT_WIDTH = 3
ADAM_LR = 0.001
ADAM_B1 = 0.9
ADAM_B2 = 0.999
ADAM_EPS = 1e-08
ADAM_WD = 0.01
ADAM_STEP = 10
LANES = 128
SUBLANES = 8
VMEM_LIMIT = 56 * 1024 * 1024
MXU_DEPTH = 256
MM_TK = 2816
MESH = pl.DeviceIdType.MESH
ANY = pl.BlockSpec(memory_space=pl.ANY)


def _cp(*sem):
    return pltpu.CompilerParams(dimension_semantics=sem, vmem_limit_bytes=VMEM_LIMIT)


def _tile(n, pref, unit=LANES):
    if n <= pref:
        return n
    t = (pref // unit) * unit
    while t > unit and n % t:
        t -= unit
    assert n % t == 0, (n, pref)
    return t


def _sigmoid(v):
    return 0.5 * jnp.tanh(0.5 * v) + 0.5


def _mm(name, pairs, a_specs, b_specs, dims, out_shape, o_spec, grid, acc_shape,
        res=None, res_spec=None, dep=None):
    np_ = len(pairs)
    nk = grid[2]
    has_res = res is not None
    n_in = 2 * np_ + (1 if has_res else 0) + (0 if dep is None else 1)

    def body(*refs):
        a_refs = refs[:np_]
        b_refs = refs[np_:2 * np_]
        r_ref = refs[2 * np_] if has_res else None
        o_ref = refs[n_in]
        acc = refs[-1]

        def part():
            s = None
            for a_ref, b_ref in zip(a_refs, b_refs):
                blocks = [(a_ref[...], b_ref[...])] if len(a_ref.shape) == 2 else [
                    (a_ref[q], b_ref[q]) for q in range(a_ref.shape[0])]
                for av, bv in blocks:
                    d = lax.dot_general(av, bv, dims, preferred_element_type=F32)
                    s = d if s is None else s + d
            return s

        def finish(v):
            if has_res:
                v = v + r_ref[...]
            o_ref[...] = v.astype(o_ref.dtype)

        if nk == 1:
            finish(part())
        else:
            k = pl.program_id(2)

            @pl.when(k == 0)
            def _():
                acc[...] = part()

            @pl.when((k > 0) & (k < nk - 1))
            def _():
                acc[...] += part()

            @pl.when(k == nk - 1)
            def _():
                finish(acc[...] + part())

    ins = [p[0] for p in pairs] + [p[1] for p in pairs]
    specs = list(a_specs) + list(b_specs)
    if has_res:
        ins.append(res)
        specs.append(res_spec)
    if dep is not None:
        ins.append(dep)
        specs.append(ANY)
    return pl.pallas_call(
        body, name=name, grid=grid, in_specs=specs, out_specs=o_spec, out_shape=out_shape,
        scratch_shapes=[pltpu.VMEM(acc_shape if nk > 1 else (SUBLANES, LANES), F32)],
        compiler_params=_cp("parallel", "parallel", "arbitrary"))(*ins)


NN = (((1,), (0,)), ((), ()))
NT = (((1,), (1,)), ((), ()))
TN = (((0,), (0,)), ((), ()))


def _tiles_mk(m, kk, npairs=1):
    tk = _tile(kk, MM_TK, MXU_DEPTH)
    return _tile(m, 1024 if tk * npairs <= 2048 else 512), tk


def _mm_nn(name, a, b, out_dtype, res=None, dep=None):
    pairs = list(zip(a, b)) if isinstance(a, (list, tuple)) else [(a, b)]
    m, kk = pairs[0][0].shape
    n = pairs[0][1].shape[1]
    tm, tk = _tiles_mk(m, kk, len(pairs))
    tn = _tile(n, 1024)
    return _mm(name, pairs,
               [pl.BlockSpec((tm, tk), lambda i, j, k: (i, k))] * len(pairs),
               [pl.BlockSpec((tk, tn), lambda i, j, k: (k, j))] * len(pairs), NN,
               jax.ShapeDtypeStruct((m, n), out_dtype),
               pl.BlockSpec((tm, tn), lambda i, j, k: (i, j)),
               (m // tm, n // tn, kk // tk), (tm, tn), res,
               pl.BlockSpec((tm, tn), lambda i, j, k: (i, j)), dep=dep)


def _mm_nt(name, a, b, out_dtype, dep=None):
    m, n = a.shape
    kk = b.shape[0]
    tn = _tile(kk, 1024)
    tm, tk = _tiles_mk(m, n)
    return _mm(name, [(a, b)],
               [pl.BlockSpec((tm, tk), lambda i, j, k: (i, k))],
               [pl.BlockSpec((tn, tk), lambda i, j, k: (j, k))], NT,
               jax.ShapeDtypeStruct((m, kk), out_dtype),
               pl.BlockSpec((tm, tn), lambda i, j, k: (i, j)),
               (m // tm, kk // tn, n // tk), (tm, tn), dep=dep)


def _mm_tn(name, a, b, out_dtype, dep=None):
    t, m = a.shape
    n = b.shape[1]
    tn = _tile(n, 1024)
    tm, tk = _tile(m, 1024), _tile(t, MM_TK, MXU_DEPTH)
    return _mm(name, [(a, b)],
               [pl.BlockSpec((tk, tm), lambda i, j, k: (k, i))],
               [pl.BlockSpec((tk, tn), lambda i, j, k: (k, j))], TN,
               jax.ShapeDtypeStruct((m, n), out_dtype),
               pl.BlockSpec((tm, tn), lambda i, j, k: (i, j)),
               (m // tm, n // tn, t // tk), (tm, tn), dep=dep)


def _ffn_fwd(name, n, wg, wu):
    f, d = wg.shape
    t = n.shape[0]
    tm, tn = _tile(t, 1024), _tile(f, 512)

    def body(n_ref, wg_ref, wu_ref, act_ref, ds_ref, s_ref):
        nv = n_ref[...]
        g = lax.dot_general(nv, wg_ref[...], NT, preferred_element_type=F32)
        up = lax.dot_general(nv, wu_ref[...], NT, preferred_element_type=F32)
        sg = _sigmoid(g)
        silu = g * sg
        act_ref[...] = (silu * up).astype(BF16)
        ds_ref[...] = (up * (sg * (1.0 + g * (1.0 - sg)))).astype(BF16)
        s_ref[...] = silu.astype(BF16)

    w_spec = pl.BlockSpec((tn, d), lambda j, i: (j, 0))
    o_spec = pl.BlockSpec((tm, tn), lambda j, i: (i, j))
    shp = jax.ShapeDtypeStruct((t, f), BF16)
    return pl.pallas_call(
        body, name=name, grid=(f // tn, t // tm),
        in_specs=[pl.BlockSpec((tm, d), lambda j, i: (i, 0)), w_spec, w_spec],
        out_specs=[o_spec, o_spec, o_spec], out_shape=[shp, shp, shp],
        compiler_params=_cp("parallel", "parallel"))(n, wg, wu)


def _ffn_bwd_act(name, dh, wd, dsilu, silu, dep=None):
    f, d = wd.shape
    t = dh.shape[0]
    tm, tn = _tile(t, 1024), _tile(f, 512)

    def body(dh_ref, wd_ref, ds_ref, s_ref, *rest):
        dg_ref, dup_ref = rest[-2:]
        da = lax.dot_general(dh_ref[...], wd_ref[...], NT, preferred_element_type=F32)
        dg_ref[...] = (da * ds_ref[...].astype(F32)).astype(BF16)
        dup_ref[...] = (da * s_ref[...].astype(F32)).astype(BF16)

    o_spec = pl.BlockSpec((tm, tn), lambda j, i: (i, j))
    shp = jax.ShapeDtypeStruct((t, f), BF16)
    return pl.pallas_call(
        body, name=name, grid=(f // tn, t // tm),
        in_specs=[pl.BlockSpec((tm, d), lambda j, i: (i, 0)),
                  pl.BlockSpec((tn, d), lambda j, i: (j, 0)), o_spec, o_spec]
        + ([] if dep is None else [ANY]),
        out_specs=[o_spec, o_spec], out_shape=[shp, shp],
        compiler_params=_cp("parallel", "parallel"))(dh, wd, dsilu, silu, *([] if dep is None else [dep]))


def _rms_fwd(name, h, gain):
    t, d = h.shape
    tr = _tile(t, 512, SUBLANES)

    def body(h_ref, g_ref, n_ref):
        hv = h_ref[...]
        r = lax.rsqrt(jnp.mean(hv * hv, axis=-1, keepdims=True) + EPS)
        n_ref[...] = (hv * r * g_ref[...]).astype(BF16)

    return pl.pallas_call(
        body, name=name, grid=(t // tr,),
        in_specs=[pl.BlockSpec((tr, d), lambda i: (i, 0)), pl.BlockSpec((1, d), lambda i: (0, 0))],
        out_specs=pl.BlockSpec((tr, d), lambda i: (i, 0)),
        out_shape=jax.ShapeDtypeStruct((t, d), BF16),
        compiler_params=_cp("parallel"))(h, gain)


def _rms_bwd_math(hv, gain, dn):
    d = hv.shape[-1]
    r = lax.rsqrt(jnp.mean(hv * hv, axis=-1, keepdims=True) + EPS)
    xhat = hv * r
    dxh = dn * gain
    dh = r * (dxh - xhat * (jnp.sum(dxh * xhat, axis=-1, keepdims=True) / d))
    dgain = jnp.sum(dn * xhat, axis=0, keepdims=True)
    return dh, dgain


def _rms_bwd(name, h, gain, dn, dres):
    t, d = h.shape
    tr = _tile(t, 256, SUBLANES)

    def body(h_ref, g_ref, dn_ref, dr_ref, dh_ref, dhb_ref, dg_ref):
        dh, dgain = _rms_bwd_math(h_ref[...], g_ref[...], dn_ref[...].astype(F32))
        dh = dh + dr_ref[...]
        dh_ref[...] = dh
        dhb_ref[...] = dh.astype(BF16)

        @pl.when(pl.program_id(0) == 0)
        def _():
            dg_ref[...] = dgain

        @pl.when(pl.program_id(0) > 0)
        def _():
            dg_ref[...] += dgain

    row = pl.BlockSpec((tr, d), lambda i: (i, 0))
    vec = pl.BlockSpec((1, d), lambda i: (0, 0))
    return pl.pallas_call(
        body, name=name, grid=(t // tr,), in_specs=[row, vec, row, row],
        out_specs=[row, row, vec],
        out_shape=[jax.ShapeDtypeStruct((t, d), F32), jax.ShapeDtypeStruct((t, d), BF16),
                   jax.ShapeDtypeStruct((1, d), F32)],
        compiler_params=_cp("arbitrary"))(h, gain, dn, dres)


def _loss_head(name, h, gain, tgt):
    t, d = h.shape
    tr = _tile(t, 256, SUBLANES)

    def body(h_ref, g_ref, t_ref, dh_ref, dhb_ref, dg_ref, ls_ref):
        hv = h_ref[...]
        gv = g_ref[...]
        r = lax.rsqrt(jnp.mean(hv * hv, axis=-1, keepdims=True) + EPS)
        err = hv * r * gv - t_ref[...]
        lsum = 0.5 * jnp.sum(err * err, axis=0, keepdims=True) / d
        dh, dgain = _rms_bwd_math(hv, gv, err / d)
        dh_ref[...] = dh
        dhb_ref[...] = dh.astype(BF16)

        @pl.when(pl.program_id(0) == 0)
        def _():
            dg_ref[...] = dgain
            ls_ref[...] = lsum

        @pl.when(pl.program_id(0) > 0)
        def _():
            dg_ref[...] += dgain
            ls_ref[...] += lsum

    row = pl.BlockSpec((tr, d), lambda i: (i, 0))
    vec = pl.BlockSpec((1, d), lambda i: (0, 0))
    return pl.pallas_call(
        body, name=name, grid=(t // tr,), in_specs=[row, vec, row],
        out_specs=[row, row, vec, vec],
        out_shape=[jax.ShapeDtypeStruct((t, d), F32), jax.ShapeDtypeStruct((t, d), BF16),
                   jax.ShapeDtypeStruct((1, d), F32), jax.ShapeDtypeStruct((1, d), F32)],
        compiler_params=_cp("arbitrary"))(h, gain, tgt)


def _conv_geom(t, seq, c, k):
    halo = 32 if k - 1 > SUBLANES else SUBLANES
    assert k - 1 <= halo
    tm = min(256, seq // 2)
    tc = min(512, c)
    assert seq % tm == 0 and tm % halo == 0 and c % tc == 0 and t % seq == 0
    return halo, tm, tc, min(128, tm), min(LANES, tc)


def _pre(kind, a, b):
    if kind == "glu":
        return a * _sigmoid(b)
    if kind == "mul":
        return a * b
    return a


def _taps(k):
    return sorted((s % SUBLANES, s // SUBLANES, s) for s in range(k))


def _conv_fwd(name, seq, c, w, x1, c1, x2=None, c2=0, pre=None, bias=None, post=None, cpost=0):
    t = x1.shape[0]
    k = w.shape[0]
    halo, tm, tc, sr, sl = _conv_geom(t, seq, c, k)
    nb, cps = tm // halo, seq // tm
    two = x2 is not None
    has_bias, has_post = bias is not None, post is not None

    def body(*refs):
        it = iter(refs)
        x1c, x1h = next(it), next(it)
        x2c, x2h = (next(it), next(it)) if two else (None, None)
        w_ref = next(it)
        b_ref = next(it) if has_bias else None
        p_ref = next(it) if has_post else None
        o_ref = next(it)
        y_ref = next(it) if has_post else None
        xs = next(it)
        first = (pl.program_id(1) % cps) == 0
        hv = _pre(pre, x1h[...].astype(F32), x2h[...].astype(F32) if two else None)
        xs[0:halo, :] = jnp.where(first, 0.0, hv)
        xs[halo:halo + tm, :] = _pre(pre, x1c[...].astype(F32), x2c[...].astype(F32) if two else None)
        for l0 in range(0, tc, sl):
            ls = slice(l0, l0 + sl)
            for r0 in range(0, tm, sr):
                win = xs[r0:r0 + sr + halo, ls]
                acc = jnp.zeros((sr, sl), F32)
                rolled = {}
                for r, q, s in _taps(k):
                    if r not in rolled:
                        rolled[r] = win if r == 0 else pltpu.roll(win, r, 0)
                    lo = halo - SUBLANES * q
                    acc = acc + w_ref[k - 1 - s:k - s, ls] * rolled[r][lo:lo + sr]
                if has_bias:
                    acc = acc + b_ref[:, ls]
                o_ref[r0:r0 + sr, ls] = acc.astype(o_ref.dtype)
                if has_post:
                    y_ref[r0:r0 + sr, ls] = (acc * p_ref[r0:r0 + sr, ls].astype(F32)).astype(y_ref.dtype)

    def cur(off):
        return pl.BlockSpec((tm, tc), lambda j, i: (i, off // tc + j))

    def prev(off):
        return pl.BlockSpec((halo, tc), lambda j, i: (jnp.maximum(i * nb - 1, 0), off // tc + j))

    ins, specs = [x1, x1], [cur(c1), prev(c1)]
    if two:
        ins += [x2, x2]
        specs += [cur(c2), prev(c2)]
    ins.append(w)
    specs.append(pl.BlockSpec((k, tc), lambda j, i: (0, j)))
    if has_bias:
        ins.append(bias)
        specs.append(pl.BlockSpec((1, tc), lambda j, i: (0, j)))
    if has_post:
        ins.append(post)
        specs.append(cur(cpost))
    o_spec = pl.BlockSpec((tm, tc), lambda j, i: (i, j))
    shp = jax.ShapeDtypeStruct((t, c), BF16)
    return pl.pallas_call(
        body, name=name, grid=(c // tc, t // tm), in_specs=specs,
        out_specs=[o_spec, o_spec] if has_post else o_spec,
        out_shape=[shp, shp] if has_post else shp,
        scratch_shapes=[pltpu.VMEM((halo + tm, tc), F32)],
        compiler_params=_cp("parallel", "parallel"))(*ins)


def _conv_bwd(name, seq, c, w, d1, cd1, d2=None, cd2=0, dpre=None,
              x1=None, c1=0, x2=None, c2=0, pre=None):
    t = d1.shape[0]
    k = w.shape[0]
    halo, tm, tc, sr, sl = _conv_geom(t, seq, c, k)
    nb, cps = tm // halo, seq // tm
    nchunks = t // tm
    dtwo, xtwo, has_x = d2 is not None, x2 is not None, x1 is not None

    def body(*refs):
        it = iter(refs)
        d1c, d1n = next(it), next(it)
        d2c, d2n = (next(it), next(it)) if dtwo else (None, None)
        x1c, x1h = (next(it), next(it)) if has_x else (None, None)
        x2c, x2h = (next(it), next(it)) if xtwo else (None, None)
        w_ref = next(it)
        dx_ref = next(it)
        dw_ref = next(it) if has_x else None
        ds = next(it)
        xs = next(it) if has_x else None
        i = pl.program_id(1)
        last = (i % cps) == cps - 1
        ds[0:tm, :] = _pre(dpre, d1c[...].astype(F32), d2c[...].astype(F32) if dtwo else None)
        nv = _pre(dpre, d1n[...].astype(F32), d2n[...].astype(F32) if dtwo else None)
        ds[tm:tm + halo, :] = jnp.where(last, 0.0, nv)
        if has_x:
            first = (i % cps) == 0
            hv = _pre(pre, x1h[...].astype(F32), x2h[...].astype(F32) if xtwo else None)
            xs[0:halo, :] = jnp.where(first, 0.0, hv)
            xs[halo:halo + tm, :] = _pre(pre, x1c[...].astype(F32), x2c[...].astype(F32) if xtwo else None)

            @pl.when(i == 0)
            def _():
                dw_ref[...] = jnp.zeros_like(dw_ref)

        for l0 in range(0, tc, sl):
            ls = slice(l0, l0 + sl)
            for r0 in range(0, tm, sr):
                win = ds[r0:r0 + sr + halo, ls]
                nrow = sr + halo
                acc = jnp.zeros((sr, sl), F32)
                rolled = {}
                for r, q, s in _taps(k):
                    if r not in rolled:
                        rolled[r] = win if r == 0 else pltpu.roll(win, nrow - r, 0)
                    lo = SUBLANES * q
                    acc = acc + w_ref[k - 1 - s:k - s, ls] * rolled[r][lo:lo + sr]
                dx_ref[r0:r0 + sr, ls] = acc.astype(dx_ref.dtype)
                if has_x:
                    dcur = win[0:sr]
                    xwin = xs[r0:r0 + sr + halo, ls]
                    xrolled = {}
                    for r, q, s in _taps(k):
                        if r not in xrolled:
                            xrolled[r] = xwin if r == 0 else pltpu.roll(xwin, r, 0)
                        lo = halo - SUBLANES * q
                        part = jnp.sum(dcur * xrolled[r][lo:lo + sr], axis=0, keepdims=True)
                        dw_ref[k - 1 - s:k - s, ls] += part

    def cur(off):
        return pl.BlockSpec((tm, tc), lambda j, i: (i, off // tc + j))

    def prev(off):
        return pl.BlockSpec((halo, tc), lambda j, i: (jnp.maximum(i * nb - 1, 0), off // tc + j))

    def nxt(off):
        return pl.BlockSpec((halo, tc),
                            lambda j, i: (jnp.minimum((i + 1) * nb, nchunks * nb - 1), off // tc + j))

    ins, specs = [d1, d1], [cur(cd1), nxt(cd1)]
    if dtwo:
        ins += [d2, d2]
        specs += [cur(cd2), nxt(cd2)]
    if has_x:
        ins += [x1, x1]
        specs += [cur(c1), prev(c1)]
    if xtwo:
        ins += [x2, x2]
        specs += [cur(c2), prev(c2)]
    ins.append(w)
    specs.append(pl.BlockSpec((k, tc), lambda j, i: (0, j)))
    o_specs = [pl.BlockSpec((tm, tc), lambda j, i: (i, j))]
    o_shapes = [jax.ShapeDtypeStruct((t, c), BF16)]
    scratch = [pltpu.VMEM((tm + halo, tc), F32)]
    if has_x:
        o_specs.append(pl.BlockSpec((k, tc), lambda j, i: (0, j)))
        o_shapes.append(jax.ShapeDtypeStruct((k, c), F32))
        scratch.append(pltpu.VMEM((halo + tm, tc), F32))
    out = pl.pallas_call(
        body, name=name, grid=(c // tc, t // tm), in_specs=specs, out_specs=o_specs,
        out_shape=o_shapes, scratch_shapes=scratch,
        compiler_params=_cp("parallel", "arbitrary"))(*ins)
    return out if has_x else out[0]


def _pool_taps(c):
    kmax = max(POOL_WINDOWS)
    grp = c // len(POOL_WINDOWS)
    cols = []
    for wdw in POOL_WINDOWS:
        col = jnp.concatenate([jnp.zeros((kmax - wdw,), F32), jnp.ones((wdw,), F32)])
        cols.append(jnp.tile(col[:, None], (1, grp)))
    return jnp.concatenate(cols, axis=1)


def _counts(i, tr, seq, grp):
    pos = (i * tr + lax.broadcasted_iota(jnp.int32, (tr, 1), 0)) % seq + 1
    return [1.0 / jnp.minimum(pos, wdw).astype(F32) for wdw in POOL_WINDOWS]


def _ln_stats(a2):
    mu = jnp.mean(a2, axis=-1, keepdims=True)
    xc = a2 - mu
    rstd = lax.rsqrt(jnp.mean(xc * xc, axis=-1, keepdims=True) + EPS)
    return xc * rstd, rstd


def _even_fwd(name, seq, a2, ws, u, ln_g, ln_b, w_pool, scale):
    t, c = a2.shape
    ng = len(POOL_WINDOWS)
    grp = c // ng
    tr = _tile(t, 256, SUBLANES)

    def body(a_ref, ws_ref, b_ref, g_ref, bb_ref, wp_ref, sc_ref, z_ref, pm_ref):
        xhat, _ = _ln_stats(a_ref[...].astype(F32))
        l = xhat * g_ref[...] + bb_ref[...]
        z_ref[:, 0:c] = (l * _sigmoid(l)).astype(BF16)
        inv = _counts(pl.program_id(0), tr, seq, grp)
        for g in range(ng):
            gs = slice(g * grp, (g + 1) * grp)
            pm = (ws_ref[:, gs].astype(F32) * inv[g] - b_ref[:, gs].astype(F32)).astype(BF16)
            pm_ref[:, gs] = pm
            q = jnp.dot(pm, wp_ref[g], preferred_element_type=F32)
            z_ref[:, c + g * grp:c + (g + 1) * grp] = (q * sc_ref[:, gs]).astype(BF16)

    row = pl.BlockSpec((tr, c), lambda i: (i, 0))
    vec = pl.BlockSpec((1, c), lambda i: (0, 0))
    return pl.pallas_call(
        body, name=name, grid=(t // tr,),
        in_specs=[row, row, pl.BlockSpec((tr, c), lambda i: (i, 2)), vec, vec,
                  pl.BlockSpec((ng, grp, grp), lambda i: (0, 0, 0)), vec],
        out_specs=[pl.BlockSpec((tr, 2 * c), lambda i: (i, 0)), row],
        out_shape=[jax.ShapeDtypeStruct((t, 2 * c), BF16), jax.ShapeDtypeStruct((t, c), BF16)],
        compiler_params=_cp("parallel"))(a2, ws, u, ln_g, ln_b, w_pool, scale)


def _even_bwd(name, seq, dz, a2, pm, ln_g, ln_b, w_pool, scale):
    t, c = a2.shape
    ng = len(POOL_WINDOWS)
    grp = c // ng
    tr = _tile(t, 256, SUBLANES)

    def body(dz_ref, a_ref, pm_ref, g_ref, bb_ref, wp_ref, sc_ref,
             da_ref, dws_ref, dpm_ref, vec_ref, dwp_ref):
        i = pl.program_id(0)

        @pl.when(i == 0)
        def _():
            vec_ref[...] = jnp.zeros_like(vec_ref)
            dwp_ref[...] = jnp.zeros_like(dwp_ref)

        xhat, rstd = _ln_stats(a_ref[...].astype(F32))
        gv = g_ref[...]
        l = xhat * gv + bb_ref[...]
        sg = _sigmoid(l)
        dl = dz_ref[:, 0:c].astype(F32) * (sg * (1.0 + l * (1.0 - sg)))
        dxh = dl * gv
        da2 = rstd * (dxh - jnp.mean(dxh, axis=-1, keepdims=True)
                      - xhat * jnp.mean(dxh * xhat, axis=-1, keepdims=True))
        da_ref[...] = da2.astype(BF16)
        vec_ref[0:1, :] += jnp.sum(dl * xhat, axis=0, keepdims=True)
        vec_ref[1:2, :] += jnp.sum(dl, axis=0, keepdims=True)
        vec_ref[2:3, :] += jnp.sum(da2, axis=0, keepdims=True)
        inv = _counts(i, tr, seq, grp)
        for g in range(ng):
            gs = slice(g * grp, (g + 1) * grp)
            pmv = pm_ref[:, gs]
            wp = wp_ref[g]
            dp = dz_ref[:, c + g * grp:c + (g + 1) * grp].astype(F32)
            q = jnp.dot(pmv, wp, preferred_element_type=F32)
            vec_ref[3:4, gs] += jnp.sum(dp * q, axis=0, keepdims=True)
            dq = (dp * sc_ref[:, gs]).astype(BF16)
            dpm = lax.dot_general(dq, wp, NT, preferred_element_type=F32)
            dwp_ref[g] += lax.dot_general(pmv, dq, TN, preferred_element_type=F32)
            dpm_ref[:, gs] = dpm.astype(BF16)
            dws_ref[:, gs] = (dpm * inv[g]).astype(BF16)

    row = pl.BlockSpec((tr, c), lambda i: (i, 0))
    vec = pl.BlockSpec((1, c), lambda i: (0, 0))
    rshape = jax.ShapeDtypeStruct((t, c), BF16)
    return pl.pallas_call(
        body, name=name, grid=(t // tr,),
        in_specs=[pl.BlockSpec((tr, 2 * c), lambda i: (i, 0)), row, row, vec, vec,
                  pl.BlockSpec((ng, grp, grp), lambda i: (0, 0, 0)), vec],
        out_specs=[row, row, row, pl.BlockSpec((SUBLANES, c), lambda i: (0, 0)),
                   pl.BlockSpec((ng, grp, grp), lambda i: (0, 0, 0))],
        out_shape=[rshape, rshape, rshape, jax.ShapeDtypeStruct((SUBLANES, c), F32),
                   jax.ShapeDtypeStruct((ng, grp, grp), F32)],
        compiler_params=_cp("arbitrary"))(dz, a2, pm, ln_g, ln_b, w_pool, scale)


def _even_du(name, u, da1, dbp, dpm):
    t, c = da1.shape
    tr = _tile(t, 256, SUBLANES)

    def body(u_ref, da_ref, dbp_ref, dpm_ref, du_ref):
        val = u_ref[:, 0:c].astype(F32)
        sg = _sigmoid(u_ref[:, c:2 * c].astype(F32))
        da = da_ref[...].astype(F32)
        du_ref[:, 0:c] = (da * sg).astype(BF16)
        du_ref[:, c:2 * c] = (da * val * sg * (1.0 - sg)).astype(BF16)
        du_ref[:, 2 * c:3 * c] = (dbp_ref[...].astype(F32) - dpm_ref[...].astype(F32)).astype(BF16)

    row = pl.BlockSpec((tr, c), lambda i: (i, 0))
    wide = pl.BlockSpec((tr, 3 * c), lambda i: (i, 0))
    return pl.pallas_call(
        body, name=name, grid=(t // tr,), in_specs=[wide, row, row, row], out_specs=wide,
        out_shape=jax.ShapeDtypeStruct((t, 3 * c), BF16),
        compiler_params=_cp("parallel"))(u, da1, dbp, dpm)


def _odd_du(name, u, dy, co, dxc):
    t, c = dy.shape
    tr = _tile(t, 256, SUBLANES)

    def body(u_ref, dy_ref, co_ref, dx_ref, du_ref):
        dx = dx_ref[...].astype(F32)
        du_ref[:, 0:c] = (dy_ref[...].astype(F32) * co_ref[...].astype(F32)).astype(BF16)
        du_ref[:, c:2 * c] = (dx * u_ref[:, 2 * c:3 * c].astype(F32)).astype(BF16)
        du_ref[:, 2 * c:3 * c] = (dx * u_ref[:, c:2 * c].astype(F32)).astype(BF16)

    row = pl.BlockSpec((tr, c), lambda i: (i, 0))
    wide = pl.BlockSpec((tr, 3 * c), lambda i: (i, 0))
    return pl.pallas_call(
        body, name=name, grid=(t // tr,), in_specs=[wide, row, row, row], out_specs=wide,
        out_shape=jax.ShapeDtypeStruct((t, 3 * c), BF16),
        compiler_params=_cp("parallel"))(u, dy, co, dxc)


def _local_step(x, tgt, seq, small, get_w, put_g, sync):
    t, d = x.shape
    c = d // 2
    cw_e, cw_o = small["conv_w_e"], small["conv_w_o"]
    wp = small["w_pool_e"].astype(BF16)
    ptaps = _pool_taps(c)
    row = lambda v: v.reshape(1, -1)

    we = {"w_in": get_w("in_e", x)[0]}
    n0 = _rms_fwd("rms_fwd_mix0", x, row(small["mix_norm_e"]))
    u0 = _mm_nn("mm_in_e", n0, we["w_in"], BF16)
    sync("fwd_a", u0)
    a2 = _conv_fwd("conv_e_fwd", seq, c, cw_e, u0, 0, u0, c, "glu", bias=row(small["conv_b_e"]))
    ws = _conv_fwd("pool_fwd", seq, c, ptaps, u0, 2 * c)
    z0, pm = _even_fwd("even_fwd", seq, a2, ws, u0, row(small["ln_g_e"]), row(small["ln_b_e"]),
                       wp, row(small["pool_scale_e"]))
    we["w_out"] = get_w("out_e", z0)[0]
    h1 = _mm_nn("mm_out_e", z0, we["w_out"], F32, res=x)
    sync("fwd_b", h1)
    n1 = _rms_fwd("rms_fwd_ffn0", h1, row(small["ffn_norm"][0]))
    wf0 = dict(zip(("w_gate", "w_up"), get_w("gu0", n1)))
    act0, ds0, s0 = _ffn_fwd("ffn0_fwd", n1, wf0["w_gate"], wf0["w_up"])
    dep = sync("fwd_c", act0)
    wf0["w_down"] = get_w("down0", act0)[0]
    h2 = _mm_nn("mm_down0", act0, wf0["w_down"], F32, res=h1, dep=dep)
    dep = sync("fwd_d", h2)
    n2 = _rms_fwd("rms_fwd_mix1", h2, row(small["mix_norm_o"]))
    wo = {"w_in": get_w("in_o", n2)[0]}
    u1 = _mm_nn("mm_in_o", n2, wo["w_in"], BF16, dep=dep)
    co, y1 = _conv_fwd("conv_o_fwd", seq, d, cw_o, u1, d, u1, 2 * d, "mul", post=u1, cpost=0)
    dep = sync("fwd_e", y1)
    wo["w_out"] = get_w("out_o", y1)[0]
    h3 = _mm_nn("mm_out_o", y1, wo["w_out"], F32, res=h2, dep=dep)
    sync("fwd_f", h3)
    n3 = _rms_fwd("rms_fwd_ffn1", h3, row(small["ffn_norm"][1]))
    wf1 = dict(zip(("w_gate", "w_up"), get_w("gu1", n3)))
    act1, ds1, s1 = _ffn_fwd("ffn1_fwd", n3, wf1["w_gate"], wf1["w_up"])
    wf1["w_down"] = get_w("down1", act1)[0]
    h4 = _mm_nn("mm_down1", act1, wf1["w_down"], F32, res=h3)

    dh4, dh4b, d_final, lsum = _loss_head("loss_head", h4, row(small["final_norm"]), tgt)

    def ffn_bwd(tag, dh, dhb, h_in, gain, n, dsilu, silu, act, w, dep):
        dg, dup = _ffn_bwd_act("ffn%s_bwd_act" % tag, dhb, w["w_down"], dsilu, silu, dep=dep)
        dwd = _mm_tn("mm_dwd%s" % tag, act, dhb, BF16)
        dwg = _mm_tn("mm_dwg%s" % tag, dg, n, BF16, dep=sync("bwd_ffn" + tag, dwd))
        dwu = _mm_tn("mm_dwu%s" % tag, dup, n, BF16)
        dn = _mm_nn("mm_ffn_dn%s" % tag, [dg, dup], [w["w_gate"], w["w_up"]], BF16)
        dh_in, dhb_in, dgain = _rms_bwd("rms_bwd_ffn%s" % tag, h_in, gain, dn, dh)
        dep = put_g("ffn" + tag, {"w_gate": dwg, "w_up": dwu, "w_down": dwd})
        return dh_in, dhb_in, dgain, dep

    dh3, dh3b, d_ffn1, dep = ffn_bwd("1", dh4, dh4b, h3, row(small["ffn_norm"][1]), n3, ds1, s1,
                                     act1, wf1, None)
    dw_out_o = _mm_tn("mm_dw_out_o", y1, dh3b, BF16, dep=dep)
    dy1 = _mm_nt("mm_dy_o", dh3b, wo["w_out"], BF16, dep=sync("bwd_mix_o", dw_out_o))
    dxc, dcw_o = _conv_bwd("conv_o_bwd", seq, d, cw_o, dy1, 0, u1, 0, "mul",
                           x1=u1, c1=d, x2=u1, c2=2 * d, pre="mul")
    du1 = _odd_du("odd_du", u1, dy1, co, dxc)
    dw_in_o = _mm_tn("mm_dw_in_o", n2, du1, BF16)
    dn2 = _mm_nt("mm_dn_o", du1, wo["w_in"], BF16)
    dh2, dh2b, d_mix_o = _rms_bwd("rms_bwd_mix1", h2, row(small["mix_norm_o"]), dn2, dh3)
    dep = put_g("mix_o", {"w_in": dw_in_o, "w_out": dw_out_o})

    dh1, dh1b, d_ffn0, dep = ffn_bwd("0", dh2, dh2b, h1, row(small["ffn_norm"][0]), n1, ds0, s0,
                                     act0, wf0, dep)
    dw_out_e = _mm_tn("mm_dw_out_e", z0, dh1b, BF16, dep=dep)
    dz0 = _mm_nt("mm_dz_e", dh1b, we["w_out"], BF16, dep=sync("bwd_mix_e", dw_out_e))
    da2, dws, dpm, vecs, dwp = _even_bwd("even_bwd", seq, dz0, a2, pm, row(small["ln_g_e"]),
                                         row(small["ln_b_e"]), wp, row(small["pool_scale_e"]))
    da1, dcw_e = _conv_bwd("conv_e_bwd", seq, c, cw_e, da2, 0, x1=u0, c1=0, x2=u0, c2=c, pre="glu")
    dbp = _conv_bwd("pool_bwd", seq, c, ptaps, dws, 0)
    du0 = _even_du("even_du", u0, da1, dbp, dpm)
    dw_in_e = _mm_tn("mm_dw_in_e", n0, du0, BF16)
    dep = put_g("mix_e", {"w_in": dw_in_e, "w_out": dw_out_e})
    dn0 = _mm_nt("mm_dn_e", du0, we["w_in"], BF16, dep=dep)
    dx, _, d_mix_e = _rms_bwd("rms_bwd_mix0", x, row(small["mix_norm_e"]), dn0, dh1)

    gsmall = {"mix_norm_e": d_mix_e[0], "conv_w_e": dcw_e, "conv_b_e": vecs[2], "ln_g_e": vecs[0],
              "ln_b_e": vecs[1], "w_pool_e": dwp, "pool_scale_e": vecs[3], "mix_norm_o": d_mix_o[0],
              "conv_w_o": dcw_o, "ffn_norm": jnp.concatenate([d_ffn0, d_ffn1], axis=0),
              "final_norm": d_final[0]}
    return lsum, dx, gsmall


def _place():
    x, y, c = (lax.axis_index(a) for a in MESH_AXES)
    return x, y, c


def _index(p):
    return 4 * p[0] + 2 * p[1] + p[2]


def _slab(ref, kind, d, n):
    if kind == "blk":
        return ref.at[d]
    return ref.at[:, pl.ds(pl.multiple_of(d * n, LANES), n)]


def _all_gather(name, shards, kinds):
    na = len(shards)

    def body(*refs):
        x_refs, o_refs = refs[:na], refs[na:2 * na]
        send_sems, recv_sems, local_sems = refs[2 * na:]
        x, y, c = _place()
        me, sib = (x, y, c), (x, y, 1 - c)
        chips = [(1 - x, y), (x, 1 - y), (1 - x, 1 - y)]

        def slot(a, p):
            return _slab(o_refs[a], kinds[a], _index(p), shards[a].shape[1])

        def copy(a, k, block, to, src=None):
            return pltpu.make_async_remote_copy(
                src_ref=slot(a, block) if src is None else src, dst_ref=slot(a, block),
                send_sem=send_sems.at[a, k], recv_sem=recv_sems.at[a, k],
                device_id=to, device_id_type=MESH)

        mine = [pltpu.make_async_copy(x_refs[a], slot(a, me), local_sems.at[a]) for a in range(na)]
        for cp in mine:
            cp.start()
        first = []
        for a in range(na):
            first.append(copy(a, 0, me, sib, src=x_refs[a]))
            first += [copy(a, 1 + j, me, (*chip, c), src=x_refs[a]) for j, chip in enumerate(chips)]
        for cp in first:
            cp.start()
        passed = []
        for j, chip in enumerate(chips):
            for a in range(na):
                copy(a, 1 + j, (*chip, c), me).wait_recv()
                fwd = copy(a, 4 + j, (*chip, c), sib)
                fwd.start()
                passed.append(fwd)
        for a in range(na):
            copy(a, 0, sib, me).wait_recv()
            for j, chip in enumerate(chips):
                copy(a, 4 + j, (*chip, 1 - c), me).wait_recv()
        for cp in first + passed:
            cp.wait_send()
        for cp in mine:
            cp.wait()

    shapes = []
    for s, kind in zip(shards, kinds):
        m, n = s.shape
        shapes.append(jax.ShapeDtypeStruct((NDEV, m, n) if kind == "blk" else (m, NDEV * n), s.dtype))
    return pl.pallas_call(
        body, name=name, in_specs=[ANY] * na, out_specs=[ANY] * na, out_shape=shapes,
        scratch_shapes=[pltpu.SemaphoreType.DMA((na, 7)), pltpu.SemaphoreType.DMA((na, 7)),
                        pltpu.SemaphoreType.DMA((na,))])(*shards)


HBM = pl.BlockSpec(memory_space=pltpu.HBM)
SEM = pl.BlockSpec(memory_space=pltpu.SEMAPHORE)
EFFECT = pltpu.SideEffectType.DATAFLOW_SIDE_EFFECTING
NCHIPS = 4


def _in_hbm(a):
    return pltpu.with_memory_space_constraint(a, pltpu.HBM)


def _gathered_shape(s, kind):
    m, n = s.shape
    return (NDEV, m, n) if kind == "blk" else (m, NDEV * n)


def _first_targets():
    x, y, c = _place()
    return [(x, y, 1 - c), (1 - x, y, c), (x, 1 - y, c), (1 - x, 1 - y, c)]


def _gather_start(name, shards, kinds, after):
    na = len(shards)

    def body(*refs):
        x_refs, land_refs = refs[:na], refs[na:2 * na]
        send_sems, recv_sems = refs[2 * na + 1], refs[2 * na + 2]
        token = refs[-1]
        me = _index(_place())
        for a in range(na):
            for k, to in enumerate(_first_targets()):
                pltpu.make_async_remote_copy(
                    src_ref=x_refs[a], dst_ref=_slab(land_refs[a], kinds[a], me, shards[a].shape[1]),
                    send_sem=send_sems.at[4 * a + k], recv_sem=recv_sems.at[4 * a + k],
                    device_id=to, device_id_type=MESH).start()
        token[...] = jnp.zeros_like(token)

    lands = [lax.empty(_gathered_shape(s, k), s.dtype) for s, k in zip(shards, kinds)]
    outs = pl.pallas_call(
        body, name=name,
        out_shape=(pltpu.SemaphoreType.DMA((4 * na,)), pltpu.SemaphoreType.DMA((4 * na,)),
                   *[pltpu.HBM(s.shape, s.dtype) for s in shards],
                   *[pltpu.HBM(l.shape, l.dtype) for l in lands],
                   jax.ShapeDtypeStruct((SUBLANES, LANES), F32)),
        in_specs=[HBM] * (2 * na) + [ANY],
        out_specs=(SEM, SEM, *[HBM] * (2 * na), pl.BlockSpec(memory_space=pltpu.VMEM)),
        input_output_aliases={i: 2 + i for i in range(2 * na)},
        compiler_params=pltpu.CompilerParams(has_side_effects=EFFECT),
    )(*[_in_hbm(s) for s in shards], *[_in_hbm(l) for l in lands], after)
    return outs[0], outs[1], outs[2:2 + na], outs[2 + na:2 + 2 * na], outs[-1]


def _gather_wait(name, started, kinds, after):
    send_sems, recv_sems, shards, lands, _ = started
    na = len(shards)

    def body(*refs):
        x_refs, land_refs = refs[:na], refs[na:2 * na]
        s_sems, r_sems = refs[2 * na], refs[2 * na + 1]
        for a in range(na):
            for k, frm in enumerate(_first_targets()):
                cp = pltpu.make_async_remote_copy(
                    src_ref=x_refs[a],
                    dst_ref=_slab(land_refs[a], kinds[a], _index(frm), shards[a].shape[1]),
                    send_sem=s_sems.at[4 * a + k], recv_sem=r_sems.at[4 * a + k],
                    device_id=frm, device_id_type=MESH)
                cp.wait_send()
                cp.wait_recv()

    outs = pl.pallas_call(
        body, name=name,
        out_shape=(*[pltpu.HBM(s.shape, s.dtype) for s in shards],
                   *[pltpu.HBM(l.shape, l.dtype) for l in lands]),
        in_specs=[HBM] * (2 * na) + [SEM, SEM, ANY], out_specs=[HBM] * (2 * na),
        input_output_aliases={i: i for i in range(2 * na)},
        compiler_params=pltpu.CompilerParams(has_side_effects=EFFECT),
    )(*shards, *lands, send_sems, recv_sems, after)
    return outs[:na], outs[na:]


def _split_start(name, bufs, ncopies, plan, after):
    nb = len(bufs)

    def body(*refs):
        send_sems, recv_sems, token = refs[nb + 1], refs[nb + 2], refs[-1]
        for k, (src, dst, to, _) in enumerate(plan(refs[:nb])):
            pltpu.make_async_remote_copy(src_ref=src, dst_ref=dst, send_sem=send_sems.at[k],
                                         recv_sem=recv_sems.at[k], device_id=to, device_id_type=MESH).start()
        token[...] = jnp.zeros_like(token)

    outs = pl.pallas_call(
        body, name=name,
        out_shape=(pltpu.SemaphoreType.DMA((ncopies,)), pltpu.SemaphoreType.DMA((ncopies,)),
                   *[pltpu.HBM(b.shape, b.dtype) for b in bufs],
                   jax.ShapeDtypeStruct((SUBLANES, LANES), F32)),
        in_specs=[HBM] * nb + [ANY],
        out_specs=(SEM, SEM, *[HBM] * nb, pl.BlockSpec(memory_space=pltpu.VMEM)),
        input_output_aliases={i: 2 + i for i in range(nb)},
        compiler_params=pltpu.CompilerParams(has_side_effects=EFFECT),
    )(*[_in_hbm(b) for b in bufs], after)
    return outs[0], outs[1], list(outs[2:2 + nb]), outs[-1]


def _split_wait(name, started, plan, after):
    send_sems, recv_sems, bufs, _ = started
    nb = len(bufs)

    def body(*refs):
        s_sems, r_sems = refs[nb], refs[nb + 1]
        for k, (src, _, to, landed) in enumerate(plan(refs[:nb])):
            cp = pltpu.make_async_remote_copy(src_ref=src, dst_ref=landed, send_sem=s_sems.at[k],
                                              recv_sem=r_sems.at[k], device_id=to, device_id_type=MESH)
            cp.wait_send()
            cp.wait_recv()

    outs = pl.pallas_call(
        body, name=name, out_shape=tuple(pltpu.HBM(b.shape, b.dtype) for b in bufs),
        in_specs=[HBM] * nb + [SEM, SEM, ANY], out_specs=[HBM] * nb,
        input_output_aliases={i: i for i in range(nb)},
        compiler_params=pltpu.CompilerParams(has_side_effects=EFFECT),
    )(*bufs, send_sems, recv_sems, after)
    return list(outs)


def _forward_plan(kinds, nloc):
    def plan(lands):
        x, y, c = _place()
        out = []
        for a, land in enumerate(lands):
            for chip in [(1 - x, y), (x, 1 - y), (1 - x, 1 - y)]:
                mine = _slab(land, kinds[a], _index((*chip, c)), nloc[a])
                out.append((mine, mine, (x, y, 1 - c), _slab(land, kinds[a], _index((*chip, 1 - c)), nloc[a])))
        return out
    return plan


def _own_copy(name, shard, land, kind, me):
    m, n = shard.shape
    tr = _tile(m, max(SUBLANES, 1048576 // n), SUBLANES)

    def body(s_ref, x_ref, land_ref, o_ref):
        o_ref[...] = x_ref[...]

    if kind == "blk":
        o_spec = pl.BlockSpec((None, tr, n), lambda i, s: (s[0], i, 0))
    else:
        o_spec = pl.BlockSpec((tr, n), lambda i, s: (i, s[0]))
    return pl.pallas_call(
        body, name=name,
        grid_spec=pltpu.PrefetchScalarGridSpec(
            num_scalar_prefetch=1, grid=(m // tr,),
            in_specs=[pl.BlockSpec((tr, n), lambda i, s: (i, 0)), ANY], out_specs=o_spec),
        out_shape=jax.ShapeDtypeStruct(land.shape, land.dtype),
        input_output_aliases={2: 0}, compiler_params=_cp("parallel"))(me, shard, land)


def _everyone_plan(refs):
    x, y, c = _place()
    out = []
    for dx, dy, dc in [(a, b, e) for a in (0, 1) for b in (0, 1) for e in (0, 1)][1:]:
        peer = (x ^ dx, y ^ dy, c ^ dc)
        out.append((refs[0], refs[1].at[_index((x, y, c))], peer, refs[1].at[_index(peer)]))
    return out


def _pair_plan(kinds, nloc):
    na = len(kinds)

    def plan(refs):
        x, y, c = _place()
        out = []
        for a in range(na):
            for j in range(NCHIPS):
                dst = refs[na + a].at[j]
                out.append((_slab(refs[a], kinds[a], 2 * j + (1 - c), nloc[a]), dst, (x, y, 1 - c), dst))
        return out
    return plan


def _chip_sum(name, full, kind, n, from_sib, place):
    _, m, _ = from_sib.shape
    tr = _tile(m, max(SUBLANES, 1048576 // n), SUBLANES)

    def body(s_ref, mine_ref, sib_ref, csum_ref, land_ref):
        v = (mine_ref[...].astype(F32) + sib_ref[...].astype(F32)).astype(csum_ref.dtype)
        csum_ref[...] = v

        @pl.when(pl.program_id(1) == s_ref[1])
        def _():
            land_ref[...] = v

    if kind == "blk":
        mine_spec = pl.BlockSpec((None, tr, n), lambda i, j, s: (2 * j + s[0], i, 0))
    else:
        mine_spec = pl.BlockSpec((tr, n), lambda i, j, s: (i, 2 * j + s[0]))
    slot = pl.BlockSpec((None, tr, n), lambda i, j, s: (j, i, 0))
    shp = jax.ShapeDtypeStruct((NCHIPS, m, n), from_sib.dtype)
    return pl.pallas_call(
        body, name=name,
        grid_spec=pltpu.PrefetchScalarGridSpec(
            num_scalar_prefetch=1, grid=(m // tr, NCHIPS), in_specs=[mine_spec, slot],
            out_specs=[slot, pl.BlockSpec((None, tr, n), lambda i, j, s: (s[1], i, 0))]),
        out_shape=[shp, shp], compiler_params=_cp("parallel", "arbitrary"))(place, full, from_sib)


def _other_chips():
    x, y, c = _place()
    return [(1 - x, y, c), (x, 1 - y, c), (1 - x, 1 - y, c)]


def _scatter_start(name, csums, lands, after):
    na = len(csums)

    def body(*refs):
        c_refs, land_refs = refs[:na], refs[na:2 * na]
        send_sems, recv_sems = refs[2 * na + 1], refs[2 * na + 2]
        token = refs[-1]
        x, y, _ = _place()
        for a in range(na):
            for k, to in enumerate(_other_chips()):
                pltpu.make_async_remote_copy(
                    src_ref=c_refs[a].at[2 * to[0] + to[1]], dst_ref=land_refs[a].at[2 * x + y],
                    send_sem=send_sems.at[3 * a + k], recv_sem=recv_sems.at[3 * a + k],
                    device_id=to, device_id_type=MESH).start()
        token[...] = jnp.zeros_like(token)

    outs = pl.pallas_call(
        body, name=name,
        out_shape=(pltpu.SemaphoreType.DMA((3 * na,)), pltpu.SemaphoreType.DMA((3 * na,)),
                   *[pltpu.HBM(s.shape, s.dtype) for s in csums],
                   *[pltpu.HBM(l.shape, l.dtype) for l in lands],
                   jax.ShapeDtypeStruct((SUBLANES, LANES), F32)),
        in_specs=[HBM] * (2 * na) + [ANY],
        out_specs=(SEM, SEM, *[HBM] * (2 * na), pl.BlockSpec(memory_space=pltpu.VMEM)),
        input_output_aliases={i: 2 + i for i in range(2 * na)},
        compiler_params=pltpu.CompilerParams(has_side_effects=EFFECT),
    )(*[_in_hbm(s) for s in csums], *[_in_hbm(l) for l in lands], after)
    return outs[0], outs[1], outs[2:2 + na], outs[2 + na:2 + 2 * na], outs[-1]


def _scatter_wait(name, started, after):
    send_sems, recv_sems, csums, lands, _ = started
    na = len(csums)

    def body(*refs):
        c_refs, land_refs = refs[:na], refs[na:2 * na]
        s_sems, r_sems = refs[2 * na], refs[2 * na + 1]
        for a in range(na):
            for k, frm in enumerate(_other_chips()):
                cp = pltpu.make_async_remote_copy(
                    src_ref=c_refs[a].at[2 * frm[0] + frm[1]], dst_ref=land_refs[a].at[2 * frm[0] + frm[1]],
                    send_sem=s_sems.at[3 * a + k], recv_sem=r_sems.at[3 * a + k],
                    device_id=frm, device_id_type=MESH)
                cp.wait_send()
                cp.wait_recv()

    outs = pl.pallas_call(
        body, name=name,
        out_shape=(*[pltpu.HBM(s.shape, s.dtype) for s in csums],
                   *[pltpu.HBM(l.shape, l.dtype) for l in lands]),
        in_specs=[HBM] * (2 * na) + [SEM, SEM, ANY], out_specs=[HBM] * (2 * na),
        input_output_aliases={i: i for i in range(2 * na)},
        compiler_params=pltpu.CompilerParams(has_side_effects=EFFECT),
    )(*csums, *lands, send_sems, recv_sems, after)
    return outs[na:]


def _adam_math(w, g, m, v):
    m = ADAM_B1 * m + (1.0 - ADAM_B1) * g
    v = ADAM_B2 * v + (1.0 - ADAM_B2) * (g * g)
    m_hat = m / (1.0 - ADAM_B1 ** ADAM_STEP)
    v_hat = v / (1.0 - ADAM_B2 ** ADAM_STEP)
    delta = -ADAM_LR * (m_hat / (jnp.sqrt(v_hat) + ADAM_EPS) + ADAM_WD * w)
    return delta, m, v


def _sum_adamw(name, parts, w, m, v, layer, prev=None, dep=None):
    nl, r, c = w.shape
    nparts = parts.shape[0]
    tr = _tile(r, max(SUBLANES, 262144 // c), SUBLANES)

    def body(p_ref, w_ref, m_ref, v_ref, *rest):
        g_ref, d_ref, mo_ref, vo_ref = rest[-4:]
        g = p_ref[0].astype(F32)
        for s in range(1, nparts):
            g = g + p_ref[s].astype(F32)
        delta, mn, vn = _adam_math(w_ref[...], g, m_ref[...], v_ref[...])
        g_ref[...] = g
        d_ref[...] = delta
        mo_ref[...] = mn
        vo_ref[...] = vn

    row = pl.BlockSpec((None, tr, c), lambda i: (layer, i, 0))
    shp = jax.ShapeDtypeStruct((nl, r, c), F32)
    extra = ([] if prev is None else list(prev)) + ([] if dep is None else [dep])
    return pl.pallas_call(
        body, name=name, grid=(r // tr,),
        in_specs=[pl.BlockSpec((nparts, tr, c), lambda i: (0, i, 0)), row, row, row] + [ANY] * len(extra),
        out_specs=[row, row, row, row], out_shape=[shp, shp, shp, shp],
        input_output_aliases={} if prev is None else {4 + i: i for i in range(4)},
        compiler_params=_cp("parallel"))(parts, w, m, v, *extra)


def _sum_parts(name, parts):
    _, r, c = parts.shape

    def body(p_ref, o_ref):
        g = p_ref[0]
        for s in range(1, NDEV):
            g = g + p_ref[s]
        o_ref[...] = g

    return pl.pallas_call(
        body, name=name, grid=(1,),
        in_specs=[pl.BlockSpec((NDEV, r, c), lambda i: (0, 0, 0))],
        out_specs=pl.BlockSpec((r, c), lambda i: (0, 0)),
        out_shape=jax.ShapeDtypeStruct((r, c), F32), compiler_params=_cp("arbitrary"))(parts)


def _adamw(name, w, g, m, v):
    r, c = w.shape

    def body(w_ref, g_ref, m_ref, v_ref, d_ref, mo_ref, vo_ref):
        delta, mn, vn = _adam_math(w_ref[...], g_ref[...], m_ref[...], v_ref[...])
        d_ref[...] = delta
        mo_ref[...] = mn
        vo_ref[...] = vn

    full = pl.BlockSpec((r, c), lambda i: (0, 0))
    shp = jax.ShapeDtypeStruct((r, c), F32)
    return pl.pallas_call(
        body, name=name, grid=(1,), in_specs=[full] * 4, out_specs=[full] * 3,
        out_shape=[shp] * 3, compiler_params=_cp("arbitrary"))(w, g, m, v)


def _pack(arrays):
    flat = jnp.concatenate([a.reshape(-1) for a in arrays])
    unit = SUBLANES * LANES
    pad = (-flat.shape[0]) % unit
    return jnp.pad(flat, (0, pad)).reshape(-1, LANES)


def _unpack(buf, shapes):
    flat = buf.reshape(-1)
    out, off = [], 0
    for shp in shapes:
        size = 1
        for s in shp:
            size *= s
        out.append(flat[off:off + size].reshape(shp))
        off += size
    return out


WEIGHTS = ["mix_norm_e", "w_in_e", "conv_w_e", "conv_b_e", "ln_g_e", "ln_b_e", "w_pool_e",
           "pool_scale_e", "w_out_e", "mix_norm_o", "w_in_o", "conv_w_o", "w_out_o", "ffn_norm",
           "w_gate", "w_up", "w_down", "final_norm"]
BIG = ["w_in_e", "w_out_e", "w_in_o", "w_out_o", "w_gate", "w_up", "w_down"]
SHARDED_SMALL = {"conv_w_e": 1, "w_pool_e": 1, "mix_norm_o": 0, "conv_w_o": 1}
SMALL = [n for n in WEIGHTS if n not in BIG]


def kernel(x, mix_norm_e, w_in_e, conv_w_e, conv_b_e, ln_g_e, ln_b_e, w_pool_e, pool_scale_e, w_out_e, mix_norm_o, w_in_o, conv_w_o, w_out_o, ffn_norm, w_gate, w_up, w_down, final_norm, loss_target, m_mix_norm_e, m_w_in_e, m_conv_w_e, m_conv_b_e, m_ln_g_e, m_ln_b_e, m_w_pool_e, m_pool_scale_e, m_w_out_e, m_mix_norm_o, m_w_in_o, m_conv_w_o, m_w_out_o, m_ffn_norm, m_w_gate, m_w_up, m_w_down, m_final_norm, v_mix_norm_e, v_w_in_e, v_conv_w_e, v_conv_b_e, v_ln_g_e, v_ln_b_e, v_w_pool_e, v_pool_scale_e, v_w_out_e, v_mix_norm_o, v_w_in_o, v_conv_w_o, v_w_out_o, v_ffn_norm, v_w_gate, v_w_up, v_w_down, v_final_norm):
    wts = dict(zip(WEIGHTS, [mix_norm_e, w_in_e, conv_w_e, conv_b_e, ln_g_e, ln_b_e, w_pool_e, pool_scale_e, w_out_e, mix_norm_o, w_in_o, conv_w_o, w_out_o, ffn_norm, w_gate, w_up, w_down, final_norm]))
    mom = dict(zip(WEIGHTS, [m_mix_norm_e, m_w_in_e, m_conv_w_e, m_conv_b_e, m_ln_g_e, m_ln_b_e, m_w_pool_e, m_pool_scale_e, m_w_out_e, m_mix_norm_o, m_w_in_o, m_conv_w_o, m_w_out_o, m_ffn_norm, m_w_gate, m_w_up, m_w_down, m_final_norm]))
    var = dict(zip(WEIGHTS, [v_mix_norm_e, v_w_in_e, v_conv_w_e, v_conv_b_e, v_ln_g_e, v_ln_b_e, v_w_pool_e, v_pool_scale_e, v_w_out_e, v_mix_norm_o, v_w_in_o, v_conv_w_o, v_w_out_o, v_ffn_norm, v_w_gate, v_w_up, v_w_down, v_final_norm]))
    bsz, seq, d = x.shape
    t = bsz * seq
    me = _index(_place())
    me_arr = jnp.reshape(me, (1,)).astype(jnp.int32)

    sh_names = list(SHARDED_SMALL)
    sh_local = [wts[n][0] for n in sh_names]
    packed = _pack(sh_local)
    gathered, = _all_gather("gather_small", [packed], ["blk"])
    small = {n: wts[n][0] for n in SMALL if n not in SHARDED_SMALL and n not in ("ffn_norm", "final_norm")}
    small["ffn_norm"], small["final_norm"] = ffn_norm, final_norm
    per_dev = [_unpack(gathered[s], [a.shape for a in sh_local]) for s in range(NDEV)]
    for i, n in enumerate(sh_names):
        small[n] = jnp.concatenate([per_dev[s][i] for s in range(NDEV)], axis=SHARDED_SMALL[n])

    for state in (wts, mom, var):
        for n in ("w_gate", "w_up"):
            state[n] = jnp.swapaxes(state[n], 1, 2)
    bf = lambda a: a.astype(BF16)
    mix_kinds, ffn_kinds = ["col", "blk"], ["blk", "blk", "blk"]
    ffn_names = ("w_gate", "w_up", "w_down")
    groups = {
        "mix_e": ([w_in_e.shape[2], d], mix_kinds, [("w_in_e", 0), ("w_out_e", 0)]),
        "ffn0": ([d, d, d], ffn_kinds, [(n, 0) for n in ffn_names]),
        "mix_o": ([w_in_o.shape[2], d], mix_kinds, [("w_in_o", 0), ("w_out_o", 0)]),
        "ffn1": ([d, d, d], ffn_kinds, [(n, 1) for n in ffn_names]),
    }
    gathers = {
        "in_e": ([bf(w_in_e[0])], ["col"]), "out_e": ([bf(w_out_e[0])], ["blk"]),
        "gu0": ([bf(wts["w_gate"][0]), bf(wts["w_up"][0])], ["blk", "blk"]), "down0": ([bf(w_down[0])], ["blk"]),
        "in_o": ([bf(w_in_o[0])], ["col"]), "out_o": ([bf(w_out_o[0])], ["blk"]),
        "gu1": ([bf(wts["w_gate"][1]), bf(wts["w_up"][1])], ["blk", "blk"]), "down1": ([bf(w_down[1])], ["blk"]),
    }
    started, prev = {}, gathered
    for grp, (shards, kinds) in gathers.items():
        started[grp] = _gather_start("gather_start_" + grp, shards, kinds, prev)
        prev = started[grp][4]
    all_started = prev[0, 0:1]

    passing, shards_of = {}, {}

    def pass_on(grp, after):
        shards, kinds = gathers[grp]
        shards_of[grp], lands = _gather_wait("gather_wait_" + grp, started[grp], kinds, after)
        plan = _forward_plan(kinds, [s.shape[1] for s in shards])
        passing[grp] = (_split_start("forward_start_" + grp, lands, 3 * len(lands), plan, after), plan)
        return passing[grp][0][3]

    def get_w(grp, after):
        if grp not in passing:
            after = pass_on(grp, after)
        st, plan = passing[grp]
        lands = _split_wait("forward_wait_" + grp, st, plan, after)
        full = [_own_copy("own_copy_%s%d" % (grp, a), shards_of[grp][a], lands[a], gathers[grp][1][a], me_arr)
                for a in range(len(lands))]
        return [f.reshape(-1, d) if kind == "blk" else f for f, kind in zip(full, gathers[grp][1])]

    cx, cy, cc = _place()
    place = jnp.stack([cc, 2 * cx + cy]).astype(jnp.int32)
    bwd_order = ["ffn1", "mix_o", "ffn0", "mix_e"]
    pairing, pending, results = {}, {}, {}

    def put_g(grp, grads):
        nloc, kinds, _ = groups[grp]
        if len(kinds) == 2:
            fulls = [grads["w_in"], grads["w_out"].reshape(NDEV, -1, d)]
        else:
            fulls = [grads[n].reshape(NDEV, -1, d) for n in ffn_names]
        empties = []
        for g, kind, n in zip(fulls, kinds, nloc):
            empties.append(lax.empty((NCHIPS, g.shape[1] if kind == "blk" else g.shape[0], n), g.dtype))
        plan = _pair_plan(kinds, nloc)
        pairing[grp] = (_split_start("pair_start_" + grp, fulls + empties, NCHIPS * len(fulls), plan, place),
                        plan, kinds, nloc)
        token = pairing[grp][0][3]
        return send_sums(grp, token) if grp == bwd_order[-1] else token

    def send_sums(grp, after):
        st, plan, kinds, nloc = pairing[grp]
        bufs = _split_wait("pair_wait_" + grp, st, plan, after)
        na = len(kinds)
        sums = [_chip_sum("chip_sum_%s%d" % (grp, a), bufs[a], kinds[a], nloc[a], bufs[na + a], place)
                for a in range(na)]
        pending[grp] = _scatter_start("scatter_start_" + grp, [s[0] for s in sums], [s[1] for s in sums], after)
        return pending[grp][4]

    def finish(grp, after):
        lands = _scatter_wait("scatter_wait_" + grp, pending[grp], after)
        dep = None
        for (n, l), parts in zip(groups[grp][2], lands):
            results[n] = _sum_adamw("adamw_%s%d" % (n, l), parts, wts[n], mom[n], var[n], l, results.get(n), dep)
            dep = results[n][1]
        return dep

    fwd_sync = {"fwd_a": ["out_e"], "fwd_b": ["gu0"], "fwd_c": ["down0", "in_o"], "fwd_d": ["out_o"],
                "fwd_e": ["gu1"], "fwd_f": ["down1"]}

    def sync(tag, after):
        if tag in fwd_sync:
            for grp in fwd_sync[tag]:
                after = pass_on(grp, after)
            return after
        if tag == "bwd_mix_o":
            return send_sums("ffn1", after)
        if tag == "bwd_ffn0":
            return finish("ffn1", send_sums("mix_o", after))
        if tag == "bwd_mix_e":
            return finish("mix_o", send_sums("ffn0", after))
        return None

    small["mix_norm_e"] = small["mix_norm_e"] + all_started
    lsum, dx, gsmall = _local_step(x.reshape(t, d), loss_target.reshape(t, d), seq, small, get_w, put_g, sync)
    loss = lax.psum(jnp.sum(lsum), MESH_AXES)

    out_g, out_d, out_m, out_v = {}, {}, {}, {}

    gs_list = [gsmall[n] for n in SMALL]
    gs_mine = _pack(gs_list)
    small_st = _split_start("small_grads_start", [gs_mine, lax.empty((NDEV,) + gs_mine.shape, F32)], NDEV - 1,
                            _everyone_plan, dx)
    bufs = _split_wait("small_grads_wait", small_st, _everyone_plan, finish("ffn0", small_st[3]))
    gs_all = _own_copy("small_grads_own", bufs[0], bufs[1], "blk", me_arr)
    gs_sum = _unpack(_sum_parts("sum_small_grads", gs_all), [a.shape for a in gs_list])
    local_g = []
    for n, g in zip(SMALL, gs_sum):
        if n in SHARDED_SMALL:
            ax = SHARDED_SMALL[n]
            size = wts[n].shape[ax + 1]
            g = lax.dynamic_slice_in_dim(g, me * size, size, axis=ax)
        local_g.append(g.reshape(wts[n].shape))
    shapes = [wts[n].shape for n in SMALL]
    upd = _adamw("adamw_small", _pack([wts[n] for n in SMALL]), _pack(local_g),
                 _pack([mom[n] for n in SMALL]), _pack([var[n] for n in SMALL]))
    for i, outd in enumerate((out_d, out_m, out_v)):
        for n, a in zip(SMALL, _unpack(upd[i], shapes)):
            outd[n] = a
    for n, g in zip(SMALL, local_g):
        out_g[n] = g

    finish("mix_e", upd[0])
    for n in BIG:
        res = [jnp.swapaxes(a, 1, 2) for a in results[n]] if n in ("w_gate", "w_up") else results[n]
        out_g[n], out_d[n], out_m[n], out_v[n] = res

    return (loss, dx.reshape(bsz, seq, d), *[out_g[n] for n in WEIGHTS], *[out_d[n] for n in WEIGHTS],
            *[out_m[n] for n in WEIGHTS], *[out_v[n] for n in WEIGHTS])
```

```python
import functools

import jax
import jax.numpy as jnp
from jax import lax
from jax.experimental import pallas as pl
from jax.experimental.pallas import tpu as pltpu

F32 = jnp.float32
BF16 = jnp.bfloat16
NDEV = 8
MESH_AXES = ("x", "y", "c")
EPS = 1e-6
POOL_WINDOWS = (2, 4, 8, 16)
CONV_WIDTH = 31
SHORT_WIDTH = 3
ADAM_LR = 0.001
ADAM_B1 = 0.9
ADAM_B2 = 0.999
ADAM_EPS = 1e-08
ADAM_WD = 0.01
ADAM_STEP = 10
LANES = 128
SUBLANES = 8
VMEM_LIMIT = 56 * 1024 * 1024
MXU_DEPTH = 256
MM_TK = 2816
MESH = pl.DeviceIdType.MESH
ANY = pl.BlockSpec(memory_space=pl.ANY)


def _cp(*sem):
    return pltpu.CompilerParams(dimension_semantics=sem, vmem_limit_bytes=VMEM_LIMIT)


def _tile(n, pref, unit=LANES):
    if n <= pref:
        return n
    t = (pref // unit) * unit
    while t > unit and n % t:
        t -= unit
    assert n % t == 0, (n, pref)
    return t


def _sigmoid(v):
    return 0.5 * jnp.tanh(0.5 * v) + 0.5


def _mm(name, pairs, a_specs, b_specs, dims, out_shape, o_spec, grid, acc_shape,
        res=None, res_spec=None, dep=None):
    np_ = len(pairs)
    nk = grid[2]
    has_res = res is not None
    n_in = 2 * np_ + (1 if has_res else 0) + (0 if dep is None else 1)

    def body(*refs):
        a_refs = refs[:np_]
        b_refs = refs[np_:2 * np_]
        r_ref = refs[2 * np_] if has_res else None
        o_ref = refs[n_in]
        acc = refs[-1]

        def part():
            s = None
            for a_ref, b_ref in zip(a_refs, b_refs):
                blocks = [(a_ref[...], b_ref[...])] if len(a_ref.shape) == 2 else [
                    (a_ref[q], b_ref[q]) for q in range(a_ref.shape[0])]
                for av, bv in blocks:
                    d = lax.dot_general(av, bv, dims, preferred_element_type=F32)
                    s = d if s is None else s + d
            return s

        def finish(v):
            if has_res:
                v = v + r_ref[...]
            o_ref[...] = v.astype(o_ref.dtype)

        if nk == 1:
            finish(part())
        else:
            k = pl.program_id(2)

            @pl.when(k == 0)
            def _():
                acc[...] = part()

            @pl.when((k > 0) & (k < nk - 1))
            def _():
                acc[...] += part()

            @pl.when(k == nk - 1)
            def _():
                finish(acc[...] + part())

    ins = [p[0] for p in pairs] + [p[1] for p in pairs]
    specs = list(a_specs) + list(b_specs)
    if has_res:
        ins.append(res)
        specs.append(res_spec)
    if dep is not None:
        ins.append(dep)
        specs.append(ANY)
    return pl.pallas_call(
        body, name=name, grid=grid, in_specs=specs, out_specs=o_spec, out_shape=out_shape,
        scratch_shapes=[pltpu.VMEM(acc_shape if nk > 1 else (SUBLANES, LANES), F32)],
        compiler_params=_cp("parallel", "parallel", "arbitrary"))(*ins)


NN = (((1,), (0,)), ((), ()))
NT = (((1,), (1,)), ((), ()))
TN = (((0,), (0,)), ((), ()))


def _tiles_mk(m, kk, npairs=1):
    tk = _tile(kk, MM_TK, MXU_DEPTH)
    return _tile(m, 1024 if tk * npairs <= MM_TK else 512), tk


def _mm_nn(name, a, b, out_dtype, res=None, dep=None):
    pairs = list(zip(a, b)) if isinstance(a, (list, tuple)) else [(a, b)]
    m, kk = pairs[0][0].shape
    n = pairs[0][1].shape[1]
    tm, tk = _tiles_mk(m, kk, len(pairs))
    tn = _tile(n, 1024)
    return _mm(name, pairs,
               [pl.BlockSpec((tm, tk), lambda i, j, k: (i, k))] * len(pairs),
               [pl.BlockSpec((tk, tn), lambda i, j, k: (k, j))] * len(pairs), NN,
               jax.ShapeDtypeStruct((m, n), out_dtype),
               pl.BlockSpec((tm, tn), lambda i, j, k: (i, j)),
               (m // tm, n // tn, kk // tk), (tm, tn), res,
               pl.BlockSpec((tm, tn), lambda i, j, k: (i, j)), dep=dep)


def _mm_nt(name, a, b, out_dtype, dep=None):
    m, n = a.shape
    kk = b.shape[0]
    tn = _tile(kk, 1024)
    tm, tk = _tiles_mk(m, n)
    return _mm(name, [(a, b)],
               [pl.BlockSpec((tm, tk), lambda i, j, k: (i, k))],
               [pl.BlockSpec((tn, tk), lambda i, j, k: (j, k))], NT,
               jax.ShapeDtypeStruct((m, kk), out_dtype),
               pl.BlockSpec((tm, tn), lambda i, j, k: (i, j)),
               (m // tm, kk // tn, n // tk), (tm, tn), dep=dep)


def _mm_tn(name, a, b, out_dtype, dep=None):
    t, m = a.shape
    n = b.shape[1]
    tn = _tile(n, 1024)
    tm, tk = _tile(m, 1408), _tile(t, MM_TK, MXU_DEPTH)
    return _mm(name, [(a, b)],
               [pl.BlockSpec((tk, tm), lambda i, j, k: (k, i))],
               [pl.BlockSpec((tk, tn), lambda i, j, k: (k, j))], TN,
               jax.ShapeDtypeStruct((m, n), out_dtype),
               pl.BlockSpec((tm, tn), lambda i, j, k: (i, j)),
               (m // tm, n // tn, t // tk), (tm, tn), dep=dep)


def _ffn_fwd(name, n, wg, wu):
    f, d = wg.shape
    t = n.shape[0]
    tm, tn = _tile(t, 1024), _tile(f, 512)

    def body(n_ref, wg_ref, wu_ref, act_ref, ds_ref, s_ref):
        nv = n_ref[...]
        g = lax.dot_general(nv, wg_ref[...], NT, preferred_element_type=F32)
        up = lax.dot_general(nv, wu_ref[...], NT, preferred_element_type=F32)
        sg = _sigmoid(g)
        silu = g * sg
        act_ref[...] = (silu * up).astype(BF16)
        ds_ref[...] = (up * (sg * (1.0 + g * (1.0 - sg)))).astype(BF16)
        s_ref[...] = silu.astype(BF16)

    w_spec = pl.BlockSpec((tn, d), lambda j, i: (j, 0))
    o_spec = pl.BlockSpec((tm, tn), lambda j, i: (i, j))
    shp = jax.ShapeDtypeStruct((t, f), BF16)
    return pl.pallas_call(
        body, name=name, grid=(f // tn, t // tm),
        in_specs=[pl.BlockSpec((tm, d), lambda j, i: (i, 0)), w_spec, w_spec],
        out_specs=[o_spec, o_spec, o_spec], out_shape=[shp, shp, shp],
        compiler_params=_cp("parallel", "parallel"))(n, wg, wu)


def _ffn_bwd_act(name, dh, wd, dsilu, silu, dep=None):
    f, d = wd.shape
    t = dh.shape[0]
    tm, tn = _tile(t, 1024), _tile(f, 512)

    def body(dh_ref, wd_ref, ds_ref, s_ref, *rest):
        dg_ref, dup_ref = rest[-2:]
        da = lax.dot_general(dh_ref[...], wd_ref[...], NT, preferred_element_type=F32)
        dg_ref[...] = (da * ds_ref[...].astype(F32)).astype(BF16)
        dup_ref[...] = (da * s_ref[...].astype(F32)).astype(BF16)

    o_spec = pl.BlockSpec((tm, tn), lambda j, i: (i, j))
    shp = jax.ShapeDtypeStruct((t, f), BF16)
    return pl.pallas_call(
        body, name=name, grid=(f // tn, t // tm),
        in_specs=[pl.BlockSpec((tm, d), lambda j, i: (i, 0)),
                  pl.BlockSpec((tn, d), lambda j, i: (j, 0)), o_spec, o_spec]
        + ([] if dep is None else [ANY]),
        out_specs=[o_spec, o_spec], out_shape=[shp, shp],
        compiler_params=_cp("parallel", "parallel"))(dh, wd, dsilu, silu, *([] if dep is None else [dep]))


def _rms_fwd(name, h, gain):
    t, d = h.shape
    tr = _tile(t, 512, SUBLANES)

    def body(h_ref, g_ref, n_ref):
        hv = h_ref[...]
        r = lax.rsqrt(jnp.mean(hv * hv, axis=-1, keepdims=True) + EPS)
        n_ref[...] = (hv * r * g_ref[...]).astype(BF16)

    return pl.pallas_call(
        body, name=name, grid=(t // tr,),
        in_specs=[pl.BlockSpec((tr, d), lambda i: (i, 0)), pl.BlockSpec((1, d), lambda i: (0, 0))],
        out_specs=pl.BlockSpec((tr, d), lambda i: (i, 0)),
        out_shape=jax.ShapeDtypeStruct((t, d), BF16),
        compiler_params=_cp("parallel"))(h, gain)


def _rms_bwd_math(hv, gain, dn):
    d = hv.shape[-1]
    r = lax.rsqrt(jnp.mean(hv * hv, axis=-1, keepdims=True) + EPS)
    xhat = hv * r
    dxh = dn * gain
    dh = r * (dxh - xhat * (jnp.sum(dxh * xhat, axis=-1, keepdims=True) / d))
    dgain = jnp.sum(dn * xhat, axis=0, keepdims=True)
    return dh, dgain


def _rms_bwd(name, h, gain, dn, dres):
    t, d = h.shape
    tr = _tile(t, 256, SUBLANES)

    def body(h_ref, g_ref, dn_ref, dr_ref, dh_ref, dhb_ref, dg_ref):
        dh, dgain = _rms_bwd_math(h_ref[...], g_ref[...], dn_ref[...].astype(F32))
        dh = dh + dr_ref[...]
        dh_ref[...] = dh
        dhb_ref[...] = dh.astype(BF16)

        @pl.when(pl.program_id(0) == 0)
        def _():
            dg_ref[...] = dgain

        @pl.when(pl.program_id(0) > 0)
        def _():
            dg_ref[...] += dgain

    row = pl.BlockSpec((tr, d), lambda i: (i, 0))
    vec = pl.BlockSpec((1, d), lambda i: (0, 0))
    return pl.pallas_call(
        body, name=name, grid=(t // tr,), in_specs=[row, vec, row, row],
        out_specs=[row, row, vec],
        out_shape=[jax.ShapeDtypeStruct((t, d), F32), jax.ShapeDtypeStruct((t, d), BF16),
                   jax.ShapeDtypeStruct((1, d), F32)],
        compiler_params=_cp("arbitrary"))(h, gain, dn, dres)


def _loss_head(name, h, gain, tgt):
    t, d = h.shape
    tr = _tile(t, 256, SUBLANES)

    def body(h_ref, g_ref, t_ref, dh_ref, dhb_ref, dg_ref, ls_ref):
        hv = h_ref[...]
        gv = g_ref[...]
        r = lax.rsqrt(jnp.mean(hv * hv, axis=-1, keepdims=True) + EPS)
        err = hv * r * gv - t_ref[...]
        lsum = 0.5 * jnp.sum(err * err, axis=0, keepdims=True) / d
        dh, dgain = _rms_bwd_math(hv, gv, err / d)
        dh_ref[...] = dh
        dhb_ref[...] = dh.astype(BF16)

        @pl.when(pl.program_id(0) == 0)
        def _():
            dg_ref[...] = dgain
            ls_ref[...] = lsum

        @pl.when(pl.program_id(0) > 0)
        def _():
            dg_ref[...] += dgain
            ls_ref[...] += lsum

    row = pl.BlockSpec((tr, d), lambda i: (i, 0))
    vec = pl.BlockSpec((1, d), lambda i: (0, 0))
    return pl.pallas_call(
        body, name=name, grid=(t // tr,), in_specs=[row, vec, row],
        out_specs=[row, row, vec, vec],
        out_shape=[jax.ShapeDtypeStruct((t, d), F32), jax.ShapeDtypeStruct((t, d), BF16),
                   jax.ShapeDtypeStruct((1, d), F32), jax.ShapeDtypeStruct((1, d), F32)],
        compiler_params=_cp("arbitrary"))(h, gain, tgt)


def _conv_geom(t, seq, c, k):
    halo = 32 if k - 1 > SUBLANES else SUBLANES
    assert k - 1 <= halo
    tm = min(256, seq // 2)
    tc = min(512, c)
    assert seq % tm == 0 and tm % halo == 0 and c % tc == 0 and t % seq == 0
    return halo, tm, tc, min(128, tm), min(LANES, tc)


def _pre(kind, a, b):
    if kind == "glu":
        return a * _sigmoid(b)
    if kind == "mul":
        return a * b
    return a


def _taps(k):
    return sorted((s % SUBLANES, s // SUBLANES, s) for s in range(k))


def _conv_fwd(name, seq, c, w, x1, c1, x2=None, c2=0, pre=None, bias=None, post=None, cpost=0):
    t = x1.shape[0]
    k = w.shape[0]
    halo, tm, tc, sr, sl = _conv_geom(t, seq, c, k)
    nb, cps = tm // halo, seq // tm
    two = x2 is not None
    has_bias, has_post = bias is not None, post is not None

    def body(*refs):
        it = iter(refs)
        x1c, x1h = next(it), next(it)
        x2c, x2h = (next(it), next(it)) if two else (None, None)
        w_ref = next(it)
        b_ref = next(it) if has_bias else None
        p_ref = next(it) if has_post else None
        o_ref = next(it)
        y_ref = next(it) if has_post else None
        xs = next(it)
        first = (pl.program_id(1) % cps) == 0
        hv = _pre(pre, x1h[...].astype(F32), x2h[...].astype(F32) if two else None)
        xs[0:halo, :] = jnp.where(first, 0.0, hv)
        xs[halo:halo + tm, :] = _pre(pre, x1c[...].astype(F32), x2c[...].astype(F32) if two else None)
        for l0 in range(0, tc, sl):
            ls = slice(l0, l0 + sl)
            for r0 in range(0, tm, sr):
                win = xs[r0:r0 + sr + halo, ls]
                acc = jnp.zeros((sr, sl), F32)
                rolled = {}
                for r, q, s in _taps(k):
                    if r not in rolled:
                        rolled[r] = win if r == 0 else pltpu.roll(win, r, 0)
                    lo = halo - SUBLANES * q
                    acc = acc + w_ref[k - 1 - s:k - s, ls] * rolled[r][lo:lo + sr]
                if has_bias:
                    acc = acc + b_ref[:, ls]
                o_ref[r0:r0 + sr, ls] = acc.astype(o_ref.dtype)
                if has_post:
                    y_ref[r0:r0 + sr, ls] = (acc * p_ref[r0:r0 + sr, ls].astype(F32)).astype(y_ref.dtype)

    def cur(off):
        return pl.BlockSpec((tm, tc), lambda j, i: (i, off // tc + j))

    def prev(off):
        return pl.BlockSpec((halo, tc), lambda j, i: (jnp.maximum(i * nb - 1, 0), off // tc + j))

    ins, specs = [x1, x1], [cur(c1), prev(c1)]
    if two:
        ins += [x2, x2]
        specs += [cur(c2), prev(c2)]
    ins.append(w)
    specs.append(pl.BlockSpec((k, tc), lambda j, i: (0, j)))
    if has_bias:
        ins.append(bias)
        specs.append(pl.BlockSpec((1, tc), lambda j, i: (0, j)))
    if has_post:
        ins.append(post)
        specs.append(cur(cpost))
    o_spec = pl.BlockSpec((tm, tc), lambda j, i: (i, j))
    shp = jax.ShapeDtypeStruct((t, c), BF16)
    return pl.pallas_call(
        body, name=name, grid=(c // tc, t // tm), in_specs=specs,
        out_specs=[o_spec, o_spec] if has_post else o_spec,
        out_shape=[shp, shp] if has_post else shp,
        scratch_shapes=[pltpu.VMEM((halo + tm, tc), F32)],
        compiler_params=_cp("parallel", "parallel"))(*ins)


def _conv_bwd(name, seq, c, w, d1, cd1, d2=None, cd2=0, dpre=None,
              x1=None, c1=0, x2=None, c2=0, pre=None):
    t = d1.shape[0]
    k = w.shape[0]
    halo, tm, tc, sr, sl = _conv_geom(t, seq, c, k)
    nb, cps = tm // halo, seq // tm
    nchunks = t // tm
    dtwo, xtwo, has_x = d2 is not None, x2 is not None, x1 is not None

    def body(*refs):
        it = iter(refs)
        d1c, d1n = next(it), next(it)
        d2c, d2n = (next(it), next(it)) if dtwo else (None, None)
        x1c, x1h = (next(it), next(it)) if has_x else (None, None)
        x2c, x2h = (next(it), next(it)) if xtwo else (None, None)
        w_ref = next(it)
        dx_ref = next(it)
        dw_ref = next(it) if has_x else None
        ds = next(it)
        xs = next(it) if has_x else None
        i = pl.program_id(1)
        last = (i % cps) == cps - 1
        ds[0:tm, :] = _pre(dpre, d1c[...].astype(F32), d2c[...].astype(F32) if dtwo else None)
        nv = _pre(dpre, d1n[...].astype(F32), d2n[...].astype(F32) if dtwo else None)
        ds[tm:tm + halo, :] = jnp.where(last, 0.0, nv)
        if has_x:
            first = (i % cps) == 0
            hv = _pre(pre, x1h[...].astype(F32), x2h[...].astype(F32) if xtwo else None)
            xs[0:halo, :] = jnp.where(first, 0.0, hv)
            xs[halo:halo + tm, :] = _pre(pre, x1c[...].astype(F32), x2c[...].astype(F32) if xtwo else None)

            @pl.when(i == 0)
            def _():
                dw_ref[...] = jnp.zeros_like(dw_ref)

        for l0 in range(0, tc, sl):
            ls = slice(l0, l0 + sl)
            for r0 in range(0, tm, sr):
                win = ds[r0:r0 + sr + halo, ls]
                nrow = sr + halo
                acc = jnp.zeros((sr, sl), F32)
                rolled = {}
                for r, q, s in _taps(k):
                    if r not in rolled:
                        rolled[r] = win if r == 0 else pltpu.roll(win, nrow - r, 0)
                    lo = SUBLANES * q
                    acc = acc + w_ref[k - 1 - s:k - s, ls] * rolled[r][lo:lo + sr]
                dx_ref[r0:r0 + sr, ls] = acc.astype(dx_ref.dtype)
                if has_x:
                    dcur = win[0:sr]
                    xwin = xs[r0:r0 + sr + halo, ls]
                    xrolled = {}
                    for r, q, s in _taps(k):
                        if r not in xrolled:
                            xrolled[r] = xwin if r == 0 else pltpu.roll(xwin, r, 0)
                        lo = halo - SUBLANES * q
                        part = jnp.sum(dcur * xrolled[r][lo:lo + sr], axis=0, keepdims=True)
                        dw_ref[k - 1 - s:k - s, ls] += part

    def cur(off):
        return pl.BlockSpec((tm, tc), lambda j, i: (i, off // tc + j))

    def prev(off):
        return pl.BlockSpec((halo, tc), lambda j, i: (jnp.maximum(i * nb - 1, 0), off // tc + j))

    def nxt(off):
        return pl.BlockSpec((halo, tc),
                            lambda j, i: (jnp.minimum((i + 1) * nb, nchunks * nb - 1), off // tc + j))

    ins, specs = [d1, d1], [cur(cd1), nxt(cd1)]
    if dtwo:
        ins += [d2, d2]
        specs += [cur(cd2), nxt(cd2)]
    if has_x:
        ins += [x1, x1]
        specs += [cur(c1), prev(c1)]
    if xtwo:
        ins += [x2, x2]
        specs += [cur(c2), prev(c2)]
    ins.append(w)
    specs.append(pl.BlockSpec((k, tc), lambda j, i: (0, j)))
    o_specs = [pl.BlockSpec((tm, tc), lambda j, i: (i, j))]
    o_shapes = [jax.ShapeDtypeStruct((t, c), BF16)]
    scratch = [pltpu.VMEM((tm + halo, tc), F32)]
    if has_x:
        o_specs.append(pl.BlockSpec((k, tc), lambda j, i: (0, j)))
        o_shapes.append(jax.ShapeDtypeStruct((k, c), F32))
        scratch.append(pltpu.VMEM((halo + tm, tc), F32))
    out = pl.pallas_call(
        body, name=name, grid=(c // tc, t // tm), in_specs=specs, out_specs=o_specs,
        out_shape=o_shapes, scratch_shapes=scratch,
        compiler_params=_cp("parallel", "arbitrary"))(*ins)
    return out if has_x else out[0]


def _pool_taps(c):
    kmax = max(POOL_WINDOWS)
    grp = c // len(POOL_WINDOWS)
    cols = []
    for wdw in POOL_WINDOWS:
        col = jnp.concatenate([jnp.zeros((kmax - wdw,), F32), jnp.ones((wdw,), F32)])
        cols.append(jnp.tile(col[:, None], (1, grp)))
    return jnp.concatenate(cols, axis=1)


def _counts(i, tr, seq, grp):
    pos = (i * tr + lax.broadcasted_iota(jnp.int32, (tr, 1), 0)) % seq + 1
    return [1.0 / jnp.minimum(pos, wdw).astype(F32) for wdw in POOL_WINDOWS]


def _ln_stats(a2):
    mu = jnp.mean(a2, axis=-1, keepdims=True)
    xc = a2 - mu
    rstd = lax.rsqrt(jnp.mean(xc * xc, axis=-1, keepdims=True) + EPS)
    return xc * rstd, rstd


def _even_fwd(name, seq, a2, ws, u, ln_g, ln_b, w_pool, scale):
    t, c = a2.shape
    ng = len(POOL_WINDOWS)
    grp = c // ng
    tr = _tile(t, 256, SUBLANES)

    def body(a_ref, ws_ref, b_ref, g_ref, bb_ref, wp_ref, sc_ref, z_ref, pm_ref):
        xhat, _ = _ln_stats(a_ref[...].astype(F32))
        l = xhat * g_ref[...] + bb_ref[...]
        z_ref[:, 0:c] = (l * _sigmoid(l)).astype(BF16)
        inv = _counts(pl.program_id(0), tr, seq, grp)
        for g in range(ng):
            gs = slice(g * grp, (g + 1) * grp)
            pm = (ws_ref[:, gs].astype(F32) * inv[g] - b_ref[:, gs].astype(F32)).astype(BF16)
            pm_ref[:, gs] = pm
            q = jnp.dot(pm, wp_ref[g], preferred_element_type=F32)
            z_ref[:, c + g * grp:c + (g + 1) * grp] = (q * sc_ref[:, gs]).astype(BF16)

    row = pl.BlockSpec((tr, c), lambda i: (i, 0))
    vec = pl.BlockSpec((1, c), lambda i: (0, 0))
    return pl.pallas_call(
        body, name=name, grid=(t // tr,),
        in_specs=[row, row, pl.BlockSpec((tr, c), lambda i: (i, 2)), vec, vec,
                  pl.BlockSpec((ng, grp, grp), lambda i: (0, 0, 0)), vec],
        out_specs=[pl.BlockSpec((tr, 2 * c), lambda i: (i, 0)), row],
        out_shape=[jax.ShapeDtypeStruct((t, 2 * c), BF16), jax.ShapeDtypeStruct((t, c), BF16)],
        compiler_params=_cp("parallel"))(a2, ws, u, ln_g, ln_b, w_pool, scale)


def _even_bwd(name, seq, dz, a2, pm, ln_g, ln_b, w_pool, scale):
    t, c = a2.shape
    ng = len(POOL_WINDOWS)
    grp = c // ng
    tr = _tile(t, 256, SUBLANES)

    def body(dz_ref, a_ref, pm_ref, g_ref, bb_ref, wp_ref, sc_ref,
             da_ref, dws_ref, dpm_ref, vec_ref, dwp_ref):
        i = pl.program_id(0)

        @pl.when(i == 0)
        def _():
            vec_ref[...] = jnp.zeros_like(vec_ref)
            dwp_ref[...] = jnp.zeros_like(dwp_ref)

        xhat, rstd = _ln_stats(a_ref[...].astype(F32))
        gv = g_ref[...]
        l = xhat * gv + bb_ref[...]
        sg = _sigmoid(l)
        dl = dz_ref[:, 0:c].astype(F32) * (sg * (1.0 + l * (1.0 - sg)))
        dxh = dl * gv
        da2 = rstd * (dxh - jnp.mean(dxh, axis=-1, keepdims=True)
                      - xhat * jnp.mean(dxh * xhat, axis=-1, keepdims=True))
        da_ref[...] = da2.astype(BF16)
        vec_ref[0:1, :] += jnp.sum(dl * xhat, axis=0, keepdims=True)
        vec_ref[1:2, :] += jnp.sum(dl, axis=0, keepdims=True)
        vec_ref[2:3, :] += jnp.sum(da2, axis=0, keepdims=True)
        inv = _counts(i, tr, seq, grp)
        for g in range(ng):
            gs = slice(g * grp, (g + 1) * grp)
            pmv = pm_ref[:, gs]
            wp = wp_ref[g]
            dp = dz_ref[:, c + g * grp:c + (g + 1) * grp].astype(F32)
            q = jnp.dot(pmv, wp, preferred_element_type=F32)
            vec_ref[3:4, gs] += jnp.sum(dp * q, axis=0, keepdims=True)
            dq = (dp * sc_ref[:, gs]).astype(BF16)
            dpm = lax.dot_general(dq, wp, NT, preferred_element_type=F32)
            dwp_ref[g] += lax.dot_general(pmv, dq, TN, preferred_element_type=F32)
            dpm_ref[:, gs] = dpm.astype(BF16)
            dws_ref[:, gs] = (dpm * inv[g]).astype(BF16)

    row = pl.BlockSpec((tr, c), lambda i: (i, 0))
    vec = pl.BlockSpec((1, c), lambda i: (0, 0))
    rshape = jax.ShapeDtypeStruct((t, c), BF16)
    return pl.pallas_call(
        body, name=name, grid=(t // tr,),
        in_specs=[pl.BlockSpec((tr, 2 * c), lambda i: (i, 0)), row, row, vec, vec,
                  pl.BlockSpec((ng, grp, grp), lambda i: (0, 0, 0)), vec],
        out_specs=[row, row, row, pl.BlockSpec((SUBLANES, c), lambda i: (0, 0)),
                   pl.BlockSpec((ng, grp, grp), lambda i: (0, 0, 0))],
        out_shape=[rshape, rshape, rshape, jax.ShapeDtypeStruct((SUBLANES, c), F32),
                   jax.ShapeDtypeStruct((ng, grp, grp), F32)],
        compiler_params=_cp("arbitrary"))(dz, a2, pm, ln_g, ln_b, w_pool, scale)


def _even_du(name, u, da1, dbp, dpm):
    t, c = da1.shape
    tr = _tile(t, 256, SUBLANES)

    def body(u_ref, da_ref, dbp_ref, dpm_ref, du_ref):
        val = u_ref[:, 0:c].astype(F32)
        sg = _sigmoid(u_ref[:, c:2 * c].astype(F32))
        da = da_ref[...].astype(F32)
        du_ref[:, 0:c] = (da * sg).astype(BF16)
        du_ref[:, c:2 * c] = (da * val * sg * (1.0 - sg)).astype(BF16)
        du_ref[:, 2 * c:3 * c] = (dbp_ref[...].astype(F32) - dpm_ref[...].astype(F32)).astype(BF16)

    row = pl.BlockSpec((tr, c), lambda i: (i, 0))
    wide = pl.BlockSpec((tr, 3 * c), lambda i: (i, 0))
    return pl.pallas_call(
        body, name=name, grid=(t // tr,), in_specs=[wide, row, row, row], out_specs=wide,
        out_shape=jax.ShapeDtypeStruct((t, 3 * c), BF16),
        compiler_params=_cp("parallel"))(u, da1, dbp, dpm)


def _odd_du(name, u, dy, co, dxc):
    t, c = dy.shape
    tr = _tile(t, 256, SUBLANES)

    def body(u_ref, dy_ref, co_ref, dx_ref, du_ref):
        dx = dx_ref[...].astype(F32)
        du_ref[:, 0:c] = (dy_ref[...].astype(F32) * co_ref[...].astype(F32)).astype(BF16)
        du_ref[:, c:2 * c] = (dx * u_ref[:, 2 * c:3 * c].astype(F32)).astype(BF16)
        du_ref[:, 2 * c:3 * c] = (dx * u_ref[:, c:2 * c].astype(F32)).astype(BF16)

    row = pl.BlockSpec((tr, c), lambda i: (i, 0))
    wide = pl.BlockSpec((tr, 3 * c), lambda i: (i, 0))
    return pl.pallas_call(
        body, name=name, grid=(t // tr,), in_specs=[wide, row, row, row], out_specs=wide,
        out_shape=jax.ShapeDtypeStruct((t, 3 * c), BF16),
        compiler_params=_cp("parallel"))(u, dy, co, dxc)


def _local_step(x, tgt, seq, small, get_w, put_g, sync):
    t, d = x.shape
    c = d // 2
    cw_e, cw_o = small["conv_w_e"], small["conv_w_o"]
    wp = small["w_pool_e"].astype(BF16)
    ptaps = _pool_taps(c)
    row = lambda v: v.reshape(1, -1)

    we = {"w_in": get_w("in_e", x)[0]}
    n0 = _rms_fwd("rms_fwd_mix0", x, row(small["mix_norm_e"]))
    u0 = _mm_nn("mm_in_e", n0, we["w_in"], BF16)
    sync("fwd_a", u0)
    a2 = _conv_fwd("conv_e_fwd", seq, c, cw_e, u0, 0, u0, c, "glu", bias=row(small["conv_b_e"]))
    ws = _conv_fwd("pool_fwd", seq, c, ptaps, u0, 2 * c)
    z0, pm = _even_fwd("even_fwd", seq, a2, ws, u0, row(small["ln_g_e"]), row(small["ln_b_e"]),
                       wp, row(small["pool_scale_e"]))
    we["w_out"] = get_w("out_e", z0)[0]
    h1 = _mm_nn("mm_out_e", z0, we["w_out"], F32, res=x)
    sync("fwd_b", h1)
    n1 = _rms_fwd("rms_fwd_ffn0", h1, row(small["ffn_norm"][0]))
    wf0 = dict(zip(("w_gate", "w_up"), get_w("gu0", n1)))
    act0, ds0, s0 = _ffn_fwd("ffn0_fwd", n1, wf0["w_gate"], wf0["w_up"])
    dep = sync("fwd_c", act0)
    wf0["w_down"] = get_w("down0", act0)[0]
    h2 = _mm_nn("mm_down0", act0, wf0["w_down"], F32, res=h1, dep=dep)
    dep = sync("fwd_d", h2)
    n2 = _rms_fwd("rms_fwd_mix1", h2, row(small["mix_norm_o"]))
    wo = {"w_in": get_w("in_o", n2)[0]}
    u1 = _mm_nn("mm_in_o", n2, wo["w_in"], BF16, dep=dep)
    co, y1 = _conv_fwd("conv_o_fwd", seq, d, cw_o, u1, d, u1, 2 * d, "mul", post=u1, cpost=0)
    dep = sync("fwd_e", y1)
    wo["w_out"] = get_w("out_o", y1)[0]
    h3 = _mm_nn("mm_out_o", y1, wo["w_out"], F32, res=h2, dep=dep)
    sync("fwd_f", h3)
    n3 = _rms_fwd("rms_fwd_ffn1", h3, row(small["ffn_norm"][1]))
    wf1 = dict(zip(("w_gate", "w_up"), get_w("gu1", n3)))
    act1, ds1, s1 = _ffn_fwd("ffn1_fwd", n3, wf1["w_gate"], wf1["w_up"])
    wf1["w_down"] = get_w("down1", act1)[0]
    h4 = _mm_nn("mm_down1", act1, wf1["w_down"], F32, res=h3)

    dh4, dh4b, d_final, lsum = _loss_head("loss_head", h4, row(small["final_norm"]), tgt)

    def ffn_bwd(tag, dh, dhb, h_in, gain, n, dsilu, silu, act, w, dep):
        dg, dup = _ffn_bwd_act("ffn%s_bwd_act" % tag, dhb, w["w_down"], dsilu, silu, dep=dep)
        dwd = _mm_tn("mm_dwd%s" % tag, act, dhb, BF16)
        dwg = _mm_tn("mm_dwg%s" % tag, dg, n, BF16, dep=sync("bwd_ffn" + tag, dwd))
        dwu = _mm_tn("mm_dwu%s" % tag, dup, n, BF16)
        dn = _mm_nn("mm_ffn_dn%s" % tag, [dg, dup], [w["w_gate"], w["w_up"]], BF16)
        dh_in, dhb_in, dgain = _rms_bwd("rms_bwd_ffn%s" % tag, h_in, gain, dn, dh)
        dep = put_g("ffn" + tag, {"w_gate": dwg, "w_up": dwu, "w_down": dwd})
        return dh_in, dhb_in, dgain, dep

    dh3, dh3b, d_ffn1, dep = ffn_bwd("1", dh4, dh4b, h3, row(small["ffn_norm"][1]), n3, ds1, s1,
                                     act1, wf1, None)
    dw_out_o = _mm_tn("mm_dw_out_o", y1, dh3b, BF16, dep=dep)
    dy1 = _mm_nt("mm_dy_o", dh3b, wo["w_out"], BF16, dep=sync("bwd_mix_o", dw_out_o))
    dxc, dcw_o = _conv_bwd("conv_o_bwd", seq, d, cw_o, dy1, 0, u1, 0, "mul",
                           x1=u1, c1=d, x2=u1, c2=2 * d, pre="mul")
    du1 = _odd_du("odd_du", u1, dy1, co, dxc)
    dw_in_o = _mm_tn("mm_dw_in_o", n2, du1, BF16)
    dn2 = _mm_nt("mm_dn_o", du1, wo["w_in"], BF16)
    dh2, dh2b, d_mix_o = _rms_bwd("rms_bwd_mix1", h2, row(small["mix_norm_o"]), dn2, dh3)
    dep = put_g("mix_o", {"w_in": dw_in_o, "w_out": dw_out_o})

    dh1, dh1b, d_ffn0, dep = ffn_bwd("0", dh2, dh2b, h1, row(small["ffn_norm"][0]), n1, ds0, s0,
                                     act0, wf0, dep)
    dw_out_e = _mm_tn("mm_dw_out_e", z0, dh1b, BF16, dep=dep)
    dz0 = _mm_nt("mm_dz_e", dh1b, we["w_out"], BF16, dep=sync("bwd_mix_e", dw_out_e))
    da2, dws, dpm, vecs, dwp = _even_bwd("even_bwd", seq, dz0, a2, pm, row(small["ln_g_e"]),
                                         row(small["ln_b_e"]), wp, row(small["pool_scale_e"]))
    da1, dcw_e = _conv_bwd("conv_e_bwd", seq, c, cw_e, da2, 0, x1=u0, c1=0, x2=u0, c2=c, pre="glu")
    dbp = _conv_bwd("pool_bwd", seq, c, ptaps, dws, 0)
    du0 = _even_du("even_du", u0, da1, dbp, dpm)
    dep = put_g("small", {"conv_w_e": dcw_e, "conv_b_e": vecs[2], "ln_g_e": vecs[0], "ln_b_e": vecs[1],
                          "w_pool_e": dwp, "pool_scale_e": vecs[3], "mix_norm_o": d_mix_o[0],
                          "conv_w_o": dcw_o, "ffn_norm": jnp.concatenate([d_ffn0, d_ffn1], axis=0),
                          "final_norm": d_final[0]})
    dw_in_e = _mm_tn("mm_dw_in_e", n0, du0, BF16, dep=dep)
    dep = put_g("mix_e", {"w_in": dw_in_e, "w_out": dw_out_e})
    dn0 = _mm_nt("mm_dn_e", du0, we["w_in"], BF16, dep=dep)
    dx, _, d_mix_e = _rms_bwd("rms_bwd_mix0", x, row(small["mix_norm_e"]), dn0, dh1)
    return lsum, dx, d_mix_e[0]


def _place():
    x, y, c = (lax.axis_index(a) for a in MESH_AXES)
    return x, y, c


def _index(p):
    return 4 * p[0] + 2 * p[1] + p[2]


def _slab(ref, kind, d, n):
    if kind == "blk":
        return ref.at[d]
    return ref.at[:, pl.ds(pl.multiple_of(d * n, LANES), n)]


def _all_gather(name, shards, kinds):
    na = len(shards)

    def body(*refs):
        x_refs, o_refs = refs[:na], refs[na:2 * na]
        send_sems, recv_sems, local_sems = refs[2 * na:]
        x, y, c = _place()
        me, sib = (x, y, c), (x, y, 1 - c)
        chips = [(1 - x, y), (x, 1 - y), (1 - x, 1 - y)]

        def slot(a, p):
            return _slab(o_refs[a], kinds[a], _index(p), shards[a].shape[1])

        def copy(a, k, block, to, src=None):
            return pltpu.make_async_remote_copy(
                src_ref=slot(a, block) if src is None else src, dst_ref=slot(a, block),
                send_sem=send_sems.at[a, k], recv_sem=recv_sems.at[a, k],
                device_id=to, device_id_type=MESH)

        mine = [pltpu.make_async_copy(x_refs[a], slot(a, me), local_sems.at[a]) for a in range(na)]
        for cp in mine:
            cp.start()
        first = []
        for a in range(na):
            first.append(copy(a, 0, me, sib, src=x_refs[a]))
            first += [copy(a, 1 + j, me, (*chip, c), src=x_refs[a]) for j, chip in enumerate(chips)]
        for cp in first:
            cp.start()
        passed = []
        for j, chip in enumerate(chips):
            for a in range(na):
                copy(a, 1 + j, (*chip, c), me).wait_recv()
                fwd = copy(a, 4 + j, (*chip, c), sib)
                fwd.start()
                passed.append(fwd)
        for a in range(na):
            copy(a, 0, sib, me).wait_recv()
            for j, chip in enumerate(chips):
                copy(a, 4 + j, (*chip, 1 - c), me).wait_recv()
        for cp in first + passed:
            cp.wait_send()
        for cp in mine:
            cp.wait()

    shapes = []
    for s, kind in zip(shards, kinds):
        m, n = s.shape
        shapes.append(jax.ShapeDtypeStruct((NDEV, m, n) if kind == "blk" else (m, NDEV * n), s.dtype))
    return pl.pallas_call(
        body, name=name, in_specs=[ANY] * na, out_specs=[ANY] * na, out_shape=shapes,
        scratch_shapes=[pltpu.SemaphoreType.DMA((na, 7)), pltpu.SemaphoreType.DMA((na, 7)),
                        pltpu.SemaphoreType.DMA((na,))])(*shards)


HBM = pl.BlockSpec(memory_space=pltpu.HBM)
SEM = pl.BlockSpec(memory_space=pltpu.SEMAPHORE)
EFFECT = pltpu.SideEffectType.DATAFLOW_SIDE_EFFECTING
NCHIPS = 4


def _in_hbm(a):
    return pltpu.with_memory_space_constraint(a, pltpu.HBM)


def _gathered_shape(s, kind):
    m, n = s.shape
    return (NDEV, m, n) if kind == "blk" else (m, NDEV * n)


def _first_targets():
    x, y, c = _place()
    return [(x, y, 1 - c), (1 - x, y, c), (x, 1 - y, c), (1 - x, 1 - y, c)]


def _gather_start(name, shards, kinds, after):
    na = len(shards)

    def body(*refs):
        x_refs, land_refs = refs[:na], refs[na:2 * na]
        send_sems, recv_sems = refs[2 * na + 1], refs[2 * na + 2]
        token = refs[-1]
        me = _index(_place())
        for a in range(na):
            for k, to in enumerate(_first_targets()):
                pltpu.make_async_remote_copy(
                    src_ref=x_refs[a], dst_ref=_slab(land_refs[a], kinds[a], me, shards[a].shape[1]),
                    send_sem=send_sems.at[4 * a + k], recv_sem=recv_sems.at[4 * a + k],
                    device_id=to, device_id_type=MESH).start()
        token[...] = jnp.zeros_like(token)

    lands = [lax.empty(_gathered_shape(s, k), s.dtype) for s, k in zip(shards, kinds)]
    outs = pl.pallas_call(
        body, name=name,
        out_shape=(pltpu.SemaphoreType.DMA((4 * na,)), pltpu.SemaphoreType.DMA((4 * na,)),
                   *[pltpu.HBM(s.shape, s.dtype) for s in shards],
                   *[pltpu.HBM(l.shape, l.dtype) for l in lands],
                   jax.ShapeDtypeStruct((SUBLANES, LANES), F32)),
        in_specs=[HBM] * (2 * na) + [ANY],
        out_specs=(SEM, SEM, *[HBM] * (2 * na), pl.BlockSpec(memory_space=pltpu.VMEM)),
        input_output_aliases={i: 2 + i for i in range(2 * na)},
        compiler_params=pltpu.CompilerParams(has_side_effects=EFFECT),
    )(*[_in_hbm(s) for s in shards], *[_in_hbm(l) for l in lands], after)
    return outs[0], outs[1], outs[2:2 + na], outs[2 + na:2 + 2 * na], outs[-1]


def _gather_wait(name, started, kinds, after):
    send_sems, recv_sems, shards, lands, _ = started
    na = len(shards)

    def body(*refs):
        x_refs, land_refs = refs[:na], refs[na:2 * na]
        s_sems, r_sems = refs[2 * na], refs[2 * na + 1]
        for a in range(na):
            for k, frm in enumerate(_first_targets()):
                cp = pltpu.make_async_remote_copy(
                    src_ref=x_refs[a],
                    dst_ref=_slab(land_refs[a], kinds[a], _index(frm), shards[a].shape[1]),
                    send_sem=s_sems.at[4 * a + k], recv_sem=r_sems.at[4 * a + k],
                    device_id=frm, device_id_type=MESH)
                cp.wait_send()
                cp.wait_recv()

    outs = pl.pallas_call(
        body, name=name,
        out_shape=(*[pltpu.HBM(s.shape, s.dtype) for s in shards],
                   *[pltpu.HBM(l.shape, l.dtype) for l in lands]),
        in_specs=[HBM] * (2 * na) + [SEM, SEM, ANY], out_specs=[HBM] * (2 * na),
        input_output_aliases={i: i for i in range(2 * na)},
        compiler_params=pltpu.CompilerParams(has_side_effects=EFFECT),
    )(*shards, *lands, send_sems, recv_sems, after)
    return outs[:na], outs[na:]


def _split_start(name, bufs, ncopies, plan, after):
    nb = len(bufs)

    def body(*refs):
        send_sems, recv_sems, token = refs[nb + 1], refs[nb + 2], refs[-1]
        for k, (src, dst, to, _) in enumerate(plan(refs[:nb])):
            pltpu.make_async_remote_copy(src_ref=src, dst_ref=dst, send_sem=send_sems.at[k],
                                         recv_sem=recv_sems.at[k], device_id=to, device_id_type=MESH).start()
        token[...] = jnp.zeros_like(token)

    outs = pl.pallas_call(
        body, name=name,
        out_shape=(pltpu.SemaphoreType.DMA((ncopies,)), pltpu.SemaphoreType.DMA((ncopies,)),
                   *[pltpu.HBM(b.shape, b.dtype) for b in bufs],
                   jax.ShapeDtypeStruct((SUBLANES, LANES), F32)),
        in_specs=[HBM] * nb + [ANY],
        out_specs=(SEM, SEM, *[HBM] * nb, pl.BlockSpec(memory_space=pltpu.VMEM)),
        input_output_aliases={i: 2 + i for i in range(nb)},
        compiler_params=pltpu.CompilerParams(has_side_effects=EFFECT),
    )(*[_in_hbm(b) for b in bufs], after)
    return outs[0], outs[1], list(outs[2:2 + nb]), outs[-1]


def _split_wait(name, started, plan, after):
    send_sems, recv_sems, bufs, _ = started
    nb = len(bufs)

    def body(*refs):
        s_sems, r_sems = refs[nb], refs[nb + 1]
        for k, (src, _, to, landed) in enumerate(plan(refs[:nb])):
            cp = pltpu.make_async_remote_copy(src_ref=src, dst_ref=landed, send_sem=s_sems.at[k],
                                              recv_sem=r_sems.at[k], device_id=to, device_id_type=MESH)
            cp.wait_send()
            cp.wait_recv()

    outs = pl.pallas_call(
        body, name=name, out_shape=tuple(pltpu.HBM(b.shape, b.dtype) for b in bufs),
        in_specs=[HBM] * nb + [SEM, SEM, ANY], out_specs=[HBM] * nb,
        input_output_aliases={i: i for i in range(nb)},
        compiler_params=pltpu.CompilerParams(has_side_effects=EFFECT),
    )(*bufs, send_sems, recv_sems, after)
    return list(outs)


def _forward_plan(kinds, nloc):
    def plan(lands):
        x, y, c = _place()
        out = []
        for a, land in enumerate(lands):
            for chip in [(1 - x, y), (x, 1 - y), (1 - x, 1 - y)]:
                mine = _slab(land, kinds[a], _index((*chip, c)), nloc[a])
                out.append((mine, mine, (x, y, 1 - c), _slab(land, kinds[a], _index((*chip, 1 - c)), nloc[a])))
        return out
    return plan


def _own_copy(name, shard, land, kind, me):
    m, n = shard.shape
    tr = _tile(m, max(SUBLANES, 1048576 // n), SUBLANES)

    def body(s_ref, x_ref, land_ref, o_ref):
        o_ref[...] = x_ref[...]

    if kind == "blk":
        o_spec = pl.BlockSpec((None, tr, n), lambda i, s: (s[0], i, 0))
    else:
        o_spec = pl.BlockSpec((tr, n), lambda i, s: (i, s[0]))
    return pl.pallas_call(
        body, name=name,
        grid_spec=pltpu.PrefetchScalarGridSpec(
            num_scalar_prefetch=1, grid=(m // tr,),
            in_specs=[pl.BlockSpec((tr, n), lambda i, s: (i, 0)), ANY], out_specs=o_spec),
        out_shape=jax.ShapeDtypeStruct(land.shape, land.dtype),
        input_output_aliases={2: 0}, compiler_params=_cp("parallel"))(me, shard, land)


def _everyone_plan(refs):
    x, y, c = _place()
    out = []
    for dx, dy, dc in [(a, b, e) for a in (0, 1) for b in (0, 1) for e in (0, 1)][1:]:
        peer = (x ^ dx, y ^ dy, c ^ dc)
        out.append((refs[0], refs[1].at[_index((x, y, c))], peer, refs[1].at[_index(peer)]))
    return out


def _pair_plan(kinds, nloc):
    na = len(kinds)

    def plan(refs):
        x, y, c = _place()
        out = []
        for a in range(na):
            for j in range(NCHIPS):
                dst = refs[na + a].at[j]
                out.append((_slab(refs[a], kinds[a], 2 * j + (1 - c), nloc[a]), dst, (x, y, 1 - c), dst))
        return out
    return plan


def _chip_sum(name, full, kind, n, from_sib, place):
    _, m, _ = from_sib.shape
    tr = _tile(m, max(SUBLANES, 1048576 // n), SUBLANES)

    def body(s_ref, mine_ref, sib_ref, csum_ref, land_ref):
        v = (mine_ref[...].astype(F32) + sib_ref[...].astype(F32)).astype(csum_ref.dtype)
        csum_ref[...] = v

        @pl.when(pl.program_id(1) == s_ref[1])
        def _():
            land_ref[...] = v

    if kind == "blk":
        mine_spec = pl.BlockSpec((None, tr, n), lambda i, j, s: (2 * j + s[0], i, 0))
    else:
        mine_spec = pl.BlockSpec((tr, n), lambda i, j, s: (i, 2 * j + s[0]))
    slot = pl.BlockSpec((None, tr, n), lambda i, j, s: (j, i, 0))
    shp = jax.ShapeDtypeStruct((NCHIPS, m, n), from_sib.dtype)
    return pl.pallas_call(
        body, name=name,
        grid_spec=pltpu.PrefetchScalarGridSpec(
            num_scalar_prefetch=1, grid=(m // tr, NCHIPS), in_specs=[mine_spec, slot],
            out_specs=[slot, pl.BlockSpec((None, tr, n), lambda i, j, s: (s[1], i, 0))]),
        out_shape=[shp, shp], compiler_params=_cp("parallel", "arbitrary"))(place, full, from_sib)


def _other_chips():
    x, y, c = _place()
    return [(1 - x, y, c), (x, 1 - y, c), (1 - x, 1 - y, c)]


def _scatter_start(name, csums, lands, after):
    na = len(csums)

    def body(*refs):
        c_refs, land_refs = refs[:na], refs[na:2 * na]
        send_sems, recv_sems = refs[2 * na + 1], refs[2 * na + 2]
        token = refs[-1]
        x, y, _ = _place()
        for a in range(na):
            for k, to in enumerate(_other_chips()):
                pltpu.make_async_remote_copy(
                    src_ref=c_refs[a].at[2 * to[0] + to[1]], dst_ref=land_refs[a].at[2 * x + y],
                    send_sem=send_sems.at[3 * a + k], recv_sem=recv_sems.at[3 * a + k],
                    device_id=to, device_id_type=MESH).start()
        token[...] = jnp.zeros_like(token)

    outs = pl.pallas_call(
        body, name=name,
        out_shape=(pltpu.SemaphoreType.DMA((3 * na,)), pltpu.SemaphoreType.DMA((3 * na,)),
                   *[pltpu.HBM(s.shape, s.dtype) for s in csums],
                   *[pltpu.HBM(l.shape, l.dtype) for l in lands],
                   jax.ShapeDtypeStruct((SUBLANES, LANES), F32)),
        in_specs=[HBM] * (2 * na) + [ANY],
        out_specs=(SEM, SEM, *[HBM] * (2 * na), pl.BlockSpec(memory_space=pltpu.VMEM)),
        input_output_aliases={i: 2 + i for i in range(2 * na)},
        compiler_params=pltpu.CompilerParams(has_side_effects=EFFECT),
    )(*[_in_hbm(s) for s in csums], *[_in_hbm(l) for l in lands], after)
    return outs[0], outs[1], outs[2:2 + na], outs[2 + na:2 + 2 * na], outs[-1]


def _scatter_wait(name, started, after):
    send_sems, recv_sems, csums, lands, _ = started
    na = len(csums)

    def body(*refs):
        c_refs, land_refs = refs[:na], refs[na:2 * na]
        s_sems, r_sems = refs[2 * na], refs[2 * na + 1]
        for a in range(na):
            for k, frm in enumerate(_other_chips()):
                cp = pltpu.make_async_remote_copy(
                    src_ref=c_refs[a].at[2 * frm[0] + frm[1]], dst_ref=land_refs[a].at[2 * frm[0] + frm[1]],
                    send_sem=s_sems.at[3 * a + k], recv_sem=r_sems.at[3 * a + k],
                    device_id=frm, device_id_type=MESH)
                cp.wait_send()
                cp.wait_recv()

    outs = pl.pallas_call(
        body, name=name,
        out_shape=(*[pltpu.HBM(s.shape, s.dtype) for s in csums],
                   *[pltpu.HBM(l.shape, l.dtype) for l in lands]),
        in_specs=[HBM] * (2 * na) + [SEM, SEM, ANY], out_specs=[HBM] * (2 * na),
        input_output_aliases={i: i for i in range(2 * na)},
        compiler_params=pltpu.CompilerParams(has_side_effects=EFFECT),
    )(*csums, *lands, send_sems, recv_sems, after)
    return outs[na:]


def _adam_math(w, g, m, v):
    m = ADAM_B1 * m + (1.0 - ADAM_B1) * g
    v = ADAM_B2 * v + (1.0 - ADAM_B2) * (g * g)
    m_hat = m / (1.0 - ADAM_B1 ** ADAM_STEP)
    v_hat = v / (1.0 - ADAM_B2 ** ADAM_STEP)
    delta = -ADAM_LR * (m_hat / (jnp.sqrt(v_hat) + ADAM_EPS) + ADAM_WD * w)
    return delta, m, v


def _sum_adamw(name, parts, w, m, v, layer, prev=None, dep=None):
    nl, r, c = w.shape
    nparts = parts.shape[0]
    tr = _tile(r, max(SUBLANES, 262144 // c), SUBLANES)

    def body(p_ref, w_ref, m_ref, v_ref, *rest):
        g_ref, d_ref, mo_ref, vo_ref = rest[-4:]
        g = p_ref[0].astype(F32)
        for s in range(1, nparts):
            g = g + p_ref[s].astype(F32)
        delta, mn, vn = _adam_math(w_ref[...], g, m_ref[...], v_ref[...])
        g_ref[...] = g
        d_ref[...] = delta
        mo_ref[...] = mn
        vo_ref[...] = vn

    row = pl.BlockSpec((None, tr, c), lambda i: (layer, i, 0))
    shp = jax.ShapeDtypeStruct((nl, r, c), F32)
    extra = ([] if prev is None else list(prev)) + ([] if dep is None else [dep])
    return pl.pallas_call(
        body, name=name, grid=(r // tr,),
        in_specs=[pl.BlockSpec((nparts, tr, c), lambda i: (0, i, 0)), row, row, row] + [ANY] * len(extra),
        out_specs=[row, row, row, row], out_shape=[shp, shp, shp, shp],
        input_output_aliases={} if prev is None else {4 + i: i for i in range(4)},
        compiler_params=_cp("parallel"))(parts, w, m, v, *extra)


def _sum_parts(name, parts):
    _, r, c = parts.shape

    def body(p_ref, o_ref):
        g = p_ref[0]
        for s in range(1, NDEV):
            g = g + p_ref[s]
        o_ref[...] = g

    return pl.pallas_call(
        body, name=name, grid=(1,),
        in_specs=[pl.BlockSpec((NDEV, r, c), lambda i: (0, 0, 0))],
        out_specs=pl.BlockSpec((r, c), lambda i: (0, 0)),
        out_shape=jax.ShapeDtypeStruct((r, c), F32), compiler_params=_cp("arbitrary"))(parts)


def _adamw(name, w, g, m, v):
    r, c = w.shape

    def body(w_ref, g_ref, m_ref, v_ref, d_ref, mo_ref, vo_ref):
        delta, mn, vn = _adam_math(w_ref[...], g_ref[...], m_ref[...], v_ref[...])
        d_ref[...] = delta
        mo_ref[...] = mn
        vo_ref[...] = vn

    full = pl.BlockSpec((r, c), lambda i: (0, 0))
    shp = jax.ShapeDtypeStruct((r, c), F32)
    return pl.pallas_call(
        body, name=name, grid=(1,), in_specs=[full] * 4, out_specs=[full] * 3,
        out_shape=[shp] * 3, compiler_params=_cp("arbitrary"))(w, g, m, v)


def _pack(arrays):
    flat = jnp.concatenate([a.reshape(-1) for a in arrays])
    unit = SUBLANES * LANES
    pad = (-flat.shape[0]) % unit
    return jnp.pad(flat, (0, pad)).reshape(-1, LANES)


def _unpack(buf, shapes):
    flat = buf.reshape(-1)
    out, off = [], 0
    for shp in shapes:
        size = 1
        for s in shp:
            size *= s
        out.append(flat[off:off + size].reshape(shp))
        off += size
    return out


WEIGHTS = ["mix_norm_e", "w_in_e", "conv_w_e", "conv_b_e", "ln_g_e", "ln_b_e", "w_pool_e",
           "pool_scale_e", "w_out_e", "mix_norm_o", "w_in_o", "conv_w_o", "w_out_o", "ffn_norm",
           "w_gate", "w_up", "w_down", "final_norm"]
BIG = ["w_in_e", "w_out_e", "w_in_o", "w_out_o", "w_gate", "w_up", "w_down"]
SHARDED_SMALL = {"conv_w_e": 1, "w_pool_e": 1, "mix_norm_o": 0, "conv_w_o": 1}
SMALL = [n for n in WEIGHTS if n not in BIG]


def kernel(x, mix_norm_e, w_in_e, conv_w_e, conv_b_e, ln_g_e, ln_b_e, w_pool_e, pool_scale_e, w_out_e, mix_norm_o, w_in_o, conv_w_o, w_out_o, ffn_norm, w_gate, w_up, w_down, final_norm, loss_target, m_mix_norm_e, m_w_in_e, m_conv_w_e, m_conv_b_e, m_ln_g_e, m_ln_b_e, m_w_pool_e, m_pool_scale_e, m_w_out_e, m_mix_norm_o, m_w_in_o, m_conv_w_o, m_w_out_o, m_ffn_norm, m_w_gate, m_w_up, m_w_down, m_final_norm, v_mix_norm_e, v_w_in_e, v_conv_w_e, v_conv_b_e, v_ln_g_e, v_ln_b_e, v_w_pool_e, v_pool_scale_e, v_w_out_e, v_mix_norm_o, v_w_in_o, v_conv_w_o, v_w_out_o, v_ffn_norm, v_w_gate, v_w_up, v_w_down, v_final_norm):
    wts = dict(zip(WEIGHTS, [mix_norm_e, w_in_e, conv_w_e, conv_b_e, ln_g_e, ln_b_e, w_pool_e, pool_scale_e, w_out_e, mix_norm_o, w_in_o, conv_w_o, w_out_o, ffn_norm, w_gate, w_up, w_down, final_norm]))
    mom = dict(zip(WEIGHTS, [m_mix_norm_e, m_w_in_e, m_conv_w_e, m_conv_b_e, m_ln_g_e, m_ln_b_e, m_w_pool_e, m_pool_scale_e, m_w_out_e, m_mix_norm_o, m_w_in_o, m_conv_w_o, m_w_out_o, m_ffn_norm, m_w_gate, m_w_up, m_w_down, m_final_norm]))
    var = dict(zip(WEIGHTS, [v_mix_norm_e, v_w_in_e, v_conv_w_e, v_conv_b_e, v_ln_g_e, v_ln_b_e, v_w_pool_e, v_pool_scale_e, v_w_out_e, v_mix_norm_o, v_w_in_o, v_conv_w_o, v_w_out_o, v_ffn_norm, v_w_gate, v_w_up, v_w_down, v_final_norm]))
    bsz, seq, d = x.shape
    t = bsz * seq
    me = _index(_place())
    me_arr = jnp.reshape(me, (1,)).astype(jnp.int32)

    sh_names = list(SHARDED_SMALL)
    sh_local = [wts[n][0] for n in sh_names]
    packed = _pack(sh_local)
    gathered, = _all_gather("gather_small", [packed], ["blk"])
    small = {n: wts[n][0] for n in SMALL if n not in SHARDED_SMALL and n not in ("ffn_norm", "final_norm")}
    small["ffn_norm"], small["final_norm"] = ffn_norm, final_norm
    per_dev = [_unpack(gathered[s], [a.shape for a in sh_local]) for s in range(NDEV)]
    for i, n in enumerate(sh_names):
        small[n] = jnp.concatenate([per_dev[s][i] for s in range(NDEV)], axis=SHARDED_SMALL[n])

    for state in (wts, mom, var):
        for n in ("w_gate", "w_up"):
            state[n] = jnp.swapaxes(state[n], 1, 2)
    bf = lambda a: a.astype(BF16)
    mix_kinds, ffn_kinds = ["col", "blk"], ["blk", "blk", "blk"]
    ffn_names = ("w_gate", "w_up", "w_down")
    groups = {
        "mix_e": ([w_in_e.shape[2], d], mix_kinds, [("w_in_e", 0), ("w_out_e", 0)]),
        "ffn0": ([d, d, d], ffn_kinds, [(n, 0) for n in ffn_names]),
        "mix_o": ([w_in_o.shape[2], d], mix_kinds, [("w_in_o", 0), ("w_out_o", 0)]),
        "ffn1": ([d, d, d], ffn_kinds, [(n, 1) for n in ffn_names]),
    }
    gathers = {
        "in_e": ([bf(w_in_e[0])], ["col"]), "out_e": ([bf(w_out_e[0])], ["blk"]),
        "gu0": ([bf(wts["w_gate"][0]), bf(wts["w_up"][0])], ["blk", "blk"]), "down0": ([bf(w_down[0])], ["blk"]),
        "in_o": ([bf(w_in_o[0])], ["col"]), "out_o": ([bf(w_out_o[0])], ["blk"]),
        "gu1": ([bf(wts["w_gate"][1]), bf(wts["w_up"][1])], ["blk", "blk"]), "down1": ([bf(w_down[1])], ["blk"]),
    }
    started, prev = {}, gathered
    for grp, (shards, kinds) in gathers.items():
        started[grp] = _gather_start("gather_start_" + grp, shards, kinds, prev)
        prev = started[grp][4]
    all_started = prev[0, 0:1]

    passing, shards_of = {}, {}

    def pass_on(grp, after):
        shards, kinds = gathers[grp]
        shards_of[grp], lands = _gather_wait("gather_wait_" + grp, started[grp], kinds, after)
        plan = _forward_plan(kinds, [s.shape[1] for s in shards])
        passing[grp] = (_split_start("forward_start_" + grp, lands, 3 * len(lands), plan, after), plan)
        return passing[grp][0][3]

    def get_w(grp, after):
        if grp not in passing:
            after = pass_on(grp, after)
        st, plan = passing[grp]
        lands = _split_wait("forward_wait_" + grp, st, plan, after)
        full = [_own_copy("own_copy_%s%d" % (grp, a), shards_of[grp][a], lands[a], gathers[grp][1][a], me_arr)
                for a in range(len(lands))]
        return [f.reshape(-1, d) if kind == "blk" else f for f, kind in zip(full, gathers[grp][1])]

    cx, cy, cc = _place()
    place = jnp.stack([cc, 2 * cx + cy]).astype(jnp.int32)
    bwd_order = ["ffn1", "mix_o", "ffn0", "mix_e"]
    pairing, pending, results = {}, {}, {}

    early_names = [n for n in SMALL if n != "mix_norm_e"]
    small_sent = {}

    def send_small(tag, arrays, after):
        mine = _pack(arrays)
        small_sent[tag] = _split_start(tag + "_start", [mine, lax.empty((NDEV,) + mine.shape, F32)], NDEV - 1,
                                       _everyone_plan, after)
        return small_sent[tag][3]

    def summed_small(tag, shapes, after):
        bufs = _split_wait(tag + "_wait", small_sent[tag], _everyone_plan, after)
        parts = _own_copy(tag + "_own", bufs[0], bufs[1], "blk", me_arr)
        return _unpack(_sum_parts(tag + "_sum", parts), shapes)

    def put_g(grp, grads):
        if grp == "small":
            small_sent["shapes"] = [grads[n].shape for n in early_names]
            return send_small("small_grads", [grads[n] for n in early_names], place)
        nloc, kinds, _ = groups[grp]
        if len(kinds) == 2:
            fulls = [grads["w_in"], grads["w_out"].reshape(NDEV, -1, d)]
        else:
            fulls = [grads[n].reshape(NDEV, -1, d) for n in ffn_names]
        empties = []
        for g, kind, n in zip(fulls, kinds, nloc):
            empties.append(lax.empty((NCHIPS, g.shape[1] if kind == "blk" else g.shape[0], n), g.dtype))
        plan = _pair_plan(kinds, nloc)
        pairing[grp] = (_split_start("pair_start_" + grp, fulls + empties, NCHIPS * len(fulls), plan, place),
                        plan, kinds, nloc)
        token = pairing[grp][0][3]
        return send_sums(grp, token) if grp == bwd_order[-1] else token

    def send_sums(grp, after):
        st, plan, kinds, nloc = pairing[grp]
        bufs = _split_wait("pair_wait_" + grp, st, plan, after)
        na = len(kinds)
        sums = [_chip_sum("chip_sum_%s%d" % (grp, a), bufs[a], kinds[a], nloc[a], bufs[na + a], place)
                for a in range(na)]
        pending[grp] = _scatter_start("scatter_start_" + grp, [s[0] for s in sums], [s[1] for s in sums], after)
        return pending[grp][4]

    def finish(grp, after):
        lands = _scatter_wait("scatter_wait_" + grp, pending[grp], after)
        dep = None
        for (n, l), parts in zip(groups[grp][2], lands):
            results[n] = _sum_adamw("adamw_%s%d" % (n, l), parts, wts[n], mom[n], var[n], l, results.get(n), dep)
            dep = results[n][1]
        return dep

    fwd_sync = {"fwd_a": ["out_e"], "fwd_b": ["gu0"], "fwd_c": ["down0", "in_o"], "fwd_d": ["out_o"],
                "fwd_e": ["gu1"], "fwd_f": ["down1"]}

    def sync(tag, after):
        if tag in fwd_sync:
            for grp in fwd_sync[tag]:
                after = pass_on(grp, after)
            return after
        if tag == "bwd_mix_o":
            return send_sums("ffn1", after)
        if tag == "bwd_ffn0":
            return finish("ffn1", send_sums("mix_o", after))
        if tag == "bwd_mix_e":
            return finish("mix_o", send_sums("ffn0", after))
        return None

    small["mix_norm_e"] = small["mix_norm_e"] + all_started
    lsum, dx, d_mix_e = _local_step(x.reshape(t, d), loss_target.reshape(t, d), seq, small, get_w, put_g, sync)
    loss = lax.psum(jnp.sum(lsum), MESH_AXES)

    out_g, out_d, out_m, out_v = {}, {}, {}, {}

    dep = finish("ffn0", send_small("last_grad", [d_mix_e], dx))
    sums = dict(zip(early_names, summed_small("small_grads", small_sent["shapes"], dep)))
    sums["mix_norm_e"], = summed_small("last_grad", [d_mix_e.shape], dep)
    gs_sum = [sums[n] for n in SMALL]
    local_g = []
    for n, g in zip(SMALL, gs_sum):
        if n in SHARDED_SMALL:
            ax = SHARDED_SMALL[n]
            size = wts[n].shape[ax + 1]
            g = lax.dynamic_slice_in_dim(g, me * size, size, axis=ax)
        local_g.append(g.reshape(wts[n].shape))
    shapes = [wts[n].shape for n in SMALL]
    upd = _adamw("adamw_small", _pack([wts[n] for n in SMALL]), _pack(local_g),
                 _pack([mom[n] for n in SMALL]), _pack([var[n] for n in SMALL]))
    for i, outd in enumerate((out_d, out_m, out_v)):
        for n, a in zip(SMALL, _unpack(upd[i], shapes)):
            outd[n] = a
    for n, g in zip(SMALL, local_g):
        out_g[n] = g

    finish("mix_e", upd[0])
    for n in BIG:
        res = [jnp.swapaxes(a, 1, 2) for a in results[n]] if n in ("w_gate", "w_up") else results[n]
        out_g[n], out_d[n], out_m[n], out_v[n] = res

    return (loss, dx.reshape(bsz, seq, d), *[out_g[n] for n in WEIGHTS], *[out_d[n] for n in WEIGHTS],
            *[out_m[n] for n in WEIGHTS], *[out_v[n] for n in WEIGHTS])
```

```python
import jax
import jax.numpy as jnp
from jax import lax
from jax.experimental import pallas as pl
from jax.experimental.pallas import tpu as pltpu

F32 = jnp.float32
BF16 = jnp.bfloat16
NDEV = 8
MESH_AXES = ("x", "y", "c")
EPS = 1e-6
POOL_WINDOWS = (2, 4, 8, 16)
CONV_WIDTH = 31
SHORT_WIDTH = 3
ADAM_LR = 0.001
ADAM_B1 = 0.9
ADAM_B2 = 0.999
ADAM_EPS = 1e-08
ADAM_WD = 0.01
ADAM_STEP = 10
LANES = 128
SUBLANES = 8
VMEM_LIMIT = 56 * 1024 * 1024
MXU_DEPTH = 256
MM_TK = 2816
MESH = pl.DeviceIdType.MESH
ANY = pl.BlockSpec(memory_space=pl.ANY)


def _cp(*sem):
    return pltpu.CompilerParams(dimension_semantics=sem, vmem_limit_bytes=VMEM_LIMIT)


def _tile(n, pref, unit=LANES):
    if n <= pref:
        return n
    t = (pref // unit) * unit
    while t > unit and n % t:
        t -= unit
    assert n % t == 0, (n, pref)
    return t


def _sigmoid(v):
    return 0.5 * jnp.tanh(0.5 * v) + 0.5


def _mm(name, pairs, a_specs, b_specs, dims, out_shape, o_spec, grid, acc_shape,
        res=None, res_spec=None, dep=None):
    np_ = len(pairs)
    nk = grid[2]
    has_res = res is not None
    n_in = 2 * np_ + (1 if has_res else 0) + (0 if dep is None else 1)

    def body(*refs):
        a_refs = refs[:np_]
        b_refs = refs[np_:2 * np_]
        r_ref = refs[2 * np_] if has_res else None
        o_ref = refs[n_in]
        acc = refs[-1]

        def part():
            s = None
            for a_ref, b_ref in zip(a_refs, b_refs):
                blocks = [(a_ref[...], b_ref[...])] if len(a_ref.shape) == 2 else [
                    (a_ref[q], b_ref[q]) for q in range(a_ref.shape[0])]
                for av, bv in blocks:
                    d = lax.dot_general(av, bv, dims, preferred_element_type=F32)
                    s = d if s is None else s + d
            return s

        def finish(v):
            if has_res:
                v = v + r_ref[...]
            o_ref[...] = v.astype(o_ref.dtype)

        if nk == 1:
            finish(part())
        else:
            k = pl.program_id(2)

            @pl.when(k == 0)
            def _():
                acc[...] = part()

            @pl.when((k > 0) & (k < nk - 1))
            def _():
                acc[...] += part()

            @pl.when(k == nk - 1)
            def _():
                finish(acc[...] + part())

    ins = [p[0] for p in pairs] + [p[1] for p in pairs]
    specs = list(a_specs) + list(b_specs)
    if has_res:
        ins.append(res)
        specs.append(res_spec)
    if dep is not None:
        ins.append(dep)
        specs.append(ANY)
    return pl.pallas_call(
        body, name=name, grid=grid, in_specs=specs, out_specs=o_spec, out_shape=out_shape,
        scratch_shapes=[pltpu.VMEM(acc_shape if nk > 1 else (SUBLANES, LANES), F32)],
        compiler_params=_cp("parallel", "parallel", "arbitrary"))(*ins)


NN = (((1,), (0,)), ((), ()))
NT = (((1,), (1,)), ((), ()))
TN = (((0,), (0,)), ((), ()))


def _tiles_mk(m, kk, npairs=1):
    tk = _tile(kk, MM_TK, MXU_DEPTH)
    return _tile(m, 1024 if tk * npairs <= MM_TK else 512), tk


def _mm_nn(name, a, b, out_dtype, res=None, dep=None):
    pairs = list(zip(a, b)) if isinstance(a, (list, tuple)) else [(a, b)]
    m, kk = pairs[0][0].shape
    n = pairs[0][1].shape[1]
    tm, tk = _tiles_mk(m, kk, len(pairs))
    tn = _tile(n, 1024)
    return _mm(name, pairs,
               [pl.BlockSpec((tm, tk), lambda i, j, k: (i, k))] * len(pairs),
               [pl.BlockSpec((tk, tn), lambda i, j, k: (k, j))] * len(pairs), NN,
               jax.ShapeDtypeStruct((m, n), out_dtype),
               pl.BlockSpec((tm, tn), lambda i, j, k: (i, j)),
               (m // tm, n // tn, kk // tk), (tm, tn), res,
               pl.BlockSpec((tm, tn), lambda i, j, k: (i, j)), dep=dep)


def _mm_nt(name, a, b, out_dtype, dep=None):
    m, n = a.shape
    kk = b.shape[0]
    tn = _tile(kk, 1024)
    tm, tk = _tiles_mk(m, n)
    return _mm(name, [(a, b)],
               [pl.BlockSpec((tm, tk), lambda i, j, k: (i, k))],
               [pl.BlockSpec((tn, tk), lambda i, j, k: (j, k))], NT,
               jax.ShapeDtypeStruct((m, kk), out_dtype),
               pl.BlockSpec((tm, tn), lambda i, j, k: (i, j)),
               (m // tm, kk // tn, n // tk), (tm, tn), dep=dep)


def _mm_tn(name, a, b, out_dtype, dep=None):
    t, m = a.shape
    n = b.shape[1]
    tn = _tile(n, 1024)
    tm, tk = _tile(m, 1408), _tile(t, MM_TK, MXU_DEPTH)
    return _mm(name, [(a, b)],
               [pl.BlockSpec((tk, tm), lambda i, j, k: (k, i))],
               [pl.BlockSpec((tk, tn), lambda i, j, k: (k, j))], TN,
               jax.ShapeDtypeStruct((m, n), out_dtype),
               pl.BlockSpec((tm, tn), lambda i, j, k: (i, j)),
               (m // tm, n // tn, t // tk), (tm, tn), dep=dep)


def _ffn_fwd(name, n, wg, wu):
    f, d = wg.shape
    t = n.shape[0]
    tm, tn = _tile(t, 1024), _tile(f, 512)

    def body(n_ref, wg_ref, wu_ref, act_ref, ds_ref, s_ref):
        nv = n_ref[...]
        g = lax.dot_general(nv, wg_ref[...], NT, preferred_element_type=F32)
        up = lax.dot_general(nv, wu_ref[...], NT, preferred_element_type=F32)
        sg = _sigmoid(g)
        silu = g * sg
        act_ref[...] = (silu * up).astype(BF16)
        ds_ref[...] = (up * (sg * (1.0 + g * (1.0 - sg)))).astype(BF16)
        s_ref[...] = silu.astype(BF16)

    w_spec = pl.BlockSpec((tn, d), lambda j, i: (j, 0))
    o_spec = pl.BlockSpec((tm, tn), lambda j, i: (i, j))
    shp = jax.ShapeDtypeStruct((t, f), BF16)
    return pl.pallas_call(
        body, name=name, grid=(f // tn, t // tm),
        in_specs=[pl.BlockSpec((tm, d), lambda j, i: (i, 0)), w_spec, w_spec],
        out_specs=[o_spec, o_spec, o_spec], out_shape=[shp, shp, shp],
        compiler_params=_cp("parallel", "parallel"))(n, wg, wu)


def _ffn_bwd_act(name, dh, wd, dsilu, silu, dep=None):
    f, d = wd.shape
    t = dh.shape[0]
    tm, tn = _tile(t, 1024), _tile(f, 512)

    def body(dh_ref, wd_ref, ds_ref, s_ref, *rest):
        dg_ref, dup_ref = rest[-2:]
        da = lax.dot_general(dh_ref[...], wd_ref[...], NT, preferred_element_type=F32)
        dg_ref[...] = (da * ds_ref[...].astype(F32)).astype(BF16)
        dup_ref[...] = (da * s_ref[...].astype(F32)).astype(BF16)

    o_spec = pl.BlockSpec((tm, tn), lambda i, j: (i, j))
    shp = jax.ShapeDtypeStruct((t, f), BF16)
    return pl.pallas_call(
        body, name=name, grid=(t // tm, f // tn),
        in_specs=[pl.BlockSpec((tm, d), lambda i, j: (i, 0)),
                  pl.BlockSpec((tn, d), lambda i, j: (j, 0)), o_spec, o_spec]
        + ([] if dep is None else [ANY]),
        out_specs=[o_spec, o_spec], out_shape=[shp, shp],
        compiler_params=_cp("parallel", "parallel"))(dh, wd, dsilu, silu, *([] if dep is None else [dep]))


def _rms_fwd(name, h, gain):
    t, d = h.shape
    tr = _tile(t, 512, SUBLANES)

    def body(h_ref, g_ref, n_ref):
        hv = h_ref[...]
        r = lax.rsqrt(jnp.mean(hv * hv, axis=-1, keepdims=True) + EPS)
        n_ref[...] = (hv * r * g_ref[...]).astype(BF16)

    return pl.pallas_call(
        body, name=name, grid=(t // tr,),
        in_specs=[pl.BlockSpec((tr, d), lambda i: (i, 0)), pl.BlockSpec((1, d), lambda i: (0, 0))],
        out_specs=pl.BlockSpec((tr, d), lambda i: (i, 0)),
        out_shape=jax.ShapeDtypeStruct((t, d), BF16),
        compiler_params=_cp("parallel"))(h, gain)


def _rms_bwd_math(hv, gain, dn):
    d = hv.shape[-1]
    r = lax.rsqrt(jnp.mean(hv * hv, axis=-1, keepdims=True) + EPS)
    xhat = hv * r
    dxh = dn * gain
    dh = r * (dxh - xhat * (jnp.sum(dxh * xhat, axis=-1, keepdims=True) / d))
    dgain = jnp.sum(dn * xhat, axis=0, keepdims=True)
    return dh, dgain


def _rms_bwd(name, h, gain, dn, dres):
    t, d = h.shape
    tr = _tile(t, 256, SUBLANES)

    def body(h_ref, g_ref, dn_ref, dr_ref, dh_ref, dhb_ref, dg_ref):
        dh, dgain = _rms_bwd_math(h_ref[...], g_ref[...], dn_ref[...].astype(F32))
        dh = dh + dr_ref[...]
        dh_ref[...] = dh
        dhb_ref[...] = dh.astype(BF16)

        @pl.when(pl.program_id(0) == 0)
        def _():
            dg_ref[...] = dgain

        @pl.when(pl.program_id(0) > 0)
        def _():
            dg_ref[...] += dgain

    row = pl.BlockSpec((tr, d), lambda i: (i, 0))
    vec = pl.BlockSpec((1, d), lambda i: (0, 0))
    return pl.pallas_call(
        body, name=name, grid=(t // tr,), in_specs=[row, vec, row, row],
        out_specs=[row, row, vec],
        out_shape=[jax.ShapeDtypeStruct((t, d), F32), jax.ShapeDtypeStruct((t, d), BF16),
                   jax.ShapeDtypeStruct((1, d), F32)],
        compiler_params=_cp("arbitrary"))(h, gain, dn, dres)


def _loss_head(name, h, gain, tgt):
    t, d = h.shape
    tr = _tile(t, 256, SUBLANES)

    def body(h_ref, g_ref, t_ref, dh_ref, dhb_ref, dg_ref, ls_ref):
        hv = h_ref[...]
        gv = g_ref[...]
        r = lax.rsqrt(jnp.mean(hv * hv, axis=-1, keepdims=True) + EPS)
        err = hv * r * gv - t_ref[...]
        lsum = 0.5 * jnp.sum(err * err, axis=0, keepdims=True) / d
        dh, dgain = _rms_bwd_math(hv, gv, err / d)
        dh_ref[...] = dh
        dhb_ref[...] = dh.astype(BF16)

        @pl.when(pl.program_id(0) == 0)
        def _():
            dg_ref[...] = dgain
            ls_ref[...] = lsum

        @pl.when(pl.program_id(0) > 0)
        def _():
            dg_ref[...] += dgain
            ls_ref[...] += lsum

    row = pl.BlockSpec((tr, d), lambda i: (i, 0))
    vec = pl.BlockSpec((1, d), lambda i: (0, 0))
    return pl.pallas_call(
        body, name=name, grid=(t // tr,), in_specs=[row, vec, row],
        out_specs=[row, row, vec, vec],
        out_shape=[jax.ShapeDtypeStruct((t, d), F32), jax.ShapeDtypeStruct((t, d), BF16),
                   jax.ShapeDtypeStruct((1, d), F32), jax.ShapeDtypeStruct((1, d), F32)],
        compiler_params=_cp("arbitrary"))(h, gain, tgt)


def _conv_geom(t, seq, c, k):
    halo = 32 if k - 1 > SUBLANES else SUBLANES
    assert k - 1 <= halo
    tm = min(256, seq // 2)
    tc = min(512, c)
    assert seq % tm == 0 and tm % halo == 0 and c % tc == 0 and t % seq == 0
    return halo, tm, tc, min(64 if halo > SUBLANES else 128, tm), min(LANES, tc)


def _pre(kind, a, b):
    if kind == "glu":
        return a * _sigmoid(b)
    if kind == "mul":
        return a * b
    return a


def _taps(k):
    return sorted((s % SUBLANES, s // SUBLANES, s) for s in range(k))


def _conv_fwd(name, seq, c, w, x1, c1, x2=None, c2=0, pre=None, bias=None, post=None, cpost=0):
    t = x1.shape[0]
    k = w.shape[0]
    halo, tm, tc, sr, sl = _conv_geom(t, seq, c, k)
    nb, cps = tm // halo, seq // tm
    two = x2 is not None
    has_bias, has_post = bias is not None, post is not None

    def body(*refs):
        it = iter(refs)
        x1c, x1h = next(it), next(it)
        x2c, x2h = (next(it), next(it)) if two else (None, None)
        w_ref = next(it)
        b_ref = next(it) if has_bias else None
        p_ref = next(it) if has_post else None
        o_ref = next(it)
        y_ref = next(it) if has_post else None
        xs = next(it)
        first = (pl.program_id(1) % cps) == 0
        hv = _pre(pre, x1h[...].astype(F32), x2h[...].astype(F32) if two else None)
        xs[0:halo, :] = jnp.where(first, 0.0, hv)
        xs[halo:halo + tm, :] = _pre(pre, x1c[...].astype(F32), x2c[...].astype(F32) if two else None)
        for l0 in range(0, tc, sl):
            ls = slice(l0, l0 + sl)
            for r0 in range(0, tm, sr):
                win = xs[r0:r0 + sr + halo, ls]
                acc = jnp.zeros((sr, sl), F32)
                rolled = {}
                for r, q, s in _taps(k):
                    if r not in rolled:
                        rolled[r] = win if r == 0 else pltpu.roll(win, r, 0)
                    lo = halo - SUBLANES * q
                    acc = acc + w_ref[k - 1 - s:k - s, ls] * rolled[r][lo:lo + sr]
                if has_bias:
                    acc = acc + b_ref[:, ls]
                o_ref[r0:r0 + sr, ls] = acc.astype(o_ref.dtype)
                if has_post:
                    y_ref[r0:r0 + sr, ls] = (acc * p_ref[r0:r0 + sr, ls].astype(F32)).astype(y_ref.dtype)

    def cur(off):
        return pl.BlockSpec((tm, tc), lambda j, i: (i, off // tc + j))

    def prev(off):
        return pl.BlockSpec((halo, tc), lambda j, i: (jnp.maximum(i * nb - 1, 0), off // tc + j))

    ins, specs = [x1, x1], [cur(c1), prev(c1)]
    if two:
        ins += [x2, x2]
        specs += [cur(c2), prev(c2)]
    ins.append(w)
    specs.append(pl.BlockSpec((k, tc), lambda j, i: (0, j)))
    if has_bias:
        ins.append(bias)
        specs.append(pl.BlockSpec((1, tc), lambda j, i: (0, j)))
    if has_post:
        ins.append(post)
        specs.append(cur(cpost))
    o_spec = pl.BlockSpec((tm, tc), lambda j, i: (i, j))
    shp = jax.ShapeDtypeStruct((t, c), BF16)
    return pl.pallas_call(
        body, name=name, grid=(c // tc, t // tm), in_specs=specs,
        out_specs=[o_spec, o_spec] if has_post else o_spec,
        out_shape=[shp, shp] if has_post else shp,
        scratch_shapes=[pltpu.VMEM((halo + tm, tc), F32)],
        compiler_params=_cp("parallel", "parallel"))(*ins)


def _conv_bwd(name, seq, c, w, d1, cd1, d2=None, cd2=0, dpre=None,
              x1=None, c1=0, x2=None, c2=0, pre=None):
    t = d1.shape[0]
    k = w.shape[0]
    halo, tm, tc, sr, sl = _conv_geom(t, seq, c, k)
    nb, cps = tm // halo, seq // tm
    nchunks = t // tm
    dtwo, xtwo, has_x = d2 is not None, x2 is not None, x1 is not None

    def body(*refs):
        it = iter(refs)
        d1c, d1n = next(it), next(it)
        d2c, d2n = (next(it), next(it)) if dtwo else (None, None)
        x1c, x1h = (next(it), next(it)) if has_x else (None, None)
        x2c, x2h = (next(it), next(it)) if xtwo else (None, None)
        w_ref = next(it)
        dx_ref = next(it)
        dw_ref = next(it) if has_x else None
        ds = next(it)
        xs = next(it) if has_x else None
        i = pl.program_id(1)
        last = (i % cps) == cps - 1
        ds[0:tm, :] = _pre(dpre, d1c[...].astype(F32), d2c[...].astype(F32) if dtwo else None)
        nv = _pre(dpre, d1n[...].astype(F32), d2n[...].astype(F32) if dtwo else None)
        ds[tm:tm + halo, :] = jnp.where(last, 0.0, nv)
        if has_x:
            first = (i % cps) == 0
            hv = _pre(pre, x1h[...].astype(F32), x2h[...].astype(F32) if xtwo else None)
            xs[0:halo, :] = jnp.where(first, 0.0, hv)
            xs[halo:halo + tm, :] = _pre(pre, x1c[...].astype(F32), x2c[...].astype(F32) if xtwo else None)

            @pl.when(i == 0)
            def _():
                dw_ref[...] = jnp.zeros_like(dw_ref)

        for l0 in range(0, tc, sl):
            ls = slice(l0, l0 + sl)
            for r0 in range(0, tm, sr):
                win = ds[r0:r0 + sr + halo, ls]
                nrow = sr + halo
                acc = jnp.zeros((sr, sl), F32)
                rolled = {}
                for r, q, s in _taps(k):
                    if r not in rolled:
                        rolled[r] = win if r == 0 else pltpu.roll(win, nrow - r, 0)
                    lo = SUBLANES * q
                    acc = acc + w_ref[k - 1 - s:k - s, ls] * rolled[r][lo:lo + sr]
                dx_ref[r0:r0 + sr, ls] = acc.astype(dx_ref.dtype)
                if has_x:
                    dcur = win[0:sr]
                    xwin = xs[r0:r0 + sr + halo, ls]
                    xrolled = {}
                    for r, q, s in _taps(k):
                        if r not in xrolled:
                            xrolled[r] = xwin if r == 0 else pltpu.roll(xwin, r, 0)
                        lo = halo - SUBLANES * q
                        part = jnp.sum(dcur * xrolled[r][lo:lo + sr], axis=0, keepdims=True)
                        dw_ref[k - 1 - s:k - s, ls] += part

    def cur(off):
        return pl.BlockSpec((tm, tc), lambda j, i: (i, off // tc + j))

    def prev(off):
        return pl.BlockSpec((halo, tc), lambda j, i: (jnp.maximum(i * nb - 1, 0), off // tc + j))

    def nxt(off):
        return pl.BlockSpec((halo, tc),
                            lambda j, i: (jnp.minimum((i + 1) * nb, nchunks * nb - 1), off // tc + j))

    ins, specs = [d1, d1], [cur(cd1), nxt(cd1)]
    if dtwo:
        ins += [d2, d2]
        specs += [cur(cd2), nxt(cd2)]
    if has_x:
        ins += [x1, x1]
        specs += [cur(c1), prev(c1)]
    if xtwo:
        ins += [x2, x2]
        specs += [cur(c2), prev(c2)]
    ins.append(w)
    specs.append(pl.BlockSpec((k, tc), lambda j, i: (0, j)))
    o_specs = [pl.BlockSpec((tm, tc), lambda j, i: (i, j))]
    o_shapes = [jax.ShapeDtypeStruct((t, c), BF16)]
    scratch = [pltpu.VMEM((tm + halo, tc), F32)]
    if has_x:
        o_specs.append(pl.BlockSpec((k, tc), lambda j, i: (0, j)))
        o_shapes.append(jax.ShapeDtypeStruct((k, c), F32))
        scratch.append(pltpu.VMEM((halo + tm, tc), F32))
    out = pl.pallas_call(
        body, name=name, grid=(c // tc, t // tm), in_specs=specs, out_specs=o_specs,
        out_shape=o_shapes, scratch_shapes=scratch,
        compiler_params=_cp("parallel", "arbitrary"))(*ins)
    return out if has_x else out[0]


def _pool_taps(c):
    kmax = max(POOL_WINDOWS)
    grp = c // len(POOL_WINDOWS)
    cols = []
    for wdw in POOL_WINDOWS:
        col = jnp.concatenate([jnp.zeros((kmax - wdw,), F32), jnp.ones((wdw,), F32)])
        cols.append(jnp.tile(col[:, None], (1, grp)))
    return jnp.concatenate(cols, axis=1)


def _counts(i, tr, seq, grp):
    pos = (i * tr + lax.broadcasted_iota(jnp.int32, (tr, 1), 0)) % seq + 1
    return [1.0 / jnp.minimum(pos, wdw).astype(F32) for wdw in POOL_WINDOWS]


def _ln_stats(a2):
    mu = jnp.mean(a2, axis=-1, keepdims=True)
    xc = a2 - mu
    rstd = lax.rsqrt(jnp.mean(xc * xc, axis=-1, keepdims=True) + EPS)
    return xc * rstd, rstd


def _even_fwd(name, seq, a2, ws, u, ln_g, ln_b, w_pool, scale):
    t, c = a2.shape
    ng = len(POOL_WINDOWS)
    grp = c // ng
    tr = _tile(t, 256, SUBLANES)

    def body(a_ref, ws_ref, b_ref, g_ref, bb_ref, wp_ref, sc_ref, z_ref, pm_ref):
        xhat, _ = _ln_stats(a_ref[...].astype(F32))
        l = xhat * g_ref[...] + bb_ref[...]
        z_ref[:, 0:c] = (l * _sigmoid(l)).astype(BF16)
        inv = _counts(pl.program_id(0), tr, seq, grp)
        for g in range(ng):
            gs = slice(g * grp, (g + 1) * grp)
            pm = (ws_ref[:, gs].astype(F32) * inv[g] - b_ref[:, gs].astype(F32)).astype(BF16)
            pm_ref[:, gs] = pm
            q = jnp.dot(pm, wp_ref[g], preferred_element_type=F32)
            z_ref[:, c + g * grp:c + (g + 1) * grp] = (q * sc_ref[:, gs]).astype(BF16)

    row = pl.BlockSpec((tr, c), lambda i: (i, 0))
    vec = pl.BlockSpec((1, c), lambda i: (0, 0))
    return pl.pallas_call(
        body, name=name, grid=(t // tr,),
        in_specs=[row, row, pl.BlockSpec((tr, c), lambda i: (i, 2)), vec, vec,
                  pl.BlockSpec((ng, grp, grp), lambda i: (0, 0, 0)), vec],
        out_specs=[pl.BlockSpec((tr, 2 * c), lambda i: (i, 0)), row],
        out_shape=[jax.ShapeDtypeStruct((t, 2 * c), BF16), jax.ShapeDtypeStruct((t, c), BF16)],
        compiler_params=_cp("parallel"))(a2, ws, u, ln_g, ln_b, w_pool, scale)


def _even_bwd(name, seq, dz, a2, pm, ln_g, ln_b, w_pool, scale):
    t, c = a2.shape
    ng = len(POOL_WINDOWS)
    grp = c // ng
    tr = _tile(t, 256, SUBLANES)

    def body(dz_ref, a_ref, pm_ref, g_ref, bb_ref, wp_ref, sc_ref,
             da_ref, dws_ref, dpm_ref, vec_ref, dwp_ref):
        i = pl.program_id(0)

        @pl.when(i == 0)
        def _():
            vec_ref[...] = jnp.zeros_like(vec_ref)
            dwp_ref[...] = jnp.zeros_like(dwp_ref)

        xhat, rstd = _ln_stats(a_ref[...].astype(F32))
        gv = g_ref[...]
        l = xhat * gv + bb_ref[...]
        sg = _sigmoid(l)
        dl = dz_ref[:, 0:c].astype(F32) * (sg * (1.0 + l * (1.0 - sg)))
        dxh = dl * gv
        da2 = rstd * (dxh - jnp.mean(dxh, axis=-1, keepdims=True)
                      - xhat * jnp.mean(dxh * xhat, axis=-1, keepdims=True))
        da_ref[...] = da2.astype(BF16)
        vec_ref[0:1, :] += jnp.sum(dl * xhat, axis=0, keepdims=True)
        vec_ref[1:2, :] += jnp.sum(dl, axis=0, keepdims=True)
        vec_ref[2:3, :] += jnp.sum(da2, axis=0, keepdims=True)
        inv = _counts(i, tr, seq, grp)
        for g in range(ng):
            gs = slice(g * grp, (g + 1) * grp)
            pmv = pm_ref[:, gs]
            wp = wp_ref[g]
            dp = dz_ref[:, c + g * grp:c + (g + 1) * grp].astype(F32)
            q = jnp.dot(pmv, wp, preferred_element_type=F32)
            vec_ref[3:4, gs] += jnp.sum(dp * q, axis=0, keepdims=True)
            dq = (dp * sc_ref[:, gs]).astype(BF16)
            dpm = lax.dot_general(dq, wp, NT, preferred_element_type=F32)
            dwp_ref[g] += lax.dot_general(pmv, dq, TN, preferred_element_type=F32)
            dpm_ref[:, gs] = dpm.astype(BF16)
            dws_ref[:, gs] = (dpm * inv[g]).astype(BF16)

    row = pl.BlockSpec((tr, c), lambda i: (i, 0))
    vec = pl.BlockSpec((1, c), lambda i: (0, 0))
    rshape = jax.ShapeDtypeStruct((t, c), BF16)
    return pl.pallas_call(
        body, name=name, grid=(t // tr,),
        in_specs=[pl.BlockSpec((tr, 2 * c), lambda i: (i, 0)), row, row, vec, vec,
                  pl.BlockSpec((ng, grp, grp), lambda i: (0, 0, 0)), vec],
        out_specs=[row, row, row, pl.BlockSpec((SUBLANES, c), lambda i: (0, 0)),
                   pl.BlockSpec((ng, grp, grp), lambda i: (0, 0, 0))],
        out_shape=[rshape, rshape, rshape, jax.ShapeDtypeStruct((SUBLANES, c), F32),
                   jax.ShapeDtypeStruct((ng, grp, grp), F32)],
        compiler_params=_cp("arbitrary"))(dz, a2, pm, ln_g, ln_b, w_pool, scale)


def _even_du(name, u, da1, dbp, dpm):
    t, c = da1.shape
    tr = _tile(t, 256, SUBLANES)

    def body(u_ref, da_ref, dbp_ref, dpm_ref, du_ref):
        val = u_ref[:, 0:c].astype(F32)
        sg = _sigmoid(u_ref[:, c:2 * c].astype(F32))
        da = da_ref[...].astype(F32)
        du_ref[:, 0:c] = (da * sg).astype(BF16)
        du_ref[:, c:2 * c] = (da * val * sg * (1.0 - sg)).astype(BF16)
        du_ref[:, 2 * c:3 * c] = (dbp_ref[...].astype(F32) - dpm_ref[...].astype(F32)).astype(BF16)

    row = pl.BlockSpec((tr, c), lambda i: (i, 0))
    wide = pl.BlockSpec((tr, 3 * c), lambda i: (i, 0))
    return pl.pallas_call(
        body, name=name, grid=(t // tr,), in_specs=[wide, row, row, row], out_specs=wide,
        out_shape=jax.ShapeDtypeStruct((t, 3 * c), BF16),
        compiler_params=_cp("parallel"))(u, da1, dbp, dpm)


def _odd_du(name, u, dy, co, dxc):
    t, c = dy.shape
    tr = _tile(t, 256, SUBLANES)

    def body(u_ref, dy_ref, co_ref, dx_ref, du_ref):
        dx = dx_ref[...].astype(F32)
        du_ref[:, 0:c] = (dy_ref[...].astype(F32) * co_ref[...].astype(F32)).astype(BF16)
        du_ref[:, c:2 * c] = (dx * u_ref[:, 2 * c:3 * c].astype(F32)).astype(BF16)
        du_ref[:, 2 * c:3 * c] = (dx * u_ref[:, c:2 * c].astype(F32)).astype(BF16)

    row = pl.BlockSpec((tr, c), lambda i: (i, 0))
    wide = pl.BlockSpec((tr, 3 * c), lambda i: (i, 0))
    return pl.pallas_call(
        body, name=name, grid=(t // tr,), in_specs=[wide, row, row, row], out_specs=wide,
        out_shape=jax.ShapeDtypeStruct((t, 3 * c), BF16),
        compiler_params=_cp("parallel"))(u, dy, co, dxc)


def _local_step(x, tgt, seq, small, get_w, put_g, sync):
    t, d = x.shape
    c = d // 2
    cw_e, cw_o = small["conv_w_e"], small["conv_w_o"]
    wp = small["w_pool_e"].astype(BF16)
    ptaps = _pool_taps(c)
    row = lambda v: v.reshape(1, -1)

    we = {"w_in": get_w("in_e", x)[0]}
    n0 = _rms_fwd("rms_fwd_mix0", x, row(small["mix_norm_e"]))
    u0 = _mm_nn("mm_in_e", n0, we["w_in"], BF16)
    sync("fwd_a", u0)
    a2 = _conv_fwd("conv_e_fwd", seq, c, cw_e, u0, 0, u0, c, "glu", bias=row(small["conv_b_e"]))
    ws = _conv_fwd("pool_fwd", seq, c, ptaps, u0, 2 * c)
    z0, pm = _even_fwd("even_fwd", seq, a2, ws, u0, row(small["ln_g_e"]), row(small["ln_b_e"]),
                       wp, row(small["pool_scale_e"]))
    we["w_out"] = get_w("out_e", z0)[0]
    h1 = _mm_nn("mm_out_e", z0, we["w_out"], F32, res=x)
    sync("fwd_b", h1)
    n1 = _rms_fwd("rms_fwd_ffn0", h1, row(small["ffn_norm"][0]))
    wf0 = dict(zip(("w_gate", "w_up"), get_w("gu0", n1)))
    act0, ds0, s0 = _ffn_fwd("ffn0_fwd", n1, wf0["w_gate"], wf0["w_up"])
    dep = sync("fwd_c", act0)
    wf0["w_down"] = get_w("down0", act0)[0]
    h2 = _mm_nn("mm_down0", act0, wf0["w_down"], F32, res=h1, dep=dep)
    dep = sync("fwd_d", h2)
    n2 = _rms_fwd("rms_fwd_mix1", h2, row(small["mix_norm_o"]))
    wo = {"w_in": get_w("in_o", n2)[0]}
    u1 = _mm_nn("mm_in_o", n2, wo["w_in"], BF16, dep=dep)
    co, y1 = _conv_fwd("conv_o_fwd", seq, d, cw_o, u1, d, u1, 2 * d, "mul", post=u1, cpost=0)
    dep = sync("fwd_e", y1)
    wo["w_out"] = get_w("out_o", y1)[0]
    h3 = _mm_nn("mm_out_o", y1, wo["w_out"], F32, res=h2, dep=dep)
    sync("fwd_f", h3)
    n3 = _rms_fwd("rms_fwd_ffn1", h3, row(small["ffn_norm"][1]))
    wf1 = dict(zip(("w_gate", "w_up"), get_w("gu1", n3)))
    act1, ds1, s1 = _ffn_fwd("ffn1_fwd", n3, wf1["w_gate"], wf1["w_up"])
    wf1["w_down"] = get_w("down1", act1)[0]
    h4 = _mm_nn("mm_down1", act1, wf1["w_down"], F32, res=h3)

    dh4, dh4b, d_final, lsum = _loss_head("loss_head", h4, row(small["final_norm"]), tgt)

    def ffn_bwd(tag, dh, dhb, h_in, gain, n, dsilu, silu, act, w, dep):
        dg, dup = _ffn_bwd_act("ffn%s_bwd_act" % tag, dhb, w["w_down"], dsilu, silu, dep=dep)
        dwd = _mm_tn("mm_dwd%s" % tag, act, dhb, BF16)
        dwg = _mm_tn("mm_dwg%s" % tag, dg, n, BF16, dep=sync("bwd_ffn" + tag, dwd))
        dwu = _mm_tn("mm_dwu%s" % tag, dup, n, BF16)
        dn = _mm_nn("mm_ffn_dn%s" % tag, [dg, dup], [w["w_gate"], w["w_up"]], BF16)
        dh_in, dhb_in, dgain = _rms_bwd("rms_bwd_ffn%s" % tag, h_in, gain, dn, dh)
        dep = put_g("ffn" + tag, {"w_gate": dwg, "w_up": dwu, "w_down": dwd})
        return dh_in, dhb_in, dgain, dep

    dh3, dh3b, d_ffn1, dep = ffn_bwd("1", dh4, dh4b, h3, row(small["ffn_norm"][1]), n3, ds1, s1,
                                     act1, wf1, None)
    dw_out_o = _mm_tn("mm_dw_out_o", y1, dh3b, BF16, dep=dep)
    dy1 = _mm_nt("mm_dy_o", dh3b, wo["w_out"], BF16, dep=sync("bwd_mix_o", dw_out_o))
    dxc, dcw_o = _conv_bwd("conv_o_bwd", seq, d, cw_o, dy1, 0, u1, 0, "mul",
                           x1=u1, c1=d, x2=u1, c2=2 * d, pre="mul")
    du1 = _odd_du("odd_du", u1, dy1, co, dxc)
    dw_in_o = _mm_tn("mm_dw_in_o", n2, du1, BF16)
    dn2 = _mm_nt("mm_dn_o", du1, wo["w_in"], BF16)
    dh2, dh2b, d_mix_o = _rms_bwd("rms_bwd_mix1", h2, row(small["mix_norm_o"]), dn2, dh3)
    dep = put_g("mix_o", {"w_in": dw_in_o, "w_out": dw_out_o})

    dh1, dh1b, d_ffn0, dep = ffn_bwd("0", dh2, dh2b, h1, row(small["ffn_norm"][0]), n1, ds0, s0,
                                     act0, wf0, dep)
    dw_out_e = _mm_tn("mm_dw_out_e", z0, dh1b, BF16, dep=dep)
    dz0 = _mm_nt("mm_dz_e", dh1b, we["w_out"], BF16, dep=sync("bwd_mix_e", dw_out_e))
    da2, dws, dpm, vecs, dwp = _even_bwd("even_bwd", seq, dz0, a2, pm, row(small["ln_g_e"]),
                                         row(small["ln_b_e"]), wp, row(small["pool_scale_e"]))
    da1, dcw_e = _conv_bwd("conv_e_bwd", seq, c, cw_e, da2, 0, x1=u0, c1=0, x2=u0, c2=c, pre="glu")
    dbp = _conv_bwd("pool_bwd", seq, c, ptaps, dws, 0)
    du0 = _even_du("even_du", u0, da1, dbp, dpm)
    dep = put_g("small", {"conv_w_e": dcw_e, "conv_b_e": vecs[2], "ln_g_e": vecs[0], "ln_b_e": vecs[1],
                          "w_pool_e": dwp, "pool_scale_e": vecs[3], "mix_norm_o": d_mix_o[0],
                          "conv_w_o": dcw_o, "ffn_norm": jnp.concatenate([d_ffn0, d_ffn1], axis=0),
                          "final_norm": d_final[0]})
    dw_in_e = _mm_tn("mm_dw_in_e", n0, du0, BF16, dep=dep)
    dep = put_g("mix_e", {"w_in": dw_in_e, "w_out": dw_out_e})
    dn0 = _mm_nt("mm_dn_e", du0, we["w_in"], BF16, dep=dep)
    dx, _, d_mix_e = _rms_bwd("rms_bwd_mix0", x, row(small["mix_norm_e"]), dn0, dh1)
    return lsum, dx, d_mix_e[0]


def _place():
    x, y, c = (lax.axis_index(a) for a in MESH_AXES)
    return x, y, c


def _index(p):
    return 4 * p[0] + 2 * p[1] + p[2]


def _slab(ref, kind, d, n):
    if kind == "blk":
        return ref.at[d]
    return ref.at[:, pl.ds(pl.multiple_of(d * n, LANES), n)]


HBM = pl.BlockSpec(memory_space=pltpu.HBM)
SEM = pl.BlockSpec(memory_space=pltpu.SEMAPHORE)
EFFECT = pltpu.SideEffectType.DATAFLOW_SIDE_EFFECTING
NCHIPS = 4


def _in_hbm(a):
    return pltpu.with_memory_space_constraint(a, pltpu.HBM)


def _gathered_shape(s, kind):
    m, n = s.shape
    return (NDEV, m, n) if kind == "blk" else (m, NDEV * n)


def _first_targets():
    x, y, c = _place()
    return [(x, y, 1 - c), (1 - x, y, c), (x, 1 - y, c), (1 - x, 1 - y, c)]


def _gather_start(name, shards, kinds, after):
    na = len(shards)

    def body(*refs):
        x_refs, land_refs = refs[:na], refs[na:2 * na]
        send_sems, recv_sems = refs[2 * na + 1], refs[2 * na + 2]
        token = refs[-1]
        me = _index(_place())
        for a in range(na):
            for k, to in enumerate(_first_targets()):
                pltpu.make_async_remote_copy(
                    src_ref=x_refs[a], dst_ref=_slab(land_refs[a], kinds[a], me, shards[a].shape[1]),
                    send_sem=send_sems.at[4 * a + k], recv_sem=recv_sems.at[4 * a + k],
                    device_id=to, device_id_type=MESH).start()
        token[...] = jnp.zeros_like(token)

    lands = [lax.empty(_gathered_shape(s, k), s.dtype) for s, k in zip(shards, kinds)]
    outs = pl.pallas_call(
        body, name=name,
        out_shape=(pltpu.SemaphoreType.DMA((4 * na,)), pltpu.SemaphoreType.DMA((4 * na,)),
                   *[pltpu.HBM(s.shape, s.dtype) for s in shards],
                   *[pltpu.HBM(l.shape, l.dtype) for l in lands],
                   jax.ShapeDtypeStruct((SUBLANES, LANES), F32)),
        in_specs=[HBM] * (2 * na) + [ANY],
        out_specs=(SEM, SEM, *[HBM] * (2 * na), pl.BlockSpec(memory_space=pltpu.VMEM)),
        input_output_aliases={i: 2 + i for i in range(2 * na)},
        compiler_params=pltpu.CompilerParams(has_side_effects=EFFECT),
    )(*[_in_hbm(s) for s in shards], *[_in_hbm(l) for l in lands], after)
    return outs[0], outs[1], outs[2:2 + na], outs[2 + na:2 + 2 * na], outs[-1]


def _gather_wait(name, started, kinds, after):
    send_sems, recv_sems, shards, lands, _ = started
    na = len(shards)

    def body(*refs):
        x_refs, land_refs = refs[:na], refs[na:2 * na]
        s_sems, r_sems = refs[2 * na], refs[2 * na + 1]
        for a in range(na):
            for k, frm in enumerate(_first_targets()):
                cp = pltpu.make_async_remote_copy(
                    src_ref=x_refs[a],
                    dst_ref=_slab(land_refs[a], kinds[a], _index(frm), shards[a].shape[1]),
                    send_sem=s_sems.at[4 * a + k], recv_sem=r_sems.at[4 * a + k],
                    device_id=frm, device_id_type=MESH)
                cp.wait_send()
                cp.wait_recv()

    outs = pl.pallas_call(
        body, name=name,
        out_shape=(*[pltpu.HBM(s.shape, s.dtype) for s in shards],
                   *[pltpu.HBM(l.shape, l.dtype) for l in lands]),
        in_specs=[HBM] * (2 * na) + [SEM, SEM, ANY], out_specs=[HBM] * (2 * na),
        input_output_aliases={i: i for i in range(2 * na)},
        compiler_params=pltpu.CompilerParams(has_side_effects=EFFECT),
    )(*shards, *lands, send_sems, recv_sems, after)
    return outs[:na], outs[na:]


def _split_start(name, bufs, ncopies, plan, after):
    nb = len(bufs)

    def body(*refs):
        send_sems, recv_sems, token = refs[nb + 1], refs[nb + 2], refs[-1]
        for k, (src, dst, to, _) in enumerate(plan(refs[:nb])):
            pltpu.make_async_remote_copy(src_ref=src, dst_ref=dst, send_sem=send_sems.at[k],
                                         recv_sem=recv_sems.at[k], device_id=to, device_id_type=MESH).start()
        token[...] = jnp.zeros_like(token)

    outs = pl.pallas_call(
        body, name=name,
        out_shape=(pltpu.SemaphoreType.DMA((ncopies,)), pltpu.SemaphoreType.DMA((ncopies,)),
                   *[pltpu.HBM(b.shape, b.dtype) for b in bufs],
                   jax.ShapeDtypeStruct((SUBLANES, LANES), F32)),
        in_specs=[HBM] * nb + [ANY],
        out_specs=(SEM, SEM, *[HBM] * nb, pl.BlockSpec(memory_space=pltpu.VMEM)),
        input_output_aliases={i: 2 + i for i in range(nb)},
        compiler_params=pltpu.CompilerParams(has_side_effects=EFFECT),
    )(*[_in_hbm(b) for b in bufs], after)
    return outs[0], outs[1], list(outs[2:2 + nb]), outs[-1]


def _split_wait(name, started, plan, after):
    send_sems, recv_sems, bufs, _ = started
    nb = len(bufs)

    def body(*refs):
        s_sems, r_sems = refs[nb], refs[nb + 1]
        for k, (src, _, to, landed) in enumerate(plan(refs[:nb])):
            cp = pltpu.make_async_remote_copy(src_ref=src, dst_ref=landed, send_sem=s_sems.at[k],
                                              recv_sem=r_sems.at[k], device_id=to, device_id_type=MESH)
            cp.wait_send()
            cp.wait_recv()

    outs = pl.pallas_call(
        body, name=name, out_shape=tuple(pltpu.HBM(b.shape, b.dtype) for b in bufs),
        in_specs=[HBM] * nb + [SEM, SEM, ANY], out_specs=[HBM] * nb,
        input_output_aliases={i: i for i in range(nb)},
        compiler_params=pltpu.CompilerParams(has_side_effects=EFFECT),
    )(*bufs, send_sems, recv_sems, after)
    return list(outs)


def _forward_plan(kinds, nloc):
    def plan(lands):
        x, y, c = _place()
        out = []
        for a, land in enumerate(lands):
            for chip in [(1 - x, y), (x, 1 - y), (1 - x, 1 - y)]:
                mine = _slab(land, kinds[a], _index((*chip, c)), nloc[a])
                out.append((mine, mine, (x, y, 1 - c), _slab(land, kinds[a], _index((*chip, 1 - c)), nloc[a])))
        return out
    return plan


def _own_copy(name, shard, land, kind, me):
    m, n = shard.shape
    tr = _tile(m, max(SUBLANES, 1048576 // n), SUBLANES)

    def body(s_ref, x_ref, land_ref, o_ref):
        o_ref[...] = x_ref[...]

    if kind == "blk":
        o_spec = pl.BlockSpec((None, tr, n), lambda i, s: (s[0], i, 0))
    else:
        o_spec = pl.BlockSpec((tr, n), lambda i, s: (i, s[0]))
    return pl.pallas_call(
        body, name=name,
        grid_spec=pltpu.PrefetchScalarGridSpec(
            num_scalar_prefetch=1, grid=(m // tr,),
            in_specs=[pl.BlockSpec((tr, n), lambda i, s: (i, 0)), ANY], out_specs=o_spec),
        out_shape=jax.ShapeDtypeStruct(land.shape, land.dtype),
        input_output_aliases={2: 0}, compiler_params=_cp("parallel"))(me, shard, land)


def _everyone_plan(refs):
    x, y, c = _place()
    out = []
    for dx, dy, dc in [(a, b, e) for a in (0, 1) for b in (0, 1) for e in (0, 1)][1:]:
        peer = (x ^ dx, y ^ dy, c ^ dc)
        out.append((refs[0], refs[1].at[_index((x, y, c))], peer, refs[1].at[_index(peer)]))
    return out


def _pair_plan(kinds, nloc):
    na = len(kinds)

    def plan(refs):
        x, y, c = _place()
        out = []
        for a in range(na):
            for j in range(NCHIPS):
                dst = refs[na + a].at[j]
                out.append((_slab(refs[a], kinds[a], 2 * j + (1 - c), nloc[a]), dst, (x, y, 1 - c), dst))
        return out
    return plan


def _chip_sum(name, full, kind, n, from_sib, place):
    _, m, _ = from_sib.shape
    tr = _tile(m, max(SUBLANES, 1048576 // n), SUBLANES)

    def body(s_ref, mine_ref, sib_ref, csum_ref, land_ref):
        v = (mine_ref[...].astype(F32) + sib_ref[...].astype(F32)).astype(csum_ref.dtype)
        csum_ref[...] = v

        @pl.when(pl.program_id(1) == s_ref[1])
        def _():
            land_ref[...] = v

    if kind == "blk":
        mine_spec = pl.BlockSpec((None, tr, n), lambda i, j, s: (2 * j + s[0], i, 0))
    else:
        mine_spec = pl.BlockSpec((tr, n), lambda i, j, s: (i, 2 * j + s[0]))
    slot = pl.BlockSpec((None, tr, n), lambda i, j, s: (j, i, 0))
    shp = jax.ShapeDtypeStruct((NCHIPS, m, n), from_sib.dtype)
    return pl.pallas_call(
        body, name=name,
        grid_spec=pltpu.PrefetchScalarGridSpec(
            num_scalar_prefetch=1, grid=(m // tr, NCHIPS), in_specs=[mine_spec, slot],
            out_specs=[slot, pl.BlockSpec((None, tr, n), lambda i, j, s: (s[1], i, 0))]),
        out_shape=[shp, shp], compiler_params=_cp("parallel", "arbitrary"))(place, full, from_sib)


def _other_chips():
    x, y, c = _place()
    return [(1 - x, y, c), (x, 1 - y, c), (1 - x, 1 - y, c)]


def _scatter_start(name, csums, lands, after):
    na = len(csums)

    def body(*refs):
        c_refs, land_refs = refs[:na], refs[na:2 * na]
        send_sems, recv_sems = refs[2 * na + 1], refs[2 * na + 2]
        token = refs[-1]
        x, y, _ = _place()
        for a in range(na):
            for k, to in enumerate(_other_chips()):
                pltpu.make_async_remote_copy(
                    src_ref=c_refs[a].at[2 * to[0] + to[1]], dst_ref=land_refs[a].at[2 * x + y],
                    send_sem=send_sems.at[3 * a + k], recv_sem=recv_sems.at[3 * a + k],
                    device_id=to, device_id_type=MESH).start()
        token[...] = jnp.zeros_like(token)

    outs = pl.pallas_call(
        body, name=name,
        out_shape=(pltpu.SemaphoreType.DMA((3 * na,)), pltpu.SemaphoreType.DMA((3 * na,)),
                   *[pltpu.HBM(s.shape, s.dtype) for s in csums],
                   *[pltpu.HBM(l.shape, l.dtype) for l in lands],
                   jax.ShapeDtypeStruct((SUBLANES, LANES), F32)),
        in_specs=[HBM] * (2 * na) + [ANY],
        out_specs=(SEM, SEM, *[HBM] * (2 * na), pl.BlockSpec(memory_space=pltpu.VMEM)),
        input_output_aliases={i: 2 + i for i in range(2 * na)},
        compiler_params=pltpu.CompilerParams(has_side_effects=EFFECT),
    )(*[_in_hbm(s) for s in csums], *[_in_hbm(l) for l in lands], after)
    return outs[0], outs[1], outs[2:2 + na], outs[2 + na:2 + 2 * na], outs[-1]


def _scatter_wait(name, started, after):
    send_sems, recv_sems, csums, lands, _ = started
    na = len(csums)

    def body(*refs):
        c_refs, land_refs = refs[:na], refs[na:2 * na]
        s_sems, r_sems = refs[2 * na], refs[2 * na + 1]
        for a in range(na):
            for k, frm in enumerate(_other_chips()):
                cp = pltpu.make_async_remote_copy(
                    src_ref=c_refs[a].at[2 * frm[0] + frm[1]], dst_ref=land_refs[a].at[2 * frm[0] + frm[1]],
                    send_sem=s_sems.at[3 * a + k], recv_sem=r_sems.at[3 * a + k],
                    device_id=frm, device_id_type=MESH)
                cp.wait_send()
                cp.wait_recv()

    outs = pl.pallas_call(
        body, name=name,
        out_shape=(*[pltpu.HBM(s.shape, s.dtype) for s in csums],
                   *[pltpu.HBM(l.shape, l.dtype) for l in lands]),
        in_specs=[HBM] * (2 * na) + [SEM, SEM, ANY], out_specs=[HBM] * (2 * na),
        input_output_aliases={i: i for i in range(2 * na)},
        compiler_params=pltpu.CompilerParams(has_side_effects=EFFECT),
    )(*csums, *lands, send_sems, recv_sems, after)
    return outs[na:]


def _adam_math(w, g, m, v):
    m = ADAM_B1 * m + (1.0 - ADAM_B1) * g
    v = ADAM_B2 * v + (1.0 - ADAM_B2) * (g * g)
    m_hat = m / (1.0 - ADAM_B1 ** ADAM_STEP)
    v_hat = v / (1.0 - ADAM_B2 ** ADAM_STEP)
    delta = -ADAM_LR * (m_hat / (jnp.sqrt(v_hat) + ADAM_EPS) + ADAM_WD * w)
    return delta, m, v


def _sum_adamw(name, parts, w, m, v, layer, prev=None, dep=None):
    nl, r, c = w.shape
    nparts = parts.shape[0]
    tr = _tile(r, max(SUBLANES, 262144 // c), SUBLANES)

    def body(p_ref, w_ref, m_ref, v_ref, *rest):
        g_ref, d_ref, mo_ref, vo_ref = rest[-4:]
        g = p_ref[0].astype(F32)
        for s in range(1, nparts):
            g = g + p_ref[s].astype(F32)
        delta, mn, vn = _adam_math(w_ref[...], g, m_ref[...], v_ref[...])
        g_ref[...] = g
        d_ref[...] = delta
        mo_ref[...] = mn
        vo_ref[...] = vn

    row = pl.BlockSpec((None, tr, c), lambda i: (layer, i, 0))
    shp = jax.ShapeDtypeStruct((nl, r, c), F32)
    extra = ([] if prev is None else list(prev)) + ([] if dep is None else [dep])
    return pl.pallas_call(
        body, name=name, grid=(r // tr,),
        in_specs=[pl.BlockSpec((nparts, tr, c), lambda i: (0, i, 0)), row, row, row] + [ANY] * len(extra),
        out_specs=[row, row, row, row], out_shape=[shp, shp, shp, shp],
        input_output_aliases={} if prev is None else {4 + i: i for i in range(4)},
        compiler_params=_cp("parallel"))(parts, w, m, v, *extra)


def _sum_parts(name, parts):
    _, r, c = parts.shape

    def body(p_ref, o_ref):
        g = p_ref[0]
        for s in range(1, NDEV):
            g = g + p_ref[s]
        o_ref[...] = g

    return pl.pallas_call(
        body, name=name, grid=(1,),
        in_specs=[pl.BlockSpec((NDEV, r, c), lambda i: (0, 0, 0))],
        out_specs=pl.BlockSpec((r, c), lambda i: (0, 0)),
        out_shape=jax.ShapeDtypeStruct((r, c), F32), compiler_params=_cp("arbitrary"))(parts)


def _adamw(name, w, g, m, v):
    r, c = w.shape

    def body(w_ref, g_ref, m_ref, v_ref, d_ref, mo_ref, vo_ref):
        delta, mn, vn = _adam_math(w_ref[...], g_ref[...], m_ref[...], v_ref[...])
        d_ref[...] = delta
        mo_ref[...] = mn
        vo_ref[...] = vn

    full = pl.BlockSpec((r, c), lambda i: (0, 0))
    shp = jax.ShapeDtypeStruct((r, c), F32)
    return pl.pallas_call(
        body, name=name, grid=(1,), in_specs=[full] * 4, out_specs=[full] * 3,
        out_shape=[shp] * 3, compiler_params=_cp("arbitrary"))(w, g, m, v)


def _pack(arrays):
    flat = jnp.concatenate([a.reshape(-1) for a in arrays])
    unit = SUBLANES * LANES
    pad = (-flat.shape[0]) % unit
    return jnp.pad(flat, (0, pad)).reshape(-1, LANES)


def _unpack(buf, shapes):
    flat = buf.reshape(-1)
    out, off = [], 0
    for shp in shapes:
        size = 1
        for s in shp:
            size *= s
        out.append(flat[off:off + size].reshape(shp))
        off += size
    return out


WEIGHTS = ["mix_norm_e", "w_in_e", "conv_w_e", "conv_b_e", "ln_g_e", "ln_b_e", "w_pool_e",
           "pool_scale_e", "w_out_e", "mix_norm_o", "w_in_o", "conv_w_o", "w_out_o", "ffn_norm",
           "w_gate", "w_up", "w_down", "final_norm"]
BIG = ["w_in_e", "w_out_e", "w_in_o", "w_out_o", "w_gate", "w_up", "w_down"]
SHARDED_SMALL = {"conv_w_e": 1, "w_pool_e": 1, "mix_norm_o": 0, "conv_w_o": 1}
SMALL = [n for n in WEIGHTS if n not in BIG]


def kernel(x, mix_norm_e, w_in_e, conv_w_e, conv_b_e, ln_g_e, ln_b_e, w_pool_e, pool_scale_e, w_out_e, mix_norm_o, w_in_o, conv_w_o, w_out_o, ffn_norm, w_gate, w_up, w_down, final_norm, loss_target, m_mix_norm_e, m_w_in_e, m_conv_w_e, m_conv_b_e, m_ln_g_e, m_ln_b_e, m_w_pool_e, m_pool_scale_e, m_w_out_e, m_mix_norm_o, m_w_in_o, m_conv_w_o, m_w_out_o, m_ffn_norm, m_w_gate, m_w_up, m_w_down, m_final_norm, v_mix_norm_e, v_w_in_e, v_conv_w_e, v_conv_b_e, v_ln_g_e, v_ln_b_e, v_w_pool_e, v_pool_scale_e, v_w_out_e, v_mix_norm_o, v_w_in_o, v_conv_w_o, v_w_out_o, v_ffn_norm, v_w_gate, v_w_up, v_w_down, v_final_norm):
    wts = dict(zip(WEIGHTS, [mix_norm_e, w_in_e, conv_w_e, conv_b_e, ln_g_e, ln_b_e, w_pool_e, pool_scale_e, w_out_e, mix_norm_o, w_in_o, conv_w_o, w_out_o, ffn_norm, w_gate, w_up, w_down, final_norm]))
    mom = dict(zip(WEIGHTS, [m_mix_norm_e, m_w_in_e, m_conv_w_e, m_conv_b_e, m_ln_g_e, m_ln_b_e, m_w_pool_e, m_pool_scale_e, m_w_out_e, m_mix_norm_o, m_w_in_o, m_conv_w_o, m_w_out_o, m_ffn_norm, m_w_gate, m_w_up, m_w_down, m_final_norm]))
    var = dict(zip(WEIGHTS, [v_mix_norm_e, v_w_in_e, v_conv_w_e, v_conv_b_e, v_ln_g_e, v_ln_b_e, v_w_pool_e, v_pool_scale_e, v_w_out_e, v_mix_norm_o, v_w_in_o, v_conv_w_o, v_w_out_o, v_ffn_norm, v_w_gate, v_w_up, v_w_down, v_final_norm]))
    bsz, seq, d = x.shape
    t = bsz * seq
    me = _index(_place())
    me_arr = jnp.reshape(me, (1,)).astype(jnp.int32)

    sh_names = list(SHARDED_SMALL)
    sh_local = [wts[n][0] for n in sh_names]
    packed = _pack(sh_local)
    params_st = _split_start("small_params_start", [packed, lax.empty((NDEV,) + packed.shape, F32)], NDEV - 1,
                             _everyone_plan, x)

    for state in (wts, mom, var):
        for n in ("w_gate", "w_up"):
            state[n] = jnp.swapaxes(state[n], 1, 2)
    bf = lambda a: a.astype(BF16)
    mix_kinds, ffn_kinds = ["col", "blk"], ["blk", "blk", "blk"]
    ffn_names = ("w_gate", "w_up", "w_down")
    groups = {
        "mix_e": ([w_in_e.shape[2], d], mix_kinds, [("w_in_e", 0), ("w_out_e", 0)]),
        "ffn0": ([d, d, d], ffn_kinds, [(n, 0) for n in ffn_names]),
        "mix_o": ([w_in_o.shape[2], d], mix_kinds, [("w_in_o", 0), ("w_out_o", 0)]),
        "ffn1": ([d, d, d], ffn_kinds, [(n, 1) for n in ffn_names]),
    }
    gathers = {
        "in_e": ([bf(w_in_e[0])], ["col"]), "out_e": ([bf(w_out_e[0])], ["blk"]),
        "gu0": ([bf(wts["w_gate"][0]), bf(wts["w_up"][0])], ["blk", "blk"]), "down0": ([bf(w_down[0])], ["blk"]),
        "in_o": ([bf(w_in_o[0])], ["col"]), "out_o": ([bf(w_out_o[0])], ["blk"]),
        "gu1": ([bf(wts["w_gate"][1]), bf(wts["w_up"][1])], ["blk", "blk"]), "down1": ([bf(w_down[1])], ["blk"]),
    }
    started, prev = {}, params_st[3]
    for grp, (shards, kinds) in gathers.items():
        started[grp] = _gather_start("gather_start_" + grp, shards, kinds, prev)
        prev = started[grp][4]
    all_started = prev[0, 0:1]

    bufs = _split_wait("small_params_wait", params_st, _everyone_plan, prev)
    gathered = _own_copy("small_params_own", bufs[0], bufs[1], "blk", me_arr)
    small = {n: wts[n][0] for n in SMALL if n not in SHARDED_SMALL and n not in ("ffn_norm", "final_norm")}
    small["ffn_norm"], small["final_norm"] = ffn_norm, final_norm
    per_dev = [_unpack(gathered[s], [a.shape for a in sh_local]) for s in range(NDEV)]
    for i, n in enumerate(sh_names):
        small[n] = jnp.concatenate([per_dev[s][i] for s in range(NDEV)], axis=SHARDED_SMALL[n])

    passing, shards_of = {}, {}

    def pass_on(grp, after):
        shards, kinds = gathers[grp]
        shards_of[grp], lands = _gather_wait("gather_wait_" + grp, started[grp], kinds, after)
        plan = _forward_plan(kinds, [s.shape[1] for s in shards])
        passing[grp] = (_split_start("forward_start_" + grp, lands, 3 * len(lands), plan, after), plan)
        return passing[grp][0][3]

    def get_w(grp, after):
        if grp not in passing:
            after = pass_on(grp, after)
        st, plan = passing[grp]
        lands = _split_wait("forward_wait_" + grp, st, plan, after)
        full = [_own_copy("own_copy_%s%d" % (grp, a), shards_of[grp][a], lands[a], gathers[grp][1][a], me_arr)
                for a in range(len(lands))]
        return [f.reshape(-1, d) if kind == "blk" else f for f, kind in zip(full, gathers[grp][1])]

    cx, cy, cc = _place()
    place = jnp.stack([cc, 2 * cx + cy]).astype(jnp.int32)
    bwd_order = ["ffn1", "mix_o", "ffn0", "mix_e"]
    pairing, pending, results = {}, {}, {}

    early_names = [n for n in SMALL if n != "mix_norm_e"]
    small_sent = {}

    def send_small(tag, arrays, after):
        mine = _pack(arrays)
        small_sent[tag] = _split_start(tag + "_start", [mine, lax.empty((NDEV,) + mine.shape, F32)], NDEV - 1,
                                       _everyone_plan, after)
        return small_sent[tag][3]

    def summed_small(tag, shapes, after):
        bufs = _split_wait(tag + "_wait", small_sent[tag], _everyone_plan, after)
        parts = _own_copy(tag + "_own", bufs[0], bufs[1], "blk", me_arr)
        return _unpack(_sum_parts(tag + "_sum", parts), shapes)

    def put_g(grp, grads):
        if grp == "small":
            small_sent["shapes"] = [grads[n].shape for n in early_names]
            return send_small("small_grads", [grads[n] for n in early_names], place)
        nloc, kinds, _ = groups[grp]
        if len(kinds) == 2:
            fulls = [grads["w_in"], grads["w_out"].reshape(NDEV, -1, d)]
        else:
            fulls = [grads[n].reshape(NDEV, -1, d) for n in ffn_names]
        empties = []
        for g, kind, n in zip(fulls, kinds, nloc):
            empties.append(lax.empty((NCHIPS, g.shape[1] if kind == "blk" else g.shape[0], n), g.dtype))
        plan = _pair_plan(kinds, nloc)
        pairing[grp] = (_split_start("pair_start_" + grp, fulls + empties, NCHIPS * len(fulls), plan, place),
                        plan, kinds, nloc)
        token = pairing[grp][0][3]
        return send_sums(grp, token) if grp == bwd_order[-1] else token

    def send_sums(grp, after):
        st, plan, kinds, nloc = pairing[grp]
        bufs = _split_wait("pair_wait_" + grp, st, plan, after)
        na = len(kinds)
        sums = [_chip_sum("chip_sum_%s%d" % (grp, a), bufs[a], kinds[a], nloc[a], bufs[na + a], place)
                for a in range(na)]
        pending[grp] = _scatter_start("scatter_start_" + grp, [s[0] for s in sums], [s[1] for s in sums], after)
        return pending[grp][4]

    def finish(grp, after):
        lands = _scatter_wait("scatter_wait_" + grp, pending[grp], after)
        dep = None
        for (n, l), parts in zip(groups[grp][2], lands):
            results[n] = _sum_adamw("adamw_%s%d" % (n, l), parts, wts[n], mom[n], var[n], l, results.get(n), dep)
            dep = results[n][1]
        return dep

    fwd_sync = {"fwd_a": ["out_e"], "fwd_b": ["gu0"], "fwd_c": ["down0", "in_o"], "fwd_d": ["out_o"],
                "fwd_e": ["gu1"], "fwd_f": ["down1"]}

    def sync(tag, after):
        if tag in fwd_sync:
            for grp in fwd_sync[tag]:
                after = pass_on(grp, after)
            return after
        if tag == "bwd_mix_o":
            return send_sums("ffn1", after)
        if tag == "bwd_ffn0":
            return finish("ffn1", send_sums("mix_o", after))
        if tag == "bwd_mix_e":
            return finish("mix_o", send_sums("ffn0", after))
        return None

    small["mix_norm_e"] = small["mix_norm_e"] + all_started
    lsum, dx, d_mix_e = _local_step(x.reshape(t, d), loss_target.reshape(t, d), seq, small, get_w, put_g, sync)
    loss = lax.psum(jnp.sum(lsum), MESH_AXES)

    out_g, out_d, out_m, out_v = {}, {}, {}, {}

    dep = finish("ffn0", send_small("last_grad", [d_mix_e], dx))
    sums = dict(zip(early_names, summed_small("small_grads", small_sent["shapes"], dep)))
    sums["mix_norm_e"], = summed_small("last_grad", [d_mix_e.shape], dep)
    gs_sum = [sums[n] for n in SMALL]
    local_g = []
    for n, g in zip(SMALL, gs_sum):
        if n in SHARDED_SMALL:
            ax = SHARDED_SMALL[n]
            size = wts[n].shape[ax + 1]
            g = lax.dynamic_slice_in_dim(g, me * size, size, axis=ax)
        local_g.append(g.reshape(wts[n].shape))
    shapes = [wts[n].shape for n in SMALL]
    upd = _adamw("adamw_small", _pack([wts[n] for n in SMALL]), _pack(local_g),
                 _pack([mom[n] for n in SMALL]), _pack([var[n] for n in SMALL]))
    for i, outd in enumerate((out_d, out_m, out_v)):
        for n, a in zip(SMALL, _unpack(upd[i], shapes)):
            outd[n] = a
    for n, g in zip(SMALL, local_g):
        out_g[n] = g

    finish("mix_e", upd[0])
    for n in BIG:
        res = [jnp.swapaxes(a, 1, 2) for a in results[n]] if n in ("w_gate", "w_up") else results[n]
        out_g[n], out_d[n], out_m[n], out_v[n] = res

    return (loss, dx.reshape(bsz, seq, d), *[out_g[n] for n in WEIGHTS], *[out_d[n] for n in WEIGHTS],
            *[out_m[n] for n in WEIGHTS], *[out_v[n] for n in WEIGHTS])
```

```python
import jax
import jax.numpy as jnp
from jax import lax
from jax.experimental import pallas as pl
from jax.experimental.pallas import tpu as pltpu

F32 = jnp.float32
BF16 = jnp.bfloat16
NDEV = 8
MESH_AXES = ("x", "y", "c")
EPS = 1e-6
POOL_WINDOWS = (2, 4, 8, 16)
CONV_WIDTH = 31
SHORT_WIDTH = 3
ADAM_LR = 0.001
ADAM_B1 = 0.9
ADAM_B2 = 0.999
ADAM_EPS = 1e-08
ADAM_WD = 0.01
ADAM_STEP = 10
LANES = 128
SUBLANES = 8
VMEM_LIMIT = 56 * 1024 * 1024
MXU_DEPTH = 256
MM_TK = 2816
MESH = pl.DeviceIdType.MESH
ANY = pl.BlockSpec(memory_space=pl.ANY)


def _cp(*sem):
    return pltpu.CompilerParams(dimension_semantics=sem, vmem_limit_bytes=VMEM_LIMIT)


def _tile(n, pref, unit=LANES):
    if n <= pref:
        return n
    t = (pref // unit) * unit
    while t > unit and n % t:
        t -= unit
    assert n % t == 0, (n, pref)
    return t


def _sigmoid(v):
    return 0.5 * jnp.tanh(0.5 * v) + 0.5


def _mm(name, pairs, a_specs, b_specs, dims, out_shape, o_spec, grid, acc_shape,
        res=None, res_spec=None, dep=None):
    np_ = len(pairs)
    nk = grid[2]
    has_res = res is not None
    n_in = 2 * np_ + (1 if has_res else 0) + (0 if dep is None else 1)

    def body(*refs):
        a_refs = refs[:np_]
        b_refs = refs[np_:2 * np_]
        r_ref = refs[2 * np_] if has_res else None
        o_ref = refs[n_in]
        acc = refs[-1]

        def part():
            s = None
            for a_ref, b_ref in zip(a_refs, b_refs):
                blocks = [(a_ref[...], b_ref[...])] if len(a_ref.shape) == 2 else [
                    (a_ref[q], b_ref[q]) for q in range(a_ref.shape[0])]
                for av, bv in blocks:
                    d = lax.dot_general(av, bv, dims, preferred_element_type=F32)
                    s = d if s is None else s + d
            return s

        def finish(v):
            if has_res:
                v = v + r_ref[...]
            o_ref[...] = v.astype(o_ref.dtype)

        if nk == 1:
            finish(part())
        else:
            k = pl.program_id(2)

            @pl.when(k == 0)
            def _():
                acc[...] = part()

            @pl.when((k > 0) & (k < nk - 1))
            def _():
                acc[...] += part()

            @pl.when(k == nk - 1)
            def _():
                finish(acc[...] + part())

    ins = [p[0] for p in pairs] + [p[1] for p in pairs]
    specs = list(a_specs) + list(b_specs)
    if has_res:
        ins.append(res)
        specs.append(res_spec)
    if dep is not None:
        ins.append(dep)
        specs.append(ANY)
    return pl.pallas_call(
        body, name=name, grid=grid, in_specs=specs, out_specs=o_spec, out_shape=out_shape,
        scratch_shapes=[pltpu.VMEM(acc_shape if nk > 1 else (SUBLANES, LANES), F32)],
        compiler_params=_cp("parallel", "parallel", "arbitrary"))(*ins)


NN = (((1,), (0,)), ((), ()))
NT = (((1,), (1,)), ((), ()))
TN = (((0,), (0,)), ((), ()))


def _tiles_mk(m, kk, npairs=1):
    tk = _tile(kk, MM_TK, MXU_DEPTH)
    return _tile(m, 1024 if tk * npairs <= MM_TK else 512), tk


def _mm_nn(name, a, b, out_dtype, res=None, dep=None):
    pairs = list(zip(a, b)) if isinstance(a, (list, tuple)) else [(a, b)]
    m, kk = pairs[0][0].shape
    n = pairs[0][1].shape[1]
    tm, tk = _tiles_mk(m, kk, len(pairs))
    tn = _tile(n, 1024)
    return _mm(name, pairs,
               [pl.BlockSpec((tm, tk), lambda i, j, k: (i, k))] * len(pairs),
               [pl.BlockSpec((tk, tn), lambda i, j, k: (k, j))] * len(pairs), NN,
               jax.ShapeDtypeStruct((m, n), out_dtype),
               pl.BlockSpec((tm, tn), lambda i, j, k: (i, j)),
               (m // tm, n // tn, kk // tk), (tm, tn), res,
               pl.BlockSpec((tm, tn), lambda i, j, k: (i, j)), dep=dep)


def _mm_nt(name, a, b, out_dtype, dep=None):
    m, n = a.shape
    kk = b.shape[0]
    tn = _tile(kk, 1024)
    tm, tk = _tiles_mk(m, n)
    return _mm(name, [(a, b)],
               [pl.BlockSpec((tm, tk), lambda i, j, k: (i, k))],
               [pl.BlockSpec((tn, tk), lambda i, j, k: (j, k))], NT,
               jax.ShapeDtypeStruct((m, kk), out_dtype),
               pl.BlockSpec((tm, tn), lambda i, j, k: (i, j)),
               (m // tm, kk // tn, n // tk), (tm, tn), dep=dep)


def _mm_tn(name, a, b, out_dtype, dep=None):
    t, m = a.shape
    n = b.shape[1]
    tn = _tile(n, 1024)
    tm, tk = _tile(m, 1408), _tile(t, MM_TK, MXU_DEPTH)
    return _mm(name, [(a, b)],
               [pl.BlockSpec((tk, tm), lambda i, j, k: (k, i))],
               [pl.BlockSpec((tk, tn), lambda i, j, k: (k, j))], TN,
               jax.ShapeDtypeStruct((m, n), out_dtype),
               pl.BlockSpec((tm, tn), lambda i, j, k: (i, j)),
               (m // tm, n // tn, t // tk), (tm, tn), dep=dep)


def _ffn_fwd(name, n, wg, wu):
    f, d = wg.shape
    t = n.shape[0]
    tm, tn = _tile(t, 1024), _tile(f, 512)

    def body(n_ref, wg_ref, wu_ref, act_ref, ds_ref, s_ref):
        nv = n_ref[...]
        g = lax.dot_general(nv, wg_ref[...], NT, preferred_element_type=F32)
        up = lax.dot_general(nv, wu_ref[...], NT, preferred_element_type=F32)
        sg = _sigmoid(g)
        silu = g * sg
        act_ref[...] = (silu * up).astype(BF16)
        ds_ref[...] = (up * (sg * (1.0 + g * (1.0 - sg)))).astype(BF16)
        s_ref[...] = silu.astype(BF16)

    w_spec = pl.BlockSpec((tn, d), lambda j, i: (j, 0))
    o_spec = pl.BlockSpec((tm, tn), lambda j, i: (i, j))
    shp = jax.ShapeDtypeStruct((t, f), BF16)
    return pl.pallas_call(
        body, name=name, grid=(f // tn, t // tm),
        in_specs=[pl.BlockSpec((tm, d), lambda j, i: (i, 0)), w_spec, w_spec],
        out_specs=[o_spec, o_spec, o_spec], out_shape=[shp, shp, shp],
        compiler_params=_cp("parallel", "parallel"))(n, wg, wu)


def _ffn_bwd_act(name, dh, wd, dsilu, silu, dep=None):
    f, d = wd.shape
    t = dh.shape[0]
    tm, tn = _tile(t, 1024), _tile(f, 512)

    def body(dh_ref, wd_ref, ds_ref, s_ref, *rest):
        dg_ref, dup_ref = rest[-2:]
        da = lax.dot_general(dh_ref[...], wd_ref[...], NT, preferred_element_type=F32)
        dg_ref[...] = (da * ds_ref[...].astype(F32)).astype(BF16)
        dup_ref[...] = (da * s_ref[...].astype(F32)).astype(BF16)

    o_spec = pl.BlockSpec((tm, tn), lambda i, j: (i, j))
    shp = jax.ShapeDtypeStruct((t, f), BF16)
    return pl.pallas_call(
        body, name=name, grid=(t // tm, f // tn),
        in_specs=[pl.BlockSpec((tm, d), lambda i, j: (i, 0)),
                  pl.BlockSpec((tn, d), lambda i, j: (j, 0)), o_spec, o_spec]
        + ([] if dep is None else [ANY]),
        out_specs=[o_spec, o_spec], out_shape=[shp, shp],
        compiler_params=_cp("parallel", "parallel"))(dh, wd, dsilu, silu, *([] if dep is None else [dep]))


def _rms_fwd(name, h, gain):
    t, d = h.shape
    tr = _tile(t, 512, SUBLANES)

    def body(h_ref, g_ref, n_ref):
        hv = h_ref[...]
        r = lax.rsqrt(jnp.mean(hv * hv, axis=-1, keepdims=True) + EPS)
        n_ref[...] = (hv * r * g_ref[...]).astype(BF16)

    return pl.pallas_call(
        body, name=name, grid=(t // tr,),
        in_specs=[pl.BlockSpec((tr, d), lambda i: (i, 0)), pl.BlockSpec((1, d), lambda i: (0, 0))],
        out_specs=pl.BlockSpec((tr, d), lambda i: (i, 0)),
        out_shape=jax.ShapeDtypeStruct((t, d), BF16),
        compiler_params=_cp("parallel"))(h, gain)


def _rms_bwd_math(hv, gain, dn):
    d = hv.shape[-1]
    r = lax.rsqrt(jnp.mean(hv * hv, axis=-1, keepdims=True) + EPS)
    xhat = hv * r
    dxh = dn * gain
    dh = r * (dxh - xhat * (jnp.sum(dxh * xhat, axis=-1, keepdims=True) / d))
    dgain = jnp.sum(dn * xhat, axis=0, keepdims=True)
    return dh, dgain


def _rms_bwd(name, h, gain, dn, dres):
    t, d = h.shape
    tr = _tile(t, 256, SUBLANES)

    def body(h_ref, g_ref, dn_ref, dr_ref, dh_ref, dhb_ref, dg_ref):
        dh, dgain = _rms_bwd_math(h_ref[...], g_ref[...], dn_ref[...].astype(F32))
        dh = dh + dr_ref[...]
        dh_ref[...] = dh
        dhb_ref[...] = dh.astype(BF16)

        @pl.when(pl.program_id(0) == 0)
        def _():
            dg_ref[...] = dgain

        @pl.when(pl.program_id(0) > 0)
        def _():
            dg_ref[...] += dgain

    row = pl.BlockSpec((tr, d), lambda i: (i, 0))
    vec = pl.BlockSpec((1, d), lambda i: (0, 0))
    return pl.pallas_call(
        body, name=name, grid=(t // tr,), in_specs=[row, vec, row, row],
        out_specs=[row, row, vec],
        out_shape=[jax.ShapeDtypeStruct((t, d), F32), jax.ShapeDtypeStruct((t, d), BF16),
                   jax.ShapeDtypeStruct((1, d), F32)],
        compiler_params=_cp("arbitrary"))(h, gain, dn, dres)


def _loss_head(name, h, gain, tgt):
    t, d = h.shape
    tr = _tile(t, 256, SUBLANES)

    def body(h_ref, g_ref, t_ref, dh_ref, dhb_ref, dg_ref, ls_ref):
        hv = h_ref[...]
        gv = g_ref[...]
        r = lax.rsqrt(jnp.mean(hv * hv, axis=-1, keepdims=True) + EPS)
        err = hv * r * gv - t_ref[...]
        lsum = 0.5 * jnp.sum(err * err, axis=0, keepdims=True) / d
        dh, dgain = _rms_bwd_math(hv, gv, err / d)
        dh_ref[...] = dh
        dhb_ref[...] = dh.astype(BF16)

        @pl.when(pl.program_id(0) == 0)
        def _():
            dg_ref[...] = dgain
            ls_ref[...] = lsum

        @pl.when(pl.program_id(0) > 0)
        def _():
            dg_ref[...] += dgain
            ls_ref[...] += lsum

    row = pl.BlockSpec((tr, d), lambda i: (i, 0))
    vec = pl.BlockSpec((1, d), lambda i: (0, 0))
    return pl.pallas_call(
        body, name=name, grid=(t // tr,), in_specs=[row, vec, row],
        out_specs=[row, row, vec, vec],
        out_shape=[jax.ShapeDtypeStruct((t, d), F32), jax.ShapeDtypeStruct((t, d), BF16),
                   jax.ShapeDtypeStruct((1, d), F32), jax.ShapeDtypeStruct((1, d), F32)],
        compiler_params=_cp("arbitrary"))(h, gain, tgt)


def _conv_geom(t, seq, c, k, full_width=False):
    halo = 32 if k - 1 > SUBLANES else SUBLANES
    assert k - 1 <= halo
    tm = min(256 if halo > SUBLANES else 1024, seq // 2)
    tc = c if full_width else min(512, c)
    assert seq % tm == 0 and tm % halo == 0 and c % tc == 0 and t % seq == 0
    return halo, tm, tc, min(64 if halo > SUBLANES else 128, tm), min(LANES, tc)


def _pre(kind, a, b):
    if kind == "glu":
        return a * _sigmoid(b)
    if kind == "mul":
        return a * b
    return a


def _taps(k):
    return sorted((s % SUBLANES, s // SUBLANES, s) for s in range(k))


def _conv_fwd(name, seq, c, w, x1, c1, x2=None, c2=0, pre=None, bias=None, post=None, cpost=0, live=None):
    t = x1.shape[0]
    k = w.shape[0]
    halo, tm, tc, sr, sl = _conv_geom(t, seq, c, k, live is not None)
    nb, cps = tm // halo, seq // tm
    two = x2 is not None
    has_bias, has_post = bias is not None, post is not None

    def body(*refs):
        it = iter(refs)
        x1c, x1h = next(it), next(it)
        x2c, x2h = (next(it), next(it)) if two else (None, None)
        w_ref = next(it)
        b_ref = next(it) if has_bias else None
        p_ref = next(it) if has_post else None
        o_ref = next(it)
        y_ref = next(it) if has_post else None
        xs = next(it)
        first = (pl.program_id(1) % cps) == 0
        hv = _pre(pre, x1h[...].astype(F32), x2h[...].astype(F32) if two else None)
        xs[0:halo, :] = jnp.where(first, 0.0, hv)
        xs[halo:halo + tm, :] = _pre(pre, x1c[...].astype(F32), x2c[...].astype(F32) if two else None)
        for l0 in range(0, tc, sl):
            ls = slice(l0, l0 + sl)
            for r0 in range(0, tm, sr):
                win = xs[r0:r0 + sr + halo, ls]
                acc = jnp.zeros((sr, sl), F32)
                rolled = {}
                for r, q, s in _taps(k if live is None else live[l0 // sl]):
                    if r not in rolled:
                        rolled[r] = win if r == 0 else pltpu.roll(win, r, 0)
                    lo = halo - SUBLANES * q
                    acc = acc + w_ref[k - 1 - s:k - s, ls] * rolled[r][lo:lo + sr]
                if has_bias:
                    acc = acc + b_ref[:, ls]
                o_ref[r0:r0 + sr, ls] = acc.astype(o_ref.dtype)
                if has_post:
                    y_ref[r0:r0 + sr, ls] = (acc * p_ref[r0:r0 + sr, ls].astype(F32)).astype(y_ref.dtype)

    def cur(off):
        return pl.BlockSpec((tm, tc), lambda j, i: (i, off // tc + j))

    def prev(off):
        return pl.BlockSpec((halo, tc), lambda j, i: (jnp.maximum(i * nb - 1, 0), off // tc + j))

    ins, specs = [x1, x1], [cur(c1), prev(c1)]
    if two:
        ins += [x2, x2]
        specs += [cur(c2), prev(c2)]
    ins.append(w)
    specs.append(pl.BlockSpec((k, tc), lambda j, i: (0, j)))
    if has_bias:
        ins.append(bias)
        specs.append(pl.BlockSpec((1, tc), lambda j, i: (0, j)))
    if has_post:
        ins.append(post)
        specs.append(cur(cpost))
    o_spec = pl.BlockSpec((tm, tc), lambda j, i: (i, j))
    shp = jax.ShapeDtypeStruct((t, c), BF16)
    return pl.pallas_call(
        body, name=name, grid=(c // tc, t // tm), in_specs=specs,
        out_specs=[o_spec, o_spec] if has_post else o_spec,
        out_shape=[shp, shp] if has_post else shp,
        scratch_shapes=[pltpu.VMEM((halo + tm, tc), F32)],
        compiler_params=_cp("parallel", "parallel"))(*ins)


def _conv_bwd(name, seq, c, w, d1, cd1, d2=None, cd2=0, dpre=None,
              x1=None, c1=0, x2=None, c2=0, pre=None, live=None):
    t = d1.shape[0]
    k = w.shape[0]
    assert live is None or x1 is None
    halo, tm, tc, sr, sl = _conv_geom(t, seq, c, k, live is not None)
    nb, cps = tm // halo, seq // tm
    nchunks = t // tm
    dtwo, xtwo, has_x = d2 is not None, x2 is not None, x1 is not None

    def body(*refs):
        it = iter(refs)
        d1c, d1n = next(it), next(it)
        d2c, d2n = (next(it), next(it)) if dtwo else (None, None)
        x1c, x1h = (next(it), next(it)) if has_x else (None, None)
        x2c, x2h = (next(it), next(it)) if xtwo else (None, None)
        w_ref = next(it)
        dx_ref = next(it)
        dw_ref = next(it) if has_x else None
        ds = next(it)
        xs = next(it) if has_x else None
        i = pl.program_id(1)
        last = (i % cps) == cps - 1
        ds[0:tm, :] = _pre(dpre, d1c[...].astype(F32), d2c[...].astype(F32) if dtwo else None)
        nv = _pre(dpre, d1n[...].astype(F32), d2n[...].astype(F32) if dtwo else None)
        ds[tm:tm + halo, :] = jnp.where(last, 0.0, nv)
        if has_x:
            first = (i % cps) == 0
            hv = _pre(pre, x1h[...].astype(F32), x2h[...].astype(F32) if xtwo else None)
            xs[0:halo, :] = jnp.where(first, 0.0, hv)
            xs[halo:halo + tm, :] = _pre(pre, x1c[...].astype(F32), x2c[...].astype(F32) if xtwo else None)

            @pl.when(i == 0)
            def _():
                dw_ref[...] = jnp.zeros_like(dw_ref)

        for l0 in range(0, tc, sl):
            ls = slice(l0, l0 + sl)
            for r0 in range(0, tm, sr):
                win = ds[r0:r0 + sr + halo, ls]
                nrow = sr + halo
                acc = jnp.zeros((sr, sl), F32)
                rolled = {}
                for r, q, s in _taps(k if live is None else live[l0 // sl]):
                    if r not in rolled:
                        rolled[r] = win if r == 0 else pltpu.roll(win, nrow - r, 0)
                    lo = SUBLANES * q
                    acc = acc + w_ref[k - 1 - s:k - s, ls] * rolled[r][lo:lo + sr]
                dx_ref[r0:r0 + sr, ls] = acc.astype(dx_ref.dtype)
                if has_x:
                    dcur = win[0:sr]
                    xwin = xs[r0:r0 + sr + halo, ls]
                    xrolled = {}
                    for r, q, s in _taps(k):
                        if r not in xrolled:
                            xrolled[r] = xwin if r == 0 else pltpu.roll(xwin, r, 0)
                        lo = halo - SUBLANES * q
                        part = jnp.sum(dcur * xrolled[r][lo:lo + sr], axis=0, keepdims=True)
                        dw_ref[k - 1 - s:k - s, ls] += part

    def cur(off):
        return pl.BlockSpec((tm, tc), lambda j, i: (i, off // tc + j))

    def prev(off):
        return pl.BlockSpec((halo, tc), lambda j, i: (jnp.maximum(i * nb - 1, 0), off // tc + j))

    def nxt(off):
        return pl.BlockSpec((halo, tc),
                            lambda j, i: (jnp.minimum((i + 1) * nb, nchunks * nb - 1), off // tc + j))

    ins, specs = [d1, d1], [cur(cd1), nxt(cd1)]
    if dtwo:
        ins += [d2, d2]
        specs += [cur(cd2), nxt(cd2)]
    if has_x:
        ins += [x1, x1]
        specs += [cur(c1), prev(c1)]
    if xtwo:
        ins += [x2, x2]
        specs += [cur(c2), prev(c2)]
    ins.append(w)
    specs.append(pl.BlockSpec((k, tc), lambda j, i: (0, j)))
    o_specs = [pl.BlockSpec((tm, tc), lambda j, i: (i, j))]
    o_shapes = [jax.ShapeDtypeStruct((t, c), BF16)]
    scratch = [pltpu.VMEM((tm + halo, tc), F32)]
    if has_x:
        o_specs.append(pl.BlockSpec((k, tc), lambda j, i: (0, j)))
        o_shapes.append(jax.ShapeDtypeStruct((k, c), F32))
        scratch.append(pltpu.VMEM((halo + tm, tc), F32))
    out = pl.pallas_call(
        body, name=name, grid=(c // tc, t // tm), in_specs=specs, out_specs=o_specs,
        out_shape=o_shapes, scratch_shapes=scratch,
        compiler_params=_cp("parallel", "arbitrary"))(*ins)
    return out if has_x else out[0]


def _pool_taps(c):
    kmax = max(POOL_WINDOWS)
    grp = c // len(POOL_WINDOWS)
    cols = []
    for wdw in POOL_WINDOWS:
        col = jnp.concatenate([jnp.zeros((kmax - wdw,), F32), jnp.ones((wdw,), F32)])
        cols.append(jnp.tile(col[:, None], (1, grp)))
    return jnp.concatenate(cols, axis=1)


def _pool_live(c):
    grp, sl = c // len(POOL_WINDOWS), min(LANES, c)
    return tuple(max(POOL_WINDOWS[g] for g in range(l0 // grp, (l0 + sl - 1) // grp + 1))
                 for l0 in range(0, c, sl))


def _counts(i, tr, seq, grp):
    pos = (i * tr + lax.broadcasted_iota(jnp.int32, (tr, 1), 0)) % seq + 1
    return [1.0 / jnp.minimum(pos, wdw).astype(F32) for wdw in POOL_WINDOWS]


def _ln_stats(a2):
    mu = jnp.mean(a2, axis=-1, keepdims=True)
    xc = a2 - mu
    rstd = lax.rsqrt(jnp.mean(xc * xc, axis=-1, keepdims=True) + EPS)
    return xc * rstd, rstd


def _even_fwd(name, seq, a2, ws, u, ln_g, ln_b, w_pool, scale):
    t, c = a2.shape
    ng = len(POOL_WINDOWS)
    grp = c // ng
    tr = _tile(t, 256, SUBLANES)

    def body(a_ref, ws_ref, b_ref, g_ref, bb_ref, wp_ref, sc_ref, z_ref, pm_ref):
        xhat, _ = _ln_stats(a_ref[...].astype(F32))
        l = xhat * g_ref[...] + bb_ref[...]
        z_ref[:, 0:c] = (l * _sigmoid(l)).astype(BF16)
        inv = _counts(pl.program_id(0), tr, seq, grp)
        for g in range(ng):
            gs = slice(g * grp, (g + 1) * grp)
            pm = (ws_ref[:, gs].astype(F32) * inv[g] - b_ref[:, gs].astype(F32)).astype(BF16)
            pm_ref[:, gs] = pm
            q = jnp.dot(pm, wp_ref[g], preferred_element_type=F32)
            z_ref[:, c + g * grp:c + (g + 1) * grp] = (q * sc_ref[:, gs]).astype(BF16)

    row = pl.BlockSpec((tr, c), lambda i: (i, 0))
    vec = pl.BlockSpec((1, c), lambda i: (0, 0))
    return pl.pallas_call(
        body, name=name, grid=(t // tr,),
        in_specs=[row, row, pl.BlockSpec((tr, c), lambda i: (i, 2)), vec, vec,
                  pl.BlockSpec((ng, grp, grp), lambda i: (0, 0, 0)), vec],
        out_specs=[pl.BlockSpec((tr, 2 * c), lambda i: (i, 0)), row],
        out_shape=[jax.ShapeDtypeStruct((t, 2 * c), BF16), jax.ShapeDtypeStruct((t, c), BF16)],
        compiler_params=_cp("parallel"))(a2, ws, u, ln_g, ln_b, w_pool, scale)


def _even_bwd(name, seq, dz, a2, pm, ln_g, ln_b, w_pool, scale):
    t, c = a2.shape
    ng = len(POOL_WINDOWS)
    grp = c // ng
    tr = _tile(t, 256, SUBLANES)

    def body(dz_ref, a_ref, pm_ref, g_ref, bb_ref, wp_ref, sc_ref,
             da_ref, dws_ref, dpm_ref, vec_ref, dwp_ref):
        i = pl.program_id(0)

        @pl.when(i == 0)
        def _():
            vec_ref[...] = jnp.zeros_like(vec_ref)
            dwp_ref[...] = jnp.zeros_like(dwp_ref)

        xhat, rstd = _ln_stats(a_ref[...].astype(F32))
        gv = g_ref[...]
        l = xhat * gv + bb_ref[...]
        sg = _sigmoid(l)
        dl = dz_ref[:, 0:c].astype(F32) * (sg * (1.0 + l * (1.0 - sg)))
        dxh = dl * gv
        da2 = rstd * (dxh - jnp.mean(dxh, axis=-1, keepdims=True)
                      - xhat * jnp.mean(dxh * xhat, axis=-1, keepdims=True))
        da_ref[...] = da2.astype(BF16)
        vec_ref[0:1, :] += jnp.sum(dl * xhat, axis=0, keepdims=True)
        vec_ref[1:2, :] += jnp.sum(dl, axis=0, keepdims=True)
        vec_ref[2:3, :] += jnp.sum(da2, axis=0, keepdims=True)
        inv = _counts(i, tr, seq, grp)
        for g in range(ng):
            gs = slice(g * grp, (g + 1) * grp)
            pmv = pm_ref[:, gs]
            wp = wp_ref[g]
            dp = dz_ref[:, c + g * grp:c + (g + 1) * grp].astype(F32)
            q = jnp.dot(pmv, wp, preferred_element_type=F32)
            vec_ref[3:4, gs] += jnp.sum(dp * q, axis=0, keepdims=True)
            dq = (dp * sc_ref[:, gs]).astype(BF16)
            dpm = lax.dot_general(dq, wp, NT, preferred_element_type=F32)
            dwp_ref[g] += lax.dot_general(pmv, dq, TN, preferred_element_type=F32)
            dpm_ref[:, gs] = dpm.astype(BF16)
            dws_ref[:, gs] = (dpm * inv[g]).astype(BF16)

    row = pl.BlockSpec((tr, c), lambda i: (i, 0))
    vec = pl.BlockSpec((1, c), lambda i: (0, 0))
    rshape = jax.ShapeDtypeStruct((t, c), BF16)
    return pl.pallas_call(
        body, name=name, grid=(t // tr,),
        in_specs=[pl.BlockSpec((tr, 2 * c), lambda i: (i, 0)), row, row, vec, vec,
                  pl.BlockSpec((ng, grp, grp), lambda i: (0, 0, 0)), vec],
        out_specs=[row, row, row, pl.BlockSpec((SUBLANES, c), lambda i: (0, 0)),
                   pl.BlockSpec((ng, grp, grp), lambda i: (0, 0, 0))],
        out_shape=[rshape, rshape, rshape, jax.ShapeDtypeStruct((SUBLANES, c), F32),
                   jax.ShapeDtypeStruct((ng, grp, grp), F32)],
        compiler_params=_cp("arbitrary"))(dz, a2, pm, ln_g, ln_b, w_pool, scale)


def _even_du(name, u, da1, dbp, dpm):
    t, c = da1.shape
    tr = _tile(t, 256, SUBLANES)

    def body(u_ref, da_ref, dbp_ref, dpm_ref, du_ref):
        val = u_ref[:, 0:c].astype(F32)
        sg = _sigmoid(u_ref[:, c:2 * c].astype(F32))
        da = da_ref[...].astype(F32)
        du_ref[:, 0:c] = (da * sg).astype(BF16)
        du_ref[:, c:2 * c] = (da * val * sg * (1.0 - sg)).astype(BF16)
        du_ref[:, 2 * c:3 * c] = (dbp_ref[...].astype(F32) - dpm_ref[...].astype(F32)).astype(BF16)

    row = pl.BlockSpec((tr, c), lambda i: (i, 0))
    wide = pl.BlockSpec((tr, 3 * c), lambda i: (i, 0))
    return pl.pallas_call(
        body, name=name, grid=(t // tr,), in_specs=[wide, row, row, row], out_specs=wide,
        out_shape=jax.ShapeDtypeStruct((t, 3 * c), BF16),
        compiler_params=_cp("parallel"))(u, da1, dbp, dpm)


def _odd_du(name, u, dy, co, dxc):
    t, c = dy.shape
    tr = _tile(t, 256, SUBLANES)

    def body(u_ref, dy_ref, co_ref, dx_ref, du_ref):
        dx = dx_ref[...].astype(F32)
        du_ref[:, 0:c] = (dy_ref[...].astype(F32) * co_ref[...].astype(F32)).astype(BF16)
        du_ref[:, c:2 * c] = (dx * u_ref[:, 2 * c:3 * c].astype(F32)).astype(BF16)
        du_ref[:, 2 * c:3 * c] = (dx * u_ref[:, c:2 * c].astype(F32)).astype(BF16)

    row = pl.BlockSpec((tr, c), lambda i: (i, 0))
    wide = pl.BlockSpec((tr, 3 * c), lambda i: (i, 0))
    return pl.pallas_call(
        body, name=name, grid=(t // tr,), in_specs=[wide, row, row, row], out_specs=wide,
        out_shape=jax.ShapeDtypeStruct((t, 3 * c), BF16),
        compiler_params=_cp("parallel"))(u, dy, co, dxc)


def _local_step(x, tgt, seq, small, get_w, put_g, sync):
    t, d = x.shape
    c = d // 2
    cw_e, cw_o = small["conv_w_e"], small["conv_w_o"]
    wp = small["w_pool_e"].astype(BF16)
    ptaps = _pool_taps(c)
    row = lambda v: v.reshape(1, -1)

    we = {"w_in": get_w("in_e", x)[0]}
    n0 = _rms_fwd("rms_fwd_mix0", x, row(small["mix_norm_e"]))
    u0 = _mm_nn("mm_in_e", n0, we["w_in"], BF16)
    sync("fwd_a", u0)
    a2 = _conv_fwd("conv_e_fwd", seq, c, cw_e, u0, 0, u0, c, "glu", bias=row(small["conv_b_e"]))
    ws = _conv_fwd("pool_fwd", seq, c, ptaps, u0, 2 * c, live=_pool_live(c))
    z0, pm = _even_fwd("even_fwd", seq, a2, ws, u0, row(small["ln_g_e"]), row(small["ln_b_e"]),
                       wp, row(small["pool_scale_e"]))
    we["w_out"] = get_w("out_e", z0)[0]
    h1 = _mm_nn("mm_out_e", z0, we["w_out"], F32, res=x)
    sync("fwd_b", h1)
    n1 = _rms_fwd("rms_fwd_ffn0", h1, row(small["ffn_norm"][0]))
    wf0 = dict(zip(("w_gate", "w_up"), get_w("gu0", n1)))
    act0, ds0, s0 = _ffn_fwd("ffn0_fwd", n1, wf0["w_gate"], wf0["w_up"])
    dep = sync("fwd_c", act0)
    wf0["w_down"] = get_w("down0", act0)[0]
    h2 = _mm_nn("mm_down0", act0, wf0["w_down"], F32, res=h1, dep=dep)
    dep = sync("fwd_d", h2)
    n2 = _rms_fwd("rms_fwd_mix1", h2, row(small["mix_norm_o"]))
    wo = {"w_in": get_w("in_o", n2)[0]}
    u1 = _mm_nn("mm_in_o", n2, wo["w_in"], BF16, dep=dep)
    co, y1 = _conv_fwd("conv_o_fwd", seq, d, cw_o, u1, d, u1, 2 * d, "mul", post=u1, cpost=0)
    dep = sync("fwd_e", y1)
    wo["w_out"] = get_w("out_o", y1)[0]
    h3 = _mm_nn("mm_out_o", y1, wo["w_out"], F32, res=h2, dep=dep)
    sync("fwd_f", h3)
    n3 = _rms_fwd("rms_fwd_ffn1", h3, row(small["ffn_norm"][1]))
    wf1 = dict(zip(("w_gate", "w_up"), get_w("gu1", n3)))
    act1, ds1, s1 = _ffn_fwd("ffn1_fwd", n3, wf1["w_gate"], wf1["w_up"])
    wf1["w_down"] = get_w("down1", act1)[0]
    h4 = _mm_nn("mm_down1", act1, wf1["w_down"], F32, res=h3)

    dh4, dh4b, d_final, lsum = _loss_head("loss_head", h4, row(small["final_norm"]), tgt)

    def ffn_bwd(tag, dh, dhb, h_in, gain, n, dsilu, silu, act, w, dep):
        dg, dup = _ffn_bwd_act("ffn%s_bwd_act" % tag, dhb, w["w_down"], dsilu, silu, dep=dep)
        dwd = _mm_tn("mm_dwd%s" % tag, act, dhb, BF16)
        dwg = _mm_tn("mm_dwg%s" % tag, dg, n, BF16, dep=sync("bwd_ffn" + tag, dwd))
        dwu = _mm_tn("mm_dwu%s" % tag, dup, n, BF16)
        dn = _mm_nn("mm_ffn_dn%s" % tag, [dg, dup], [w["w_gate"], w["w_up"]], BF16)
        dh_in, dhb_in, dgain = _rms_bwd("rms_bwd_ffn%s" % tag, h_in, gain, dn, dh)
        dep = put_g("ffn" + tag, {"w_gate": dwg, "w_up": dwu, "w_down": dwd})
        return dh_in, dhb_in, dgain, dep

    dh3, dh3b, d_ffn1, dep = ffn_bwd("1", dh4, dh4b, h3, row(small["ffn_norm"][1]), n3, ds1, s1,
                                     act1, wf1, None)
    dw_out_o = _mm_tn("mm_dw_out_o", y1, dh3b, BF16, dep=dep)
    dy1 = _mm_nt("mm_dy_o", dh3b, wo["w_out"], BF16, dep=sync("bwd_mix_o", dw_out_o))
    dxc, dcw_o = _conv_bwd("conv_o_bwd", seq, d, cw_o, dy1, 0, u1, 0, "mul",
                           x1=u1, c1=d, x2=u1, c2=2 * d, pre="mul")
    du1 = _odd_du("odd_du", u1, dy1, co, dxc)
    dw_in_o = _mm_tn("mm_dw_in_o", n2, du1, BF16)
    dn2 = _mm_nt("mm_dn_o", du1, wo["w_in"], BF16)
    dh2, dh2b, d_mix_o = _rms_bwd("rms_bwd_mix1", h2, row(small["mix_norm_o"]), dn2, dh3)
    dep = put_g("mix_o", {"w_in": dw_in_o, "w_out": dw_out_o})

    dh1, dh1b, d_ffn0, dep = ffn_bwd("0", dh2, dh2b, h1, row(small["ffn_norm"][0]), n1, ds0, s0,
                                     act0, wf0, dep)
    dw_out_e = _mm_tn("mm_dw_out_e", z0, dh1b, BF16, dep=dep)
    dz0 = _mm_nt("mm_dz_e", dh1b, we["w_out"], BF16, dep=sync("bwd_mix_e", dw_out_e))
    da2, dws, dpm, vecs, dwp = _even_bwd("even_bwd", seq, dz0, a2, pm, row(small["ln_g_e"]),
                                         row(small["ln_b_e"]), wp, row(small["pool_scale_e"]))
    da1, dcw_e = _conv_bwd("conv_e_bwd", seq, c, cw_e, da2, 0, x1=u0, c1=0, x2=u0, c2=c, pre="glu")
    dbp = _conv_bwd("pool_bwd", seq, c, ptaps, dws, 0, live=_pool_live(c))
    du0 = _even_du("even_du", u0, da1, dbp, dpm)
    dep = put_g("small", {"conv_w_e": dcw_e, "conv_b_e": vecs[2], "ln_g_e": vecs[0], "ln_b_e": vecs[1],
                          "w_pool_e": dwp, "pool_scale_e": vecs[3], "mix_norm_o": d_mix_o[0],
                          "conv_w_o": dcw_o, "ffn_norm": jnp.concatenate([d_ffn0, d_ffn1], axis=0),
                          "final_norm": d_final[0]})
    dw_in_e = _mm_tn("mm_dw_in_e", n0, du0, BF16, dep=dep)
    dep = put_g("mix_e", {"w_in": dw_in_e, "w_out": dw_out_e})
    dn0 = _mm_nt("mm_dn_e", du0, we["w_in"], BF16, dep=dep)
    dx, _, d_mix_e = _rms_bwd("rms_bwd_mix0", x, row(small["mix_norm_e"]), dn0, dh1)
    return lsum, dx, d_mix_e[0]


def _place():
    x, y, c = (lax.axis_index(a) for a in MESH_AXES)
    return x, y, c


def _index(p):
    return 4 * p[0] + 2 * p[1] + p[2]


def _slab(ref, kind, d, n):
    if kind == "blk":
        return ref.at[d]
    return ref.at[:, pl.ds(pl.multiple_of(d * n, LANES), n)]


HBM = pl.BlockSpec(memory_space=pltpu.HBM)
SEM = pl.BlockSpec(memory_space=pltpu.SEMAPHORE)
EFFECT = pltpu.SideEffectType.DATAFLOW_SIDE_EFFECTING
NCHIPS = 4


def _in_hbm(a):
    return pltpu.with_memory_space_constraint(a, pltpu.HBM)


def _gathered_shape(s, kind):
    m, n = s.shape
    return (NDEV, m, n) if kind == "blk" else (m, NDEV * n)


def _first_targets():
    x, y, c = _place()
    return [(x, y, 1 - c), (1 - x, y, c), (x, 1 - y, c), (1 - x, 1 - y, c)]


def _gather_start(name, shards, kinds, after):
    na = len(shards)

    def body(*refs):
        x_refs, land_refs = refs[:na], refs[na:2 * na]
        send_sems, recv_sems = refs[2 * na + 1], refs[2 * na + 2]
        token = refs[-1]
        me = _index(_place())
        for a in range(na):
            for k, to in enumerate(_first_targets()):
                pltpu.make_async_remote_copy(
                    src_ref=x_refs[a], dst_ref=_slab(land_refs[a], kinds[a], me, shards[a].shape[1]),
                    send_sem=send_sems.at[4 * a + k], recv_sem=recv_sems.at[4 * a + k],
                    device_id=to, device_id_type=MESH).start()
        token[...] = jnp.zeros_like(token)

    lands = [lax.empty(_gathered_shape(s, k), s.dtype) for s, k in zip(shards, kinds)]
    outs = pl.pallas_call(
        body, name=name,
        out_shape=(pltpu.SemaphoreType.DMA((4 * na,)), pltpu.SemaphoreType.DMA((4 * na,)),
                   *[pltpu.HBM(s.shape, s.dtype) for s in shards],
                   *[pltpu.HBM(l.shape, l.dtype) for l in lands],
                   jax.ShapeDtypeStruct((SUBLANES, LANES), F32)),
        in_specs=[HBM] * (2 * na) + [ANY],
        out_specs=(SEM, SEM, *[HBM] * (2 * na), pl.BlockSpec(memory_space=pltpu.VMEM)),
        input_output_aliases={i: 2 + i for i in range(2 * na)},
        compiler_params=pltpu.CompilerParams(has_side_effects=EFFECT),
    )(*[_in_hbm(s) for s in shards], *[_in_hbm(l) for l in lands], after)
    return outs[0], outs[1], outs[2:2 + na], outs[2 + na:2 + 2 * na], outs[-1]


def _gather_wait(name, started, kinds, after):
    send_sems, recv_sems, shards, lands, _ = started
    na = len(shards)

    def body(*refs):
        x_refs, land_refs = refs[:na], refs[na:2 * na]
        s_sems, r_sems = refs[2 * na], refs[2 * na + 1]
        for a in range(na):
            for k, frm in enumerate(_first_targets()):
                cp = pltpu.make_async_remote_copy(
                    src_ref=x_refs[a],
                    dst_ref=_slab(land_refs[a], kinds[a], _index(frm), shards[a].shape[1]),
                    send_sem=s_sems.at[4 * a + k], recv_sem=r_sems.at[4 * a + k],
                    device_id=frm, device_id_type=MESH)
                cp.wait_send()
                cp.wait_recv()

    outs = pl.pallas_call(
        body, name=name,
        out_shape=(*[pltpu.HBM(s.shape, s.dtype) for s in shards],
                   *[pltpu.HBM(l.shape, l.dtype) for l in lands]),
        in_specs=[HBM] * (2 * na) + [SEM, SEM, ANY], out_specs=[HBM] * (2 * na),
        input_output_aliases={i: i for i in range(2 * na)},
        compiler_params=pltpu.CompilerParams(has_side_effects=EFFECT),
    )(*shards, *lands, send_sems, recv_sems, after)
    return outs[:na], outs[na:]


def _split_start(name, bufs, ncopies, plan, after):
    nb = len(bufs)

    def body(*refs):
        send_sems, recv_sems, token = refs[nb + 1], refs[nb + 2], refs[-1]
        for k, (src, dst, to, _) in enumerate(plan(refs[:nb])):
            pltpu.make_async_remote_copy(src_ref=src, dst_ref=dst, send_sem=send_sems.at[k],
                                         recv_sem=recv_sems.at[k], device_id=to, device_id_type=MESH).start()
        token[...] = jnp.zeros_like(token)

    outs = pl.pallas_call(
        body, name=name,
        out_shape=(pltpu.SemaphoreType.DMA((ncopies,)), pltpu.SemaphoreType.DMA((ncopies,)),
                   *[pltpu.HBM(b.shape, b.dtype) for b in bufs],
                   jax.ShapeDtypeStruct((SUBLANES, LANES), F32)),
        in_specs=[HBM] * nb + [ANY],
        out_specs=(SEM, SEM, *[HBM] * nb, pl.BlockSpec(memory_space=pltpu.VMEM)),
        input_output_aliases={i: 2 + i for i in range(nb)},
        compiler_params=pltpu.CompilerParams(has_side_effects=EFFECT),
    )(*[_in_hbm(b) for b in bufs], after)
    return outs[0], outs[1], list(outs[2:2 + nb]), outs[-1]


def _split_wait(name, started, plan, after):
    send_sems, recv_sems, bufs, _ = started
    nb = len(bufs)

    def body(*refs):
        s_sems, r_sems = refs[nb], refs[nb + 1]
        for k, (src, _, to, landed) in enumerate(plan(refs[:nb])):
            cp = pltpu.make_async_remote_copy(src_ref=src, dst_ref=landed, send_sem=s_sems.at[k],
                                              recv_sem=r_sems.at[k], device_id=to, device_id_type=MESH)
            cp.wait_send()
            cp.wait_recv()

    outs = pl.pallas_call(
        body, name=name, out_shape=tuple(pltpu.HBM(b.shape, b.dtype) for b in bufs),
        in_specs=[HBM] * nb + [SEM, SEM, ANY], out_specs=[HBM] * nb,
        input_output_aliases={i: i for i in range(nb)},
        compiler_params=pltpu.CompilerParams(has_side_effects=EFFECT),
    )(*bufs, send_sems, recv_sems, after)
    return list(outs)


def _forward_plan(kinds, nloc):
    def plan(lands):
        x, y, c = _place()
        out = []
        for a, land in enumerate(lands):
            for chip in [(1 - x, y), (x, 1 - y), (1 - x, 1 - y)]:
                mine = _slab(land, kinds[a], _index((*chip, c)), nloc[a])
                out.append((mine, mine, (x, y, 1 - c), _slab(land, kinds[a], _index((*chip, 1 - c)), nloc[a])))
        return out
    return plan


def _own_copy(name, shard, land, kind, me):
    m, n = shard.shape
    tr = _tile(m, max(SUBLANES, 1048576 // n), SUBLANES)

    def body(s_ref, x_ref, land_ref, o_ref):
        o_ref[...] = x_ref[...]

    if kind == "blk":
        o_spec = pl.BlockSpec((None, tr, n), lambda i, s: (s[0], i, 0))
    else:
        o_spec = pl.BlockSpec((tr, n), lambda i, s: (i, s[0]))
    return pl.pallas_call(
        body, name=name,
        grid_spec=pltpu.PrefetchScalarGridSpec(
            num_scalar_prefetch=1, grid=(m // tr,),
            in_specs=[pl.BlockSpec((tr, n), lambda i, s: (i, 0)), ANY], out_specs=o_spec),
        out_shape=jax.ShapeDtypeStruct(land.shape, land.dtype),
        input_output_aliases={2: 0}, compiler_params=_cp("parallel"))(me, shard, land)


def _everyone_plan(refs):
    x, y, c = _place()
    out = []
    for dx, dy, dc in [(a, b, e) for a in (0, 1) for b in (0, 1) for e in (0, 1)][1:]:
        peer = (x ^ dx, y ^ dy, c ^ dc)
        out.append((refs[0], refs[1].at[_index((x, y, c))], peer, refs[1].at[_index(peer)]))
    return out


def _pair_plan(kinds, nloc):
    na = len(kinds)

    def plan(refs):
        x, y, c = _place()
        out = []
        for a in range(na):
            for j in range(NCHIPS):
                dst = refs[na + a].at[j]
                out.append((_slab(refs[a], kinds[a], 2 * j + (1 - c), nloc[a]), dst, (x, y, 1 - c), dst))
        return out
    return plan


def _chip_sum(name, full, kind, n, from_sib, place):
    _, m, _ = from_sib.shape
    tr = _tile(m, max(SUBLANES, 1048576 // n), SUBLANES)

    def body(s_ref, mine_ref, sib_ref, csum_ref, land_ref):
        v = (mine_ref[...].astype(F32) + sib_ref[...].astype(F32)).astype(csum_ref.dtype)
        csum_ref[...] = v

        @pl.when(pl.program_id(1) == s_ref[1])
        def _():
            land_ref[...] = v

    if kind == "blk":
        mine_spec = pl.BlockSpec((None, tr, n), lambda i, j, s: (2 * j + s[0], i, 0))
    else:
        mine_spec = pl.BlockSpec((tr, n), lambda i, j, s: (i, 2 * j + s[0]))
    slot = pl.BlockSpec((None, tr, n), lambda i, j, s: (j, i, 0))
    shp = jax.ShapeDtypeStruct((NCHIPS, m, n), from_sib.dtype)
    return pl.pallas_call(
        body, name=name,
        grid_spec=pltpu.PrefetchScalarGridSpec(
            num_scalar_prefetch=1, grid=(m // tr, NCHIPS), in_specs=[mine_spec, slot],
            out_specs=[slot, pl.BlockSpec((None, tr, n), lambda i, j, s: (s[1], i, 0))]),
        out_shape=[shp, shp], compiler_params=_cp("parallel", "arbitrary"))(place, full, from_sib)


def _other_chips():
    x, y, c = _place()
    return [(1 - x, y, c), (x, 1 - y, c), (1 - x, 1 - y, c)]


def _scatter_start(name, csums, lands, after):
    na = len(csums)

    def body(*refs):
        c_refs, land_refs = refs[:na], refs[na:2 * na]
        send_sems, recv_sems = refs[2 * na + 1], refs[2 * na + 2]
        token = refs[-1]
        x, y, _ = _place()
        for a in range(na):
            for k, to in enumerate(_other_chips()):
                pltpu.make_async_remote_copy(
                    src_ref=c_refs[a].at[2 * to[0] + to[1]], dst_ref=land_refs[a].at[2 * x + y],
                    send_sem=send_sems.at[3 * a + k], recv_sem=recv_sems.at[3 * a + k],
                    device_id=to, device_id_type=MESH).start()
        token[...] = jnp.zeros_like(token)

    outs = pl.pallas_call(
        body, name=name,
        out_shape=(pltpu.SemaphoreType.DMA((3 * na,)), pltpu.SemaphoreType.DMA((3 * na,)),
                   *[pltpu.HBM(s.shape, s.dtype) for s in csums],
                   *[pltpu.HBM(l.shape, l.dtype) for l in lands],
                   jax.ShapeDtypeStruct((SUBLANES, LANES), F32)),
        in_specs=[HBM] * (2 * na) + [ANY],
        out_specs=(SEM, SEM, *[HBM] * (2 * na), pl.BlockSpec(memory_space=pltpu.VMEM)),
        input_output_aliases={i: 2 + i for i in range(2 * na)},
        compiler_params=pltpu.CompilerParams(has_side_effects=EFFECT),
    )(*[_in_hbm(s) for s in csums], *[_in_hbm(l) for l in lands], after)
    return outs[0], outs[1], outs[2:2 + na], outs[2 + na:2 + 2 * na], outs[-1]


def _scatter_wait(name, started, after):
    send_sems, recv_sems, csums, lands, _ = started
    na = len(csums)

    def body(*refs):
        c_refs, land_refs = refs[:na], refs[na:2 * na]
        s_sems, r_sems = refs[2 * na], refs[2 * na + 1]
        for a in range(na):
            for k, frm in enumerate(_other_chips()):
                cp = pltpu.make_async_remote_copy(
                    src_ref=c_refs[a].at[2 * frm[0] + frm[1]], dst_ref=land_refs[a].at[2 * frm[0] + frm[1]],
                    send_sem=s_sems.at[3 * a + k], recv_sem=r_sems.at[3 * a + k],
                    device_id=frm, device_id_type=MESH)
                cp.wait_send()
                cp.wait_recv()

    outs = pl.pallas_call(
        body, name=name,
        out_shape=(*[pltpu.HBM(s.shape, s.dtype) for s in csums],
                   *[pltpu.HBM(l.shape, l.dtype) for l in lands]),
        in_specs=[HBM] * (2 * na) + [SEM, SEM, ANY], out_specs=[HBM] * (2 * na),
        input_output_aliases={i: i for i in range(2 * na)},
        compiler_params=pltpu.CompilerParams(has_side_effects=EFFECT),
    )(*csums, *lands, send_sems, recv_sems, after)
    return outs[na:]


def _adam_math(w, g, m, v):
    m = ADAM_B1 * m + (1.0 - ADAM_B1) * g
    v = ADAM_B2 * v + (1.0 - ADAM_B2) * (g * g)
    m_hat = m / (1.0 - ADAM_B1 ** ADAM_STEP)
    v_hat = v / (1.0 - ADAM_B2 ** ADAM_STEP)
    delta = -ADAM_LR * (m_hat / (jnp.sqrt(v_hat) + ADAM_EPS) + ADAM_WD * w)
    return delta, m, v


def _sum_adamw(name, parts, w, m, v, layer, prev=None, dep=None):
    nl, r, c = w.shape
    nparts = parts.shape[0]
    tr = _tile(r, max(SUBLANES, 360448 // c), SUBLANES)

    def body(p_ref, w_ref, m_ref, v_ref, *rest):
        g_ref, d_ref, mo_ref, vo_ref = rest[-4:]
        g = p_ref[0].astype(F32)
        for s in range(1, nparts):
            g = g + p_ref[s].astype(F32)
        delta, mn, vn = _adam_math(w_ref[...], g, m_ref[...], v_ref[...])
        g_ref[...] = g
        d_ref[...] = delta
        mo_ref[...] = mn
        vo_ref[...] = vn

    row = pl.BlockSpec((None, tr, c), lambda i: (layer, i, 0))
    shp = jax.ShapeDtypeStruct((nl, r, c), F32)
    extra = ([] if prev is None else list(prev)) + ([] if dep is None else [dep])
    return pl.pallas_call(
        body, name=name, grid=(r // tr,),
        in_specs=[pl.BlockSpec((nparts, tr, c), lambda i: (0, i, 0)), row, row, row] + [ANY] * len(extra),
        out_specs=[row, row, row, row], out_shape=[shp, shp, shp, shp],
        input_output_aliases={} if prev is None else {4 + i: i for i in range(4)},
        compiler_params=_cp("parallel"))(parts, w, m, v, *extra)


def _sum_parts(name, parts):
    _, r, c = parts.shape

    def body(p_ref, o_ref):
        g = p_ref[0]
        for s in range(1, NDEV):
            g = g + p_ref[s]
        o_ref[...] = g

    return pl.pallas_call(
        body, name=name, grid=(1,),
        in_specs=[pl.BlockSpec((NDEV, r, c), lambda i: (0, 0, 0))],
        out_specs=pl.BlockSpec((r, c), lambda i: (0, 0)),
        out_shape=jax.ShapeDtypeStruct((r, c), F32), compiler_params=_cp("arbitrary"))(parts)


def _adamw(name, w, g, m, v):
    r, c = w.shape

    def body(w_ref, g_ref, m_ref, v_ref, d_ref, mo_ref, vo_ref):
        delta, mn, vn = _adam_math(w_ref[...], g_ref[...], m_ref[...], v_ref[...])
        d_ref[...] = delta
        mo_ref[...] = mn
        vo_ref[...] = vn

    full = pl.BlockSpec((r, c), lambda i: (0, 0))
    shp = jax.ShapeDtypeStruct((r, c), F32)
    return pl.pallas_call(
        body, name=name, grid=(1,), in_specs=[full] * 4, out_specs=[full] * 3,
        out_shape=[shp] * 3, compiler_params=_cp("arbitrary"))(w, g, m, v)


def _pack(arrays):
    flat = jnp.concatenate([a.reshape(-1) for a in arrays])
    unit = SUBLANES * LANES
    pad = (-flat.shape[0]) % unit
    return jnp.pad(flat, (0, pad)).reshape(-1, LANES)


def _unpack(buf, shapes):
    flat = buf.reshape(-1)
    out, off = [], 0
    for shp in shapes:
        size = 1
        for s in shp:
            size *= s
        out.append(flat[off:off + size].reshape(shp))
        off += size
    return out


WEIGHTS = ["mix_norm_e", "w_in_e", "conv_w_e", "conv_b_e", "ln_g_e", "ln_b_e", "w_pool_e",
           "pool_scale_e", "w_out_e", "mix_norm_o", "w_in_o", "conv_w_o", "w_out_o", "ffn_norm",
           "w_gate", "w_up", "w_down", "final_norm"]
BIG = ["w_in_e", "w_out_e", "w_in_o", "w_out_o", "w_gate", "w_up", "w_down"]
SHARDED_SMALL = {"conv_w_e": 1, "w_pool_e": 1, "mix_norm_o": 0, "conv_w_o": 1}
SMALL = [n for n in WEIGHTS if n not in BIG]


def kernel(x, mix_norm_e, w_in_e, conv_w_e, conv_b_e, ln_g_e, ln_b_e, w_pool_e, pool_scale_e, w_out_e, mix_norm_o, w_in_o, conv_w_o, w_out_o, ffn_norm, w_gate, w_up, w_down, final_norm, loss_target, m_mix_norm_e, m_w_in_e, m_conv_w_e, m_conv_b_e, m_ln_g_e, m_ln_b_e, m_w_pool_e, m_pool_scale_e, m_w_out_e, m_mix_norm_o, m_w_in_o, m_conv_w_o, m_w_out_o, m_ffn_norm, m_w_gate, m_w_up, m_w_down, m_final_norm, v_mix_norm_e, v_w_in_e, v_conv_w_e, v_conv_b_e, v_ln_g_e, v_ln_b_e, v_w_pool_e, v_pool_scale_e, v_w_out_e, v_mix_norm_o, v_w_in_o, v_conv_w_o, v_w_out_o, v_ffn_norm, v_w_gate, v_w_up, v_w_down, v_final_norm):
    wts = dict(zip(WEIGHTS, [mix_norm_e, w_in_e, conv_w_e, conv_b_e, ln_g_e, ln_b_e, w_pool_e, pool_scale_e, w_out_e, mix_norm_o, w_in_o, conv_w_o, w_out_o, ffn_norm, w_gate, w_up, w_down, final_norm]))
    mom = dict(zip(WEIGHTS, [m_mix_norm_e, m_w_in_e, m_conv_w_e, m_conv_b_e, m_ln_g_e, m_ln_b_e, m_w_pool_e, m_pool_scale_e, m_w_out_e, m_mix_norm_o, m_w_in_o, m_conv_w_o, m_w_out_o, m_ffn_norm, m_w_gate, m_w_up, m_w_down, m_final_norm]))
    var = dict(zip(WEIGHTS, [v_mix_norm_e, v_w_in_e, v_conv_w_e, v_conv_b_e, v_ln_g_e, v_ln_b_e, v_w_pool_e, v_pool_scale_e, v_w_out_e, v_mix_norm_o, v_w_in_o, v_conv_w_o, v_w_out_o, v_ffn_norm, v_w_gate, v_w_up, v_w_down, v_final_norm]))
    bsz, seq, d = x.shape
    t = bsz * seq
    me = _index(_place())
    me_arr = jnp.reshape(me, (1,)).astype(jnp.int32)

    sh_names = list(SHARDED_SMALL)
    sh_local = [wts[n][0] for n in sh_names]
    packed = _pack(sh_local)
    params_st = _split_start("small_params_start", [packed, lax.empty((NDEV,) + packed.shape, F32)], NDEV - 1,
                             _everyone_plan, x)

    for state in (wts, mom, var):
        for n in ("w_gate", "w_up"):
            state[n] = jnp.swapaxes(state[n], 1, 2)
    bf = lambda a: a.astype(BF16)
    mix_kinds, ffn_kinds = ["col", "blk"], ["blk", "blk", "blk"]
    ffn_names = ("w_gate", "w_up", "w_down")
    groups = {
        "mix_e": ([w_in_e.shape[2], d], mix_kinds, [("w_in_e", 0), ("w_out_e", 0)]),
        "ffn0": ([d, d, d], ffn_kinds, [(n, 0) for n in ffn_names]),
        "mix_o": ([w_in_o.shape[2], d], mix_kinds, [("w_in_o", 0), ("w_out_o", 0)]),
        "ffn1": ([d, d, d], ffn_kinds, [(n, 1) for n in ffn_names]),
    }
    gathers = {
        "in_e": ([bf(w_in_e[0])], ["col"]), "out_e": ([bf(w_out_e[0])], ["blk"]),
        "gu0": ([bf(wts["w_gate"][0]), bf(wts["w_up"][0])], ["blk", "blk"]), "down0": ([bf(w_down[0])], ["blk"]),
        "in_o": ([bf(w_in_o[0])], ["col"]), "out_o": ([bf(w_out_o[0])], ["blk"]),
        "gu1": ([bf(wts["w_gate"][1]), bf(wts["w_up"][1])], ["blk", "blk"]), "down1": ([bf(w_down[1])], ["blk"]),
    }
    started, prev = {}, params_st[3]
    for grp, (shards, kinds) in gathers.items():
        started[grp] = _gather_start("gather_start_" + grp, shards, kinds, prev)
        prev = started[grp][4]
    all_started = prev[0, 0:1]

    bufs = _split_wait("small_params_wait", params_st, _everyone_plan, prev)
    gathered = _own_copy("small_params_own", bufs[0], bufs[1], "blk", me_arr)
    small = {n: wts[n][0] for n in SMALL if n not in SHARDED_SMALL and n not in ("ffn_norm", "final_norm")}
    small["ffn_norm"], small["final_norm"] = ffn_norm, final_norm
    per_dev = [_unpack(gathered[s], [a.shape for a in sh_local]) for s in range(NDEV)]
    for i, n in enumerate(sh_names):
        small[n] = jnp.concatenate([per_dev[s][i] for s in range(NDEV)], axis=SHARDED_SMALL[n])

    passing, shards_of = {}, {}

    def pass_on(grp, after):
        shards, kinds = gathers[grp]
        shards_of[grp], lands = _gather_wait("gather_wait_" + grp, started[grp], kinds, after)
        plan = _forward_plan(kinds, [s.shape[1] for s in shards])
        passing[grp] = (_split_start("forward_start_" + grp, lands, 3 * len(lands), plan, after), plan)
        return passing[grp][0][3]

    def get_w(grp, after):
        if grp not in passing:
            after = pass_on(grp, after)
        st, plan = passing[grp]
        lands = _split_wait("forward_wait_" + grp, st, plan, after)
        full = [_own_copy("own_copy_%s%d" % (grp, a), shards_of[grp][a], lands[a], gathers[grp][1][a], me_arr)
                for a in range(len(lands))]
        return [f.reshape(-1, d) if kind == "blk" else f for f, kind in zip(full, gathers[grp][1])]

    cx, cy, cc = _place()
    place = jnp.stack([cc, 2 * cx + cy]).astype(jnp.int32)
    bwd_order = ["ffn1", "mix_o", "ffn0", "mix_e"]
    pairing, pending, results = {}, {}, {}

    early_names = [n for n in SMALL if n != "mix_norm_e"]
    small_sent = {}

    def send_small(tag, arrays, after):
        mine = _pack(arrays)
        small_sent[tag] = _split_start(tag + "_start", [mine, lax.empty((NDEV,) + mine.shape, F32)], NDEV - 1,
                                       _everyone_plan, after)
        return small_sent[tag][3]

    def summed_small(tag, shapes, after):
        bufs = _split_wait(tag + "_wait", small_sent[tag], _everyone_plan, after)
        parts = _own_copy(tag + "_own", bufs[0], bufs[1], "blk", me_arr)
        return _unpack(_sum_parts(tag + "_sum", parts), shapes)

    def put_g(grp, grads):
        if grp == "small":
            small_sent["shapes"] = [grads[n].shape for n in early_names]
            return send_small("small_grads", [grads[n] for n in early_names], place)
        nloc, kinds, _ = groups[grp]
        if len(kinds) == 2:
            fulls = [grads["w_in"], grads["w_out"].reshape(NDEV, -1, d)]
        else:
            fulls = [grads[n].reshape(NDEV, -1, d) for n in ffn_names]
        empties = []
        for g, kind, n in zip(fulls, kinds, nloc):
            empties.append(lax.empty((NCHIPS, g.shape[1] if kind == "blk" else g.shape[0], n), g.dtype))
        plan = _pair_plan(kinds, nloc)
        pairing[grp] = (_split_start("pair_start_" + grp, fulls + empties, NCHIPS * len(fulls), plan, place),
                        plan, kinds, nloc)
        token = pairing[grp][0][3]
        return send_sums(grp, token) if grp == bwd_order[-1] else token

    def send_sums(grp, after):
        st, plan, kinds, nloc = pairing[grp]
        bufs = _split_wait("pair_wait_" + grp, st, plan, after)
        na = len(kinds)
        sums = [_chip_sum("chip_sum_%s%d" % (grp, a), bufs[a], kinds[a], nloc[a], bufs[na + a], place)
                for a in range(na)]
        pending[grp] = _scatter_start("scatter_start_" + grp, [s[0] for s in sums], [s[1] for s in sums], after)
        return pending[grp][4]

    def finish(grp, after):
        lands = _scatter_wait("scatter_wait_" + grp, pending[grp], after)
        dep = None
        for (n, l), parts in zip(groups[grp][2], lands):
            results[n] = _sum_adamw("adamw_%s%d" % (n, l), parts, wts[n], mom[n], var[n], l, results.get(n), dep)
            dep = results[n][1]
        return dep

    fwd_sync = {"fwd_a": ["out_e"], "fwd_b": ["gu0"], "fwd_c": ["down0", "in_o"], "fwd_d": ["out_o"],
                "fwd_e": ["gu1"], "fwd_f": ["down1"]}

    def sync(tag, after):
        if tag in fwd_sync:
            for grp in fwd_sync[tag]:
                after = pass_on(grp, after)
            return after
        if tag == "bwd_mix_o":
            return send_sums("ffn1", after)
        if tag == "bwd_ffn0":
            return finish("ffn1", send_sums("mix_o", after))
        if tag == "bwd_mix_e":
            return finish("mix_o", send_sums("ffn0", after))
        return None

    small["mix_norm_e"] = small["mix_norm_e"] + all_started
    lsum, dx, d_mix_e = _local_step(x.reshape(t, d), loss_target.reshape(t, d), seq, small, get_w, put_g, sync)
    loss = lax.psum(jnp.sum(lsum), MESH_AXES)

    out_g, out_d, out_m, out_v = {}, {}, {}, {}

    dep = finish("ffn0", send_small("last_grad", [d_mix_e], dx))
    sums = dict(zip(early_names, summed_small("small_grads", small_sent["shapes"], dep)))
    sums["mix_norm_e"], = summed_small("last_grad", [d_mix_e.shape], dep)
    gs_sum = [sums[n] for n in SMALL]
    local_g = []
    for n, g in zip(SMALL, gs_sum):
        if n in SHARDED_SMALL:
            ax = SHARDED_SMALL[n]
            size = wts[n].shape[ax + 1]
            g = lax.dynamic_slice_in_dim(g, me * size, size, axis=ax)
        local_g.append(g.reshape(wts[n].shape))
    shapes = [wts[n].shape for n in SMALL]
    upd = _adamw("adamw_small", _pack([wts[n] for n in SMALL]), _pack(local_g),
                 _pack([mom[n] for n in SMALL]), _pack([var[n] for n in SMALL]))
    for i, outd in enumerate((out_d, out_m, out_v)):
        for n, a in zip(SMALL, _unpack(upd[i], shapes)):
            outd[n] = a
    for n, g in zip(SMALL, local_g):
        out_g[n] = g

    finish("mix_e", upd[0])
    for n in BIG:
        res = [jnp.swapaxes(a, 1, 2) for a in results[n]] if n in ("w_gate", "w_up") else results[n]
        out_g[n], out_d[n], out_m[n], out_v[n] = res

    return (loss, dx.reshape(bsz, seq, d), *[out_g[n] for n in WEIGHTS], *[out_d[n] for n in WEIGHTS],
            *[out_m[n] for n in WEIGHTS], *[out_v[n] for n in WEIGHTS])
```

```python
import jax
import jax.numpy as jnp
from jax import lax
from jax.experimental import pallas as pl
from jax.experimental.pallas import tpu as pltpu

F32 = jnp.float32
BF16 = jnp.bfloat16
NDEV = 8
MESH_AXES = ("x", "y", "c")
EPS = 1e-6
POOL_WINDOWS = (2, 4, 8, 16)
CONV_WIDTH = 31
SHORT_WIDTH = 3
ADAM_LR = 0.001
ADAM_B1 = 0.9
ADAM_B2 = 0.999
ADAM_EPS = 1e-08
ADAM_WD = 0.01
ADAM_STEP = 10
LANES = 128
SUBLANES = 8
VMEM_LIMIT = 56 * 1024 * 1024
MXU_DEPTH = 256
MM_TK = 2816
MESH = pl.DeviceIdType.MESH
ANY = pl.BlockSpec(memory_space=pl.ANY)


def _cp(*sem):
    return pltpu.CompilerParams(dimension_semantics=sem, vmem_limit_bytes=VMEM_LIMIT)


def _tile(n, pref, unit=LANES):
    if n <= pref:
        return n
    t = (pref // unit) * unit
    while t > unit and n % t:
        t -= unit
    assert n % t == 0, (n, pref)
    return t


def _sigmoid(v):
    return 0.5 * jnp.tanh(0.5 * v) + 0.5


def _mm(name, pairs, a_specs, b_specs, dims, out_shape, o_spec, grid, acc_shape,
        res=None, res_spec=None, dep=None):
    np_ = len(pairs)
    nk = grid[2]
    has_res = res is not None
    n_in = 2 * np_ + (1 if has_res else 0) + (0 if dep is None else 1)

    def body(*refs):
        a_refs = refs[:np_]
        b_refs = refs[np_:2 * np_]
        r_ref = refs[2 * np_] if has_res else None
        o_ref = refs[n_in]
        acc = refs[-1]

        def part():
            s = None
            for a_ref, b_ref in zip(a_refs, b_refs):
                blocks = [(a_ref[...], b_ref[...])] if len(a_ref.shape) == 2 else [
                    (a_ref[q], b_ref[q]) for q in range(a_ref.shape[0])]
                for av, bv in blocks:
                    d = lax.dot_general(av, bv, dims, preferred_element_type=F32)
                    s = d if s is None else s + d
            return s

        def finish(v):
            if has_res:
                v = v + r_ref[...]
            o_ref[...] = v.astype(o_ref.dtype)

        if nk == 1:
            finish(part())
        else:
            k = pl.program_id(2)

            @pl.when(k == 0)
            def _():
                acc[...] = part()

            @pl.when((k > 0) & (k < nk - 1))
            def _():
                acc[...] += part()

            @pl.when(k == nk - 1)
            def _():
                finish(acc[...] + part())

    ins = [p[0] for p in pairs] + [p[1] for p in pairs]
    specs = list(a_specs) + list(b_specs)
    if has_res:
        ins.append(res)
        specs.append(res_spec)
    if dep is not None:
        ins.append(dep)
        specs.append(ANY)
    return pl.pallas_call(
        body, name=name, grid=grid, in_specs=specs, out_specs=o_spec, out_shape=out_shape,
        scratch_shapes=[pltpu.VMEM(acc_shape if nk > 1 else (SUBLANES, LANES), F32)],
        compiler_params=_cp("parallel", "parallel", "arbitrary"))(*ins)


NN = (((1,), (0,)), ((), ()))
NT = (((1,), (1,)), ((), ()))
TN = (((0,), (0,)), ((), ()))


def _tiles_mk(m, kk, npairs=1):
    tk = _tile(kk, MM_TK, MXU_DEPTH)
    return _tile(m, 1024 if tk * npairs <= MM_TK else 512), tk


def _mm_nn(name, a, b, out_dtype, res=None, dep=None):
    pairs = list(zip(a, b)) if isinstance(a, (list, tuple)) else [(a, b)]
    m, kk = pairs[0][0].shape
    n = pairs[0][1].shape[1]
    tm, tk = _tiles_mk(m, kk, len(pairs))
    tn = _tile(n, 1024)
    return _mm(name, pairs,
               [pl.BlockSpec((tm, tk), lambda i, j, k: (i, k))] * len(pairs),
               [pl.BlockSpec((tk, tn), lambda i, j, k: (k, j))] * len(pairs), NN,
               jax.ShapeDtypeStruct((m, n), out_dtype),
               pl.BlockSpec((tm, tn), lambda i, j, k: (i, j)),
               (m // tm, n // tn, kk // tk), (tm, tn), res,
               pl.BlockSpec((tm, tn), lambda i, j, k: (i, j)), dep=dep)


def _mm_nt(name, a, b, out_dtype, dep=None):
    m, n = a.shape
    kk = b.shape[0]
    tn = _tile(kk, 1024)
    tm, tk = _tiles_mk(m, n)
    return _mm(name, [(a, b)],
               [pl.BlockSpec((tm, tk), lambda i, j, k: (i, k))],
               [pl.BlockSpec((tn, tk), lambda i, j, k: (j, k))], NT,
               jax.ShapeDtypeStruct((m, kk), out_dtype),
               pl.BlockSpec((tm, tn), lambda i, j, k: (i, j)),
               (m // tm, kk // tn, n // tk), (tm, tn), dep=dep)


def _mm_tn(name, a, b, out_dtype, dep=None):
    t, m = a.shape
    n = b.shape[1]
    tn = _tile(n, 1024)
    tm, tk = _tile(m, 1408), _tile(t, MM_TK, MXU_DEPTH)
    return _mm(name, [(a, b)],
               [pl.BlockSpec((tk, tm), lambda i, j, k: (k, i))],
               [pl.BlockSpec((tk, tn), lambda i, j, k: (k, j))], TN,
               jax.ShapeDtypeStruct((m, n), out_dtype),
               pl.BlockSpec((tm, tn), lambda i, j, k: (i, j)),
               (m // tm, n // tn, t // tk), (tm, tn), dep=dep)


def _ffn_fwd(name, n, wg, wu):
    f, d = wg.shape
    t = n.shape[0]
    tm, tn = _tile(t, 1024), _tile(f, 512)

    def body(n_ref, wg_ref, wu_ref, act_ref, ds_ref, s_ref):
        nv = n_ref[...]
        g = lax.dot_general(nv, wg_ref[...], NT, preferred_element_type=F32)
        up = lax.dot_general(nv, wu_ref[...], NT, preferred_element_type=F32)
        sg = _sigmoid(g)
        silu = g * sg
        act_ref[...] = (silu * up).astype(BF16)
        ds_ref[...] = (up * (sg * (1.0 + g * (1.0 - sg)))).astype(BF16)
        s_ref[...] = silu.astype(BF16)

    w_spec = pl.BlockSpec((tn, d), lambda j, i: (j, 0))
    o_spec = pl.BlockSpec((tm, tn), lambda j, i: (i, j))
    shp = jax.ShapeDtypeStruct((t, f), BF16)
    return pl.pallas_call(
        body, name=name, grid=(f // tn, t // tm),
        in_specs=[pl.BlockSpec((tm, d), lambda j, i: (i, 0)), w_spec, w_spec],
        out_specs=[o_spec, o_spec, o_spec], out_shape=[shp, shp, shp],
        compiler_params=_cp("parallel", "parallel"))(n, wg, wu)


def _ffn_bwd_act(name, dh, wd, dsilu, silu, dep=None):
    f, d = wd.shape
    t = dh.shape[0]
    tm, tn = _tile(t, 1024), _tile(f, 512)

    def body(dh_ref, wd_ref, ds_ref, s_ref, *rest):
        dg_ref, dup_ref = rest[-2:]
        da = lax.dot_general(dh_ref[...], wd_ref[...], NT, preferred_element_type=F32)
        dg_ref[...] = (da * ds_ref[...].astype(F32)).astype(BF16)
        dup_ref[...] = (da * s_ref[...].astype(F32)).astype(BF16)

    o_spec = pl.BlockSpec((tm, tn), lambda i, j: (i, j))
    shp = jax.ShapeDtypeStruct((t, f), BF16)
    return pl.pallas_call(
        body, name=name, grid=(t // tm, f // tn),
        in_specs=[pl.BlockSpec((tm, d), lambda i, j: (i, 0)),
                  pl.BlockSpec((tn, d), lambda i, j: (j, 0)), o_spec, o_spec]
        + ([] if dep is None else [ANY]),
        out_specs=[o_spec, o_spec], out_shape=[shp, shp],
        compiler_params=_cp("parallel", "parallel"))(dh, wd, dsilu, silu, *([] if dep is None else [dep]))


def _rms_fwd(name, h, gain):
    t, d = h.shape
    tr = _tile(t, 512, SUBLANES)

    def body(h_ref, g_ref, n_ref):
        hv = h_ref[...]
        r = lax.rsqrt(jnp.mean(hv * hv, axis=-1, keepdims=True) + EPS)
        n_ref[...] = (hv * r * g_ref[...]).astype(BF16)

    return pl.pallas_call(
        body, name=name, grid=(t // tr,),
        in_specs=[pl.BlockSpec((tr, d), lambda i: (i, 0)), pl.BlockSpec((1, d), lambda i: (0, 0))],
        out_specs=pl.BlockSpec((tr, d), lambda i: (i, 0)),
        out_shape=jax.ShapeDtypeStruct((t, d), BF16),
        compiler_params=_cp("parallel"))(h, gain)


def _rms_bwd_math(hv, gain, dn):
    d = hv.shape[-1]
    r = lax.rsqrt(jnp.mean(hv * hv, axis=-1, keepdims=True) + EPS)
    xhat = hv * r
    dxh = dn * gain
    dh = r * (dxh - xhat * (jnp.sum(dxh * xhat, axis=-1, keepdims=True) / d))
    dgain = jnp.sum(dn * xhat, axis=0, keepdims=True)
    return dh, dgain


def _rms_bwd(name, h, gain, dn, dres):
    t, d = h.shape
    tr = _tile(t, 256, SUBLANES)

    def body(h_ref, g_ref, dn_ref, dr_ref, dh_ref, dhb_ref, dg_ref):
        dh, dgain = _rms_bwd_math(h_ref[...], g_ref[...], dn_ref[...].astype(F32))
        dh = dh + dr_ref[...]
        dh_ref[...] = dh
        dhb_ref[...] = dh.astype(BF16)

        @pl.when(pl.program_id(0) == 0)
        def _():
            dg_ref[...] = dgain

        @pl.when(pl.program_id(0) > 0)
        def _():
            dg_ref[...] += dgain

    row = pl.BlockSpec((tr, d), lambda i: (i, 0))
    vec = pl.BlockSpec((1, d), lambda i: (0, 0))
    return pl.pallas_call(
        body, name=name, grid=(t // tr,), in_specs=[row, vec, row, row],
        out_specs=[row, row, vec],
        out_shape=[jax.ShapeDtypeStruct((t, d), F32), jax.ShapeDtypeStruct((t, d), BF16),
                   jax.ShapeDtypeStruct((1, d), F32)],
        compiler_params=_cp("arbitrary"))(h, gain, dn, dres)


def _loss_head(name, h, gain, tgt):
    t, d = h.shape
    tr = _tile(t, 256, SUBLANES)

    def body(h_ref, g_ref, t_ref, dh_ref, dhb_ref, dg_ref, ls_ref):
        hv = h_ref[...]
        gv = g_ref[...]
        r = lax.rsqrt(jnp.mean(hv * hv, axis=-1, keepdims=True) + EPS)
        err = hv * r * gv - t_ref[...]
        lsum = 0.5 * jnp.sum(err * err, axis=0, keepdims=True) / d
        dh, dgain = _rms_bwd_math(hv, gv, err / d)
        dh_ref[...] = dh
        dhb_ref[...] = dh.astype(BF16)

        @pl.when(pl.program_id(0) == 0)
        def _():
            dg_ref[...] = dgain
            ls_ref[...] = lsum

        @pl.when(pl.program_id(0) > 0)
        def _():
            dg_ref[...] += dgain
            ls_ref[...] += lsum

    row = pl.BlockSpec((tr, d), lambda i: (i, 0))
    vec = pl.BlockSpec((1, d), lambda i: (0, 0))
    return pl.pallas_call(
        body, name=name, grid=(t // tr,), in_specs=[row, vec, row],
        out_specs=[row, row, vec, vec],
        out_shape=[jax.ShapeDtypeStruct((t, d), F32), jax.ShapeDtypeStruct((t, d), BF16),
                   jax.ShapeDtypeStruct((1, d), F32), jax.ShapeDtypeStruct((1, d), F32)],
        compiler_params=_cp("arbitrary"))(h, gain, tgt)


def _conv_geom(t, seq, c, k, full_width=False):
    halo = 32 if k - 1 > SUBLANES else SUBLANES
    assert k - 1 <= halo
    tm = min(256 if halo > SUBLANES else 1024, seq // 2)
    tc = c if full_width else min(512, c)
    assert seq % tm == 0 and tm % halo == 0 and c % tc == 0 and t % seq == 0
    return halo, tm, tc, min(64 if halo > SUBLANES else 128, tm), min(LANES, tc)


def _pre(kind, a, b):
    if kind == "glu":
        return a * _sigmoid(b)
    if kind == "mul":
        return a * b
    return a


def _taps(k):
    return sorted((s % SUBLANES, s // SUBLANES, s) for s in range(k))


def _conv_fwd(name, seq, c, w, x1, c1, x2=None, c2=0, pre=None, bias=None, post=None, cpost=0, live=None):
    t = x1.shape[0]
    k = w.shape[0]
    halo, tm, tc, sr, sl = _conv_geom(t, seq, c, k, live is not None)
    nb, cps = tm // halo, seq // tm
    two = x2 is not None
    has_bias, has_post = bias is not None, post is not None

    def body(*refs):
        it = iter(refs)
        x1c, x1h = next(it), next(it)
        x2c, x2h = (next(it), next(it)) if two else (None, None)
        w_ref = next(it)
        b_ref = next(it) if has_bias else None
        p_ref = next(it) if has_post else None
        o_ref = next(it)
        y_ref = next(it) if has_post else None
        xs = next(it)
        first = (pl.program_id(1) % cps) == 0
        hv = _pre(pre, x1h[...].astype(F32), x2h[...].astype(F32) if two else None)
        xs[0:halo, :] = jnp.where(first, 0.0, hv)
        xs[halo:halo + tm, :] = _pre(pre, x1c[...].astype(F32), x2c[...].astype(F32) if two else None)
        for l0 in range(0, tc, sl):
            ls = slice(l0, l0 + sl)
            for r0 in range(0, tm, sr):
                win = xs[r0:r0 + sr + halo, ls]
                acc = jnp.zeros((sr, sl), F32)
                rolled = {}
                for r, q, s in _taps(k if live is None else live[l0 // sl]):
                    if r not in rolled:
                        rolled[r] = win if r == 0 else pltpu.roll(win, r, 0)
                    lo = halo - SUBLANES * q
                    acc = acc + w_ref[k - 1 - s:k - s, ls] * rolled[r][lo:lo + sr]
                if has_bias:
                    acc = acc + b_ref[:, ls]
                o_ref[r0:r0 + sr, ls] = acc.astype(o_ref.dtype)
                if has_post:
                    y_ref[r0:r0 + sr, ls] = (acc * p_ref[r0:r0 + sr, ls].astype(F32)).astype(y_ref.dtype)

    def cur(off):
        return pl.BlockSpec((tm, tc), lambda j, i: (i, off // tc + j))

    def prev(off):
        return pl.BlockSpec((halo, tc), lambda j, i: (jnp.maximum(i * nb - 1, 0), off // tc + j))

    ins, specs = [x1, x1], [cur(c1), prev(c1)]
    if two:
        ins += [x2, x2]
        specs += [cur(c2), prev(c2)]
    ins.append(w)
    specs.append(pl.BlockSpec((k, tc), lambda j, i: (0, j)))
    if has_bias:
        ins.append(bias)
        specs.append(pl.BlockSpec((1, tc), lambda j, i: (0, j)))
    if has_post:
        ins.append(post)
        specs.append(cur(cpost))
    o_spec = pl.BlockSpec((tm, tc), lambda j, i: (i, j))
    shp = jax.ShapeDtypeStruct((t, c), BF16)
    return pl.pallas_call(
        body, name=name, grid=(c // tc, t // tm), in_specs=specs,
        out_specs=[o_spec, o_spec] if has_post else o_spec,
        out_shape=[shp, shp] if has_post else shp,
        scratch_shapes=[pltpu.VMEM((halo + tm, tc), F32)],
        compiler_params=_cp("parallel", "parallel"))(*ins)


def _conv_bwd(name, seq, c, w, d1, cd1, d2=None, cd2=0, dpre=None,
              x1=None, c1=0, x2=None, c2=0, pre=None, live=None, dep=None):
    t = d1.shape[0]
    k = w.shape[0]
    assert live is None or x1 is None
    halo, tm, tc, sr, sl = _conv_geom(t, seq, c, k, live is not None)
    nb, cps = tm // halo, seq // tm
    nchunks = t // tm
    dtwo, xtwo, has_x = d2 is not None, x2 is not None, x1 is not None

    def body(*refs):
        it = iter(refs)
        d1c, d1n = next(it), next(it)
        d2c, d2n = (next(it), next(it)) if dtwo else (None, None)
        x1c, x1h = (next(it), next(it)) if has_x else (None, None)
        x2c, x2h = (next(it), next(it)) if xtwo else (None, None)
        w_ref = next(it)
        if dep is not None:
            next(it)
        dx_ref = next(it)
        dw_ref = next(it) if has_x else None
        ds = next(it)
        xs = next(it) if has_x else None
        i = pl.program_id(1)
        last = (i % cps) == cps - 1
        ds[0:tm, :] = _pre(dpre, d1c[...].astype(F32), d2c[...].astype(F32) if dtwo else None)
        nv = _pre(dpre, d1n[...].astype(F32), d2n[...].astype(F32) if dtwo else None)
        ds[tm:tm + halo, :] = jnp.where(last, 0.0, nv)
        if has_x:
            first = (i % cps) == 0
            hv = _pre(pre, x1h[...].astype(F32), x2h[...].astype(F32) if xtwo else None)
            xs[0:halo, :] = jnp.where(first, 0.0, hv)
            xs[halo:halo + tm, :] = _pre(pre, x1c[...].astype(F32), x2c[...].astype(F32) if xtwo else None)

            @pl.when(i == 0)
            def _():
                dw_ref[...] = jnp.zeros_like(dw_ref)

        for l0 in range(0, tc, sl):
            ls = slice(l0, l0 + sl)
            for r0 in range(0, tm, sr):
                win = ds[r0:r0 + sr + halo, ls]
                nrow = sr + halo
                acc = jnp.zeros((sr, sl), F32)
                rolled = {}
                for r, q, s in _taps(k if live is None else live[l0 // sl]):
                    if r not in rolled:
                        rolled[r] = win if r == 0 else pltpu.roll(win, nrow - r, 0)
                    lo = SUBLANES * q
                    acc = acc + w_ref[k - 1 - s:k - s, ls] * rolled[r][lo:lo + sr]
                dx_ref[r0:r0 + sr, ls] = acc.astype(dx_ref.dtype)
                if has_x:
                    dcur = win[0:sr]
                    xwin = xs[r0:r0 + sr + halo, ls]
                    xrolled = {}
                    for r, q, s in _taps(k):
                        if r not in xrolled:
                            xrolled[r] = xwin if r == 0 else pltpu.roll(xwin, r, 0)
                        lo = halo - SUBLANES * q
                        part = jnp.sum(dcur * xrolled[r][lo:lo + sr], axis=0, keepdims=True)
                        dw_ref[k - 1 - s:k - s, ls] += part

    def cur(off):
        return pl.BlockSpec((tm, tc), lambda j, i: (i, off // tc + j))

    def prev(off):
        return pl.BlockSpec((halo, tc), lambda j, i: (jnp.maximum(i * nb - 1, 0), off // tc + j))

    def nxt(off):
        return pl.BlockSpec((halo, tc),
                            lambda j, i: (jnp.minimum((i + 1) * nb, nchunks * nb - 1), off // tc + j))

    ins, specs = [d1, d1], [cur(cd1), nxt(cd1)]
    if dtwo:
        ins += [d2, d2]
        specs += [cur(cd2), nxt(cd2)]
    if has_x:
        ins += [x1, x1]
        specs += [cur(c1), prev(c1)]
    if xtwo:
        ins += [x2, x2]
        specs += [cur(c2), prev(c2)]
    ins.append(w)
    specs.append(pl.BlockSpec((k, tc), lambda j, i: (0, j)))
    if dep is not None:
        ins.append(dep)
        specs.append(ANY)
    o_specs = [pl.BlockSpec((tm, tc), lambda j, i: (i, j))]
    o_shapes = [jax.ShapeDtypeStruct((t, c), BF16)]
    scratch = [pltpu.VMEM((tm + halo, tc), F32)]
    if has_x:
        o_specs.append(pl.BlockSpec((k, tc), lambda j, i: (0, j)))
        o_shapes.append(jax.ShapeDtypeStruct((k, c), F32))
        scratch.append(pltpu.VMEM((halo + tm, tc), F32))
    out = pl.pallas_call(
        body, name=name, grid=(c // tc, t // tm), in_specs=specs, out_specs=o_specs,
        out_shape=o_shapes, scratch_shapes=scratch,
        compiler_params=_cp("parallel", "arbitrary"))(*ins)
    return out if has_x else out[0]


def _pool_taps(c):
    kmax = max(POOL_WINDOWS)
    grp = c // len(POOL_WINDOWS)
    cols = []
    for wdw in POOL_WINDOWS:
        col = jnp.concatenate([jnp.zeros((kmax - wdw,), F32), jnp.ones((wdw,), F32)])
        cols.append(jnp.tile(col[:, None], (1, grp)))
    return jnp.concatenate(cols, axis=1)


def _pool_live(c):
    grp, sl = c // len(POOL_WINDOWS), min(LANES, c)
    return tuple(max(POOL_WINDOWS[g] for g in range(l0 // grp, (l0 + sl - 1) // grp + 1))
                 for l0 in range(0, c, sl))


def _counts(i, tr, seq, grp):
    pos = (i * tr + lax.broadcasted_iota(jnp.int32, (tr, 1), 0)) % seq + 1
    return [1.0 / jnp.minimum(pos, wdw).astype(F32) for wdw in POOL_WINDOWS]


def _ln_stats(a2):
    mu = jnp.mean(a2, axis=-1, keepdims=True)
    xc = a2 - mu
    rstd = lax.rsqrt(jnp.mean(xc * xc, axis=-1, keepdims=True) + EPS)
    return xc * rstd, rstd


def _even_fwd(name, seq, a2, ws, u, ln_g, ln_b, w_pool, scale):
    t, c = a2.shape
    ng = len(POOL_WINDOWS)
    grp = c // ng
    tr = _tile(t, 256, SUBLANES)

    def body(a_ref, ws_ref, b_ref, g_ref, bb_ref, wp_ref, sc_ref, z_ref, pm_ref):
        xhat, _ = _ln_stats(a_ref[...].astype(F32))
        l = xhat * g_ref[...] + bb_ref[...]
        z_ref[:, 0:c] = (l * _sigmoid(l)).astype(BF16)
        inv = _counts(pl.program_id(0), tr, seq, grp)
        for g in range(ng):
            gs = slice(g * grp, (g + 1) * grp)
            pm = (ws_ref[:, gs].astype(F32) * inv[g] - b_ref[:, gs].astype(F32)).astype(BF16)
            pm_ref[:, gs] = pm
            q = jnp.dot(pm, wp_ref[g], preferred_element_type=F32)
            z_ref[:, c + g * grp:c + (g + 1) * grp] = (q * sc_ref[:, gs]).astype(BF16)

    row = pl.BlockSpec((tr, c), lambda i: (i, 0))
    vec = pl.BlockSpec((1, c), lambda i: (0, 0))
    return pl.pallas_call(
        body, name=name, grid=(t // tr,),
        in_specs=[row, row, pl.BlockSpec((tr, c), lambda i: (i, 2)), vec, vec,
                  pl.BlockSpec((ng, grp, grp), lambda i: (0, 0, 0)), vec],
        out_specs=[pl.BlockSpec((tr, 2 * c), lambda i: (i, 0)), row],
        out_shape=[jax.ShapeDtypeStruct((t, 2 * c), BF16), jax.ShapeDtypeStruct((t, c), BF16)],
        compiler_params=_cp("parallel"))(a2, ws, u, ln_g, ln_b, w_pool, scale)


def _even_bwd(name, seq, dz, a2, pm, ln_g, ln_b, w_pool, scale):
    t, c = a2.shape
    ng = len(POOL_WINDOWS)
    grp = c // ng
    tr = _tile(t, 256, SUBLANES)

    def body(dz_ref, a_ref, pm_ref, g_ref, bb_ref, wp_ref, sc_ref,
             da_ref, dws_ref, dpm_ref, vec_ref, dwp_ref):
        i = pl.program_id(0)

        @pl.when(i == 0)
        def _():
            vec_ref[...] = jnp.zeros_like(vec_ref)
            dwp_ref[...] = jnp.zeros_like(dwp_ref)

        xhat, rstd = _ln_stats(a_ref[...].astype(F32))
        gv = g_ref[...]
        l = xhat * gv + bb_ref[...]
        sg = _sigmoid(l)
        dl = dz_ref[:, 0:c].astype(F32) * (sg * (1.0 + l * (1.0 - sg)))
        dxh = dl * gv
        da2 = rstd * (dxh - jnp.mean(dxh, axis=-1, keepdims=True)
                      - xhat * jnp.mean(dxh * xhat, axis=-1, keepdims=True))
        da_ref[...] = da2.astype(BF16)
        vec_ref[0:1, :] += jnp.sum(dl * xhat, axis=0, keepdims=True)
        vec_ref[1:2, :] += jnp.sum(dl, axis=0, keepdims=True)
        vec_ref[2:3, :] += jnp.sum(da2, axis=0, keepdims=True)
        inv = _counts(i, tr, seq, grp)
        for g in range(ng):
            gs = slice(g * grp, (g + 1) * grp)
            pmv = pm_ref[:, gs]
            wp = wp_ref[g]
            dp = dz_ref[:, c + g * grp:c + (g + 1) * grp].astype(F32)
            q = jnp.dot(pmv, wp, preferred_element_type=F32)
            vec_ref[3:4, gs] += jnp.sum(dp * q, axis=0, keepdims=True)
            dq = (dp * sc_ref[:, gs]).astype(BF16)
            dpm = lax.dot_general(dq, wp, NT, preferred_element_type=F32)
            dwp_ref[g] += lax.dot_general(pmv, dq, TN, preferred_element_type=F32)
            dpm_ref[:, gs] = dpm.astype(BF16)
            dws_ref[:, gs] = (dpm * inv[g]).astype(BF16)

    row = pl.BlockSpec((tr, c), lambda i: (i, 0))
    vec = pl.BlockSpec((1, c), lambda i: (0, 0))
    rshape = jax.ShapeDtypeStruct((t, c), BF16)
    return pl.pallas_call(
        body, name=name, grid=(t // tr,),
        in_specs=[pl.BlockSpec((tr, 2 * c), lambda i: (i, 0)), row, row, vec, vec,
                  pl.BlockSpec((ng, grp, grp), lambda i: (0, 0, 0)), vec],
        out_specs=[row, row, row, pl.BlockSpec((SUBLANES, c), lambda i: (0, 0)),
                   pl.BlockSpec((ng, grp, grp), lambda i: (0, 0, 0))],
        out_shape=[rshape, rshape, rshape, jax.ShapeDtypeStruct((SUBLANES, c), F32),
                   jax.ShapeDtypeStruct((ng, grp, grp), F32)],
        compiler_params=_cp("arbitrary"))(dz, a2, pm, ln_g, ln_b, w_pool, scale)


def _even_du(name, u, da1, dbp, dpm):
    t, c = da1.shape
    tr = _tile(t, 256, SUBLANES)

    def body(u_ref, da_ref, dbp_ref, dpm_ref, du_ref):
        val = u_ref[:, 0:c].astype(F32)
        sg = _sigmoid(u_ref[:, c:2 * c].astype(F32))
        da = da_ref[...].astype(F32)
        du_ref[:, 0:c] = (da * sg).astype(BF16)
        du_ref[:, c:2 * c] = (da * val * sg * (1.0 - sg)).astype(BF16)
        du_ref[:, 2 * c:3 * c] = (dbp_ref[...].astype(F32) - dpm_ref[...].astype(F32)).astype(BF16)

    row = pl.BlockSpec((tr, c), lambda i: (i, 0))
    wide = pl.BlockSpec((tr, 3 * c), lambda i: (i, 0))
    return pl.pallas_call(
        body, name=name, grid=(t // tr,), in_specs=[wide, row, row, row], out_specs=wide,
        out_shape=jax.ShapeDtypeStruct((t, 3 * c), BF16),
        compiler_params=_cp("parallel"))(u, da1, dbp, dpm)


def _odd_du(name, u, dy, co, dxc):
    t, c = dy.shape
    tr = _tile(t, 256, SUBLANES)

    def body(u_ref, dy_ref, co_ref, dx_ref, du_ref):
        dx = dx_ref[...].astype(F32)
        du_ref[:, 0:c] = (dy_ref[...].astype(F32) * co_ref[...].astype(F32)).astype(BF16)
        du_ref[:, c:2 * c] = (dx * u_ref[:, 2 * c:3 * c].astype(F32)).astype(BF16)
        du_ref[:, 2 * c:3 * c] = (dx * u_ref[:, c:2 * c].astype(F32)).astype(BF16)

    row = pl.BlockSpec((tr, c), lambda i: (i, 0))
    wide = pl.BlockSpec((tr, 3 * c), lambda i: (i, 0))
    return pl.pallas_call(
        body, name=name, grid=(t // tr,), in_specs=[wide, row, row, row], out_specs=wide,
        out_shape=jax.ShapeDtypeStruct((t, 3 * c), BF16),
        compiler_params=_cp("parallel"))(u, dy, co, dxc)


def _local_step(x, tgt, seq, small, get_w, put_g, sync):
    t, d = x.shape
    c = d // 2
    cw_e, cw_o = small["conv_w_e"], small["conv_w_o"]
    wp = small["w_pool_e"].astype(BF16)
    ptaps = _pool_taps(c)
    row = lambda v: v.reshape(1, -1)

    we = {"w_in": get_w("in_e", x)[0]}
    n0 = _rms_fwd("rms_fwd_mix0", x, row(small["mix_norm_e"]))
    u0 = _mm_nn("mm_in_e", n0, we["w_in"], BF16)
    sync("fwd_a", u0)
    a2 = _conv_fwd("conv_e_fwd", seq, c, cw_e, u0, 0, u0, c, "glu", bias=row(small["conv_b_e"]))
    ws = _conv_fwd("pool_fwd", seq, c, ptaps, u0, 2 * c, live=_pool_live(c))
    z0, pm = _even_fwd("even_fwd", seq, a2, ws, u0, row(small["ln_g_e"]), row(small["ln_b_e"]),
                       wp, row(small["pool_scale_e"]))
    we["w_out"] = get_w("out_e", z0)[0]
    h1 = _mm_nn("mm_out_e", z0, we["w_out"], F32, res=x)
    sync("fwd_b", h1)
    n1 = _rms_fwd("rms_fwd_ffn0", h1, row(small["ffn_norm"][0]))
    wf0 = dict(zip(("w_gate", "w_up"), get_w("gu0", n1)))
    act0, ds0, s0 = _ffn_fwd("ffn0_fwd", n1, wf0["w_gate"], wf0["w_up"])
    dep = sync("fwd_c", act0)
    wf0["w_down"] = get_w("down0", act0)[0]
    h2 = _mm_nn("mm_down0", act0, wf0["w_down"], F32, res=h1, dep=dep)
    dep = sync("fwd_d", h2)
    n2 = _rms_fwd("rms_fwd_mix1", h2, row(small["mix_norm_o"]))
    wo = {"w_in": get_w("in_o", n2)[0]}
    u1 = _mm_nn("mm_in_o", n2, wo["w_in"], BF16, dep=dep)
    co, y1 = _conv_fwd("conv_o_fwd", seq, d, cw_o, u1, d, u1, 2 * d, "mul", post=u1, cpost=0)
    dep = sync("fwd_e", y1)
    wo["w_out"] = get_w("out_o", y1)[0]
    h3 = _mm_nn("mm_out_o", y1, wo["w_out"], F32, res=h2, dep=dep)
    sync("fwd_f", h3)
    n3 = _rms_fwd("rms_fwd_ffn1", h3, row(small["ffn_norm"][1]))
    wf1 = dict(zip(("w_gate", "w_up"), get_w("gu1", n3)))
    act1, ds1, s1 = _ffn_fwd("ffn1_fwd", n3, wf1["w_gate"], wf1["w_up"])
    wf1["w_down"] = get_w("down1", act1)[0]
    h4 = _mm_nn("mm_down1", act1, wf1["w_down"], F32, res=h3)

    dh4, dh4b, d_final, lsum = _loss_head("loss_head", h4, row(small["final_norm"]), tgt)

    def ffn_bwd(tag, dh, dhb, h_in, gain, n, dsilu, silu, act, w, dep):
        dg, dup = _ffn_bwd_act("ffn%s_bwd_act" % tag, dhb, w["w_down"], dsilu, silu, dep=dep)
        dwd = _mm_tn("mm_dwd%s" % tag, act, dhb, BF16)
        dwg = _mm_tn("mm_dwg%s" % tag, dg, n, BF16, dep=sync("bwd_ffn" + tag, dwd))
        dwu = _mm_tn("mm_dwu%s" % tag, dup, n, BF16)
        dn = _mm_nn("mm_ffn_dn%s" % tag, [dg, dup], [w["w_gate"], w["w_up"]], BF16)
        dh_in, dhb_in, dgain = _rms_bwd("rms_bwd_ffn%s" % tag, h_in, gain, dn, dh)
        dep = put_g("ffn" + tag, {"w_gate": dwg, "w_up": dwu, "w_down": dwd})
        return dh_in, dhb_in, dgain, dep

    dh3, dh3b, d_ffn1, dep = ffn_bwd("1", dh4, dh4b, h3, row(small["ffn_norm"][1]), n3, ds1, s1,
                                     act1, wf1, None)
    dw_out_o = _mm_tn("mm_dw_out_o", y1, dh3b, BF16, dep=dep)
    dy1 = _mm_nt("mm_dy_o", dh3b, wo["w_out"], BF16, dep=sync("bwd_mix_o", dw_out_o))
    dxc, dcw_o = _conv_bwd("conv_o_bwd", seq, d, cw_o, dy1, 0, u1, 0, "mul",
                           x1=u1, c1=d, x2=u1, c2=2 * d, pre="mul")
    du1 = _odd_du("odd_du", u1, dy1, co, dxc)
    dw_in_o = _mm_tn("mm_dw_in_o", n2, du1, BF16)
    dn2 = _mm_nt("mm_dn_o", du1, wo["w_in"], BF16)
    dh2, dh2b, d_mix_o = _rms_bwd("rms_bwd_mix1", h2, row(small["mix_norm_o"]), dn2, dh3)
    dep = put_g("mix_o", {"w_in": dw_in_o, "w_out": dw_out_o})

    dh1, dh1b, d_ffn0, dep = ffn_bwd("0", dh2, dh2b, h1, row(small["ffn_norm"][0]), n1, ds0, s0,
                                     act0, wf0, dep)
    dw_out_e = _mm_tn("mm_dw_out_e", z0, dh1b, BF16, dep=dep)
    dz0 = _mm_nt("mm_dz_e", dh1b, we["w_out"], BF16, dep=sync("bwd_mix_e", dw_out_e))
    da2, dws, dpm, vecs, dwp = _even_bwd("even_bwd", seq, dz0, a2, pm, row(small["ln_g_e"]),
                                         row(small["ln_b_e"]), wp, row(small["pool_scale_e"]))
    dep = put_g("small", {"conv_b_e": vecs[2], "ln_g_e": vecs[0], "ln_b_e": vecs[1],
                          "w_pool_e": dwp, "pool_scale_e": vecs[3], "mix_norm_o": d_mix_o[0],
                          "conv_w_o": dcw_o, "ffn_norm": jnp.concatenate([d_ffn0, d_ffn1], axis=0),
                          "final_norm": d_final[0]})
    da1, dcw_e = _conv_bwd("conv_e_bwd", seq, c, cw_e, da2, 0, x1=u0, c1=0, x2=u0, c2=c, pre="glu", dep=dep)
    dbp = _conv_bwd("pool_bwd", seq, c, ptaps, dws, 0, live=_pool_live(c))
    du0 = _even_du("even_du", u0, da1, dbp, dpm)
    dw_in_e = _mm_tn("mm_dw_in_e", n0, du0, BF16)
    dep = put_g("mix_e", {"w_in": dw_in_e, "w_out": dw_out_e})
    dn0 = _mm_nt("mm_dn_e", du0, we["w_in"], BF16, dep=dep)
    dx, _, d_mix_e = _rms_bwd("rms_bwd_mix0", x, row(small["mix_norm_e"]), dn0, dh1)
    return lsum, dx, {"conv_w_e": dcw_e, "mix_norm_e": d_mix_e[0]}


def _place():
    x, y, c = (lax.axis_index(a) for a in MESH_AXES)
    return x, y, c


def _index(p):
    return 4 * p[0] + 2 * p[1] + p[2]


def _slab(ref, kind, d, n):
    if kind == "blk":
        return ref.at[d]
    return ref.at[:, pl.ds(pl.multiple_of(d * n, LANES), n)]


HBM = pl.BlockSpec(memory_space=pltpu.HBM)
SEM = pl.BlockSpec(memory_space=pltpu.SEMAPHORE)
EFFECT = pltpu.SideEffectType.DATAFLOW_SIDE_EFFECTING
NCHIPS = 4


def _in_hbm(a):
    return pltpu.with_memory_space_constraint(a, pltpu.HBM)


def _gathered_shape(s, kind):
    m, n = s.shape
    return (NDEV, m, n) if kind == "blk" else (m, NDEV * n)


def _first_targets():
    x, y, c = _place()
    return [(x, y, 1 - c), (1 - x, y, c), (x, 1 - y, c), (1 - x, 1 - y, c)]


def _gather_start(name, shards, kinds, after):
    na = len(shards)

    def body(*refs):
        x_refs, land_refs = refs[:na], refs[na:2 * na]
        send_sems, recv_sems = refs[2 * na + 1], refs[2 * na + 2]
        token = refs[-1]
        me = _index(_place())
        for a in range(na):
            for k, to in enumerate(_first_targets()):
                pltpu.make_async_remote_copy(
                    src_ref=x_refs[a], dst_ref=_slab(land_refs[a], kinds[a], me, shards[a].shape[1]),
                    send_sem=send_sems.at[4 * a + k], recv_sem=recv_sems.at[4 * a + k],
                    device_id=to, device_id_type=MESH).start()
        token[...] = jnp.zeros_like(token)

    lands = [lax.empty(_gathered_shape(s, k), s.dtype) for s, k in zip(shards, kinds)]
    outs = pl.pallas_call(
        body, name=name,
        out_shape=(pltpu.SemaphoreType.DMA((4 * na,)), pltpu.SemaphoreType.DMA((4 * na,)),
                   *[pltpu.HBM(s.shape, s.dtype) for s in shards],
                   *[pltpu.HBM(l.shape, l.dtype) for l in lands],
                   jax.ShapeDtypeStruct((SUBLANES, LANES), F32)),
        in_specs=[HBM] * (2 * na) + [ANY],
        out_specs=(SEM, SEM, *[HBM] * (2 * na), pl.BlockSpec(memory_space=pltpu.VMEM)),
        input_output_aliases={i: 2 + i for i in range(2 * na)},
        compiler_params=pltpu.CompilerParams(has_side_effects=EFFECT),
    )(*[_in_hbm(s) for s in shards], *[_in_hbm(l) for l in lands], after)
    return outs[0], outs[1], outs[2:2 + na], outs[2 + na:2 + 2 * na], outs[-1]


def _gather_wait(name, started, kinds, after):
    send_sems, recv_sems, shards, lands, _ = started
    na = len(shards)

    def body(*refs):
        x_refs, land_refs = refs[:na], refs[na:2 * na]
        s_sems, r_sems = refs[2 * na], refs[2 * na + 1]
        for a in range(na):
            for k, frm in enumerate(_first_targets()):
                cp = pltpu.make_async_remote_copy(
                    src_ref=x_refs[a],
                    dst_ref=_slab(land_refs[a], kinds[a], _index(frm), shards[a].shape[1]),
                    send_sem=s_sems.at[4 * a + k], recv_sem=r_sems.at[4 * a + k],
                    device_id=frm, device_id_type=MESH)
                cp.wait_send()
                cp.wait_recv()

    outs = pl.pallas_call(
        body, name=name,
        out_shape=(*[pltpu.HBM(s.shape, s.dtype) for s in shards],
                   *[pltpu.HBM(l.shape, l.dtype) for l in lands]),
        in_specs=[HBM] * (2 * na) + [SEM, SEM, ANY], out_specs=[HBM] * (2 * na),
        input_output_aliases={i: i for i in range(2 * na)},
        compiler_params=pltpu.CompilerParams(has_side_effects=EFFECT),
    )(*shards, *lands, send_sems, recv_sems, after)
    return outs[:na], outs[na:]


def _split_start(name, bufs, ncopies, plan, after):
    nb = len(bufs)

    def body(*refs):
        send_sems, recv_sems, token = refs[nb + 1], refs[nb + 2], refs[-1]
        for k, (src, dst, to, _) in enumerate(plan(refs[:nb])):
            pltpu.make_async_remote_copy(src_ref=src, dst_ref=dst, send_sem=send_sems.at[k],
                                         recv_sem=recv_sems.at[k], device_id=to, device_id_type=MESH).start()
        token[...] = jnp.zeros_like(token)

    outs = pl.pallas_call(
        body, name=name,
        out_shape=(pltpu.SemaphoreType.DMA((ncopies,)), pltpu.SemaphoreType.DMA((ncopies,)),
                   *[pltpu.HBM(b.shape, b.dtype) for b in bufs],
                   jax.ShapeDtypeStruct((SUBLANES, LANES), F32)),
        in_specs=[HBM] * nb + [ANY],
        out_specs=(SEM, SEM, *[HBM] * nb, pl.BlockSpec(memory_space=pltpu.VMEM)),
        input_output_aliases={i: 2 + i for i in range(nb)},
        compiler_params=pltpu.CompilerParams(has_side_effects=EFFECT),
    )(*[_in_hbm(b) for b in bufs], after)
    return outs[0], outs[1], list(outs[2:2 + nb]), outs[-1]


def _split_wait(name, started, plan, after):
    send_sems, recv_sems, bufs, _ = started
    nb = len(bufs)

    def body(*refs):
        s_sems, r_sems = refs[nb], refs[nb + 1]
        for k, (src, _, to, landed) in enumerate(plan(refs[:nb])):
            cp = pltpu.make_async_remote_copy(src_ref=src, dst_ref=landed, send_sem=s_sems.at[k],
                                              recv_sem=r_sems.at[k], device_id=to, device_id_type=MESH)
            cp.wait_send()
            cp.wait_recv()

    outs = pl.pallas_call(
        body, name=name, out_shape=tuple(pltpu.HBM(b.shape, b.dtype) for b in bufs),
        in_specs=[HBM] * nb + [SEM, SEM, ANY], out_specs=[HBM] * nb,
        input_output_aliases={i: i for i in range(nb)},
        compiler_params=pltpu.CompilerParams(has_side_effects=EFFECT),
    )(*bufs, send_sems, recv_sems, after)
    return list(outs)


def _forward_plan(kinds, nloc):
    def plan(lands):
        x, y, c = _place()
        out = []
        for a, land in enumerate(lands):
            for chip in [(1 - x, y), (x, 1 - y), (1 - x, 1 - y)]:
                mine = _slab(land, kinds[a], _index((*chip, c)), nloc[a])
                out.append((mine, mine, (x, y, 1 - c), _slab(land, kinds[a], _index((*chip, 1 - c)), nloc[a])))
        return out
    return plan


def _own_copy(name, shard, land, kind, me):
    m, n = shard.shape
    tr = _tile(m, max(SUBLANES, 1048576 // n), SUBLANES)

    def body(s_ref, x_ref, land_ref, o_ref):
        o_ref[...] = x_ref[...]

    if kind == "blk":
        o_spec = pl.BlockSpec((None, tr, n), lambda i, s: (s[0], i, 0))
    else:
        o_spec = pl.BlockSpec((tr, n), lambda i, s: (i, s[0]))
    return pl.pallas_call(
        body, name=name,
        grid_spec=pltpu.PrefetchScalarGridSpec(
            num_scalar_prefetch=1, grid=(m // tr,),
            in_specs=[pl.BlockSpec((tr, n), lambda i, s: (i, 0)), ANY], out_specs=o_spec),
        out_shape=jax.ShapeDtypeStruct(land.shape, land.dtype),
        input_output_aliases={2: 0}, compiler_params=_cp("parallel"))(me, shard, land)


def _everyone_plan(refs):
    x, y, c = _place()
    out = []
    for dx, dy, dc in [(a, b, e) for a in (0, 1) for b in (0, 1) for e in (0, 1)][1:]:
        peer = (x ^ dx, y ^ dy, c ^ dc)
        out.append((refs[0], refs[1].at[_index((x, y, c))], peer, refs[1].at[_index(peer)]))
    return out


def _pair_plan(kinds, nloc):
    na = len(kinds)

    def plan(refs):
        x, y, c = _place()
        out = []
        for a in range(na):
            for j in range(NCHIPS):
                dst = refs[na + a].at[j]
                out.append((_slab(refs[a], kinds[a], 2 * j + (1 - c), nloc[a]), dst, (x, y, 1 - c), dst))
        return out
    return plan


def _chip_sum(name, full, kind, n, from_sib, place):
    _, m, _ = from_sib.shape
    tr = _tile(m, max(SUBLANES, 1048576 // n), SUBLANES)

    def body(s_ref, mine_ref, sib_ref, csum_ref, land_ref):
        v = (mine_ref[...].astype(F32) + sib_ref[...].astype(F32)).astype(csum_ref.dtype)
        csum_ref[...] = v

        @pl.when(pl.program_id(1) == s_ref[1])
        def _():
            land_ref[...] = v

    if kind == "blk":
        mine_spec = pl.BlockSpec((None, tr, n), lambda i, j, s: (2 * j + s[0], i, 0))
    else:
        mine_spec = pl.BlockSpec((tr, n), lambda i, j, s: (i, 2 * j + s[0]))
    slot = pl.BlockSpec((None, tr, n), lambda i, j, s: (j, i, 0))
    shp = jax.ShapeDtypeStruct((NCHIPS, m, n), from_sib.dtype)
    return pl.pallas_call(
        body, name=name,
        grid_spec=pltpu.PrefetchScalarGridSpec(
            num_scalar_prefetch=1, grid=(m // tr, NCHIPS), in_specs=[mine_spec, slot],
            out_specs=[slot, pl.BlockSpec((None, tr, n), lambda i, j, s: (s[1], i, 0))]),
        out_shape=[shp, shp], compiler_params=_cp("parallel", "arbitrary"))(place, full, from_sib)


def _other_chips():
    x, y, c = _place()
    return [(1 - x, y, c), (x, 1 - y, c), (1 - x, 1 - y, c)]


def _scatter_start(name, csums, lands, after):
    na = len(csums)

    def body(*refs):
        c_refs, land_refs = refs[:na], refs[na:2 * na]
        send_sems, recv_sems = refs[2 * na + 1], refs[2 * na + 2]
        token = refs[-1]
        x, y, _ = _place()
        for a in range(na):
            for k, to in enumerate(_other_chips()):
                pltpu.make_async_remote_copy(
                    src_ref=c_refs[a].at[2 * to[0] + to[1]], dst_ref=land_refs[a].at[2 * x + y],
                    send_sem=send_sems.at[3 * a + k], recv_sem=recv_sems.at[3 * a + k],
                    device_id=to, device_id_type=MESH).start()
        token[...] = jnp.zeros_like(token)

    outs = pl.pallas_call(
        body, name=name,
        out_shape=(pltpu.SemaphoreType.DMA((3 * na,)), pltpu.SemaphoreType.DMA((3 * na,)),
                   *[pltpu.HBM(s.shape, s.dtype) for s in csums],
                   *[pltpu.HBM(l.shape, l.dtype) for l in lands],
                   jax.ShapeDtypeStruct((SUBLANES, LANES), F32)),
        in_specs=[HBM] * (2 * na) + [ANY],
        out_specs=(SEM, SEM, *[HBM] * (2 * na), pl.BlockSpec(memory_space=pltpu.VMEM)),
        input_output_aliases={i: 2 + i for i in range(2 * na)},
        compiler_params=pltpu.CompilerParams(has_side_effects=EFFECT),
    )(*[_in_hbm(s) for s in csums], *[_in_hbm(l) for l in lands], after)
    return outs[0], outs[1], outs[2:2 + na], outs[2 + na:2 + 2 * na], outs[-1]


def _scatter_wait(name, started, after):
    send_sems, recv_sems, csums, lands, _ = started
    na = len(csums)

    def body(*refs):
        c_refs, land_refs = refs[:na], refs[na:2 * na]
        s_sems, r_sems = refs[2 * na], refs[2 * na + 1]
        for a in range(na):
            for k, frm in enumerate(_other_chips()):
                cp = pltpu.make_async_remote_copy(
                    src_ref=c_refs[a].at[2 * frm[0] + frm[1]], dst_ref=land_refs[a].at[2 * frm[0] + frm[1]],
                    send_sem=s_sems.at[3 * a + k], recv_sem=r_sems.at[3 * a + k],
                    device_id=frm, device_id_type=MESH)
                cp.wait_send()
                cp.wait_recv()

    outs = pl.pallas_call(
        body, name=name,
        out_shape=(*[pltpu.HBM(s.shape, s.dtype) for s in csums],
                   *[pltpu.HBM(l.shape, l.dtype) for l in lands]),
        in_specs=[HBM] * (2 * na) + [SEM, SEM, ANY], out_specs=[HBM] * (2 * na),
        input_output_aliases={i: i for i in range(2 * na)},
        compiler_params=pltpu.CompilerParams(has_side_effects=EFFECT),
    )(*csums, *lands, send_sems, recv_sems, after)
    return outs[na:]


def _adam_math(w, g, m, v):
    m = ADAM_B1 * m + (1.0 - ADAM_B1) * g
    v = ADAM_B2 * v + (1.0 - ADAM_B2) * (g * g)
    m_hat = m / (1.0 - ADAM_B1 ** ADAM_STEP)
    v_hat = v / (1.0 - ADAM_B2 ** ADAM_STEP)
    delta = -ADAM_LR * (m_hat / (jnp.sqrt(v_hat) + ADAM_EPS) + ADAM_WD * w)
    return delta, m, v


def _sum_adamw(name, parts, w, m, v, layer, prev=None, dep=None):
    nl, r, c = w.shape
    nparts = parts.shape[0]
    tr = _tile(r, max(SUBLANES, 360448 // c), SUBLANES)

    def body(p_ref, w_ref, m_ref, v_ref, *rest):
        g_ref, d_ref, mo_ref, vo_ref = rest[-4:]
        g = p_ref[0].astype(F32)
        for s in range(1, nparts):
            g = g + p_ref[s].astype(F32)
        delta, mn, vn = _adam_math(w_ref[...], g, m_ref[...], v_ref[...])
        g_ref[...] = g
        d_ref[...] = delta
        mo_ref[...] = mn
        vo_ref[...] = vn

    row = pl.BlockSpec((None, tr, c), lambda i: (layer, i, 0))
    shp = jax.ShapeDtypeStruct((nl, r, c), F32)
    extra = ([] if prev is None else list(prev)) + ([] if dep is None else [dep])
    return pl.pallas_call(
        body, name=name, grid=(r // tr,),
        in_specs=[pl.BlockSpec((nparts, tr, c), lambda i: (0, i, 0)), row, row, row] + [ANY] * len(extra),
        out_specs=[row, row, row, row], out_shape=[shp, shp, shp, shp],
        input_output_aliases={} if prev is None else {4 + i: i for i in range(4)},
        compiler_params=_cp("parallel"))(parts, w, m, v, *extra)


def _sum_parts(name, parts):
    _, r, c = parts.shape

    def body(p_ref, o_ref):
        g = p_ref[0]
        for s in range(1, NDEV):
            g = g + p_ref[s]
        o_ref[...] = g

    return pl.pallas_call(
        body, name=name, grid=(1,),
        in_specs=[pl.BlockSpec((NDEV, r, c), lambda i: (0, 0, 0))],
        out_specs=pl.BlockSpec((r, c), lambda i: (0, 0)),
        out_shape=jax.ShapeDtypeStruct((r, c), F32), compiler_params=_cp("arbitrary"))(parts)


def _adamw(name, w, g, m, v):
    r, c = w.shape

    def body(w_ref, g_ref, m_ref, v_ref, d_ref, mo_ref, vo_ref):
        delta, mn, vn = _adam_math(w_ref[...], g_ref[...], m_ref[...], v_ref[...])
        d_ref[...] = delta
        mo_ref[...] = mn
        vo_ref[...] = vn

    full = pl.BlockSpec((r, c), lambda i: (0, 0))
    shp = jax.ShapeDtypeStruct((r, c), F32)
    return pl.pallas_call(
        body, name=name, grid=(1,), in_specs=[full] * 4, out_specs=[full] * 3,
        out_shape=[shp] * 3, compiler_params=_cp("arbitrary"))(w, g, m, v)


def _pack(arrays):
    flat = jnp.concatenate([a.reshape(-1) for a in arrays])
    unit = SUBLANES * LANES
    pad = (-flat.shape[0]) % unit
    return jnp.pad(flat, (0, pad)).reshape(-1, LANES)


def _unpack(buf, shapes):
    flat = buf.reshape(-1)
    out, off = [], 0
    for shp in shapes:
        size = 1
        for s in shp:
            size *= s
        out.append(flat[off:off + size].reshape(shp))
        off += size
    return out


WEIGHTS = ["mix_norm_e", "w_in_e", "conv_w_e", "conv_b_e", "ln_g_e", "ln_b_e", "w_pool_e",
           "pool_scale_e", "w_out_e", "mix_norm_o", "w_in_o", "conv_w_o", "w_out_o", "ffn_norm",
           "w_gate", "w_up", "w_down", "final_norm"]
BIG = ["w_in_e", "w_out_e", "w_in_o", "w_out_o", "w_gate", "w_up", "w_down"]
SHARDED_SMALL = {"conv_w_e": 1, "w_pool_e": 1, "mix_norm_o": 0, "conv_w_o": 1}
SMALL = [n for n in WEIGHTS if n not in BIG]


def kernel(x, mix_norm_e, w_in_e, conv_w_e, conv_b_e, ln_g_e, ln_b_e, w_pool_e, pool_scale_e, w_out_e, mix_norm_o, w_in_o, conv_w_o, w_out_o, ffn_norm, w_gate, w_up, w_down, final_norm, loss_target, m_mix_norm_e, m_w_in_e, m_conv_w_e, m_conv_b_e, m_ln_g_e, m_ln_b_e, m_w_pool_e, m_pool_scale_e, m_w_out_e, m_mix_norm_o, m_w_in_o, m_conv_w_o, m_w_out_o, m_ffn_norm, m_w_gate, m_w_up, m_w_down, m_final_norm, v_mix_norm_e, v_w_in_e, v_conv_w_e, v_conv_b_e, v_ln_g_e, v_ln_b_e, v_w_pool_e, v_pool_scale_e, v_w_out_e, v_mix_norm_o, v_w_in_o, v_conv_w_o, v_w_out_o, v_ffn_norm, v_w_gate, v_w_up, v_w_down, v_final_norm):
    wts = dict(zip(WEIGHTS, [mix_norm_e, w_in_e, conv_w_e, conv_b_e, ln_g_e, ln_b_e, w_pool_e, pool_scale_e, w_out_e, mix_norm_o, w_in_o, conv_w_o, w_out_o, ffn_norm, w_gate, w_up, w_down, final_norm]))
    mom = dict(zip(WEIGHTS, [m_mix_norm_e, m_w_in_e, m_conv_w_e, m_conv_b_e, m_ln_g_e, m_ln_b_e, m_w_pool_e, m_pool_scale_e, m_w_out_e, m_mix_norm_o, m_w_in_o, m_conv_w_o, m_w_out_o, m_ffn_norm, m_w_gate, m_w_up, m_w_down, m_final_norm]))
    var = dict(zip(WEIGHTS, [v_mix_norm_e, v_w_in_e, v_conv_w_e, v_conv_b_e, v_ln_g_e, v_ln_b_e, v_w_pool_e, v_pool_scale_e, v_w_out_e, v_mix_norm_o, v_w_in_o, v_conv_w_o, v_w_out_o, v_ffn_norm, v_w_gate, v_w_up, v_w_down, v_final_norm]))
    bsz, seq, d = x.shape
    t = bsz * seq
    me = _index(_place())
    me_arr = jnp.reshape(me, (1,)).astype(jnp.int32)

    sh_names = list(SHARDED_SMALL)
    sh_local = [wts[n][0] for n in sh_names]
    packed = _pack(sh_local)
    params_st = _split_start("small_params_start", [packed, lax.empty((NDEV,) + packed.shape, F32)], NDEV - 1,
                             _everyone_plan, x)

    for state in (wts, mom, var):
        for n in ("w_gate", "w_up"):
            state[n] = jnp.swapaxes(state[n], 1, 2)
    bf = lambda a: a.astype(BF16)
    mix_kinds, ffn_kinds = ["col", "blk"], ["blk", "blk", "blk"]
    ffn_names = ("w_gate", "w_up", "w_down")
    groups = {
        "mix_e": ([w_in_e.shape[2], d], mix_kinds, [("w_in_e", 0), ("w_out_e", 0)]),
        "ffn0": ([d, d, d], ffn_kinds, [(n, 0) for n in ffn_names]),
        "mix_o": ([w_in_o.shape[2], d], mix_kinds, [("w_in_o", 0), ("w_out_o", 0)]),
        "ffn1": ([d, d, d], ffn_kinds, [(n, 1) for n in ffn_names]),
    }
    gathers = {
        "in_e": ([bf(w_in_e[0])], ["col"]), "out_e": ([bf(w_out_e[0])], ["blk"]),
        "gu0": ([bf(wts["w_gate"][0]), bf(wts["w_up"][0])], ["blk", "blk"]), "down0": ([bf(w_down[0])], ["blk"]),
        "in_o": ([bf(w_in_o[0])], ["col"]), "out_o": ([bf(w_out_o[0])], ["blk"]),
        "gu1": ([bf(wts["w_gate"][1]), bf(wts["w_up"][1])], ["blk", "blk"]), "down1": ([bf(w_down[1])], ["blk"]),
    }
    started, prev = {}, params_st[3]
    for grp, (shards, kinds) in gathers.items():
        started[grp] = _gather_start("gather_start_" + grp, shards, kinds, prev)
        prev = started[grp][4]
    all_started = prev[0, 0:1]

    bufs = _split_wait("small_params_wait", params_st, _everyone_plan, prev)
    gathered = _own_copy("small_params_own", bufs[0], bufs[1], "blk", me_arr)
    small = {n: wts[n][0] for n in SMALL if n not in SHARDED_SMALL and n not in ("ffn_norm", "final_norm")}
    small["ffn_norm"], small["final_norm"] = ffn_norm, final_norm
    flat, off = gathered.reshape(NDEV, -1), 0
    for n, a in zip(sh_names, sh_local):
        ax, shp = SHARDED_SMALL[n], a.shape
        blocks = jnp.moveaxis(flat[:, off:off + a.size].reshape((NDEV,) + shp), 0, ax)
        small[n] = blocks.reshape(shp[:ax] + (NDEV * shp[ax],) + shp[ax + 1:])
        off += a.size

    passing, shards_of = {}, {}

    def pass_on(grp, after):
        shards, kinds = gathers[grp]
        shards_of[grp], lands = _gather_wait("gather_wait_" + grp, started[grp], kinds, after)
        plan = _forward_plan(kinds, [s.shape[1] for s in shards])
        passing[grp] = (_split_start("forward_start_" + grp, lands, 3 * len(lands), plan, after), plan)
        return passing[grp][0][3]

    def get_w(grp, after):
        if grp not in passing:
            after = pass_on(grp, after)
        st, plan = passing[grp]
        lands = _split_wait("forward_wait_" + grp, st, plan, after)
        full = [_own_copy("own_copy_%s%d" % (grp, a), shards_of[grp][a], lands[a], gathers[grp][1][a], me_arr)
                for a in range(len(lands))]
        return [f.reshape(-1, d) if kind == "blk" else f for f, kind in zip(full, gathers[grp][1])]

    cx, cy, cc = _place()
    place = jnp.stack([cc, 2 * cx + cy]).astype(jnp.int32)
    bwd_order = ["ffn1", "mix_o", "ffn0", "mix_e"]
    pairing, pending, results = {}, {}, {}

    late_names = ["conv_w_e", "mix_norm_e"]
    early_names = [n for n in SMALL if n not in late_names]
    small_sent = {}

    def send_small(tag, arrays, after):
        mine = _pack(arrays)
        small_sent[tag] = _split_start(tag + "_start", [mine, lax.empty((NDEV,) + mine.shape, F32)], NDEV - 1,
                                       _everyone_plan, after)
        return small_sent[tag][3]

    def summed_small(tag, shapes, after):
        bufs = _split_wait(tag + "_wait", small_sent[tag], _everyone_plan, after)
        parts = _own_copy(tag + "_own", bufs[0], bufs[1], "blk", me_arr)
        return _unpack(_sum_parts(tag + "_sum", parts), shapes)

    def put_g(grp, grads):
        if grp == "small":
            small_sent["shapes"] = [grads[n].shape for n in early_names]
            return send_small("small_grads", [grads[n] for n in early_names], place)
        nloc, kinds, _ = groups[grp]
        if len(kinds) == 2:
            fulls = [grads["w_in"], grads["w_out"].reshape(NDEV, -1, d)]
        else:
            fulls = [grads[n].reshape(NDEV, -1, d) for n in ffn_names]
        empties = []
        for g, kind, n in zip(fulls, kinds, nloc):
            empties.append(lax.empty((NCHIPS, g.shape[1] if kind == "blk" else g.shape[0], n), g.dtype))
        plan = _pair_plan(kinds, nloc)
        pairing[grp] = (_split_start("pair_start_" + grp, fulls + empties, NCHIPS * len(fulls), plan, place),
                        plan, kinds, nloc)
        token = pairing[grp][0][3]
        return send_sums(grp, token) if grp == bwd_order[-1] else token

    def send_sums(grp, after):
        st, plan, kinds, nloc = pairing[grp]
        bufs = _split_wait("pair_wait_" + grp, st, plan, after)
        na = len(kinds)
        sums = [_chip_sum("chip_sum_%s%d" % (grp, a), bufs[a], kinds[a], nloc[a], bufs[na + a], place)
                for a in range(na)]
        pending[grp] = _scatter_start("scatter_start_" + grp, [s[0] for s in sums], [s[1] for s in sums], after)
        return pending[grp][4]

    def finish(grp, after):
        lands = _scatter_wait("scatter_wait_" + grp, pending[grp], after)
        dep = None
        for (n, l), parts in zip(groups[grp][2], lands):
            results[n] = _sum_adamw("adamw_%s%d" % (n, l), parts, wts[n], mom[n], var[n], l, results.get(n), dep)
            dep = results[n][1]
        return dep

    fwd_sync = {"fwd_a": ["out_e"], "fwd_b": ["gu0"], "fwd_c": ["down0", "in_o"], "fwd_d": ["out_o"],
                "fwd_e": ["gu1"], "fwd_f": ["down1"]}

    def sync(tag, after):
        if tag in fwd_sync:
            for grp in fwd_sync[tag]:
                after = pass_on(grp, after)
            return after
        if tag == "bwd_mix_o":
            return send_sums("ffn1", after)
        if tag == "bwd_ffn0":
            return finish("ffn1", send_sums("mix_o", after))
        if tag == "bwd_mix_e":
            return finish("mix_o", send_sums("ffn0", after))
        return None

    small["mix_norm_e"] = small["mix_norm_e"] + all_started
    lsum, dx, late = _local_step(x.reshape(t, d), loss_target.reshape(t, d), seq, small, get_w, put_g, sync)
    loss = lax.psum(jnp.sum(lsum), MESH_AXES)

    out_g, out_d, out_m, out_v = {}, {}, {}, {}

    dep = finish("ffn0", send_small("last_grads", [late[n] for n in late_names], dx))
    sums = dict(zip(early_names, summed_small("small_grads", small_sent["shapes"], dep)))
    sums.update(zip(late_names, summed_small("last_grads", [late[n].shape for n in late_names], dep)))
    gs_sum = [sums[n] for n in SMALL]
    local_g = []
    for n, g in zip(SMALL, gs_sum):
        if n in SHARDED_SMALL:
            ax = SHARDED_SMALL[n]
            size = wts[n].shape[ax + 1]
            g = lax.dynamic_slice_in_dim(g, me * size, size, axis=ax)
        local_g.append(g.reshape(wts[n].shape))
    shapes = [wts[n].shape for n in SMALL]
    upd = _adamw("adamw_small", _pack([wts[n] for n in SMALL]), _pack(local_g),
                 _pack([mom[n] for n in SMALL]), _pack([var[n] for n in SMALL]))
    for i, outd in enumerate((out_d, out_m, out_v)):
        for n, a in zip(SMALL, _unpack(upd[i], shapes)):
            outd[n] = a
    for n, g in zip(SMALL, local_g):
        out_g[n] = g

    finish("mix_e", upd[0])
    for n in BIG:
        res = [jnp.swapaxes(a, 1, 2) for a in results[n]] if n in ("w_gate", "w_up") else results[n]
        out_g[n], out_d[n], out_m[n], out_v[n] = res

    return (loss, dx.reshape(bsz, seq, d), *[out_g[n] for n in WEIGHTS], *[out_d[n] for n in WEIGHTS],
            *[out_m[n] for n in WEIGHTS], *[out_v[n] for n in WEIGHTS])
```

```python
import jax
import jax.numpy as jnp
from jax import lax
from jax.experimental import pallas as pl
from jax.experimental.pallas import tpu as pltpu

F32 = jnp.float32
BF16 = jnp.bfloat16
NDEV = 8
MESH_AXES = ("x", "y", "c")
EPS = 1e-6
POOL_WINDOWS = (2, 4, 8, 16)
ADAM_LR = 0.001
ADAM_B1 = 0.9
ADAM_B2 = 0.999
ADAM_EPS = 1e-08
ADAM_WD = 0.01
ADAM_STEP = 10
LANES = 128
SUBLANES = 8
VMEM_LIMIT = 56 * 1024 * 1024
MXU_DEPTH = 256
MM_TK = 2816
MESH = pl.DeviceIdType.MESH
ANY = pl.BlockSpec(memory_space=pl.ANY)


def _cp(*sem):
    return pltpu.CompilerParams(dimension_semantics=sem, vmem_limit_bytes=VMEM_LIMIT)


def _tile(n, pref, unit=LANES):
    if n <= pref:
        return n
    t = (pref // unit) * unit
    while t > unit and n % t:
        t -= unit
    assert n % t == 0, (n, pref)
    return t


def _sigmoid(v):
    return 0.5 * jnp.tanh(0.5 * v) + 0.5


def _mm(name, pairs, a_specs, b_specs, dims, out_shape, o_spec, grid, acc_shape,
        res=None, res_spec=None, dep=None):
    np_ = len(pairs)
    nk = grid[2]
    has_res = res is not None
    n_in = 2 * np_ + (1 if has_res else 0) + (0 if dep is None else 1)

    def body(*refs):
        a_refs = refs[:np_]
        b_refs = refs[np_:2 * np_]
        r_ref = refs[2 * np_] if has_res else None
        o_ref = refs[n_in]
        acc = refs[-1]

        def part():
            s = None
            for a_ref, b_ref in zip(a_refs, b_refs):
                d = lax.dot_general(a_ref[...], b_ref[...], dims, preferred_element_type=F32)
                s = d if s is None else s + d
            return s

        def finish(v):
            if has_res:
                v = v + r_ref[...]
            o_ref[...] = v.astype(o_ref.dtype)

        if nk == 1:
            finish(part())
        else:
            k = pl.program_id(2)

            @pl.when(k == 0)
            def _():
                acc[...] = part()

            @pl.when((k > 0) & (k < nk - 1))
            def _():
                acc[...] += part()

            @pl.when(k == nk - 1)
            def _():
                finish(acc[...] + part())

    ins = [p[0] for p in pairs] + [p[1] for p in pairs]
    specs = list(a_specs) + list(b_specs)
    if has_res:
        ins.append(res)
        specs.append(res_spec)
    if dep is not None:
        ins.append(dep)
        specs.append(ANY)
    return pl.pallas_call(
        body, name=name, grid=grid, in_specs=specs, out_specs=o_spec, out_shape=out_shape,
        scratch_shapes=[pltpu.VMEM(acc_shape if nk > 1 else (SUBLANES, LANES), F32)],
        compiler_params=_cp("parallel", "parallel", "arbitrary"))(*ins)


NN = (((1,), (0,)), ((), ()))
NT = (((1,), (1,)), ((), ()))
TN = (((0,), (0,)), ((), ()))


def _tiles_mk(m, kk):
    return _tile(m, 1024), _tile(kk, MM_TK, MXU_DEPTH)


def _mm_nn(name, a, b, out_dtype, res=None, dep=None):
    pairs = list(zip(a, b)) if isinstance(a, (list, tuple)) else [(a, b)]
    m, kk = pairs[0][0].shape
    n = pairs[0][1].shape[1]
    tm, tk = _tiles_mk(m, kk)
    tn = _tile(n, 1024 if tk * len(pairs) <= MM_TK else 512)
    return _mm(name, pairs,
               [pl.BlockSpec((tm, tk), lambda i, j, k: (i, k))] * len(pairs),
               [pl.BlockSpec((tk, tn), lambda i, j, k: (k, j))] * len(pairs), NN,
               jax.ShapeDtypeStruct((m, n), out_dtype),
               pl.BlockSpec((tm, tn), lambda i, j, k: (i, j)),
               (m // tm, n // tn, kk // tk), (tm, tn), res,
               pl.BlockSpec((tm, tn), lambda i, j, k: (i, j)), dep=dep)


def _mm_nt(name, a, b, out_dtype, dep=None):
    m, n = a.shape
    kk = b.shape[0]
    tn = _tile(kk, 1024)
    tm, tk = _tiles_mk(m, n)
    return _mm(name, [(a, b)],
               [pl.BlockSpec((tm, tk), lambda i, j, k: (i, k))],
               [pl.BlockSpec((tn, tk), lambda i, j, k: (j, k))], NT,
               jax.ShapeDtypeStruct((m, kk), out_dtype),
               pl.BlockSpec((tm, tn), lambda i, j, k: (i, j)),
               (m // tm, kk // tn, n // tk), (tm, tn), dep=dep)


def _mm_tn(name, a, b, out_dtype, dep=None):
    t, m = a.shape
    n = b.shape[1]
    tn = _tile(n, 1024)
    tm, tk = _tile(m, 1408), _tile(t, MM_TK, MXU_DEPTH)
    return _mm(name, [(a, b)],
               [pl.BlockSpec((tk, tm), lambda i, j, k: (k, i))],
               [pl.BlockSpec((tk, tn), lambda i, j, k: (k, j))], TN,
               jax.ShapeDtypeStruct((m, n), out_dtype),
               pl.BlockSpec((tm, tn), lambda i, j, k: (i, j)),
               (m // tm, n // tn, t // tk), (tm, tn), dep=dep)


def _ffn_fwd(name, n, wg, wu):
    f, d = wg.shape
    t = n.shape[0]
    tm, tn = _tile(t, 1024), _tile(f, 512)

    def body(n_ref, wg_ref, wu_ref, act_ref, ds_ref, s_ref):
        nv = n_ref[...]
        g = lax.dot_general(nv, wg_ref[...], NT, preferred_element_type=F32)
        up = lax.dot_general(nv, wu_ref[...], NT, preferred_element_type=F32)
        sg = _sigmoid(g)
        silu = g * sg
        act_ref[...] = (silu * up).astype(BF16)
        ds_ref[...] = (up * (sg * (1.0 + g * (1.0 - sg)))).astype(BF16)
        s_ref[...] = silu.astype(BF16)

    w_spec = pl.BlockSpec((tn, d), lambda j, i: (j, 0))
    o_spec = pl.BlockSpec((tm, tn), lambda j, i: (i, j))
    shp = jax.ShapeDtypeStruct((t, f), BF16)
    return pl.pallas_call(
        body, name=name, grid=(f // tn, t // tm),
        in_specs=[pl.BlockSpec((tm, d), lambda j, i: (i, 0)), w_spec, w_spec],
        out_specs=[o_spec, o_spec, o_spec], out_shape=[shp, shp, shp],
        compiler_params=_cp("parallel", "parallel"))(n, wg, wu)


def _ffn_bwd_act(name, dh, wd, dsilu, silu, dep=None):
    f, d = wd.shape
    t = dh.shape[0]
    tm, tn = _tile(t, 1024), _tile(f, 512)

    def body(dh_ref, wd_ref, ds_ref, s_ref, *rest):
        dg_ref, dup_ref = rest[-2:]
        da = lax.dot_general(dh_ref[...], wd_ref[...], NT, preferred_element_type=F32)
        dg_ref[...] = (da * ds_ref[...].astype(F32)).astype(BF16)
        dup_ref[...] = (da * s_ref[...].astype(F32)).astype(BF16)

    o_spec = pl.BlockSpec((tm, tn), lambda i, j: (i, j))
    shp = jax.ShapeDtypeStruct((t, f), BF16)
    return pl.pallas_call(
        body, name=name, grid=(t // tm, f // tn),
        in_specs=[pl.BlockSpec((tm, d), lambda i, j: (i, 0)),
                  pl.BlockSpec((tn, d), lambda i, j: (j, 0)), o_spec, o_spec]
        + ([] if dep is None else [ANY]),
        out_specs=[o_spec, o_spec], out_shape=[shp, shp],
        compiler_params=_cp("parallel", "parallel"))(dh, wd, dsilu, silu, *([] if dep is None else [dep]))


def _rms_fwd(name, h, gain):
    t, d = h.shape
    tr = _tile(t, 512, SUBLANES)

    def body(h_ref, g_ref, n_ref):
        hv = h_ref[...]
        r = lax.rsqrt(jnp.mean(hv * hv, axis=-1, keepdims=True) + EPS)
        n_ref[...] = (hv * r * g_ref[...]).astype(BF16)

    return pl.pallas_call(
        body, name=name, grid=(t // tr,),
        in_specs=[pl.BlockSpec((tr, d), lambda i: (i, 0)), pl.BlockSpec((1, d), lambda i: (0, 0))],
        out_specs=pl.BlockSpec((tr, d), lambda i: (i, 0)),
        out_shape=jax.ShapeDtypeStruct((t, d), BF16),
        compiler_params=_cp("parallel"))(h, gain)


def _rms_bwd_math(hv, gain, dn):
    d = hv.shape[-1]
    r = lax.rsqrt(jnp.mean(hv * hv, axis=-1, keepdims=True) + EPS)
    xhat = hv * r
    dxh = dn * gain
    dh = r * (dxh - xhat * (jnp.sum(dxh * xhat, axis=-1, keepdims=True) / d))
    dgain = jnp.sum(dn * xhat, axis=0, keepdims=True)
    return dh, dgain


def _rms_bwd(name, h, gain, dn, dres):
    t, d = h.shape
    tr = _tile(t, 256, SUBLANES)

    def body(h_ref, g_ref, dn_ref, dr_ref, dh_ref, dhb_ref, dg_ref):
        dh, dgain = _rms_bwd_math(h_ref[...], g_ref[...], dn_ref[...].astype(F32))
        dh = dh + dr_ref[...]
        dh_ref[...] = dh
        dhb_ref[...] = dh.astype(BF16)

        @pl.when(pl.program_id(0) == 0)
        def _():
            dg_ref[...] = dgain

        @pl.when(pl.program_id(0) > 0)
        def _():
            dg_ref[...] += dgain

    row = pl.BlockSpec((tr, d), lambda i: (i, 0))
    vec = pl.BlockSpec((1, d), lambda i: (0, 0))
    return pl.pallas_call(
        body, name=name, grid=(t // tr,), in_specs=[row, vec, row, row],
        out_specs=[row, row, vec],
        out_shape=[jax.ShapeDtypeStruct((t, d), F32), jax.ShapeDtypeStruct((t, d), BF16),
                   jax.ShapeDtypeStruct((1, d), F32)],
        compiler_params=_cp("arbitrary"))(h, gain, dn, dres)


def _loss_head(name, h, gain, tgt):
    t, d = h.shape
    tr = _tile(t, 256, SUBLANES)

    def body(h_ref, g_ref, t_ref, dh_ref, dhb_ref, dg_ref, ls_ref):
        hv = h_ref[...]
        gv = g_ref[...]
        r = lax.rsqrt(jnp.mean(hv * hv, axis=-1, keepdims=True) + EPS)
        err = hv * r * gv - t_ref[...]
        lsum = 0.5 * jnp.sum(err * err, axis=0, keepdims=True) / d
        dh, dgain = _rms_bwd_math(hv, gv, err / d)
        dh_ref[...] = dh
        dhb_ref[...] = dh.astype(BF16)

        @pl.when(pl.program_id(0) == 0)
        def _():
            dg_ref[...] = dgain
            ls_ref[...] = lsum

        @pl.when(pl.program_id(0) > 0)
        def _():
            dg_ref[...] += dgain
            ls_ref[...] += lsum

    row = pl.BlockSpec((tr, d), lambda i: (i, 0))
    vec = pl.BlockSpec((1, d), lambda i: (0, 0))
    return pl.pallas_call(
        body, name=name, grid=(t // tr,), in_specs=[row, vec, row],
        out_specs=[row, row, vec, vec],
        out_shape=[jax.ShapeDtypeStruct((t, d), F32), jax.ShapeDtypeStruct((t, d), BF16),
                   jax.ShapeDtypeStruct((1, d), F32), jax.ShapeDtypeStruct((1, d), F32)],
        compiler_params=_cp("arbitrary"))(h, gain, tgt)


def _conv_geom(t, seq, c, k, full_width=False):
    halo = 32 if k - 1 > SUBLANES else SUBLANES
    assert k - 1 <= halo
    tm = min(256 if halo > SUBLANES else 1024, seq // 2)
    tc = c if full_width else min(512, c)
    assert seq % tm == 0 and tm % halo == 0 and c % tc == 0 and t % seq == 0
    return halo, tm, tc, min(64 if halo > SUBLANES else 128, tm), min(LANES, tc)


def _pre(kind, a, b):
    if kind == "glu":
        return a * _sigmoid(b)
    if kind == "mul":
        return a * b
    return a


def _taps(k):
    return sorted((s % SUBLANES, s // SUBLANES, s) for s in range(k))


def _conv_fwd(name, seq, c, w, x1, c1, x2=None, c2=0, pre=None, bias=None, post=None, cpost=0, live=None):
    t = x1.shape[0]
    k = w.shape[0]
    halo, tm, tc, sr, sl = _conv_geom(t, seq, c, k, live is not None)
    nb, cps = tm // halo, seq // tm
    two = x2 is not None
    has_bias, has_post = bias is not None, post is not None

    def body(*refs):
        it = iter(refs)
        x1c, x1h = next(it), next(it)
        x2c, x2h = (next(it), next(it)) if two else (None, None)
        w_ref = next(it)
        b_ref = next(it) if has_bias else None
        p_ref = next(it) if has_post else None
        o_ref = next(it)
        y_ref = next(it) if has_post else None
        xs = next(it)
        first = (pl.program_id(1) % cps) == 0
        hv = _pre(pre, x1h[...].astype(F32), x2h[...].astype(F32) if two else None)
        xs[0:halo, :] = jnp.where(first, 0.0, hv)
        xs[halo:halo + tm, :] = _pre(pre, x1c[...].astype(F32), x2c[...].astype(F32) if two else None)
        for l0 in range(0, tc, sl):
            ls = slice(l0, l0 + sl)
            for r0 in range(0, tm, sr):
                win = xs[r0:r0 + sr + halo, ls]
                acc = jnp.zeros((sr, sl), F32)
                rolled = {}
                for r, q, s in _taps(k if live is None else live[l0 // sl]):
                    if r not in rolled:
                        rolled[r] = win if r == 0 else pltpu.roll(win, r, 0)
                    lo = halo - SUBLANES * q
                    acc = acc + w_ref[k - 1 - s:k - s, ls] * rolled[r][lo:lo + sr]
                if has_bias:
                    acc = acc + b_ref[:, ls]
                o_ref[r0:r0 + sr, ls] = acc.astype(o_ref.dtype)
                if has_post:
                    y_ref[r0:r0 + sr, ls] = (acc * p_ref[r0:r0 + sr, ls].astype(F32)).astype(y_ref.dtype)

    def cur(off):
        return pl.BlockSpec((tm, tc), lambda j, i: (i, off // tc + j))

    def prev(off):
        return pl.BlockSpec((halo, tc), lambda j, i: (jnp.maximum(i * nb - 1, 0), off // tc + j))

    ins, specs = [x1, x1], [cur(c1), prev(c1)]
    if two:
        ins += [x2, x2]
        specs += [cur(c2), prev(c2)]
    ins.append(w)
    specs.append(pl.BlockSpec((k, tc), lambda j, i: (0, j)))
    if has_bias:
        ins.append(bias)
        specs.append(pl.BlockSpec((1, tc), lambda j, i: (0, j)))
    if has_post:
        ins.append(post)
        specs.append(cur(cpost))
    o_spec = pl.BlockSpec((tm, tc), lambda j, i: (i, j))
    shp = jax.ShapeDtypeStruct((t, c), BF16)
    return pl.pallas_call(
        body, name=name, grid=(c // tc, t // tm), in_specs=specs,
        out_specs=[o_spec, o_spec] if has_post else o_spec,
        out_shape=[shp, shp] if has_post else shp,
        scratch_shapes=[pltpu.VMEM((halo + tm, tc), F32)],
        compiler_params=_cp("parallel", "parallel"))(*ins)


def _conv_bwd(name, seq, c, w, d1, cd1, d2=None, cd2=0, dpre=None,
              x1=None, c1=0, x2=None, c2=0, pre=None, live=None, dep=None):
    t = d1.shape[0]
    k = w.shape[0]
    assert live is None or x1 is None
    halo, tm, tc, sr, sl = _conv_geom(t, seq, c, k, live is not None)
    nb, cps = tm // halo, seq // tm
    nchunks = t // tm
    dtwo, xtwo, has_x = d2 is not None, x2 is not None, x1 is not None

    def body(*refs):
        it = iter(refs)
        d1c, d1n = next(it), next(it)
        d2c, d2n = (next(it), next(it)) if dtwo else (None, None)
        x1c, x1h = (next(it), next(it)) if has_x else (None, None)
        x2c, x2h = (next(it), next(it)) if xtwo else (None, None)
        w_ref = next(it)
        if dep is not None:
            next(it)
        dx_ref = next(it)
        dw_ref = next(it) if has_x else None
        ds = next(it)
        xs = next(it) if has_x else None
        i = pl.program_id(1)
        last = (i % cps) == cps - 1
        ds[0:tm, :] = _pre(dpre, d1c[...].astype(F32), d2c[...].astype(F32) if dtwo else None)
        nv = _pre(dpre, d1n[...].astype(F32), d2n[...].astype(F32) if dtwo else None)
        ds[tm:tm + halo, :] = jnp.where(last, 0.0, nv)
        if has_x:
            first = (i % cps) == 0
            hv = _pre(pre, x1h[...].astype(F32), x2h[...].astype(F32) if xtwo else None)
            xs[0:halo, :] = jnp.where(first, 0.0, hv)
            xs[halo:halo + tm, :] = _pre(pre, x1c[...].astype(F32), x2c[...].astype(F32) if xtwo else None)

            @pl.when(i == 0)
            def _():
                dw_ref[...] = jnp.zeros_like(dw_ref)

        for l0 in range(0, tc, sl):
            ls = slice(l0, l0 + sl)
            for r0 in range(0, tm, sr):
                win = ds[r0:r0 + sr + halo, ls]
                nrow = sr + halo
                acc = jnp.zeros((sr, sl), F32)
                rolled = {}
                for r, q, s in _taps(k if live is None else live[l0 // sl]):
                    if r not in rolled:
                        rolled[r] = win if r == 0 else pltpu.roll(win, nrow - r, 0)
                    lo = SUBLANES * q
                    acc = acc + w_ref[k - 1 - s:k - s, ls] * rolled[r][lo:lo + sr]
                dx_ref[r0:r0 + sr, ls] = acc.astype(dx_ref.dtype)
                if has_x:
                    dcur = win[0:sr]
                    xwin = xs[r0:r0 + sr + halo, ls]
                    xrolled = {}
                    for r, q, s in _taps(k):
                        if r not in xrolled:
                            xrolled[r] = xwin if r == 0 else pltpu.roll(xwin, r, 0)
                        lo = halo - SUBLANES * q
                        part = jnp.sum(dcur * xrolled[r][lo:lo + sr], axis=0, keepdims=True)
                        dw_ref[k - 1 - s:k - s, ls] += part

    def cur(off):
        return pl.BlockSpec((tm, tc), lambda j, i: (i, off // tc + j))

    def prev(off):
        return pl.BlockSpec((halo, tc), lambda j, i: (jnp.maximum(i * nb - 1, 0), off // tc + j))

    def nxt(off):
        return pl.BlockSpec((halo, tc),
                            lambda j, i: (jnp.minimum((i + 1) * nb, nchunks * nb - 1), off // tc + j))

    ins, specs = [d1, d1], [cur(cd1), nxt(cd1)]
    if dtwo:
        ins += [d2, d2]
        specs += [cur(cd2), nxt(cd2)]
    if has_x:
        ins += [x1, x1]
        specs += [cur(c1), prev(c1)]
    if xtwo:
        ins += [x2, x2]
        specs += [cur(c2), prev(c2)]
    ins.append(w)
    specs.append(pl.BlockSpec((k, tc), lambda j, i: (0, j)))
    if dep is not None:
        ins.append(dep)
        specs.append(ANY)
    o_specs = [pl.BlockSpec((tm, tc), lambda j, i: (i, j))]
    o_shapes = [jax.ShapeDtypeStruct((t, c), BF16)]
    scratch = [pltpu.VMEM((tm + halo, tc), F32)]
    if has_x:
        o_specs.append(pl.BlockSpec((k, tc), lambda j, i: (0, j)))
        o_shapes.append(jax.ShapeDtypeStruct((k, c), F32))
        scratch.append(pltpu.VMEM((halo + tm, tc), F32))
    out = pl.pallas_call(
        body, name=name, grid=(c // tc, t // tm), in_specs=specs, out_specs=o_specs,
        out_shape=o_shapes, scratch_shapes=scratch,
        compiler_params=_cp("parallel", "arbitrary"))(*ins)
    return out if has_x else out[0]


def _pool_taps(c):
    kmax = max(POOL_WINDOWS)
    grp = c // len(POOL_WINDOWS)
    cols = []
    for wdw in POOL_WINDOWS:
        col = jnp.concatenate([jnp.zeros((kmax - wdw,), F32), jnp.ones((wdw,), F32)])
        cols.append(jnp.tile(col[:, None], (1, grp)))
    return jnp.concatenate(cols, axis=1)


def _pool_live(c):
    grp, sl = c // len(POOL_WINDOWS), min(LANES, c)
    return tuple(max(POOL_WINDOWS[g] for g in range(l0 // grp, (l0 + sl - 1) // grp + 1))
                 for l0 in range(0, c, sl))


def _counts(i, tr, seq, grp):
    pos = (i * tr + lax.broadcasted_iota(jnp.int32, (tr, 1), 0)) % seq + 1
    return [1.0 / jnp.minimum(pos, wdw).astype(F32) for wdw in POOL_WINDOWS]


def _ln_stats(a2):
    mu = jnp.mean(a2, axis=-1, keepdims=True)
    xc = a2 - mu
    rstd = lax.rsqrt(jnp.mean(xc * xc, axis=-1, keepdims=True) + EPS)
    return xc * rstd, rstd


def _even_fwd(name, seq, a2, ws, u, ln_g, ln_b, w_pool, scale):
    t, c = a2.shape
    ng = len(POOL_WINDOWS)
    grp = c // ng
    tr = _tile(t, 256, SUBLANES)

    def body(a_ref, ws_ref, b_ref, g_ref, bb_ref, wp_ref, sc_ref, z_ref, pm_ref):
        xhat, _ = _ln_stats(a_ref[...].astype(F32))
        l = xhat * g_ref[...] + bb_ref[...]
        z_ref[:, 0:c] = (l * _sigmoid(l)).astype(BF16)
        inv = _counts(pl.program_id(0), tr, seq, grp)
        for g in range(ng):
            gs = slice(g * grp, (g + 1) * grp)
            pm = (ws_ref[:, gs].astype(F32) * inv[g] - b_ref[:, gs].astype(F32)).astype(BF16)
            pm_ref[:, gs] = pm
            q = jnp.dot(pm, wp_ref[g], preferred_element_type=F32)
            z_ref[:, c + g * grp:c + (g + 1) * grp] = (q * sc_ref[:, gs]).astype(BF16)

    row = pl.BlockSpec((tr, c), lambda i: (i, 0))
    vec = pl.BlockSpec((1, c), lambda i: (0, 0))
    return pl.pallas_call(
        body, name=name, grid=(t // tr,),
        in_specs=[row, row, pl.BlockSpec((tr, c), lambda i: (i, 2)), vec, vec,
                  pl.BlockSpec((ng, grp, grp), lambda i: (0, 0, 0)), vec],
        out_specs=[pl.BlockSpec((tr, 2 * c), lambda i: (i, 0)), row],
        out_shape=[jax.ShapeDtypeStruct((t, 2 * c), BF16), jax.ShapeDtypeStruct((t, c), BF16)],
        compiler_params=_cp("parallel"))(a2, ws, u, ln_g, ln_b, w_pool, scale)


def _even_bwd(name, seq, dz, a2, pm, ln_g, ln_b, w_pool, scale):
    t, c = a2.shape
    ng = len(POOL_WINDOWS)
    grp = c // ng
    tr = _tile(t, 256, SUBLANES)

    def body(dz_ref, a_ref, pm_ref, g_ref, bb_ref, wp_ref, sc_ref,
             da_ref, dws_ref, dpm_ref, vec_ref, dwp_ref):
        i = pl.program_id(0)

        @pl.when(i == 0)
        def _():
            vec_ref[...] = jnp.zeros_like(vec_ref)
            dwp_ref[...] = jnp.zeros_like(dwp_ref)

        xhat, rstd = _ln_stats(a_ref[...].astype(F32))
        gv = g_ref[...]
        l = xhat * gv + bb_ref[...]
        sg = _sigmoid(l)
        dl = dz_ref[:, 0:c].astype(F32) * (sg * (1.0 + l * (1.0 - sg)))
        dxh = dl * gv
        da2 = rstd * (dxh - jnp.mean(dxh, axis=-1, keepdims=True)
                      - xhat * jnp.mean(dxh * xhat, axis=-1, keepdims=True))
        da_ref[...] = da2.astype(BF16)
        vec_ref[0:1, :] += jnp.sum(dl * xhat, axis=0, keepdims=True)
        vec_ref[1:2, :] += jnp.sum(dl, axis=0, keepdims=True)
        vec_ref[2:3, :] += jnp.sum(da2, axis=0, keepdims=True)
        inv = _counts(i, tr, seq, grp)
        for g in range(ng):
            gs = slice(g * grp, (g + 1) * grp)
            pmv = pm_ref[:, gs]
            wp = wp_ref[g]
            dp = dz_ref[:, c + g * grp:c + (g + 1) * grp].astype(F32)
            q = jnp.dot(pmv, wp, preferred_element_type=F32)
            vec_ref[3:4, gs] += jnp.sum(dp * q, axis=0, keepdims=True)
            dq = (dp * sc_ref[:, gs]).astype(BF16)
            dpm = lax.dot_general(dq, wp, NT, preferred_element_type=F32)
            dwp_ref[g] += lax.dot_general(pmv, dq, TN, preferred_element_type=F32)
            dpm_ref[:, gs] = dpm.astype(BF16)
            dws_ref[:, gs] = (dpm * inv[g]).astype(BF16)

    row = pl.BlockSpec((tr, c), lambda i: (i, 0))
    vec = pl.BlockSpec((1, c), lambda i: (0, 0))
    rshape = jax.ShapeDtypeStruct((t, c), BF16)
    return pl.pallas_call(
        body, name=name, grid=(t // tr,),
        in_specs=[pl.BlockSpec((tr, 2 * c), lambda i: (i, 0)), row, row, vec, vec,
                  pl.BlockSpec((ng, grp, grp), lambda i: (0, 0, 0)), vec],
        out_specs=[row, row, row, pl.BlockSpec((SUBLANES, c), lambda i: (0, 0)),
                   pl.BlockSpec((ng, grp, grp), lambda i: (0, 0, 0))],
        out_shape=[rshape, rshape, rshape, jax.ShapeDtypeStruct((SUBLANES, c), F32),
                   jax.ShapeDtypeStruct((ng, grp, grp), F32)],
        compiler_params=_cp("arbitrary"))(dz, a2, pm, ln_g, ln_b, w_pool, scale)


def _even_du(name, u, da1, dbp, dpm):
    t, c = da1.shape
    tr = _tile(t, 256, SUBLANES)

    def body(u_ref, da_ref, dbp_ref, dpm_ref, du_ref):
        val = u_ref[:, 0:c].astype(F32)
        sg = _sigmoid(u_ref[:, c:2 * c].astype(F32))
        da = da_ref[...].astype(F32)
        du_ref[:, 0:c] = (da * sg).astype(BF16)
        du_ref[:, c:2 * c] = (da * val * sg * (1.0 - sg)).astype(BF16)
        du_ref[:, 2 * c:3 * c] = (dbp_ref[...].astype(F32) - dpm_ref[...].astype(F32)).astype(BF16)

    row = pl.BlockSpec((tr, c), lambda i: (i, 0))
    wide = pl.BlockSpec((tr, 3 * c), lambda i: (i, 0))
    return pl.pallas_call(
        body, name=name, grid=(t // tr,), in_specs=[wide, row, row, row], out_specs=wide,
        out_shape=jax.ShapeDtypeStruct((t, 3 * c), BF16),
        compiler_params=_cp("parallel"))(u, da1, dbp, dpm)


def _odd_du(name, u, dy, co, dxc):
    t, c = dy.shape
    tr = _tile(t, 256, SUBLANES)

    def body(u_ref, dy_ref, co_ref, dx_ref, du_ref):
        dx = dx_ref[...].astype(F32)
        du_ref[:, 0:c] = (dy_ref[...].astype(F32) * co_ref[...].astype(F32)).astype(BF16)
        du_ref[:, c:2 * c] = (dx * u_ref[:, 2 * c:3 * c].astype(F32)).astype(BF16)
        du_ref[:, 2 * c:3 * c] = (dx * u_ref[:, c:2 * c].astype(F32)).astype(BF16)

    row = pl.BlockSpec((tr, c), lambda i: (i, 0))
    wide = pl.BlockSpec((tr, 3 * c), lambda i: (i, 0))
    return pl.pallas_call(
        body, name=name, grid=(t // tr,), in_specs=[wide, row, row, row], out_specs=wide,
        out_shape=jax.ShapeDtypeStruct((t, 3 * c), BF16),
        compiler_params=_cp("parallel"))(u, dy, co, dxc)


def _local_step(x, tgt, seq, small, get_w, put_g, sync):
    t, d = x.shape
    c = d // 2
    cw_e, cw_o = small["conv_w_e"], small["conv_w_o"]
    wp = small["w_pool_e"].astype(BF16)
    ptaps = _pool_taps(c)
    row = lambda v: v.reshape(1, -1)

    we = {"w_in": get_w("in_e", x)[0]}
    n0 = _rms_fwd("rms_fwd_mix0", x, row(small["mix_norm_e"]))
    u0 = _mm_nn("mm_in_e", n0, we["w_in"], BF16)
    sync("fwd_a", u0)
    a2 = _conv_fwd("conv_e_fwd", seq, c, cw_e, u0, 0, u0, c, "glu", bias=row(small["conv_b_e"]))
    ws = _conv_fwd("pool_fwd", seq, c, ptaps, u0, 2 * c, live=_pool_live(c))
    z0, pm = _even_fwd("even_fwd", seq, a2, ws, u0, row(small["ln_g_e"]), row(small["ln_b_e"]),
                       wp, row(small["pool_scale_e"]))
    we["w_out"] = get_w("out_e", z0)[0]
    h1 = _mm_nn("mm_out_e", z0, we["w_out"], F32, res=x)
    sync("fwd_b", h1)
    n1 = _rms_fwd("rms_fwd_ffn0", h1, row(small["ffn_norm"][0]))
    wf0 = dict(zip(("w_gate", "w_up"), get_w("gu0", n1)))
    act0, ds0, s0 = _ffn_fwd("ffn0_fwd", n1, wf0["w_gate"], wf0["w_up"])
    dep = sync("fwd_c", act0)
    wf0["w_down"] = get_w("down0", act0)[0]
    h2 = _mm_nn("mm_down0", act0, wf0["w_down"], F32, res=h1, dep=dep)
    dep = sync("fwd_d", h2)
    n2 = _rms_fwd("rms_fwd_mix1", h2, row(small["mix_norm_o"]))
    wo = {"w_in": get_w("in_o", n2)[0]}
    u1 = _mm_nn("mm_in_o", n2, wo["w_in"], BF16, dep=dep)
    co, y1 = _conv_fwd("conv_o_fwd", seq, d, cw_o, u1, d, u1, 2 * d, "mul", post=u1, cpost=0)
    dep = sync("fwd_e", y1)
    wo["w_out"] = get_w("out_o", y1)[0]
    h3 = _mm_nn("mm_out_o", y1, wo["w_out"], F32, res=h2, dep=dep)
    sync("fwd_f", h3)
    n3 = _rms_fwd("rms_fwd_ffn1", h3, row(small["ffn_norm"][1]))
    wf1 = dict(zip(("w_gate", "w_up"), get_w("gu1", n3)))
    act1, ds1, s1 = _ffn_fwd("ffn1_fwd", n3, wf1["w_gate"], wf1["w_up"])
    wf1["w_down"] = get_w("down1", act1)[0]
    h4 = _mm_nn("mm_down1", act1, wf1["w_down"], F32, res=h3)

    dh4, dh4b, d_final, lsum = _loss_head("loss_head", h4, row(small["final_norm"]), tgt)

    def ffn_bwd(tag, dh, dhb, h_in, gain, n, dsilu, silu, act, w, dep):
        dg, dup = _ffn_bwd_act("ffn%s_bwd_act" % tag, dhb, w["w_down"], dsilu, silu, dep=dep)
        dwd = _mm_tn("mm_dwd%s" % tag, act, dhb, BF16)
        dwg = _mm_tn("mm_dwg%s" % tag, dg, n, BF16, dep=sync("bwd_ffn" + tag, dwd))
        dwu = _mm_tn("mm_dwu%s" % tag, dup, n, BF16)
        dn = _mm_nn("mm_ffn_dn%s" % tag, [dg, dup], [w["w_gate"], w["w_up"]], BF16)
        dh_in, dhb_in, dgain = _rms_bwd("rms_bwd_ffn%s" % tag, h_in, gain, dn, dh)
        dep = put_g("ffn" + tag, {"w_gate": dwg, "w_up": dwu, "w_down": dwd})
        return dh_in, dhb_in, dgain, dep

    dh3, dh3b, d_ffn1, dep = ffn_bwd("1", dh4, dh4b, h3, row(small["ffn_norm"][1]), n3, ds1, s1,
                                     act1, wf1, None)
    dw_out_o = _mm_tn("mm_dw_out_o", y1, dh3b, BF16, dep=dep)
    dy1 = _mm_nt("mm_dy_o", dh3b, wo["w_out"], BF16, dep=sync("bwd_mix_o", dw_out_o))
    dxc, dcw_o = _conv_bwd("conv_o_bwd", seq, d, cw_o, dy1, 0, u1, 0, "mul",
                           x1=u1, c1=d, x2=u1, c2=2 * d, pre="mul")
    du1 = _odd_du("odd_du", u1, dy1, co, dxc)
    dw_in_o = _mm_tn("mm_dw_in_o", n2, du1, BF16)
    dn2 = _mm_nt("mm_dn_o", du1, wo["w_in"], BF16)
    dh2, dh2b, d_mix_o = _rms_bwd("rms_bwd_mix1", h2, row(small["mix_norm_o"]), dn2, dh3)
    dep = put_g("mix_o", {"w_in": dw_in_o, "w_out": dw_out_o})

    dh1, dh1b, d_ffn0, dep = ffn_bwd("0", dh2, dh2b, h1, row(small["ffn_norm"][0]), n1, ds0, s0,
                                     act0, wf0, dep)
    dw_out_e = _mm_tn("mm_dw_out_e", z0, dh1b, BF16, dep=dep)
    dz0 = _mm_nt("mm_dz_e", dh1b, we["w_out"], BF16, dep=sync("bwd_mix_e", dw_out_e))
    da2, dws, dpm, vecs, dwp = _even_bwd("even_bwd", seq, dz0, a2, pm, row(small["ln_g_e"]),
                                         row(small["ln_b_e"]), wp, row(small["pool_scale_e"]))
    dep = put_g("small", {"conv_b_e": vecs[2], "ln_g_e": vecs[0], "ln_b_e": vecs[1],
                          "w_pool_e": dwp, "pool_scale_e": vecs[3], "mix_norm_o": d_mix_o[0],
                          "conv_w_o": dcw_o, "ffn_norm": jnp.concatenate([d_ffn0, d_ffn1], axis=0),
                          "final_norm": d_final[0]})
    da1, dcw_e = _conv_bwd("conv_e_bwd", seq, c, cw_e, da2, 0, x1=u0, c1=0, x2=u0, c2=c, pre="glu", dep=dep)
    dbp = _conv_bwd("pool_bwd", seq, c, ptaps, dws, 0, live=_pool_live(c))
    du0 = _even_du("even_du", u0, da1, dbp, dpm)
    dw_in_e = _mm_tn("mm_dw_in_e", n0, du0, BF16)
    dep = put_g("mix_e", {"w_in": dw_in_e, "w_out": dw_out_e})
    dn0 = _mm_nt("mm_dn_e", du0, we["w_in"], BF16, dep=dep)
    dx, _, d_mix_e = _rms_bwd("rms_bwd_mix0", x, row(small["mix_norm_e"]), dn0, dh1)
    return lsum, dx, {"conv_w_e": dcw_e, "mix_norm_e": d_mix_e[0]}


def _place():
    x, y, c = (lax.axis_index(a) for a in MESH_AXES)
    return x, y, c


def _index(p):
    return 4 * p[0] + 2 * p[1] + p[2]


def _slab(ref, kind, d, n):
    if kind == "blk":
        return ref.at[d]
    return ref.at[:, pl.ds(pl.multiple_of(d * n, LANES), n)]


HBM = pl.BlockSpec(memory_space=pltpu.HBM)
SEM = pl.BlockSpec(memory_space=pltpu.SEMAPHORE)
EFFECT = pltpu.SideEffectType.DATAFLOW_SIDE_EFFECTING
NCHIPS = 4


def _in_hbm(a):
    return pltpu.with_memory_space_constraint(a, pltpu.HBM)


def _gathered_shape(s, kind):
    m, n = s.shape
    return (NDEV, m, n) if kind == "blk" else (m, NDEV * n)


def _first_targets():
    x, y, c = _place()
    return [(x, y, 1 - c), (1 - x, y, c), (x, 1 - y, c), (1 - x, 1 - y, c)]


def _gather_start(name, shards, kinds, after):
    na = len(shards)

    def body(*refs):
        x_refs, land_refs = refs[:na], refs[na:2 * na]
        send_sems, recv_sems = refs[2 * na + 1], refs[2 * na + 2]
        token = refs[-1]
        me = _index(_place())
        for a in range(na):
            for k, to in enumerate(_first_targets()):
                pltpu.make_async_remote_copy(
                    src_ref=x_refs[a], dst_ref=_slab(land_refs[a], kinds[a], me, shards[a].shape[1]),
                    send_sem=send_sems.at[4 * a + k], recv_sem=recv_sems.at[4 * a + k],
                    device_id=to, device_id_type=MESH).start()
        token[...] = jnp.zeros_like(token)

    lands = [lax.empty(_gathered_shape(s, k), s.dtype) for s, k in zip(shards, kinds)]
    outs = pl.pallas_call(
        body, name=name,
        out_shape=(pltpu.SemaphoreType.DMA((4 * na,)), pltpu.SemaphoreType.DMA((4 * na,)),
                   *[pltpu.HBM(s.shape, s.dtype) for s in shards],
                   *[pltpu.HBM(l.shape, l.dtype) for l in lands],
                   jax.ShapeDtypeStruct((SUBLANES, LANES), F32)),
        in_specs=[HBM] * (2 * na) + [ANY],
        out_specs=(SEM, SEM, *[HBM] * (2 * na), pl.BlockSpec(memory_space=pltpu.VMEM)),
        input_output_aliases={i: 2 + i for i in range(2 * na)},
        compiler_params=pltpu.CompilerParams(has_side_effects=EFFECT),
    )(*[_in_hbm(s) for s in shards], *[_in_hbm(l) for l in lands], after)
    return outs[0], outs[1], outs[2:2 + na], outs[2 + na:2 + 2 * na], outs[-1]


def _gather_wait(name, started, kinds, after):
    send_sems, recv_sems, shards, lands, _ = started
    na = len(shards)

    def body(*refs):
        x_refs, land_refs = refs[:na], refs[na:2 * na]
        s_sems, r_sems = refs[2 * na], refs[2 * na + 1]
        for a in range(na):
            for k, frm in enumerate(_first_targets()):
                cp = pltpu.make_async_remote_copy(
                    src_ref=x_refs[a],
                    dst_ref=_slab(land_refs[a], kinds[a], _index(frm), shards[a].shape[1]),
                    send_sem=s_sems.at[4 * a + k], recv_sem=r_sems.at[4 * a + k],
                    device_id=frm, device_id_type=MESH)
                cp.wait_send()
                cp.wait_recv()

    outs = pl.pallas_call(
        body, name=name,
        out_shape=(*[pltpu.HBM(s.shape, s.dtype) for s in shards],
                   *[pltpu.HBM(l.shape, l.dtype) for l in lands]),
        in_specs=[HBM] * (2 * na) + [SEM, SEM, ANY], out_specs=[HBM] * (2 * na),
        input_output_aliases={i: i for i in range(2 * na)},
        compiler_params=pltpu.CompilerParams(has_side_effects=EFFECT),
    )(*shards, *lands, send_sems, recv_sems, after)
    return outs[:na], outs[na:]


def _split_start(name, bufs, ncopies, plan, after):
    nb = len(bufs)

    def body(*refs):
        send_sems, recv_sems, token = refs[nb + 1], refs[nb + 2], refs[-1]
        for k, (src, dst, to, _) in enumerate(plan(refs[:nb])):
            pltpu.make_async_remote_copy(src_ref=src, dst_ref=dst, send_sem=send_sems.at[k],
                                         recv_sem=recv_sems.at[k], device_id=to, device_id_type=MESH).start()
        token[...] = jnp.zeros_like(token)

    outs = pl.pallas_call(
        body, name=name,
        out_shape=(pltpu.SemaphoreType.DMA((ncopies,)), pltpu.SemaphoreType.DMA((ncopies,)),
                   *[pltpu.HBM(b.shape, b.dtype) for b in bufs],
                   jax.ShapeDtypeStruct((SUBLANES, LANES), F32)),
        in_specs=[HBM] * nb + [ANY],
        out_specs=(SEM, SEM, *[HBM] * nb, pl.BlockSpec(memory_space=pltpu.VMEM)),
        input_output_aliases={i: 2 + i for i in range(nb)},
        compiler_params=pltpu.CompilerParams(has_side_effects=EFFECT),
    )(*[_in_hbm(b) for b in bufs], after)
    return outs[0], outs[1], list(outs[2:2 + nb]), outs[-1]


def _split_wait(name, started, plan, after):
    send_sems, recv_sems, bufs, _ = started
    nb = len(bufs)

    def body(*refs):
        s_sems, r_sems = refs[nb], refs[nb + 1]
        for k, (src, _, to, landed) in enumerate(plan(refs[:nb])):
            cp = pltpu.make_async_remote_copy(src_ref=src, dst_ref=landed, send_sem=s_sems.at[k],
                                              recv_sem=r_sems.at[k], device_id=to, device_id_type=MESH)
            cp.wait_send()
            cp.wait_recv()

    outs = pl.pallas_call(
        body, name=name, out_shape=tuple(pltpu.HBM(b.shape, b.dtype) for b in bufs),
        in_specs=[HBM] * nb + [SEM, SEM, ANY], out_specs=[HBM] * nb,
        input_output_aliases={i: i for i in range(nb)},
        compiler_params=pltpu.CompilerParams(has_side_effects=EFFECT),
    )(*bufs, send_sems, recv_sems, after)
    return list(outs)


def _forward_plan(kinds, nloc):
    def plan(lands):
        x, y, c = _place()
        out = []
        for a, land in enumerate(lands):
            for chip in [(1 - x, y), (x, 1 - y), (1 - x, 1 - y)]:
                mine = _slab(land, kinds[a], _index((*chip, c)), nloc[a])
                out.append((mine, mine, (x, y, 1 - c), _slab(land, kinds[a], _index((*chip, 1 - c)), nloc[a])))
        return out
    return plan


def _own_copy(name, shard, land, kind, me):
    m, n = shard.shape
    tr = _tile(m, max(SUBLANES, 1048576 // n), SUBLANES)

    def body(s_ref, x_ref, land_ref, o_ref):
        o_ref[...] = x_ref[...]

    if kind == "blk":
        o_spec = pl.BlockSpec((None, tr, n), lambda i, s: (s[0], i, 0))
    else:
        o_spec = pl.BlockSpec((tr, n), lambda i, s: (i, s[0]))
    return pl.pallas_call(
        body, name=name,
        grid_spec=pltpu.PrefetchScalarGridSpec(
            num_scalar_prefetch=1, grid=(m // tr,),
            in_specs=[pl.BlockSpec((tr, n), lambda i, s: (i, 0)), ANY], out_specs=o_spec),
        out_shape=jax.ShapeDtypeStruct(land.shape, land.dtype),
        input_output_aliases={2: 0}, compiler_params=_cp("parallel"))(me, shard, land)


def _everyone_plan(refs):
    x, y, c = _place()
    out = []
    for dx, dy, dc in [(a, b, e) for a in (0, 1) for b in (0, 1) for e in (0, 1)][1:]:
        peer = (x ^ dx, y ^ dy, c ^ dc)
        out.append((refs[0], refs[1].at[_index((x, y, c))], peer, refs[1].at[_index(peer)]))
    return out


def _pair_plan(kinds, nloc):
    na = len(kinds)

    def plan(refs):
        x, y, c = _place()
        out = []
        for a in range(na):
            for j in range(NCHIPS):
                dst = refs[na + a].at[j]
                out.append((_slab(refs[a], kinds[a], 2 * j + (1 - c), nloc[a]), dst, (x, y, 1 - c), dst))
        return out
    return plan


def _chip_sum(name, full, kind, n, from_sib, place):
    _, m, _ = from_sib.shape
    tr = _tile(m, max(SUBLANES, 1048576 // n), SUBLANES)

    def body(s_ref, mine_ref, sib_ref, csum_ref, land_ref):
        v = (mine_ref[...].astype(F32) + sib_ref[...].astype(F32)).astype(csum_ref.dtype)
        csum_ref[...] = v

        @pl.when(pl.program_id(1) == s_ref[1])
        def _():
            land_ref[...] = v

    if kind == "blk":
        mine_spec = pl.BlockSpec((None, tr, n), lambda i, j, s: (2 * j + s[0], i, 0))
    else:
        mine_spec = pl.BlockSpec((tr, n), lambda i, j, s: (i, 2 * j + s[0]))
    slot = pl.BlockSpec((None, tr, n), lambda i, j, s: (j, i, 0))
    shp = jax.ShapeDtypeStruct((NCHIPS, m, n), from_sib.dtype)
    return pl.pallas_call(
        body, name=name,
        grid_spec=pltpu.PrefetchScalarGridSpec(
            num_scalar_prefetch=1, grid=(m // tr, NCHIPS), in_specs=[mine_spec, slot],
            out_specs=[slot, pl.BlockSpec((None, tr, n), lambda i, j, s: (s[1], i, 0))]),
        out_shape=[shp, shp], compiler_params=_cp("parallel", "arbitrary"))(place, full, from_sib)


def _other_chips():
    x, y, c = _place()
    return [(1 - x, y, c), (x, 1 - y, c), (1 - x, 1 - y, c)]


def _scatter_start(name, csums, lands, after):
    na = len(csums)

    def body(*refs):
        c_refs, land_refs = refs[:na], refs[na:2 * na]
        send_sems, recv_sems = refs[2 * na + 1], refs[2 * na + 2]
        token = refs[-1]
        x, y, _ = _place()
        for a in range(na):
            for k, to in enumerate(_other_chips()):
                pltpu.make_async_remote_copy(
                    src_ref=c_refs[a].at[2 * to[0] + to[1]], dst_ref=land_refs[a].at[2 * x + y],
                    send_sem=send_sems.at[3 * a + k], recv_sem=recv_sems.at[3 * a + k],
                    device_id=to, device_id_type=MESH).start()
        token[...] = jnp.zeros_like(token)

    outs = pl.pallas_call(
        body, name=name,
        out_shape=(pltpu.SemaphoreType.DMA((3 * na,)), pltpu.SemaphoreType.DMA((3 * na,)),
                   *[pltpu.HBM(s.shape, s.dtype) for s in csums],
                   *[pltpu.HBM(l.shape, l.dtype) for l in lands],
                   jax.ShapeDtypeStruct((SUBLANES, LANES), F32)),
        in_specs=[HBM] * (2 * na) + [ANY],
        out_specs=(SEM, SEM, *[HBM] * (2 * na), pl.BlockSpec(memory_space=pltpu.VMEM)),
        input_output_aliases={i: 2 + i for i in range(2 * na)},
        compiler_params=pltpu.CompilerParams(has_side_effects=EFFECT),
    )(*[_in_hbm(s) for s in csums], *[_in_hbm(l) for l in lands], after)
    return outs[0], outs[1], outs[2:2 + na], outs[2 + na:2 + 2 * na], outs[-1]


def _scatter_wait(name, started, after):
    send_sems, recv_sems, csums, lands, _ = started
    na = len(csums)

    def body(*refs):
        c_refs, land_refs = refs[:na], refs[na:2 * na]
        s_sems, r_sems = refs[2 * na], refs[2 * na + 1]
        for a in range(na):
            for k, frm in enumerate(_other_chips()):
                cp = pltpu.make_async_remote_copy(
                    src_ref=c_refs[a].at[2 * frm[0] + frm[1]], dst_ref=land_refs[a].at[2 * frm[0] + frm[1]],
                    send_sem=s_sems.at[3 * a + k], recv_sem=r_sems.at[3 * a + k],
                    device_id=frm, device_id_type=MESH)
                cp.wait_send()
                cp.wait_recv()

    outs = pl.pallas_call(
        body, name=name,
        out_shape=(*[pltpu.HBM(s.shape, s.dtype) for s in csums],
                   *[pltpu.HBM(l.shape, l.dtype) for l in lands]),
        in_specs=[HBM] * (2 * na) + [SEM, SEM, ANY], out_specs=[HBM] * (2 * na),
        input_output_aliases={i: i for i in range(2 * na)},
        compiler_params=pltpu.CompilerParams(has_side_effects=EFFECT),
    )(*csums, *lands, send_sems, recv_sems, after)
    return outs[na:]


def _adam_math(w, g, m, v):
    m = ADAM_B1 * m + (1.0 - ADAM_B1) * g
    v = ADAM_B2 * v + (1.0 - ADAM_B2) * (g * g)
    m_hat = m / (1.0 - ADAM_B1 ** ADAM_STEP)
    v_hat = v / (1.0 - ADAM_B2 ** ADAM_STEP)
    delta = -ADAM_LR * (m_hat / (jnp.sqrt(v_hat) + ADAM_EPS) + ADAM_WD * w)
    return delta, m, v


def _sum_adamw(name, parts, w, m, v, layer, prev=None, dep=None):
    nl, r, c = w.shape
    nparts = parts.shape[0]
    tr = _tile(r, max(SUBLANES, 360448 // c), SUBLANES)

    def body(p_ref, w_ref, m_ref, v_ref, *rest):
        g_ref, d_ref, mo_ref, vo_ref = rest[-4:]
        g = p_ref[0].astype(F32)
        for s in range(1, nparts):
            g = g + p_ref[s].astype(F32)
        delta, mn, vn = _adam_math(w_ref[...], g, m_ref[...], v_ref[...])
        g_ref[...] = g
        d_ref[...] = delta
        mo_ref[...] = mn
        vo_ref[...] = vn

    row = pl.BlockSpec((None, tr, c), lambda i: (layer, i, 0))
    shp = jax.ShapeDtypeStruct((nl, r, c), F32)
    extra = ([] if prev is None else list(prev)) + ([] if dep is None else [dep])
    return pl.pallas_call(
        body, name=name, grid=(r // tr,),
        in_specs=[pl.BlockSpec((nparts, tr, c), lambda i: (0, i, 0)), row, row, row] + [ANY] * len(extra),
        out_specs=[row, row, row, row], out_shape=[shp, shp, shp, shp],
        input_output_aliases={} if prev is None else {4 + i: i for i in range(4)},
        compiler_params=_cp("parallel"))(parts, w, m, v, *extra)


def _sum_parts(name, parts):
    _, r, c = parts.shape

    def body(p_ref, o_ref):
        g = p_ref[0]
        for s in range(1, NDEV):
            g = g + p_ref[s]
        o_ref[...] = g

    return pl.pallas_call(
        body, name=name, grid=(1,),
        in_specs=[pl.BlockSpec((NDEV, r, c), lambda i: (0, 0, 0))],
        out_specs=pl.BlockSpec((r, c), lambda i: (0, 0)),
        out_shape=jax.ShapeDtypeStruct((r, c), F32), compiler_params=_cp("arbitrary"))(parts)


def _adamw(name, w, g, m, v):
    r, c = w.shape

    def body(w_ref, g_ref, m_ref, v_ref, d_ref, mo_ref, vo_ref):
        delta, mn, vn = _adam_math(w_ref[...], g_ref[...], m_ref[...], v_ref[...])
        d_ref[...] = delta
        mo_ref[...] = mn
        vo_ref[...] = vn

    full = pl.BlockSpec((r, c), lambda i: (0, 0))
    shp = jax.ShapeDtypeStruct((r, c), F32)
    return pl.pallas_call(
        body, name=name, grid=(1,), in_specs=[full] * 4, out_specs=[full] * 3,
        out_shape=[shp] * 3, compiler_params=_cp("arbitrary"))(w, g, m, v)


def _pack(arrays):
    flat = jnp.concatenate([a.reshape(-1) for a in arrays])
    unit = SUBLANES * LANES
    pad = (-flat.shape[0]) % unit
    return jnp.pad(flat, (0, pad)).reshape(-1, LANES)


def _unpack(buf, shapes):
    flat = buf.reshape(-1)
    out, off = [], 0
    for shp in shapes:
        size = 1
        for s in shp:
            size *= s
        out.append(flat[off:off + size].reshape(shp))
        off += size
    return out


WEIGHTS = ["mix_norm_e", "w_in_e", "conv_w_e", "conv_b_e", "ln_g_e", "ln_b_e", "w_pool_e",
           "pool_scale_e", "w_out_e", "mix_norm_o", "w_in_o", "conv_w_o", "w_out_o", "ffn_norm",
           "w_gate", "w_up", "w_down", "final_norm"]
BIG = ["w_in_e", "w_out_e", "w_in_o", "w_out_o", "w_gate", "w_up", "w_down"]
SHARDED_SMALL = {"conv_w_e": 1, "w_pool_e": 1, "mix_norm_o": 0, "conv_w_o": 1}
SMALL = [n for n in WEIGHTS if n not in BIG]


def kernel(x, mix_norm_e, w_in_e, conv_w_e, conv_b_e, ln_g_e, ln_b_e, w_pool_e, pool_scale_e, w_out_e, mix_norm_o, w_in_o, conv_w_o, w_out_o, ffn_norm, w_gate, w_up, w_down, final_norm, loss_target, m_mix_norm_e, m_w_in_e, m_conv_w_e, m_conv_b_e, m_ln_g_e, m_ln_b_e, m_w_pool_e, m_pool_scale_e, m_w_out_e, m_mix_norm_o, m_w_in_o, m_conv_w_o, m_w_out_o, m_ffn_norm, m_w_gate, m_w_up, m_w_down, m_final_norm, v_mix_norm_e, v_w_in_e, v_conv_w_e, v_conv_b_e, v_ln_g_e, v_ln_b_e, v_w_pool_e, v_pool_scale_e, v_w_out_e, v_mix_norm_o, v_w_in_o, v_conv_w_o, v_w_out_o, v_ffn_norm, v_w_gate, v_w_up, v_w_down, v_final_norm):
    wts = dict(zip(WEIGHTS, [mix_norm_e, w_in_e, conv_w_e, conv_b_e, ln_g_e, ln_b_e, w_pool_e, pool_scale_e, w_out_e, mix_norm_o, w_in_o, conv_w_o, w_out_o, ffn_norm, w_gate, w_up, w_down, final_norm]))
    mom = dict(zip(WEIGHTS, [m_mix_norm_e, m_w_in_e, m_conv_w_e, m_conv_b_e, m_ln_g_e, m_ln_b_e, m_w_pool_e, m_pool_scale_e, m_w_out_e, m_mix_norm_o, m_w_in_o, m_conv_w_o, m_w_out_o, m_ffn_norm, m_w_gate, m_w_up, m_w_down, m_final_norm]))
    var = dict(zip(WEIGHTS, [v_mix_norm_e, v_w_in_e, v_conv_w_e, v_conv_b_e, v_ln_g_e, v_ln_b_e, v_w_pool_e, v_pool_scale_e, v_w_out_e, v_mix_norm_o, v_w_in_o, v_conv_w_o, v_w_out_o, v_ffn_norm, v_w_gate, v_w_up, v_w_down, v_final_norm]))
    bsz, seq, d = x.shape
    t = bsz * seq
    me = _index(_place())
    me_arr = jnp.reshape(me, (1,)).astype(jnp.int32)

    sh_names = list(SHARDED_SMALL)
    sh_local = [wts[n][0] for n in sh_names]
    packed = _pack(sh_local)
    params_st = _split_start("small_params_start", [packed, lax.empty((NDEV,) + packed.shape, F32)], NDEV - 1,
                             _everyone_plan, x)

    for state in (wts, mom, var):
        for n in ("w_gate", "w_up"):
            state[n] = jnp.swapaxes(state[n], 1, 2)
    bf = lambda a: a.astype(BF16)
    mix_kinds, ffn_kinds = ["col", "blk"], ["blk", "blk", "blk"]
    ffn_names = ("w_gate", "w_up", "w_down")
    groups = {
        "mix_e": ([w_in_e.shape[2], d], mix_kinds, [("w_in_e", 0), ("w_out_e", 0)]),
        "ffn0": ([d, d, d], ffn_kinds, [(n, 0) for n in ffn_names]),
        "mix_o": ([w_in_o.shape[2], d], mix_kinds, [("w_in_o", 0), ("w_out_o", 0)]),
        "ffn1": ([d, d, d], ffn_kinds, [(n, 1) for n in ffn_names]),
    }
    gathers = {
        "in_e": ([bf(w_in_e[0])], ["col"]), "out_e": ([bf(w_out_e[0])], ["blk"]),
        "gu0": ([bf(wts["w_gate"][0]), bf(wts["w_up"][0])], ["blk", "blk"]), "down0": ([bf(w_down[0])], ["blk"]),
        "in_o": ([bf(w_in_o[0])], ["col"]), "out_o": ([bf(w_out_o[0])], ["blk"]),
        "gu1": ([bf(wts["w_gate"][1]), bf(wts["w_up"][1])], ["blk", "blk"]), "down1": ([bf(w_down[1])], ["blk"]),
    }
    started, prev = {}, params_st[3]
    for grp, (shards, kinds) in gathers.items():
        started[grp] = _gather_start("gather_start_" + grp, shards, kinds, prev)
        prev = started[grp][4]
    all_started = prev[0, 0:1]

    bufs = _split_wait("small_params_wait", params_st, _everyone_plan, prev)
    gathered = _own_copy("small_params_own", bufs[0], bufs[1], "blk", me_arr)
    small = {n: wts[n][0] for n in SMALL if n not in SHARDED_SMALL and n not in ("ffn_norm", "final_norm")}
    small["ffn_norm"], small["final_norm"] = ffn_norm, final_norm
    flat, off = gathered.reshape(NDEV, -1), 0
    for n, a in zip(sh_names, sh_local):
        ax, shp = SHARDED_SMALL[n], a.shape
        blocks = jnp.moveaxis(flat[:, off:off + a.size].reshape((NDEV,) + shp), 0, ax)
        small[n] = blocks.reshape(shp[:ax] + (NDEV * shp[ax],) + shp[ax + 1:])
        off += a.size

    passing, shards_of = {}, {}

    def pass_on(grp, after):
        shards, kinds = gathers[grp]
        shards_of[grp], lands = _gather_wait("gather_wait_" + grp, started[grp], kinds, after)
        plan = _forward_plan(kinds, [s.shape[1] for s in shards])
        passing[grp] = (_split_start("forward_start_" + grp, lands, 3 * len(lands), plan, after), plan)
        return passing[grp][0][3]

    def get_w(grp, after):
        if grp not in passing:
            after = pass_on(grp, after)
        st, plan = passing[grp]
        lands = _split_wait("forward_wait_" + grp, st, plan, after)
        full = [_own_copy("own_copy_%s%d" % (grp, a), shards_of[grp][a], lands[a], gathers[grp][1][a], me_arr)
                for a in range(len(lands))]
        return [f.reshape(-1, d) if kind == "blk" else f for f, kind in zip(full, gathers[grp][1])]

    cx, cy, cc = _place()
    place = jnp.stack([cc, 2 * cx + cy]).astype(jnp.int32)
    bwd_order = ["ffn1", "mix_o", "ffn0", "mix_e"]
    pairing, pending, results = {}, {}, {}

    late_names = ["conv_w_e", "mix_norm_e"]
    early_names = [n for n in SMALL if n not in late_names]
    small_sent = {}

    def send_small(tag, arrays, after):
        mine = _pack(arrays)
        small_sent[tag] = _split_start(tag + "_start", [mine, lax.empty((NDEV,) + mine.shape, F32)], NDEV - 1,
                                       _everyone_plan, after)
        return small_sent[tag][3]

    def summed_small(tag, shapes, after):
        bufs = _split_wait(tag + "_wait", small_sent[tag], _everyone_plan, after)
        parts = _own_copy(tag + "_own", bufs[0], bufs[1], "blk", me_arr)
        return _unpack(_sum_parts(tag + "_sum", parts), shapes)

    def put_g(grp, grads):
        if grp == "small":
            small_sent["shapes"] = [grads[n].shape for n in early_names]
            return send_small("small_grads", [grads[n] for n in early_names], place)
        nloc, kinds, _ = groups[grp]
        if len(kinds) == 2:
            fulls = [grads["w_in"], grads["w_out"].reshape(NDEV, -1, d)]
        else:
            fulls = [grads[n].reshape(NDEV, -1, d) for n in ffn_names]
        empties = []
        for g, kind, n in zip(fulls, kinds, nloc):
            empties.append(lax.empty((NCHIPS, g.shape[1] if kind == "blk" else g.shape[0], n), g.dtype))
        plan = _pair_plan(kinds, nloc)
        pairing[grp] = (_split_start("pair_start_" + grp, fulls + empties, NCHIPS * len(fulls), plan, place),
                        plan, kinds, nloc)
        token = pairing[grp][0][3]
        return send_sums(grp, token) if grp == bwd_order[-1] else token

    def send_sums(grp, after):
        st, plan, kinds, nloc = pairing[grp]
        bufs = _split_wait("pair_wait_" + grp, st, plan, after)
        na = len(kinds)
        sums = [_chip_sum("chip_sum_%s%d" % (grp, a), bufs[a], kinds[a], nloc[a], bufs[na + a], place)
                for a in range(na)]
        pending[grp] = _scatter_start("scatter_start_" + grp, [s[0] for s in sums], [s[1] for s in sums], after)
        return pending[grp][4]

    def finish(grp, after):
        lands = _scatter_wait("scatter_wait_" + grp, pending[grp], after)
        dep = None
        for (n, l), parts in zip(groups[grp][2], lands):
            results[n] = _sum_adamw("adamw_%s%d" % (n, l), parts, wts[n], mom[n], var[n], l, results.get(n), dep)
            dep = results[n][1]
        return dep

    fwd_sync = {"fwd_a": ["out_e"], "fwd_b": ["gu0"], "fwd_c": ["down0", "in_o"], "fwd_d": ["out_o"],
                "fwd_e": ["gu1"], "fwd_f": ["down1"]}

    def sync(tag, after):
        if tag in fwd_sync:
            for grp in fwd_sync[tag]:
                after = pass_on(grp, after)
            return after
        if tag == "bwd_mix_o":
            return send_sums("ffn1", after)
        if tag == "bwd_ffn0":
            return finish("ffn1", send_sums("mix_o", after))
        if tag == "bwd_mix_e":
            return finish("mix_o", send_sums("ffn0", after))
        return None

    small["mix_norm_e"] = small["mix_norm_e"] + all_started
    lsum, dx, late = _local_step(x.reshape(t, d), loss_target.reshape(t, d), seq, small, get_w, put_g, sync)
    loss = lax.psum(jnp.sum(lsum), MESH_AXES)

    out_g, out_d, out_m, out_v = {}, {}, {}, {}

    dep = finish("ffn0", send_small("last_grads", [late[n] for n in late_names], dx))
    sums = dict(zip(early_names, summed_small("small_grads", small_sent["shapes"], dep)))
    sums.update(zip(late_names, summed_small("last_grads", [late[n].shape for n in late_names], dep)))
    gs_sum = [sums[n] for n in SMALL]
    local_g = []
    for n, g in zip(SMALL, gs_sum):
        if n in SHARDED_SMALL:
            ax = SHARDED_SMALL[n]
            size = wts[n].shape[ax + 1]
            g = lax.dynamic_slice_in_dim(g, me * size, size, axis=ax)
        local_g.append(g.reshape(wts[n].shape))
    shapes = [wts[n].shape for n in SMALL]
    upd = _adamw("adamw_small", _pack([wts[n] for n in SMALL]), _pack(local_g),
                 _pack([mom[n] for n in SMALL]), _pack([var[n] for n in SMALL]))
    for i, outd in enumerate((out_d, out_m, out_v)):
        for n, a in zip(SMALL, _unpack(upd[i], shapes)):
            outd[n] = a
    for n, g in zip(SMALL, local_g):
        out_g[n] = g

    finish("mix_e", upd[0])
    for n in BIG:
        res = [jnp.swapaxes(a, 1, 2) for a in results[n]] if n in ("w_gate", "w_up") else results[n]
        out_g[n], out_d[n], out_m[n], out_v[n] = res

    return (loss, dx.reshape(bsz, seq, d), *[out_g[n] for n in WEIGHTS], *[out_d[n] for n in WEIGHTS],
            *[out_m[n] for n in WEIGHTS], *[out_v[n] for n in WEIGHTS])
```

```python
import jax
import jax.numpy as jnp
from jax import lax
from jax.experimental import pallas as pl
from jax.experimental.pallas import tpu as pltpu

F32 = jnp.float32
BF16 = jnp.bfloat16
NDEV = 8
MESH_AXES = ("x", "y", "c")
EPS = 1e-6
POOL_WINDOWS = (2, 4, 8, 16)
ADAM_LR = 0.001
ADAM_B1 = 0.9
ADAM_B2 = 0.999
ADAM_EPS = 1e-08
ADAM_WD = 0.01
ADAM_STEP = 10
LANES = 128
SUBLANES = 8
VMEM_LIMIT = 56 * 1024 * 1024
MXU_DEPTH = 256
MM_TK = 2816
MESH = pl.DeviceIdType.MESH
ANY = pl.BlockSpec(memory_space=pl.ANY)


def _cp(*sem):
    return pltpu.CompilerParams(dimension_semantics=sem, vmem_limit_bytes=VMEM_LIMIT)


def _tile(n, pref, unit=LANES):
    if n <= pref:
        return n
    t = (pref // unit) * unit
    while t > unit and n % t:
        t -= unit
    assert n % t == 0, (n, pref)
    return t


def _sigmoid(v):
    return 0.5 * jnp.tanh(0.5 * v) + 0.5


def _mm(name, pairs, a_specs, b_specs, dims, out_shape, o_spec, grid, acc_shape,
        res=None, res_spec=None, dep=None):
    np_ = len(pairs)
    nk = grid[2]
    has_res = res is not None
    n_in = 2 * np_ + (1 if has_res else 0) + (0 if dep is None else 1)

    def body(*refs):
        a_refs = refs[:np_]
        b_refs = refs[np_:2 * np_]
        r_ref = refs[2 * np_] if has_res else None
        o_ref = refs[n_in]
        acc = refs[-1]

        def part():
            s = None
            for a_ref, b_ref in zip(a_refs, b_refs):
                d = lax.dot_general(a_ref[...], b_ref[...], dims, preferred_element_type=F32)
                s = d if s is None else s + d
            return s

        def finish(v):
            if has_res:
                v = v + r_ref[...]
            o_ref[...] = v.astype(o_ref.dtype)

        if nk == 1:
            finish(part())
        else:
            k = pl.program_id(2)

            @pl.when(k == 0)
            def _():
                acc[...] = part()

            @pl.when((k > 0) & (k < nk - 1))
            def _():
                acc[...] += part()

            @pl.when(k == nk - 1)
            def _():
                finish(acc[...] + part())

    ins = [p[0] for p in pairs] + [p[1] for p in pairs]
    specs = list(a_specs) + list(b_specs)
    if has_res:
        ins.append(res)
        specs.append(res_spec)
    if dep is not None:
        ins.append(dep)
        specs.append(ANY)
    return pl.pallas_call(
        body, name=name, grid=grid, in_specs=specs, out_specs=o_spec, out_shape=out_shape,
        scratch_shapes=[pltpu.VMEM(acc_shape if nk > 1 else (SUBLANES, LANES), F32)],
        compiler_params=_cp("parallel", "parallel", "arbitrary"))(*ins)


NN = (((1,), (0,)), ((), ()))
NT = (((1,), (1,)), ((), ()))
TN = (((0,), (0,)), ((), ()))


def _tiles_mk(m, kk):
    return _tile(m, 1024), _tile(kk, MM_TK, MXU_DEPTH)


def _mm_nn(name, a, b, out_dtype, res=None, dep=None):
    pairs = list(zip(a, b)) if isinstance(a, (list, tuple)) else [(a, b)]
    m, kk = pairs[0][0].shape
    n = pairs[0][1].shape[1]
    tm, tk = _tiles_mk(m, kk)
    tn = _tile(n, 1024 if tk * len(pairs) <= MM_TK else 512)
    return _mm(name, pairs,
               [pl.BlockSpec((tm, tk), lambda i, j, k: (i, k))] * len(pairs),
               [pl.BlockSpec((tk, tn), lambda i, j, k: (k, j))] * len(pairs), NN,
               jax.ShapeDtypeStruct((m, n), out_dtype),
               pl.BlockSpec((tm, tn), lambda i, j, k: (i, j)),
               (m // tm, n // tn, kk // tk), (tm, tn), res,
               pl.BlockSpec((tm, tn), lambda i, j, k: (i, j)), dep=dep)


def _mm_nt(name, a, b, out_dtype, dep=None):
    m, n = a.shape
    kk = b.shape[0]
    tn = _tile(kk, 1024)
    tm, tk = _tiles_mk(m, n)
    return _mm(name, [(a, b)],
               [pl.BlockSpec((tm, tk), lambda i, j, k: (i, k))],
               [pl.BlockSpec((tn, tk), lambda i, j, k: (j, k))], NT,
               jax.ShapeDtypeStruct((m, kk), out_dtype),
               pl.BlockSpec((tm, tn), lambda i, j, k: (i, j)),
               (m // tm, kk // tn, n // tk), (tm, tn), dep=dep)


def _mm_tn(name, a, b, out_dtype, dep=None):
    t, m = a.shape
    n = b.shape[1]
    tn = _tile(n, 1024)
    tm, tk = _tile(m, 1408), _tile(t, MM_TK, MXU_DEPTH)
    return _mm(name, [(a, b)],
               [pl.BlockSpec((tk, tm), lambda i, j, k: (k, i))],
               [pl.BlockSpec((tk, tn), lambda i, j, k: (k, j))], TN,
               jax.ShapeDtypeStruct((m, n), out_dtype),
               pl.BlockSpec((tm, tn), lambda i, j, k: (i, j)),
               (m // tm, n // tn, t // tk), (tm, tn), dep=dep)


def _ffn_fwd(name, n, wg, wu):
    f, d = wg.shape
    t = n.shape[0]
    tm, tn = _tile(t, 1024), _tile(f, 512)

    def body(n_ref, wg_ref, wu_ref, act_ref, ds_ref, s_ref):
        nv = n_ref[...]
        g = lax.dot_general(nv, wg_ref[...], NT, preferred_element_type=F32)
        up = lax.dot_general(nv, wu_ref[...], NT, preferred_element_type=F32)
        sg = _sigmoid(g)
        silu = g * sg
        act_ref[...] = (silu * up).astype(BF16)
        ds_ref[...] = (up * (sg * (1.0 + g * (1.0 - sg)))).astype(BF16)
        s_ref[...] = silu.astype(BF16)

    w_spec = pl.BlockSpec((tn, d), lambda j, i: (j, 0))
    o_spec = pl.BlockSpec((tm, tn), lambda j, i: (i, j))
    shp = jax.ShapeDtypeStruct((t, f), BF16)
    return pl.pallas_call(
        body, name=name, grid=(f // tn, t // tm),
        in_specs=[pl.BlockSpec((tm, d), lambda j, i: (i, 0)), w_spec, w_spec],
        out_specs=[o_spec, o_spec, o_spec], out_shape=[shp, shp, shp],
        compiler_params=_cp("parallel", "parallel"))(n, wg, wu)


def _ffn_bwd_act(name, dh, wd, dsilu, silu, dep=None):
    f, d = wd.shape
    t = dh.shape[0]
    tm, tn = _tile(t, 1024), _tile(f, 512)

    def body(dh_ref, wd_ref, ds_ref, s_ref, *rest):
        dg_ref, dup_ref = rest[-2:]
        da = lax.dot_general(dh_ref[...], wd_ref[...], NT, preferred_element_type=F32)
        dg_ref[...] = (da * ds_ref[...].astype(F32)).astype(BF16)
        dup_ref[...] = (da * s_ref[...].astype(F32)).astype(BF16)

    o_spec = pl.BlockSpec((tm, tn), lambda i, j: (i, j))
    shp = jax.ShapeDtypeStruct((t, f), BF16)
    return pl.pallas_call(
        body, name=name, grid=(t // tm, f // tn),
        in_specs=[pl.BlockSpec((tm, d), lambda i, j: (i, 0)),
                  pl.BlockSpec((tn, d), lambda i, j: (j, 0)), o_spec, o_spec]
        + ([] if dep is None else [ANY]),
        out_specs=[o_spec, o_spec], out_shape=[shp, shp],
        compiler_params=_cp("parallel", "parallel"))(dh, wd, dsilu, silu, *([] if dep is None else [dep]))


def _rms_fwd(name, h, gain):
    t, d = h.shape
    tr = _tile(t, 512, SUBLANES)

    def body(h_ref, g_ref, n_ref):
        hv = h_ref[...]
        r = lax.rsqrt(jnp.mean(hv * hv, axis=-1, keepdims=True) + EPS)
        n_ref[...] = (hv * r * g_ref[...]).astype(BF16)

    return pl.pallas_call(
        body, name=name, grid=(t // tr,),
        in_specs=[pl.BlockSpec((tr, d), lambda i: (i, 0)), pl.BlockSpec((1, d), lambda i: (0, 0))],
        out_specs=pl.BlockSpec((tr, d), lambda i: (i, 0)),
        out_shape=jax.ShapeDtypeStruct((t, d), BF16),
        compiler_params=_cp("parallel"))(h, gain)


def _rms_bwd_math(hv, gain, dn):
    d = hv.shape[-1]
    r = lax.rsqrt(jnp.mean(hv * hv, axis=-1, keepdims=True) + EPS)
    xhat = hv * r
    dxh = dn * gain
    dh = r * (dxh - xhat * (jnp.sum(dxh * xhat, axis=-1, keepdims=True) / d))
    dgain = jnp.sum(dn * xhat, axis=0, keepdims=True)
    return dh, dgain


def _rms_bwd(name, h, gain, dn, dres):
    t, d = h.shape
    tr = _tile(t, 256, SUBLANES)

    def body(h_ref, g_ref, dn_ref, dr_ref, dh_ref, dhb_ref, dg_ref):
        dh, dgain = _rms_bwd_math(h_ref[...], g_ref[...], dn_ref[...].astype(F32))
        dh = dh + dr_ref[...]
        dh_ref[...] = dh
        dhb_ref[...] = dh.astype(BF16)

        @pl.when(pl.program_id(0) == 0)
        def _():
            dg_ref[...] = dgain

        @pl.when(pl.program_id(0) > 0)
        def _():
            dg_ref[...] += dgain

    row = pl.BlockSpec((tr, d), lambda i: (i, 0))
    vec = pl.BlockSpec((1, d), lambda i: (0, 0))
    return pl.pallas_call(
        body, name=name, grid=(t // tr,), in_specs=[row, vec, row, row],
        out_specs=[row, row, vec],
        out_shape=[jax.ShapeDtypeStruct((t, d), F32), jax.ShapeDtypeStruct((t, d), BF16),
                   jax.ShapeDtypeStruct((1, d), F32)],
        compiler_params=_cp("arbitrary"))(h, gain, dn, dres)


def _loss_head(name, h, gain, tgt):
    t, d = h.shape
    tr = _tile(t, 256, SUBLANES)

    def body(h_ref, g_ref, t_ref, dh_ref, dhb_ref, dg_ref, ls_ref):
        hv = h_ref[...]
        gv = g_ref[...]
        r = lax.rsqrt(jnp.mean(hv * hv, axis=-1, keepdims=True) + EPS)
        err = hv * r * gv - t_ref[...]
        lsum = 0.5 * jnp.sum(err * err, axis=0, keepdims=True) / d
        dh, dgain = _rms_bwd_math(hv, gv, err / d)
        dh_ref[...] = dh
        dhb_ref[...] = dh.astype(BF16)

        @pl.when(pl.program_id(0) == 0)
        def _():
            dg_ref[...] = dgain
            ls_ref[...] = lsum

        @pl.when(pl.program_id(0) > 0)
        def _():
            dg_ref[...] += dgain
            ls_ref[...] += lsum

    row = pl.BlockSpec((tr, d), lambda i: (i, 0))
    vec = pl.BlockSpec((1, d), lambda i: (0, 0))
    return pl.pallas_call(
        body, name=name, grid=(t // tr,), in_specs=[row, vec, row],
        out_specs=[row, row, vec, vec],
        out_shape=[jax.ShapeDtypeStruct((t, d), F32), jax.ShapeDtypeStruct((t, d), BF16),
                   jax.ShapeDtypeStruct((1, d), F32), jax.ShapeDtypeStruct((1, d), F32)],
        compiler_params=_cp("arbitrary"))(h, gain, tgt)


def _conv_geom(t, seq, c, k, full_width=False):
    halo = 32 if k - 1 > SUBLANES else SUBLANES
    assert k - 1 <= halo
    tm = min(256 if halo > SUBLANES else 1024, seq // 2)
    tc = c if full_width else min(512, c)
    assert seq % tm == 0 and tm % halo == 0 and c % tc == 0 and t % seq == 0
    return halo, tm, tc, min(64 if halo > SUBLANES else 128, tm), min(LANES, tc)


def _pre(kind, a, b):
    if kind == "glu":
        return a * _sigmoid(b)
    if kind == "mul":
        return a * b
    return a


def _taps(k):
    return sorted((s % SUBLANES, s // SUBLANES, s) for s in range(k))


def _conv_fwd(name, seq, c, w, x1, c1, x2=None, c2=0, pre=None, bias=None, post=None, cpost=0, live=None):
    t = x1.shape[0]
    k = w.shape[0]
    halo, tm, tc, sr, sl = _conv_geom(t, seq, c, k, live is not None)
    nb, cps = tm // halo, seq // tm
    two = x2 is not None
    has_bias, has_post = bias is not None, post is not None

    def body(*refs):
        it = iter(refs)
        x1c, x1h = next(it), next(it)
        x2c, x2h = (next(it), next(it)) if two else (None, None)
        w_ref = next(it)
        b_ref = next(it) if has_bias else None
        p_ref = next(it) if has_post else None
        o_ref = next(it)
        y_ref = next(it) if has_post else None
        xs = next(it)
        first = (pl.program_id(1) % cps) == 0
        hv = _pre(pre, x1h[...].astype(F32), x2h[...].astype(F32) if two else None)
        xs[0:halo, :] = jnp.where(first, 0.0, hv)
        xs[halo:halo + tm, :] = _pre(pre, x1c[...].astype(F32), x2c[...].astype(F32) if two else None)
        for l0 in range(0, tc, sl):
            ls = slice(l0, l0 + sl)
            for r0 in range(0, tm, sr):
                win = xs[r0:r0 + sr + halo, ls]
                acc = jnp.zeros((sr, sl), F32)
                rolled = {}
                for r, q, s in _taps(k if live is None else live[l0 // sl]):
                    if r not in rolled:
                        rolled[r] = win if r == 0 else pltpu.roll(win, r, 0)
                    lo = halo - SUBLANES * q
                    acc = acc + w_ref[k - 1 - s:k - s, ls] * rolled[r][lo:lo + sr]
                if has_bias:
                    acc = acc + b_ref[:, ls]
                o_ref[r0:r0 + sr, ls] = acc.astype(o_ref.dtype)
                if has_post:
                    y_ref[r0:r0 + sr, ls] = (acc * p_ref[r0:r0 + sr, ls].astype(F32)).astype(y_ref.dtype)

    def cur(off):
        return pl.BlockSpec((tm, tc), lambda j, i: (i, off // tc + j))

    def prev(off):
        return pl.BlockSpec((halo, tc), lambda j, i: (jnp.maximum(i * nb - 1, 0), off // tc + j))

    ins, specs = [x1, x1], [cur(c1), prev(c1)]
    if two:
        ins += [x2, x2]
        specs += [cur(c2), prev(c2)]
    ins.append(w)
    specs.append(pl.BlockSpec((k, tc), lambda j, i: (0, j)))
    if has_bias:
        ins.append(bias)
        specs.append(pl.BlockSpec((1, tc), lambda j, i: (0, j)))
    if has_post:
        ins.append(post)
        specs.append(cur(cpost))
    o_spec = pl.BlockSpec((tm, tc), lambda j, i: (i, j))
    shp = jax.ShapeDtypeStruct((t, c), BF16)
    return pl.pallas_call(
        body, name=name, grid=(c // tc, t // tm), in_specs=specs,
        out_specs=[o_spec, o_spec] if has_post else o_spec,
        out_shape=[shp, shp] if has_post else shp,
        scratch_shapes=[pltpu.VMEM((halo + tm, tc), F32)],
        compiler_params=_cp("parallel", "parallel"))(*ins)


def _conv_bwd(name, seq, c, w, d1, cd1, d2=None, cd2=0, dpre=None,
              x1=None, c1=0, x2=None, c2=0, pre=None, live=None, dep=None):
    t = d1.shape[0]
    k = w.shape[0]
    assert live is None or x1 is None
    halo, tm, tc, sr, sl = _conv_geom(t, seq, c, k, live is not None)
    nb, cps = tm // halo, seq // tm
    nchunks = t // tm
    dtwo, xtwo, has_x = d2 is not None, x2 is not None, x1 is not None

    def body(*refs):
        it = iter(refs)
        d1c, d1n = next(it), next(it)
        d2c, d2n = (next(it), next(it)) if dtwo else (None, None)
        x1c, x1h = (next(it), next(it)) if has_x else (None, None)
        x2c, x2h = (next(it), next(it)) if xtwo else (None, None)
        w_ref = next(it)
        if dep is not None:
            next(it)
        dx_ref = next(it)
        dw_ref = next(it) if has_x else None
        ds = next(it)
        xs = next(it) if has_x else None
        i = pl.program_id(1)
        last = (i % cps) == cps - 1
        ds[0:tm, :] = _pre(dpre, d1c[...].astype(F32), d2c[...].astype(F32) if dtwo else None)
        nv = _pre(dpre, d1n[...].astype(F32), d2n[...].astype(F32) if dtwo else None)
        ds[tm:tm + halo, :] = jnp.where(last, 0.0, nv)
        if has_x:
            first = (i % cps) == 0
            hv = _pre(pre, x1h[...].astype(F32), x2h[...].astype(F32) if xtwo else None)
            xs[0:halo, :] = jnp.where(first, 0.0, hv)
            xs[halo:halo + tm, :] = _pre(pre, x1c[...].astype(F32), x2c[...].astype(F32) if xtwo else None)

            @pl.when(i == 0)
            def _():
                dw_ref[...] = jnp.zeros_like(dw_ref)

        for l0 in range(0, tc, sl):
            ls = slice(l0, l0 + sl)
            for r0 in range(0, tm, sr):
                win = ds[r0:r0 + sr + halo, ls]
                nrow = sr + halo
                acc = jnp.zeros((sr, sl), F32)
                rolled = {}
                for r, q, s in _taps(k if live is None else live[l0 // sl]):
                    if r not in rolled:
                        rolled[r] = win if r == 0 else pltpu.roll(win, nrow - r, 0)
                    lo = SUBLANES * q
                    acc = acc + w_ref[k - 1 - s:k - s, ls] * rolled[r][lo:lo + sr]
                dx_ref[r0:r0 + sr, ls] = acc.astype(dx_ref.dtype)
                if has_x:
                    dcur = win[0:sr]
                    xwin = xs[r0:r0 + sr + halo, ls]
                    xrolled = {}
                    for r, q, s in _taps(k):
                        if r not in xrolled:
                            xrolled[r] = xwin if r == 0 else pltpu.roll(xwin, r, 0)
                        lo = halo - SUBLANES * q
                        part = jnp.sum(dcur * xrolled[r][lo:lo + sr], axis=0, keepdims=True)
                        dw_ref[k - 1 - s:k - s, ls] += part

    def cur(off):
        return pl.BlockSpec((tm, tc), lambda j, i: (i, off // tc + j))

    def prev(off):
        return pl.BlockSpec((halo, tc), lambda j, i: (jnp.maximum(i * nb - 1, 0), off // tc + j))

    def nxt(off):
        return pl.BlockSpec((halo, tc),
                            lambda j, i: (jnp.minimum((i + 1) * nb, nchunks * nb - 1), off // tc + j))

    ins, specs = [d1, d1], [cur(cd1), nxt(cd1)]
    if dtwo:
        ins += [d2, d2]
        specs += [cur(cd2), nxt(cd2)]
    if has_x:
        ins += [x1, x1]
        specs += [cur(c1), prev(c1)]
    if xtwo:
        ins += [x2, x2]
        specs += [cur(c2), prev(c2)]
    ins.append(w)
    specs.append(pl.BlockSpec((k, tc), lambda j, i: (0, j)))
    if dep is not None:
        ins.append(dep)
        specs.append(ANY)
    o_specs = [pl.BlockSpec((tm, tc), lambda j, i: (i, j))]
    o_shapes = [jax.ShapeDtypeStruct((t, c), BF16)]
    scratch = [pltpu.VMEM((tm + halo, tc), F32)]
    if has_x:
        o_specs.append(pl.BlockSpec((k, tc), lambda j, i: (0, j)))
        o_shapes.append(jax.ShapeDtypeStruct((k, c), F32))
        scratch.append(pltpu.VMEM((halo + tm, tc), F32))
    out = pl.pallas_call(
        body, name=name, grid=(c // tc, t // tm), in_specs=specs, out_specs=o_specs,
        out_shape=o_shapes, scratch_shapes=scratch,
        compiler_params=_cp("parallel", "arbitrary"))(*ins)
    return out if has_x else out[0]


def _pool_taps(c):
    kmax = max(POOL_WINDOWS)
    grp = c // len(POOL_WINDOWS)
    cols = []
    for wdw in POOL_WINDOWS:
        col = jnp.concatenate([jnp.zeros((kmax - wdw,), F32), jnp.ones((wdw,), F32)])
        cols.append(jnp.tile(col[:, None], (1, grp)))
    return jnp.concatenate(cols, axis=1)


def _pool_live(c):
    grp, sl = c // len(POOL_WINDOWS), min(LANES, c)
    return tuple(max(POOL_WINDOWS[g] for g in range(l0 // grp, (l0 + sl - 1) // grp + 1))
                 for l0 in range(0, c, sl))


def _counts(i, tr, seq, grp):
    pos = (i * tr + lax.broadcasted_iota(jnp.int32, (tr, 1), 0)) % seq + 1
    return [1.0 / jnp.minimum(pos, wdw).astype(F32) for wdw in POOL_WINDOWS]


def _ln_stats(a2):
    mu = jnp.mean(a2, axis=-1, keepdims=True)
    xc = a2 - mu
    rstd = lax.rsqrt(jnp.mean(xc * xc, axis=-1, keepdims=True) + EPS)
    return xc * rstd, rstd


def _even_fwd(name, seq, a2, ws, u, ln_g, ln_b, w_pool, scale):
    t, c = a2.shape
    ng = len(POOL_WINDOWS)
    grp = c // ng
    tr = _tile(t, 256, SUBLANES)

    def body(a_ref, ws_ref, b_ref, g_ref, bb_ref, wp_ref, sc_ref, z_ref, pm_ref):
        xhat, _ = _ln_stats(a_ref[...].astype(F32))
        l = xhat * g_ref[...] + bb_ref[...]
        z_ref[:, 0:c] = (l * _sigmoid(l)).astype(BF16)
        inv = _counts(pl.program_id(0), tr, seq, grp)
        for g in range(ng):
            gs = slice(g * grp, (g + 1) * grp)
            pm = (ws_ref[:, gs].astype(F32) * inv[g] - b_ref[:, gs].astype(F32)).astype(BF16)
            pm_ref[:, gs] = pm
            q = jnp.dot(pm, wp_ref[g], preferred_element_type=F32)
            z_ref[:, c + g * grp:c + (g + 1) * grp] = (q * sc_ref[:, gs]).astype(BF16)

    row = pl.BlockSpec((tr, c), lambda i: (i, 0))
    vec = pl.BlockSpec((1, c), lambda i: (0, 0))
    return pl.pallas_call(
        body, name=name, grid=(t // tr,),
        in_specs=[row, row, pl.BlockSpec((tr, c), lambda i: (i, 2)), vec, vec,
                  pl.BlockSpec((ng, grp, grp), lambda i: (0, 0, 0)), vec],
        out_specs=[pl.BlockSpec((tr, 2 * c), lambda i: (i, 0)), row],
        out_shape=[jax.ShapeDtypeStruct((t, 2 * c), BF16), jax.ShapeDtypeStruct((t, c), BF16)],
        compiler_params=_cp("parallel"))(a2, ws, u, ln_g, ln_b, w_pool, scale)


def _even_bwd(name, seq, dz, a2, pm, ln_g, ln_b, w_pool, scale):
    t, c = a2.shape
    ng = len(POOL_WINDOWS)
    grp = c // ng
    tr = _tile(t, 256, SUBLANES)

    def body(dz_ref, a_ref, pm_ref, g_ref, bb_ref, wp_ref, sc_ref,
             da_ref, dws_ref, dpm_ref, vec_ref, dwp_ref):
        i = pl.program_id(0)

        @pl.when(i == 0)
        def _():
            vec_ref[...] = jnp.zeros_like(vec_ref)
            dwp_ref[...] = jnp.zeros_like(dwp_ref)

        xhat, rstd = _ln_stats(a_ref[...].astype(F32))
        gv = g_ref[...]
        l = xhat * gv + bb_ref[...]
        sg = _sigmoid(l)
        dl = dz_ref[:, 0:c].astype(F32) * (sg * (1.0 + l * (1.0 - sg)))
        dxh = dl * gv
        da2 = rstd * (dxh - jnp.mean(dxh, axis=-1, keepdims=True)
                      - xhat * jnp.mean(dxh * xhat, axis=-1, keepdims=True))
        da_ref[...] = da2.astype(BF16)
        vec_ref[0:1, :] += jnp.sum(dl * xhat, axis=0, keepdims=True)
        vec_ref[1:2, :] += jnp.sum(dl, axis=0, keepdims=True)
        vec_ref[2:3, :] += jnp.sum(da2, axis=0, keepdims=True)
        inv = _counts(i, tr, seq, grp)
        for g in range(ng):
            gs = slice(g * grp, (g + 1) * grp)
            pmv = pm_ref[:, gs]
            wp = wp_ref[g]
            dp = dz_ref[:, c + g * grp:c + (g + 1) * grp].astype(F32)
            q = jnp.dot(pmv, wp, preferred_element_type=F32)
            vec_ref[3:4, gs] += jnp.sum(dp * q, axis=0, keepdims=True)
            dq = (dp * sc_ref[:, gs]).astype(BF16)
            dpm = lax.dot_general(dq, wp, NT, preferred_element_type=F32)
            dwp_ref[g] += lax.dot_general(pmv, dq, TN, preferred_element_type=F32)
            dpm_ref[:, gs] = dpm.astype(BF16)
            dws_ref[:, gs] = (dpm * inv[g]).astype(BF16)

    row = pl.BlockSpec((tr, c), lambda i: (i, 0))
    vec = pl.BlockSpec((1, c), lambda i: (0, 0))
    rshape = jax.ShapeDtypeStruct((t, c), BF16)
    return pl.pallas_call(
        body, name=name, grid=(t // tr,),
        in_specs=[pl.BlockSpec((tr, 2 * c), lambda i: (i, 0)), row, row, vec, vec,
                  pl.BlockSpec((ng, grp, grp), lambda i: (0, 0, 0)), vec],
        out_specs=[row, row, row, pl.BlockSpec((SUBLANES, c), lambda i: (0, 0)),
                   pl.BlockSpec((ng, grp, grp), lambda i: (0, 0, 0))],
        out_shape=[rshape, rshape, rshape, jax.ShapeDtypeStruct((SUBLANES, c), F32),
                   jax.ShapeDtypeStruct((ng, grp, grp), F32)],
        compiler_params=_cp("arbitrary"))(dz, a2, pm, ln_g, ln_b, w_pool, scale)


def _even_du(name, u, da1, dbp, dpm):
    t, c = da1.shape
    tr = _tile(t, 256, SUBLANES)

    def body(u_ref, da_ref, dbp_ref, dpm_ref, du_ref):
        val = u_ref[:, 0:c].astype(F32)
        sg = _sigmoid(u_ref[:, c:2 * c].astype(F32))
        da = da_ref[...].astype(F32)
        du_ref[:, 0:c] = (da * sg).astype(BF16)
        du_ref[:, c:2 * c] = (da * val * sg * (1.0 - sg)).astype(BF16)
        du_ref[:, 2 * c:3 * c] = (dbp_ref[...].astype(F32) - dpm_ref[...].astype(F32)).astype(BF16)

    row = pl.BlockSpec((tr, c), lambda i: (i, 0))
    wide = pl.BlockSpec((tr, 3 * c), lambda i: (i, 0))
    return pl.pallas_call(
        body, name=name, grid=(t // tr,), in_specs=[wide, row, row, row], out_specs=wide,
        out_shape=jax.ShapeDtypeStruct((t, 3 * c), BF16),
        compiler_params=_cp("parallel"))(u, da1, dbp, dpm)


def _odd_du(name, u, dy, co, dxc):
    t, c = dy.shape
    tr = _tile(t, 256, SUBLANES)

    def body(u_ref, dy_ref, co_ref, dx_ref, du_ref):
        dx = dx_ref[...].astype(F32)
        du_ref[:, 0:c] = (dy_ref[...].astype(F32) * co_ref[...].astype(F32)).astype(BF16)
        du_ref[:, c:2 * c] = (dx * u_ref[:, 2 * c:3 * c].astype(F32)).astype(BF16)
        du_ref[:, 2 * c:3 * c] = (dx * u_ref[:, c:2 * c].astype(F32)).astype(BF16)

    row = pl.BlockSpec((tr, c), lambda i: (i, 0))
    wide = pl.BlockSpec((tr, 3 * c), lambda i: (i, 0))
    return pl.pallas_call(
        body, name=name, grid=(t // tr,), in_specs=[wide, row, row, row], out_specs=wide,
        out_shape=jax.ShapeDtypeStruct((t, 3 * c), BF16),
        compiler_params=_cp("parallel"))(u, dy, co, dxc)


def _local_step(x, tgt, seq, small, get_w, put_g, sync):
    t, d = x.shape
    c = d // 2
    cw_e, cw_o = small["conv_w_e"], small["conv_w_o"]
    wp = small["w_pool_e"].astype(BF16)
    ptaps = _pool_taps(c)
    row = lambda v: v.reshape(1, -1)

    we = {"w_in": get_w("in_e", x)[0]}
    n0 = _rms_fwd("rms_fwd_mix0", x, row(small["mix_norm_e"]))
    u0 = _mm_nn("mm_in_e", n0, we["w_in"], BF16)
    sync("fwd_a", u0)
    a2 = _conv_fwd("conv_e_fwd", seq, c, cw_e, u0, 0, u0, c, "glu", bias=row(small["conv_b_e"]))
    ws = _conv_fwd("pool_fwd", seq, c, ptaps, u0, 2 * c, live=_pool_live(c))
    z0, pm = _even_fwd("even_fwd", seq, a2, ws, u0, row(small["ln_g_e"]), row(small["ln_b_e"]),
                       wp, row(small["pool_scale_e"]))
    we["w_out"] = get_w("out_e", z0)[0]
    h1 = _mm_nn("mm_out_e", z0, we["w_out"], F32, res=x)
    sync("fwd_b", h1)
    n1 = _rms_fwd("rms_fwd_ffn0", h1, row(small["ffn_norm"][0]))
    wf0 = dict(zip(("w_gate", "w_up"), get_w("gu0", n1)))
    act0, ds0, s0 = _ffn_fwd("ffn0_fwd", n1, wf0["w_gate"], wf0["w_up"])
    dep = sync("fwd_c", act0)
    wf0["w_down"] = get_w("down0", act0)[0]
    h2 = _mm_nn("mm_down0", act0, wf0["w_down"], F32, res=h1, dep=dep)
    dep = sync("fwd_d", h2)
    n2 = _rms_fwd("rms_fwd_mix1", h2, row(small["mix_norm_o"]))
    wo = {"w_in": get_w("in_o", n2)[0]}
    u1 = _mm_nn("mm_in_o", n2, wo["w_in"], BF16, dep=dep)
    co, y1 = _conv_fwd("conv_o_fwd", seq, d, cw_o, u1, d, u1, 2 * d, "mul", post=u1, cpost=0)
    dep = sync("fwd_e", y1)
    wo["w_out"] = get_w("out_o", y1)[0]
    h3 = _mm_nn("mm_out_o", y1, wo["w_out"], F32, res=h2, dep=dep)
    sync("fwd_f", h3)
    n3 = _rms_fwd("rms_fwd_ffn1", h3, row(small["ffn_norm"][1]))
    wf1 = dict(zip(("w_gate", "w_up"), get_w("gu1", n3)))
    act1, ds1, s1 = _ffn_fwd("ffn1_fwd", n3, wf1["w_gate"], wf1["w_up"])
    wf1["w_down"] = get_w("down1", act1)[0]
    h4 = _mm_nn("mm_down1", act1, wf1["w_down"], F32, res=h3)

    dh4, dh4b, d_final, lsum = _loss_head("loss_head", h4, row(small["final_norm"]), tgt)

    def ffn_bwd(tag, dh, dhb, h_in, gain, n, dsilu, silu, act, w, dep):
        dg, dup = _ffn_bwd_act("ffn%s_bwd_act" % tag, dhb, w["w_down"], dsilu, silu, dep=dep)
        dwd = _mm_tn("mm_dwd%s" % tag, act, dhb, BF16)
        dwg = _mm_tn("mm_dwg%s" % tag, dg, n, BF16, dep=sync("bwd_ffn" + tag, dwd))
        dwu = _mm_tn("mm_dwu%s" % tag, dup, n, BF16)
        dn = _mm_nn("mm_ffn_dn%s" % tag, [dg, dup], [w["w_gate"], w["w_up"]], BF16)
        dh_in, dhb_in, dgain = _rms_bwd("rms_bwd_ffn%s" % tag, h_in, gain, dn, dh)
        dep = put_g("ffn" + tag, {"w_gate": dwg, "w_up": dwu, "w_down": dwd})
        return dh_in, dhb_in, dgain, dep

    dh3, dh3b, d_ffn1, dep = ffn_bwd("1", dh4, dh4b, h3, row(small["ffn_norm"][1]), n3, ds1, s1,
                                     act1, wf1, None)
    dw_out_o = _mm_tn("mm_dw_out_o", y1, dh3b, BF16, dep=dep)
    dy1 = _mm_nt("mm_dy_o", dh3b, wo["w_out"], BF16, dep=sync("bwd_mix_o", dw_out_o))
    dxc, dcw_o = _conv_bwd("conv_o_bwd", seq, d, cw_o, dy1, 0, u1, 0, "mul",
                           x1=u1, c1=d, x2=u1, c2=2 * d, pre="mul")
    du1 = _odd_du("odd_du", u1, dy1, co, dxc)
    dw_in_o = _mm_tn("mm_dw_in_o", n2, du1, BF16)
    dn2 = _mm_nt("mm_dn_o", du1, wo["w_in"], BF16)
    dh2, dh2b, d_mix_o = _rms_bwd("rms_bwd_mix1", h2, row(small["mix_norm_o"]), dn2, dh3)
    dep = put_g("mix_o", {"w_in": dw_in_o, "w_out": dw_out_o})

    dh1, dh1b, d_ffn0, dep = ffn_bwd("0", dh2, dh2b, h1, row(small["ffn_norm"][0]), n1, ds0, s0,
                                     act0, wf0, dep)
    dw_out_e = _mm_tn("mm_dw_out_e", z0, dh1b, BF16, dep=dep)
    dz0 = _mm_nt("mm_dz_e", dh1b, we["w_out"], BF16, dep=sync("bwd_mix_e", dw_out_e))
    da2, dws, dpm, vecs, dwp = _even_bwd("even_bwd", seq, dz0, a2, pm, row(small["ln_g_e"]),
                                         row(small["ln_b_e"]), wp, row(small["pool_scale_e"]))
    dep = put_g("small", {"conv_b_e": vecs[2], "ln_g_e": vecs[0], "ln_b_e": vecs[1],
                          "w_pool_e": dwp, "pool_scale_e": vecs[3], "mix_norm_o": d_mix_o[0],
                          "conv_w_o": dcw_o, "ffn_norm": jnp.concatenate([d_ffn0, d_ffn1], axis=0),
                          "final_norm": d_final[0], "loss": lsum})
    da1, dcw_e = _conv_bwd("conv_e_bwd", seq, c, cw_e, da2, 0, x1=u0, c1=0, x2=u0, c2=c, pre="glu", dep=dep)
    dbp = _conv_bwd("pool_bwd", seq, c, ptaps, dws, 0, live=_pool_live(c))
    du0 = _even_du("even_du", u0, da1, dbp, dpm)
    dw_in_e = _mm_tn("mm_dw_in_e", n0, du0, BF16)
    dep = put_g("mix_e", {"w_in": dw_in_e, "w_out": dw_out_e})
    dn0 = _mm_nt("mm_dn_e", du0, we["w_in"], BF16, dep=dep)
    dx, _, d_mix_e = _rms_bwd("rms_bwd_mix0", x, row(small["mix_norm_e"]), dn0, dh1)
    return dx, {"conv_w_e": dcw_e, "mix_norm_e": d_mix_e[0]}


def _place():
    x, y, c = (lax.axis_index(a) for a in MESH_AXES)
    return x, y, c


def _index(p):
    return 4 * p[0] + 2 * p[1] + p[2]


def _slab(ref, kind, d, n):
    if kind == "blk":
        return ref.at[d]
    return ref.at[:, pl.ds(pl.multiple_of(d * n, LANES), n)]


HBM = pl.BlockSpec(memory_space=pltpu.HBM)
SEM = pl.BlockSpec(memory_space=pltpu.SEMAPHORE)
EFFECT = pltpu.SideEffectType.DATAFLOW_SIDE_EFFECTING
NCHIPS = 4


def _in_hbm(a):
    return pltpu.with_memory_space_constraint(a, pltpu.HBM)


def _gathered_shape(s, kind):
    m, n = s.shape
    return (NDEV, m, n) if kind == "blk" else (m, NDEV * n)


def _first_targets():
    x, y, c = _place()
    return [(x, y, 1 - c), (1 - x, y, c), (x, 1 - y, c), (1 - x, 1 - y, c)]


def _gather_start(name, shards, kinds, after):
    na = len(shards)

    def body(*refs):
        x_refs, land_refs = refs[:na], refs[na:2 * na]
        send_sems, recv_sems = refs[2 * na + 1], refs[2 * na + 2]
        token = refs[-1]
        me = _index(_place())
        for a in range(na):
            for k, to in enumerate(_first_targets()):
                pltpu.make_async_remote_copy(
                    src_ref=x_refs[a], dst_ref=_slab(land_refs[a], kinds[a], me, shards[a].shape[1]),
                    send_sem=send_sems.at[4 * a + k], recv_sem=recv_sems.at[4 * a + k],
                    device_id=to, device_id_type=MESH).start()
        token[...] = jnp.zeros_like(token)

    lands = [lax.empty(_gathered_shape(s, k), s.dtype) for s, k in zip(shards, kinds)]
    outs = pl.pallas_call(
        body, name=name,
        out_shape=(pltpu.SemaphoreType.DMA((4 * na,)), pltpu.SemaphoreType.DMA((4 * na,)),
                   *[pltpu.HBM(s.shape, s.dtype) for s in shards],
                   *[pltpu.HBM(l.shape, l.dtype) for l in lands],
                   jax.ShapeDtypeStruct((SUBLANES, LANES), F32)),
        in_specs=[HBM] * (2 * na) + [ANY],
        out_specs=(SEM, SEM, *[HBM] * (2 * na), pl.BlockSpec(memory_space=pltpu.VMEM)),
        input_output_aliases={i: 2 + i for i in range(2 * na)},
        compiler_params=pltpu.CompilerParams(has_side_effects=EFFECT),
    )(*[_in_hbm(s) for s in shards], *[_in_hbm(l) for l in lands], after)
    return outs[0], outs[1], outs[2:2 + na], outs[2 + na:2 + 2 * na], outs[-1]


def _gather_wait(name, started, kinds, after):
    send_sems, recv_sems, shards, lands, _ = started
    na = len(shards)

    def body(*refs):
        x_refs, land_refs = refs[:na], refs[na:2 * na]
        s_sems, r_sems = refs[2 * na], refs[2 * na + 1]
        for a in range(na):
            for k, frm in enumerate(_first_targets()):
                cp = pltpu.make_async_remote_copy(
                    src_ref=x_refs[a],
                    dst_ref=_slab(land_refs[a], kinds[a], _index(frm), shards[a].shape[1]),
                    send_sem=s_sems.at[4 * a + k], recv_sem=r_sems.at[4 * a + k],
                    device_id=frm, device_id_type=MESH)
                cp.wait_send()
                cp.wait_recv()

    outs = pl.pallas_call(
        body, name=name,
        out_shape=(*[pltpu.HBM(s.shape, s.dtype) for s in shards],
                   *[pltpu.HBM(l.shape, l.dtype) for l in lands]),
        in_specs=[HBM] * (2 * na) + [SEM, SEM, ANY], out_specs=[HBM] * (2 * na),
        input_output_aliases={i: i for i in range(2 * na)},
        compiler_params=pltpu.CompilerParams(has_side_effects=EFFECT),
    )(*shards, *lands, send_sems, recv_sems, after)
    return outs[:na], outs[na:]


def _split_start(name, bufs, ncopies, plan, after):
    nb = len(bufs)

    def body(*refs):
        send_sems, recv_sems, token = refs[nb + 1], refs[nb + 2], refs[-1]
        for k, (src, dst, to, _) in enumerate(plan(refs[:nb])):
            pltpu.make_async_remote_copy(src_ref=src, dst_ref=dst, send_sem=send_sems.at[k],
                                         recv_sem=recv_sems.at[k], device_id=to, device_id_type=MESH).start()
        token[...] = jnp.zeros_like(token)

    outs = pl.pallas_call(
        body, name=name,
        out_shape=(pltpu.SemaphoreType.DMA((ncopies,)), pltpu.SemaphoreType.DMA((ncopies,)),
                   *[pltpu.HBM(b.shape, b.dtype) for b in bufs],
                   jax.ShapeDtypeStruct((SUBLANES, LANES), F32)),
        in_specs=[HBM] * nb + [ANY],
        out_specs=(SEM, SEM, *[HBM] * nb, pl.BlockSpec(memory_space=pltpu.VMEM)),
        input_output_aliases={i: 2 + i for i in range(nb)},
        compiler_params=pltpu.CompilerParams(has_side_effects=EFFECT),
    )(*[_in_hbm(b) for b in bufs], after)
    return outs[0], outs[1], list(outs[2:2 + nb]), outs[-1]


def _split_wait(name, started, plan, after):
    send_sems, recv_sems, bufs, _ = started
    nb = len(bufs)

    def body(*refs):
        s_sems, r_sems = refs[nb], refs[nb + 1]
        for k, (src, _, to, landed) in enumerate(plan(refs[:nb])):
            cp = pltpu.make_async_remote_copy(src_ref=src, dst_ref=landed, send_sem=s_sems.at[k],
                                              recv_sem=r_sems.at[k], device_id=to, device_id_type=MESH)
            cp.wait_send()
            cp.wait_recv()

    outs = pl.pallas_call(
        body, name=name, out_shape=tuple(pltpu.HBM(b.shape, b.dtype) for b in bufs),
        in_specs=[HBM] * nb + [SEM, SEM, ANY], out_specs=[HBM] * nb,
        input_output_aliases={i: i for i in range(nb)},
        compiler_params=pltpu.CompilerParams(has_side_effects=EFFECT),
    )(*bufs, send_sems, recv_sems, after)
    return list(outs)


def _forward_plan(kinds, nloc):
    def plan(lands):
        x, y, c = _place()
        out = []
        for a, land in enumerate(lands):
            for chip in [(1 - x, y), (x, 1 - y), (1 - x, 1 - y)]:
                mine = _slab(land, kinds[a], _index((*chip, c)), nloc[a])
                out.append((mine, mine, (x, y, 1 - c), _slab(land, kinds[a], _index((*chip, 1 - c)), nloc[a])))
        return out
    return plan


def _own_copy(name, shard, land, kind, me):
    m, n = shard.shape
    tr = _tile(m, max(SUBLANES, 1048576 // n), SUBLANES)

    def body(s_ref, x_ref, land_ref, o_ref):
        o_ref[...] = x_ref[...]

    if kind == "blk":
        o_spec = pl.BlockSpec((None, tr, n), lambda i, s: (s[0], i, 0))
    else:
        o_spec = pl.BlockSpec((tr, n), lambda i, s: (i, s[0]))
    return pl.pallas_call(
        body, name=name,
        grid_spec=pltpu.PrefetchScalarGridSpec(
            num_scalar_prefetch=1, grid=(m // tr,),
            in_specs=[pl.BlockSpec((tr, n), lambda i, s: (i, 0)), ANY], out_specs=o_spec),
        out_shape=jax.ShapeDtypeStruct(land.shape, land.dtype),
        input_output_aliases={2: 0}, compiler_params=_cp("parallel"))(me, shard, land)


def _everyone_plan(refs):
    x, y, c = _place()
    out = []
    for dx, dy, dc in [(a, b, e) for a in (0, 1) for b in (0, 1) for e in (0, 1)][1:]:
        peer = (x ^ dx, y ^ dy, c ^ dc)
        out.append((refs[0], refs[1].at[_index((x, y, c))], peer, refs[1].at[_index(peer)]))
    return out


def _pair_plan(kinds, nloc):
    na = len(kinds)

    def plan(refs):
        x, y, c = _place()
        out = []
        for a in range(na):
            for j in range(NCHIPS):
                dst = refs[na + a].at[j]
                out.append((_slab(refs[a], kinds[a], 2 * j + (1 - c), nloc[a]), dst, (x, y, 1 - c), dst))
        return out
    return plan


def _chip_sum(name, full, kind, n, from_sib, place):
    _, m, _ = from_sib.shape
    tr = _tile(m, max(SUBLANES, 1048576 // n), SUBLANES)

    def body(s_ref, mine_ref, sib_ref, csum_ref, land_ref):
        v = (mine_ref[...].astype(F32) + sib_ref[...].astype(F32)).astype(csum_ref.dtype)
        csum_ref[...] = v

        @pl.when(pl.program_id(1) == s_ref[1])
        def _():
            land_ref[...] = v

    if kind == "blk":
        mine_spec = pl.BlockSpec((None, tr, n), lambda i, j, s: (2 * j + s[0], i, 0))
    else:
        mine_spec = pl.BlockSpec((tr, n), lambda i, j, s: (i, 2 * j + s[0]))
    slot = pl.BlockSpec((None, tr, n), lambda i, j, s: (j, i, 0))
    shp = jax.ShapeDtypeStruct((NCHIPS, m, n), from_sib.dtype)
    return pl.pallas_call(
        body, name=name,
        grid_spec=pltpu.PrefetchScalarGridSpec(
            num_scalar_prefetch=1, grid=(m // tr, NCHIPS), in_specs=[mine_spec, slot],
            out_specs=[slot, pl.BlockSpec((None, tr, n), lambda i, j, s: (s[1], i, 0))]),
        out_shape=[shp, shp], compiler_params=_cp("parallel", "arbitrary"))(place, full, from_sib)


def _other_chips():
    x, y, c = _place()
    return [(1 - x, y, c), (x, 1 - y, c), (1 - x, 1 - y, c)]


def _scatter_start(name, csums, lands, after):
    na = len(csums)

    def body(*refs):
        c_refs, land_refs = refs[:na], refs[na:2 * na]
        send_sems, recv_sems = refs[2 * na + 1], refs[2 * na + 2]
        token = refs[-1]
        x, y, _ = _place()
        for a in range(na):
            for k, to in enumerate(_other_chips()):
                pltpu.make_async_remote_copy(
                    src_ref=c_refs[a].at[2 * to[0] + to[1]], dst_ref=land_refs[a].at[2 * x + y],
                    send_sem=send_sems.at[3 * a + k], recv_sem=recv_sems.at[3 * a + k],
                    device_id=to, device_id_type=MESH).start()
        token[...] = jnp.zeros_like(token)

    outs = pl.pallas_call(
        body, name=name,
        out_shape=(pltpu.SemaphoreType.DMA((3 * na,)), pltpu.SemaphoreType.DMA((3 * na,)),
                   *[pltpu.HBM(s.shape, s.dtype) for s in csums],
                   *[pltpu.HBM(l.shape, l.dtype) for l in lands],
                   jax.ShapeDtypeStruct((SUBLANES, LANES), F32)),
        in_specs=[HBM] * (2 * na) + [ANY],
        out_specs=(SEM, SEM, *[HBM] * (2 * na), pl.BlockSpec(memory_space=pltpu.VMEM)),
        input_output_aliases={i: 2 + i for i in range(2 * na)},
        compiler_params=pltpu.CompilerParams(has_side_effects=EFFECT),
    )(*[_in_hbm(s) for s in csums], *[_in_hbm(l) for l in lands], after)
    return outs[0], outs[1], outs[2:2 + na], outs[2 + na:2 + 2 * na], outs[-1]


def _scatter_wait(name, started, after):
    send_sems, recv_sems, csums, lands, _ = started
    na = len(csums)

    def body(*refs):
        c_refs, land_refs = refs[:na], refs[na:2 * na]
        s_sems, r_sems = refs[2 * na], refs[2 * na + 1]
        for a in range(na):
            for k, frm in enumerate(_other_chips()):
                cp = pltpu.make_async_remote_copy(
                    src_ref=c_refs[a].at[2 * frm[0] + frm[1]], dst_ref=land_refs[a].at[2 * frm[0] + frm[1]],
                    send_sem=s_sems.at[3 * a + k], recv_sem=r_sems.at[3 * a + k],
                    device_id=frm, device_id_type=MESH)
                cp.wait_send()
                cp.wait_recv()

    outs = pl.pallas_call(
        body, name=name,
        out_shape=(*[pltpu.HBM(s.shape, s.dtype) for s in csums],
                   *[pltpu.HBM(l.shape, l.dtype) for l in lands]),
        in_specs=[HBM] * (2 * na) + [SEM, SEM, ANY], out_specs=[HBM] * (2 * na),
        input_output_aliases={i: i for i in range(2 * na)},
        compiler_params=pltpu.CompilerParams(has_side_effects=EFFECT),
    )(*csums, *lands, send_sems, recv_sems, after)
    return outs[na:]


def _adam_math(w, g, m, v):
    m = ADAM_B1 * m + (1.0 - ADAM_B1) * g
    v = ADAM_B2 * v + (1.0 - ADAM_B2) * (g * g)
    m_hat = m / (1.0 - ADAM_B1 ** ADAM_STEP)
    v_hat = v / (1.0 - ADAM_B2 ** ADAM_STEP)
    delta = -ADAM_LR * (m_hat / (jnp.sqrt(v_hat) + ADAM_EPS) + ADAM_WD * w)
    return delta, m, v


def _sum_adamw(name, parts, w, m, v, layer, prev=None, dep=None):
    nl, r, c = w.shape
    nparts = parts.shape[0]
    tr = _tile(r, max(SUBLANES, 360448 // c), SUBLANES)

    def body(p_ref, w_ref, m_ref, v_ref, *rest):
        g_ref, d_ref, mo_ref, vo_ref = rest[-4:]
        g = p_ref[0].astype(F32)
        for s in range(1, nparts):
            g = g + p_ref[s].astype(F32)
        delta, mn, vn = _adam_math(w_ref[...], g, m_ref[...], v_ref[...])
        g_ref[...] = g
        d_ref[...] = delta
        mo_ref[...] = mn
        vo_ref[...] = vn

    row = pl.BlockSpec((None, tr, c), lambda i: (layer, i, 0))
    shp = jax.ShapeDtypeStruct((nl, r, c), F32)
    extra = ([] if prev is None else list(prev)) + ([] if dep is None else [dep])
    return pl.pallas_call(
        body, name=name, grid=(r // tr,),
        in_specs=[pl.BlockSpec((nparts, tr, c), lambda i: (0, i, 0)), row, row, row] + [ANY] * len(extra),
        out_specs=[row, row, row, row], out_shape=[shp, shp, shp, shp],
        input_output_aliases={} if prev is None else {4 + i: i for i in range(4)},
        compiler_params=_cp("parallel"))(parts, w, m, v, *extra)


def _sum_parts(name, parts):
    _, r, c = parts.shape

    def body(p_ref, o_ref):
        g = p_ref[0]
        for s in range(1, NDEV):
            g = g + p_ref[s]
        o_ref[...] = g

    return pl.pallas_call(
        body, name=name, grid=(1,),
        in_specs=[pl.BlockSpec((NDEV, r, c), lambda i: (0, 0, 0))],
        out_specs=pl.BlockSpec((r, c), lambda i: (0, 0)),
        out_shape=jax.ShapeDtypeStruct((r, c), F32), compiler_params=_cp("arbitrary"))(parts)


def _adamw(name, w, g, m, v):
    r, c = w.shape

    def body(w_ref, g_ref, m_ref, v_ref, d_ref, mo_ref, vo_ref):
        delta, mn, vn = _adam_math(w_ref[...], g_ref[...], m_ref[...], v_ref[...])
        d_ref[...] = delta
        mo_ref[...] = mn
        vo_ref[...] = vn

    full = pl.BlockSpec((r, c), lambda i: (0, 0))
    shp = jax.ShapeDtypeStruct((r, c), F32)
    return pl.pallas_call(
        body, name=name, grid=(1,), in_specs=[full] * 4, out_specs=[full] * 3,
        out_shape=[shp] * 3, compiler_params=_cp("arbitrary"))(w, g, m, v)


def _pack(arrays):
    flat = jnp.concatenate([a.reshape(-1) for a in arrays])
    unit = SUBLANES * LANES
    pad = (-flat.shape[0]) % unit
    return jnp.pad(flat, (0, pad)).reshape(-1, LANES)


def _unpack(buf, shapes):
    flat = buf.reshape(-1)
    out, off = [], 0
    for shp in shapes:
        size = 1
        for s in shp:
            size *= s
        out.append(flat[off:off + size].reshape(shp))
        off += size
    return out


WEIGHTS = ["mix_norm_e", "w_in_e", "conv_w_e", "conv_b_e", "ln_g_e", "ln_b_e", "w_pool_e",
           "pool_scale_e", "w_out_e", "mix_norm_o", "w_in_o", "conv_w_o", "w_out_o", "ffn_norm",
           "w_gate", "w_up", "w_down", "final_norm"]
BIG = ["w_in_e", "w_out_e", "w_in_o", "w_out_o", "w_gate", "w_up", "w_down"]
SHARDED_SMALL = {"conv_w_e": 1, "w_pool_e": 1, "mix_norm_o": 0, "conv_w_o": 1}
SMALL = [n for n in WEIGHTS if n not in BIG]


def kernel(x, mix_norm_e, w_in_e, conv_w_e, conv_b_e, ln_g_e, ln_b_e, w_pool_e, pool_scale_e, w_out_e, mix_norm_o, w_in_o, conv_w_o, w_out_o, ffn_norm, w_gate, w_up, w_down, final_norm, loss_target, m_mix_norm_e, m_w_in_e, m_conv_w_e, m_conv_b_e, m_ln_g_e, m_ln_b_e, m_w_pool_e, m_pool_scale_e, m_w_out_e, m_mix_norm_o, m_w_in_o, m_conv_w_o, m_w_out_o, m_ffn_norm, m_w_gate, m_w_up, m_w_down, m_final_norm, v_mix_norm_e, v_w_in_e, v_conv_w_e, v_conv_b_e, v_ln_g_e, v_ln_b_e, v_w_pool_e, v_pool_scale_e, v_w_out_e, v_mix_norm_o, v_w_in_o, v_conv_w_o, v_w_out_o, v_ffn_norm, v_w_gate, v_w_up, v_w_down, v_final_norm):
    wts = dict(zip(WEIGHTS, [mix_norm_e, w_in_e, conv_w_e, conv_b_e, ln_g_e, ln_b_e, w_pool_e, pool_scale_e, w_out_e, mix_norm_o, w_in_o, conv_w_o, w_out_o, ffn_norm, w_gate, w_up, w_down, final_norm]))
    mom = dict(zip(WEIGHTS, [m_mix_norm_e, m_w_in_e, m_conv_w_e, m_conv_b_e, m_ln_g_e, m_ln_b_e, m_w_pool_e, m_pool_scale_e, m_w_out_e, m_mix_norm_o, m_w_in_o, m_conv_w_o, m_w_out_o, m_ffn_norm, m_w_gate, m_w_up, m_w_down, m_final_norm]))
    var = dict(zip(WEIGHTS, [v_mix_norm_e, v_w_in_e, v_conv_w_e, v_conv_b_e, v_ln_g_e, v_ln_b_e, v_w_pool_e, v_pool_scale_e, v_w_out_e, v_mix_norm_o, v_w_in_o, v_conv_w_o, v_w_out_o, v_ffn_norm, v_w_gate, v_w_up, v_w_down, v_final_norm]))
    bsz, seq, d = x.shape
    t = bsz * seq
    me = _index(_place())
    me_arr = jnp.reshape(me, (1,)).astype(jnp.int32)

    sh_names = list(SHARDED_SMALL)
    sh_local = [wts[n][0] for n in sh_names]
    packed = _pack(sh_local)
    params_st = _split_start("small_params_start", [packed, lax.empty((NDEV,) + packed.shape, F32)], NDEV - 1,
                             _everyone_plan, x)

    for state in (wts, mom, var):
        for n in ("w_gate", "w_up"):
            state[n] = jnp.swapaxes(state[n], 1, 2)
    bf = lambda a: a.astype(BF16)
    mix_kinds, ffn_kinds = ["col", "blk"], ["blk", "blk", "blk"]
    ffn_names = ("w_gate", "w_up", "w_down")
    groups = {
        "mix_e": ([w_in_e.shape[2], d], mix_kinds, [("w_in_e", 0), ("w_out_e", 0)]),
        "ffn0": ([d, d, d], ffn_kinds, [(n, 0) for n in ffn_names]),
        "mix_o": ([w_in_o.shape[2], d], mix_kinds, [("w_in_o", 0), ("w_out_o", 0)]),
        "ffn1": ([d, d, d], ffn_kinds, [(n, 1) for n in ffn_names]),
    }
    gathers = {
        "in_e": ([bf(w_in_e[0])], ["col"]), "out_e": ([bf(w_out_e[0])], ["blk"]),
        "gu0": ([bf(wts["w_gate"][0]), bf(wts["w_up"][0])], ["blk", "blk"]), "down0": ([bf(w_down[0])], ["blk"]),
        "in_o": ([bf(w_in_o[0])], ["col"]), "out_o": ([bf(w_out_o[0])], ["blk"]),
        "gu1": ([bf(wts["w_gate"][1]), bf(wts["w_up"][1])], ["blk", "blk"]), "down1": ([bf(w_down[1])], ["blk"]),
    }
    started, prev = {}, params_st[3]
    for grp, (shards, kinds) in gathers.items():
        started[grp] = _gather_start("gather_start_" + grp, shards, kinds, prev)
        prev = started[grp][4]
    all_started = prev[0, 0:1]

    bufs = _split_wait("small_params_wait", params_st, _everyone_plan, prev)
    gathered = _own_copy("small_params_own", bufs[0], bufs[1], "blk", me_arr)
    small = {n: wts[n][0] for n in SMALL if n not in SHARDED_SMALL and n not in ("ffn_norm", "final_norm")}
    small["ffn_norm"], small["final_norm"] = ffn_norm, final_norm
    flat, off = gathered.reshape(NDEV, -1), 0
    for n, a in zip(sh_names, sh_local):
        ax, shp = SHARDED_SMALL[n], a.shape
        blocks = jnp.moveaxis(flat[:, off:off + a.size].reshape((NDEV,) + shp), 0, ax)
        small[n] = blocks.reshape(shp[:ax] + (NDEV * shp[ax],) + shp[ax + 1:])
        off += a.size

    passing, shards_of = {}, {}

    def pass_on(grp, after):
        shards, kinds = gathers[grp]
        shards_of[grp], lands = _gather_wait("gather_wait_" + grp, started[grp], kinds, after)
        plan = _forward_plan(kinds, [s.shape[1] for s in shards])
        passing[grp] = (_split_start("forward_start_" + grp, lands, 3 * len(lands), plan, after), plan)
        return passing[grp][0][3]

    def get_w(grp, after):
        if grp not in passing:
            after = pass_on(grp, after)
        st, plan = passing[grp]
        lands = _split_wait("forward_wait_" + grp, st, plan, after)
        full = [_own_copy("own_copy_%s%d" % (grp, a), shards_of[grp][a], lands[a], gathers[grp][1][a], me_arr)
                for a in range(len(lands))]
        return [f.reshape(-1, d) if kind == "blk" else f for f, kind in zip(full, gathers[grp][1])]

    cx, cy, cc = _place()
    place = jnp.stack([cc, 2 * cx + cy]).astype(jnp.int32)
    bwd_order = ["ffn1", "mix_o", "ffn0", "mix_e"]
    pairing, pending, results = {}, {}, {}

    late_names = ["conv_w_e", "mix_norm_e"]
    early_names = [n for n in SMALL if n not in late_names] + ["loss"]
    small_sent = {}

    def send_small(tag, arrays, after):
        mine = _pack(arrays)
        small_sent[tag] = _split_start(tag + "_start", [mine, lax.empty((NDEV,) + mine.shape, F32)], NDEV - 1,
                                       _everyone_plan, after)
        return small_sent[tag][3]

    def summed_small(tag, shapes, after):
        bufs = _split_wait(tag + "_wait", small_sent[tag], _everyone_plan, after)
        parts = _own_copy(tag + "_own", bufs[0], bufs[1], "blk", me_arr)
        return _unpack(_sum_parts(tag + "_sum", parts), shapes)

    def put_g(grp, grads):
        if grp == "small":
            small_sent["shapes"] = [grads[n].shape for n in early_names]
            return send_small("small_grads", [grads[n] for n in early_names], place)
        nloc, kinds, _ = groups[grp]
        if len(kinds) == 2:
            fulls = [grads["w_in"], grads["w_out"].reshape(NDEV, -1, d)]
        else:
            fulls = [grads[n].reshape(NDEV, -1, d) for n in ffn_names]
        empties = []
        for g, kind, n in zip(fulls, kinds, nloc):
            empties.append(lax.empty((NCHIPS, g.shape[1] if kind == "blk" else g.shape[0], n), g.dtype))
        plan = _pair_plan(kinds, nloc)
        pairing[grp] = (_split_start("pair_start_" + grp, fulls + empties, NCHIPS * len(fulls), plan, place),
                        plan, kinds, nloc)
        token = pairing[grp][0][3]
        return send_sums(grp, token) if grp == bwd_order[-1] else token

    def send_sums(grp, after):
        st, plan, kinds, nloc = pairing[grp]
        bufs = _split_wait("pair_wait_" + grp, st, plan, after)
        na = len(kinds)
        sums = [_chip_sum("chip_sum_%s%d" % (grp, a), bufs[a], kinds[a], nloc[a], bufs[na + a], place)
                for a in range(na)]
        pending[grp] = _scatter_start("scatter_start_" + grp, [s[0] for s in sums], [s[1] for s in sums], after)
        return pending[grp][4]

    def finish(grp, after):
        lands = _scatter_wait("scatter_wait_" + grp, pending[grp], after)
        dep = None
        for (n, l), parts in zip(groups[grp][2], lands):
            results[n] = _sum_adamw("adamw_%s%d" % (n, l), parts, wts[n], mom[n], var[n], l, results.get(n), dep)
            dep = results[n][1]
        return dep

    fwd_sync = {"fwd_a": ["out_e"], "fwd_b": ["gu0"], "fwd_c": ["down0", "in_o"], "fwd_d": ["out_o"],
                "fwd_e": ["gu1"], "fwd_f": ["down1"]}

    def sync(tag, after):
        if tag in fwd_sync:
            for grp in fwd_sync[tag]:
                after = pass_on(grp, after)
            return after
        if tag == "bwd_mix_o":
            return send_sums("ffn1", after)
        if tag == "bwd_ffn0":
            return finish("ffn1", send_sums("mix_o", after))
        if tag == "bwd_mix_e":
            return finish("mix_o", send_sums("ffn0", after))
        return None

    small["mix_norm_e"] = small["mix_norm_e"] + all_started
    dx, late = _local_step(x.reshape(t, d), loss_target.reshape(t, d), seq, small, get_w, put_g, sync)

    out_g, out_d, out_m, out_v = {}, {}, {}, {}

    dep = finish("ffn0", send_small("last_grads", [late[n] for n in late_names], dx))
    sums = dict(zip(early_names, summed_small("small_grads", small_sent["shapes"], dep)))
    sums.update(zip(late_names, summed_small("last_grads", [late[n].shape for n in late_names], dep)))
    loss = jnp.sum(sums["loss"])
    gs_sum = [sums[n] for n in SMALL]
    local_g = []
    for n, g in zip(SMALL, gs_sum):
        if n in SHARDED_SMALL:
            ax = SHARDED_SMALL[n]
            size = wts[n].shape[ax + 1]
            g = lax.dynamic_slice_in_dim(g, me * size, size, axis=ax)
        local_g.append(g.reshape(wts[n].shape))
    shapes = [wts[n].shape for n in SMALL]
    upd = _adamw("adamw_small", _pack([wts[n] for n in SMALL]), _pack(local_g),
                 _pack([mom[n] for n in SMALL]), _pack([var[n] for n in SMALL]))
    for i, outd in enumerate((out_d, out_m, out_v)):
        for n, a in zip(SMALL, _unpack(upd[i], shapes)):
            outd[n] = a
    for n, g in zip(SMALL, local_g):
        out_g[n] = g

    finish("mix_e", upd[0])
    for n in BIG:
        res = [jnp.swapaxes(a, 1, 2) for a in results[n]] if n in ("w_gate", "w_up") else results[n]
        out_g[n], out_d[n], out_m[n], out_v[n] = res

    return (loss, dx.reshape(bsz, seq, d), *[out_g[n] for n in WEIGHTS], *[out_d[n] for n in WEIGHTS],
            *[out_m[n] for n in WEIGHTS], *[out_v[n] for n in WEIGHTS])
```

```python
import jax
import jax.numpy as jnp
from jax import lax
from jax.experimental import pallas as pl
from jax.experimental.pallas import tpu as pltpu

F32 = jnp.float32
BF16 = jnp.bfloat16
NDEV = 8
MESH_AXES = ("x", "y", "c")
EPS = 1e-6
POOL_WINDOWS = (2, 4, 8, 16)
ADAM_LR = 0.001
ADAM_B1 = 0.9
ADAM_B2 = 0.999
ADAM_EPS = 1e-08
ADAM_WD = 0.01
ADAM_STEP = 10
LANES = 128
SUBLANES = 8
VMEM_LIMIT = 56 * 1024 * 1024
MXU_DEPTH = 256
MM_TK = 2816
MESH = pl.DeviceIdType.MESH
ANY = pl.BlockSpec(memory_space=pl.ANY)


def _cp(*sem):
    return pltpu.CompilerParams(dimension_semantics=sem, vmem_limit_bytes=VMEM_LIMIT)


def _tile(n, pref, unit=LANES):
    if n <= pref:
        return n
    t = (pref // unit) * unit
    while t > unit and n % t:
        t -= unit
    assert n % t == 0, (n, pref)
    return t


def _sigmoid(v):
    return 0.5 * jnp.tanh(0.5 * v) + 0.5


def _mm(name, pairs, a_specs, b_specs, dims, out_shape, o_spec, grid, acc_shape,
        res=None, res_spec=None, dep=None):
    np_ = len(pairs)
    nk = grid[2]
    has_res = res is not None
    n_in = 2 * np_ + (1 if has_res else 0) + (0 if dep is None else 1)

    def body(*refs):
        a_refs = refs[:np_]
        b_refs = refs[np_:2 * np_]
        r_ref = refs[2 * np_] if has_res else None
        o_ref = refs[n_in]
        acc = refs[-1]

        def part():
            s = None
            for a_ref, b_ref in zip(a_refs, b_refs):
                d = lax.dot_general(a_ref[...], b_ref[...], dims, preferred_element_type=F32)
                s = d if s is None else s + d
            return s

        def finish(v):
            if has_res:
                v = v + r_ref[...]
            o_ref[...] = v.astype(o_ref.dtype)

        if nk == 1:
            finish(part())
        else:
            k = pl.program_id(2)

            @pl.when(k == 0)
            def _():
                acc[...] = part()

            @pl.when((k > 0) & (k < nk - 1))
            def _():
                acc[...] += part()

            @pl.when(k == nk - 1)
            def _():
                finish(acc[...] + part())

    ins = [p[0] for p in pairs] + [p[1] for p in pairs]
    specs = list(a_specs) + list(b_specs)
    if has_res:
        ins.append(res)
        specs.append(res_spec)
    if dep is not None:
        ins.append(dep)
        specs.append(ANY)
    return pl.pallas_call(
        body, name=name, grid=grid, in_specs=specs, out_specs=o_spec, out_shape=out_shape,
        scratch_shapes=[pltpu.VMEM(acc_shape if nk > 1 else (SUBLANES, LANES), F32)],
        compiler_params=_cp("parallel", "parallel", "arbitrary"))(*ins)


NN = (((1,), (0,)), ((), ()))
NT = (((1,), (1,)), ((), ()))
TN = (((0,), (0,)), ((), ()))


def _tiles_mk(m, kk):
    return _tile(m, 1024), _tile(kk, MM_TK, MXU_DEPTH)


def _mm_nn(name, a, b, out_dtype, res=None, dep=None):
    pairs = list(zip(a, b)) if isinstance(a, (list, tuple)) else [(a, b)]
    m, kk = pairs[0][0].shape
    n = pairs[0][1].shape[1]
    tm, tk = _tiles_mk(m, kk)
    tn = _tile(n, 1024 if tk * len(pairs) <= MM_TK else 512)
    return _mm(name, pairs,
               [pl.BlockSpec((tm, tk), lambda i, j, k: (i, k))] * len(pairs),
               [pl.BlockSpec((tk, tn), lambda i, j, k: (k, j))] * len(pairs), NN,
               jax.ShapeDtypeStruct((m, n), out_dtype),
               pl.BlockSpec((tm, tn), lambda i, j, k: (i, j)),
               (m // tm, n // tn, kk // tk), (tm, tn), res,
               pl.BlockSpec((tm, tn), lambda i, j, k: (i, j)), dep=dep)


def _mm_nt(name, a, b, out_dtype, dep=None):
    m, n = a.shape
    kk = b.shape[0]
    tn = _tile(kk, 1024)
    tm, tk = _tiles_mk(m, n)
    return _mm(name, [(a, b)],
               [pl.BlockSpec((tm, tk), lambda i, j, k: (i, k))],
               [pl.BlockSpec((tn, tk), lambda i, j, k: (j, k))], NT,
               jax.ShapeDtypeStruct((m, kk), out_dtype),
               pl.BlockSpec((tm, tn), lambda i, j, k: (i, j)),
               (m // tm, kk // tn, n // tk), (tm, tn), dep=dep)


def _mm_tn(name, a, b, out_dtype, dep=None):
    t, m = a.shape
    n = b.shape[1]
    tn = _tile(n, 1024)
    tm, tk = _tile(m, 1408), _tile(t, MM_TK, MXU_DEPTH)
    return _mm(name, [(a, b)],
               [pl.BlockSpec((tk, tm), lambda i, j, k: (k, i))],
               [pl.BlockSpec((tk, tn), lambda i, j, k: (k, j))], TN,
               jax.ShapeDtypeStruct((m, n), out_dtype),
               pl.BlockSpec((tm, tn), lambda i, j, k: (i, j)),
               (m // tm, n // tn, t // tk), (tm, tn), dep=dep)


def _ffn_fwd(name, n, wg, wu, dep=None):
    f, d = wg.shape
    t = n.shape[0]
    tm, tn = _tile(t, 1024), _tile(f, 512)

    def body(n_ref, wg_ref, wu_ref, *rest):
        act_ref, ds_ref, s_ref = rest[-3:]
        nv = n_ref[...]
        g = lax.dot_general(nv, wg_ref[...], NT, preferred_element_type=F32)
        up = lax.dot_general(nv, wu_ref[...], NT, preferred_element_type=F32)
        sg = _sigmoid(g)
        silu = g * sg
        act_ref[...] = (silu * up).astype(BF16)
        ds_ref[...] = (up * (sg * (1.0 + g * (1.0 - sg)))).astype(BF16)
        s_ref[...] = silu.astype(BF16)

    w_spec = pl.BlockSpec((tn, d), lambda j, i: (j, 0))
    o_spec = pl.BlockSpec((tm, tn), lambda j, i: (i, j))
    shp = jax.ShapeDtypeStruct((t, f), BF16)
    return pl.pallas_call(
        body, name=name, grid=(f // tn, t // tm),
        in_specs=[pl.BlockSpec((tm, d), lambda j, i: (i, 0)), w_spec, w_spec] + ([] if dep is None else [ANY]),
        out_specs=[o_spec, o_spec, o_spec], out_shape=[shp, shp, shp],
        compiler_params=_cp("parallel", "parallel"))(n, wg, wu, *([] if dep is None else [dep]))


def _ffn_bwd_act(name, dh, wd, dsilu, silu, dep=None):
    f, d = wd.shape
    t = dh.shape[0]
    tm, tn = _tile(t, 1024), _tile(f, 512)

    def body(dh_ref, wd_ref, ds_ref, s_ref, *rest):
        dg_ref, dup_ref = rest[-2:]
        da = lax.dot_general(dh_ref[...], wd_ref[...], NT, preferred_element_type=F32)
        dg_ref[...] = (da * ds_ref[...].astype(F32)).astype(BF16)
        dup_ref[...] = (da * s_ref[...].astype(F32)).astype(BF16)

    o_spec = pl.BlockSpec((tm, tn), lambda i, j: (i, j))
    shp = jax.ShapeDtypeStruct((t, f), BF16)
    return pl.pallas_call(
        body, name=name, grid=(t // tm, f // tn),
        in_specs=[pl.BlockSpec((tm, d), lambda i, j: (i, 0)),
                  pl.BlockSpec((tn, d), lambda i, j: (j, 0)), o_spec, o_spec]
        + ([] if dep is None else [ANY]),
        out_specs=[o_spec, o_spec], out_shape=[shp, shp],
        compiler_params=_cp("parallel", "parallel"))(dh, wd, dsilu, silu, *([] if dep is None else [dep]))


def _rms_fwd(name, h, gain):
    t, d = h.shape
    tr = _tile(t, 512, SUBLANES)

    def body(h_ref, g_ref, n_ref):
        hv = h_ref[...]
        r = lax.rsqrt(jnp.mean(hv * hv, axis=-1, keepdims=True) + EPS)
        n_ref[...] = (hv * r * g_ref[...]).astype(BF16)

    return pl.pallas_call(
        body, name=name, grid=(t // tr,),
        in_specs=[pl.BlockSpec((tr, d), lambda i: (i, 0)), pl.BlockSpec((1, d), lambda i: (0, 0))],
        out_specs=pl.BlockSpec((tr, d), lambda i: (i, 0)),
        out_shape=jax.ShapeDtypeStruct((t, d), BF16),
        compiler_params=_cp("parallel"))(h, gain)


def _rms_bwd_math(hv, gain, dn):
    d = hv.shape[-1]
    r = lax.rsqrt(jnp.mean(hv * hv, axis=-1, keepdims=True) + EPS)
    xhat = hv * r
    dxh = dn * gain
    dh = r * (dxh - xhat * (jnp.sum(dxh * xhat, axis=-1, keepdims=True) / d))
    dgain = jnp.sum(dn * xhat, axis=0, keepdims=True)
    return dh, dgain


def _rms_bwd(name, h, gain, dn, dres, bf16_too=True):
    t, d = h.shape
    tr = _tile(t, 256, SUBLANES)

    def body(h_ref, g_ref, dn_ref, dr_ref, dh_ref, *rest):
        dg_ref = rest[-1]
        dh, dgain = _rms_bwd_math(h_ref[...], g_ref[...], dn_ref[...].astype(F32))
        dh = dh + dr_ref[...]
        dh_ref[...] = dh
        if bf16_too:
            rest[0][...] = dh.astype(BF16)

        @pl.when(pl.program_id(0) == 0)
        def _():
            dg_ref[...] = dgain

        @pl.when(pl.program_id(0) > 0)
        def _():
            dg_ref[...] += dgain

    row = pl.BlockSpec((tr, d), lambda i: (i, 0))
    vec = pl.BlockSpec((1, d), lambda i: (0, 0))
    halves = [jax.ShapeDtypeStruct((t, d), BF16)] if bf16_too else []
    out = pl.pallas_call(
        body, name=name, grid=(t // tr,), in_specs=[row, vec, row, row],
        out_specs=[row] + [row] * len(halves) + [vec],
        out_shape=[jax.ShapeDtypeStruct((t, d), F32)] + halves + [jax.ShapeDtypeStruct((1, d), F32)],
        compiler_params=_cp("arbitrary"))(h, gain, dn, dres)
    return (out[0], out[1], out[2]) if bf16_too else (out[0], None, out[1])


def _loss_head(name, h, gain, tgt):
    t, d = h.shape
    tr = _tile(t, 256, SUBLANES)

    def body(h_ref, g_ref, t_ref, dh_ref, dhb_ref, dg_ref, ls_ref):
        hv = h_ref[...]
        gv = g_ref[...]
        r = lax.rsqrt(jnp.mean(hv * hv, axis=-1, keepdims=True) + EPS)
        err = hv * r * gv - t_ref[...]
        lsum = 0.5 * jnp.sum(err * err, axis=0, keepdims=True) / d
        dh, dgain = _rms_bwd_math(hv, gv, err / d)
        dh_ref[...] = dh
        dhb_ref[...] = dh.astype(BF16)

        @pl.when(pl.program_id(0) == 0)
        def _():
            dg_ref[...] = dgain
            ls_ref[...] = lsum

        @pl.when(pl.program_id(0) > 0)
        def _():
            dg_ref[...] += dgain
            ls_ref[...] += lsum

    row = pl.BlockSpec((tr, d), lambda i: (i, 0))
    vec = pl.BlockSpec((1, d), lambda i: (0, 0))
    return pl.pallas_call(
        body, name=name, grid=(t // tr,), in_specs=[row, vec, row],
        out_specs=[row, row, vec, vec],
        out_shape=[jax.ShapeDtypeStruct((t, d), F32), jax.ShapeDtypeStruct((t, d), BF16),
                   jax.ShapeDtypeStruct((1, d), F32), jax.ShapeDtypeStruct((1, d), F32)],
        compiler_params=_cp("arbitrary"))(h, gain, tgt)


def _conv_geom(t, seq, c, k, full_width=False):
    halo = 32 if k - 1 > SUBLANES else SUBLANES
    assert k - 1 <= halo
    tm = min(256 if halo > SUBLANES else 1024, seq // 2)
    tc = c if full_width else min(512, c)
    assert seq % tm == 0 and tm % halo == 0 and c % tc == 0 and t % seq == 0
    return halo, tm, tc, min(64 if halo > SUBLANES else 128, tm), min(LANES, tc)


def _pre(kind, a, b):
    if kind == "glu":
        return a * _sigmoid(b)
    if kind == "mul":
        return a * b
    return a


def _taps(k):
    return sorted((s % SUBLANES, s // SUBLANES, s) for s in range(k))


def _conv_fwd(name, seq, c, w, x1, c1, x2=None, c2=0, pre=None, bias=None, post=None, cpost=0, live=None):
    t = x1.shape[0]
    k = w.shape[0]
    halo, tm, tc, sr, sl = _conv_geom(t, seq, c, k, live is not None)
    nb, cps = tm // halo, seq // tm
    two = x2 is not None
    has_bias, has_post = bias is not None, post is not None

    def body(*refs):
        it = iter(refs)
        x1c, x1h = next(it), next(it)
        x2c, x2h = (next(it), next(it)) if two else (None, None)
        w_ref = next(it)
        b_ref = next(it) if has_bias else None
        p_ref = next(it) if has_post else None
        o_ref = next(it)
        y_ref = next(it) if has_post else None
        xs = next(it)
        first = (pl.program_id(1) % cps) == 0
        hv = _pre(pre, x1h[...].astype(F32), x2h[...].astype(F32) if two else None)
        xs[0:halo, :] = jnp.where(first, 0.0, hv)
        xs[halo:halo + tm, :] = _pre(pre, x1c[...].astype(F32), x2c[...].astype(F32) if two else None)
        for l0 in range(0, tc, sl):
            ls = slice(l0, l0 + sl)
            for r0 in range(0, tm, sr):
                win = xs[r0:r0 + sr + halo, ls]
                acc = jnp.zeros((sr, sl), F32)
                rolled = {}
                for r, q, s in _taps(k if live is None else live[l0 // sl]):
                    if r not in rolled:
                        rolled[r] = win if r == 0 else pltpu.roll(win, r, 0)
                    lo = halo - SUBLANES * q
                    acc = acc + w_ref[k - 1 - s:k - s, ls] * rolled[r][lo:lo + sr]
                if has_bias:
                    acc = acc + b_ref[:, ls]
                o_ref[r0:r0 + sr, ls] = acc.astype(o_ref.dtype)
                if has_post:
                    y_ref[r0:r0 + sr, ls] = (acc * p_ref[r0:r0 + sr, ls].astype(F32)).astype(y_ref.dtype)

    def cur(off):
        return pl.BlockSpec((tm, tc), lambda j, i: (i, off // tc + j))

    def prev(off):
        return pl.BlockSpec((halo, tc), lambda j, i: (jnp.maximum(i * nb - 1, 0), off // tc + j))

    ins, specs = [x1, x1], [cur(c1), prev(c1)]
    if two:
        ins += [x2, x2]
        specs += [cur(c2), prev(c2)]
    ins.append(w)
    specs.append(pl.BlockSpec((k, tc), lambda j, i: (0, j)))
    if has_bias:
        ins.append(bias)
        specs.append(pl.BlockSpec((1, tc), lambda j, i: (0, j)))
    if has_post:
        ins.append(post)
        specs.append(cur(cpost))
    o_spec = pl.BlockSpec((tm, tc), lambda j, i: (i, j))
    shp = jax.ShapeDtypeStruct((t, c), BF16)
    return pl.pallas_call(
        body, name=name, grid=(c // tc, t // tm), in_specs=specs,
        out_specs=[o_spec, o_spec] if has_post else o_spec,
        out_shape=[shp, shp] if has_post else shp,
        scratch_shapes=[pltpu.VMEM((halo + tm, tc), F32)],
        compiler_params=_cp("parallel", "parallel"))(*ins)


def _conv_bwd(name, seq, c, w, d1, cd1, d2=None, cd2=0, dpre=None,
              x1=None, c1=0, x2=None, c2=0, pre=None, live=None, dep=None):
    t = d1.shape[0]
    k = w.shape[0]
    assert live is None or x1 is None
    halo, tm, tc, sr, sl = _conv_geom(t, seq, c, k, live is not None)
    nb, cps = tm // halo, seq // tm
    nchunks = t // tm
    dtwo, xtwo, has_x = d2 is not None, x2 is not None, x1 is not None

    def body(*refs):
        it = iter(refs)
        d1c, d1n = next(it), next(it)
        d2c, d2n = (next(it), next(it)) if dtwo else (None, None)
        x1c, x1h = (next(it), next(it)) if has_x else (None, None)
        x2c, x2h = (next(it), next(it)) if xtwo else (None, None)
        w_ref = next(it)
        if dep is not None:
            next(it)
        dx_ref = next(it)
        dw_ref = next(it) if has_x else None
        ds = next(it)
        xs = next(it) if has_x else None
        i = pl.program_id(1)
        last = (i % cps) == cps - 1
        ds[0:tm, :] = _pre(dpre, d1c[...].astype(F32), d2c[...].astype(F32) if dtwo else None)
        nv = _pre(dpre, d1n[...].astype(F32), d2n[...].astype(F32) if dtwo else None)
        ds[tm:tm + halo, :] = jnp.where(last, 0.0, nv)
        if has_x:
            first = (i % cps) == 0
            hv = _pre(pre, x1h[...].astype(F32), x2h[...].astype(F32) if xtwo else None)
            xs[0:halo, :] = jnp.where(first, 0.0, hv)
            xs[halo:halo + tm, :] = _pre(pre, x1c[...].astype(F32), x2c[...].astype(F32) if xtwo else None)

            @pl.when(i == 0)
            def _():
                dw_ref[...] = jnp.zeros_like(dw_ref)

        for l0 in range(0, tc, sl):
            ls = slice(l0, l0 + sl)
            for r0 in range(0, tm, sr):
                win = ds[r0:r0 + sr + halo, ls]
                nrow = sr + halo
                acc = jnp.zeros((sr, sl), F32)
                rolled = {}
                for r, q, s in _taps(k if live is None else live[l0 // sl]):
                    if r not in rolled:
                        rolled[r] = win if r == 0 else pltpu.roll(win, nrow - r, 0)
                    lo = SUBLANES * q
                    acc = acc + w_ref[k - 1 - s:k - s, ls] * rolled[r][lo:lo + sr]
                dx_ref[r0:r0 + sr, ls] = acc.astype(dx_ref.dtype)
                if has_x:
                    dcur = win[0:sr]
                    xwin = xs[r0:r0 + sr + halo, ls]
                    xrolled = {}
                    for r, q, s in _taps(k):
                        if r not in xrolled:
                            xrolled[r] = xwin if r == 0 else pltpu.roll(xwin, r, 0)
                        lo = halo - SUBLANES * q
                        part = jnp.sum(dcur * xrolled[r][lo:lo + sr], axis=0, keepdims=True)
                        dw_ref[k - 1 - s:k - s, ls] += part

    def cur(off):
        return pl.BlockSpec((tm, tc), lambda j, i: (i, off // tc + j))

    def prev(off):
        return pl.BlockSpec((halo, tc), lambda j, i: (jnp.maximum(i * nb - 1, 0), off // tc + j))

    def nxt(off):
        return pl.BlockSpec((halo, tc),
                            lambda j, i: (jnp.minimum((i + 1) * nb, nchunks * nb - 1), off // tc + j))

    ins, specs = [d1, d1], [cur(cd1), nxt(cd1)]
    if dtwo:
        ins += [d2, d2]
        specs += [cur(cd2), nxt(cd2)]
    if has_x:
        ins += [x1, x1]
        specs += [cur(c1), prev(c1)]
    if xtwo:
        ins += [x2, x2]
        specs += [cur(c2), prev(c2)]
    ins.append(w)
    specs.append(pl.BlockSpec((k, tc), lambda j, i: (0, j)))
    if dep is not None:
        ins.append(dep)
        specs.append(ANY)
    o_specs = [pl.BlockSpec((tm, tc), lambda j, i: (i, j))]
    o_shapes = [jax.ShapeDtypeStruct((t, c), BF16)]
    scratch = [pltpu.VMEM((tm + halo, tc), F32)]
    if has_x:
        o_specs.append(pl.BlockSpec((k, tc), lambda j, i: (0, j)))
        o_shapes.append(jax.ShapeDtypeStruct((k, c), F32))
        scratch.append(pltpu.VMEM((halo + tm, tc), F32))
    out = pl.pallas_call(
        body, name=name, grid=(c // tc, t // tm), in_specs=specs, out_specs=o_specs,
        out_shape=o_shapes, scratch_shapes=scratch,
        compiler_params=_cp("parallel", "arbitrary"))(*ins)
    return out if has_x else out[0]


def _pool_taps(c):
    kmax = max(POOL_WINDOWS)
    grp = c // len(POOL_WINDOWS)
    cols = []
    for wdw in POOL_WINDOWS:
        col = jnp.concatenate([jnp.zeros((kmax - wdw,), F32), jnp.ones((wdw,), F32)])
        cols.append(jnp.tile(col[:, None], (1, grp)))
    return jnp.concatenate(cols, axis=1)


def _pool_live(c):
    grp, sl = c // len(POOL_WINDOWS), min(LANES, c)
    return tuple(max(POOL_WINDOWS[g] for g in range(l0 // grp, (l0 + sl - 1) // grp + 1))
                 for l0 in range(0, c, sl))


def _counts(i, tr, seq, grp):
    pos = (i * tr + lax.broadcasted_iota(jnp.int32, (tr, 1), 0)) % seq + 1
    return [1.0 / jnp.minimum(pos, wdw).astype(F32) for wdw in POOL_WINDOWS]


def _ln_stats(a2):
    mu = jnp.mean(a2, axis=-1, keepdims=True)
    xc = a2 - mu
    rstd = lax.rsqrt(jnp.mean(xc * xc, axis=-1, keepdims=True) + EPS)
    return xc * rstd, rstd


def _even_fwd(name, seq, a2, ws, u, ln_g, ln_b, w_pool, scale):
    t, c = a2.shape
    ng = len(POOL_WINDOWS)
    grp = c // ng
    tr = _tile(t, 256, SUBLANES)

    def body(a_ref, ws_ref, b_ref, g_ref, bb_ref, wp_ref, sc_ref, z_ref, pm_ref):
        xhat, _ = _ln_stats(a_ref[...].astype(F32))
        l = xhat * g_ref[...] + bb_ref[...]
        z_ref[:, 0:c] = (l * _sigmoid(l)).astype(BF16)
        inv = _counts(pl.program_id(0), tr, seq, grp)
        for g in range(ng):
            gs = slice(g * grp, (g + 1) * grp)
            pm = (ws_ref[:, gs].astype(F32) * inv[g] - b_ref[:, gs].astype(F32)).astype(BF16)
            pm_ref[:, gs] = pm
            q = jnp.dot(pm, wp_ref[g], preferred_element_type=F32)
            z_ref[:, c + g * grp:c + (g + 1) * grp] = (q * sc_ref[:, gs]).astype(BF16)

    row = pl.BlockSpec((tr, c), lambda i: (i, 0))
    vec = pl.BlockSpec((1, c), lambda i: (0, 0))
    return pl.pallas_call(
        body, name=name, grid=(t // tr,),
        in_specs=[row, row, pl.BlockSpec((tr, c), lambda i: (i, 2)), vec, vec,
                  pl.BlockSpec((ng, grp, grp), lambda i: (0, 0, 0)), vec],
        out_specs=[pl.BlockSpec((tr, 2 * c), lambda i: (i, 0)), row],
        out_shape=[jax.ShapeDtypeStruct((t, 2 * c), BF16), jax.ShapeDtypeStruct((t, c), BF16)],
        compiler_params=_cp("parallel"))(a2, ws, u, ln_g, ln_b, w_pool, scale)


def _even_bwd(name, seq, dz, a2, pm, ln_g, ln_b, w_pool, scale):
    t, c = a2.shape
    ng = len(POOL_WINDOWS)
    grp = c // ng
    tr = _tile(t, 256, SUBLANES)

    def body(dz_ref, a_ref, pm_ref, g_ref, bb_ref, wp_ref, sc_ref,
             da_ref, dws_ref, dpm_ref, vec_ref, dwp_ref):
        i = pl.program_id(0)

        @pl.when(i == 0)
        def _():
            vec_ref[...] = jnp.zeros_like(vec_ref)
            dwp_ref[...] = jnp.zeros_like(dwp_ref)

        xhat, rstd = _ln_stats(a_ref[...].astype(F32))
        gv = g_ref[...]
        l = xhat * gv + bb_ref[...]
        sg = _sigmoid(l)
        dl = dz_ref[:, 0:c].astype(F32) * (sg * (1.0 + l * (1.0 - sg)))
        dxh = dl * gv
        da2 = rstd * (dxh - jnp.mean(dxh, axis=-1, keepdims=True)
                      - xhat * jnp.mean(dxh * xhat, axis=-1, keepdims=True))
        da_ref[...] = da2.astype(BF16)
        vec_ref[0:1, :] += jnp.sum(dl * xhat, axis=0, keepdims=True)
        vec_ref[1:2, :] += jnp.sum(dl, axis=0, keepdims=True)
        vec_ref[2:3, :] += jnp.sum(da2, axis=0, keepdims=True)
        inv = _counts(i, tr, seq, grp)
        for g in range(ng):
            gs = slice(g * grp, (g + 1) * grp)
            pmv = pm_ref[:, gs]
            wp = wp_ref[g]
            dp = dz_ref[:, c + g * grp:c + (g + 1) * grp].astype(F32)
            q = jnp.dot(pmv, wp, preferred_element_type=F32)
            vec_ref[3:4, gs] += jnp.sum(dp * q, axis=0, keepdims=True)
            dq = (dp * sc_ref[:, gs]).astype(BF16)
            dpm = lax.dot_general(dq, wp, NT, preferred_element_type=F32)
            dwp_ref[g] += lax.dot_general(pmv, dq, TN, preferred_element_type=F32)
            dpm_ref[:, gs] = dpm.astype(BF16)
            dws_ref[:, gs] = (dpm * inv[g]).astype(BF16)

    row = pl.BlockSpec((tr, c), lambda i: (i, 0))
    vec = pl.BlockSpec((1, c), lambda i: (0, 0))
    rshape = jax.ShapeDtypeStruct((t, c), BF16)
    return pl.pallas_call(
        body, name=name, grid=(t // tr,),
        in_specs=[pl.BlockSpec((tr, 2 * c), lambda i: (i, 0)), row, row, vec, vec,
                  pl.BlockSpec((ng, grp, grp), lambda i: (0, 0, 0)), vec],
        out_specs=[row, row, row, pl.BlockSpec((SUBLANES, c), lambda i: (0, 0)),
                   pl.BlockSpec((ng, grp, grp), lambda i: (0, 0, 0))],
        out_shape=[rshape, rshape, rshape, jax.ShapeDtypeStruct((SUBLANES, c), F32),
                   jax.ShapeDtypeStruct((ng, grp, grp), F32)],
        compiler_params=_cp("arbitrary"))(dz, a2, pm, ln_g, ln_b, w_pool, scale)


def _even_du(name, u, da1, dbp, dpm):
    t, c = da1.shape
    tr = _tile(t, 256, SUBLANES)

    def body(u_ref, da_ref, dbp_ref, dpm_ref, du_ref):
        val = u_ref[:, 0:c].astype(F32)
        sg = _sigmoid(u_ref[:, c:2 * c].astype(F32))
        da = da_ref[...].astype(F32)
        du_ref[:, 0:c] = (da * sg).astype(BF16)
        du_ref[:, c:2 * c] = (da * val * sg * (1.0 - sg)).astype(BF16)
        du_ref[:, 2 * c:3 * c] = (dbp_ref[...].astype(F32) - dpm_ref[...].astype(F32)).astype(BF16)

    row = pl.BlockSpec((tr, c), lambda i: (i, 0))
    wide = pl.BlockSpec((tr, 3 * c), lambda i: (i, 0))
    return pl.pallas_call(
        body, name=name, grid=(t // tr,), in_specs=[wide, row, row, row], out_specs=wide,
        out_shape=jax.ShapeDtypeStruct((t, 3 * c), BF16),
        compiler_params=_cp("parallel"))(u, da1, dbp, dpm)


def _odd_du(name, u, dy, co, dxc):
    t, c = dy.shape
    tr = _tile(t, 256, SUBLANES)

    def body(u_ref, dy_ref, co_ref, dx_ref, du_ref):
        dx = dx_ref[...].astype(F32)
        du_ref[:, 0:c] = (dy_ref[...].astype(F32) * co_ref[...].astype(F32)).astype(BF16)
        du_ref[:, c:2 * c] = (dx * u_ref[:, 2 * c:3 * c].astype(F32)).astype(BF16)
        du_ref[:, 2 * c:3 * c] = (dx * u_ref[:, c:2 * c].astype(F32)).astype(BF16)

    row = pl.BlockSpec((tr, c), lambda i: (i, 0))
    wide = pl.BlockSpec((tr, 3 * c), lambda i: (i, 0))
    return pl.pallas_call(
        body, name=name, grid=(t // tr,), in_specs=[wide, row, row, row], out_specs=wide,
        out_shape=jax.ShapeDtypeStruct((t, 3 * c), BF16),
        compiler_params=_cp("parallel"))(u, dy, co, dxc)


def _local_step(x, tgt, seq, small, get_w, put_g, sync):
    t, d = x.shape
    c = d // 2
    cw_e, cw_o = small["conv_w_e"], small["conv_w_o"]
    wp = small["w_pool_e"].astype(BF16)
    ptaps = _pool_taps(c)
    row = lambda v: v.reshape(1, -1)

    we = {"w_in": get_w("in_e", x)[0]}
    n0 = _rms_fwd("rms_fwd_mix0", x, row(small["mix_norm_e"]))
    u0 = _mm_nn("mm_in_e", n0, we["w_in"], BF16)
    sync("fwd_a", u0)
    a2 = _conv_fwd("conv_e_fwd", seq, c, cw_e, u0, 0, u0, c, "glu", bias=row(small["conv_b_e"]))
    ws = _conv_fwd("pool_fwd", seq, c, ptaps, u0, 2 * c, live=_pool_live(c))
    z0, pm = _even_fwd("even_fwd", seq, a2, ws, u0, row(small["ln_g_e"]), row(small["ln_b_e"]),
                       wp, row(small["pool_scale_e"]))
    we["w_out"] = get_w("out_e", z0)[0]
    h1 = _mm_nn("mm_out_e", z0, we["w_out"], F32, res=x)
    sync("fwd_b", h1)
    n1 = _rms_fwd("rms_fwd_ffn0", h1, row(small["ffn_norm"][0]))
    wf0 = dict(zip(("w_gate", "w_up"), get_w("gu0", n1)))
    act0, ds0, s0 = _ffn_fwd("ffn0_fwd", n1, wf0["w_gate"], wf0["w_up"])
    dep = sync("fwd_c", act0)
    wf0["w_down"] = get_w("down0", act0)[0]
    h2 = _mm_nn("mm_down0", act0, wf0["w_down"], F32, res=h1, dep=dep)
    dep = sync("fwd_d", h2)
    n2 = _rms_fwd("rms_fwd_mix1", h2, row(small["mix_norm_o"]))
    wo = {"w_in": get_w("in_o", n2)[0]}
    u1 = _mm_nn("mm_in_o", n2, wo["w_in"], BF16, dep=dep)
    co, y1 = _conv_fwd("conv_o_fwd", seq, d, cw_o, u1, d, u1, 2 * d, "mul", post=u1, cpost=0)
    dep = sync("fwd_e", y1)
    wo["w_out"] = get_w("out_o", y1)[0]
    h3 = _mm_nn("mm_out_o", y1, wo["w_out"], F32, res=h2, dep=dep)
    dep = sync("fwd_f", h3)
    n3 = _rms_fwd("rms_fwd_ffn1", h3, row(small["ffn_norm"][1]))
    wf1 = dict(zip(("w_gate", "w_up"), get_w("gu1", n3)))
    act1, ds1, s1 = _ffn_fwd("ffn1_fwd", n3, wf1["w_gate"], wf1["w_up"], dep=dep)
    wf1["w_down"] = get_w("down1", act1)[0]
    h4 = _mm_nn("mm_down1", act1, wf1["w_down"], F32, res=h3)

    dh4, dh4b, d_final, lsum = _loss_head("loss_head", h4, row(small["final_norm"]), tgt)

    def ffn_bwd(tag, dh, dhb, h_in, gain, n, dsilu, silu, act, w, dep):
        dg, dup = _ffn_bwd_act("ffn%s_bwd_act" % tag, dhb, w["w_down"], dsilu, silu, dep=dep)
        dwd = _mm_tn("mm_dwd%s" % tag, act, dhb, BF16)
        dwg = _mm_tn("mm_dwg%s" % tag, dg, n, BF16, dep=sync("bwd_ffn" + tag, dwd))
        dwu = _mm_tn("mm_dwu%s" % tag, dup, n, BF16)
        dn = _mm_nn("mm_ffn_dn%s" % tag, [dg, dup], [w["w_gate"], w["w_up"]], BF16)
        dh_in, dhb_in, dgain = _rms_bwd("rms_bwd_ffn%s" % tag, h_in, gain, dn, dh)
        dep = put_g("ffn" + tag, {"w_gate": dwg, "w_up": dwu, "w_down": dwd})
        return dh_in, dhb_in, dgain, dep

    dh3, dh3b, d_ffn1, dep = ffn_bwd("1", dh4, dh4b, h3, row(small["ffn_norm"][1]), n3, ds1, s1,
                                     act1, wf1, None)
    dw_out_o = _mm_tn("mm_dw_out_o", y1, dh3b, BF16, dep=dep)
    dy1 = _mm_nt("mm_dy_o", dh3b, wo["w_out"], BF16, dep=sync("bwd_mix_o", dw_out_o))
    dxc, dcw_o = _conv_bwd("conv_o_bwd", seq, d, cw_o, dy1, 0, u1, 0, "mul",
                           x1=u1, c1=d, x2=u1, c2=2 * d, pre="mul")
    du1 = _odd_du("odd_du", u1, dy1, co, dxc)
    dw_in_o = _mm_tn("mm_dw_in_o", n2, du1, BF16)
    dn2 = _mm_nt("mm_dn_o", du1, wo["w_in"], BF16)
    dh2, dh2b, d_mix_o = _rms_bwd("rms_bwd_mix1", h2, row(small["mix_norm_o"]), dn2, dh3)
    dep = put_g("mix_o", {"w_in": dw_in_o, "w_out": dw_out_o})

    dh1, dh1b, d_ffn0, dep = ffn_bwd("0", dh2, dh2b, h1, row(small["ffn_norm"][0]), n1, ds0, s0,
                                     act0, wf0, dep)
    dw_out_e = _mm_tn("mm_dw_out_e", z0, dh1b, BF16, dep=dep)
    dz0 = _mm_nt("mm_dz_e", dh1b, we["w_out"], BF16, dep=sync("bwd_mix_e", dw_out_e))
    da2, dws, dpm, vecs, dwp = _even_bwd("even_bwd", seq, dz0, a2, pm, row(small["ln_g_e"]),
                                         row(small["ln_b_e"]), wp, row(small["pool_scale_e"]))
    dep = put_g("small", {"conv_b_e": vecs[2], "ln_g_e": vecs[0], "ln_b_e": vecs[1],
                          "w_pool_e": dwp, "pool_scale_e": vecs[3], "mix_norm_o": d_mix_o[0],
                          "conv_w_o": dcw_o, "ffn_norm": jnp.concatenate([d_ffn0, d_ffn1], axis=0),
                          "final_norm": d_final[0], "loss": lsum})
    da1, dcw_e = _conv_bwd("conv_e_bwd", seq, c, cw_e, da2, 0, x1=u0, c1=0, x2=u0, c2=c, pre="glu", dep=dep)
    dbp = _conv_bwd("pool_bwd", seq, c, ptaps, dws, 0, live=_pool_live(c))
    du0 = _even_du("even_du", u0, da1, dbp, dpm)
    dw_in_e = _mm_tn("mm_dw_in_e", n0, du0, BF16)
    dep = put_g("mix_e", {"w_in": dw_in_e, "w_out": dw_out_e})
    dn0 = _mm_nt("mm_dn_e", du0, we["w_in"], BF16, dep=dep)
    dx, _, d_mix_e = _rms_bwd("rms_bwd_mix0", x, row(small["mix_norm_e"]), dn0, dh1, bf16_too=False)
    return dx, {"conv_w_e": dcw_e, "mix_norm_e": d_mix_e[0]}


def _place():
    x, y, c = (lax.axis_index(a) for a in MESH_AXES)
    return x, y, c


def _index(p):
    return 4 * p[0] + 2 * p[1] + p[2]


def _slab(ref, kind, d, n):
    if kind == "blk":
        return ref.at[d]
    return ref.at[:, pl.ds(pl.multiple_of(d * n, LANES), n)]


HBM = pl.BlockSpec(memory_space=pltpu.HBM)
SEM = pl.BlockSpec(memory_space=pltpu.SEMAPHORE)
EFFECT = pltpu.SideEffectType.DATAFLOW_SIDE_EFFECTING
NCHIPS = 4


def _in_hbm(a):
    return pltpu.with_memory_space_constraint(a, pltpu.HBM)


def _gathered_shape(s, kind):
    m, n = s.shape
    return (NDEV, m, n) if kind == "blk" else (m, NDEV * n)


def _first_targets():
    x, y, c = _place()
    return [(x, y, 1 - c), (1 - x, y, c), (x, 1 - y, c), (1 - x, 1 - y, c)]


def _gather_start(name, shards, kinds, after):
    na = len(shards)

    def body(*refs):
        x_refs, land_refs = refs[:na], refs[na:2 * na]
        send_sems, recv_sems = refs[2 * na + 1], refs[2 * na + 2]
        token = refs[-1]
        me = _index(_place())
        for a in range(na):
            for k, to in enumerate(_first_targets()):
                pltpu.make_async_remote_copy(
                    src_ref=x_refs[a], dst_ref=_slab(land_refs[a], kinds[a], me, shards[a].shape[1]),
                    send_sem=send_sems.at[4 * a + k], recv_sem=recv_sems.at[4 * a + k],
                    device_id=to, device_id_type=MESH).start()
        token[...] = jnp.zeros_like(token)

    lands = [lax.empty(_gathered_shape(s, k), s.dtype) for s, k in zip(shards, kinds)]
    outs = pl.pallas_call(
        body, name=name,
        out_shape=(pltpu.SemaphoreType.DMA((4 * na,)), pltpu.SemaphoreType.DMA((4 * na,)),
                   *[pltpu.HBM(s.shape, s.dtype) for s in shards],
                   *[pltpu.HBM(l.shape, l.dtype) for l in lands],
                   jax.ShapeDtypeStruct((SUBLANES, LANES), F32)),
        in_specs=[HBM] * (2 * na) + [ANY],
        out_specs=(SEM, SEM, *[HBM] * (2 * na), pl.BlockSpec(memory_space=pltpu.VMEM)),
        input_output_aliases={i: 2 + i for i in range(2 * na)},
        compiler_params=pltpu.CompilerParams(has_side_effects=EFFECT),
    )(*[_in_hbm(s) for s in shards], *[_in_hbm(l) for l in lands], after)
    return outs[0], outs[1], outs[2:2 + na], outs[2 + na:2 + 2 * na], outs[-1]


def _gather_wait(name, started, kinds, after):
    send_sems, recv_sems, shards, lands, _ = started
    na = len(shards)

    def body(*refs):
        x_refs, land_refs = refs[:na], refs[na:2 * na]
        s_sems, r_sems = refs[2 * na], refs[2 * na + 1]
        for a in range(na):
            for k, frm in enumerate(_first_targets()):
                cp = pltpu.make_async_remote_copy(
                    src_ref=x_refs[a],
                    dst_ref=_slab(land_refs[a], kinds[a], _index(frm), shards[a].shape[1]),
                    send_sem=s_sems.at[4 * a + k], recv_sem=r_sems.at[4 * a + k],
                    device_id=frm, device_id_type=MESH)
                cp.wait_send()
                cp.wait_recv()

    outs = pl.pallas_call(
        body, name=name,
        out_shape=(*[pltpu.HBM(s.shape, s.dtype) for s in shards],
                   *[pltpu.HBM(l.shape, l.dtype) for l in lands]),
        in_specs=[HBM] * (2 * na) + [SEM, SEM, ANY], out_specs=[HBM] * (2 * na),
        input_output_aliases={i: i for i in range(2 * na)},
        compiler_params=pltpu.CompilerParams(has_side_effects=EFFECT),
    )(*shards, *lands, send_sems, recv_sems, after)
    return outs[:na], outs[na:]


def _split_start(name, bufs, ncopies, plan, after):
    nb = len(bufs)

    def body(*refs):
        send_sems, recv_sems, token = refs[nb + 1], refs[nb + 2], refs[-1]
        for k, (src, dst, to, _) in enumerate(plan(refs[:nb])):
            pltpu.make_async_remote_copy(src_ref=src, dst_ref=dst, send_sem=send_sems.at[k],
                                         recv_sem=recv_sems.at[k], device_id=to, device_id_type=MESH).start()
        token[...] = jnp.zeros_like(token)

    outs = pl.pallas_call(
        body, name=name,
        out_shape=(pltpu.SemaphoreType.DMA((ncopies,)), pltpu.SemaphoreType.DMA((ncopies,)),
                   *[pltpu.HBM(b.shape, b.dtype) for b in bufs],
                   jax.ShapeDtypeStruct((SUBLANES, LANES), F32)),
        in_specs=[HBM] * nb + [ANY],
        out_specs=(SEM, SEM, *[HBM] * nb, pl.BlockSpec(memory_space=pltpu.VMEM)),
        input_output_aliases={i: 2 + i for i in range(nb)},
        compiler_params=pltpu.CompilerParams(has_side_effects=EFFECT),
    )(*[_in_hbm(b) for b in bufs], after)
    return outs[0], outs[1], list(outs[2:2 + nb]), outs[-1]


def _split_wait(name, started, plan, after):
    send_sems, recv_sems, bufs, _ = started
    nb = len(bufs)

    def body(*refs):
        s_sems, r_sems = refs[nb], refs[nb + 1]
        for k, (src, _, to, landed) in enumerate(plan(refs[:nb])):
            cp = pltpu.make_async_remote_copy(src_ref=src, dst_ref=landed, send_sem=s_sems.at[k],
                                              recv_sem=r_sems.at[k], device_id=to, device_id_type=MESH)
            cp.wait_send()
            cp.wait_recv()

    outs = pl.pallas_call(
        body, name=name, out_shape=tuple(pltpu.HBM(b.shape, b.dtype) for b in bufs),
        in_specs=[HBM] * nb + [SEM, SEM, ANY], out_specs=[HBM] * nb,
        input_output_aliases={i: i for i in range(nb)},
        compiler_params=pltpu.CompilerParams(has_side_effects=EFFECT),
    )(*bufs, send_sems, recv_sems, after)
    return list(outs)


def _forward_plan(kinds, nloc):
    def plan(lands):
        x, y, c = _place()
        out = []
        for a, land in enumerate(lands):
            for chip in [(1 - x, y), (x, 1 - y), (1 - x, 1 - y)]:
                mine = _slab(land, kinds[a], _index((*chip, c)), nloc[a])
                out.append((mine, mine, (x, y, 1 - c), _slab(land, kinds[a], _index((*chip, 1 - c)), nloc[a])))
        return out
    return plan


def _own_copy(name, shard, land, kind, me):
    m, n = shard.shape
    tr = _tile(m, max(SUBLANES, 1048576 // n), SUBLANES)

    def body(s_ref, x_ref, land_ref, o_ref):
        o_ref[...] = x_ref[...]

    if kind == "blk":
        o_spec = pl.BlockSpec((None, tr, n), lambda i, s: (s[0], i, 0))
    else:
        o_spec = pl.BlockSpec((tr, n), lambda i, s: (i, s[0]))
    return pl.pallas_call(
        body, name=name,
        grid_spec=pltpu.PrefetchScalarGridSpec(
            num_scalar_prefetch=1, grid=(m // tr,),
            in_specs=[pl.BlockSpec((tr, n), lambda i, s: (i, 0)), ANY], out_specs=o_spec),
        out_shape=jax.ShapeDtypeStruct(land.shape, land.dtype),
        input_output_aliases={2: 0}, compiler_params=_cp("parallel"))(me, shard, land)


def _everyone_plan(refs):
    x, y, c = _place()
    out = []
    for dx, dy, dc in [(a, b, e) for a in (0, 1) for b in (0, 1) for e in (0, 1)][1:]:
        peer = (x ^ dx, y ^ dy, c ^ dc)
        out.append((refs[0], refs[1].at[_index((x, y, c))], peer, refs[1].at[_index(peer)]))
    return out


def _pair_plan(kinds, nloc):
    na = len(kinds)

    def plan(refs):
        x, y, c = _place()
        out = []
        for a in range(na):
            for j in range(NCHIPS):
                dst = refs[na + a].at[j]
                out.append((_slab(refs[a], kinds[a], 2 * j + (1 - c), nloc[a]), dst, (x, y, 1 - c), dst))
        return out
    return plan


def _chip_sum(name, full, kind, n, from_sib, place):
    _, m, _ = from_sib.shape
    tr = _tile(m, max(SUBLANES, 1048576 // n), SUBLANES)

    def body(s_ref, mine_ref, sib_ref, csum_ref, land_ref):
        v = (mine_ref[...].astype(F32) + sib_ref[...].astype(F32)).astype(csum_ref.dtype)
        csum_ref[...] = v

        @pl.when(pl.program_id(1) == s_ref[1])
        def _():
            land_ref[...] = v

    if kind == "blk":
        mine_spec = pl.BlockSpec((None, tr, n), lambda i, j, s: (2 * j + s[0], i, 0))
    else:
        mine_spec = pl.BlockSpec((tr, n), lambda i, j, s: (i, 2 * j + s[0]))
    slot = pl.BlockSpec((None, tr, n), lambda i, j, s: (j, i, 0))
    shp = jax.ShapeDtypeStruct((NCHIPS, m, n), from_sib.dtype)
    return pl.pallas_call(
        body, name=name,
        grid_spec=pltpu.PrefetchScalarGridSpec(
            num_scalar_prefetch=1, grid=(m // tr, NCHIPS), in_specs=[mine_spec, slot],
            out_specs=[slot, pl.BlockSpec((None, tr, n), lambda i, j, s: (s[1], i, 0))]),
        out_shape=[shp, shp], compiler_params=_cp("parallel", "arbitrary"))(place, full, from_sib)


def _other_chips():
    x, y, c = _place()
    return [(1 - x, y, c), (x, 1 - y, c), (1 - x, 1 - y, c)]


def _scatter_start(name, csums, lands, after):
    na = len(csums)

    def body(*refs):
        c_refs, land_refs = refs[:na], refs[na:2 * na]
        send_sems, recv_sems = refs[2 * na + 1], refs[2 * na + 2]
        token = refs[-1]
        x, y, _ = _place()
        for a in range(na):
            for k, to in enumerate(_other_chips()):
                pltpu.make_async_remote_copy(
                    src_ref=c_refs[a].at[2 * to[0] + to[1]], dst_ref=land_refs[a].at[2 * x + y],
                    send_sem=send_sems.at[3 * a + k], recv_sem=recv_sems.at[3 * a + k],
                    device_id=to, device_id_type=MESH).start()
        token[...] = jnp.zeros_like(token)

    outs = pl.pallas_call(
        body, name=name,
        out_shape=(pltpu.SemaphoreType.DMA((3 * na,)), pltpu.SemaphoreType.DMA((3 * na,)),
                   *[pltpu.HBM(s.shape, s.dtype) for s in csums],
                   *[pltpu.HBM(l.shape, l.dtype) for l in lands],
                   jax.ShapeDtypeStruct((SUBLANES, LANES), F32)),
        in_specs=[HBM] * (2 * na) + [ANY],
        out_specs=(SEM, SEM, *[HBM] * (2 * na), pl.BlockSpec(memory_space=pltpu.VMEM)),
        input_output_aliases={i: 2 + i for i in range(2 * na)},
        compiler_params=pltpu.CompilerParams(has_side_effects=EFFECT),
    )(*[_in_hbm(s) for s in csums], *[_in_hbm(l) for l in lands], after)
    return outs[0], outs[1], outs[2:2 + na], outs[2 + na:2 + 2 * na], outs[-1]


def _scatter_wait(name, started, after):
    send_sems, recv_sems, csums, lands, _ = started
    na = len(csums)

    def body(*refs):
        c_refs, land_refs = refs[:na], refs[na:2 * na]
        s_sems, r_sems = refs[2 * na], refs[2 * na + 1]
        for a in range(na):
            for k, frm in enumerate(_other_chips()):
                cp = pltpu.make_async_remote_copy(
                    src_ref=c_refs[a].at[2 * frm[0] + frm[1]], dst_ref=land_refs[a].at[2 * frm[0] + frm[1]],
                    send_sem=s_sems.at[3 * a + k], recv_sem=r_sems.at[3 * a + k],
                    device_id=frm, device_id_type=MESH)
                cp.wait_send()
                cp.wait_recv()

    outs = pl.pallas_call(
        body, name=name,
        out_shape=(*[pltpu.HBM(s.shape, s.dtype) for s in csums],
                   *[pltpu.HBM(l.shape, l.dtype) for l in lands]),
        in_specs=[HBM] * (2 * na) + [SEM, SEM, ANY], out_specs=[HBM] * (2 * na),
        input_output_aliases={i: i for i in range(2 * na)},
        compiler_params=pltpu.CompilerParams(has_side_effects=EFFECT),
    )(*csums, *lands, send_sems, recv_sems, after)
    return outs[na:]


def _adam_math(w, g, m, v):
    m = ADAM_B1 * m + (1.0 - ADAM_B1) * g
    v = ADAM_B2 * v + (1.0 - ADAM_B2) * (g * g)
    m_hat = m / (1.0 - ADAM_B1 ** ADAM_STEP)
    v_hat = v / (1.0 - ADAM_B2 ** ADAM_STEP)
    delta = -ADAM_LR * (m_hat / (jnp.sqrt(v_hat) + ADAM_EPS) + ADAM_WD * w)
    return delta, m, v


def _sum_adamw(name, parts, w, m, v, layer, prev=None, dep=None):
    nl, r, c = w.shape
    nparts = parts.shape[0]
    tr = _tile(r, max(SUBLANES, 360448 // c), SUBLANES)

    def body(p_ref, w_ref, m_ref, v_ref, *rest):
        g_ref, d_ref, mo_ref, vo_ref = rest[-4:]
        g = p_ref[0].astype(F32)
        for s in range(1, nparts):
            g = g + p_ref[s].astype(F32)
        delta, mn, vn = _adam_math(w_ref[...], g, m_ref[...], v_ref[...])
        g_ref[...] = g
        d_ref[...] = delta
        mo_ref[...] = mn
        vo_ref[...] = vn

    row = pl.BlockSpec((None, tr, c), lambda i: (layer, i, 0))
    shp = jax.ShapeDtypeStruct((nl, r, c), F32)
    extra = ([] if prev is None else list(prev)) + ([] if dep is None else [dep])
    return pl.pallas_call(
        body, name=name, grid=(r // tr,),
        in_specs=[pl.BlockSpec((nparts, tr, c), lambda i: (0, i, 0)), row, row, row] + [ANY] * len(extra),
        out_specs=[row, row, row, row], out_shape=[shp, shp, shp, shp],
        input_output_aliases={} if prev is None else {4 + i: i for i in range(4)},
        compiler_params=_cp("parallel"))(parts, w, m, v, *extra)


def _sum_parts(name, parts):
    _, r, c = parts.shape

    def body(p_ref, o_ref):
        g = p_ref[0]
        for s in range(1, NDEV):
            g = g + p_ref[s]
        o_ref[...] = g

    return pl.pallas_call(
        body, name=name, grid=(1,),
        in_specs=[pl.BlockSpec((NDEV, r, c), lambda i: (0, 0, 0))],
        out_specs=pl.BlockSpec((r, c), lambda i: (0, 0)),
        out_shape=jax.ShapeDtypeStruct((r, c), F32), compiler_params=_cp("arbitrary"))(parts)


def _adamw(name, w, g, m, v):
    r, c = w.shape

    def body(w_ref, g_ref, m_ref, v_ref, d_ref, mo_ref, vo_ref):
        delta, mn, vn = _adam_math(w_ref[...], g_ref[...], m_ref[...], v_ref[...])
        d_ref[...] = delta
        mo_ref[...] = mn
        vo_ref[...] = vn

    full = pl.BlockSpec((r, c), lambda i: (0, 0))
    shp = jax.ShapeDtypeStruct((r, c), F32)
    return pl.pallas_call(
        body, name=name, grid=(1,), in_specs=[full] * 4, out_specs=[full] * 3,
        out_shape=[shp] * 3, compiler_params=_cp("arbitrary"))(w, g, m, v)


def _pack(arrays):
    flat = jnp.concatenate([a.reshape(-1) for a in arrays])
    unit = SUBLANES * LANES
    pad = (-flat.shape[0]) % unit
    return jnp.pad(flat, (0, pad)).reshape(-1, LANES)


def _unpack(buf, shapes):
    flat = buf.reshape(-1)
    out, off = [], 0
    for shp in shapes:
        size = 1
        for s in shp:
            size *= s
        out.append(flat[off:off + size].reshape(shp))
        off += size
    return out


WEIGHTS = ["mix_norm_e", "w_in_e", "conv_w_e", "conv_b_e", "ln_g_e", "ln_b_e", "w_pool_e",
           "pool_scale_e", "w_out_e", "mix_norm_o", "w_in_o", "conv_w_o", "w_out_o", "ffn_norm",
           "w_gate", "w_up", "w_down", "final_norm"]
BIG = ["w_in_e", "w_out_e", "w_in_o", "w_out_o", "w_gate", "w_up", "w_down"]
SHARDED_SMALL = {"conv_w_e": 1, "w_pool_e": 1, "mix_norm_o": 0, "conv_w_o": 1}
SMALL = [n for n in WEIGHTS if n not in BIG]


def kernel(x, mix_norm_e, w_in_e, conv_w_e, conv_b_e, ln_g_e, ln_b_e, w_pool_e, pool_scale_e, w_out_e, mix_norm_o, w_in_o, conv_w_o, w_out_o, ffn_norm, w_gate, w_up, w_down, final_norm, loss_target, m_mix_norm_e, m_w_in_e, m_conv_w_e, m_conv_b_e, m_ln_g_e, m_ln_b_e, m_w_pool_e, m_pool_scale_e, m_w_out_e, m_mix_norm_o, m_w_in_o, m_conv_w_o, m_w_out_o, m_ffn_norm, m_w_gate, m_w_up, m_w_down, m_final_norm, v_mix_norm_e, v_w_in_e, v_conv_w_e, v_conv_b_e, v_ln_g_e, v_ln_b_e, v_w_pool_e, v_pool_scale_e, v_w_out_e, v_mix_norm_o, v_w_in_o, v_conv_w_o, v_w_out_o, v_ffn_norm, v_w_gate, v_w_up, v_w_down, v_final_norm):
    wts = dict(zip(WEIGHTS, [mix_norm_e, w_in_e, conv_w_e, conv_b_e, ln_g_e, ln_b_e, w_pool_e, pool_scale_e, w_out_e, mix_norm_o, w_in_o, conv_w_o, w_out_o, ffn_norm, w_gate, w_up, w_down, final_norm]))
    mom = dict(zip(WEIGHTS, [m_mix_norm_e, m_w_in_e, m_conv_w_e, m_conv_b_e, m_ln_g_e, m_ln_b_e, m_w_pool_e, m_pool_scale_e, m_w_out_e, m_mix_norm_o, m_w_in_o, m_conv_w_o, m_w_out_o, m_ffn_norm, m_w_gate, m_w_up, m_w_down, m_final_norm]))
    var = dict(zip(WEIGHTS, [v_mix_norm_e, v_w_in_e, v_conv_w_e, v_conv_b_e, v_ln_g_e, v_ln_b_e, v_w_pool_e, v_pool_scale_e, v_w_out_e, v_mix_norm_o, v_w_in_o, v_conv_w_o, v_w_out_o, v_ffn_norm, v_w_gate, v_w_up, v_w_down, v_final_norm]))
    bsz, seq, d = x.shape
    t = bsz * seq
    me = _index(_place())
    me_arr = jnp.reshape(me, (1,)).astype(jnp.int32)

    sh_names = list(SHARDED_SMALL)
    sh_local = [wts[n][0] for n in sh_names]
    packed = _pack(sh_local)
    params_st = _split_start("small_params_start", [packed, lax.empty((NDEV,) + packed.shape, F32)], NDEV - 1,
                             _everyone_plan, x)

    for state in (wts, mom, var):
        for n in ("w_gate", "w_up"):
            state[n] = jnp.swapaxes(state[n], 1, 2)
    bf = lambda a: a.astype(BF16)
    mix_kinds, ffn_kinds = ["col", "blk"], ["blk", "blk", "blk"]
    ffn_names = ("w_gate", "w_up", "w_down")
    groups = {
        "mix_e": ([w_in_e.shape[2], d], mix_kinds, [("w_in_e", 0), ("w_out_e", 0)]),
        "ffn0": ([d, d, d], ffn_kinds, [(n, 0) for n in ffn_names]),
        "mix_o": ([w_in_o.shape[2], d], mix_kinds, [("w_in_o", 0), ("w_out_o", 0)]),
        "ffn1": ([d, d, d], ffn_kinds, [(n, 1) for n in ffn_names]),
    }
    gathers = {
        "in_e": ([bf(w_in_e[0])], ["col"]), "out_e": ([bf(w_out_e[0])], ["blk"]),
        "gu0": ([bf(wts["w_gate"][0]), bf(wts["w_up"][0])], ["blk", "blk"]), "down0": ([bf(w_down[0])], ["blk"]),
        "in_o": ([bf(w_in_o[0])], ["col"]), "out_o": ([bf(w_out_o[0])], ["blk"]),
        "gu1": ([bf(wts["w_gate"][1]), bf(wts["w_up"][1])], ["blk", "blk"]), "down1": ([bf(w_down[1])], ["blk"]),
    }
    started, prev = {}, params_st[3]
    for grp, (shards, kinds) in gathers.items():
        started[grp] = _gather_start("gather_start_" + grp, shards, kinds, prev)
        prev = started[grp][4]
    all_started = prev[0, 0:1]

    bufs = _split_wait("small_params_wait", params_st, _everyone_plan, prev)
    gathered = _own_copy("small_params_own", bufs[0], bufs[1], "blk", me_arr)
    small = {n: wts[n][0] for n in SMALL if n not in SHARDED_SMALL and n not in ("ffn_norm", "final_norm")}
    small["ffn_norm"], small["final_norm"] = ffn_norm, final_norm
    flat, off = gathered.reshape(NDEV, -1), 0
    for n, a in zip(sh_names, sh_local):
        ax, shp = SHARDED_SMALL[n], a.shape
        blocks = jnp.moveaxis(flat[:, off:off + a.size].reshape((NDEV,) + shp), 0, ax)
        small[n] = blocks.reshape(shp[:ax] + (NDEV * shp[ax],) + shp[ax + 1:])
        off += a.size

    passing, shards_of = {}, {}

    def pass_on(grp, after):
        shards, kinds = gathers[grp]
        shards_of[grp], lands = _gather_wait("gather_wait_" + grp, started[grp], kinds, after)
        plan = _forward_plan(kinds, [s.shape[1] for s in shards])
        passing[grp] = (_split_start("forward_start_" + grp, lands, 3 * len(lands), plan, after), plan)
        return passing[grp][0][3]

    def get_w(grp, after):
        if grp not in passing:
            after = pass_on(grp, after)
        st, plan = passing[grp]
        lands = _split_wait("forward_wait_" + grp, st, plan, after)
        full = [_own_copy("own_copy_%s%d" % (grp, a), shards_of[grp][a], lands[a], gathers[grp][1][a], me_arr)
                for a in range(len(lands))]
        return [f.reshape(-1, d) if kind == "blk" else f for f, kind in zip(full, gathers[grp][1])]

    cx, cy, cc = _place()
    place = jnp.stack([cc, 2 * cx + cy]).astype(jnp.int32)
    bwd_order = ["ffn1", "mix_o", "ffn0", "mix_e"]
    pairing, pending, results = {}, {}, {}

    late_names = ["conv_w_e", "mix_norm_e"]
    early_names = [n for n in SMALL if n not in late_names] + ["loss"]
    small_sent = {}

    def send_small(tag, arrays, after):
        mine = _pack(arrays)
        small_sent[tag] = _split_start(tag + "_start", [mine, lax.empty((NDEV,) + mine.shape, F32)], NDEV - 1,
                                       _everyone_plan, after)
        return small_sent[tag][3]

    def summed_small(tag, shapes, after):
        bufs = _split_wait(tag + "_wait", small_sent[tag], _everyone_plan, after)
        parts = _own_copy(tag + "_own", bufs[0], bufs[1], "blk", me_arr)
        return _unpack(_sum_parts(tag + "_sum", parts), shapes)

    def put_g(grp, grads):
        if grp == "small":
            small_sent["shapes"] = [grads[n].shape for n in early_names]
            return send_small("small_grads", [grads[n] for n in early_names], place)
        nloc, kinds, _ = groups[grp]
        if len(kinds) == 2:
            fulls = [grads["w_in"], grads["w_out"].reshape(NDEV, -1, d)]
        else:
            fulls = [grads[n].reshape(NDEV, -1, d) for n in ffn_names]
        empties = []
        for g, kind, n in zip(fulls, kinds, nloc):
            empties.append(lax.empty((NCHIPS, g.shape[1] if kind == "blk" else g.shape[0], n), g.dtype))
        plan = _pair_plan(kinds, nloc)
        pairing[grp] = (_split_start("pair_start_" + grp, fulls + empties, NCHIPS * len(fulls), plan, place),
                        plan, kinds, nloc)
        token = pairing[grp][0][3]
        return send_sums(grp, token) if grp == bwd_order[-1] else token

    def send_sums(grp, after):
        st, plan, kinds, nloc = pairing[grp]
        bufs = _split_wait("pair_wait_" + grp, st, plan, after)
        na = len(kinds)
        sums = [_chip_sum("chip_sum_%s%d" % (grp, a), bufs[a], kinds[a], nloc[a], bufs[na + a], place)
                for a in range(na)]
        pending[grp] = _scatter_start("scatter_start_" + grp, [s[0] for s in sums], [s[1] for s in sums], after)
        return pending[grp][4]

    def finish(grp, after):
        lands = _scatter_wait("scatter_wait_" + grp, pending[grp], after)
        dep = None
        for (n, l), parts in zip(groups[grp][2], lands):
            results[n] = _sum_adamw("adamw_%s%d" % (n, l), parts, wts[n], mom[n], var[n], l, results.get(n), dep)
            dep = results[n][1]
        return dep

    fwd_sync = {"fwd_a": ["out_e"], "fwd_b": ["gu0"], "fwd_c": ["down0", "in_o"], "fwd_d": ["out_o"],
                "fwd_e": ["gu1"], "fwd_f": ["down1"]}

    def sync(tag, after):
        if tag in fwd_sync:
            for grp in fwd_sync[tag]:
                after = pass_on(grp, after)
            return after
        if tag == "bwd_mix_o":
            return send_sums("ffn1", after)
        if tag == "bwd_ffn0":
            return finish("ffn1", send_sums("mix_o", after))
        if tag == "bwd_mix_e":
            return finish("mix_o", send_sums("ffn0", after))
        return None

    small["mix_norm_e"] = small["mix_norm_e"] + all_started
    dx, late = _local_step(x.reshape(t, d), loss_target.reshape(t, d), seq, small, get_w, put_g, sync)

    out_g, out_d, out_m, out_v = {}, {}, {}, {}

    dep = finish("ffn0", send_small("last_grads", [late[n] for n in late_names], dx))
    sums = dict(zip(early_names, summed_small("small_grads", small_sent["shapes"], dep)))
    sums.update(zip(late_names, summed_small("last_grads", [late[n].shape for n in late_names], dep)))
    loss = jnp.sum(sums["loss"])
    gs_sum = [sums[n] for n in SMALL]
    local_g = []
    for n, g in zip(SMALL, gs_sum):
        if n in SHARDED_SMALL:
            ax = SHARDED_SMALL[n]
            size = wts[n].shape[ax + 1]
            g = lax.dynamic_slice_in_dim(g, me * size, size, axis=ax)
        local_g.append(g.reshape(wts[n].shape))
    shapes = [wts[n].shape for n in SMALL]
    upd = _adamw("adamw_small", _pack([wts[n] for n in SMALL]), _pack(local_g),
                 _pack([mom[n] for n in SMALL]), _pack([var[n] for n in SMALL]))
    for i, outd in enumerate((out_d, out_m, out_v)):
        for n, a in zip(SMALL, _unpack(upd[i], shapes)):
            outd[n] = a
    for n, g in zip(SMALL, local_g):
        out_g[n] = g

    finish("mix_e", upd[0])
    for n in BIG:
        res = [jnp.swapaxes(a, 1, 2) for a in results[n]] if n in ("w_gate", "w_up") else results[n]
        out_g[n], out_d[n], out_m[n], out_v[n] = res

    return (loss, dx.reshape(bsz, seq, d), *[out_g[n] for n in WEIGHTS], *[out_d[n] for n in WEIGHTS],
            *[out_m[n] for n in WEIGHTS], *[out_v[n] for n in WEIGHTS])
```

```python
import jax
import jax.numpy as jnp
from jax import lax
from jax.experimental import pallas as pl
from jax.experimental.pallas import tpu as pltpu

F32 = jnp.float32
BF16 = jnp.bfloat16
NDEV = 8
MESH_AXES = ("x", "y", "c")
EPS = 1e-6
POOL_WINDOWS = (2, 4, 8, 16)
ADAM_LR = 0.001
ADAM_B1 = 0.9
ADAM_B2 = 0.999
ADAM_EPS = 1e-08
ADAM_WD = 0.01
ADAM_STEP = 10
LANES = 128
SUBLANES = 8
VMEM_LIMIT = 56 * 1024 * 1024
MXU_DEPTH = 256
MM_TK = 2816
MESH = pl.DeviceIdType.MESH
ANY = pl.BlockSpec(memory_space=pl.ANY)


def _cp(*sem):
    return pltpu.CompilerParams(dimension_semantics=sem, vmem_limit_bytes=VMEM_LIMIT)


def _tile(n, pref, unit=LANES):
    if n <= pref:
        return n
    t = (pref // unit) * unit
    while t > unit and n % t:
        t -= unit
    assert n % t == 0, (n, pref)
    return t


def _sigmoid(v):
    return 0.5 * jnp.tanh(0.5 * v) + 0.5


def _mm(name, pairs, a_specs, b_specs, dims, out_shape, o_spec, grid, acc_shape,
        res=None, res_spec=None, dep=None):
    np_ = len(pairs)
    nk = grid[2]
    has_res = res is not None
    n_in = 2 * np_ + (1 if has_res else 0) + (0 if dep is None else 1)

    def body(*refs):
        a_refs = refs[:np_]
        b_refs = refs[np_:2 * np_]
        r_ref = refs[2 * np_] if has_res else None
        o_ref = refs[n_in]
        acc = refs[-1]

        def part():
            s = None
            for a_ref, b_ref in zip(a_refs, b_refs):
                d = lax.dot_general(a_ref[...], b_ref[...], dims, preferred_element_type=F32)
                s = d if s is None else s + d
            return s

        def finish(v):
            if has_res:
                v = v + r_ref[...]
            o_ref[...] = v.astype(o_ref.dtype)

        if nk == 1:
            finish(part())
        else:
            k = pl.program_id(2)

            @pl.when(k == 0)
            def _():
                acc[...] = part()

            @pl.when((k > 0) & (k < nk - 1))
            def _():
                acc[...] += part()

            @pl.when(k == nk - 1)
            def _():
                finish(acc[...] + part())

    ins = [p[0] for p in pairs] + [p[1] for p in pairs]
    specs = list(a_specs) + list(b_specs)
    if has_res:
        ins.append(res)
        specs.append(res_spec)
    if dep is not None:
        ins.append(dep)
        specs.append(ANY)
    return pl.pallas_call(
        body, name=name, grid=grid, in_specs=specs, out_specs=o_spec, out_shape=out_shape,
        scratch_shapes=[pltpu.VMEM(acc_shape if nk > 1 else (SUBLANES, LANES), F32)],
        compiler_params=_cp("parallel", "parallel", "arbitrary"))(*ins)


NN = (((1,), (0,)), ((), ()))
NT = (((1,), (1,)), ((), ()))
TN = (((0,), (0,)), ((), ()))


def _tiles_mk(m, kk):
    return _tile(m, 1024), _tile(kk, MM_TK, MXU_DEPTH)


def _mm_nn(name, a, b, out_dtype, res=None, dep=None):
    pairs = list(zip(a, b)) if isinstance(a, (list, tuple)) else [(a, b)]
    m, kk = pairs[0][0].shape
    n = pairs[0][1].shape[1]
    tm, tk = _tiles_mk(m, kk)
    tn = _tile(n, 1024 if tk * len(pairs) <= MM_TK else 512)
    return _mm(name, pairs,
               [pl.BlockSpec((tm, tk), lambda i, j, k: (i, k))] * len(pairs),
               [pl.BlockSpec((tk, tn), lambda i, j, k: (k, j))] * len(pairs), NN,
               jax.ShapeDtypeStruct((m, n), out_dtype),
               pl.BlockSpec((tm, tn), lambda i, j, k: (i, j)),
               (m // tm, n // tn, kk // tk), (tm, tn), res,
               pl.BlockSpec((tm, tn), lambda i, j, k: (i, j)), dep=dep)


def _mm_nt(name, a, b, out_dtype, dep=None):
    m, n = a.shape
    kk = b.shape[0]
    tn = _tile(kk, 1024)
    tm, tk = _tiles_mk(m, n)
    return _mm(name, [(a, b)],
               [pl.BlockSpec((tm, tk), lambda i, j, k: (i, k))],
               [pl.BlockSpec((tn, tk), lambda i, j, k: (j, k))], NT,
               jax.ShapeDtypeStruct((m, kk), out_dtype),
               pl.BlockSpec((tm, tn), lambda i, j, k: (i, j)),
               (m // tm, kk // tn, n // tk), (tm, tn), dep=dep)


def _mm_tn(name, a, b, out_dtype, dep=None):
    t, m = a.shape
    n = b.shape[1]
    tn = _tile(n, 1024)
    tm, tk = _tile(m, 1408), _tile(t, MM_TK, MXU_DEPTH)
    return _mm(name, [(a, b)],
               [pl.BlockSpec((tk, tm), lambda i, j, k: (k, i))],
               [pl.BlockSpec((tk, tn), lambda i, j, k: (k, j))], TN,
               jax.ShapeDtypeStruct((m, n), out_dtype),
               pl.BlockSpec((tm, tn), lambda i, j, k: (i, j)),
               (m // tm, n // tn, t // tk), (tm, tn), dep=dep)


def _ffn_fwd(name, n, wg, wu, dep=None):
    f, d = wg.shape
    t = n.shape[0]
    tm, tn = _tile(t, 1024), _tile(f, 512)

    def body(n_ref, wg_ref, wu_ref, *rest):
        act_ref, ds_ref, s_ref = rest[-3:]
        nv = n_ref[...]
        g = lax.dot_general(nv, wg_ref[...], NT, preferred_element_type=F32)
        up = lax.dot_general(nv, wu_ref[...], NT, preferred_element_type=F32)
        sg = _sigmoid(g)
        silu = g * sg
        act_ref[...] = (silu * up).astype(BF16)
        ds_ref[...] = (up * (sg * (1.0 + g * (1.0 - sg)))).astype(BF16)
        s_ref[...] = silu.astype(BF16)

    w_spec = pl.BlockSpec((tn, d), lambda j, i: (j, 0))
    o_spec = pl.BlockSpec((tm, tn), lambda j, i: (i, j))
    shp = jax.ShapeDtypeStruct((t, f), BF16)
    return pl.pallas_call(
        body, name=name, grid=(f // tn, t // tm),
        in_specs=[pl.BlockSpec((tm, d), lambda j, i: (i, 0)), w_spec, w_spec] + ([] if dep is None else [ANY]),
        out_specs=[o_spec, o_spec, o_spec], out_shape=[shp, shp, shp],
        compiler_params=_cp("parallel", "parallel"))(n, wg, wu, *([] if dep is None else [dep]))


def _ffn_bwd_act(name, dh, wd, dsilu, silu, dep=None):
    f, d = wd.shape
    t = dh.shape[0]
    tm, tn = _tile(t, 1024), _tile(f, 512)

    def body(dh_ref, wd_ref, ds_ref, s_ref, *rest):
        dg_ref, dup_ref = rest[-2:]
        da = lax.dot_general(dh_ref[...], wd_ref[...], NT, preferred_element_type=F32)
        dg_ref[...] = (da * ds_ref[...].astype(F32)).astype(BF16)
        dup_ref[...] = (da * s_ref[...].astype(F32)).astype(BF16)

    o_spec = pl.BlockSpec((tm, tn), lambda i, j: (i, j))
    shp = jax.ShapeDtypeStruct((t, f), BF16)
    return pl.pallas_call(
        body, name=name, grid=(t // tm, f // tn),
        in_specs=[pl.BlockSpec((tm, d), lambda i, j: (i, 0)),
                  pl.BlockSpec((tn, d), lambda i, j: (j, 0)), o_spec, o_spec]
        + ([] if dep is None else [ANY]),
        out_specs=[o_spec, o_spec], out_shape=[shp, shp],
        compiler_params=_cp("parallel", "parallel"))(dh, wd, dsilu, silu, *([] if dep is None else [dep]))


def _rms_fwd(name, h, gain):
    t, d = h.shape
    tr = _tile(t, 512, SUBLANES)

    def body(h_ref, g_ref, n_ref):
        hv = h_ref[...]
        r = lax.rsqrt(jnp.mean(hv * hv, axis=-1, keepdims=True) + EPS)
        n_ref[...] = (hv * r * g_ref[...]).astype(BF16)

    return pl.pallas_call(
        body, name=name, grid=(t // tr,),
        in_specs=[pl.BlockSpec((tr, d), lambda i: (i, 0)), pl.BlockSpec((1, d), lambda i: (0, 0))],
        out_specs=pl.BlockSpec((tr, d), lambda i: (i, 0)),
        out_shape=jax.ShapeDtypeStruct((t, d), BF16),
        compiler_params=_cp("parallel"))(h, gain)


def _rms_bwd_math(hv, gain, dn):
    d = hv.shape[-1]
    r = lax.rsqrt(jnp.mean(hv * hv, axis=-1, keepdims=True) + EPS)
    xhat = hv * r
    dxh = dn * gain
    dh = r * (dxh - xhat * (jnp.sum(dxh * xhat, axis=-1, keepdims=True) / d))
    dgain = jnp.sum(dn * xhat, axis=0, keepdims=True)
    return dh, dgain


def _rms_bwd(name, h, gain, dn, dres, bf16_too=True, dep=None):
    t, d = h.shape
    tr = _tile(t, 256, SUBLANES)

    def body(h_ref, g_ref, dn_ref, dr_ref, *rest):
        outs = rest[0 if dep is None else 1:]
        dh_ref, dg_ref = outs[0], outs[-1]
        dh, dgain = _rms_bwd_math(h_ref[...], g_ref[...], dn_ref[...].astype(F32))
        dh = dh + dr_ref[...]
        dh_ref[...] = dh
        if bf16_too:
            outs[1][...] = dh.astype(BF16)

        @pl.when(pl.program_id(0) == 0)
        def _():
            dg_ref[...] = dgain

        @pl.when(pl.program_id(0) > 0)
        def _():
            dg_ref[...] += dgain

    row = pl.BlockSpec((tr, d), lambda i: (i, 0))
    vec = pl.BlockSpec((1, d), lambda i: (0, 0))
    halves = [jax.ShapeDtypeStruct((t, d), BF16)] if bf16_too else []
    out = pl.pallas_call(
        body, name=name, grid=(t // tr,), in_specs=[row, vec, row, row] + ([] if dep is None else [ANY]),
        out_specs=[row] + [row] * len(halves) + [vec],
        out_shape=[jax.ShapeDtypeStruct((t, d), F32)] + halves + [jax.ShapeDtypeStruct((1, d), F32)],
        compiler_params=_cp("arbitrary"))(h, gain, dn, dres, *([] if dep is None else [dep]))
    return (out[0], out[1], out[2]) if bf16_too else (out[0], None, out[1])


def _loss_head(name, h, gain, tgt):
    t, d = h.shape
    tr = _tile(t, 256, SUBLANES)

    def body(h_ref, g_ref, t_ref, dh_ref, dhb_ref, dg_ref, ls_ref):
        hv = h_ref[...]
        gv = g_ref[...]
        r = lax.rsqrt(jnp.mean(hv * hv, axis=-1, keepdims=True) + EPS)
        err = hv * r * gv - t_ref[...]
        lsum = 0.5 * jnp.sum(err * err, axis=0, keepdims=True) / d
        dh, dgain = _rms_bwd_math(hv, gv, err / d)
        dh_ref[...] = dh
        dhb_ref[...] = dh.astype(BF16)

        @pl.when(pl.program_id(0) == 0)
        def _():
            dg_ref[...] = dgain
            ls_ref[...] = lsum

        @pl.when(pl.program_id(0) > 0)
        def _():
            dg_ref[...] += dgain
            ls_ref[...] += lsum

    row = pl.BlockSpec((tr, d), lambda i: (i, 0))
    vec = pl.BlockSpec((1, d), lambda i: (0, 0))
    return pl.pallas_call(
        body, name=name, grid=(t // tr,), in_specs=[row, vec, row],
        out_specs=[row, row, vec, vec],
        out_shape=[jax.ShapeDtypeStruct((t, d), F32), jax.ShapeDtypeStruct((t, d), BF16),
                   jax.ShapeDtypeStruct((1, d), F32), jax.ShapeDtypeStruct((1, d), F32)],
        compiler_params=_cp("arbitrary"))(h, gain, tgt)


def _conv_geom(t, seq, c, k, full_width=False):
    halo = 32 if k - 1 > SUBLANES else SUBLANES
    assert k - 1 <= halo
    tm = min(256 if halo > SUBLANES else 1024, seq // 2)
    tc = c if full_width else min(512, c)
    assert seq % tm == 0 and tm % halo == 0 and c % tc == 0 and t % seq == 0
    return halo, tm, tc, min(64 if halo > SUBLANES else 128, tm), min(LANES, tc)


def _pre(kind, a, b):
    if kind == "glu":
        return a * _sigmoid(b)
    if kind == "mul":
        return a * b
    return a


def _taps(k):
    return sorted((s % SUBLANES, s // SUBLANES, s) for s in range(k))


def _conv_fwd(name, seq, c, w, x1, c1, x2=None, c2=0, pre=None, bias=None, post=None, cpost=0, live=None):
    t = x1.shape[0]
    k = w.shape[0]
    halo, tm, tc, sr, sl = _conv_geom(t, seq, c, k, live is not None)
    nb, cps = tm // halo, seq // tm
    two = x2 is not None
    has_bias, has_post = bias is not None, post is not None

    def body(*refs):
        it = iter(refs)
        x1c, x1h = next(it), next(it)
        x2c, x2h = (next(it), next(it)) if two else (None, None)
        w_ref = next(it)
        b_ref = next(it) if has_bias else None
        p_ref = next(it) if has_post else None
        o_ref = next(it)
        y_ref = next(it) if has_post else None
        xs = next(it)
        first = (pl.program_id(1) % cps) == 0
        hv = _pre(pre, x1h[...].astype(F32), x2h[...].astype(F32) if two else None)
        xs[0:halo, :] = jnp.where(first, 0.0, hv)
        xs[halo:halo + tm, :] = _pre(pre, x1c[...].astype(F32), x2c[...].astype(F32) if two else None)
        for l0 in range(0, tc, sl):
            ls = slice(l0, l0 + sl)
            for r0 in range(0, tm, sr):
                win = xs[r0:r0 + sr + halo, ls]
                acc = jnp.zeros((sr, sl), F32)
                rolled = {}
                for r, q, s in _taps(k if live is None else live[l0 // sl]):
                    if r not in rolled:
                        rolled[r] = win if r == 0 else pltpu.roll(win, r, 0)
                    lo = halo - SUBLANES * q
                    acc = acc + w_ref[k - 1 - s:k - s, ls] * rolled[r][lo:lo + sr]
                if has_bias:
                    acc = acc + b_ref[:, ls]
                o_ref[r0:r0 + sr, ls] = acc.astype(o_ref.dtype)
                if has_post:
                    y_ref[r0:r0 + sr, ls] = (acc * p_ref[r0:r0 + sr, ls].astype(F32)).astype(y_ref.dtype)

    def cur(off):
        return pl.BlockSpec((tm, tc), lambda j, i: (i, off // tc + j))

    def prev(off):
        return pl.BlockSpec((halo, tc), lambda j, i: (jnp.maximum(i * nb - 1, 0), off // tc + j))

    ins, specs = [x1, x1], [cur(c1), prev(c1)]
    if two:
        ins += [x2, x2]
        specs += [cur(c2), prev(c2)]
    ins.append(w)
    specs.append(pl.BlockSpec((k, tc), lambda j, i: (0, j)))
    if has_bias:
        ins.append(bias)
        specs.append(pl.BlockSpec((1, tc), lambda j, i: (0, j)))
    if has_post:
        ins.append(post)
        specs.append(cur(cpost))
    o_spec = pl.BlockSpec((tm, tc), lambda j, i: (i, j))
    shp = jax.ShapeDtypeStruct((t, c), BF16)
    return pl.pallas_call(
        body, name=name, grid=(c // tc, t // tm), in_specs=specs,
        out_specs=[o_spec, o_spec] if has_post else o_spec,
        out_shape=[shp, shp] if has_post else shp,
        scratch_shapes=[pltpu.VMEM((halo + tm, tc), F32)],
        compiler_params=_cp("parallel", "parallel"))(*ins)


def _conv_bwd(name, seq, c, w, d1, cd1, d2=None, cd2=0, dpre=None,
              x1=None, c1=0, x2=None, c2=0, pre=None, live=None, dep=None):
    t = d1.shape[0]
    k = w.shape[0]
    assert live is None or x1 is None
    halo, tm, tc, sr, sl = _conv_geom(t, seq, c, k, live is not None)
    nb, cps = tm // halo, seq // tm
    nchunks = t // tm
    dtwo, xtwo, has_x = d2 is not None, x2 is not None, x1 is not None

    def body(*refs):
        it = iter(refs)
        d1c, d1n = next(it), next(it)
        d2c, d2n = (next(it), next(it)) if dtwo else (None, None)
        x1c, x1h = (next(it), next(it)) if has_x else (None, None)
        x2c, x2h = (next(it), next(it)) if xtwo else (None, None)
        w_ref = next(it)
        if dep is not None:
            next(it)
        dx_ref = next(it)
        dw_ref = next(it) if has_x else None
        ds = next(it)
        xs = next(it) if has_x else None
        i = pl.program_id(1)
        last = (i % cps) == cps - 1
        ds[0:tm, :] = _pre(dpre, d1c[...].astype(F32), d2c[...].astype(F32) if dtwo else None)
        nv = _pre(dpre, d1n[...].astype(F32), d2n[...].astype(F32) if dtwo else None)
        ds[tm:tm + halo, :] = jnp.where(last, 0.0, nv)
        if has_x:
            first = (i % cps) == 0
            hv = _pre(pre, x1h[...].astype(F32), x2h[...].astype(F32) if xtwo else None)
            xs[0:halo, :] = jnp.where(first, 0.0, hv)
            xs[halo:halo + tm, :] = _pre(pre, x1c[...].astype(F32), x2c[...].astype(F32) if xtwo else None)

            @pl.when(i == 0)
            def _():
                dw_ref[...] = jnp.zeros_like(dw_ref)

        for l0 in range(0, tc, sl):
            ls = slice(l0, l0 + sl)
            for r0 in range(0, tm, sr):
                win = ds[r0:r0 + sr + halo, ls]
                nrow = sr + halo
                acc = jnp.zeros((sr, sl), F32)
                rolled = {}
                for r, q, s in _taps(k if live is None else live[l0 // sl]):
                    if r not in rolled:
                        rolled[r] = win if r == 0 else pltpu.roll(win, nrow - r, 0)
                    lo = SUBLANES * q
                    acc = acc + w_ref[k - 1 - s:k - s, ls] * rolled[r][lo:lo + sr]
                dx_ref[r0:r0 + sr, ls] = acc.astype(dx_ref.dtype)
                if has_x:
                    dcur = win[0:sr]
                    xwin = xs[r0:r0 + sr + halo, ls]
                    xrolled = {}
                    for r, q, s in _taps(k):
                        if r not in xrolled:
                            xrolled[r] = xwin if r == 0 else pltpu.roll(xwin, r, 0)
                        lo = halo - SUBLANES * q
                        part = jnp.sum(dcur * xrolled[r][lo:lo + sr], axis=0, keepdims=True)
                        dw_ref[k - 1 - s:k - s, ls] += part

    def cur(off):
        return pl.BlockSpec((tm, tc), lambda j, i: (i, off // tc + j))

    def prev(off):
        return pl.BlockSpec((halo, tc), lambda j, i: (jnp.maximum(i * nb - 1, 0), off // tc + j))

    def nxt(off):
        return pl.BlockSpec((halo, tc),
                            lambda j, i: (jnp.minimum((i + 1) * nb, nchunks * nb - 1), off // tc + j))

    ins, specs = [d1, d1], [cur(cd1), nxt(cd1)]
    if dtwo:
        ins += [d2, d2]
        specs += [cur(cd2), nxt(cd2)]
    if has_x:
        ins += [x1, x1]
        specs += [cur(c1), prev(c1)]
    if xtwo:
        ins += [x2, x2]
        specs += [cur(c2), prev(c2)]
    ins.append(w)
    specs.append(pl.BlockSpec((k, tc), lambda j, i: (0, j)))
    if dep is not None:
        ins.append(dep)
        specs.append(ANY)
    o_specs = [pl.BlockSpec((tm, tc), lambda j, i: (i, j))]
    o_shapes = [jax.ShapeDtypeStruct((t, c), BF16)]
    scratch = [pltpu.VMEM((tm + halo, tc), F32)]
    if has_x:
        o_specs.append(pl.BlockSpec((k, tc), lambda j, i: (0, j)))
        o_shapes.append(jax.ShapeDtypeStruct((k, c), F32))
        scratch.append(pltpu.VMEM((halo + tm, tc), F32))
    out = pl.pallas_call(
        body, name=name, grid=(c // tc, t // tm), in_specs=specs, out_specs=o_specs,
        out_shape=o_shapes, scratch_shapes=scratch,
        compiler_params=_cp("parallel", "arbitrary"))(*ins)
    return out if has_x else out[0]


def _pool_taps(c):
    kmax = max(POOL_WINDOWS)
    grp = c // len(POOL_WINDOWS)
    cols = []
    for wdw in POOL_WINDOWS:
        col = jnp.concatenate([jnp.zeros((kmax - wdw,), F32), jnp.ones((wdw,), F32)])
        cols.append(jnp.tile(col[:, None], (1, grp)))
    return jnp.concatenate(cols, axis=1)


def _pool_live(c):
    grp, sl = c // len(POOL_WINDOWS), min(LANES, c)
    return tuple(max(POOL_WINDOWS[g] for g in range(l0 // grp, (l0 + sl - 1) // grp + 1))
                 for l0 in range(0, c, sl))


def _counts(i, tr, seq, grp):
    pos = (i * tr + lax.broadcasted_iota(jnp.int32, (tr, 1), 0)) % seq + 1
    return [1.0 / jnp.minimum(pos, wdw).astype(F32) for wdw in POOL_WINDOWS]


def _ln_stats(a2):
    mu = jnp.mean(a2, axis=-1, keepdims=True)
    xc = a2 - mu
    rstd = lax.rsqrt(jnp.mean(xc * xc, axis=-1, keepdims=True) + EPS)
    return xc * rstd, rstd


def _even_fwd(name, seq, a2, ws, u, ln_g, ln_b, w_pool, scale):
    t, c = a2.shape
    ng = len(POOL_WINDOWS)
    grp = c // ng
    tr = _tile(t, 256, SUBLANES)

    def body(a_ref, ws_ref, b_ref, g_ref, bb_ref, wp_ref, sc_ref, z_ref, pm_ref):
        xhat, _ = _ln_stats(a_ref[...].astype(F32))
        l = xhat * g_ref[...] + bb_ref[...]
        z_ref[:, 0:c] = (l * _sigmoid(l)).astype(BF16)
        inv = _counts(pl.program_id(0), tr, seq, grp)
        for g in range(ng):
            gs = slice(g * grp, (g + 1) * grp)
            pm = (ws_ref[:, gs].astype(F32) * inv[g] - b_ref[:, gs].astype(F32)).astype(BF16)
            pm_ref[:, gs] = pm
            q = jnp.dot(pm, wp_ref[g], preferred_element_type=F32)
            z_ref[:, c + g * grp:c + (g + 1) * grp] = (q * sc_ref[:, gs]).astype(BF16)

    row = pl.BlockSpec((tr, c), lambda i: (i, 0))
    vec = pl.BlockSpec((1, c), lambda i: (0, 0))
    return pl.pallas_call(
        body, name=name, grid=(t // tr,),
        in_specs=[row, row, pl.BlockSpec((tr, c), lambda i: (i, 2)), vec, vec,
                  pl.BlockSpec((ng, grp, grp), lambda i: (0, 0, 0)), vec],
        out_specs=[pl.BlockSpec((tr, 2 * c), lambda i: (i, 0)), row],
        out_shape=[jax.ShapeDtypeStruct((t, 2 * c), BF16), jax.ShapeDtypeStruct((t, c), BF16)],
        compiler_params=_cp("parallel"))(a2, ws, u, ln_g, ln_b, w_pool, scale)


def _even_bwd(name, seq, dz, a2, pm, ln_g, ln_b, w_pool, scale):
    t, c = a2.shape
    ng = len(POOL_WINDOWS)
    grp = c // ng
    tr = _tile(t, 256, SUBLANES)

    def body(dz_ref, a_ref, pm_ref, g_ref, bb_ref, wp_ref, sc_ref,
             da_ref, dws_ref, dpm_ref, vec_ref, dwp_ref):
        i = pl.program_id(0)

        @pl.when(i == 0)
        def _():
            vec_ref[...] = jnp.zeros_like(vec_ref)
            dwp_ref[...] = jnp.zeros_like(dwp_ref)

        xhat, rstd = _ln_stats(a_ref[...].astype(F32))
        gv = g_ref[...]
        l = xhat * gv + bb_ref[...]
        sg = _sigmoid(l)
        dl = dz_ref[:, 0:c].astype(F32) * (sg * (1.0 + l * (1.0 - sg)))
        dxh = dl * gv
        da2 = rstd * (dxh - jnp.mean(dxh, axis=-1, keepdims=True)
                      - xhat * jnp.mean(dxh * xhat, axis=-1, keepdims=True))
        da_ref[...] = da2.astype(BF16)
        vec_ref[0:1, :] += jnp.sum(dl * xhat, axis=0, keepdims=True)
        vec_ref[1:2, :] += jnp.sum(dl, axis=0, keepdims=True)
        vec_ref[2:3, :] += jnp.sum(da2, axis=0, keepdims=True)
        inv = _counts(i, tr, seq, grp)
        for g in range(ng):
            gs = slice(g * grp, (g + 1) * grp)
            pmv = pm_ref[:, gs]
            wp = wp_ref[g]
            dp = dz_ref[:, c + g * grp:c + (g + 1) * grp].astype(F32)
            q = jnp.dot(pmv, wp, preferred_element_type=F32)
            vec_ref[3:4, gs] += jnp.sum(dp * q, axis=0, keepdims=True)
            dq = (dp * sc_ref[:, gs]).astype(BF16)
            dpm = lax.dot_general(dq, wp, NT, preferred_element_type=F32)
            dwp_ref[g] += lax.dot_general(pmv, dq, TN, preferred_element_type=F32)
            dpm_ref[:, gs] = dpm.astype(BF16)
            dws_ref[:, gs] = (dpm * inv[g]).astype(BF16)

    row = pl.BlockSpec((tr, c), lambda i: (i, 0))
    vec = pl.BlockSpec((1, c), lambda i: (0, 0))
    rshape = jax.ShapeDtypeStruct((t, c), BF16)
    return pl.pallas_call(
        body, name=name, grid=(t // tr,),
        in_specs=[pl.BlockSpec((tr, 2 * c), lambda i: (i, 0)), row, row, vec, vec,
                  pl.BlockSpec((ng, grp, grp), lambda i: (0, 0, 0)), vec],
        out_specs=[row, row, row, pl.BlockSpec((SUBLANES, c), lambda i: (0, 0)),
                   pl.BlockSpec((ng, grp, grp), lambda i: (0, 0, 0))],
        out_shape=[rshape, rshape, rshape, jax.ShapeDtypeStruct((SUBLANES, c), F32),
                   jax.ShapeDtypeStruct((ng, grp, grp), F32)],
        compiler_params=_cp("arbitrary"))(dz, a2, pm, ln_g, ln_b, w_pool, scale)


def _even_du(name, u, da1, dbp, dpm):
    t, c = da1.shape
    tr = _tile(t, 256, SUBLANES)

    def body(u_ref, da_ref, dbp_ref, dpm_ref, du_ref):
        val = u_ref[:, 0:c].astype(F32)
        sg = _sigmoid(u_ref[:, c:2 * c].astype(F32))
        da = da_ref[...].astype(F32)
        du_ref[:, 0:c] = (da * sg).astype(BF16)
        du_ref[:, c:2 * c] = (da * val * sg * (1.0 - sg)).astype(BF16)
        du_ref[:, 2 * c:3 * c] = (dbp_ref[...].astype(F32) - dpm_ref[...].astype(F32)).astype(BF16)

    row = pl.BlockSpec((tr, c), lambda i: (i, 0))
    wide = pl.BlockSpec((tr, 3 * c), lambda i: (i, 0))
    return pl.pallas_call(
        body, name=name, grid=(t // tr,), in_specs=[wide, row, row, row], out_specs=wide,
        out_shape=jax.ShapeDtypeStruct((t, 3 * c), BF16),
        compiler_params=_cp("parallel"))(u, da1, dbp, dpm)


def _odd_du(name, u, dy, co, dxc):
    t, c = dy.shape
    tr = _tile(t, 256, SUBLANES)

    def body(u_ref, dy_ref, co_ref, dx_ref, du_ref):
        dx = dx_ref[...].astype(F32)
        du_ref[:, 0:c] = (dy_ref[...].astype(F32) * co_ref[...].astype(F32)).astype(BF16)
        du_ref[:, c:2 * c] = (dx * u_ref[:, 2 * c:3 * c].astype(F32)).astype(BF16)
        du_ref[:, 2 * c:3 * c] = (dx * u_ref[:, c:2 * c].astype(F32)).astype(BF16)

    row = pl.BlockSpec((tr, c), lambda i: (i, 0))
    wide = pl.BlockSpec((tr, 3 * c), lambda i: (i, 0))
    return pl.pallas_call(
        body, name=name, grid=(t // tr,), in_specs=[wide, row, row, row], out_specs=wide,
        out_shape=jax.ShapeDtypeStruct((t, 3 * c), BF16),
        compiler_params=_cp("parallel"))(u, dy, co, dxc)


def _local_step(x, tgt, seq, small, get_w, put_g, sync):
    t, d = x.shape
    c = d // 2
    cw_e, cw_o = small["conv_w_e"], small["conv_w_o"]
    wp = small["w_pool_e"].astype(BF16)
    ptaps = _pool_taps(c)
    row = lambda v: v.reshape(1, -1)

    we = {"w_in": get_w("in_e", x)[0]}
    n0 = _rms_fwd("rms_fwd_mix0", x, row(small["mix_norm_e"]))
    u0 = _mm_nn("mm_in_e", n0, we["w_in"], BF16)
    sync("fwd_a", u0)
    a2 = _conv_fwd("conv_e_fwd", seq, c, cw_e, u0, 0, u0, c, "glu", bias=row(small["conv_b_e"]))
    ws = _conv_fwd("pool_fwd", seq, c, ptaps, u0, 2 * c, live=_pool_live(c))
    z0, pm = _even_fwd("even_fwd", seq, a2, ws, u0, row(small["ln_g_e"]), row(small["ln_b_e"]),
                       wp, row(small["pool_scale_e"]))
    we["w_out"] = get_w("out_e", z0)[0]
    h1 = _mm_nn("mm_out_e", z0, we["w_out"], F32, res=x)
    sync("fwd_b", h1)
    n1 = _rms_fwd("rms_fwd_ffn0", h1, row(small["ffn_norm"][0]))
    wf0 = dict(zip(("w_gate", "w_up"), get_w("gu0", n1)))
    act0, ds0, s0 = _ffn_fwd("ffn0_fwd", n1, wf0["w_gate"], wf0["w_up"])
    dep = sync("fwd_c", act0)
    wf0["w_down"] = get_w("down0", act0)[0]
    h2 = _mm_nn("mm_down0", act0, wf0["w_down"], F32, res=h1, dep=dep)
    dep = sync("fwd_d", h2)
    n2 = _rms_fwd("rms_fwd_mix1", h2, row(small["mix_norm_o"]))
    wo = {"w_in": get_w("in_o", n2)[0]}
    u1 = _mm_nn("mm_in_o", n2, wo["w_in"], BF16, dep=dep)
    co, y1 = _conv_fwd("conv_o_fwd", seq, d, cw_o, u1, d, u1, 2 * d, "mul", post=u1, cpost=0)
    dep = sync("fwd_e", y1)
    wo["w_out"] = get_w("out_o", y1)[0]
    h3 = _mm_nn("mm_out_o", y1, wo["w_out"], F32, res=h2, dep=dep)
    dep = sync("fwd_f", h3)
    n3 = _rms_fwd("rms_fwd_ffn1", h3, row(small["ffn_norm"][1]))
    wf1 = dict(zip(("w_gate", "w_up"), get_w("gu1", n3)))
    act1, ds1, s1 = _ffn_fwd("ffn1_fwd", n3, wf1["w_gate"], wf1["w_up"], dep=dep)
    wf1["w_down"] = get_w("down1", act1)[0]
    h4 = _mm_nn("mm_down1", act1, wf1["w_down"], F32, res=h3)

    dh4, dh4b, d_final, lsum = _loss_head("loss_head", h4, row(small["final_norm"]), tgt)

    def ffn_bwd(tag, dh, dhb, h_in, gain, n, dsilu, silu, act, w, dep):
        dg, dup = _ffn_bwd_act("ffn%s_bwd_act" % tag, dhb, w["w_down"], dsilu, silu, dep=dep)
        dwd = _mm_tn("mm_dwd%s" % tag, act, dhb, BF16)
        dwg = _mm_tn("mm_dwg%s" % tag, dg, n, BF16, dep=sync("bwd_ffn" + tag, dwd))
        dwu = _mm_tn("mm_dwu%s" % tag, dup, n, BF16)
        dn = _mm_nn("mm_ffn_dn%s" % tag, [dg, dup], [w["w_gate"], w["w_up"]], BF16)
        dh_in, dhb_in, dgain = _rms_bwd("rms_bwd_ffn%s" % tag, h_in, gain, dn, dh)
        dep = put_g("ffn" + tag, {"w_gate": dwg, "w_up": dwu, "w_down": dwd})
        return dh_in, dhb_in, dgain, dep

    dh3, dh3b, d_ffn1, dep = ffn_bwd("1", dh4, dh4b, h3, row(small["ffn_norm"][1]), n3, ds1, s1,
                                     act1, wf1, None)
    dw_out_o = _mm_tn("mm_dw_out_o", y1, dh3b, BF16, dep=dep)
    dy1 = _mm_nt("mm_dy_o", dh3b, wo["w_out"], BF16, dep=sync("bwd_mix_o", dw_out_o))
    dxc, dcw_o = _conv_bwd("conv_o_bwd", seq, d, cw_o, dy1, 0, u1, 0, "mul",
                           x1=u1, c1=d, x2=u1, c2=2 * d, pre="mul")
    du1 = _odd_du("odd_du", u1, dy1, co, dxc)
    dw_in_o = _mm_tn("mm_dw_in_o", n2, du1, BF16)
    dn2 = _mm_nt("mm_dn_o", du1, wo["w_in"], BF16)
    dh2, dh2b, d_mix_o = _rms_bwd("rms_bwd_mix1", h2, row(small["mix_norm_o"]), dn2, dh3)
    dep = put_g("mix_o", {"w_in": dw_in_o, "w_out": dw_out_o})

    dh1, dh1b, d_ffn0, dep = ffn_bwd("0", dh2, dh2b, h1, row(small["ffn_norm"][0]), n1, ds0, s0,
                                     act0, wf0, dep)
    dw_out_e = _mm_tn("mm_dw_out_e", z0, dh1b, BF16, dep=dep)
    dz0 = _mm_nt("mm_dz_e", dh1b, we["w_out"], BF16, dep=sync("bwd_mix_e", dw_out_e))
    da2, dws, dpm, vecs, dwp = _even_bwd("even_bwd", seq, dz0, a2, pm, row(small["ln_g_e"]),
                                         row(small["ln_b_e"]), wp, row(small["pool_scale_e"]))
    dep = put_g("small", {"conv_b_e": vecs[2], "ln_g_e": vecs[0], "ln_b_e": vecs[1],
                          "w_pool_e": dwp, "pool_scale_e": vecs[3], "mix_norm_o": d_mix_o[0],
                          "conv_w_o": dcw_o, "ffn_norm": jnp.concatenate([d_ffn0, d_ffn1], axis=0),
                          "final_norm": d_final[0], "loss": lsum})
    da1, dcw_e = _conv_bwd("conv_e_bwd", seq, c, cw_e, da2, 0, x1=u0, c1=0, x2=u0, c2=c, pre="glu", dep=dep)
    dbp = _conv_bwd("pool_bwd", seq, c, ptaps, dws, 0, live=_pool_live(c))
    du0 = _even_du("even_du", u0, da1, dbp, dpm)
    dw_in_e = _mm_tn("mm_dw_in_e", n0, du0, BF16)
    dep = put_g("mix_e", {"w_in": dw_in_e, "w_out": dw_out_e})
    dn0 = _mm_nt("mm_dn_e", du0, we["w_in"], BF16, dep=dep)
    dx, _, d_mix_e = _rms_bwd("rms_bwd_mix0", x, row(small["mix_norm_e"]), dn0, dh1, bf16_too=False,
                              dep=sync("bwd_last", dn0))
    return dx, {"conv_w_e": dcw_e, "mix_norm_e": d_mix_e[0]}


def _place():
    x, y, c = (lax.axis_index(a) for a in MESH_AXES)
    return x, y, c


def _index(p):
    return 4 * p[0] + 2 * p[1] + p[2]


def _slab(ref, kind, d, n):
    if kind == "blk":
        return ref.at[d]
    return ref.at[:, pl.ds(pl.multiple_of(d * n, LANES), n)]


HBM = pl.BlockSpec(memory_space=pltpu.HBM)
SEM = pl.BlockSpec(memory_space=pltpu.SEMAPHORE)
EFFECT = pltpu.SideEffectType.DATAFLOW_SIDE_EFFECTING
NCHIPS = 4


def _in_hbm(a):
    return pltpu.with_memory_space_constraint(a, pltpu.HBM)


def _gathered_shape(s, kind):
    m, n = s.shape
    return (NDEV, m, n) if kind == "blk" else (m, NDEV * n)


def _first_targets():
    x, y, c = _place()
    return [(x, y, 1 - c), (1 - x, y, c), (x, 1 - y, c), (1 - x, 1 - y, c)]


def _gather_start(name, shards, kinds, after):
    na = len(shards)

    def body(*refs):
        x_refs, land_refs = refs[:na], refs[na:2 * na]
        send_sems, recv_sems = refs[2 * na + 1], refs[2 * na + 2]
        token = refs[-1]
        me = _index(_place())
        for a in range(na):
            for k, to in enumerate(_first_targets()):
                pltpu.make_async_remote_copy(
                    src_ref=x_refs[a], dst_ref=_slab(land_refs[a], kinds[a], me, shards[a].shape[1]),
                    send_sem=send_sems.at[4 * a + k], recv_sem=recv_sems.at[4 * a + k],
                    device_id=to, device_id_type=MESH).start()
        token[...] = jnp.zeros_like(token)

    lands = [lax.empty(_gathered_shape(s, k), s.dtype) for s, k in zip(shards, kinds)]
    outs = pl.pallas_call(
        body, name=name,
        out_shape=(pltpu.SemaphoreType.DMA((4 * na,)), pltpu.SemaphoreType.DMA((4 * na,)),
                   *[pltpu.HBM(s.shape, s.dtype) for s in shards],
                   *[pltpu.HBM(l.shape, l.dtype) for l in lands],
                   jax.ShapeDtypeStruct((SUBLANES, LANES), F32)),
        in_specs=[HBM] * (2 * na) + [ANY],
        out_specs=(SEM, SEM, *[HBM] * (2 * na), pl.BlockSpec(memory_space=pltpu.VMEM)),
        input_output_aliases={i: 2 + i for i in range(2 * na)},
        compiler_params=pltpu.CompilerParams(has_side_effects=EFFECT),
    )(*[_in_hbm(s) for s in shards], *[_in_hbm(l) for l in lands], after)
    return outs[0], outs[1], outs[2:2 + na], outs[2 + na:2 + 2 * na], outs[-1]


def _gather_wait(name, started, kinds, after):
    send_sems, recv_sems, shards, lands, _ = started
    na = len(shards)

    def body(*refs):
        x_refs, land_refs = refs[:na], refs[na:2 * na]
        s_sems, r_sems = refs[2 * na], refs[2 * na + 1]
        for a in range(na):
            for k, frm in enumerate(_first_targets()):
                cp = pltpu.make_async_remote_copy(
                    src_ref=x_refs[a],
                    dst_ref=_slab(land_refs[a], kinds[a], _index(frm), shards[a].shape[1]),
                    send_sem=s_sems.at[4 * a + k], recv_sem=r_sems.at[4 * a + k],
                    device_id=frm, device_id_type=MESH)
                cp.wait_send()
                cp.wait_recv()

    outs = pl.pallas_call(
        body, name=name,
        out_shape=(*[pltpu.HBM(s.shape, s.dtype) for s in shards],
                   *[pltpu.HBM(l.shape, l.dtype) for l in lands]),
        in_specs=[HBM] * (2 * na) + [SEM, SEM, ANY], out_specs=[HBM] * (2 * na),
        input_output_aliases={i: i for i in range(2 * na)},
        compiler_params=pltpu.CompilerParams(has_side_effects=EFFECT),
    )(*shards, *lands, send_sems, recv_sems, after)
    return outs[:na], outs[na:]


def _split_start(name, bufs, ncopies, plan, after):
    nb = len(bufs)

    def body(*refs):
        send_sems, recv_sems, token = refs[nb + 1], refs[nb + 2], refs[-1]
        for k, (src, dst, to, _) in enumerate(plan(refs[:nb])):
            pltpu.make_async_remote_copy(src_ref=src, dst_ref=dst, send_sem=send_sems.at[k],
                                         recv_sem=recv_sems.at[k], device_id=to, device_id_type=MESH).start()
        token[...] = jnp.zeros_like(token)

    outs = pl.pallas_call(
        body, name=name,
        out_shape=(pltpu.SemaphoreType.DMA((ncopies,)), pltpu.SemaphoreType.DMA((ncopies,)),
                   *[pltpu.HBM(b.shape, b.dtype) for b in bufs],
                   jax.ShapeDtypeStruct((SUBLANES, LANES), F32)),
        in_specs=[HBM] * nb + [ANY],
        out_specs=(SEM, SEM, *[HBM] * nb, pl.BlockSpec(memory_space=pltpu.VMEM)),
        input_output_aliases={i: 2 + i for i in range(nb)},
        compiler_params=pltpu.CompilerParams(has_side_effects=EFFECT),
    )(*[_in_hbm(b) for b in bufs], after)
    return outs[0], outs[1], list(outs[2:2 + nb]), outs[-1]


def _split_wait(name, started, plan, after):
    send_sems, recv_sems, bufs, _ = started
    nb = len(bufs)

    def body(*refs):
        s_sems, r_sems = refs[nb], refs[nb + 1]
        for k, (src, _, to, landed) in enumerate(plan(refs[:nb])):
            cp = pltpu.make_async_remote_copy(src_ref=src, dst_ref=landed, send_sem=s_sems.at[k],
                                              recv_sem=r_sems.at[k], device_id=to, device_id_type=MESH)
            cp.wait_send()
            cp.wait_recv()

    outs = pl.pallas_call(
        body, name=name, out_shape=tuple(pltpu.HBM(b.shape, b.dtype) for b in bufs),
        in_specs=[HBM] * nb + [SEM, SEM, ANY], out_specs=[HBM] * nb,
        input_output_aliases={i: i for i in range(nb)},
        compiler_params=pltpu.CompilerParams(has_side_effects=EFFECT),
    )(*bufs, send_sems, recv_sems, after)
    return list(outs)


def _forward_plan(kinds, nloc):
    def plan(lands):
        x, y, c = _place()
        out = []
        for a, land in enumerate(lands):
            for chip in [(1 - x, y), (x, 1 - y), (1 - x, 1 - y)]:
                mine = _slab(land, kinds[a], _index((*chip, c)), nloc[a])
                out.append((mine, mine, (x, y, 1 - c), _slab(land, kinds[a], _index((*chip, 1 - c)), nloc[a])))
        return out
    return plan


def _own_copy(name, shard, land, kind, me):
    m, n = shard.shape
    tr = _tile(m, max(SUBLANES, 1048576 // n), SUBLANES)

    def body(s_ref, x_ref, land_ref, o_ref):
        o_ref[...] = x_ref[...]

    if kind == "blk":
        o_spec = pl.BlockSpec((None, tr, n), lambda i, s: (s[0], i, 0))
    else:
        o_spec = pl.BlockSpec((tr, n), lambda i, s: (i, s[0]))
    return pl.pallas_call(
        body, name=name,
        grid_spec=pltpu.PrefetchScalarGridSpec(
            num_scalar_prefetch=1, grid=(m // tr,),
            in_specs=[pl.BlockSpec((tr, n), lambda i, s: (i, 0)), ANY], out_specs=o_spec),
        out_shape=jax.ShapeDtypeStruct(land.shape, land.dtype),
        input_output_aliases={2: 0}, compiler_params=_cp("parallel"))(me, shard, land)


def _everyone_plan(refs):
    x, y, c = _place()
    out = []
    for dx, dy, dc in [(a, b, e) for a in (0, 1) for b in (0, 1) for e in (0, 1)][1:]:
        peer = (x ^ dx, y ^ dy, c ^ dc)
        out.append((refs[0], refs[1].at[_index((x, y, c))], peer, refs[1].at[_index(peer)]))
    return out


def _pair_plan(kinds, nloc):
    na = len(kinds)

    def plan(refs):
        x, y, c = _place()
        out = []
        for a in range(na):
            for j in range(NCHIPS):
                dst = refs[na + a].at[j]
                out.append((_slab(refs[a], kinds[a], 2 * j + (1 - c), nloc[a]), dst, (x, y, 1 - c), dst))
        return out
    return plan


def _chip_sum(name, full, kind, n, from_sib, place):
    _, m, _ = from_sib.shape
    tr = _tile(m, max(SUBLANES, 1048576 // n), SUBLANES)

    def body(s_ref, mine_ref, sib_ref, csum_ref, land_ref):
        v = (mine_ref[...].astype(F32) + sib_ref[...].astype(F32)).astype(csum_ref.dtype)
        csum_ref[...] = v

        @pl.when(pl.program_id(1) == s_ref[1])
        def _():
            land_ref[...] = v

    if kind == "blk":
        mine_spec = pl.BlockSpec((None, tr, n), lambda i, j, s: (2 * j + s[0], i, 0))
    else:
        mine_spec = pl.BlockSpec((tr, n), lambda i, j, s: (i, 2 * j + s[0]))
    slot = pl.BlockSpec((None, tr, n), lambda i, j, s: (j, i, 0))
    shp = jax.ShapeDtypeStruct((NCHIPS, m, n), from_sib.dtype)
    return pl.pallas_call(
        body, name=name,
        grid_spec=pltpu.PrefetchScalarGridSpec(
            num_scalar_prefetch=1, grid=(m // tr, NCHIPS), in_specs=[mine_spec, slot],
            out_specs=[slot, pl.BlockSpec((None, tr, n), lambda i, j, s: (s[1], i, 0))]),
        out_shape=[shp, shp], compiler_params=_cp("parallel", "arbitrary"))(place, full, from_sib)


def _other_chips():
    x, y, c = _place()
    return [(1 - x, y, c), (x, 1 - y, c), (1 - x, 1 - y, c)]


def _scatter_start(name, csums, lands, after):
    na = len(csums)

    def body(*refs):
        c_refs, land_refs = refs[:na], refs[na:2 * na]
        send_sems, recv_sems = refs[2 * na + 1], refs[2 * na + 2]
        token = refs[-1]
        x, y, _ = _place()
        for a in range(na):
            for k, to in enumerate(_other_chips()):
                pltpu.make_async_remote_copy(
                    src_ref=c_refs[a].at[2 * to[0] + to[1]], dst_ref=land_refs[a].at[2 * x + y],
                    send_sem=send_sems.at[3 * a + k], recv_sem=recv_sems.at[3 * a + k],
                    device_id=to, device_id_type=MESH).start()
        token[...] = jnp.zeros_like(token)

    outs = pl.pallas_call(
        body, name=name,
        out_shape=(pltpu.SemaphoreType.DMA((3 * na,)), pltpu.SemaphoreType.DMA((3 * na,)),
                   *[pltpu.HBM(s.shape, s.dtype) for s in csums],
                   *[pltpu.HBM(l.shape, l.dtype) for l in lands],
                   jax.ShapeDtypeStruct((SUBLANES, LANES), F32)),
        in_specs=[HBM] * (2 * na) + [ANY],
        out_specs=(SEM, SEM, *[HBM] * (2 * na), pl.BlockSpec(memory_space=pltpu.VMEM)),
        input_output_aliases={i: 2 + i for i in range(2 * na)},
        compiler_params=pltpu.CompilerParams(has_side_effects=EFFECT),
    )(*[_in_hbm(s) for s in csums], *[_in_hbm(l) for l in lands], after)
    return outs[0], outs[1], outs[2:2 + na], outs[2 + na:2 + 2 * na], outs[-1]


def _scatter_wait(name, started, after):
    send_sems, recv_sems, csums, lands, _ = started
    na = len(csums)

    def body(*refs):
        c_refs, land_refs = refs[:na], refs[na:2 * na]
        s_sems, r_sems = refs[2 * na], refs[2 * na + 1]
        for a in range(na):
            for k, frm in enumerate(_other_chips()):
                cp = pltpu.make_async_remote_copy(
                    src_ref=c_refs[a].at[2 * frm[0] + frm[1]], dst_ref=land_refs[a].at[2 * frm[0] + frm[1]],
                    send_sem=s_sems.at[3 * a + k], recv_sem=r_sems.at[3 * a + k],
                    device_id=frm, device_id_type=MESH)
                cp.wait_send()
                cp.wait_recv()

    outs = pl.pallas_call(
        body, name=name,
        out_shape=(*[pltpu.HBM(s.shape, s.dtype) for s in csums],
                   *[pltpu.HBM(l.shape, l.dtype) for l in lands]),
        in_specs=[HBM] * (2 * na) + [SEM, SEM, ANY], out_specs=[HBM] * (2 * na),
        input_output_aliases={i: i for i in range(2 * na)},
        compiler_params=pltpu.CompilerParams(has_side_effects=EFFECT),
    )(*csums, *lands, send_sems, recv_sems, after)
    return outs[na:]


def _adam_math(w, g, m, v):
    m = ADAM_B1 * m + (1.0 - ADAM_B1) * g
    v = ADAM_B2 * v + (1.0 - ADAM_B2) * (g * g)
    m_hat = m / (1.0 - ADAM_B1 ** ADAM_STEP)
    v_hat = v / (1.0 - ADAM_B2 ** ADAM_STEP)
    delta = -ADAM_LR * (m_hat / (jnp.sqrt(v_hat) + ADAM_EPS) + ADAM_WD * w)
    return delta, m, v


def _sum_adamw(name, parts, w, m, v, layer, prev=None, dep=None):
    nl, r, c = w.shape
    nparts = parts.shape[0]
    tr = _tile(r, max(SUBLANES, 360448 // c), SUBLANES)

    def body(p_ref, w_ref, m_ref, v_ref, *rest):
        g_ref, d_ref, mo_ref, vo_ref = rest[-4:]
        g = p_ref[0].astype(F32)
        for s in range(1, nparts):
            g = g + p_ref[s].astype(F32)
        delta, mn, vn = _adam_math(w_ref[...], g, m_ref[...], v_ref[...])
        g_ref[...] = g
        d_ref[...] = delta
        mo_ref[...] = mn
        vo_ref[...] = vn

    row = pl.BlockSpec((None, tr, c), lambda i: (layer, i, 0))
    shp = jax.ShapeDtypeStruct((nl, r, c), F32)
    extra = ([] if prev is None else list(prev)) + ([] if dep is None else [dep])
    return pl.pallas_call(
        body, name=name, grid=(r // tr,),
        in_specs=[pl.BlockSpec((nparts, tr, c), lambda i: (0, i, 0)), row, row, row] + [ANY] * len(extra),
        out_specs=[row, row, row, row], out_shape=[shp, shp, shp, shp],
        input_output_aliases={} if prev is None else {4 + i: i for i in range(4)},
        compiler_params=_cp("parallel"))(parts, w, m, v, *extra)


def _sum_parts(name, parts):
    _, r, c = parts.shape

    def body(p_ref, o_ref):
        g = p_ref[0]
        for s in range(1, NDEV):
            g = g + p_ref[s]
        o_ref[...] = g

    return pl.pallas_call(
        body, name=name, grid=(1,),
        in_specs=[pl.BlockSpec((NDEV, r, c), lambda i: (0, 0, 0))],
        out_specs=pl.BlockSpec((r, c), lambda i: (0, 0)),
        out_shape=jax.ShapeDtypeStruct((r, c), F32), compiler_params=_cp("arbitrary"))(parts)


def _adamw(name, w, g, m, v):
    r, c = w.shape

    def body(w_ref, g_ref, m_ref, v_ref, d_ref, mo_ref, vo_ref):
        delta, mn, vn = _adam_math(w_ref[...], g_ref[...], m_ref[...], v_ref[...])
        d_ref[...] = delta
        mo_ref[...] = mn
        vo_ref[...] = vn

    full = pl.BlockSpec((r, c), lambda i: (0, 0))
    shp = jax.ShapeDtypeStruct((r, c), F32)
    return pl.pallas_call(
        body, name=name, grid=(1,), in_specs=[full] * 4, out_specs=[full] * 3,
        out_shape=[shp] * 3, compiler_params=_cp("arbitrary"))(w, g, m, v)


def _pack(arrays):
    flat = jnp.concatenate([a.reshape(-1) for a in arrays])
    unit = SUBLANES * LANES
    pad = (-flat.shape[0]) % unit
    return jnp.pad(flat, (0, pad)).reshape(-1, LANES)


def _unpack(buf, shapes):
    flat = buf.reshape(-1)
    out, off = [], 0
    for shp in shapes:
        size = 1
        for s in shp:
            size *= s
        out.append(flat[off:off + size].reshape(shp))
        off += size
    return out


WEIGHTS = ["mix_norm_e", "w_in_e", "conv_w_e", "conv_b_e", "ln_g_e", "ln_b_e", "w_pool_e",
           "pool_scale_e", "w_out_e", "mix_norm_o", "w_in_o", "conv_w_o", "w_out_o", "ffn_norm",
           "w_gate", "w_up", "w_down", "final_norm"]
BIG = ["w_in_e", "w_out_e", "w_in_o", "w_out_o", "w_gate", "w_up", "w_down"]
SHARDED_SMALL = {"conv_w_e": 1, "w_pool_e": 1, "mix_norm_o": 0, "conv_w_o": 1}
SMALL = [n for n in WEIGHTS if n not in BIG]


def kernel(x, mix_norm_e, w_in_e, conv_w_e, conv_b_e, ln_g_e, ln_b_e, w_pool_e, pool_scale_e, w_out_e, mix_norm_o, w_in_o, conv_w_o, w_out_o, ffn_norm, w_gate, w_up, w_down, final_norm, loss_target, m_mix_norm_e, m_w_in_e, m_conv_w_e, m_conv_b_e, m_ln_g_e, m_ln_b_e, m_w_pool_e, m_pool_scale_e, m_w_out_e, m_mix_norm_o, m_w_in_o, m_conv_w_o, m_w_out_o, m_ffn_norm, m_w_gate, m_w_up, m_w_down, m_final_norm, v_mix_norm_e, v_w_in_e, v_conv_w_e, v_conv_b_e, v_ln_g_e, v_ln_b_e, v_w_pool_e, v_pool_scale_e, v_w_out_e, v_mix_norm_o, v_w_in_o, v_conv_w_o, v_w_out_o, v_ffn_norm, v_w_gate, v_w_up, v_w_down, v_final_norm):
    wts = dict(zip(WEIGHTS, [mix_norm_e, w_in_e, conv_w_e, conv_b_e, ln_g_e, ln_b_e, w_pool_e, pool_scale_e, w_out_e, mix_norm_o, w_in_o, conv_w_o, w_out_o, ffn_norm, w_gate, w_up, w_down, final_norm]))
    mom = dict(zip(WEIGHTS, [m_mix_norm_e, m_w_in_e, m_conv_w_e, m_conv_b_e, m_ln_g_e, m_ln_b_e, m_w_pool_e, m_pool_scale_e, m_w_out_e, m_mix_norm_o, m_w_in_o, m_conv_w_o, m_w_out_o, m_ffn_norm, m_w_gate, m_w_up, m_w_down, m_final_norm]))
    var = dict(zip(WEIGHTS, [v_mix_norm_e, v_w_in_e, v_conv_w_e, v_conv_b_e, v_ln_g_e, v_ln_b_e, v_w_pool_e, v_pool_scale_e, v_w_out_e, v_mix_norm_o, v_w_in_o, v_conv_w_o, v_w_out_o, v_ffn_norm, v_w_gate, v_w_up, v_w_down, v_final_norm]))
    bsz, seq, d = x.shape
    t = bsz * seq
    me = _index(_place())
    me_arr = jnp.reshape(me, (1,)).astype(jnp.int32)

    sh_names = list(SHARDED_SMALL)
    sh_local = [wts[n][0] for n in sh_names]
    packed = _pack(sh_local)
    params_st = _split_start("small_params_start", [packed, lax.empty((NDEV,) + packed.shape, F32)], NDEV - 1,
                             _everyone_plan, x)

    for state in (wts, mom, var):
        for n in ("w_gate", "w_up"):
            state[n] = jnp.swapaxes(state[n], 1, 2)
    bf = lambda a: a.astype(BF16)
    mix_kinds, ffn_kinds = ["col", "blk"], ["blk", "blk", "blk"]
    ffn_names = ("w_gate", "w_up", "w_down")
    groups = {
        "mix_e": ([w_in_e.shape[2], d], mix_kinds, [("w_in_e", 0), ("w_out_e", 0)]),
        "ffn0": ([d, d, d], ffn_kinds, [(n, 0) for n in ffn_names]),
        "mix_o": ([w_in_o.shape[2], d], mix_kinds, [("w_in_o", 0), ("w_out_o", 0)]),
        "ffn1": ([d, d, d], ffn_kinds, [(n, 1) for n in ffn_names]),
    }
    gathers = {
        "in_e": ([bf(w_in_e[0])], ["col"]), "out_e": ([bf(w_out_e[0])], ["blk"]),
        "gu0": ([bf(wts["w_gate"][0]), bf(wts["w_up"][0])], ["blk", "blk"]), "down0": ([bf(w_down[0])], ["blk"]),
        "in_o": ([bf(w_in_o[0])], ["col"]), "out_o": ([bf(w_out_o[0])], ["blk"]),
        "gu1": ([bf(wts["w_gate"][1]), bf(wts["w_up"][1])], ["blk", "blk"]), "down1": ([bf(w_down[1])], ["blk"]),
    }
    started, prev = {}, params_st[3]
    for grp, (shards, kinds) in gathers.items():
        started[grp] = _gather_start("gather_start_" + grp, shards, kinds, prev)
        prev = started[grp][4]
    all_started = prev[0, 0:1]

    bufs = _split_wait("small_params_wait", params_st, _everyone_plan, prev)
    gathered = _own_copy("small_params_own", bufs[0], bufs[1], "blk", me_arr)
    small = {n: wts[n][0] for n in SMALL if n not in SHARDED_SMALL and n not in ("ffn_norm", "final_norm")}
    small["ffn_norm"], small["final_norm"] = ffn_norm, final_norm
    flat, off = gathered.reshape(NDEV, -1), 0
    for n, a in zip(sh_names, sh_local):
        ax, shp = SHARDED_SMALL[n], a.shape
        blocks = jnp.moveaxis(flat[:, off:off + a.size].reshape((NDEV,) + shp), 0, ax)
        small[n] = blocks.reshape(shp[:ax] + (NDEV * shp[ax],) + shp[ax + 1:])
        off += a.size

    passing, shards_of = {}, {}

    def pass_on(grp, after):
        shards, kinds = gathers[grp]
        shards_of[grp], lands = _gather_wait("gather_wait_" + grp, started[grp], kinds, after)
        plan = _forward_plan(kinds, [s.shape[1] for s in shards])
        passing[grp] = (_split_start("forward_start_" + grp, lands, 3 * len(lands), plan, after), plan)
        return passing[grp][0][3]

    def get_w(grp, after):
        if grp not in passing:
            after = pass_on(grp, after)
        st, plan = passing[grp]
        lands = _split_wait("forward_wait_" + grp, st, plan, after)
        full = [_own_copy("own_copy_%s%d" % (grp, a), shards_of[grp][a], lands[a], gathers[grp][1][a], me_arr)
                for a in range(len(lands))]
        return [f.reshape(-1, d) if kind == "blk" else f for f, kind in zip(full, gathers[grp][1])]

    cx, cy, cc = _place()
    place = jnp.stack([cc, 2 * cx + cy]).astype(jnp.int32)
    bwd_order = ["ffn1", "mix_o", "ffn0", "mix_e"]
    pairing, pending, results = {}, {}, {}

    late_names = ["conv_w_e", "mix_norm_e"]
    early_names = [n for n in SMALL if n not in late_names] + ["loss"]
    small_sent = {}

    def send_small(tag, arrays, after):
        mine = _pack(arrays)
        small_sent[tag] = _split_start(tag + "_start", [mine, lax.empty((NDEV,) + mine.shape, F32)], NDEV - 1,
                                       _everyone_plan, after)
        return small_sent[tag][3]

    def summed_small(tag, shapes, after):
        bufs = _split_wait(tag + "_wait", small_sent[tag], _everyone_plan, after)
        parts = _own_copy(tag + "_own", bufs[0], bufs[1], "blk", me_arr)
        return _unpack(_sum_parts(tag + "_sum", parts), shapes)

    def put_g(grp, grads):
        if grp == "small":
            small_sent["shapes"] = [grads[n].shape for n in early_names]
            return send_small("small_grads", [grads[n] for n in early_names], place)
        nloc, kinds, _ = groups[grp]
        if len(kinds) == 2:
            fulls = [grads["w_in"], grads["w_out"].reshape(NDEV, -1, d)]
        else:
            fulls = [grads[n].reshape(NDEV, -1, d) for n in ffn_names]
        empties = []
        for g, kind, n in zip(fulls, kinds, nloc):
            empties.append(lax.empty((NCHIPS, g.shape[1] if kind == "blk" else g.shape[0], n), g.dtype))
        plan = _pair_plan(kinds, nloc)
        pairing[grp] = (_split_start("pair_start_" + grp, fulls + empties, NCHIPS * len(fulls), plan, place),
                        plan, kinds, nloc)
        token = pairing[grp][0][3]
        return token

    def send_sums(grp, after):
        st, plan, kinds, nloc = pairing[grp]
        bufs = _split_wait("pair_wait_" + grp, st, plan, after)
        na = len(kinds)
        sums = [_chip_sum("chip_sum_%s%d" % (grp, a), bufs[a], kinds[a], nloc[a], bufs[na + a], place)
                for a in range(na)]
        pending[grp] = _scatter_start("scatter_start_" + grp, [s[0] for s in sums], [s[1] for s in sums], after)
        return pending[grp][4]

    def finish(grp, after):
        lands = _scatter_wait("scatter_wait_" + grp, pending[grp], after)
        dep = None
        for (n, l), parts in zip(groups[grp][2], lands):
            results[n] = _sum_adamw("adamw_%s%d" % (n, l), parts, wts[n], mom[n], var[n], l, results.get(n), dep)
            dep = results[n][1]
        return dep

    fwd_sync = {"fwd_a": ["out_e"], "fwd_b": ["gu0"], "fwd_c": ["down0", "in_o"], "fwd_d": ["out_o"],
                "fwd_e": ["gu1"], "fwd_f": ["down1"]}

    def sync(tag, after):
        if tag in fwd_sync:
            for grp in fwd_sync[tag]:
                after = pass_on(grp, after)
            return after
        if tag == "bwd_mix_o":
            return send_sums("ffn1", after)
        if tag == "bwd_ffn0":
            return finish("ffn1", send_sums("mix_o", after))
        if tag == "bwd_mix_e":
            return finish("mix_o", send_sums("ffn0", after))
        if tag == "bwd_last":
            return send_sums("mix_e", after)
        return None

    small["mix_norm_e"] = small["mix_norm_e"] + all_started
    dx, late = _local_step(x.reshape(t, d), loss_target.reshape(t, d), seq, small, get_w, put_g, sync)

    out_g, out_d, out_m, out_v = {}, {}, {}, {}

    dep = finish("ffn0", send_small("last_grads", [late[n] for n in late_names], dx))
    dep = finish("mix_e", dep)
    sums = dict(zip(early_names, summed_small("small_grads", small_sent["shapes"], dep)))
    sums.update(zip(late_names, summed_small("last_grads", [late[n].shape for n in late_names], dep)))
    loss = jnp.sum(sums["loss"])
    gs_sum = [sums[n] for n in SMALL]
    local_g = []
    for n, g in zip(SMALL, gs_sum):
        if n in SHARDED_SMALL:
            ax = SHARDED_SMALL[n]
            size = wts[n].shape[ax + 1]
            g = lax.dynamic_slice_in_dim(g, me * size, size, axis=ax)
        local_g.append(g.reshape(wts[n].shape))
    shapes = [wts[n].shape for n in SMALL]
    upd = _adamw("adamw_small", _pack([wts[n] for n in SMALL]), _pack(local_g),
                 _pack([mom[n] for n in SMALL]), _pack([var[n] for n in SMALL]))
    for i, outd in enumerate((out_d, out_m, out_v)):
        for n, a in zip(SMALL, _unpack(upd[i], shapes)):
            outd[n] = a
    for n, g in zip(SMALL, local_g):
        out_g[n] = g

    for n in BIG:
        res = [jnp.swapaxes(a, 1, 2) for a in results[n]] if n in ("w_gate", "w_up") else results[n]
        out_g[n], out_d[n], out_m[n], out_v[n] = res

    return (loss, dx.reshape(bsz, seq, d), *[out_g[n] for n in WEIGHTS], *[out_d[n] for n in WEIGHTS],
            *[out_m[n] for n in WEIGHTS], *[out_v[n] for n in WEIGHTS])
```

```python
import jax
import jax.numpy as jnp
from jax import lax
from jax.experimental import pallas as pl
from jax.experimental.pallas import tpu as pltpu

F32 = jnp.float32
BF16 = jnp.bfloat16
NDEV = 8
MESH_AXES = ("x", "y", "c")
EPS = 1e-6
POOL_WINDOWS = (2, 4, 8, 16)
ADAM_LR = 0.001
ADAM_B1 = 0.9
ADAM_B2 = 0.999
ADAM_EPS = 1e-08
ADAM_WD = 0.01
ADAM_STEP = 10
LANES = 128
SUBLANES = 8
VMEM_LIMIT = 56 * 1024 * 1024
MXU_DEPTH = 256
MM_TK = 2816
MESH = pl.DeviceIdType.MESH
ANY = pl.BlockSpec(memory_space=pl.ANY)


def _cp(*sem):
    return pltpu.CompilerParams(dimension_semantics=sem, vmem_limit_bytes=VMEM_LIMIT)


def _tile(n, pref, unit=LANES):
    if n <= pref:
        return n
    t = (pref // unit) * unit
    while t > unit and n % t:
        t -= unit
    assert n % t == 0, (n, pref)
    return t


def _sigmoid(v):
    return 0.5 * jnp.tanh(0.5 * v) + 0.5


def _mm(name, pairs, a_specs, b_specs, dims, out_shape, o_spec, grid, acc_shape,
        res=None, res_spec=None, dep=None):
    np_ = len(pairs)
    nk = grid[2]
    has_res = res is not None
    n_in = 2 * np_ + (1 if has_res else 0) + (0 if dep is None else 1)

    def body(*refs):
        a_refs = refs[:np_]
        b_refs = refs[np_:2 * np_]
        r_ref = refs[2 * np_] if has_res else None
        o_ref = refs[n_in]
        acc = refs[-1]

        def part():
            s = None
            for a_ref, b_ref in zip(a_refs, b_refs):
                d = lax.dot_general(a_ref[...], b_ref[...], dims, preferred_element_type=F32)
                s = d if s is None else s + d
            return s

        def finish(v):
            if has_res:
                v = v + r_ref[...]
            o_ref[...] = v.astype(o_ref.dtype)

        if nk == 1:
            finish(part())
        else:
            k = pl.program_id(2)

            @pl.when(k == 0)
            def _():
                acc[...] = part()

            @pl.when((k > 0) & (k < nk - 1))
            def _():
                acc[...] += part()

            @pl.when(k == nk - 1)
            def _():
                finish(acc[...] + part())

    ins = [p[0] for p in pairs] + [p[1] for p in pairs]
    specs = list(a_specs) + list(b_specs)
    if has_res:
        ins.append(res)
        specs.append(res_spec)
    if dep is not None:
        ins.append(dep)
        specs.append(ANY)
    return pl.pallas_call(
        body, name=name, grid=grid, in_specs=specs, out_specs=o_spec, out_shape=out_shape,
        scratch_shapes=[pltpu.VMEM(acc_shape if nk > 1 else (SUBLANES, LANES), F32)],
        compiler_params=_cp("parallel", "parallel", "arbitrary"))(*ins)


NN = (((1,), (0,)), ((), ()))
NT = (((1,), (1,)), ((), ()))
TN = (((0,), (0,)), ((), ()))


def _tiles_mk(m, kk):
    return _tile(m, 1024), _tile(kk, MM_TK, MXU_DEPTH)


def _mm_nn(name, a, b, out_dtype, res=None, dep=None):
    pairs = list(zip(a, b)) if isinstance(a, (list, tuple)) else [(a, b)]
    m, kk = pairs[0][0].shape
    n = pairs[0][1].shape[1]
    tm, tk = _tiles_mk(m, kk)
    tn = _tile(n, 1024 if tk * len(pairs) <= MM_TK else 512)
    return _mm(name, pairs,
               [pl.BlockSpec((tm, tk), lambda i, j, k: (i, k))] * len(pairs),
               [pl.BlockSpec((tk, tn), lambda i, j, k: (k, j))] * len(pairs), NN,
               jax.ShapeDtypeStruct((m, n), out_dtype),
               pl.BlockSpec((tm, tn), lambda i, j, k: (i, j)),
               (m // tm, n // tn, kk // tk), (tm, tn), res,
               pl.BlockSpec((tm, tn), lambda i, j, k: (i, j)), dep=dep)


def _mm_nt(name, a, b, out_dtype, dep=None):
    m, n = a.shape
    kk = b.shape[0]
    tn = _tile(kk, 1024)
    tm, tk = _tiles_mk(m, n)
    return _mm(name, [(a, b)],
               [pl.BlockSpec((tm, tk), lambda i, j, k: (i, k))],
               [pl.BlockSpec((tn, tk), lambda i, j, k: (j, k))], NT,
               jax.ShapeDtypeStruct((m, kk), out_dtype),
               pl.BlockSpec((tm, tn), lambda i, j, k: (i, j)),
               (m // tm, kk // tn, n // tk), (tm, tn), dep=dep)


def _mm_tn(name, a, b, out_dtype, dep=None):
    t, m = a.shape
    n = b.shape[1]
    tn = _tile(n, 1024)
    tm, tk = _tile(m, 1408), _tile(t, MM_TK, MXU_DEPTH)
    return _mm(name, [(a, b)],
               [pl.BlockSpec((tk, tm), lambda i, j, k: (k, i))],
               [pl.BlockSpec((tk, tn), lambda i, j, k: (k, j))], TN,
               jax.ShapeDtypeStruct((m, n), out_dtype),
               pl.BlockSpec((tm, tn), lambda i, j, k: (i, j)),
               (m // tm, n // tn, t // tk), (tm, tn), dep=dep)


def _ffn_fwd(name, n, wg, wu, dep=None):
    f, d = wg.shape
    t = n.shape[0]
    tm, tn = _tile(t, 1024), _tile(f, 512)

    def body(n_ref, wg_ref, wu_ref, *rest):
        act_ref, ds_ref, s_ref = rest[-3:]
        nv = n_ref[...]
        g = lax.dot_general(nv, wg_ref[...], NT, preferred_element_type=F32)
        up = lax.dot_general(nv, wu_ref[...], NT, preferred_element_type=F32)
        sg = _sigmoid(g)
        silu = g * sg
        act_ref[...] = (silu * up).astype(BF16)
        ds_ref[...] = (up * (sg * (1.0 + g * (1.0 - sg)))).astype(BF16)
        s_ref[...] = silu.astype(BF16)

    w_spec = pl.BlockSpec((tn, d), lambda j, i: (j, 0))
    o_spec = pl.BlockSpec((tm, tn), lambda j, i: (i, j))
    shp = jax.ShapeDtypeStruct((t, f), BF16)
    return pl.pallas_call(
        body, name=name, grid=(f // tn, t // tm),
        in_specs=[pl.BlockSpec((tm, d), lambda j, i: (i, 0)), w_spec, w_spec] + ([] if dep is None else [ANY]),
        out_specs=[o_spec, o_spec, o_spec], out_shape=[shp, shp, shp],
        compiler_params=_cp("parallel", "parallel"))(n, wg, wu, *([] if dep is None else [dep]))


def _ffn_bwd_act(name, dh, wd, dsilu, silu, dep=None):
    f, d = wd.shape
    t = dh.shape[0]
    tm, tn = _tile(t, 1024), _tile(f, 512)

    def body(dh_ref, wd_ref, ds_ref, s_ref, *rest):
        dg_ref, dup_ref = rest[-2:]
        da = lax.dot_general(dh_ref[...], wd_ref[...], NT, preferred_element_type=F32)
        dg_ref[...] = (da * ds_ref[...].astype(F32)).astype(BF16)
        dup_ref[...] = (da * s_ref[...].astype(F32)).astype(BF16)

    o_spec = pl.BlockSpec((tm, tn), lambda i, j: (i, j))
    shp = jax.ShapeDtypeStruct((t, f), BF16)
    return pl.pallas_call(
        body, name=name, grid=(t // tm, f // tn),
        in_specs=[pl.BlockSpec((tm, d), lambda i, j: (i, 0)),
                  pl.BlockSpec((tn, d), lambda i, j: (j, 0)), o_spec, o_spec]
        + ([] if dep is None else [ANY]),
        out_specs=[o_spec, o_spec], out_shape=[shp, shp],
        compiler_params=_cp("parallel", "parallel"))(dh, wd, dsilu, silu, *([] if dep is None else [dep]))


def _rms_fwd(name, h, gain):
    t, d = h.shape
    tr = _tile(t, 512, SUBLANES)

    def body(h_ref, g_ref, n_ref):
        hv = h_ref[...]
        r = lax.rsqrt(jnp.mean(hv * hv, axis=-1, keepdims=True) + EPS)
        n_ref[...] = (hv * r * g_ref[...]).astype(BF16)

    return pl.pallas_call(
        body, name=name, grid=(t // tr,),
        in_specs=[pl.BlockSpec((tr, d), lambda i: (i, 0)), pl.BlockSpec((1, d), lambda i: (0, 0))],
        out_specs=pl.BlockSpec((tr, d), lambda i: (i, 0)),
        out_shape=jax.ShapeDtypeStruct((t, d), BF16),
        compiler_params=_cp("parallel"))(h, gain)


def _rms_bwd_math(hv, gain, dn):
    d = hv.shape[-1]
    r = lax.rsqrt(jnp.mean(hv * hv, axis=-1, keepdims=True) + EPS)
    xhat = hv * r
    dxh = dn * gain
    dh = r * (dxh - xhat * (jnp.sum(dxh * xhat, axis=-1, keepdims=True) / d))
    dgain = jnp.sum(dn * xhat, axis=0, keepdims=True)
    return dh, dgain


def _rms_bwd(name, h, gain, dn, dres, bf16_too=True):
    t, d = h.shape
    tr = _tile(t, 256, SUBLANES)

    def body(h_ref, g_ref, dn_ref, dr_ref, dh_ref, *rest):
        dg_ref = rest[-1]
        dh, dgain = _rms_bwd_math(h_ref[...], g_ref[...], dn_ref[...].astype(F32))
        dh = dh + dr_ref[...]
        dh_ref[...] = dh
        if bf16_too:
            rest[0][...] = dh.astype(BF16)

        @pl.when(pl.program_id(0) == 0)
        def _():
            dg_ref[...] = dgain

        @pl.when(pl.program_id(0) > 0)
        def _():
            dg_ref[...] += dgain

    row = pl.BlockSpec((tr, d), lambda i: (i, 0))
    vec = pl.BlockSpec((1, d), lambda i: (0, 0))
    halves = [jax.ShapeDtypeStruct((t, d), BF16)] if bf16_too else []
    out = pl.pallas_call(
        body, name=name, grid=(t // tr,), in_specs=[row, vec, row, row],
        out_specs=[row] + [row] * len(halves) + [vec],
        out_shape=[jax.ShapeDtypeStruct((t, d), F32)] + halves + [jax.ShapeDtypeStruct((1, d), F32)],
        compiler_params=_cp("arbitrary"))(h, gain, dn, dres)
    return (out[0], out[1], out[2]) if bf16_too else (out[0], None, out[1])


def _loss_head(name, h, gain, tgt):
    t, d = h.shape
    tr = _tile(t, 256, SUBLANES)

    def body(h_ref, g_ref, t_ref, dh_ref, dhb_ref, dg_ref, ls_ref):
        hv = h_ref[...]
        gv = g_ref[...]
        r = lax.rsqrt(jnp.mean(hv * hv, axis=-1, keepdims=True) + EPS)
        err = hv * r * gv - t_ref[...]
        lsum = 0.5 * jnp.sum(err * err, axis=0, keepdims=True) / d
        dh, dgain = _rms_bwd_math(hv, gv, err / d)
        dh_ref[...] = dh
        dhb_ref[...] = dh.astype(BF16)

        @pl.when(pl.program_id(0) == 0)
        def _():
            dg_ref[...] = dgain
            ls_ref[...] = lsum

        @pl.when(pl.program_id(0) > 0)
        def _():
            dg_ref[...] += dgain
            ls_ref[...] += lsum

    row = pl.BlockSpec((tr, d), lambda i: (i, 0))
    vec = pl.BlockSpec((1, d), lambda i: (0, 0))
    return pl.pallas_call(
        body, name=name, grid=(t // tr,), in_specs=[row, vec, row],
        out_specs=[row, row, vec, vec],
        out_shape=[jax.ShapeDtypeStruct((t, d), F32), jax.ShapeDtypeStruct((t, d), BF16),
                   jax.ShapeDtypeStruct((1, d), F32), jax.ShapeDtypeStruct((1, d), F32)],
        compiler_params=_cp("arbitrary"))(h, gain, tgt)


def _conv_geom(t, seq, c, k, full_width=False):
    halo = 32 if k - 1 > SUBLANES else SUBLANES
    assert k - 1 <= halo
    tm = min(256 if halo > SUBLANES else 1024, seq // 2)
    tc = c if full_width else min(512, c)
    assert seq % tm == 0 and tm % halo == 0 and c % tc == 0 and t % seq == 0
    return halo, tm, tc, min(64 if halo > SUBLANES else 128, tm), min(LANES, tc)


def _pre(kind, a, b):
    if kind == "glu":
        return a * _sigmoid(b)
    if kind == "mul":
        return a * b
    return a


def _taps(k):
    return sorted((s % SUBLANES, s // SUBLANES, s) for s in range(k))


def _conv_fwd(name, seq, c, w, x1, c1, x2=None, c2=0, pre=None, bias=None, post=None, cpost=0, live=None):
    t = x1.shape[0]
    k = w.shape[0]
    halo, tm, tc, sr, sl = _conv_geom(t, seq, c, k, live is not None)
    nb, cps = tm // halo, seq // tm
    two = x2 is not None
    has_bias, has_post = bias is not None, post is not None

    def body(*refs):
        it = iter(refs)
        x1c, x1h = next(it), next(it)
        x2c, x2h = (next(it), next(it)) if two else (None, None)
        w_ref = next(it)
        b_ref = next(it) if has_bias else None
        p_ref = next(it) if has_post else None
        o_ref = next(it)
        y_ref = next(it) if has_post else None
        xs = next(it)
        first = (pl.program_id(1) % cps) == 0
        hv = _pre(pre, x1h[...].astype(F32), x2h[...].astype(F32) if two else None)
        xs[0:halo, :] = jnp.where(first, 0.0, hv)
        xs[halo:halo + tm, :] = _pre(pre, x1c[...].astype(F32), x2c[...].astype(F32) if two else None)
        for l0 in range(0, tc, sl):
            ls = slice(l0, l0 + sl)
            for r0 in range(0, tm, sr):
                win = xs[r0:r0 + sr + halo, ls]
                acc = jnp.zeros((sr, sl), F32)
                rolled = {}
                for r, q, s in _taps(k if live is None else live[l0 // sl]):
                    if r not in rolled:
                        rolled[r] = win if r == 0 else pltpu.roll(win, r, 0)
                    lo = halo - SUBLANES * q
                    acc = acc + w_ref[k - 1 - s:k - s, ls] * rolled[r][lo:lo + sr]
                if has_bias:
                    acc = acc + b_ref[:, ls]
                o_ref[r0:r0 + sr, ls] = acc.astype(o_ref.dtype)
                if has_post:
                    y_ref[r0:r0 + sr, ls] = (acc * p_ref[r0:r0 + sr, ls].astype(F32)).astype(y_ref.dtype)

    def cur(off):
        return pl.BlockSpec((tm, tc), lambda j, i: (i, off // tc + j))

    def prev(off):
        return pl.BlockSpec((halo, tc), lambda j, i: (jnp.maximum(i * nb - 1, 0), off // tc + j))

    ins, specs = [x1, x1], [cur(c1), prev(c1)]
    if two:
        ins += [x2, x2]
        specs += [cur(c2), prev(c2)]
    ins.append(w)
    specs.append(pl.BlockSpec((k, tc), lambda j, i: (0, j)))
    if has_bias:
        ins.append(bias)
        specs.append(pl.BlockSpec((1, tc), lambda j, i: (0, j)))
    if has_post:
        ins.append(post)
        specs.append(cur(cpost))
    o_spec = pl.BlockSpec((tm, tc), lambda j, i: (i, j))
    shp = jax.ShapeDtypeStruct((t, c), BF16)
    return pl.pallas_call(
        body, name=name, grid=(c // tc, t // tm), in_specs=specs,
        out_specs=[o_spec, o_spec] if has_post else o_spec,
        out_shape=[shp, shp] if has_post else shp,
        scratch_shapes=[pltpu.VMEM((halo + tm, tc), F32)],
        compiler_params=_cp("parallel", "parallel"))(*ins)


def _conv_bwd(name, seq, c, w, d1, cd1, d2=None, cd2=0, dpre=None,
              x1=None, c1=0, x2=None, c2=0, pre=None, live=None, dep=None):
    t = d1.shape[0]
    k = w.shape[0]
    assert live is None or x1 is None
    halo, tm, tc, sr, sl = _conv_geom(t, seq, c, k, live is not None)
    nb, cps = tm // halo, seq // tm
    nchunks = t // tm
    dtwo, xtwo, has_x = d2 is not None, x2 is not None, x1 is not None

    def body(*refs):
        it = iter(refs)
        d1c, d1n = next(it), next(it)
        d2c, d2n = (next(it), next(it)) if dtwo else (None, None)
        x1c, x1h = (next(it), next(it)) if has_x else (None, None)
        x2c, x2h = (next(it), next(it)) if xtwo else (None, None)
        w_ref = next(it)
        if dep is not None:
            next(it)
        dx_ref = next(it)
        dw_ref = next(it) if has_x else None
        ds = next(it)
        xs = next(it) if has_x else None
        i = pl.program_id(1)
        last = (i % cps) == cps - 1
        ds[0:tm, :] = _pre(dpre, d1c[...].astype(F32), d2c[...].astype(F32) if dtwo else None)
        nv = _pre(dpre, d1n[...].astype(F32), d2n[...].astype(F32) if dtwo else None)
        ds[tm:tm + halo, :] = jnp.where(last, 0.0, nv)
        if has_x:
            first = (i % cps) == 0
            hv = _pre(pre, x1h[...].astype(F32), x2h[...].astype(F32) if xtwo else None)
            xs[0:halo, :] = jnp.where(first, 0.0, hv)
            xs[halo:halo + tm, :] = _pre(pre, x1c[...].astype(F32), x2c[...].astype(F32) if xtwo else None)

            @pl.when(i == 0)
            def _():
                dw_ref[...] = jnp.zeros_like(dw_ref)

        for l0 in range(0, tc, sl):
            ls = slice(l0, l0 + sl)
            for r0 in range(0, tm, sr):
                win = ds[r0:r0 + sr + halo, ls]
                nrow = sr + halo
                acc = jnp.zeros((sr, sl), F32)
                rolled = {}
                for r, q, s in _taps(k if live is None else live[l0 // sl]):
                    if r not in rolled:
                        rolled[r] = win if r == 0 else pltpu.roll(win, nrow - r, 0)
                    lo = SUBLANES * q
                    acc = acc + w_ref[k - 1 - s:k - s, ls] * rolled[r][lo:lo + sr]
                dx_ref[r0:r0 + sr, ls] = acc.astype(dx_ref.dtype)
                if has_x:
                    dcur = win[0:sr]
                    xwin = xs[r0:r0 + sr + halo, ls]
                    xrolled = {}
                    for r, q, s in _taps(k):
                        if r not in xrolled:
                            xrolled[r] = xwin if r == 0 else pltpu.roll(xwin, r, 0)
                        lo = halo - SUBLANES * q
                        part = jnp.sum(dcur * xrolled[r][lo:lo + sr], axis=0, keepdims=True)
                        dw_ref[k - 1 - s:k - s, ls] += part

    def cur(off):
        return pl.BlockSpec((tm, tc), lambda j, i: (i, off // tc + j))

    def prev(off):
        return pl.BlockSpec((halo, tc), lambda j, i: (jnp.maximum(i * nb - 1, 0), off // tc + j))

    def nxt(off):
        return pl.BlockSpec((halo, tc),
                            lambda j, i: (jnp.minimum((i + 1) * nb, nchunks * nb - 1), off // tc + j))

    ins, specs = [d1, d1], [cur(cd1), nxt(cd1)]
    if dtwo:
        ins += [d2, d2]
        specs += [cur(cd2), nxt(cd2)]
    if has_x:
        ins += [x1, x1]
        specs += [cur(c1), prev(c1)]
    if xtwo:
        ins += [x2, x2]
        specs += [cur(c2), prev(c2)]
    ins.append(w)
    specs.append(pl.BlockSpec((k, tc), lambda j, i: (0, j)))
    if dep is not None:
        ins.append(dep)
        specs.append(ANY)
    o_specs = [pl.BlockSpec((tm, tc), lambda j, i: (i, j))]
    o_shapes = [jax.ShapeDtypeStruct((t, c), BF16)]
    scratch = [pltpu.VMEM((tm + halo, tc), F32)]
    if has_x:
        o_specs.append(pl.BlockSpec((k, tc), lambda j, i: (0, j)))
        o_shapes.append(jax.ShapeDtypeStruct((k, c), F32))
        scratch.append(pltpu.VMEM((halo + tm, tc), F32))
    out = pl.pallas_call(
        body, name=name, grid=(c // tc, t // tm), in_specs=specs, out_specs=o_specs,
        out_shape=o_shapes, scratch_shapes=scratch,
        compiler_params=_cp("parallel", "arbitrary"))(*ins)
    return out if has_x else out[0]


def _pool_taps(c):
    kmax = max(POOL_WINDOWS)
    grp = c // len(POOL_WINDOWS)
    cols = []
    for wdw in POOL_WINDOWS:
        col = jnp.concatenate([jnp.zeros((kmax - wdw,), F32), jnp.ones((wdw,), F32)])
        cols.append(jnp.tile(col[:, None], (1, grp)))
    return jnp.concatenate(cols, axis=1)


def _pool_live(c):
    grp, sl = c // len(POOL_WINDOWS), min(LANES, c)
    return tuple(max(POOL_WINDOWS[g] for g in range(l0 // grp, (l0 + sl - 1) // grp + 1))
                 for l0 in range(0, c, sl))


def _counts(i, tr, seq, grp):
    pos = (i * tr + lax.broadcasted_iota(jnp.int32, (tr, 1), 0)) % seq + 1
    return [1.0 / jnp.minimum(pos, wdw).astype(F32) for wdw in POOL_WINDOWS]


def _ln_stats(a2):
    mu = jnp.mean(a2, axis=-1, keepdims=True)
    xc = a2 - mu
    rstd = lax.rsqrt(jnp.mean(xc * xc, axis=-1, keepdims=True) + EPS)
    return xc * rstd, rstd


def _even_fwd(name, seq, a2, ws, u, ln_g, ln_b, w_pool, scale):
    t, c = a2.shape
    ng = len(POOL_WINDOWS)
    grp = c // ng
    tr = _tile(t, 256, SUBLANES)

    def body(a_ref, ws_ref, b_ref, g_ref, bb_ref, wp_ref, sc_ref, z_ref, pm_ref):
        xhat, _ = _ln_stats(a_ref[...].astype(F32))
        l = xhat * g_ref[...] + bb_ref[...]
        z_ref[:, 0:c] = (l * _sigmoid(l)).astype(BF16)
        inv = _counts(pl.program_id(0), tr, seq, grp)
        for g in range(ng):
            gs = slice(g * grp, (g + 1) * grp)
            pm = (ws_ref[:, gs].astype(F32) * inv[g] - b_ref[:, gs].astype(F32)).astype(BF16)
            pm_ref[:, gs] = pm
            q = jnp.dot(pm, wp_ref[g], preferred_element_type=F32)
            z_ref[:, c + g * grp:c + (g + 1) * grp] = (q * sc_ref[:, gs]).astype(BF16)

    row = pl.BlockSpec((tr, c), lambda i: (i, 0))
    vec = pl.BlockSpec((1, c), lambda i: (0, 0))
    return pl.pallas_call(
        body, name=name, grid=(t // tr,),
        in_specs=[row, row, pl.BlockSpec((tr, c), lambda i: (i, 2)), vec, vec,
                  pl.BlockSpec((ng, grp, grp), lambda i: (0, 0, 0)), vec],
        out_specs=[pl.BlockSpec((tr, 2 * c), lambda i: (i, 0)), row],
        out_shape=[jax.ShapeDtypeStruct((t, 2 * c), BF16), jax.ShapeDtypeStruct((t, c), BF16)],
        compiler_params=_cp("parallel"))(a2, ws, u, ln_g, ln_b, w_pool, scale)


def _even_bwd(name, seq, dz, a2, pm, ln_g, ln_b, w_pool, scale, dep=None):
    t, c = a2.shape
    ng = len(POOL_WINDOWS)
    grp = c // ng
    tr = _tile(t, 256, SUBLANES)

    def body(dz_ref, a_ref, pm_ref, g_ref, bb_ref, wp_ref, sc_ref, *rest):
        da_ref, dws_ref, dpm_ref, vec_ref, dwp_ref = rest[-5:]
        i = pl.program_id(0)

        @pl.when(i == 0)
        def _():
            vec_ref[...] = jnp.zeros_like(vec_ref)
            dwp_ref[...] = jnp.zeros_like(dwp_ref)

        xhat, rstd = _ln_stats(a_ref[...].astype(F32))
        gv = g_ref[...]
        l = xhat * gv + bb_ref[...]
        sg = _sigmoid(l)
        dl = dz_ref[:, 0:c].astype(F32) * (sg * (1.0 + l * (1.0 - sg)))
        dxh = dl * gv
        da2 = rstd * (dxh - jnp.mean(dxh, axis=-1, keepdims=True)
                      - xhat * jnp.mean(dxh * xhat, axis=-1, keepdims=True))
        da_ref[...] = da2.astype(BF16)
        vec_ref[0:1, :] += jnp.sum(dl * xhat, axis=0, keepdims=True)
        vec_ref[1:2, :] += jnp.sum(dl, axis=0, keepdims=True)
        vec_ref[2:3, :] += jnp.sum(da2, axis=0, keepdims=True)
        inv = _counts(i, tr, seq, grp)
        for g in range(ng):
            gs = slice(g * grp, (g + 1) * grp)
            pmv = pm_ref[:, gs]
            wp = wp_ref[g]
            dp = dz_ref[:, c + g * grp:c + (g + 1) * grp].astype(F32)
            q = jnp.dot(pmv, wp, preferred_element_type=F32)
            vec_ref[3:4, gs] += jnp.sum(dp * q, axis=0, keepdims=True)
            dq = (dp * sc_ref[:, gs]).astype(BF16)
            dpm = lax.dot_general(dq, wp, NT, preferred_element_type=F32)
            dwp_ref[g] += lax.dot_general(pmv, dq, TN, preferred_element_type=F32)
            dpm_ref[:, gs] = dpm.astype(BF16)
            dws_ref[:, gs] = (dpm * inv[g]).astype(BF16)

    row = pl.BlockSpec((tr, c), lambda i: (i, 0))
    vec = pl.BlockSpec((1, c), lambda i: (0, 0))
    rshape = jax.ShapeDtypeStruct((t, c), BF16)
    return pl.pallas_call(
        body, name=name, grid=(t // tr,),
        in_specs=[pl.BlockSpec((tr, 2 * c), lambda i: (i, 0)), row, row, vec, vec,
                  pl.BlockSpec((ng, grp, grp), lambda i: (0, 0, 0)), vec] + ([] if dep is None else [ANY]),
        out_specs=[row, row, row, pl.BlockSpec((SUBLANES, c), lambda i: (0, 0)),
                   pl.BlockSpec((ng, grp, grp), lambda i: (0, 0, 0))],
        out_shape=[rshape, rshape, rshape, jax.ShapeDtypeStruct((SUBLANES, c), F32),
                   jax.ShapeDtypeStruct((ng, grp, grp), F32)],
        compiler_params=_cp("arbitrary"))(dz, a2, pm, ln_g, ln_b, w_pool, scale,
                                           *([] if dep is None else [dep]))


def _even_du(name, u, da1, dbp, dpm):
    t, c = da1.shape
    tr = _tile(t, 256, SUBLANES)

    def body(u_ref, da_ref, dbp_ref, dpm_ref, du_ref):
        val = u_ref[:, 0:c].astype(F32)
        sg = _sigmoid(u_ref[:, c:2 * c].astype(F32))
        da = da_ref[...].astype(F32)
        du_ref[:, 0:c] = (da * sg).astype(BF16)
        du_ref[:, c:2 * c] = (da * val * sg * (1.0 - sg)).astype(BF16)
        du_ref[:, 2 * c:3 * c] = (dbp_ref[...].astype(F32) - dpm_ref[...].astype(F32)).astype(BF16)

    row = pl.BlockSpec((tr, c), lambda i: (i, 0))
    wide = pl.BlockSpec((tr, 3 * c), lambda i: (i, 0))
    return pl.pallas_call(
        body, name=name, grid=(t // tr,), in_specs=[wide, row, row, row], out_specs=wide,
        out_shape=jax.ShapeDtypeStruct((t, 3 * c), BF16),
        compiler_params=_cp("parallel"))(u, da1, dbp, dpm)


def _odd_du(name, u, dy, co, dxc):
    t, c = dy.shape
    tr = _tile(t, 256, SUBLANES)

    def body(u_ref, dy_ref, co_ref, dx_ref, du_ref):
        dx = dx_ref[...].astype(F32)
        du_ref[:, 0:c] = (dy_ref[...].astype(F32) * co_ref[...].astype(F32)).astype(BF16)
        du_ref[:, c:2 * c] = (dx * u_ref[:, 2 * c:3 * c].astype(F32)).astype(BF16)
        du_ref[:, 2 * c:3 * c] = (dx * u_ref[:, c:2 * c].astype(F32)).astype(BF16)

    row = pl.BlockSpec((tr, c), lambda i: (i, 0))
    wide = pl.BlockSpec((tr, 3 * c), lambda i: (i, 0))
    return pl.pallas_call(
        body, name=name, grid=(t // tr,), in_specs=[wide, row, row, row], out_specs=wide,
        out_shape=jax.ShapeDtypeStruct((t, 3 * c), BF16),
        compiler_params=_cp("parallel"))(u, dy, co, dxc)


def _local_step(x, tgt, seq, small, get_w, put_g, sync):
    t, d = x.shape
    c = d // 2
    cw_e, cw_o = small["conv_w_e"], small["conv_w_o"]
    wp = small["w_pool_e"].astype(BF16)
    ptaps = _pool_taps(c)
    row = lambda v: v.reshape(1, -1)

    we = {"w_in": get_w("in_e", x)[0]}
    n0 = _rms_fwd("rms_fwd_mix0", x, row(small["mix_norm_e"]))
    u0 = _mm_nn("mm_in_e", n0, we["w_in"], BF16)
    sync("fwd_a", u0)
    a2 = _conv_fwd("conv_e_fwd", seq, c, cw_e, u0, 0, u0, c, "glu", bias=row(small["conv_b_e"]))
    ws = _conv_fwd("pool_fwd", seq, c, ptaps, u0, 2 * c, live=_pool_live(c))
    z0, pm = _even_fwd("even_fwd", seq, a2, ws, u0, row(small["ln_g_e"]), row(small["ln_b_e"]),
                       wp, row(small["pool_scale_e"]))
    we["w_out"] = get_w("out_e", z0)[0]
    h1 = _mm_nn("mm_out_e", z0, we["w_out"], F32, res=x)
    sync("fwd_b", h1)
    n1 = _rms_fwd("rms_fwd_ffn0", h1, row(small["ffn_norm"][0]))
    wf0 = dict(zip(("w_gate", "w_up"), get_w("gu0", n1)))
    act0, ds0, s0 = _ffn_fwd("ffn0_fwd", n1, wf0["w_gate"], wf0["w_up"])
    dep = sync("fwd_c", act0)
    wf0["w_down"] = get_w("down0", act0)[0]
    h2 = _mm_nn("mm_down0", act0, wf0["w_down"], F32, res=h1, dep=dep)
    dep = sync("fwd_d", h2)
    n2 = _rms_fwd("rms_fwd_mix1", h2, row(small["mix_norm_o"]))
    wo = {"w_in": get_w("in_o", n2)[0]}
    u1 = _mm_nn("mm_in_o", n2, wo["w_in"], BF16, dep=dep)
    co, y1 = _conv_fwd("conv_o_fwd", seq, d, cw_o, u1, d, u1, 2 * d, "mul", post=u1, cpost=0)
    dep = sync("fwd_e", y1)
    wo["w_out"] = get_w("out_o", y1)[0]
    h3 = _mm_nn("mm_out_o", y1, wo["w_out"], F32, res=h2, dep=dep)
    dep = sync("fwd_f", h3)
    n3 = _rms_fwd("rms_fwd_ffn1", h3, row(small["ffn_norm"][1]))
    wf1 = dict(zip(("w_gate", "w_up"), get_w("gu1", n3)))
    act1, ds1, s1 = _ffn_fwd("ffn1_fwd", n3, wf1["w_gate"], wf1["w_up"], dep=dep)
    wf1["w_down"] = get_w("down1", act1)[0]
    h4 = _mm_nn("mm_down1", act1, wf1["w_down"], F32, res=h3)

    dh4, dh4b, d_final, lsum = _loss_head("loss_head", h4, row(small["final_norm"]), tgt)

    def ffn_bwd(tag, dh, dhb, h_in, gain, n, dsilu, silu, act, w, dep):
        dg, dup = _ffn_bwd_act("ffn%s_bwd_act" % tag, dhb, w["w_down"], dsilu, silu, dep=dep)
        dwd = _mm_tn("mm_dwd%s" % tag, act, dhb, BF16)
        dwg = _mm_tn("mm_dwg%s" % tag, dg, n, BF16, dep=sync("bwd_ffn" + tag, dwd))
        dwu = _mm_tn("mm_dwu%s" % tag, dup, n, BF16)
        dn = _mm_nn("mm_ffn_dn%s" % tag, [dg, dup], [w["w_gate"], w["w_up"]], BF16)
        dh_in, dhb_in, dgain = _rms_bwd("rms_bwd_ffn%s" % tag, h_in, gain, dn, dh)
        dep = put_g("ffn" + tag, {"w_gate": dwg, "w_up": dwu, "w_down": dwd})
        return dh_in, dhb_in, dgain, dep

    dh3, dh3b, d_ffn1, dep = ffn_bwd("1", dh4, dh4b, h3, row(small["ffn_norm"][1]), n3, ds1, s1,
                                     act1, wf1, None)
    dw_out_o = _mm_tn("mm_dw_out_o", y1, dh3b, BF16, dep=dep)
    dy1 = _mm_nt("mm_dy_o", dh3b, wo["w_out"], BF16)
    dxc, dcw_o = _conv_bwd("conv_o_bwd", seq, d, cw_o, dy1, 0, u1, 0, "mul",
                           x1=u1, c1=d, x2=u1, c2=2 * d, pre="mul", dep=sync("bwd_mix_o", dy1))
    du1 = _odd_du("odd_du", u1, dy1, co, dxc)
    dw_in_o = _mm_tn("mm_dw_in_o", n2, du1, BF16)
    dn2 = _mm_nt("mm_dn_o", du1, wo["w_in"], BF16)
    dh2, dh2b, d_mix_o = _rms_bwd("rms_bwd_mix1", h2, row(small["mix_norm_o"]), dn2, dh3)
    dep = put_g("mix_o", {"w_in": dw_in_o, "w_out": dw_out_o})

    dh1, dh1b, d_ffn0, dep = ffn_bwd("0", dh2, dh2b, h1, row(small["ffn_norm"][0]), n1, ds0, s0,
                                     act0, wf0, dep)
    dw_out_e = _mm_tn("mm_dw_out_e", z0, dh1b, BF16, dep=dep)
    dz0 = _mm_nt("mm_dz_e", dh1b, we["w_out"], BF16)
    da2, dws, dpm, vecs, dwp = _even_bwd("even_bwd", seq, dz0, a2, pm, row(small["ln_g_e"]),
                                         row(small["ln_b_e"]), wp, row(small["pool_scale_e"]),
                                         dep=sync("bwd_mix_e", dz0))
    dep = put_g("small", {"conv_b_e": vecs[2], "ln_g_e": vecs[0], "ln_b_e": vecs[1],
                          "w_pool_e": dwp, "pool_scale_e": vecs[3], "mix_norm_o": d_mix_o[0],
                          "conv_w_o": dcw_o, "ffn_norm": jnp.concatenate([d_ffn0, d_ffn1], axis=0),
                          "final_norm": d_final[0], "loss": lsum})
    da1, dcw_e = _conv_bwd("conv_e_bwd", seq, c, cw_e, da2, 0, x1=u0, c1=0, x2=u0, c2=c, pre="glu", dep=dep)
    dbp = _conv_bwd("pool_bwd", seq, c, ptaps, dws, 0, live=_pool_live(c))
    du0 = _even_du("even_du", u0, da1, dbp, dpm)
    dw_in_e = _mm_tn("mm_dw_in_e", n0, du0, BF16)
    dep = put_g("mix_e", {"w_in": dw_in_e, "w_out": dw_out_e})
    dn0 = _mm_nt("mm_dn_e", du0, we["w_in"], BF16, dep=dep)
    dx, _, d_mix_e = _rms_bwd("rms_bwd_mix0", x, row(small["mix_norm_e"]), dn0, dh1, bf16_too=False)
    return dx, {"conv_w_e": dcw_e, "mix_norm_e": d_mix_e[0]}


def _place():
    x, y, c = (lax.axis_index(a) for a in MESH_AXES)
    return x, y, c


def _index(p):
    return 4 * p[0] + 2 * p[1] + p[2]


def _slab(ref, kind, d, n):
    if kind == "blk":
        return ref.at[d]
    return ref.at[:, pl.ds(pl.multiple_of(d * n, LANES), n)]


HBM = pl.BlockSpec(memory_space=pltpu.HBM)
SEM = pl.BlockSpec(memory_space=pltpu.SEMAPHORE)
EFFECT = pltpu.SideEffectType.DATAFLOW_SIDE_EFFECTING
NCHIPS = 4


def _in_hbm(a):
    return pltpu.with_memory_space_constraint(a, pltpu.HBM)


def _gathered_shape(s, kind):
    m, n = s.shape
    return (NDEV, m, n) if kind == "blk" else (m, NDEV * n)


def _first_targets():
    x, y, c = _place()
    return [(x, y, 1 - c), (1 - x, y, c), (x, 1 - y, c), (1 - x, 1 - y, c)]


def _gather_start(name, shards, kinds, after):
    na = len(shards)

    def body(*refs):
        x_refs, land_refs = refs[:na], refs[na:2 * na]
        send_sems, recv_sems = refs[2 * na + 1], refs[2 * na + 2]
        token = refs[-1]
        me = _index(_place())
        for a in range(na):
            for k, to in enumerate(_first_targets()):
                pltpu.make_async_remote_copy(
                    src_ref=x_refs[a], dst_ref=_slab(land_refs[a], kinds[a], me, shards[a].shape[1]),
                    send_sem=send_sems.at[4 * a + k], recv_sem=recv_sems.at[4 * a + k],
                    device_id=to, device_id_type=MESH).start()
        token[...] = jnp.zeros_like(token)

    lands = [lax.empty(_gathered_shape(s, k), s.dtype) for s, k in zip(shards, kinds)]
    outs = pl.pallas_call(
        body, name=name,
        out_shape=(pltpu.SemaphoreType.DMA((4 * na,)), pltpu.SemaphoreType.DMA((4 * na,)),
                   *[pltpu.HBM(s.shape, s.dtype) for s in shards],
                   *[pltpu.HBM(l.shape, l.dtype) for l in lands],
                   jax.ShapeDtypeStruct((SUBLANES, LANES), F32)),
        in_specs=[HBM] * (2 * na) + [ANY],
        out_specs=(SEM, SEM, *[HBM] * (2 * na), pl.BlockSpec(memory_space=pltpu.VMEM)),
        input_output_aliases={i: 2 + i for i in range(2 * na)},
        compiler_params=pltpu.CompilerParams(has_side_effects=EFFECT),
    )(*[_in_hbm(s) for s in shards], *[_in_hbm(l) for l in lands], after)
    return outs[0], outs[1], outs[2:2 + na], outs[2 + na:2 + 2 * na], outs[-1]


def _gather_wait(name, started, kinds, after):
    send_sems, recv_sems, shards, lands, _ = started
    na = len(shards)

    def body(*refs):
        x_refs, land_refs = refs[:na], refs[na:2 * na]
        s_sems, r_sems = refs[2 * na], refs[2 * na + 1]
        for a in range(na):
            for k, frm in enumerate(_first_targets()):
                cp = pltpu.make_async_remote_copy(
                    src_ref=x_refs[a],
                    dst_ref=_slab(land_refs[a], kinds[a], _index(frm), shards[a].shape[1]),
                    send_sem=s_sems.at[4 * a + k], recv_sem=r_sems.at[4 * a + k],
                    device_id=frm, device_id_type=MESH)
                cp.wait_send()
                cp.wait_recv()

    outs = pl.pallas_call(
        body, name=name,
        out_shape=(*[pltpu.HBM(s.shape, s.dtype) for s in shards],
                   *[pltpu.HBM(l.shape, l.dtype) for l in lands]),
        in_specs=[HBM] * (2 * na) + [SEM, SEM, ANY], out_specs=[HBM] * (2 * na),
        input_output_aliases={i: i for i in range(2 * na)},
        compiler_params=pltpu.CompilerParams(has_side_effects=EFFECT),
    )(*shards, *lands, send_sems, recv_sems, after)
    return outs[:na], outs[na:]


def _split_start(name, bufs, ncopies, plan, after):
    nb = len(bufs)

    def body(*refs):
        send_sems, recv_sems, token = refs[nb + 1], refs[nb + 2], refs[-1]
        for k, (src, dst, to, _) in enumerate(plan(refs[:nb])):
            pltpu.make_async_remote_copy(src_ref=src, dst_ref=dst, send_sem=send_sems.at[k],
                                         recv_sem=recv_sems.at[k], device_id=to, device_id_type=MESH).start()
        token[...] = jnp.zeros_like(token)

    outs = pl.pallas_call(
        body, name=name,
        out_shape=(pltpu.SemaphoreType.DMA((ncopies,)), pltpu.SemaphoreType.DMA((ncopies,)),
                   *[pltpu.HBM(b.shape, b.dtype) for b in bufs],
                   jax.ShapeDtypeStruct((SUBLANES, LANES), F32)),
        in_specs=[HBM] * nb + [ANY],
        out_specs=(SEM, SEM, *[HBM] * nb, pl.BlockSpec(memory_space=pltpu.VMEM)),
        input_output_aliases={i: 2 + i for i in range(nb)},
        compiler_params=pltpu.CompilerParams(has_side_effects=EFFECT),
    )(*[_in_hbm(b) for b in bufs], after)
    return outs[0], outs[1], list(outs[2:2 + nb]), outs[-1]


def _split_wait(name, started, plan, after):
    send_sems, recv_sems, bufs, _ = started
    nb = len(bufs)

    def body(*refs):
        s_sems, r_sems = refs[nb], refs[nb + 1]
        for k, (src, _, to, landed) in enumerate(plan(refs[:nb])):
            cp = pltpu.make_async_remote_copy(src_ref=src, dst_ref=landed, send_sem=s_sems.at[k],
                                              recv_sem=r_sems.at[k], device_id=to, device_id_type=MESH)
            cp.wait_send()
            cp.wait_recv()

    outs = pl.pallas_call(
        body, name=name, out_shape=tuple(pltpu.HBM(b.shape, b.dtype) for b in bufs),
        in_specs=[HBM] * nb + [SEM, SEM, ANY], out_specs=[HBM] * nb,
        input_output_aliases={i: i for i in range(nb)},
        compiler_params=pltpu.CompilerParams(has_side_effects=EFFECT),
    )(*bufs, send_sems, recv_sems, after)
    return list(outs)


def _forward_plan(kinds, nloc):
    def plan(lands):
        x, y, c = _place()
        out = []
        for a, land in enumerate(lands):
            for chip in [(1 - x, y), (x, 1 - y), (1 - x, 1 - y)]:
                mine = _slab(land, kinds[a], _index((*chip, c)), nloc[a])
                out.append((mine, mine, (x, y, 1 - c), _slab(land, kinds[a], _index((*chip, 1 - c)), nloc[a])))
        return out
    return plan


def _own_copy(name, shard, land, kind, me):
    m, n = shard.shape
    tr = _tile(m, max(SUBLANES, 1048576 // n), SUBLANES)

    def body(s_ref, x_ref, land_ref, o_ref):
        o_ref[...] = x_ref[...]

    if kind == "blk":
        o_spec = pl.BlockSpec((None, tr, n), lambda i, s: (s[0], i, 0))
    else:
        o_spec = pl.BlockSpec((tr, n), lambda i, s: (i, s[0]))
    return pl.pallas_call(
        body, name=name,
        grid_spec=pltpu.PrefetchScalarGridSpec(
            num_scalar_prefetch=1, grid=(m // tr,),
            in_specs=[pl.BlockSpec((tr, n), lambda i, s: (i, 0)), ANY], out_specs=o_spec),
        out_shape=jax.ShapeDtypeStruct(land.shape, land.dtype),
        input_output_aliases={2: 0}, compiler_params=_cp("parallel"))(me, shard, land)


def _everyone_plan(refs):
    x, y, c = _place()
    out = []
    for dx, dy, dc in [(a, b, e) for a in (0, 1) for b in (0, 1) for e in (0, 1)][1:]:
        peer = (x ^ dx, y ^ dy, c ^ dc)
        out.append((refs[0], refs[1].at[_index((x, y, c))], peer, refs[1].at[_index(peer)]))
    return out


def _pair_plan(kinds, nloc):
    na = len(kinds)

    def plan(refs):
        x, y, c = _place()
        out = []
        for a in range(na):
            for j in range(NCHIPS):
                dst = refs[na + a].at[j]
                out.append((_slab(refs[a], kinds[a], 2 * j + (1 - c), nloc[a]), dst, (x, y, 1 - c), dst))
        return out
    return plan


def _chip_sum(name, full, kind, n, from_sib, place):
    _, m, _ = from_sib.shape
    tr = _tile(m, max(SUBLANES, 1048576 // n), SUBLANES)

    def body(s_ref, mine_ref, sib_ref, csum_ref, land_ref):
        v = (mine_ref[...].astype(F32) + sib_ref[...].astype(F32)).astype(csum_ref.dtype)
        csum_ref[...] = v

        @pl.when(pl.program_id(1) == s_ref[1])
        def _():
            land_ref[...] = v

    if kind == "blk":
        mine_spec = pl.BlockSpec((None, tr, n), lambda i, j, s: (2 * j + s[0], i, 0))
    else:
        mine_spec = pl.BlockSpec((tr, n), lambda i, j, s: (i, 2 * j + s[0]))
    slot = pl.BlockSpec((None, tr, n), lambda i, j, s: (j, i, 0))
    shp = jax.ShapeDtypeStruct((NCHIPS, m, n), from_sib.dtype)
    return pl.pallas_call(
        body, name=name,
        grid_spec=pltpu.PrefetchScalarGridSpec(
            num_scalar_prefetch=1, grid=(m // tr, NCHIPS), in_specs=[mine_spec, slot],
            out_specs=[slot, pl.BlockSpec((None, tr, n), lambda i, j, s: (s[1], i, 0))]),
        out_shape=[shp, shp], compiler_params=_cp("parallel", "arbitrary"))(place, full, from_sib)


def _other_chips():
    x, y, c = _place()
    return [(1 - x, y, c), (x, 1 - y, c), (1 - x, 1 - y, c)]


def _scatter_start(name, csums, lands, after):
    na = len(csums)

    def body(*refs):
        c_refs, land_refs = refs[:na], refs[na:2 * na]
        send_sems, recv_sems = refs[2 * na + 1], refs[2 * na + 2]
        token = refs[-1]
        x, y, _ = _place()
        for a in range(na):
            for k, to in enumerate(_other_chips()):
                pltpu.make_async_remote_copy(
                    src_ref=c_refs[a].at[2 * to[0] + to[1]], dst_ref=land_refs[a].at[2 * x + y],
                    send_sem=send_sems.at[3 * a + k], recv_sem=recv_sems.at[3 * a + k],
                    device_id=to, device_id_type=MESH).start()
        token[...] = jnp.zeros_like(token)

    outs = pl.pallas_call(
        body, name=name,
        out_shape=(pltpu.SemaphoreType.DMA((3 * na,)), pltpu.SemaphoreType.DMA((3 * na,)),
                   *[pltpu.HBM(s.shape, s.dtype) for s in csums],
                   *[pltpu.HBM(l.shape, l.dtype) for l in lands],
                   jax.ShapeDtypeStruct((SUBLANES, LANES), F32)),
        in_specs=[HBM] * (2 * na) + [ANY],
        out_specs=(SEM, SEM, *[HBM] * (2 * na), pl.BlockSpec(memory_space=pltpu.VMEM)),
        input_output_aliases={i: 2 + i for i in range(2 * na)},
        compiler_params=pltpu.CompilerParams(has_side_effects=EFFECT),
    )(*[_in_hbm(s) for s in csums], *[_in_hbm(l) for l in lands], after)
    return outs[0], outs[1], outs[2:2 + na], outs[2 + na:2 + 2 * na], outs[-1]


def _scatter_wait(name, started, after):
    send_sems, recv_sems, csums, lands, _ = started
    na = len(csums)

    def body(*refs):
        c_refs, land_refs = refs[:na], refs[na:2 * na]
        s_sems, r_sems = refs[2 * na], refs[2 * na + 1]
        for a in range(na):
            for k, frm in enumerate(_other_chips()):
                cp = pltpu.make_async_remote_copy(
                    src_ref=c_refs[a].at[2 * frm[0] + frm[1]], dst_ref=land_refs[a].at[2 * frm[0] + frm[1]],
                    send_sem=s_sems.at[3 * a + k], recv_sem=r_sems.at[3 * a + k],
                    device_id=frm, device_id_type=MESH)
                cp.wait_send()
                cp.wait_recv()

    outs = pl.pallas_call(
        body, name=name,
        out_shape=(*[pltpu.HBM(s.shape, s.dtype) for s in csums],
                   *[pltpu.HBM(l.shape, l.dtype) for l in lands]),
        in_specs=[HBM] * (2 * na) + [SEM, SEM, ANY], out_specs=[HBM] * (2 * na),
        input_output_aliases={i: i for i in range(2 * na)},
        compiler_params=pltpu.CompilerParams(has_side_effects=EFFECT),
    )(*csums, *lands, send_sems, recv_sems, after)
    return outs[na:]


def _adam_math(w, g, m, v):
    m = ADAM_B1 * m + (1.0 - ADAM_B1) * g
    v = ADAM_B2 * v + (1.0 - ADAM_B2) * (g * g)
    m_hat = m / (1.0 - ADAM_B1 ** ADAM_STEP)
    v_hat = v / (1.0 - ADAM_B2 ** ADAM_STEP)
    delta = -ADAM_LR * (m_hat / (jnp.sqrt(v_hat) + ADAM_EPS) + ADAM_WD * w)
    return delta, m, v


def _sum_adamw(name, parts, w, m, v, layer, prev=None, dep=None):
    nl, r, c = w.shape
    nparts = parts.shape[0]
    tr = _tile(r, max(SUBLANES, 360448 // c), SUBLANES)

    def body(p_ref, w_ref, m_ref, v_ref, *rest):
        g_ref, d_ref, mo_ref, vo_ref = rest[-4:]
        g = p_ref[0].astype(F32)
        for s in range(1, nparts):
            g = g + p_ref[s].astype(F32)
        delta, mn, vn = _adam_math(w_ref[...], g, m_ref[...], v_ref[...])
        g_ref[...] = g
        d_ref[...] = delta
        mo_ref[...] = mn
        vo_ref[...] = vn

    row = pl.BlockSpec((None, tr, c), lambda i: (layer, i, 0))
    shp = jax.ShapeDtypeStruct((nl, r, c), F32)
    extra = ([] if prev is None else list(prev)) + ([] if dep is None else [dep])
    return pl.pallas_call(
        body, name=name, grid=(r // tr,),
        in_specs=[pl.BlockSpec((nparts, tr, c), lambda i: (0, i, 0)), row, row, row] + [ANY] * len(extra),
        out_specs=[row, row, row, row], out_shape=[shp, shp, shp, shp],
        input_output_aliases={} if prev is None else {4 + i: i for i in range(4)},
        compiler_params=_cp("parallel"))(parts, w, m, v, *extra)


def _sum_parts(name, parts):
    _, r, c = parts.shape

    def body(p_ref, o_ref):
        g = p_ref[0]
        for s in range(1, NDEV):
            g = g + p_ref[s]
        o_ref[...] = g

    return pl.pallas_call(
        body, name=name, grid=(1,),
        in_specs=[pl.BlockSpec((NDEV, r, c), lambda i: (0, 0, 0))],
        out_specs=pl.BlockSpec((r, c), lambda i: (0, 0)),
        out_shape=jax.ShapeDtypeStruct((r, c), F32), compiler_params=_cp("arbitrary"))(parts)


def _adamw(name, w, g, m, v):
    r, c = w.shape

    def body(w_ref, g_ref, m_ref, v_ref, d_ref, mo_ref, vo_ref):
        delta, mn, vn = _adam_math(w_ref[...], g_ref[...], m_ref[...], v_ref[...])
        d_ref[...] = delta
        mo_ref[...] = mn
        vo_ref[...] = vn

    full = pl.BlockSpec((r, c), lambda i: (0, 0))
    shp = jax.ShapeDtypeStruct((r, c), F32)
    return pl.pallas_call(
        body, name=name, grid=(1,), in_specs=[full] * 4, out_specs=[full] * 3,
        out_shape=[shp] * 3, compiler_params=_cp("arbitrary"))(w, g, m, v)


def _pack(arrays):
    flat = jnp.concatenate([a.reshape(-1) for a in arrays])
    unit = SUBLANES * LANES
    pad = (-flat.shape[0]) % unit
    return jnp.pad(flat, (0, pad)).reshape(-1, LANES)


def _unpack(buf, shapes):
    flat = buf.reshape(-1)
    out, off = [], 0
    for shp in shapes:
        size = 1
        for s in shp:
            size *= s
        out.append(flat[off:off + size].reshape(shp))
        off += size
    return out


WEIGHTS = ["mix_norm_e", "w_in_e", "conv_w_e", "conv_b_e", "ln_g_e", "ln_b_e", "w_pool_e",
           "pool_scale_e", "w_out_e", "mix_norm_o", "w_in_o", "conv_w_o", "w_out_o", "ffn_norm",
           "w_gate", "w_up", "w_down", "final_norm"]
BIG = ["w_in_e", "w_out_e", "w_in_o", "w_out_o", "w_gate", "w_up", "w_down"]
SHARDED_SMALL = {"conv_w_e": 1, "w_pool_e": 1, "mix_norm_o": 0, "conv_w_o": 1}
SMALL = [n for n in WEIGHTS if n not in BIG]


def kernel(x, mix_norm_e, w_in_e, conv_w_e, conv_b_e, ln_g_e, ln_b_e, w_pool_e, pool_scale_e, w_out_e, mix_norm_o, w_in_o, conv_w_o, w_out_o, ffn_norm, w_gate, w_up, w_down, final_norm, loss_target, m_mix_norm_e, m_w_in_e, m_conv_w_e, m_conv_b_e, m_ln_g_e, m_ln_b_e, m_w_pool_e, m_pool_scale_e, m_w_out_e, m_mix_norm_o, m_w_in_o, m_conv_w_o, m_w_out_o, m_ffn_norm, m_w_gate, m_w_up, m_w_down, m_final_norm, v_mix_norm_e, v_w_in_e, v_conv_w_e, v_conv_b_e, v_ln_g_e, v_ln_b_e, v_w_pool_e, v_pool_scale_e, v_w_out_e, v_mix_norm_o, v_w_in_o, v_conv_w_o, v_w_out_o, v_ffn_norm, v_w_gate, v_w_up, v_w_down, v_final_norm):
    wts = dict(zip(WEIGHTS, [mix_norm_e, w_in_e, conv_w_e, conv_b_e, ln_g_e, ln_b_e, w_pool_e, pool_scale_e, w_out_e, mix_norm_o, w_in_o, conv_w_o, w_out_o, ffn_norm, w_gate, w_up, w_down, final_norm]))
    mom = dict(zip(WEIGHTS, [m_mix_norm_e, m_w_in_e, m_conv_w_e, m_conv_b_e, m_ln_g_e, m_ln_b_e, m_w_pool_e, m_pool_scale_e, m_w_out_e, m_mix_norm_o, m_w_in_o, m_conv_w_o, m_w_out_o, m_ffn_norm, m_w_gate, m_w_up, m_w_down, m_final_norm]))
    var = dict(zip(WEIGHTS, [v_mix_norm_e, v_w_in_e, v_conv_w_e, v_conv_b_e, v_ln_g_e, v_ln_b_e, v_w_pool_e, v_pool_scale_e, v_w_out_e, v_mix_norm_o, v_w_in_o, v_conv_w_o, v_w_out_o, v_ffn_norm, v_w_gate, v_w_up, v_w_down, v_final_norm]))
    bsz, seq, d = x.shape
    t = bsz * seq
    me = _index(_place())
    me_arr = jnp.reshape(me, (1,)).astype(jnp.int32)

    sh_names = list(SHARDED_SMALL)
    sh_local = [wts[n][0] for n in sh_names]
    packed = _pack(sh_local)
    params_st = _split_start("small_params_start", [packed, lax.empty((NDEV,) + packed.shape, F32)], NDEV - 1,
                             _everyone_plan, x)

    for state in (wts, mom, var):
        for n in ("w_gate", "w_up"):
            state[n] = jnp.swapaxes(state[n], 1, 2)
    bf = lambda a: a.astype(BF16)
    mix_kinds, ffn_kinds = ["col", "blk"], ["blk", "blk", "blk"]
    ffn_names = ("w_gate", "w_up", "w_down")
    groups = {
        "mix_e": ([w_in_e.shape[2], d], mix_kinds, [("w_in_e", 0), ("w_out_e", 0)]),
        "ffn0": ([d, d, d], ffn_kinds, [(n, 0) for n in ffn_names]),
        "mix_o": ([w_in_o.shape[2], d], mix_kinds, [("w_in_o", 0), ("w_out_o", 0)]),
        "ffn1": ([d, d, d], ffn_kinds, [(n, 1) for n in ffn_names]),
    }
    gathers = {
        "in_e": ([bf(w_in_e[0])], ["col"]), "out_e": ([bf(w_out_e[0])], ["blk"]),
        "gu0": ([bf(wts["w_gate"][0]), bf(wts["w_up"][0])], ["blk", "blk"]), "down0": ([bf(w_down[0])], ["blk"]),
        "in_o": ([bf(w_in_o[0])], ["col"]), "out_o": ([bf(w_out_o[0])], ["blk"]),
        "gu1": ([bf(wts["w_gate"][1]), bf(wts["w_up"][1])], ["blk", "blk"]), "down1": ([bf(w_down[1])], ["blk"]),
    }
    started, prev = {}, params_st[3]
    for grp, (shards, kinds) in gathers.items():
        started[grp] = _gather_start("gather_start_" + grp, shards, kinds, prev)
        prev = started[grp][4]
    all_started = prev[0, 0:1]

    bufs = _split_wait("small_params_wait", params_st, _everyone_plan, prev)
    gathered = _own_copy("small_params_own", bufs[0], bufs[1], "blk", me_arr)
    small = {n: wts[n][0] for n in SMALL if n not in SHARDED_SMALL and n not in ("ffn_norm", "final_norm")}
    small["ffn_norm"], small["final_norm"] = ffn_norm, final_norm
    flat, off = gathered.reshape(NDEV, -1), 0
    for n, a in zip(sh_names, sh_local):
        ax, shp = SHARDED_SMALL[n], a.shape
        blocks = jnp.moveaxis(flat[:, off:off + a.size].reshape((NDEV,) + shp), 0, ax)
        small[n] = blocks.reshape(shp[:ax] + (NDEV * shp[ax],) + shp[ax + 1:])
        off += a.size

    passing, shards_of = {}, {}

    def pass_on(grp, after):
        shards, kinds = gathers[grp]
        shards_of[grp], lands = _gather_wait("gather_wait_" + grp, started[grp], kinds, after)
        plan = _forward_plan(kinds, [s.shape[1] for s in shards])
        passing[grp] = (_split_start("forward_start_" + grp, lands, 3 * len(lands), plan, after), plan)
        return passing[grp][0][3]

    def get_w(grp, after):
        if grp not in passing:
            after = pass_on(grp, after)
        st, plan = passing[grp]
        lands = _split_wait("forward_wait_" + grp, st, plan, after)
        full = [_own_copy("own_copy_%s%d" % (grp, a), shards_of[grp][a], lands[a], gathers[grp][1][a], me_arr)
                for a in range(len(lands))]
        return [f.reshape(-1, d) if kind == "blk" else f for f, kind in zip(full, gathers[grp][1])]

    cx, cy, cc = _place()
    place = jnp.stack([cc, 2 * cx + cy]).astype(jnp.int32)
    bwd_order = ["ffn1", "mix_o", "ffn0", "mix_e"]
    pairing, pending, results = {}, {}, {}

    late_names = ["conv_w_e", "mix_norm_e"]
    early_names = [n for n in SMALL if n not in late_names] + ["loss"]
    small_sent = {}

    def send_small(tag, arrays, after):
        mine = _pack(arrays)
        small_sent[tag] = _split_start(tag + "_start", [mine, lax.empty((NDEV,) + mine.shape, F32)], NDEV - 1,
                                       _everyone_plan, after)
        return small_sent[tag][3]

    def summed_small(tag, shapes, after):
        bufs = _split_wait(tag + "_wait", small_sent[tag], _everyone_plan, after)
        parts = _own_copy(tag + "_own", bufs[0], bufs[1], "blk", me_arr)
        return _unpack(_sum_parts(tag + "_sum", parts), shapes)

    def put_g(grp, grads):
        if grp == "small":
            small_sent["shapes"] = [grads[n].shape for n in early_names]
            return send_small("small_grads", [grads[n] for n in early_names], place)
        nloc, kinds, _ = groups[grp]
        if len(kinds) == 2:
            fulls = [grads["w_in"], grads["w_out"].reshape(NDEV, -1, d)]
        else:
            fulls = [grads[n].reshape(NDEV, -1, d) for n in ffn_names]
        empties = []
        for g, kind, n in zip(fulls, kinds, nloc):
            empties.append(lax.empty((NCHIPS, g.shape[1] if kind == "blk" else g.shape[0], n), g.dtype))
        plan = _pair_plan(kinds, nloc)
        pairing[grp] = (_split_start("pair_start_" + grp, fulls + empties, NCHIPS * len(fulls), plan, place),
                        plan, kinds, nloc)
        token = pairing[grp][0][3]
        return send_sums(grp, token) if grp == bwd_order[-1] else token

    def send_sums(grp, after):
        st, plan, kinds, nloc = pairing[grp]
        bufs = _split_wait("pair_wait_" + grp, st, plan, after)
        na = len(kinds)
        sums = [_chip_sum("chip_sum_%s%d" % (grp, a), bufs[a], kinds[a], nloc[a], bufs[na + a], place)
                for a in range(na)]
        pending[grp] = _scatter_start("scatter_start_" + grp, [s[0] for s in sums], [s[1] for s in sums], after)
        return pending[grp][4]

    def finish(grp, after):
        lands = _scatter_wait("scatter_wait_" + grp, pending[grp], after)
        dep = None
        for (n, l), parts in zip(groups[grp][2], lands):
            results[n] = _sum_adamw("adamw_%s%d" % (n, l), parts, wts[n], mom[n], var[n], l, results.get(n), dep)
            dep = results[n][1]
        return dep

    fwd_sync = {"fwd_a": ["out_e"], "fwd_b": ["gu0"], "fwd_c": ["down0", "in_o"], "fwd_d": ["out_o"],
                "fwd_e": ["gu1"], "fwd_f": ["down1"]}

    def sync(tag, after):
        if tag in fwd_sync:
            for grp in fwd_sync[tag]:
                after = pass_on(grp, after)
            return after
        if tag == "bwd_mix_o":
            return send_sums("ffn1", after)
        if tag == "bwd_ffn0":
            return finish("ffn1", send_sums("mix_o", after))
        if tag == "bwd_mix_e":
            return finish("mix_o", send_sums("ffn0", after))
        return None

    small["mix_norm_e"] = small["mix_norm_e"] + all_started
    dx, late = _local_step(x.reshape(t, d), loss_target.reshape(t, d), seq, small, get_w, put_g, sync)

    out_g, out_d, out_m, out_v = {}, {}, {}, {}

    dep = finish("ffn0", send_small("last_grads", [late[n] for n in late_names], dx))
    sums = dict(zip(early_names, summed_small("small_grads", small_sent["shapes"], dep)))
    sums.update(zip(late_names, summed_small("last_grads", [late[n].shape for n in late_names], dep)))
    loss = jnp.sum(sums["loss"])
    gs_sum = [sums[n] for n in SMALL]
    local_g = []
    for n, g in zip(SMALL, gs_sum):
        if n in SHARDED_SMALL:
            ax = SHARDED_SMALL[n]
            size = wts[n].shape[ax + 1]
            g = lax.dynamic_slice_in_dim(g, me * size, size, axis=ax)
        local_g.append(g.reshape(wts[n].shape))
    shapes = [wts[n].shape for n in SMALL]
    upd = _adamw("adamw_small", _pack([wts[n] for n in SMALL]), _pack(local_g),
                 _pack([mom[n] for n in SMALL]), _pack([var[n] for n in SMALL]))
    for i, outd in enumerate((out_d, out_m, out_v)):
        for n, a in zip(SMALL, _unpack(upd[i], shapes)):
            outd[n] = a
    for n, g in zip(SMALL, local_g):
        out_g[n] = g

    finish("mix_e", upd[0])
    for n in BIG:
        res = [jnp.swapaxes(a, 1, 2) for a in results[n]] if n in ("w_gate", "w_up") else results[n]
        out_g[n], out_d[n], out_m[n], out_v[n] = res

    return (loss, dx.reshape(bsz, seq, d), *[out_g[n] for n in WEIGHTS], *[out_d[n] for n in WEIGHTS],
            *[out_m[n] for n in WEIGHTS], *[out_v[n] for n in WEIGHTS])
```

```python
import jax
import jax.numpy as jnp
from jax import lax
from jax.experimental import pallas as pl
from jax.experimental.pallas import tpu as pltpu

F32 = jnp.float32
BF16 = jnp.bfloat16
NDEV = 8
MESH_AXES = ("x", "y", "c")
EPS = 1e-6
POOL_WINDOWS = (2, 4, 8, 16)
ADAM_LR = 0.001
ADAM_B1 = 0.9
ADAM_B2 = 0.999
ADAM_EPS = 1e-08
ADAM_WD = 0.01
ADAM_STEP = 10
LANES = 128
SUBLANES = 8
VMEM_LIMIT = 56 * 1024 * 1024
MXU_DEPTH = 256
MM_TK = 2816
MESH = pl.DeviceIdType.MESH
ANY = pl.BlockSpec(memory_space=pl.ANY)


def _cp(*sem):
    return pltpu.CompilerParams(dimension_semantics=sem, vmem_limit_bytes=VMEM_LIMIT)


def _tile(n, pref, unit=LANES):
    if n <= pref:
        return n
    t = (pref // unit) * unit
    while t > unit and n % t:
        t -= unit
    assert n % t == 0, (n, pref)
    return t


def _sigmoid(v):
    return 0.5 * jnp.tanh(0.5 * v) + 0.5


def _mm(name, pairs, a_specs, b_specs, dims, out_shape, o_spec, grid, acc_shape,
        res=None, res_spec=None, dep=None):
    np_ = len(pairs)
    nk = grid[2]
    has_res = res is not None
    n_in = 2 * np_ + (1 if has_res else 0) + (0 if dep is None else 1)

    def body(*refs):
        a_refs = refs[:np_]
        b_refs = refs[np_:2 * np_]
        r_ref = refs[2 * np_] if has_res else None
        o_ref = refs[n_in]
        acc = refs[-1]

        def part():
            s = None
            for a_ref, b_ref in zip(a_refs, b_refs):
                d = lax.dot_general(a_ref[...], b_ref[...], dims, preferred_element_type=F32)
                s = d if s is None else s + d
            return s

        def finish(v):
            if has_res:
                v = v + r_ref[...]
            o_ref[...] = v.astype(o_ref.dtype)

        if nk == 1:
            finish(part())
        else:
            k = pl.program_id(2)

            @pl.when(k == 0)
            def _():
                acc[...] = part()

            @pl.when((k > 0) & (k < nk - 1))
            def _():
                acc[...] += part()

            @pl.when(k == nk - 1)
            def _():
                finish(acc[...] + part())

    ins = [p[0] for p in pairs] + [p[1] for p in pairs]
    specs = list(a_specs) + list(b_specs)
    if has_res:
        ins.append(res)
        specs.append(res_spec)
    if dep is not None:
        ins.append(dep)
        specs.append(ANY)
    return pl.pallas_call(
        body, name=name, grid=grid, in_specs=specs, out_specs=o_spec, out_shape=out_shape,
        scratch_shapes=[pltpu.VMEM(acc_shape if nk > 1 else (SUBLANES, LANES), F32)],
        compiler_params=_cp("parallel", "parallel", "arbitrary"))(*ins)


NN = (((1,), (0,)), ((), ()))
NT = (((1,), (1,)), ((), ()))
TN = (((0,), (0,)), ((), ()))


def _tiles_mk(m, kk):
    return _tile(m, 1024), _tile(kk, MM_TK, MXU_DEPTH)


def _mm_nn(name, a, b, out_dtype, res=None, dep=None):
    pairs = list(zip(a, b)) if isinstance(a, (list, tuple)) else [(a, b)]
    m, kk = pairs[0][0].shape
    n = pairs[0][1].shape[1]
    tm, tk = _tiles_mk(m, kk)
    tn = _tile(n, 1024 if tk * len(pairs) <= MM_TK else 512)
    return _mm(name, pairs,
               [pl.BlockSpec((tm, tk), lambda i, j, k: (i, k))] * len(pairs),
               [pl.BlockSpec((tk, tn), lambda i, j, k: (k, j))] * len(pairs), NN,
               jax.ShapeDtypeStruct((m, n), out_dtype),
               pl.BlockSpec((tm, tn), lambda i, j, k: (i, j)),
               (m // tm, n // tn, kk // tk), (tm, tn), res,
               pl.BlockSpec((tm, tn), lambda i, j, k: (i, j)), dep=dep)


def _mm_nt(name, a, b, out_dtype, dep=None):
    m, n = a.shape
    kk = b.shape[0]
    tn = _tile(kk, 1024)
    tm, tk = _tiles_mk(m, n)
    return _mm(name, [(a, b)],
               [pl.BlockSpec((tm, tk), lambda i, j, k: (i, k))],
               [pl.BlockSpec((tn, tk), lambda i, j, k: (j, k))], NT,
               jax.ShapeDtypeStruct((m, kk), out_dtype),
               pl.BlockSpec((tm, tn), lambda i, j, k: (i, j)),
               (m // tm, kk // tn, n // tk), (tm, tn), dep=dep)


def _mm_tn(name, a, b, out_dtype, dep=None):
    t, m = a.shape
    n = b.shape[1]
    tn = _tile(n, 1024)
    tm, tk = _tile(m, 1408), _tile(t, MM_TK, MXU_DEPTH)
    return _mm(name, [(a, b)],
               [pl.BlockSpec((tk, tm), lambda i, j, k: (k, i))],
               [pl.BlockSpec((tk, tn), lambda i, j, k: (k, j))], TN,
               jax.ShapeDtypeStruct((m, n), out_dtype),
               pl.BlockSpec((tm, tn), lambda i, j, k: (i, j)),
               (m // tm, n // tn, t // tk), (tm, tn), dep=dep)


def _ffn_fwd(name, n, wg, wu, dep=None):
    f, d = wg.shape
    t = n.shape[0]
    tm, tn = _tile(t, 1024), _tile(f, 512)

    def body(n_ref, wg_ref, wu_ref, *rest):
        act_ref, ds_ref, s_ref = rest[-3:]
        nv = n_ref[...]
        g = lax.dot_general(nv, wg_ref[...], NT, preferred_element_type=F32)
        up = lax.dot_general(nv, wu_ref[...], NT, preferred_element_type=F32)
        sg = _sigmoid(g)
        silu = g * sg
        act_ref[...] = (silu * up).astype(BF16)
        ds_ref[...] = (up * (sg * (1.0 + g * (1.0 - sg)))).astype(BF16)
        s_ref[...] = silu.astype(BF16)

    w_spec = pl.BlockSpec((tn, d), lambda j, i: (j, 0))
    o_spec = pl.BlockSpec((tm, tn), lambda j, i: (i, j))
    shp = jax.ShapeDtypeStruct((t, f), BF16)
    return pl.pallas_call(
        body, name=name, grid=(f // tn, t // tm),
        in_specs=[pl.BlockSpec((tm, d), lambda j, i: (i, 0)), w_spec, w_spec] + ([] if dep is None else [ANY]),
        out_specs=[o_spec, o_spec, o_spec], out_shape=[shp, shp, shp],
        compiler_params=_cp("parallel", "parallel"))(n, wg, wu, *([] if dep is None else [dep]))


def _ffn_bwd_act(name, dh, wd, dsilu, silu, dep=None):
    f, d = wd.shape
    t = dh.shape[0]
    tm, tn = _tile(t, 1024), _tile(f, 512)

    def body(dh_ref, wd_ref, ds_ref, s_ref, *rest):
        dg_ref, dup_ref = rest[-2:]
        da = lax.dot_general(dh_ref[...], wd_ref[...], NT, preferred_element_type=F32)
        dg_ref[...] = (da * ds_ref[...].astype(F32)).astype(BF16)
        dup_ref[...] = (da * s_ref[...].astype(F32)).astype(BF16)

    o_spec = pl.BlockSpec((tm, tn), lambda i, j: (i, j))
    shp = jax.ShapeDtypeStruct((t, f), BF16)
    return pl.pallas_call(
        body, name=name, grid=(t // tm, f // tn),
        in_specs=[pl.BlockSpec((tm, d), lambda i, j: (i, 0)),
                  pl.BlockSpec((tn, d), lambda i, j: (j, 0)), o_spec, o_spec]
        + ([] if dep is None else [ANY]),
        out_specs=[o_spec, o_spec], out_shape=[shp, shp],
        compiler_params=_cp("parallel", "parallel"))(dh, wd, dsilu, silu, *([] if dep is None else [dep]))


def _rms_fwd(name, h, gain):
    t, d = h.shape
    tr = _tile(t, 512, SUBLANES)

    def body(h_ref, g_ref, n_ref):
        hv = h_ref[...]
        r = lax.rsqrt(jnp.mean(hv * hv, axis=-1, keepdims=True) + EPS)
        n_ref[...] = (hv * r * g_ref[...]).astype(BF16)

    return pl.pallas_call(
        body, name=name, grid=(t // tr,),
        in_specs=[pl.BlockSpec((tr, d), lambda i: (i, 0)), pl.BlockSpec((1, d), lambda i: (0, 0))],
        out_specs=pl.BlockSpec((tr, d), lambda i: (i, 0)),
        out_shape=jax.ShapeDtypeStruct((t, d), BF16),
        compiler_params=_cp("parallel"))(h, gain)


def _rms_bwd_math(hv, gain, dn):
    d = hv.shape[-1]
    r = lax.rsqrt(jnp.mean(hv * hv, axis=-1, keepdims=True) + EPS)
    xhat = hv * r
    dxh = dn * gain
    dh = r * (dxh - xhat * (jnp.sum(dxh * xhat, axis=-1, keepdims=True) / d))
    dgain = jnp.sum(dn * xhat, axis=0, keepdims=True)
    return dh, dgain


def _rms_bwd(name, h, gain, dn, dres, bf16_too=True):
    t, d = h.shape
    tr = _tile(t, 256, SUBLANES)

    def body(h_ref, g_ref, dn_ref, dr_ref, dh_ref, *rest):
        dg_ref = rest[-1]
        dh, dgain = _rms_bwd_math(h_ref[...], g_ref[...], dn_ref[...].astype(F32))
        dh = dh + dr_ref[...]
        dh_ref[...] = dh
        if bf16_too:
            rest[0][...] = dh.astype(BF16)

        @pl.when(pl.program_id(0) == 0)
        def _():
            dg_ref[...] = dgain

        @pl.when(pl.program_id(0) > 0)
        def _():
            dg_ref[...] += dgain

    row = pl.BlockSpec((tr, d), lambda i: (i, 0))
    vec = pl.BlockSpec((1, d), lambda i: (0, 0))
    halves = [jax.ShapeDtypeStruct((t, d), BF16)] if bf16_too else []
    out = pl.pallas_call(
        body, name=name, grid=(t // tr,), in_specs=[row, vec, row, row],
        out_specs=[row] + [row] * len(halves) + [vec],
        out_shape=[jax.ShapeDtypeStruct((t, d), F32)] + halves + [jax.ShapeDtypeStruct((1, d), F32)],
        compiler_params=_cp("arbitrary"))(h, gain, dn, dres)
    return (out[0], out[1], out[2]) if bf16_too else (out[0], None, out[1])


def _loss_head(name, h, gain, tgt):
    t, d = h.shape
    tr = _tile(t, 256, SUBLANES)

    def body(h_ref, g_ref, t_ref, dh_ref, dhb_ref, dg_ref, ls_ref):
        hv = h_ref[...]
        gv = g_ref[...]
        r = lax.rsqrt(jnp.mean(hv * hv, axis=-1, keepdims=True) + EPS)
        err = hv * r * gv - t_ref[...]
        lsum = 0.5 * jnp.sum(err * err, axis=0, keepdims=True) / d
        dh, dgain = _rms_bwd_math(hv, gv, err / d)
        dh_ref[...] = dh
        dhb_ref[...] = dh.astype(BF16)

        @pl.when(pl.program_id(0) == 0)
        def _():
            dg_ref[...] = dgain
            ls_ref[...] = lsum

        @pl.when(pl.program_id(0) > 0)
        def _():
            dg_ref[...] += dgain
            ls_ref[...] += lsum

    row = pl.BlockSpec((tr, d), lambda i: (i, 0))
    vec = pl.BlockSpec((1, d), lambda i: (0, 0))
    return pl.pallas_call(
        body, name=name, grid=(t // tr,), in_specs=[row, vec, row],
        out_specs=[row, row, vec, vec],
        out_shape=[jax.ShapeDtypeStruct((t, d), F32), jax.ShapeDtypeStruct((t, d), BF16),
                   jax.ShapeDtypeStruct((1, d), F32), jax.ShapeDtypeStruct((1, d), F32)],
        compiler_params=_cp("arbitrary"))(h, gain, tgt)


def _conv_geom(t, seq, c, k, full_width=False):
    halo = 32 if k - 1 > SUBLANES else SUBLANES
    assert k - 1 <= halo
    tm = min(256 if halo > SUBLANES else 1024, seq // 2)
    tc = c if full_width else min(512, c)
    assert seq % tm == 0 and tm % halo == 0 and c % tc == 0 and t % seq == 0
    return halo, tm, tc, min(64 if halo > SUBLANES else 128, tm), min(LANES, tc)


def _pre(kind, a, b):
    if kind == "glu":
        return a * _sigmoid(b)
    if kind == "mul":
        return a * b
    return a


def _taps(k):
    return sorted((s % SUBLANES, s // SUBLANES, s) for s in range(k))


def _conv_fwd(name, seq, c, w, x1, c1, x2=None, c2=0, pre=None, bias=None, post=None, cpost=0, live=None):
    t = x1.shape[0]
    k = w.shape[0]
    halo, tm, tc, sr, sl = _conv_geom(t, seq, c, k, live is not None)
    nb, cps = tm // halo, seq // tm
    two = x2 is not None
    has_bias, has_post = bias is not None, post is not None

    def body(*refs):
        it = iter(refs)
        x1c, x1h = next(it), next(it)
        x2c, x2h = (next(it), next(it)) if two else (None, None)
        w_ref = next(it)
        b_ref = next(it) if has_bias else None
        p_ref = next(it) if has_post else None
        o_ref = next(it)
        y_ref = next(it) if has_post else None
        xs = next(it)
        first = (pl.program_id(1) % cps) == 0
        hv = _pre(pre, x1h[...].astype(F32), x2h[...].astype(F32) if two else None)
        xs[0:halo, :] = jnp.where(first, 0.0, hv)
        xs[halo:halo + tm, :] = _pre(pre, x1c[...].astype(F32), x2c[...].astype(F32) if two else None)
        for l0 in range(0, tc, sl):
            ls = slice(l0, l0 + sl)
            for r0 in range(0, tm, sr):
                win = xs[r0:r0 + sr + halo, ls]
                acc = jnp.zeros((sr, sl), F32)
                rolled = {}
                for r, q, s in _taps(k if live is None else live[l0 // sl]):
                    if r not in rolled:
                        rolled[r] = win if r == 0 else pltpu.roll(win, r, 0)
                    lo = halo - SUBLANES * q
                    acc = acc + w_ref[k - 1 - s:k - s, ls] * rolled[r][lo:lo + sr]
                if has_bias:
                    acc = acc + b_ref[:, ls]
                o_ref[r0:r0 + sr, ls] = acc.astype(o_ref.dtype)
                if has_post:
                    y_ref[r0:r0 + sr, ls] = (acc * p_ref[r0:r0 + sr, ls].astype(F32)).astype(y_ref.dtype)

    def cur(off):
        return pl.BlockSpec((tm, tc), lambda j, i: (i, off // tc + j))

    def prev(off):
        return pl.BlockSpec((halo, tc), lambda j, i: (jnp.maximum(i * nb - 1, 0), off // tc + j))

    ins, specs = [x1, x1], [cur(c1), prev(c1)]
    if two:
        ins += [x2, x2]
        specs += [cur(c2), prev(c2)]
    ins.append(w)
    specs.append(pl.BlockSpec((k, tc), lambda j, i: (0, j)))
    if has_bias:
        ins.append(bias)
        specs.append(pl.BlockSpec((1, tc), lambda j, i: (0, j)))
    if has_post:
        ins.append(post)
        specs.append(cur(cpost))
    o_spec = pl.BlockSpec((tm, tc), lambda j, i: (i, j))
    shp = jax.ShapeDtypeStruct((t, c), BF16)
    return pl.pallas_call(
        body, name=name, grid=(c // tc, t // tm), in_specs=specs,
        out_specs=[o_spec, o_spec] if has_post else o_spec,
        out_shape=[shp, shp] if has_post else shp,
        scratch_shapes=[pltpu.VMEM((halo + tm, tc), F32)],
        compiler_params=_cp("parallel", "parallel"))(*ins)


def _conv_bwd(name, seq, c, w, d1, cd1, d2=None, cd2=0, dpre=None,
              x1=None, c1=0, x2=None, c2=0, pre=None, live=None, dep=None):
    t = d1.shape[0]
    k = w.shape[0]
    assert live is None or x1 is None
    halo, tm, tc, sr, sl = _conv_geom(t, seq, c, k, live is not None)
    nb, cps = tm // halo, seq // tm
    nchunks = t // tm
    dtwo, xtwo, has_x = d2 is not None, x2 is not None, x1 is not None

    def body(*refs):
        it = iter(refs)
        d1c, d1n = next(it), next(it)
        d2c, d2n = (next(it), next(it)) if dtwo else (None, None)
        x1c, x1h = (next(it), next(it)) if has_x else (None, None)
        x2c, x2h = (next(it), next(it)) if xtwo else (None, None)
        w_ref = next(it)
        if dep is not None:
            next(it)
        dx_ref = next(it)
        dw_ref = next(it) if has_x else None
        ds = next(it)
        xs = next(it) if has_x else None
        i = pl.program_id(1)
        last = (i % cps) == cps - 1
        ds[0:tm, :] = _pre(dpre, d1c[...].astype(F32), d2c[...].astype(F32) if dtwo else None)
        nv = _pre(dpre, d1n[...].astype(F32), d2n[...].astype(F32) if dtwo else None)
        ds[tm:tm + halo, :] = jnp.where(last, 0.0, nv)
        if has_x:
            first = (i % cps) == 0
            hv = _pre(pre, x1h[...].astype(F32), x2h[...].astype(F32) if xtwo else None)
            xs[0:halo, :] = jnp.where(first, 0.0, hv)
            xs[halo:halo + tm, :] = _pre(pre, x1c[...].astype(F32), x2c[...].astype(F32) if xtwo else None)

            @pl.when(i == 0)
            def _():
                dw_ref[...] = jnp.zeros_like(dw_ref)

        for l0 in range(0, tc, sl):
            ls = slice(l0, l0 + sl)
            for r0 in range(0, tm, sr):
                win = ds[r0:r0 + sr + halo, ls]
                nrow = sr + halo
                acc = jnp.zeros((sr, sl), F32)
                rolled = {}
                for r, q, s in _taps(k if live is None else live[l0 // sl]):
                    if r not in rolled:
                        rolled[r] = win if r == 0 else pltpu.roll(win, nrow - r, 0)
                    lo = SUBLANES * q
                    acc = acc + w_ref[k - 1 - s:k - s, ls] * rolled[r][lo:lo + sr]
                dx_ref[r0:r0 + sr, ls] = acc.astype(dx_ref.dtype)
                if has_x:
                    dcur = win[0:sr]
                    xwin = xs[r0:r0 + sr + halo, ls]
                    xrolled = {}
                    for r, q, s in _taps(k):
                        if r not in xrolled:
                            xrolled[r] = xwin if r == 0 else pltpu.roll(xwin, r, 0)
                        lo = halo - SUBLANES * q
                        part = jnp.sum(dcur * xrolled[r][lo:lo + sr], axis=0, keepdims=True)
                        dw_ref[k - 1 - s:k - s, ls] += part

    def cur(off):
        return pl.BlockSpec((tm, tc), lambda j, i: (i, off // tc + j))

    def prev(off):
        return pl.BlockSpec((halo, tc), lambda j, i: (jnp.maximum(i * nb - 1, 0), off // tc + j))

    def nxt(off):
        return pl.BlockSpec((halo, tc),
                            lambda j, i: (jnp.minimum((i + 1) * nb, nchunks * nb - 1), off // tc + j))

    ins, specs = [d1, d1], [cur(cd1), nxt(cd1)]
    if dtwo:
        ins += [d2, d2]
        specs += [cur(cd2), nxt(cd2)]
    if has_x:
        ins += [x1, x1]
        specs += [cur(c1), prev(c1)]
    if xtwo:
        ins += [x2, x2]
        specs += [cur(c2), prev(c2)]
    ins.append(w)
    specs.append(pl.BlockSpec((k, tc), lambda j, i: (0, j)))
    if dep is not None:
        ins.append(dep)
        specs.append(ANY)
    o_specs = [pl.BlockSpec((tm, tc), lambda j, i: (i, j))]
    o_shapes = [jax.ShapeDtypeStruct((t, c), BF16)]
    scratch = [pltpu.VMEM((tm + halo, tc), F32)]
    if has_x:
        o_specs.append(pl.BlockSpec((k, tc), lambda j, i: (0, j)))
        o_shapes.append(jax.ShapeDtypeStruct((k, c), F32))
        scratch.append(pltpu.VMEM((halo + tm, tc), F32))
    out = pl.pallas_call(
        body, name=name, grid=(c // tc, t // tm), in_specs=specs, out_specs=o_specs,
        out_shape=o_shapes, scratch_shapes=scratch,
        compiler_params=_cp("parallel", "arbitrary"))(*ins)
    return out if has_x else out[0]


def _pool_taps(c):
    kmax = max(POOL_WINDOWS)
    grp = c // len(POOL_WINDOWS)
    cols = []
    for wdw in POOL_WINDOWS:
        col = jnp.concatenate([jnp.zeros((kmax - wdw,), F32), jnp.ones((wdw,), F32)])
        cols.append(jnp.tile(col[:, None], (1, grp)))
    return jnp.concatenate(cols, axis=1)


def _pool_live(c):
    grp, sl = c // len(POOL_WINDOWS), min(LANES, c)
    return tuple(max(POOL_WINDOWS[g] for g in range(l0 // grp, (l0 + sl - 1) // grp + 1))
                 for l0 in range(0, c, sl))


def _counts(i, tr, seq, grp):
    pos = (i * tr + lax.broadcasted_iota(jnp.int32, (tr, 1), 0)) % seq + 1
    return [1.0 / jnp.minimum(pos, wdw).astype(F32) for wdw in POOL_WINDOWS]


def _ln_stats(a2):
    mu = jnp.mean(a2, axis=-1, keepdims=True)
    xc = a2 - mu
    rstd = lax.rsqrt(jnp.mean(xc * xc, axis=-1, keepdims=True) + EPS)
    return xc * rstd, rstd


def _even_fwd(name, seq, a2, ws, u, ln_g, ln_b, w_pool, scale):
    t, c = a2.shape
    ng = len(POOL_WINDOWS)
    grp = c // ng
    tr = _tile(t, 256, SUBLANES)

    def body(a_ref, ws_ref, b_ref, g_ref, bb_ref, wp_ref, sc_ref, z_ref, pm_ref):
        xhat, _ = _ln_stats(a_ref[...].astype(F32))
        l = xhat * g_ref[...] + bb_ref[...]
        z_ref[:, 0:c] = (l * _sigmoid(l)).astype(BF16)
        inv = _counts(pl.program_id(0), tr, seq, grp)
        for g in range(ng):
            gs = slice(g * grp, (g + 1) * grp)
            pm = (ws_ref[:, gs].astype(F32) * inv[g] - b_ref[:, gs].astype(F32)).astype(BF16)
            pm_ref[:, gs] = pm
            q = jnp.dot(pm, wp_ref[g], preferred_element_type=F32)
            z_ref[:, c + g * grp:c + (g + 1) * grp] = (q * sc_ref[:, gs]).astype(BF16)

    row = pl.BlockSpec((tr, c), lambda i: (i, 0))
    vec = pl.BlockSpec((1, c), lambda i: (0, 0))
    return pl.pallas_call(
        body, name=name, grid=(t // tr,),
        in_specs=[row, row, pl.BlockSpec((tr, c), lambda i: (i, 2)), vec, vec,
                  pl.BlockSpec((ng, grp, grp), lambda i: (0, 0, 0)), vec],
        out_specs=[pl.BlockSpec((tr, 2 * c), lambda i: (i, 0)), row],
        out_shape=[jax.ShapeDtypeStruct((t, 2 * c), BF16), jax.ShapeDtypeStruct((t, c), BF16)],
        compiler_params=_cp("parallel"))(a2, ws, u, ln_g, ln_b, w_pool, scale)


def _even_bwd(name, seq, dz, a2, pm, ln_g, ln_b, w_pool, scale, dep=None):
    t, c = a2.shape
    ng = len(POOL_WINDOWS)
    grp = c // ng
    tr = _tile(t, 256, SUBLANES)

    def body(dz_ref, a_ref, pm_ref, g_ref, bb_ref, wp_ref, sc_ref, *rest):
        da_ref, dws_ref, dpm_ref, vec_ref, dwp_ref = rest[-5:]
        i = pl.program_id(0)

        @pl.when(i == 0)
        def _():
            vec_ref[...] = jnp.zeros_like(vec_ref)
            dwp_ref[...] = jnp.zeros_like(dwp_ref)

        xhat, rstd = _ln_stats(a_ref[...].astype(F32))
        gv = g_ref[...]
        l = xhat * gv + bb_ref[...]
        sg = _sigmoid(l)
        dl = dz_ref[:, 0:c].astype(F32) * (sg * (1.0 + l * (1.0 - sg)))
        dxh = dl * gv
        da2 = rstd * (dxh - jnp.mean(dxh, axis=-1, keepdims=True)
                      - xhat * jnp.mean(dxh * xhat, axis=-1, keepdims=True))
        da_ref[...] = da2.astype(BF16)
        vec_ref[0:1, :] += jnp.sum(dl * xhat, axis=0, keepdims=True)
        vec_ref[1:2, :] += jnp.sum(dl, axis=0, keepdims=True)
        vec_ref[2:3, :] += jnp.sum(da2, axis=0, keepdims=True)
        inv = _counts(i, tr, seq, grp)
        for g in range(ng):
            gs = slice(g * grp, (g + 1) * grp)
            pmv = pm_ref[:, gs]
            wp = wp_ref[g]
            dp = dz_ref[:, c + g * grp:c + (g + 1) * grp].astype(F32)
            q = jnp.dot(pmv, wp, preferred_element_type=F32)
            vec_ref[3:4, gs] += jnp.sum(dp * q, axis=0, keepdims=True)
            dq = (dp * sc_ref[:, gs]).astype(BF16)
            dpm = lax.dot_general(dq, wp, NT, preferred_element_type=F32)
            dwp_ref[g] += lax.dot_general(pmv, dq, TN, preferred_element_type=F32)
            dpm_ref[:, gs] = dpm.astype(BF16)
            dws_ref[:, gs] = (dpm * inv[g]).astype(BF16)

    row = pl.BlockSpec((tr, c), lambda i: (i, 0))
    vec = pl.BlockSpec((1, c), lambda i: (0, 0))
    rshape = jax.ShapeDtypeStruct((t, c), BF16)
    return pl.pallas_call(
        body, name=name, grid=(t // tr,),
        in_specs=[pl.BlockSpec((tr, 2 * c), lambda i: (i, 0)), row, row, vec, vec,
                  pl.BlockSpec((ng, grp, grp), lambda i: (0, 0, 0)), vec] + ([] if dep is None else [ANY]),
        out_specs=[row, row, row, pl.BlockSpec((SUBLANES, c), lambda i: (0, 0)),
                   pl.BlockSpec((ng, grp, grp), lambda i: (0, 0, 0))],
        out_shape=[rshape, rshape, rshape, jax.ShapeDtypeStruct((SUBLANES, c), F32),
                   jax.ShapeDtypeStruct((ng, grp, grp), F32)],
        compiler_params=_cp("arbitrary"))(dz, a2, pm, ln_g, ln_b, w_pool, scale,
                                           *([] if dep is None else [dep]))


def _even_du(name, u, da1, dbp, dpm):
    t, c = da1.shape
    tr = _tile(t, 256, SUBLANES)

    def body(u_ref, da_ref, dbp_ref, dpm_ref, du_ref):
        val = u_ref[:, 0:c].astype(F32)
        sg = _sigmoid(u_ref[:, c:2 * c].astype(F32))
        da = da_ref[...].astype(F32)
        du_ref[:, 0:c] = (da * sg).astype(BF16)
        du_ref[:, c:2 * c] = (da * val * sg * (1.0 - sg)).astype(BF16)
        du_ref[:, 2 * c:3 * c] = (dbp_ref[...].astype(F32) - dpm_ref[...].astype(F32)).astype(BF16)

    row = pl.BlockSpec((tr, c), lambda i: (i, 0))
    wide = pl.BlockSpec((tr, 3 * c), lambda i: (i, 0))
    return pl.pallas_call(
        body, name=name, grid=(t // tr,), in_specs=[wide, row, row, row], out_specs=wide,
        out_shape=jax.ShapeDtypeStruct((t, 3 * c), BF16),
        compiler_params=_cp("parallel"))(u, da1, dbp, dpm)


def _odd_du(name, u, dy, co, dxc):
    t, c = dy.shape
    tr = _tile(t, 256, SUBLANES)

    def body(u_ref, dy_ref, co_ref, dx_ref, du_ref):
        dx = dx_ref[...].astype(F32)
        du_ref[:, 0:c] = (dy_ref[...].astype(F32) * co_ref[...].astype(F32)).astype(BF16)
        du_ref[:, c:2 * c] = (dx * u_ref[:, 2 * c:3 * c].astype(F32)).astype(BF16)
        du_ref[:, 2 * c:3 * c] = (dx * u_ref[:, c:2 * c].astype(F32)).astype(BF16)

    row = pl.BlockSpec((tr, c), lambda i: (i, 0))
    wide = pl.BlockSpec((tr, 3 * c), lambda i: (i, 0))
    return pl.pallas_call(
        body, name=name, grid=(t // tr,), in_specs=[wide, row, row, row], out_specs=wide,
        out_shape=jax.ShapeDtypeStruct((t, 3 * c), BF16),
        compiler_params=_cp("parallel"))(u, dy, co, dxc)


def _local_step(x, tgt, seq, small, get_w, put_g, sync):
    t, d = x.shape
    c = d // 2
    cw_e, cw_o = small["conv_w_e"], small["conv_w_o"]
    wp = small["w_pool_e"].astype(BF16)
    ptaps = _pool_taps(c)
    row = lambda v: v.reshape(1, -1)

    we = {"w_in": get_w("in_e", x)[0]}
    n0 = _rms_fwd("rms_fwd_mix0", x, row(small["mix_norm_e"]))
    u0 = _mm_nn("mm_in_e", n0, we["w_in"], BF16)
    sync("fwd_a", u0)
    a2 = _conv_fwd("conv_e_fwd", seq, c, cw_e, u0, 0, u0, c, "glu", bias=row(small["conv_b_e"]))
    ws = _conv_fwd("pool_fwd", seq, c, ptaps, u0, 2 * c, live=_pool_live(c))
    z0, pm = _even_fwd("even_fwd", seq, a2, ws, u0, row(small["ln_g_e"]), row(small["ln_b_e"]),
                       wp, row(small["pool_scale_e"]))
    we["w_out"] = get_w("out_e", z0)[0]
    h1 = _mm_nn("mm_out_e", z0, we["w_out"], F32, res=x)
    sync("fwd_b", h1)
    n1 = _rms_fwd("rms_fwd_ffn0", h1, row(small["ffn_norm"][0]))
    wf0 = dict(zip(("w_gate", "w_up"), get_w("gu0", n1)))
    act0, ds0, s0 = _ffn_fwd("ffn0_fwd", n1, wf0["w_gate"], wf0["w_up"])
    dep = sync("fwd_c", act0)
    wf0["w_down"] = get_w("down0", act0)[0]
    h2 = _mm_nn("mm_down0", act0, wf0["w_down"], F32, res=h1, dep=dep)
    dep = sync("fwd_d", h2)
    n2 = _rms_fwd("rms_fwd_mix1", h2, row(small["mix_norm_o"]))
    wo = {"w_in": get_w("in_o", n2)[0]}
    u1 = _mm_nn("mm_in_o", n2, wo["w_in"], BF16, dep=dep)
    co, y1 = _conv_fwd("conv_o_fwd", seq, d, cw_o, u1, d, u1, 2 * d, "mul", post=u1, cpost=0)
    dep = sync("fwd_e", y1)
    wo["w_out"] = get_w("out_o", y1)[0]
    h3 = _mm_nn("mm_out_o", y1, wo["w_out"], F32, res=h2, dep=dep)
    dep = sync("fwd_f", h3)
    n3 = _rms_fwd("rms_fwd_ffn1", h3, row(small["ffn_norm"][1]))
    wf1 = dict(zip(("w_gate", "w_up"), get_w("gu1", n3)))
    act1, ds1, s1 = _ffn_fwd("ffn1_fwd", n3, wf1["w_gate"], wf1["w_up"], dep=dep)
    wf1["w_down"] = get_w("down1", act1)[0]
    h4 = _mm_nn("mm_down1", act1, wf1["w_down"], F32, res=h3)

    dh4, dh4b, d_final, lsum = _loss_head("loss_head", h4, row(small["final_norm"]), tgt)

    def ffn_bwd(tag, dh, dhb, h_in, gain, n, dsilu, silu, act, w, dep):
        dg, dup = _ffn_bwd_act("ffn%s_bwd_act" % tag, dhb, w["w_down"], dsilu, silu, dep=dep)
        dwd = _mm_tn("mm_dwd%s" % tag, act, dhb, BF16)
        dwg = _mm_tn("mm_dwg%s" % tag, dg, n, BF16, dep=sync("bwd_ffn" + tag, dwd))
        dwu = _mm_tn("mm_dwu%s" % tag, dup, n, BF16)
        dn = _mm_nn("mm_ffn_dn%s" % tag, [dg, dup], [w["w_gate"], w["w_up"]], BF16)
        dh_in, dhb_in, dgain = _rms_bwd("rms_bwd_ffn%s" % tag, h_in, gain, dn, dh)
        dep = put_g("ffn" + tag, {"w_gate": dwg, "w_up": dwu, "w_down": dwd})
        return dh_in, dhb_in, dgain, dep

    dh3, dh3b, d_ffn1, dep = ffn_bwd("1", dh4, dh4b, h3, row(small["ffn_norm"][1]), n3, ds1, s1,
                                     act1, wf1, None)
    dw_out_o = _mm_tn("mm_dw_out_o", y1, dh3b, BF16, dep=dep)
    dy1 = _mm_nt("mm_dy_o", dh3b, wo["w_out"], BF16, dep=dw_out_o)
    dxc, dcw_o = _conv_bwd("conv_o_bwd", seq, d, cw_o, dy1, 0, u1, 0, "mul",
                           x1=u1, c1=d, x2=u1, c2=2 * d, pre="mul", dep=sync("bwd_mix_o", dy1))
    du1 = _odd_du("odd_du", u1, dy1, co, dxc)
    dw_in_o = _mm_tn("mm_dw_in_o", n2, du1, BF16)
    dn2 = _mm_nt("mm_dn_o", du1, wo["w_in"], BF16)
    dh2, dh2b, d_mix_o = _rms_bwd("rms_bwd_mix1", h2, row(small["mix_norm_o"]), dn2, dh3)
    dep = put_g("mix_o", {"w_in": dw_in_o, "w_out": dw_out_o})

    dh1, dh1b, d_ffn0, dep = ffn_bwd("0", dh2, dh2b, h1, row(small["ffn_norm"][0]), n1, ds0, s0,
                                     act0, wf0, dep)
    dw_out_e = _mm_tn("mm_dw_out_e", z0, dh1b, BF16, dep=dep)
    dz0 = _mm_nt("mm_dz_e", dh1b, we["w_out"], BF16, dep=dw_out_e)
    da2, dws, dpm, vecs, dwp = _even_bwd("even_bwd", seq, dz0, a2, pm, row(small["ln_g_e"]),
                                         row(small["ln_b_e"]), wp, row(small["pool_scale_e"]),
                                         dep=sync("bwd_mix_e", dz0))
    dep = put_g("small", {"conv_b_e": vecs[2], "ln_g_e": vecs[0], "ln_b_e": vecs[1],
                          "w_pool_e": dwp, "pool_scale_e": vecs[3], "mix_norm_o": d_mix_o[0],
                          "conv_w_o": dcw_o, "ffn_norm": jnp.concatenate([d_ffn0, d_ffn1], axis=0),
                          "final_norm": d_final[0], "loss": lsum})
    da1, dcw_e = _conv_bwd("conv_e_bwd", seq, c, cw_e, da2, 0, x1=u0, c1=0, x2=u0, c2=c, pre="glu", dep=dep)
    dbp = _conv_bwd("pool_bwd", seq, c, ptaps, dws, 0, live=_pool_live(c))
    du0 = _even_du("even_du", u0, da1, dbp, dpm)
    dw_in_e = _mm_tn("mm_dw_in_e", n0, du0, BF16)
    dep = put_g("mix_e", {"w_in": dw_in_e, "w_out": dw_out_e})
    dn0 = _mm_nt("mm_dn_e", du0, we["w_in"], BF16, dep=dep)
    dx, _, d_mix_e = _rms_bwd("rms_bwd_mix0", x, row(small["mix_norm_e"]), dn0, dh1, bf16_too=False)
    return dx, {"conv_w_e": dcw_e, "mix_norm_e": d_mix_e[0]}


def _place():
    x, y, c = (lax.axis_index(a) for a in MESH_AXES)
    return x, y, c


def _index(p):
    return 4 * p[0] + 2 * p[1] + p[2]


def _slab(ref, kind, d, n):
    if kind == "blk":
        return ref.at[d]
    return ref.at[:, pl.ds(pl.multiple_of(d * n, LANES), n)]


HBM = pl.BlockSpec(memory_space=pltpu.HBM)
SEM = pl.BlockSpec(memory_space=pltpu.SEMAPHORE)
EFFECT = pltpu.SideEffectType.DATAFLOW_SIDE_EFFECTING
NCHIPS = 4


def _in_hbm(a):
    return pltpu.with_memory_space_constraint(a, pltpu.HBM)


def _gathered_shape(s, kind):
    m, n = s.shape
    return (NDEV, m, n) if kind == "blk" else (m, NDEV * n)


def _first_targets():
    x, y, c = _place()
    return [(x, y, 1 - c), (1 - x, y, c), (x, 1 - y, c), (1 - x, 1 - y, c)]


def _gather_start(name, shards, kinds, after):
    na = len(shards)

    def body(*refs):
        x_refs, land_refs = refs[:na], refs[na:2 * na]
        send_sems, recv_sems = refs[2 * na + 1], refs[2 * na + 2]
        token = refs[-1]
        me = _index(_place())
        for a in range(na):
            for k, to in enumerate(_first_targets()):
                pltpu.make_async_remote_copy(
                    src_ref=x_refs[a], dst_ref=_slab(land_refs[a], kinds[a], me, shards[a].shape[1]),
                    send_sem=send_sems.at[4 * a + k], recv_sem=recv_sems.at[4 * a + k],
                    device_id=to, device_id_type=MESH).start()
        token[...] = jnp.zeros_like(token)

    lands = [lax.empty(_gathered_shape(s, k), s.dtype) for s, k in zip(shards, kinds)]
    outs = pl.pallas_call(
        body, name=name,
        out_shape=(pltpu.SemaphoreType.DMA((4 * na,)), pltpu.SemaphoreType.DMA((4 * na,)),
                   *[pltpu.HBM(s.shape, s.dtype) for s in shards],
                   *[pltpu.HBM(l.shape, l.dtype) for l in lands],
                   jax.ShapeDtypeStruct((SUBLANES, LANES), F32)),
        in_specs=[HBM] * (2 * na) + [ANY],
        out_specs=(SEM, SEM, *[HBM] * (2 * na), pl.BlockSpec(memory_space=pltpu.VMEM)),
        input_output_aliases={i: 2 + i for i in range(2 * na)},
        compiler_params=pltpu.CompilerParams(has_side_effects=EFFECT),
    )(*[_in_hbm(s) for s in shards], *[_in_hbm(l) for l in lands], after)
    return outs[0], outs[1], outs[2:2 + na], outs[2 + na:2 + 2 * na], outs[-1]


def _gather_wait(name, started, kinds, after):
    send_sems, recv_sems, shards, lands, _ = started
    na = len(shards)

    def body(*refs):
        x_refs, land_refs = refs[:na], refs[na:2 * na]
        s_sems, r_sems = refs[2 * na], refs[2 * na + 1]
        for a in range(na):
            for k, frm in enumerate(_first_targets()):
                cp = pltpu.make_async_remote_copy(
                    src_ref=x_refs[a],
                    dst_ref=_slab(land_refs[a], kinds[a], _index(frm), shards[a].shape[1]),
                    send_sem=s_sems.at[4 * a + k], recv_sem=r_sems.at[4 * a + k],
                    device_id=frm, device_id_type=MESH)
                cp.wait_send()
                cp.wait_recv()

    outs = pl.pallas_call(
        body, name=name,
        out_shape=(*[pltpu.HBM(s.shape, s.dtype) for s in shards],
                   *[pltpu.HBM(l.shape, l.dtype) for l in lands]),
        in_specs=[HBM] * (2 * na) + [SEM, SEM, ANY], out_specs=[HBM] * (2 * na),
        input_output_aliases={i: i for i in range(2 * na)},
        compiler_params=pltpu.CompilerParams(has_side_effects=EFFECT),
    )(*shards, *lands, send_sems, recv_sems, after)
    return outs[:na], outs[na:]


def _split_start(name, bufs, ncopies, plan, after):
    nb = len(bufs)

    def body(*refs):
        send_sems, recv_sems, token = refs[nb + 1], refs[nb + 2], refs[-1]
        for k, (src, dst, to, _) in enumerate(plan(refs[:nb])):
            pltpu.make_async_remote_copy(src_ref=src, dst_ref=dst, send_sem=send_sems.at[k],
                                         recv_sem=recv_sems.at[k], device_id=to, device_id_type=MESH).start()
        token[...] = jnp.zeros_like(token)

    outs = pl.pallas_call(
        body, name=name,
        out_shape=(pltpu.SemaphoreType.DMA((ncopies,)), pltpu.SemaphoreType.DMA((ncopies,)),
                   *[pltpu.HBM(b.shape, b.dtype) for b in bufs],
                   jax.ShapeDtypeStruct((SUBLANES, LANES), F32)),
        in_specs=[HBM] * nb + [ANY],
        out_specs=(SEM, SEM, *[HBM] * nb, pl.BlockSpec(memory_space=pltpu.VMEM)),
        input_output_aliases={i: 2 + i for i in range(nb)},
        compiler_params=pltpu.CompilerParams(has_side_effects=EFFECT),
    )(*[_in_hbm(b) for b in bufs], after)
    return outs[0], outs[1], list(outs[2:2 + nb]), outs[-1]


def _split_wait(name, started, plan, after):
    send_sems, recv_sems, bufs, _ = started
    nb = len(bufs)

    def body(*refs):
        s_sems, r_sems = refs[nb], refs[nb + 1]
        for k, (src, _, to, landed) in enumerate(plan(refs[:nb])):
            cp = pltpu.make_async_remote_copy(src_ref=src, dst_ref=landed, send_sem=s_sems.at[k],
                                              recv_sem=r_sems.at[k], device_id=to, device_id_type=MESH)
            cp.wait_send()
            cp.wait_recv()

    outs = pl.pallas_call(
        body, name=name, out_shape=tuple(pltpu.HBM(b.shape, b.dtype) for b in bufs),
        in_specs=[HBM] * nb + [SEM, SEM, ANY], out_specs=[HBM] * nb,
        input_output_aliases={i: i for i in range(nb)},
        compiler_params=pltpu.CompilerParams(has_side_effects=EFFECT),
    )(*bufs, send_sems, recv_sems, after)
    return list(outs)


def _forward_plan(kinds, nloc):
    def plan(lands):
        x, y, c = _place()
        out = []
        for a, land in enumerate(lands):
            for chip in [(1 - x, y), (x, 1 - y), (1 - x, 1 - y)]:
                mine = _slab(land, kinds[a], _index((*chip, c)), nloc[a])
                out.append((mine, mine, (x, y, 1 - c), _slab(land, kinds[a], _index((*chip, 1 - c)), nloc[a])))
        return out
    return plan


def _own_copy(name, shard, land, kind, me):
    m, n = shard.shape
    tr = _tile(m, max(SUBLANES, 1048576 // n), SUBLANES)

    def body(s_ref, x_ref, land_ref, o_ref):
        o_ref[...] = x_ref[...]

    if kind == "blk":
        o_spec = pl.BlockSpec((None, tr, n), lambda i, s: (s[0], i, 0))
    else:
        o_spec = pl.BlockSpec((tr, n), lambda i, s: (i, s[0]))
    return pl.pallas_call(
        body, name=name,
        grid_spec=pltpu.PrefetchScalarGridSpec(
            num_scalar_prefetch=1, grid=(m // tr,),
            in_specs=[pl.BlockSpec((tr, n), lambda i, s: (i, 0)), ANY], out_specs=o_spec),
        out_shape=jax.ShapeDtypeStruct(land.shape, land.dtype),
        input_output_aliases={2: 0}, compiler_params=_cp("parallel"))(me, shard, land)


def _everyone_plan(refs):
    x, y, c = _place()
    out = []
    for dx, dy, dc in [(a, b, e) for a in (0, 1) for b in (0, 1) for e in (0, 1)][1:]:
        peer = (x ^ dx, y ^ dy, c ^ dc)
        out.append((refs[0], refs[1].at[_index((x, y, c))], peer, refs[1].at[_index(peer)]))
    return out


def _pair_plan(kinds, nloc):
    na = len(kinds)

    def plan(refs):
        x, y, c = _place()
        out = []
        for a in range(na):
            for j in range(NCHIPS):
                dst = refs[na + a].at[j]
                out.append((_slab(refs[a], kinds[a], 2 * j + (1 - c), nloc[a]), dst, (x, y, 1 - c), dst))
        return out
    return plan


def _chip_sum(name, full, kind, n, from_sib, place):
    _, m, _ = from_sib.shape
    tr = _tile(m, max(SUBLANES, 1048576 // n), SUBLANES)

    def body(s_ref, mine_ref, sib_ref, csum_ref, land_ref):
        v = (mine_ref[...].astype(F32) + sib_ref[...].astype(F32)).astype(csum_ref.dtype)
        csum_ref[...] = v

        @pl.when(pl.program_id(1) == s_ref[1])
        def _():
            land_ref[...] = v

    if kind == "blk":
        mine_spec = pl.BlockSpec((None, tr, n), lambda i, j, s: (2 * j + s[0], i, 0))
    else:
        mine_spec = pl.BlockSpec((tr, n), lambda i, j, s: (i, 2 * j + s[0]))
    slot = pl.BlockSpec((None, tr, n), lambda i, j, s: (j, i, 0))
    shp = jax.ShapeDtypeStruct((NCHIPS, m, n), from_sib.dtype)
    return pl.pallas_call(
        body, name=name,
        grid_spec=pltpu.PrefetchScalarGridSpec(
            num_scalar_prefetch=1, grid=(m // tr, NCHIPS), in_specs=[mine_spec, slot],
            out_specs=[slot, pl.BlockSpec((None, tr, n), lambda i, j, s: (s[1], i, 0))]),
        out_shape=[shp, shp], compiler_params=_cp("parallel", "arbitrary"))(place, full, from_sib)


def _other_chips():
    x, y, c = _place()
    return [(1 - x, y, c), (x, 1 - y, c), (1 - x, 1 - y, c)]


def _scatter_start(name, csums, lands, after):
    na = len(csums)

    def body(*refs):
        c_refs, land_refs = refs[:na], refs[na:2 * na]
        send_sems, recv_sems = refs[2 * na + 1], refs[2 * na + 2]
        token = refs[-1]
        x, y, _ = _place()
        for a in range(na):
            for k, to in enumerate(_other_chips()):
                pltpu.make_async_remote_copy(
                    src_ref=c_refs[a].at[2 * to[0] + to[1]], dst_ref=land_refs[a].at[2 * x + y],
                    send_sem=send_sems.at[3 * a + k], recv_sem=recv_sems.at[3 * a + k],
                    device_id=to, device_id_type=MESH).start()
        token[...] = jnp.zeros_like(token)

    outs = pl.pallas_call(
        body, name=name,
        out_shape=(pltpu.SemaphoreType.DMA((3 * na,)), pltpu.SemaphoreType.DMA((3 * na,)),
                   *[pltpu.HBM(s.shape, s.dtype) for s in csums],
                   *[pltpu.HBM(l.shape, l.dtype) for l in lands],
                   jax.ShapeDtypeStruct((SUBLANES, LANES), F32)),
        in_specs=[HBM] * (2 * na) + [ANY],
        out_specs=(SEM, SEM, *[HBM] * (2 * na), pl.BlockSpec(memory_space=pltpu.VMEM)),
        input_output_aliases={i: 2 + i for i in range(2 * na)},
        compiler_params=pltpu.CompilerParams(has_side_effects=EFFECT),
    )(*[_in_hbm(s) for s in csums], *[_in_hbm(l) for l in lands], after)
    return outs[0], outs[1], outs[2:2 + na], outs[2 + na:2 + 2 * na], outs[-1]


def _scatter_wait(name, started, after):
    send_sems, recv_sems, csums, lands, _ = started
    na = len(csums)

    def body(*refs):
        c_refs, land_refs = refs[:na], refs[na:2 * na]
        s_sems, r_sems = refs[2 * na], refs[2 * na + 1]
        for a in range(na):
            for k, frm in enumerate(_other_chips()):
                cp = pltpu.make_async_remote_copy(
                    src_ref=c_refs[a].at[2 * frm[0] + frm[1]], dst_ref=land_refs[a].at[2 * frm[0] + frm[1]],
                    send_sem=s_sems.at[3 * a + k], recv_sem=r_sems.at[3 * a + k],
                    device_id=frm, device_id_type=MESH)
                cp.wait_send()
                cp.wait_recv()

    outs = pl.pallas_call(
        body, name=name,
        out_shape=(*[pltpu.HBM(s.shape, s.dtype) for s in csums],
                   *[pltpu.HBM(l.shape, l.dtype) for l in lands]),
        in_specs=[HBM] * (2 * na) + [SEM, SEM, ANY], out_specs=[HBM] * (2 * na),
        input_output_aliases={i: i for i in range(2 * na)},
        compiler_params=pltpu.CompilerParams(has_side_effects=EFFECT),
    )(*csums, *lands, send_sems, recv_sems, after)
    return outs[na:]


def _adam_math(w, g, m, v):
    m = ADAM_B1 * m + (1.0 - ADAM_B1) * g
    v = ADAM_B2 * v + (1.0 - ADAM_B2) * (g * g)
    m_hat = m / (1.0 - ADAM_B1 ** ADAM_STEP)
    v_hat = v / (1.0 - ADAM_B2 ** ADAM_STEP)
    delta = -ADAM_LR * (m_hat / (jnp.sqrt(v_hat) + ADAM_EPS) + ADAM_WD * w)
    return delta, m, v


def _sum_adamw(name, parts, w, m, v, layer, prev=None, dep=None):
    nl, r, c = w.shape
    nparts = parts.shape[0]
    tr = _tile(r, max(SUBLANES, 360448 // c), SUBLANES)

    def body(p_ref, w_ref, m_ref, v_ref, *rest):
        g_ref, d_ref, mo_ref, vo_ref = rest[-4:]
        g = p_ref[0].astype(F32)
        for s in range(1, nparts):
            g = g + p_ref[s].astype(F32)
        delta, mn, vn = _adam_math(w_ref[...], g, m_ref[...], v_ref[...])
        g_ref[...] = g
        d_ref[...] = delta
        mo_ref[...] = mn
        vo_ref[...] = vn

    row = pl.BlockSpec((None, tr, c), lambda i: (layer, i, 0))
    shp = jax.ShapeDtypeStruct((nl, r, c), F32)
    extra = ([] if prev is None else list(prev)) + ([] if dep is None else [dep])
    return pl.pallas_call(
        body, name=name, grid=(r // tr,),
        in_specs=[pl.BlockSpec((nparts, tr, c), lambda i: (0, i, 0)), row, row, row] + [ANY] * len(extra),
        out_specs=[row, row, row, row], out_shape=[shp, shp, shp, shp],
        input_output_aliases={} if prev is None else {4 + i: i for i in range(4)},
        compiler_params=_cp("parallel"))(parts, w, m, v, *extra)


def _sum_parts(name, parts):
    _, r, c = parts.shape

    def body(p_ref, o_ref):
        g = p_ref[0]
        for s in range(1, NDEV):
            g = g + p_ref[s]
        o_ref[...] = g

    return pl.pallas_call(
        body, name=name, grid=(1,),
        in_specs=[pl.BlockSpec((NDEV, r, c), lambda i: (0, 0, 0))],
        out_specs=pl.BlockSpec((r, c), lambda i: (0, 0)),
        out_shape=jax.ShapeDtypeStruct((r, c), F32), compiler_params=_cp("arbitrary"))(parts)


def _adamw(name, w, g, m, v):
    r, c = w.shape

    def body(w_ref, g_ref, m_ref, v_ref, d_ref, mo_ref, vo_ref):
        delta, mn, vn = _adam_math(w_ref[...], g_ref[...], m_ref[...], v_ref[...])
        d_ref[...] = delta
        mo_ref[...] = mn
        vo_ref[...] = vn

    full = pl.BlockSpec((r, c), lambda i: (0, 0))
    shp = jax.ShapeDtypeStruct((r, c), F32)
    return pl.pallas_call(
        body, name=name, grid=(1,), in_specs=[full] * 4, out_specs=[full] * 3,
        out_shape=[shp] * 3, compiler_params=_cp("arbitrary"))(w, g, m, v)


def _pack(arrays):
    flat = jnp.concatenate([a.reshape(-1) for a in arrays])
    unit = SUBLANES * LANES
    pad = (-flat.shape[0]) % unit
    return jnp.pad(flat, (0, pad)).reshape(-1, LANES)


def _unpack(buf, shapes):
    flat = buf.reshape(-1)
    out, off = [], 0
    for shp in shapes:
        size = 1
        for s in shp:
            size *= s
        out.append(flat[off:off + size].reshape(shp))
        off += size
    return out


WEIGHTS = ["mix_norm_e", "w_in_e", "conv_w_e", "conv_b_e", "ln_g_e", "ln_b_e", "w_pool_e",
           "pool_scale_e", "w_out_e", "mix_norm_o", "w_in_o", "conv_w_o", "w_out_o", "ffn_norm",
           "w_gate", "w_up", "w_down", "final_norm"]
BIG = ["w_in_e", "w_out_e", "w_in_o", "w_out_o", "w_gate", "w_up", "w_down"]
SHARDED_SMALL = {"conv_w_e": 1, "w_pool_e": 1, "mix_norm_o": 0, "conv_w_o": 1}
SMALL = [n for n in WEIGHTS if n not in BIG]


def kernel(x, mix_norm_e, w_in_e, conv_w_e, conv_b_e, ln_g_e, ln_b_e, w_pool_e, pool_scale_e, w_out_e, mix_norm_o, w_in_o, conv_w_o, w_out_o, ffn_norm, w_gate, w_up, w_down, final_norm, loss_target, m_mix_norm_e, m_w_in_e, m_conv_w_e, m_conv_b_e, m_ln_g_e, m_ln_b_e, m_w_pool_e, m_pool_scale_e, m_w_out_e, m_mix_norm_o, m_w_in_o, m_conv_w_o, m_w_out_o, m_ffn_norm, m_w_gate, m_w_up, m_w_down, m_final_norm, v_mix_norm_e, v_w_in_e, v_conv_w_e, v_conv_b_e, v_ln_g_e, v_ln_b_e, v_w_pool_e, v_pool_scale_e, v_w_out_e, v_mix_norm_o, v_w_in_o, v_conv_w_o, v_w_out_o, v_ffn_norm, v_w_gate, v_w_up, v_w_down, v_final_norm):
    wts = dict(zip(WEIGHTS, [mix_norm_e, w_in_e, conv_w_e, conv_b_e, ln_g_e, ln_b_e, w_pool_e, pool_scale_e, w_out_e, mix_norm_o, w_in_o, conv_w_o, w_out_o, ffn_norm, w_gate, w_up, w_down, final_norm]))
    mom = dict(zip(WEIGHTS, [m_mix_norm_e, m_w_in_e, m_conv_w_e, m_conv_b_e, m_ln_g_e, m_ln_b_e, m_w_pool_e, m_pool_scale_e, m_w_out_e, m_mix_norm_o, m_w_in_o, m_conv_w_o, m_w_out_o, m_ffn_norm, m_w_gate, m_w_up, m_w_down, m_final_norm]))
    var = dict(zip(WEIGHTS, [v_mix_norm_e, v_w_in_e, v_conv_w_e, v_conv_b_e, v_ln_g_e, v_ln_b_e, v_w_pool_e, v_pool_scale_e, v_w_out_e, v_mix_norm_o, v_w_in_o, v_conv_w_o, v_w_out_o, v_ffn_norm, v_w_gate, v_w_up, v_w_down, v_final_norm]))
    bsz, seq, d = x.shape
    t = bsz * seq
    me = _index(_place())
    me_arr = jnp.reshape(me, (1,)).astype(jnp.int32)

    sh_names = list(SHARDED_SMALL)
    sh_local = [wts[n][0] for n in sh_names]
    packed = _pack(sh_local)
    params_st = _split_start("small_params_start", [packed, lax.empty((NDEV,) + packed.shape, F32)], NDEV - 1,
                             _everyone_plan, x)

    for state in (wts, mom, var):
        for n in ("w_gate", "w_up"):
            state[n] = jnp.swapaxes(state[n], 1, 2)
    bf = lambda a: a.astype(BF16)
    mix_kinds, ffn_kinds = ["col", "blk"], ["blk", "blk", "blk"]
    ffn_names = ("w_gate", "w_up", "w_down")
    groups = {
        "mix_e": ([w_in_e.shape[2], d], mix_kinds, [("w_in_e", 0), ("w_out_e", 0)]),
        "ffn0": ([d, d, d], ffn_kinds, [(n, 0) for n in ffn_names]),
        "mix_o": ([w_in_o.shape[2], d], mix_kinds, [("w_in_o", 0), ("w_out_o", 0)]),
        "ffn1": ([d, d, d], ffn_kinds, [(n, 1) for n in ffn_names]),
    }
    gathers = {
        "in_e": ([bf(w_in_e[0])], ["col"]), "out_e": ([bf(w_out_e[0])], ["blk"]),
        "gu0": ([bf(wts["w_gate"][0]), bf(wts["w_up"][0])], ["blk", "blk"]), "down0": ([bf(w_down[0])], ["blk"]),
        "in_o": ([bf(w_in_o[0])], ["col"]), "out_o": ([bf(w_out_o[0])], ["blk"]),
        "gu1": ([bf(wts["w_gate"][1]), bf(wts["w_up"][1])], ["blk", "blk"]), "down1": ([bf(w_down[1])], ["blk"]),
    }
    started, prev = {}, params_st[3]
    for grp, (shards, kinds) in gathers.items():
        started[grp] = _gather_start("gather_start_" + grp, shards, kinds, prev)
        prev = started[grp][4]
    all_started = prev[0, 0:1]

    bufs = _split_wait("small_params_wait", params_st, _everyone_plan, prev)
    gathered = _own_copy("small_params_own", bufs[0], bufs[1], "blk", me_arr)
    small = {n: wts[n][0] for n in SMALL if n not in SHARDED_SMALL and n not in ("ffn_norm", "final_norm")}
    small["ffn_norm"], small["final_norm"] = ffn_norm, final_norm
    flat, off = gathered.reshape(NDEV, -1), 0
    for n, a in zip(sh_names, sh_local):
        ax, shp = SHARDED_SMALL[n], a.shape
        blocks = jnp.moveaxis(flat[:, off:off + a.size].reshape((NDEV,) + shp), 0, ax)
        small[n] = blocks.reshape(shp[:ax] + (NDEV * shp[ax],) + shp[ax + 1:])
        off += a.size

    passing, shards_of = {}, {}

    def pass_on(grp, after):
        shards, kinds = gathers[grp]
        shards_of[grp], lands = _gather_wait("gather_wait_" + grp, started[grp], kinds, after)
        plan = _forward_plan(kinds, [s.shape[1] for s in shards])
        passing[grp] = (_split_start("forward_start_" + grp, lands, 3 * len(lands), plan, after), plan)
        return passing[grp][0][3]

    def get_w(grp, after):
        if grp not in passing:
            after = pass_on(grp, after)
        st, plan = passing[grp]
        lands = _split_wait("forward_wait_" + grp, st, plan, after)
        full = [_own_copy("own_copy_%s%d" % (grp, a), shards_of[grp][a], lands[a], gathers[grp][1][a], me_arr)
                for a in range(len(lands))]
        return [f.reshape(-1, d) if kind == "blk" else f for f, kind in zip(full, gathers[grp][1])]

    cx, cy, cc = _place()
    place = jnp.stack([cc, 2 * cx + cy]).astype(jnp.int32)
    bwd_order = ["ffn1", "mix_o", "ffn0", "mix_e"]
    pairing, pending, results = {}, {}, {}

    late_names = ["conv_w_e", "mix_norm_e"]
    early_names = [n for n in SMALL if n not in late_names] + ["loss"]
    small_sent = {}

    def send_small(tag, arrays, after):
        mine = _pack(arrays)
        small_sent[tag] = _split_start(tag + "_start", [mine, lax.empty((NDEV,) + mine.shape, F32)], NDEV - 1,
                                       _everyone_plan, after)
        return small_sent[tag][3]

    def summed_small(tag, shapes, after):
        bufs = _split_wait(tag + "_wait", small_sent[tag], _everyone_plan, after)
        parts = _own_copy(tag + "_own", bufs[0], bufs[1], "blk", me_arr)
        return _unpack(_sum_parts(tag + "_sum", parts), shapes)

    def put_g(grp, grads):
        if grp == "small":
            small_sent["shapes"] = [grads[n].shape for n in early_names]
            return send_small("small_grads", [grads[n] for n in early_names], place)
        nloc, kinds, _ = groups[grp]
        if len(kinds) == 2:
            fulls = [grads["w_in"], grads["w_out"].reshape(NDEV, -1, d)]
        else:
            fulls = [grads[n].reshape(NDEV, -1, d) for n in ffn_names]
        empties = []
        for g, kind, n in zip(fulls, kinds, nloc):
            empties.append(lax.empty((NCHIPS, g.shape[1] if kind == "blk" else g.shape[0], n), g.dtype))
        plan = _pair_plan(kinds, nloc)
        pairing[grp] = (_split_start("pair_start_" + grp, fulls + empties, NCHIPS * len(fulls), plan, place),
                        plan, kinds, nloc)
        token = pairing[grp][0][3]
        return send_sums(grp, token) if grp == bwd_order[-1] else token

    def send_sums(grp, after):
        st, plan, kinds, nloc = pairing[grp]
        bufs = _split_wait("pair_wait_" + grp, st, plan, after)
        na = len(kinds)
        sums = [_chip_sum("chip_sum_%s%d" % (grp, a), bufs[a], kinds[a], nloc[a], bufs[na + a], place)
                for a in range(na)]
        pending[grp] = _scatter_start("scatter_start_" + grp, [s[0] for s in sums], [s[1] for s in sums], after)
        return pending[grp][4]

    def finish(grp, after):
        lands = _scatter_wait("scatter_wait_" + grp, pending[grp], after)
        dep = None
        for (n, l), parts in zip(groups[grp][2], lands):
            results[n] = _sum_adamw("adamw_%s%d" % (n, l), parts, wts[n], mom[n], var[n], l, results.get(n), dep)
            dep = results[n][1]
        return dep

    fwd_sync = {"fwd_a": ["out_e"], "fwd_b": ["gu0"], "fwd_c": ["down0", "in_o"], "fwd_d": ["out_o"],
                "fwd_e": ["gu1"], "fwd_f": ["down1"]}

    def sync(tag, after):
        if tag in fwd_sync:
            for grp in fwd_sync[tag]:
                after = pass_on(grp, after)
            return after
        if tag == "bwd_mix_o":
            return send_sums("ffn1", after)
        if tag == "bwd_ffn0":
            return finish("ffn1", send_sums("mix_o", after))
        if tag == "bwd_mix_e":
            return finish("mix_o", send_sums("ffn0", after))
        return None

    small["mix_norm_e"] = small["mix_norm_e"] + all_started
    dx, late = _local_step(x.reshape(t, d), loss_target.reshape(t, d), seq, small, get_w, put_g, sync)

    out_g, out_d, out_m, out_v = {}, {}, {}, {}

    dep = finish("ffn0", send_small("last_grads", [late[n] for n in late_names], dx))
    sums = dict(zip(early_names, summed_small("small_grads", small_sent["shapes"], dep)))
    sums.update(zip(late_names, summed_small("last_grads", [late[n].shape for n in late_names], dep)))
    loss = jnp.sum(sums["loss"])
    gs_sum = [sums[n] for n in SMALL]
    local_g = []
    for n, g in zip(SMALL, gs_sum):
        if n in SHARDED_SMALL:
            ax = SHARDED_SMALL[n]
            size = wts[n].shape[ax + 1]
            g = lax.dynamic_slice_in_dim(g, me * size, size, axis=ax)
        local_g.append(g.reshape(wts[n].shape))
    shapes = [wts[n].shape for n in SMALL]
    upd = _adamw("adamw_small", _pack([wts[n] for n in SMALL]), _pack(local_g),
                 _pack([mom[n] for n in SMALL]), _pack([var[n] for n in SMALL]))
    for i, outd in enumerate((out_d, out_m, out_v)):
        for n, a in zip(SMALL, _unpack(upd[i], shapes)):
            outd[n] = a
    for n, g in zip(SMALL, local_g):
        out_g[n] = g

    finish("mix_e", upd[0])
    for n in BIG:
        res = [jnp.swapaxes(a, 1, 2) for a in results[n]] if n in ("w_gate", "w_up") else results[n]
        out_g[n], out_d[n], out_m[n], out_v[n] = res

    return (loss, dx.reshape(bsz, seq, d), *[out_g[n] for n in WEIGHTS], *[out_d[n] for n in WEIGHTS],
            *[out_m[n] for n in WEIGHTS], *[out_v[n] for n in WEIGHTS])
```

```python
import jax
import jax.numpy as jnp
from jax import lax
from jax.experimental import pallas as pl
from jax.experimental.pallas import tpu as pltpu

F32 = jnp.float32
BF16 = jnp.bfloat16
NDEV = 8
MESH_AXES = ("x", "y", "c")
EPS = 1e-6
POOL_WINDOWS = (2, 4, 8, 16)
ADAM_LR = 0.001
ADAM_B1 = 0.9
ADAM_B2 = 0.999
ADAM_EPS = 1e-08
ADAM_WD = 0.01
ADAM_STEP = 10
LANES = 128
SUBLANES = 8
VMEM_LIMIT = 56 * 1024 * 1024
MXU_DEPTH = 256
MM_TK = 2816
MESH = pl.DeviceIdType.MESH
ANY = pl.BlockSpec(memory_space=pl.ANY)


def _cp(*sem):
    return pltpu.CompilerParams(dimension_semantics=sem, vmem_limit_bytes=VMEM_LIMIT)


def _tile(n, pref, unit=LANES):
    if n <= pref:
        return n
    t = (pref // unit) * unit
    while t > unit and n % t:
        t -= unit
    assert n % t == 0, (n, pref)
    return t


def _sigmoid(v):
    return 0.5 * jnp.tanh(0.5 * v) + 0.5


def _mm(name, pairs, a_specs, b_specs, dims, out_shape, o_spec, grid, acc_shape,
        res=None, res_spec=None, dep=None):
    np_ = len(pairs)
    nk = grid[2]
    has_res = res is not None
    n_in = 2 * np_ + (1 if has_res else 0) + (0 if dep is None else 1)

    def body(*refs):
        a_refs = refs[:np_]
        b_refs = refs[np_:2 * np_]
        r_ref = refs[2 * np_] if has_res else None
        o_ref = refs[n_in]
        acc = refs[-1]

        def part():
            s = None
            for a_ref, b_ref in zip(a_refs, b_refs):
                d = lax.dot_general(a_ref[...], b_ref[...], dims, preferred_element_type=F32)
                s = d if s is None else s + d
            return s

        def finish(v):
            if has_res:
                v = v + r_ref[...]
            o_ref[...] = v.astype(o_ref.dtype)

        if nk == 1:
            finish(part())
        else:
            k = pl.program_id(2)

            @pl.when(k == 0)
            def _():
                acc[...] = part()

            @pl.when((k > 0) & (k < nk - 1))
            def _():
                acc[...] += part()

            @pl.when(k == nk - 1)
            def _():
                finish(acc[...] + part())

    ins = [p[0] for p in pairs] + [p[1] for p in pairs]
    specs = list(a_specs) + list(b_specs)
    if has_res:
        ins.append(res)
        specs.append(res_spec)
    if dep is not None:
        ins.append(dep)
        specs.append(ANY)
    return pl.pallas_call(
        body, name=name, grid=grid, in_specs=specs, out_specs=o_spec, out_shape=out_shape,
        scratch_shapes=[pltpu.VMEM(acc_shape if nk > 1 else (SUBLANES, LANES), F32)],
        compiler_params=_cp("parallel", "parallel", "arbitrary"))(*ins)


NN = (((1,), (0,)), ((), ()))
NT = (((1,), (1,)), ((), ()))
TN = (((0,), (0,)), ((), ()))


def _tiles_mk(m, kk):
    return _tile(m, 1024), _tile(kk, MM_TK, MXU_DEPTH)


def _mm_nn(name, a, b, out_dtype, res=None, dep=None):
    pairs = list(zip(a, b)) if isinstance(a, (list, tuple)) else [(a, b)]
    m, kk = pairs[0][0].shape
    n = pairs[0][1].shape[1]
    tm, tk = _tiles_mk(m, kk)
    tn = _tile(n, 1024 if tk * len(pairs) <= MM_TK else 512)
    return _mm(name, pairs,
               [pl.BlockSpec((tm, tk), lambda i, j, k: (i, k))] * len(pairs),
               [pl.BlockSpec((tk, tn), lambda i, j, k: (k, j))] * len(pairs), NN,
               jax.ShapeDtypeStruct((m, n), out_dtype),
               pl.BlockSpec((tm, tn), lambda i, j, k: (i, j)),
               (m // tm, n // tn, kk // tk), (tm, tn), res,
               pl.BlockSpec((tm, tn), lambda i, j, k: (i, j)), dep=dep)


def _mm_nt(name, a, b, out_dtype, dep=None):
    m, n = a.shape
    kk = b.shape[0]
    tn = _tile(kk, 1024)
    tm, tk = _tiles_mk(m, n)
    return _mm(name, [(a, b)],
               [pl.BlockSpec((tm, tk), lambda i, j, k: (i, k))],
               [pl.BlockSpec((tn, tk), lambda i, j, k: (j, k))], NT,
               jax.ShapeDtypeStruct((m, kk), out_dtype),
               pl.BlockSpec((tm, tn), lambda i, j, k: (i, j)),
               (m // tm, kk // tn, n // tk), (tm, tn), dep=dep)


def _mm_tn(name, a, b, out_dtype, dep=None):
    t, m = a.shape
    n = b.shape[1]
    tn = _tile(n, 1024)
    tm, tk = _tile(m, 1408), _tile(t, MM_TK, MXU_DEPTH)
    return _mm(name, [(a, b)],
               [pl.BlockSpec((tk, tm), lambda i, j, k: (k, i))],
               [pl.BlockSpec((tk, tn), lambda i, j, k: (k, j))], TN,
               jax.ShapeDtypeStruct((m, n), out_dtype),
               pl.BlockSpec((tm, tn), lambda i, j, k: (i, j)),
               (m // tm, n // tn, t // tk), (tm, tn), dep=dep)


def _ffn_fwd(name, n, wg, wu, dep=None):
    f, d = wg.shape
    t = n.shape[0]
    tm, tn = _tile(t, 1024), _tile(f, 512)

    def body(n_ref, wg_ref, wu_ref, *rest):
        act_ref, ds_ref, s_ref = rest[-3:]
        nv = n_ref[...]
        g = lax.dot_general(nv, wg_ref[...], NT, preferred_element_type=F32)
        up = lax.dot_general(nv, wu_ref[...], NT, preferred_element_type=F32)
        sg = _sigmoid(g)
        silu = g * sg
        act_ref[...] = (silu * up).astype(BF16)
        ds_ref[...] = (up * (sg * (1.0 + g * (1.0 - sg)))).astype(BF16)
        s_ref[...] = silu.astype(BF16)

    w_spec = pl.BlockSpec((tn, d), lambda j, i: (j, 0))
    o_spec = pl.BlockSpec((tm, tn), lambda j, i: (i, j))
    shp = jax.ShapeDtypeStruct((t, f), BF16)
    return pl.pallas_call(
        body, name=name, grid=(f // tn, t // tm),
        in_specs=[pl.BlockSpec((tm, d), lambda j, i: (i, 0)), w_spec, w_spec] + ([] if dep is None else [ANY]),
        out_specs=[o_spec, o_spec, o_spec], out_shape=[shp, shp, shp],
        compiler_params=_cp("parallel", "parallel"))(n, wg, wu, *([] if dep is None else [dep]))


def _ffn_bwd_act(name, dh, wd, dsilu, silu, dep=None):
    f, d = wd.shape
    t = dh.shape[0]
    tm, tn = _tile(t, 1024), _tile(f, 512)

    def body(dh_ref, wd_ref, ds_ref, s_ref, *rest):
        dg_ref, dup_ref = rest[-2:]
        da = lax.dot_general(dh_ref[...], wd_ref[...], NT, preferred_element_type=F32)
        dg_ref[...] = (da * ds_ref[...].astype(F32)).astype(BF16)
        dup_ref[...] = (da * s_ref[...].astype(F32)).astype(BF16)

    o_spec = pl.BlockSpec((tm, tn), lambda i, j: (i, j))
    shp = jax.ShapeDtypeStruct((t, f), BF16)
    return pl.pallas_call(
        body, name=name, grid=(t // tm, f // tn),
        in_specs=[pl.BlockSpec((tm, d), lambda i, j: (i, 0)),
                  pl.BlockSpec((tn, d), lambda i, j: (j, 0)), o_spec, o_spec]
        + ([] if dep is None else [ANY]),
        out_specs=[o_spec, o_spec], out_shape=[shp, shp],
        compiler_params=_cp("parallel", "parallel"))(dh, wd, dsilu, silu, *([] if dep is None else [dep]))


def _rms_fwd(name, h, gain):
    t, d = h.shape
    tr = _tile(t, 512, SUBLANES)

    def body(h_ref, g_ref, n_ref):
        hv = h_ref[...]
        r = lax.rsqrt(jnp.mean(hv * hv, axis=-1, keepdims=True) + EPS)
        n_ref[...] = (hv * r * g_ref[...]).astype(BF16)

    return pl.pallas_call(
        body, name=name, grid=(t // tr,),
        in_specs=[pl.BlockSpec((tr, d), lambda i: (i, 0)), pl.BlockSpec((1, d), lambda i: (0, 0))],
        out_specs=pl.BlockSpec((tr, d), lambda i: (i, 0)),
        out_shape=jax.ShapeDtypeStruct((t, d), BF16),
        compiler_params=_cp("parallel"))(h, gain)


def _rms_bwd_math(hv, gain, dn):
    d = hv.shape[-1]
    r = lax.rsqrt(jnp.mean(hv * hv, axis=-1, keepdims=True) + EPS)
    xhat = hv * r
    dxh = dn * gain
    dh = r * (dxh - xhat * (jnp.sum(dxh * xhat, axis=-1, keepdims=True) / d))
    dgain = jnp.sum(dn * xhat, axis=0, keepdims=True)
    return dh, dgain


def _rms_bwd(name, h, gain, dn, dres, bf16_too=True):
    t, d = h.shape
    tr = _tile(t, 256, SUBLANES)

    def body(h_ref, g_ref, dn_ref, dr_ref, dh_ref, *rest):
        dg_ref = rest[-1]
        dh, dgain = _rms_bwd_math(h_ref[...], g_ref[...], dn_ref[...].astype(F32))
        dh = dh + dr_ref[...]
        dh_ref[...] = dh
        if bf16_too:
            rest[0][...] = dh.astype(BF16)

        @pl.when(pl.program_id(0) == 0)
        def _():
            dg_ref[...] = dgain

        @pl.when(pl.program_id(0) > 0)
        def _():
            dg_ref[...] += dgain

    row = pl.BlockSpec((tr, d), lambda i: (i, 0))
    vec = pl.BlockSpec((1, d), lambda i: (0, 0))
    halves = [jax.ShapeDtypeStruct((t, d), BF16)] if bf16_too else []
    out = pl.pallas_call(
        body, name=name, grid=(t // tr,), in_specs=[row, vec, row, row],
        out_specs=[row] + [row] * len(halves) + [vec],
        out_shape=[jax.ShapeDtypeStruct((t, d), F32)] + halves + [jax.ShapeDtypeStruct((1, d), F32)],
        compiler_params=_cp("arbitrary"))(h, gain, dn, dres)
    return (out[0], out[1], out[2]) if bf16_too else (out[0], None, out[1])


def _loss_head(name, h, gain, tgt):
    t, d = h.shape
    tr = _tile(t, 256, SUBLANES)

    def body(h_ref, g_ref, t_ref, dh_ref, dhb_ref, dg_ref, ls_ref):
        hv = h_ref[...]
        gv = g_ref[...]
        r = lax.rsqrt(jnp.mean(hv * hv, axis=-1, keepdims=True) + EPS)
        err = hv * r * gv - t_ref[...]
        lsum = 0.5 * jnp.sum(err * err, axis=0, keepdims=True) / d
        dh, dgain = _rms_bwd_math(hv, gv, err / d)
        dh_ref[...] = dh
        dhb_ref[...] = dh.astype(BF16)

        @pl.when(pl.program_id(0) == 0)
        def _():
            dg_ref[...] = dgain
            ls_ref[...] = lsum

        @pl.when(pl.program_id(0) > 0)
        def _():
            dg_ref[...] += dgain
            ls_ref[...] += lsum

    row = pl.BlockSpec((tr, d), lambda i: (i, 0))
    vec = pl.BlockSpec((1, d), lambda i: (0, 0))
    return pl.pallas_call(
        body, name=name, grid=(t // tr,), in_specs=[row, vec, row],
        out_specs=[row, row, vec, vec],
        out_shape=[jax.ShapeDtypeStruct((t, d), F32), jax.ShapeDtypeStruct((t, d), BF16),
                   jax.ShapeDtypeStruct((1, d), F32), jax.ShapeDtypeStruct((1, d), F32)],
        compiler_params=_cp("arbitrary"))(h, gain, tgt)


def _conv_geom(t, seq, c, k, full_width=False):
    halo = 32 if k - 1 > SUBLANES else SUBLANES
    assert k - 1 <= halo
    tm = min(256 if halo > SUBLANES else 1024, seq // 2)
    tc = c if full_width else min(512, c)
    assert seq % tm == 0 and tm % halo == 0 and c % tc == 0 and t % seq == 0
    return halo, tm, tc, min(64 if halo > SUBLANES else 128, tm), min(LANES, tc)


def _pre(kind, a, b):
    if kind == "glu":
        return a * _sigmoid(b)
    if kind == "mul":
        return a * b
    return a


def _taps(k):
    return sorted((s % SUBLANES, s // SUBLANES, s) for s in range(k))


def _conv_fwd(name, seq, c, w, x1, c1, x2=None, c2=0, pre=None, bias=None, post=None, cpost=0, live=None):
    t = x1.shape[0]
    k = w.shape[0]
    halo, tm, tc, sr, sl = _conv_geom(t, seq, c, k, live is not None)
    nb, cps = tm // halo, seq // tm
    two = x2 is not None
    has_bias, has_post = bias is not None, post is not None

    def body(*refs):
        it = iter(refs)
        x1c, x1h = next(it), next(it)
        x2c, x2h = (next(it), next(it)) if two else (None, None)
        w_ref = next(it)
        b_ref = next(it) if has_bias else None
        p_ref = next(it) if has_post else None
        o_ref = next(it)
        y_ref = next(it) if has_post else None
        xs = next(it)
        first = (pl.program_id(1) % cps) == 0
        hv = _pre(pre, x1h[...].astype(F32), x2h[...].astype(F32) if two else None)
        xs[0:halo, :] = jnp.where(first, 0.0, hv)
        xs[halo:halo + tm, :] = _pre(pre, x1c[...].astype(F32), x2c[...].astype(F32) if two else None)
        for l0 in range(0, tc, sl):
            ls = slice(l0, l0 + sl)
            for r0 in range(0, tm, sr):
                win = xs[r0:r0 + sr + halo, ls]
                acc = jnp.zeros((sr, sl), F32)
                rolled = {}
                for r, q, s in _taps(k if live is None else live[l0 // sl]):
                    if r not in rolled:
                        rolled[r] = win if r == 0 else pltpu.roll(win, r, 0)
                    lo = halo - SUBLANES * q
                    acc = acc + w_ref[k - 1 - s:k - s, ls] * rolled[r][lo:lo + sr]
                if has_bias:
                    acc = acc + b_ref[:, ls]
                o_ref[r0:r0 + sr, ls] = acc.astype(o_ref.dtype)
                if has_post:
                    y_ref[r0:r0 + sr, ls] = (acc * p_ref[r0:r0 + sr, ls].astype(F32)).astype(y_ref.dtype)

    def cur(off):
        return pl.BlockSpec((tm, tc), lambda j, i: (i, off // tc + j))

    def prev(off):
        return pl.BlockSpec((halo, tc), lambda j, i: (jnp.maximum(i * nb - 1, 0), off // tc + j))

    ins, specs = [x1, x1], [cur(c1), prev(c1)]
    if two:
        ins += [x2, x2]
        specs += [cur(c2), prev(c2)]
    ins.append(w)
    specs.append(pl.BlockSpec((k, tc), lambda j, i: (0, j)))
    if has_bias:
        ins.append(bias)
        specs.append(pl.BlockSpec((1, tc), lambda j, i: (0, j)))
    if has_post:
        ins.append(post)
        specs.append(cur(cpost))
    o_spec = pl.BlockSpec((tm, tc), lambda j, i: (i, j))
    shp = jax.ShapeDtypeStruct((t, c), BF16)
    return pl.pallas_call(
        body, name=name, grid=(c // tc, t // tm), in_specs=specs,
        out_specs=[o_spec, o_spec] if has_post else o_spec,
        out_shape=[shp, shp] if has_post else shp,
        scratch_shapes=[pltpu.VMEM((halo + tm, tc), F32)],
        compiler_params=_cp("parallel", "parallel"))(*ins)


def _conv_bwd(name, seq, c, w, d1, cd1, d2=None, cd2=0, dpre=None,
              x1=None, c1=0, x2=None, c2=0, pre=None, live=None, dep=None):
    t = d1.shape[0]
    k = w.shape[0]
    assert live is None or x1 is None
    halo, tm, tc, sr, sl = _conv_geom(t, seq, c, k, live is not None)
    nb, cps = tm // halo, seq // tm
    nchunks = t // tm
    dtwo, xtwo, has_x = d2 is not None, x2 is not None, x1 is not None

    def body(*refs):
        it = iter(refs)
        d1c, d1n = next(it), next(it)
        d2c, d2n = (next(it), next(it)) if dtwo else (None, None)
        x1c, x1h = (next(it), next(it)) if has_x else (None, None)
        x2c, x2h = (next(it), next(it)) if xtwo else (None, None)
        w_ref = next(it)
        if dep is not None:
            next(it)
        dx_ref = next(it)
        dw_ref = next(it) if has_x else None
        ds = next(it)
        xs = next(it) if has_x else None
        i = pl.program_id(1)
        last = (i % cps) == cps - 1
        ds[0:tm, :] = _pre(dpre, d1c[...].astype(F32), d2c[...].astype(F32) if dtwo else None)
        nv = _pre(dpre, d1n[...].astype(F32), d2n[...].astype(F32) if dtwo else None)
        ds[tm:tm + halo, :] = jnp.where(last, 0.0, nv)
        if has_x:
            first = (i % cps) == 0
            hv = _pre(pre, x1h[...].astype(F32), x2h[...].astype(F32) if xtwo else None)
            xs[0:halo, :] = jnp.where(first, 0.0, hv)
            xs[halo:halo + tm, :] = _pre(pre, x1c[...].astype(F32), x2c[...].astype(F32) if xtwo else None)

            @pl.when(i == 0)
            def _():
                dw_ref[...] = jnp.zeros_like(dw_ref)

        for l0 in range(0, tc, sl):
            ls = slice(l0, l0 + sl)
            for r0 in range(0, tm, sr):
                win = ds[r0:r0 + sr + halo, ls]
                nrow = sr + halo
                acc = jnp.zeros((sr, sl), F32)
                rolled = {}
                for r, q, s in _taps(k if live is None else live[l0 // sl]):
                    if r not in rolled:
                        rolled[r] = win if r == 0 else pltpu.roll(win, nrow - r, 0)
                    lo = SUBLANES * q
                    acc = acc + w_ref[k - 1 - s:k - s, ls] * rolled[r][lo:lo + sr]
                dx_ref[r0:r0 + sr, ls] = acc.astype(dx_ref.dtype)
                if has_x:
                    dcur = win[0:sr]
                    xwin = xs[r0:r0 + sr + halo, ls]
                    xrolled = {}
                    for r, q, s in _taps(k):
                        if r not in xrolled:
                            xrolled[r] = xwin if r == 0 else pltpu.roll(xwin, r, 0)
                        lo = halo - SUBLANES * q
                        part = jnp.sum(dcur * xrolled[r][lo:lo + sr], axis=0, keepdims=True)
                        dw_ref[k - 1 - s:k - s, ls] += part

    def cur(off):
        return pl.BlockSpec((tm, tc), lambda j, i: (i, off // tc + j))

    def prev(off):
        return pl.BlockSpec((halo, tc), lambda j, i: (jnp.maximum(i * nb - 1, 0), off // tc + j))

    def nxt(off):
        return pl.BlockSpec((halo, tc),
                            lambda j, i: (jnp.minimum((i + 1) * nb, nchunks * nb - 1), off // tc + j))

    ins, specs = [d1, d1], [cur(cd1), nxt(cd1)]
    if dtwo:
        ins += [d2, d2]
        specs += [cur(cd2), nxt(cd2)]
    if has_x:
        ins += [x1, x1]
        specs += [cur(c1), prev(c1)]
    if xtwo:
        ins += [x2, x2]
        specs += [cur(c2), prev(c2)]
    ins.append(w)
    specs.append(pl.BlockSpec((k, tc), lambda j, i: (0, j)))
    if dep is not None:
        ins.append(dep)
        specs.append(ANY)
    o_specs = [pl.BlockSpec((tm, tc), lambda j, i: (i, j))]
    o_shapes = [jax.ShapeDtypeStruct((t, c), BF16)]
    scratch = [pltpu.VMEM((tm + halo, tc), F32)]
    if has_x:
        o_specs.append(pl.BlockSpec((k, tc), lambda j, i: (0, j)))
        o_shapes.append(jax.ShapeDtypeStruct((k, c), F32))
        scratch.append(pltpu.VMEM((halo + tm, tc), F32))
    out = pl.pallas_call(
        body, name=name, grid=(c // tc, t // tm), in_specs=specs, out_specs=o_specs,
        out_shape=o_shapes, scratch_shapes=scratch,
        compiler_params=_cp("parallel", "arbitrary"))(*ins)
    return out if has_x else out[0]


def _pool_taps(c):
    kmax = max(POOL_WINDOWS)
    grp = c // len(POOL_WINDOWS)
    cols = []
    for wdw in POOL_WINDOWS:
        col = jnp.concatenate([jnp.zeros((kmax - wdw,), F32), jnp.ones((wdw,), F32)])
        cols.append(jnp.tile(col[:, None], (1, grp)))
    return jnp.concatenate(cols, axis=1)


def _pool_live(c):
    grp, sl = c // len(POOL_WINDOWS), min(LANES, c)
    return tuple(max(POOL_WINDOWS[g] for g in range(l0 // grp, (l0 + sl - 1) // grp + 1))
                 for l0 in range(0, c, sl))


def _counts(i, tr, seq, grp):
    pos = (i * tr + lax.broadcasted_iota(jnp.int32, (tr, 1), 0)) % seq + 1
    return [1.0 / jnp.minimum(pos, wdw).astype(F32) for wdw in POOL_WINDOWS]


def _ln_stats(a2):
    mu = jnp.mean(a2, axis=-1, keepdims=True)
    xc = a2 - mu
    rstd = lax.rsqrt(jnp.mean(xc * xc, axis=-1, keepdims=True) + EPS)
    return xc * rstd, rstd


def _even_fwd(name, seq, a2, ws, u, ln_g, ln_b, w_pool, scale):
    t, c = a2.shape
    ng = len(POOL_WINDOWS)
    grp = c // ng
    tr = _tile(t, 256, SUBLANES)

    def body(a_ref, ws_ref, b_ref, g_ref, bb_ref, wp_ref, sc_ref, z_ref, pm_ref):
        xhat, _ = _ln_stats(a_ref[...].astype(F32))
        l = xhat * g_ref[...] + bb_ref[...]
        z_ref[:, 0:c] = (l * _sigmoid(l)).astype(BF16)
        inv = _counts(pl.program_id(0), tr, seq, grp)
        for g in range(ng):
            gs = slice(g * grp, (g + 1) * grp)
            pm = (ws_ref[:, gs].astype(F32) * inv[g] - b_ref[:, gs].astype(F32)).astype(BF16)
            pm_ref[:, gs] = pm
            q = jnp.dot(pm, wp_ref[g], preferred_element_type=F32)
            z_ref[:, c + g * grp:c + (g + 1) * grp] = (q * sc_ref[:, gs]).astype(BF16)

    row = pl.BlockSpec((tr, c), lambda i: (i, 0))
    vec = pl.BlockSpec((1, c), lambda i: (0, 0))
    return pl.pallas_call(
        body, name=name, grid=(t // tr,),
        in_specs=[row, row, pl.BlockSpec((tr, c), lambda i: (i, 2)), vec, vec,
                  pl.BlockSpec((ng, grp, grp), lambda i: (0, 0, 0)), vec],
        out_specs=[pl.BlockSpec((tr, 2 * c), lambda i: (i, 0)), row],
        out_shape=[jax.ShapeDtypeStruct((t, 2 * c), BF16), jax.ShapeDtypeStruct((t, c), BF16)],
        compiler_params=_cp("parallel"))(a2, ws, u, ln_g, ln_b, w_pool, scale)


def _even_bwd(name, seq, dz, a2, pm, ln_g, ln_b, w_pool, scale):
    t, c = a2.shape
    ng = len(POOL_WINDOWS)
    grp = c // ng
    tr = _tile(t, 256, SUBLANES)

    def body(dz_ref, a_ref, pm_ref, g_ref, bb_ref, wp_ref, sc_ref,
             da_ref, dws_ref, dpm_ref, vec_ref, dwp_ref):
        i = pl.program_id(0)

        @pl.when(i == 0)
        def _():
            vec_ref[...] = jnp.zeros_like(vec_ref)
            dwp_ref[...] = jnp.zeros_like(dwp_ref)

        xhat, rstd = _ln_stats(a_ref[...].astype(F32))
        gv = g_ref[...]
        l = xhat * gv + bb_ref[...]
        sg = _sigmoid(l)
        dl = dz_ref[:, 0:c].astype(F32) * (sg * (1.0 + l * (1.0 - sg)))
        dxh = dl * gv
        da2 = rstd * (dxh - jnp.mean(dxh, axis=-1, keepdims=True)
                      - xhat * jnp.mean(dxh * xhat, axis=-1, keepdims=True))
        da_ref[...] = da2.astype(BF16)
        vec_ref[0:1, :] += jnp.sum(dl * xhat, axis=0, keepdims=True)
        vec_ref[1:2, :] += jnp.sum(dl, axis=0, keepdims=True)
        vec_ref[2:3, :] += jnp.sum(da2, axis=0, keepdims=True)
        inv = _counts(i, tr, seq, grp)
        for g in range(ng):
            gs = slice(g * grp, (g + 1) * grp)
            pmv = pm_ref[:, gs]
            wp = wp_ref[g]
            dp = dz_ref[:, c + g * grp:c + (g + 1) * grp].astype(F32)
            q = jnp.dot(pmv, wp, preferred_element_type=F32)
            vec_ref[3:4, gs] += jnp.sum(dp * q, axis=0, keepdims=True)
            dq = (dp * sc_ref[:, gs]).astype(BF16)
            dpm = lax.dot_general(dq, wp, NT, preferred_element_type=F32)
            dwp_ref[g] += lax.dot_general(pmv, dq, TN, preferred_element_type=F32)
            dpm_ref[:, gs] = dpm.astype(BF16)
            dws_ref[:, gs] = (dpm * inv[g]).astype(BF16)

    row = pl.BlockSpec((tr, c), lambda i: (i, 0))
    vec = pl.BlockSpec((1, c), lambda i: (0, 0))
    rshape = jax.ShapeDtypeStruct((t, c), BF16)
    return pl.pallas_call(
        body, name=name, grid=(t // tr,),
        in_specs=[pl.BlockSpec((tr, 2 * c), lambda i: (i, 0)), row, row, vec, vec,
                  pl.BlockSpec((ng, grp, grp), lambda i: (0, 0, 0)), vec],
        out_specs=[row, row, row, pl.BlockSpec((SUBLANES, c), lambda i: (0, 0)),
                   pl.BlockSpec((ng, grp, grp), lambda i: (0, 0, 0))],
        out_shape=[rshape, rshape, rshape, jax.ShapeDtypeStruct((SUBLANES, c), F32),
                   jax.ShapeDtypeStruct((ng, grp, grp), F32)],
        compiler_params=_cp("arbitrary"))(dz, a2, pm, ln_g, ln_b, w_pool, scale)


def _even_du(name, u, da1, dbp, dpm):
    t, c = da1.shape
    tr = _tile(t, 256, SUBLANES)

    def body(u_ref, da_ref, dbp_ref, dpm_ref, du_ref):
        val = u_ref[:, 0:c].astype(F32)
        sg = _sigmoid(u_ref[:, c:2 * c].astype(F32))
        da = da_ref[...].astype(F32)
        du_ref[:, 0:c] = (da * sg).astype(BF16)
        du_ref[:, c:2 * c] = (da * val * sg * (1.0 - sg)).astype(BF16)
        du_ref[:, 2 * c:3 * c] = (dbp_ref[...].astype(F32) - dpm_ref[...].astype(F32)).astype(BF16)

    row = pl.BlockSpec((tr, c), lambda i: (i, 0))
    wide = pl.BlockSpec((tr, 3 * c), lambda i: (i, 0))
    return pl.pallas_call(
        body, name=name, grid=(t // tr,), in_specs=[wide, row, row, row], out_specs=wide,
        out_shape=jax.ShapeDtypeStruct((t, 3 * c), BF16),
        compiler_params=_cp("parallel"))(u, da1, dbp, dpm)


def _odd_du(name, u, dy, co, dxc):
    t, c = dy.shape
    tr = _tile(t, 256, SUBLANES)

    def body(u_ref, dy_ref, co_ref, dx_ref, du_ref):
        dx = dx_ref[...].astype(F32)
        du_ref[:, 0:c] = (dy_ref[...].astype(F32) * co_ref[...].astype(F32)).astype(BF16)
        du_ref[:, c:2 * c] = (dx * u_ref[:, 2 * c:3 * c].astype(F32)).astype(BF16)
        du_ref[:, 2 * c:3 * c] = (dx * u_ref[:, c:2 * c].astype(F32)).astype(BF16)

    row = pl.BlockSpec((tr, c), lambda i: (i, 0))
    wide = pl.BlockSpec((tr, 3 * c), lambda i: (i, 0))
    return pl.pallas_call(
        body, name=name, grid=(t // tr,), in_specs=[wide, row, row, row], out_specs=wide,
        out_shape=jax.ShapeDtypeStruct((t, 3 * c), BF16),
        compiler_params=_cp("parallel"))(u, dy, co, dxc)


def _local_step(x, tgt, seq, small, get_w, put_g, sync):
    t, d = x.shape
    c = d // 2
    cw_e, cw_o = small["conv_w_e"], small["conv_w_o"]
    wp = small["w_pool_e"].astype(BF16)
    ptaps = _pool_taps(c)
    row = lambda v: v.reshape(1, -1)

    we = {"w_in": get_w("in_e", x)[0]}
    n0 = _rms_fwd("rms_fwd_mix0", x, row(small["mix_norm_e"]))
    u0 = _mm_nn("mm_in_e", n0, we["w_in"], BF16)
    sync("fwd_a", u0)
    a2 = _conv_fwd("conv_e_fwd", seq, c, cw_e, u0, 0, u0, c, "glu", bias=row(small["conv_b_e"]))
    ws = _conv_fwd("pool_fwd", seq, c, ptaps, u0, 2 * c, live=_pool_live(c))
    z0, pm = _even_fwd("even_fwd", seq, a2, ws, u0, row(small["ln_g_e"]), row(small["ln_b_e"]),
                       wp, row(small["pool_scale_e"]))
    we["w_out"] = get_w("out_e", z0)[0]
    h1 = _mm_nn("mm_out_e", z0, we["w_out"], F32, res=x)
    sync("fwd_b", h1)
    n1 = _rms_fwd("rms_fwd_ffn0", h1, row(small["ffn_norm"][0]))
    wf0 = dict(zip(("w_gate", "w_up"), get_w("gu0", n1)))
    act0, ds0, s0 = _ffn_fwd("ffn0_fwd", n1, wf0["w_gate"], wf0["w_up"])
    dep = sync("fwd_c", act0)
    wf0["w_down"] = get_w("down0", act0)[0]
    h2 = _mm_nn("mm_down0", act0, wf0["w_down"], F32, res=h1, dep=dep)
    dep = sync("fwd_d", h2)
    n2 = _rms_fwd("rms_fwd_mix1", h2, row(small["mix_norm_o"]))
    wo = {"w_in": get_w("in_o", n2)[0]}
    u1 = _mm_nn("mm_in_o", n2, wo["w_in"], BF16, dep=dep)
    co, y1 = _conv_fwd("conv_o_fwd", seq, d, cw_o, u1, d, u1, 2 * d, "mul", post=u1, cpost=0)
    dep = sync("fwd_e", y1)
    wo["w_out"] = get_w("out_o", y1)[0]
    h3 = _mm_nn("mm_out_o", y1, wo["w_out"], F32, res=h2, dep=dep)
    dep = sync("fwd_f", h3)
    n3 = _rms_fwd("rms_fwd_ffn1", h3, row(small["ffn_norm"][1]))
    wf1 = dict(zip(("w_gate", "w_up"), get_w("gu1", n3)))
    act1, ds1, s1 = _ffn_fwd("ffn1_fwd", n3, wf1["w_gate"], wf1["w_up"], dep=dep)
    wf1["w_down"] = get_w("down1", act1)[0]
    h4 = _mm_nn("mm_down1", act1, wf1["w_down"], F32, res=h3)

    dh4, dh4b, d_final, lsum = _loss_head("loss_head", h4, row(small["final_norm"]), tgt)

    def ffn_bwd(tag, dh, dhb, h_in, gain, n, dsilu, silu, act, w, dep):
        dg, dup = _ffn_bwd_act("ffn%s_bwd_act" % tag, dhb, w["w_down"], dsilu, silu, dep=dep)
        dwd = _mm_tn("mm_dwd%s" % tag, act, dhb, BF16)
        dwg = _mm_tn("mm_dwg%s" % tag, dg, n, BF16, dep=sync("bwd_ffn" + tag, dwd))
        dwu = _mm_tn("mm_dwu%s" % tag, dup, n, BF16)
        dep = put_g("ffn" + tag, {"w_gate": dwg, "w_up": dwu, "w_down": dwd})
        dn = _mm_nn("mm_ffn_dn%s" % tag, [dg, dup], [w["w_gate"], w["w_up"]], BF16, dep=dep)
        dh_in, dhb_in, dgain = _rms_bwd("rms_bwd_ffn%s" % tag, h_in, gain, dn, dh)
        return dh_in, dhb_in, dgain, None

    dh3, dh3b, d_ffn1, dep = ffn_bwd("1", dh4, dh4b, h3, row(small["ffn_norm"][1]), n3, ds1, s1,
                                     act1, wf1, None)
    dw_out_o = _mm_tn("mm_dw_out_o", y1, dh3b, BF16, dep=dep)
    dy1 = _mm_nt("mm_dy_o", dh3b, wo["w_out"], BF16, dep=sync("bwd_mix_o", dw_out_o))
    dxc, dcw_o = _conv_bwd("conv_o_bwd", seq, d, cw_o, dy1, 0, u1, 0, "mul",
                           x1=u1, c1=d, x2=u1, c2=2 * d, pre="mul")
    du1 = _odd_du("odd_du", u1, dy1, co, dxc)
    dw_in_o = _mm_tn("mm_dw_in_o", n2, du1, BF16)
    dn2 = _mm_nt("mm_dn_o", du1, wo["w_in"], BF16)
    dh2, dh2b, d_mix_o = _rms_bwd("rms_bwd_mix1", h2, row(small["mix_norm_o"]), dn2, dh3)
    dep = put_g("mix_o", {"w_in": dw_in_o, "w_out": dw_out_o})

    dh1, dh1b, d_ffn0, dep = ffn_bwd("0", dh2, dh2b, h1, row(small["ffn_norm"][0]), n1, ds0, s0,
                                     act0, wf0, dep)
    dw_out_e = _mm_tn("mm_dw_out_e", z0, dh1b, BF16, dep=dep)
    dz0 = _mm_nt("mm_dz_e", dh1b, we["w_out"], BF16, dep=sync("bwd_mix_e", dw_out_e))
    da2, dws, dpm, vecs, dwp = _even_bwd("even_bwd", seq, dz0, a2, pm, row(small["ln_g_e"]),
                                         row(small["ln_b_e"]), wp, row(small["pool_scale_e"]))
    dep = put_g("small", {"conv_b_e": vecs[2], "ln_g_e": vecs[0], "ln_b_e": vecs[1],
                          "w_pool_e": dwp, "pool_scale_e": vecs[3], "mix_norm_o": d_mix_o[0],
                          "conv_w_o": dcw_o, "ffn_norm": jnp.concatenate([d_ffn0, d_ffn1], axis=0),
                          "final_norm": d_final[0], "loss": lsum})
    da1, dcw_e = _conv_bwd("conv_e_bwd", seq, c, cw_e, da2, 0, x1=u0, c1=0, x2=u0, c2=c, pre="glu", dep=dep)
    dbp = _conv_bwd("pool_bwd", seq, c, ptaps, dws, 0, live=_pool_live(c))
    du0 = _even_du("even_du", u0, da1, dbp, dpm)
    dw_in_e = _mm_tn("mm_dw_in_e", n0, du0, BF16)
    dep = put_g("mix_e", {"w_in": dw_in_e, "w_out": dw_out_e})
    dn0 = _mm_nt("mm_dn_e", du0, we["w_in"], BF16, dep=dep)
    dx, _, d_mix_e = _rms_bwd("rms_bwd_mix0", x, row(small["mix_norm_e"]), dn0, dh1, bf16_too=False)
    return dx, {"conv_w_e": dcw_e, "mix_norm_e": d_mix_e[0]}


def _place():
    x, y, c = (lax.axis_index(a) for a in MESH_AXES)
    return x, y, c


def _index(p):
    return 4 * p[0] + 2 * p[1] + p[2]


def _slab(ref, kind, d, n):
    if kind == "blk":
        return ref.at[d]
    return ref.at[:, pl.ds(pl.multiple_of(d * n, LANES), n)]


HBM = pl.BlockSpec(memory_space=pltpu.HBM)
SEM = pl.BlockSpec(memory_space=pltpu.SEMAPHORE)
EFFECT = pltpu.SideEffectType.DATAFLOW_SIDE_EFFECTING
NCHIPS = 4


def _in_hbm(a):
    return pltpu.with_memory_space_constraint(a, pltpu.HBM)


def _gathered_shape(s, kind):
    m, n = s.shape
    return (NDEV, m, n) if kind == "blk" else (m, NDEV * n)


def _first_targets():
    x, y, c = _place()
    return [(x, y, 1 - c), (1 - x, y, c), (x, 1 - y, c), (1 - x, 1 - y, c)]


def _gather_start(name, shards, kinds, after):
    na = len(shards)

    def body(*refs):
        x_refs, land_refs = refs[:na], refs[na:2 * na]
        send_sems, recv_sems = refs[2 * na + 1], refs[2 * na + 2]
        token = refs[-1]
        me = _index(_place())
        for a in range(na):
            for k, to in enumerate(_first_targets()):
                pltpu.make_async_remote_copy(
                    src_ref=x_refs[a], dst_ref=_slab(land_refs[a], kinds[a], me, shards[a].shape[1]),
                    send_sem=send_sems.at[4 * a + k], recv_sem=recv_sems.at[4 * a + k],
                    device_id=to, device_id_type=MESH).start()
        token[...] = jnp.zeros_like(token)

    lands = [lax.empty(_gathered_shape(s, k), s.dtype) for s, k in zip(shards, kinds)]
    outs = pl.pallas_call(
        body, name=name,
        out_shape=(pltpu.SemaphoreType.DMA((4 * na,)), pltpu.SemaphoreType.DMA((4 * na,)),
                   *[pltpu.HBM(s.shape, s.dtype) for s in shards],
                   *[pltpu.HBM(l.shape, l.dtype) for l in lands],
                   jax.ShapeDtypeStruct((SUBLANES, LANES), F32)),
        in_specs=[HBM] * (2 * na) + [ANY],
        out_specs=(SEM, SEM, *[HBM] * (2 * na), pl.BlockSpec(memory_space=pltpu.VMEM)),
        input_output_aliases={i: 2 + i for i in range(2 * na)},
        compiler_params=pltpu.CompilerParams(has_side_effects=EFFECT),
    )(*[_in_hbm(s) for s in shards], *[_in_hbm(l) for l in lands], after)
    return outs[0], outs[1], outs[2:2 + na], outs[2 + na:2 + 2 * na], outs[-1]


def _gather_wait(name, started, kinds, after):
    send_sems, recv_sems, shards, lands, _ = started
    na = len(shards)

    def body(*refs):
        x_refs, land_refs = refs[:na], refs[na:2 * na]
        s_sems, r_sems = refs[2 * na], refs[2 * na + 1]
        for a in range(na):
            for k, frm in enumerate(_first_targets()):
                cp = pltpu.make_async_remote_copy(
                    src_ref=x_refs[a],
                    dst_ref=_slab(land_refs[a], kinds[a], _index(frm), shards[a].shape[1]),
                    send_sem=s_sems.at[4 * a + k], recv_sem=r_sems.at[4 * a + k],
                    device_id=frm, device_id_type=MESH)
                cp.wait_send()
                cp.wait_recv()

    outs = pl.pallas_call(
        body, name=name,
        out_shape=(*[pltpu.HBM(s.shape, s.dtype) for s in shards],
                   *[pltpu.HBM(l.shape, l.dtype) for l in lands]),
        in_specs=[HBM] * (2 * na) + [SEM, SEM, ANY], out_specs=[HBM] * (2 * na),
        input_output_aliases={i: i for i in range(2 * na)},
        compiler_params=pltpu.CompilerParams(has_side_effects=EFFECT),
    )(*shards, *lands, send_sems, recv_sems, after)
    return outs[:na], outs[na:]


def _split_start(name, bufs, ncopies, plan, after):
    nb = len(bufs)

    def body(*refs):
        send_sems, recv_sems, token = refs[nb + 1], refs[nb + 2], refs[-1]
        for k, (src, dst, to, _) in enumerate(plan(refs[:nb])):
            pltpu.make_async_remote_copy(src_ref=src, dst_ref=dst, send_sem=send_sems.at[k],
                                         recv_sem=recv_sems.at[k], device_id=to, device_id_type=MESH).start()
        token[...] = jnp.zeros_like(token)

    outs = pl.pallas_call(
        body, name=name,
        out_shape=(pltpu.SemaphoreType.DMA((ncopies,)), pltpu.SemaphoreType.DMA((ncopies,)),
                   *[pltpu.HBM(b.shape, b.dtype) for b in bufs],
                   jax.ShapeDtypeStruct((SUBLANES, LANES), F32)),
        in_specs=[HBM] * nb + [ANY],
        out_specs=(SEM, SEM, *[HBM] * nb, pl.BlockSpec(memory_space=pltpu.VMEM)),
        input_output_aliases={i: 2 + i for i in range(nb)},
        compiler_params=pltpu.CompilerParams(has_side_effects=EFFECT),
    )(*[_in_hbm(b) for b in bufs], after)
    return outs[0], outs[1], list(outs[2:2 + nb]), outs[-1]


def _split_wait(name, started, plan, after):
    send_sems, recv_sems, bufs, _ = started
    nb = len(bufs)

    def body(*refs):
        s_sems, r_sems = refs[nb], refs[nb + 1]
        for k, (src, _, to, landed) in enumerate(plan(refs[:nb])):
            cp = pltpu.make_async_remote_copy(src_ref=src, dst_ref=landed, send_sem=s_sems.at[k],
                                              recv_sem=r_sems.at[k], device_id=to, device_id_type=MESH)
            cp.wait_send()
            cp.wait_recv()

    outs = pl.pallas_call(
        body, name=name, out_shape=tuple(pltpu.HBM(b.shape, b.dtype) for b in bufs),
        in_specs=[HBM] * nb + [SEM, SEM, ANY], out_specs=[HBM] * nb,
        input_output_aliases={i: i for i in range(nb)},
        compiler_params=pltpu.CompilerParams(has_side_effects=EFFECT),
    )(*bufs, send_sems, recv_sems, after)
    return list(outs)


def _forward_plan(kinds, nloc):
    def plan(lands):
        x, y, c = _place()
        out = []
        for a, land in enumerate(lands):
            for chip in [(1 - x, y), (x, 1 - y), (1 - x, 1 - y)]:
                mine = _slab(land, kinds[a], _index((*chip, c)), nloc[a])
                out.append((mine, mine, (x, y, 1 - c), _slab(land, kinds[a], _index((*chip, 1 - c)), nloc[a])))
        return out
    return plan


def _own_copy(name, shard, land, kind, me):
    m, n = shard.shape
    tr = _tile(m, max(SUBLANES, 1048576 // n), SUBLANES)

    def body(s_ref, x_ref, land_ref, o_ref):
        o_ref[...] = x_ref[...]

    if kind == "blk":
        o_spec = pl.BlockSpec((None, tr, n), lambda i, s: (s[0], i, 0))
    else:
        o_spec = pl.BlockSpec((tr, n), lambda i, s: (i, s[0]))
    return pl.pallas_call(
        body, name=name,
        grid_spec=pltpu.PrefetchScalarGridSpec(
            num_scalar_prefetch=1, grid=(m // tr,),
            in_specs=[pl.BlockSpec((tr, n), lambda i, s: (i, 0)), ANY], out_specs=o_spec),
        out_shape=jax.ShapeDtypeStruct(land.shape, land.dtype),
        input_output_aliases={2: 0}, compiler_params=_cp("parallel"))(me, shard, land)


def _everyone_plan(refs):
    x, y, c = _place()
    out = []
    for dx, dy, dc in [(a, b, e) for a in (0, 1) for b in (0, 1) for e in (0, 1)][1:]:
        peer = (x ^ dx, y ^ dy, c ^ dc)
        out.append((refs[0], refs[1].at[_index((x, y, c))], peer, refs[1].at[_index(peer)]))
    return out


def _pair_plan(kinds, nloc):
    na = len(kinds)

    def plan(refs):
        x, y, c = _place()
        out = []
        for a in range(na):
            for j in range(NCHIPS):
                dst = refs[na + a].at[j]
                out.append((_slab(refs[a], kinds[a], 2 * j + (1 - c), nloc[a]), dst, (x, y, 1 - c), dst))
        return out
    return plan


def _chip_sum(name, full, kind, n, from_sib, place):
    _, m, _ = from_sib.shape
    tr = _tile(m, max(SUBLANES, 1048576 // n), SUBLANES)

    def body(s_ref, mine_ref, sib_ref, csum_ref, land_ref):
        v = (mine_ref[...].astype(F32) + sib_ref[...].astype(F32)).astype(csum_ref.dtype)
        csum_ref[...] = v

        @pl.when(pl.program_id(1) == s_ref[1])
        def _():
            land_ref[...] = v

    if kind == "blk":
        mine_spec = pl.BlockSpec((None, tr, n), lambda i, j, s: (2 * j + s[0], i, 0))
    else:
        mine_spec = pl.BlockSpec((tr, n), lambda i, j, s: (i, 2 * j + s[0]))
    slot = pl.BlockSpec((None, tr, n), lambda i, j, s: (j, i, 0))
    shp = jax.ShapeDtypeStruct((NCHIPS, m, n), from_sib.dtype)
    return pl.pallas_call(
        body, name=name,
        grid_spec=pltpu.PrefetchScalarGridSpec(
            num_scalar_prefetch=1, grid=(m // tr, NCHIPS), in_specs=[mine_spec, slot],
            out_specs=[slot, pl.BlockSpec((None, tr, n), lambda i, j, s: (s[1], i, 0))]),
        out_shape=[shp, shp], compiler_params=_cp("parallel", "arbitrary"))(place, full, from_sib)


def _other_chips():
    x, y, c = _place()
    return [(1 - x, y, c), (x, 1 - y, c), (1 - x, 1 - y, c)]


def _scatter_start(name, csums, lands, after):
    na = len(csums)

    def body(*refs):
        c_refs, land_refs = refs[:na], refs[na:2 * na]
        send_sems, recv_sems = refs[2 * na + 1], refs[2 * na + 2]
        token = refs[-1]
        x, y, _ = _place()
        for a in range(na):
            for k, to in enumerate(_other_chips()):
                pltpu.make_async_remote_copy(
                    src_ref=c_refs[a].at[2 * to[0] + to[1]], dst_ref=land_refs[a].at[2 * x + y],
                    send_sem=send_sems.at[3 * a + k], recv_sem=recv_sems.at[3 * a + k],
                    device_id=to, device_id_type=MESH).start()
        token[...] = jnp.zeros_like(token)

    outs = pl.pallas_call(
        body, name=name,
        out_shape=(pltpu.SemaphoreType.DMA((3 * na,)), pltpu.SemaphoreType.DMA((3 * na,)),
                   *[pltpu.HBM(s.shape, s.dtype) for s in csums],
                   *[pltpu.HBM(l.shape, l.dtype) for l in lands],
                   jax.ShapeDtypeStruct((SUBLANES, LANES), F32)),
        in_specs=[HBM] * (2 * na) + [ANY],
        out_specs=(SEM, SEM, *[HBM] * (2 * na), pl.BlockSpec(memory_space=pltpu.VMEM)),
        input_output_aliases={i: 2 + i for i in range(2 * na)},
        compiler_params=pltpu.CompilerParams(has_side_effects=EFFECT),
    )(*[_in_hbm(s) for s in csums], *[_in_hbm(l) for l in lands], after)
    return outs[0], outs[1], outs[2:2 + na], outs[2 + na:2 + 2 * na], outs[-1]


def _scatter_wait(name, started, after):
    send_sems, recv_sems, csums, lands, _ = started
    na = len(csums)

    def body(*refs):
        c_refs, land_refs = refs[:na], refs[na:2 * na]
        s_sems, r_sems = refs[2 * na], refs[2 * na + 1]
        for a in range(na):
            for k, frm in enumerate(_other_chips()):
                cp = pltpu.make_async_remote_copy(
                    src_ref=c_refs[a].at[2 * frm[0] + frm[1]], dst_ref=land_refs[a].at[2 * frm[0] + frm[1]],
                    send_sem=s_sems.at[3 * a + k], recv_sem=r_sems.at[3 * a + k],
                    device_id=frm, device_id_type=MESH)
                cp.wait_send()
                cp.wait_recv()

    outs = pl.pallas_call(
        body, name=name,
        out_shape=(*[pltpu.HBM(s.shape, s.dtype) for s in csums],
                   *[pltpu.HBM(l.shape, l.dtype) for l in lands]),
        in_specs=[HBM] * (2 * na) + [SEM, SEM, ANY], out_specs=[HBM] * (2 * na),
        input_output_aliases={i: i for i in range(2 * na)},
        compiler_params=pltpu.CompilerParams(has_side_effects=EFFECT),
    )(*csums, *lands, send_sems, recv_sems, after)
    return outs[na:]


def _adam_math(w, g, m, v):
    m = ADAM_B1 * m + (1.0 - ADAM_B1) * g
    v = ADAM_B2 * v + (1.0 - ADAM_B2) * (g * g)
    m_hat = m / (1.0 - ADAM_B1 ** ADAM_STEP)
    v_hat = v / (1.0 - ADAM_B2 ** ADAM_STEP)
    delta = -ADAM_LR * (m_hat / (jnp.sqrt(v_hat) + ADAM_EPS) + ADAM_WD * w)
    return delta, m, v


def _sum_adamw(name, parts, w, m, v, layer, prev=None, dep=None):
    nl, r, c = w.shape
    nparts = parts.shape[0]
    tr = _tile(r, max(SUBLANES, 360448 // c), SUBLANES)

    def body(p_ref, w_ref, m_ref, v_ref, *rest):
        g_ref, d_ref, mo_ref, vo_ref = rest[-4:]
        g = p_ref[0].astype(F32)
        for s in range(1, nparts):
            g = g + p_ref[s].astype(F32)
        delta, mn, vn = _adam_math(w_ref[...], g, m_ref[...], v_ref[...])
        g_ref[...] = g
        d_ref[...] = delta
        mo_ref[...] = mn
        vo_ref[...] = vn

    row = pl.BlockSpec((None, tr, c), lambda i: (layer, i, 0))
    shp = jax.ShapeDtypeStruct((nl, r, c), F32)
    extra = ([] if prev is None else list(prev)) + ([] if dep is None else [dep])
    return pl.pallas_call(
        body, name=name, grid=(r // tr,),
        in_specs=[pl.BlockSpec((nparts, tr, c), lambda i: (0, i, 0)), row, row, row] + [ANY] * len(extra),
        out_specs=[row, row, row, row], out_shape=[shp, shp, shp, shp],
        input_output_aliases={} if prev is None else {4 + i: i for i in range(4)},
        compiler_params=_cp("parallel"))(parts, w, m, v, *extra)


def _sum_parts(name, parts):
    _, r, c = parts.shape

    def body(p_ref, o_ref):
        g = p_ref[0]
        for s in range(1, NDEV):
            g = g + p_ref[s]
        o_ref[...] = g

    return pl.pallas_call(
        body, name=name, grid=(1,),
        in_specs=[pl.BlockSpec((NDEV, r, c), lambda i: (0, 0, 0))],
        out_specs=pl.BlockSpec((r, c), lambda i: (0, 0)),
        out_shape=jax.ShapeDtypeStruct((r, c), F32), compiler_params=_cp("arbitrary"))(parts)


def _adamw(name, w, g, m, v):
    r, c = w.shape

    def body(w_ref, g_ref, m_ref, v_ref, d_ref, mo_ref, vo_ref):
        delta, mn, vn = _adam_math(w_ref[...], g_ref[...], m_ref[...], v_ref[...])
        d_ref[...] = delta
        mo_ref[...] = mn
        vo_ref[...] = vn

    full = pl.BlockSpec((r, c), lambda i: (0, 0))
    shp = jax.ShapeDtypeStruct((r, c), F32)
    return pl.pallas_call(
        body, name=name, grid=(1,), in_specs=[full] * 4, out_specs=[full] * 3,
        out_shape=[shp] * 3, compiler_params=_cp("arbitrary"))(w, g, m, v)


def _pack(arrays):
    flat = jnp.concatenate([a.reshape(-1) for a in arrays])
    unit = SUBLANES * LANES
    pad = (-flat.shape[0]) % unit
    return jnp.pad(flat, (0, pad)).reshape(-1, LANES)


def _unpack(buf, shapes):
    flat = buf.reshape(-1)
    out, off = [], 0
    for shp in shapes:
        size = 1
        for s in shp:
            size *= s
        out.append(flat[off:off + size].reshape(shp))
        off += size
    return out


WEIGHTS = ["mix_norm_e", "w_in_e", "conv_w_e", "conv_b_e", "ln_g_e", "ln_b_e", "w_pool_e",
           "pool_scale_e", "w_out_e", "mix_norm_o", "w_in_o", "conv_w_o", "w_out_o", "ffn_norm",
           "w_gate", "w_up", "w_down", "final_norm"]
BIG = ["w_in_e", "w_out_e", "w_in_o", "w_out_o", "w_gate", "w_up", "w_down"]
SHARDED_SMALL = {"conv_w_e": 1, "w_pool_e": 1, "mix_norm_o": 0, "conv_w_o": 1}
SMALL = [n for n in WEIGHTS if n not in BIG]


def kernel(x, mix_norm_e, w_in_e, conv_w_e, conv_b_e, ln_g_e, ln_b_e, w_pool_e, pool_scale_e, w_out_e, mix_norm_o, w_in_o, conv_w_o, w_out_o, ffn_norm, w_gate, w_up, w_down, final_norm, loss_target, m_mix_norm_e, m_w_in_e, m_conv_w_e, m_conv_b_e, m_ln_g_e, m_ln_b_e, m_w_pool_e, m_pool_scale_e, m_w_out_e, m_mix_norm_o, m_w_in_o, m_conv_w_o, m_w_out_o, m_ffn_norm, m_w_gate, m_w_up, m_w_down, m_final_norm, v_mix_norm_e, v_w_in_e, v_conv_w_e, v_conv_b_e, v_ln_g_e, v_ln_b_e, v_w_pool_e, v_pool_scale_e, v_w_out_e, v_mix_norm_o, v_w_in_o, v_conv_w_o, v_w_out_o, v_ffn_norm, v_w_gate, v_w_up, v_w_down, v_final_norm):
    wts = dict(zip(WEIGHTS, [mix_norm_e, w_in_e, conv_w_e, conv_b_e, ln_g_e, ln_b_e, w_pool_e, pool_scale_e, w_out_e, mix_norm_o, w_in_o, conv_w_o, w_out_o, ffn_norm, w_gate, w_up, w_down, final_norm]))
    mom = dict(zip(WEIGHTS, [m_mix_norm_e, m_w_in_e, m_conv_w_e, m_conv_b_e, m_ln_g_e, m_ln_b_e, m_w_pool_e, m_pool_scale_e, m_w_out_e, m_mix_norm_o, m_w_in_o, m_conv_w_o, m_w_out_o, m_ffn_norm, m_w_gate, m_w_up, m_w_down, m_final_norm]))
    var = dict(zip(WEIGHTS, [v_mix_norm_e, v_w_in_e, v_conv_w_e, v_conv_b_e, v_ln_g_e, v_ln_b_e, v_w_pool_e, v_pool_scale_e, v_w_out_e, v_mix_norm_o, v_w_in_o, v_conv_w_o, v_w_out_o, v_ffn_norm, v_w_gate, v_w_up, v_w_down, v_final_norm]))
    bsz, seq, d = x.shape
    t = bsz * seq
    me = _index(_place())
    me_arr = jnp.reshape(me, (1,)).astype(jnp.int32)

    sh_names = list(SHARDED_SMALL)
    sh_local = [wts[n][0] for n in sh_names]
    packed = _pack(sh_local)
    params_st = _split_start("small_params_start", [packed, lax.empty((NDEV,) + packed.shape, F32)], NDEV - 1,
                             _everyone_plan, x)

    for state in (wts, mom, var):
        for n in ("w_gate", "w_up"):
            state[n] = jnp.swapaxes(state[n], 1, 2)
    bf = lambda a: a.astype(BF16)
    mix_kinds, ffn_kinds = ["col", "blk"], ["blk", "blk", "blk"]
    ffn_names = ("w_gate", "w_up", "w_down")
    groups = {
        "mix_e": ([w_in_e.shape[2], d], mix_kinds, [("w_in_e", 0), ("w_out_e", 0)]),
        "ffn0": ([d, d, d], ffn_kinds, [(n, 0) for n in ffn_names]),
        "mix_o": ([w_in_o.shape[2], d], mix_kinds, [("w_in_o", 0), ("w_out_o", 0)]),
        "ffn1": ([d, d, d], ffn_kinds, [(n, 1) for n in ffn_names]),
    }
    gathers = {
        "in_e": ([bf(w_in_e[0])], ["col"]), "out_e": ([bf(w_out_e[0])], ["blk"]),
        "gu0": ([bf(wts["w_gate"][0]), bf(wts["w_up"][0])], ["blk", "blk"]), "down0": ([bf(w_down[0])], ["blk"]),
        "in_o": ([bf(w_in_o[0])], ["col"]), "out_o": ([bf(w_out_o[0])], ["blk"]),
        "gu1": ([bf(wts["w_gate"][1]), bf(wts["w_up"][1])], ["blk", "blk"]), "down1": ([bf(w_down[1])], ["blk"]),
    }
    started, prev = {}, params_st[3]
    for grp, (shards, kinds) in gathers.items():
        started[grp] = _gather_start("gather_start_" + grp, shards, kinds, prev)
        prev = started[grp][4]
    all_started = prev[0, 0:1]

    bufs = _split_wait("small_params_wait", params_st, _everyone_plan, prev)
    gathered = _own_copy("small_params_own", bufs[0], bufs[1], "blk", me_arr)
    small = {n: wts[n][0] for n in SMALL if n not in SHARDED_SMALL and n not in ("ffn_norm", "final_norm")}
    small["ffn_norm"], small["final_norm"] = ffn_norm, final_norm
    flat, off = gathered.reshape(NDEV, -1), 0
    for n, a in zip(sh_names, sh_local):
        ax, shp = SHARDED_SMALL[n], a.shape
        blocks = jnp.moveaxis(flat[:, off:off + a.size].reshape((NDEV,) + shp), 0, ax)
        small[n] = blocks.reshape(shp[:ax] + (NDEV * shp[ax],) + shp[ax + 1:])
        off += a.size

    passing, shards_of = {}, {}

    def pass_on(grp, after):
        shards, kinds = gathers[grp]
        shards_of[grp], lands = _gather_wait("gather_wait_" + grp, started[grp], kinds, after)
        plan = _forward_plan(kinds, [s.shape[1] for s in shards])
        passing[grp] = (_split_start("forward_start_" + grp, lands, 3 * len(lands), plan, after), plan)
        return passing[grp][0][3]

    def get_w(grp, after):
        if grp not in passing:
            after = pass_on(grp, after)
        st, plan = passing[grp]
        lands = _split_wait("forward_wait_" + grp, st, plan, after)
        full = [_own_copy("own_copy_%s%d" % (grp, a), shards_of[grp][a], lands[a], gathers[grp][1][a], me_arr)
                for a in range(len(lands))]
        return [f.reshape(-1, d) if kind == "blk" else f for f, kind in zip(full, gathers[grp][1])]

    cx, cy, cc = _place()
    place = jnp.stack([cc, 2 * cx + cy]).astype(jnp.int32)
    bwd_order = ["ffn1", "mix_o", "ffn0", "mix_e"]
    pairing, pending, results = {}, {}, {}

    late_names = ["conv_w_e", "mix_norm_e"]
    early_names = [n for n in SMALL if n not in late_names] + ["loss"]
    small_sent = {}

    def send_small(tag, arrays, after):
        mine = _pack(arrays)
        small_sent[tag] = _split_start(tag + "_start", [mine, lax.empty((NDEV,) + mine.shape, F32)], NDEV - 1,
                                       _everyone_plan, after)
        return small_sent[tag][3]

    def summed_small(tag, shapes, after):
        bufs = _split_wait(tag + "_wait", small_sent[tag], _everyone_plan, after)
        parts = _own_copy(tag + "_own", bufs[0], bufs[1], "blk", me_arr)
        return _unpack(_sum_parts(tag + "_sum", parts), shapes)

    def put_g(grp, grads):
        if grp == "small":
            small_sent["shapes"] = [grads[n].shape for n in early_names]
            return send_small("small_grads", [grads[n] for n in early_names], place)
        nloc, kinds, _ = groups[grp]
        if len(kinds) == 2:
            fulls = [grads["w_in"], grads["w_out"].reshape(NDEV, -1, d)]
        else:
            fulls = [grads[n].reshape(NDEV, -1, d) for n in ffn_names]
        empties = []
        for g, kind, n in zip(fulls, kinds, nloc):
            empties.append(lax.empty((NCHIPS, g.shape[1] if kind == "blk" else g.shape[0], n), g.dtype))
        plan = _pair_plan(kinds, nloc)
        pairing[grp] = (_split_start("pair_start_" + grp, fulls + empties, NCHIPS * len(fulls), plan, place),
                        plan, kinds, nloc)
        token = pairing[grp][0][3]
        return send_sums(grp, token) if grp == bwd_order[-1] else token

    def send_sums(grp, after):
        st, plan, kinds, nloc = pairing[grp]
        bufs = _split_wait("pair_wait_" + grp, st, plan, after)
        na = len(kinds)
        sums = [_chip_sum("chip_sum_%s%d" % (grp, a), bufs[a], kinds[a], nloc[a], bufs[na + a], place)
                for a in range(na)]
        pending[grp] = _scatter_start("scatter_start_" + grp, [s[0] for s in sums], [s[1] for s in sums], after)
        return pending[grp][4]

    def finish(grp, after):
        lands = _scatter_wait("scatter_wait_" + grp, pending[grp], after)
        dep = None
        for (n, l), parts in zip(groups[grp][2], lands):
            results[n] = _sum_adamw("adamw_%s%d" % (n, l), parts, wts[n], mom[n], var[n], l, results.get(n), dep)
            dep = results[n][1]
        return dep

    fwd_sync = {"fwd_a": ["out_e"], "fwd_b": ["gu0"], "fwd_c": ["down0", "in_o"], "fwd_d": ["out_o"],
                "fwd_e": ["gu1"], "fwd_f": ["down1"]}

    def sync(tag, after):
        if tag in fwd_sync:
            for grp in fwd_sync[tag]:
                after = pass_on(grp, after)
            return after
        if tag == "bwd_mix_o":
            return send_sums("ffn1", after)
        if tag == "bwd_ffn0":
            return finish("ffn1", send_sums("mix_o", after))
        if tag == "bwd_mix_e":
            return finish("mix_o", send_sums("ffn0", after))
        return None

    small["mix_norm_e"] = small["mix_norm_e"] + all_started
    dx, late = _local_step(x.reshape(t, d), loss_target.reshape(t, d), seq, small, get_w, put_g, sync)

    out_g, out_d, out_m, out_v = {}, {}, {}, {}

    dep = finish("ffn0", send_small("last_grads", [late[n] for n in late_names], dx))
    sums = dict(zip(early_names, summed_small("small_grads", small_sent["shapes"], dep)))
    sums.update(zip(late_names, summed_small("last_grads", [late[n].shape for n in late_names], dep)))
    loss = jnp.sum(sums["loss"])
    gs_sum = [sums[n] for n in SMALL]
    local_g = []
    for n, g in zip(SMALL, gs_sum):
        if n in SHARDED_SMALL:
            ax = SHARDED_SMALL[n]
            size = wts[n].shape[ax + 1]
            g = lax.dynamic_slice_in_dim(g, me * size, size, axis=ax)
        local_g.append(g.reshape(wts[n].shape))
    shapes = [wts[n].shape for n in SMALL]
    upd = _adamw("adamw_small", _pack([wts[n] for n in SMALL]), _pack(local_g),
                 _pack([mom[n] for n in SMALL]), _pack([var[n] for n in SMALL]))
    for i, outd in enumerate((out_d, out_m, out_v)):
        for n, a in zip(SMALL, _unpack(upd[i], shapes)):
            outd[n] = a
    for n, g in zip(SMALL, local_g):
        out_g[n] = g

    finish("mix_e", upd[0])
    for n in BIG:
        res = [jnp.swapaxes(a, 1, 2) for a in results[n]] if n in ("w_gate", "w_up") else results[n]
        out_g[n], out_d[n], out_m[n], out_v[n] = res

    return (loss, dx.reshape(bsz, seq, d), *[out_g[n] for n in WEIGHTS], *[out_d[n] for n in WEIGHTS],
            *[out_m[n] for n in WEIGHTS], *[out_v[n] for n in WEIGHTS])
```

```python
import jax
import jax.numpy as jnp
from jax import lax
from jax.experimental import pallas as pl
from jax.experimental.pallas import tpu as pltpu

F32 = jnp.float32
BF16 = jnp.bfloat16
NDEV = 8
MESH_AXES = ("x", "y", "c")
EPS = 1e-6
POOL_WINDOWS = (2, 4, 8, 16)
ADAM_LR = 0.001
ADAM_B1 = 0.9
ADAM_B2 = 0.999
ADAM_EPS = 1e-08
ADAM_WD = 0.01
ADAM_STEP = 10
LANES = 128
SUBLANES = 8
VMEM_LIMIT = 56 * 1024 * 1024
MXU_DEPTH = 256
MM_TK = 2816
MESH = pl.DeviceIdType.MESH
ANY = pl.BlockSpec(memory_space=pl.ANY)


def _cp(*sem):
    return pltpu.CompilerParams(dimension_semantics=sem, vmem_limit_bytes=VMEM_LIMIT)


def _tile(n, pref, unit=LANES):
    if n <= pref:
        return n
    t = (pref // unit) * unit
    while t > unit and n % t:
        t -= unit
    assert n % t == 0, (n, pref)
    return t


def _sigmoid(v):
    return 0.5 * jnp.tanh(0.5 * v) + 0.5


def _mm(name, pairs, a_specs, b_specs, dims, out_shape, o_spec, grid, acc_shape,
        res=None, res_spec=None, dep=None):
    np_ = len(pairs)
    nk = grid[2]
    has_res = res is not None
    n_in = 2 * np_ + (1 if has_res else 0) + (0 if dep is None else 1)

    def body(*refs):
        a_refs = refs[:np_]
        b_refs = refs[np_:2 * np_]
        r_ref = refs[2 * np_] if has_res else None
        o_ref = refs[n_in]
        acc = refs[-1]

        def part():
            s = None
            for a_ref, b_ref in zip(a_refs, b_refs):
                d = lax.dot_general(a_ref[...], b_ref[...], dims, preferred_element_type=F32)
                s = d if s is None else s + d
            return s

        def finish(v):
            if has_res:
                v = v + r_ref[...]
            o_ref[...] = v.astype(o_ref.dtype)

        if nk == 1:
            finish(part())
        else:
            k = pl.program_id(2)

            @pl.when(k == 0)
            def _():
                acc[...] = part()

            @pl.when((k > 0) & (k < nk - 1))
            def _():
                acc[...] += part()

            @pl.when(k == nk - 1)
            def _():
                finish(acc[...] + part())

    ins = [p[0] for p in pairs] + [p[1] for p in pairs]
    specs = list(a_specs) + list(b_specs)
    if has_res:
        ins.append(res)
        specs.append(res_spec)
    if dep is not None:
        ins.append(dep)
        specs.append(ANY)
    return pl.pallas_call(
        body, name=name, grid=grid, in_specs=specs, out_specs=o_spec, out_shape=out_shape,
        scratch_shapes=[pltpu.VMEM(acc_shape if nk > 1 else (SUBLANES, LANES), F32)],
        compiler_params=_cp("parallel", "parallel", "arbitrary"))(*ins)


NN = (((1,), (0,)), ((), ()))
NT = (((1,), (1,)), ((), ()))
TN = (((0,), (0,)), ((), ()))


def _tiles_mk(m, kk):
    return _tile(m, 1024), _tile(kk, MM_TK, MXU_DEPTH)


def _mm_nn(name, a, b, out_dtype, res=None, dep=None):
    pairs = list(zip(a, b)) if isinstance(a, (list, tuple)) else [(a, b)]
    m, kk = pairs[0][0].shape
    n = pairs[0][1].shape[1]
    tm, tk = _tiles_mk(m, kk)
    tn = _tile(n, 1024 if tk * len(pairs) <= MM_TK else 512)
    return _mm(name, pairs,
               [pl.BlockSpec((tm, tk), lambda i, j, k: (i, k))] * len(pairs),
               [pl.BlockSpec((tk, tn), lambda i, j, k: (k, j))] * len(pairs), NN,
               jax.ShapeDtypeStruct((m, n), out_dtype),
               pl.BlockSpec((tm, tn), lambda i, j, k: (i, j)),
               (m // tm, n // tn, kk // tk), (tm, tn), res,
               pl.BlockSpec((tm, tn), lambda i, j, k: (i, j)), dep=dep)


def _mm_nt(name, a, b, out_dtype, dep=None):
    m, n = a.shape
    kk = b.shape[0]
    tn = _tile(kk, 1024)
    tm, tk = _tiles_mk(m, n)
    return _mm(name, [(a, b)],
               [pl.BlockSpec((tm, tk), lambda i, j, k: (i, k))],
               [pl.BlockSpec((tn, tk), lambda i, j, k: (j, k))], NT,
               jax.ShapeDtypeStruct((m, kk), out_dtype),
               pl.BlockSpec((tm, tn), lambda i, j, k: (i, j)),
               (m // tm, kk // tn, n // tk), (tm, tn), dep=dep)


def _mm_tn(name, a, b, out_dtype, dep=None):
    t, m = a.shape
    n = b.shape[1]
    tn = _tile(n, 1024)
    tm, tk = _tile(m, 1408), _tile(t, MM_TK, MXU_DEPTH)
    return _mm(name, [(a, b)],
               [pl.BlockSpec((tk, tm), lambda i, j, k: (k, i))],
               [pl.BlockSpec((tk, tn), lambda i, j, k: (k, j))], TN,
               jax.ShapeDtypeStruct((m, n), out_dtype),
               pl.BlockSpec((tm, tn), lambda i, j, k: (i, j)),
               (m // tm, n // tn, t // tk), (tm, tn), dep=dep)


def _ffn_fwd(name, n, wg, wu, dep=None):
    f, d = wg.shape
    t = n.shape[0]
    tm, tn = _tile(t, 1024), _tile(f, 512)

    def body(n_ref, wg_ref, wu_ref, *rest):
        act_ref, ds_ref, s_ref = rest[-3:]
        nv = n_ref[...]
        g = lax.dot_general(nv, wg_ref[...], NT, preferred_element_type=F32)
        up = lax.dot_general(nv, wu_ref[...], NT, preferred_element_type=F32)
        sg = _sigmoid(g)
        silu = g * sg
        act_ref[...] = (silu * up).astype(BF16)
        ds_ref[...] = (up * (sg * (1.0 + g * (1.0 - sg)))).astype(BF16)
        s_ref[...] = silu.astype(BF16)

    w_spec = pl.BlockSpec((tn, d), lambda j, i: (j, 0))
    o_spec = pl.BlockSpec((tm, tn), lambda j, i: (i, j))
    shp = jax.ShapeDtypeStruct((t, f), BF16)
    return pl.pallas_call(
        body, name=name, grid=(f // tn, t // tm),
        in_specs=[pl.BlockSpec((tm, d), lambda j, i: (i, 0)), w_spec, w_spec] + ([] if dep is None else [ANY]),
        out_specs=[o_spec, o_spec, o_spec], out_shape=[shp, shp, shp],
        compiler_params=_cp("parallel", "parallel"))(n, wg, wu, *([] if dep is None else [dep]))


def _ffn_bwd_act(name, dh, wd, dsilu, silu, dep=None):
    f, d = wd.shape
    t = dh.shape[0]
    tm, tn = _tile(t, 1024), _tile(f, 512)

    def body(dh_ref, wd_ref, ds_ref, s_ref, *rest):
        dg_ref, dup_ref = rest[-2:]
        da = lax.dot_general(dh_ref[...], wd_ref[...], NT, preferred_element_type=F32)
        dg_ref[...] = (da * ds_ref[...].astype(F32)).astype(BF16)
        dup_ref[...] = (da * s_ref[...].astype(F32)).astype(BF16)

    o_spec = pl.BlockSpec((tm, tn), lambda i, j: (i, j))
    shp = jax.ShapeDtypeStruct((t, f), BF16)
    return pl.pallas_call(
        body, name=name, grid=(t // tm, f // tn),
        in_specs=[pl.BlockSpec((tm, d), lambda i, j: (i, 0)),
                  pl.BlockSpec((tn, d), lambda i, j: (j, 0)), o_spec, o_spec]
        + ([] if dep is None else [ANY]),
        out_specs=[o_spec, o_spec], out_shape=[shp, shp],
        compiler_params=_cp("parallel", "parallel"))(dh, wd, dsilu, silu, *([] if dep is None else [dep]))


def _rms_fwd(name, h, gain):
    t, d = h.shape
    tr = _tile(t, 512, SUBLANES)

    def body(h_ref, g_ref, n_ref):
        hv = h_ref[...]
        r = lax.rsqrt(jnp.mean(hv * hv, axis=-1, keepdims=True) + EPS)
        n_ref[...] = (hv * r * g_ref[...]).astype(BF16)

    return pl.pallas_call(
        body, name=name, grid=(t // tr,),
        in_specs=[pl.BlockSpec((tr, d), lambda i: (i, 0)), pl.BlockSpec((1, d), lambda i: (0, 0))],
        out_specs=pl.BlockSpec((tr, d), lambda i: (i, 0)),
        out_shape=jax.ShapeDtypeStruct((t, d), BF16),
        compiler_params=_cp("parallel"))(h, gain)


def _rms_bwd_math(hv, gain, dn):
    d = hv.shape[-1]
    r = lax.rsqrt(jnp.mean(hv * hv, axis=-1, keepdims=True) + EPS)
    xhat = hv * r
    dxh = dn * gain
    dh = r * (dxh - xhat * (jnp.sum(dxh * xhat, axis=-1, keepdims=True) / d))
    dgain = jnp.sum(dn * xhat, axis=0, keepdims=True)
    return dh, dgain


def _rms_bwd(name, h, gain, dn, dres, bf16_too=True):
    t, d = h.shape
    tr = _tile(t, 256, SUBLANES)

    def body(h_ref, g_ref, dn_ref, dr_ref, dh_ref, *rest):
        dg_ref = rest[-1]
        dh, dgain = _rms_bwd_math(h_ref[...], g_ref[...], dn_ref[...].astype(F32))
        dh = dh + dr_ref[...]
        dh_ref[...] = dh
        if bf16_too:
            rest[0][...] = dh.astype(BF16)

        @pl.when(pl.program_id(0) == 0)
        def _():
            dg_ref[...] = dgain

        @pl.when(pl.program_id(0) > 0)
        def _():
            dg_ref[...] += dgain

    row = pl.BlockSpec((tr, d), lambda i: (i, 0))
    vec = pl.BlockSpec((1, d), lambda i: (0, 0))
    halves = [jax.ShapeDtypeStruct((t, d), BF16)] if bf16_too else []
    out = pl.pallas_call(
        body, name=name, grid=(t // tr,), in_specs=[row, vec, row, row],
        out_specs=[row] + [row] * len(halves) + [vec],
        out_shape=[jax.ShapeDtypeStruct((t, d), F32)] + halves + [jax.ShapeDtypeStruct((1, d), F32)],
        compiler_params=_cp("arbitrary"))(h, gain, dn, dres)
    return (out[0], out[1], out[2]) if bf16_too else (out[0], None, out[1])


def _loss_head(name, h, gain, tgt):
    t, d = h.shape
    tr = _tile(t, 256, SUBLANES)

    def body(h_ref, g_ref, t_ref, dh_ref, dhb_ref, dg_ref, ls_ref):
        hv = h_ref[...]
        gv = g_ref[...]
        r = lax.rsqrt(jnp.mean(hv * hv, axis=-1, keepdims=True) + EPS)
        err = hv * r * gv - t_ref[...]
        lsum = 0.5 * jnp.sum(err * err, axis=0, keepdims=True) / d
        dh, dgain = _rms_bwd_math(hv, gv, err / d)
        dh_ref[...] = dh
        dhb_ref[...] = dh.astype(BF16)

        @pl.when(pl.program_id(0) == 0)
        def _():
            dg_ref[...] = dgain
            ls_ref[...] = lsum

        @pl.when(pl.program_id(0) > 0)
        def _():
            dg_ref[...] += dgain
            ls_ref[...] += lsum

    row = pl.BlockSpec((tr, d), lambda i: (i, 0))
    vec = pl.BlockSpec((1, d), lambda i: (0, 0))
    return pl.pallas_call(
        body, name=name, grid=(t // tr,), in_specs=[row, vec, row],
        out_specs=[row, row, vec, vec],
        out_shape=[jax.ShapeDtypeStruct((t, d), F32), jax.ShapeDtypeStruct((t, d), BF16),
                   jax.ShapeDtypeStruct((1, d), F32), jax.ShapeDtypeStruct((1, d), F32)],
        compiler_params=_cp("arbitrary"))(h, gain, tgt)


def _conv_geom(t, seq, c, k, full_width=False):
    halo = 32 if k - 1 > SUBLANES else SUBLANES
    assert k - 1 <= halo
    tm = min(256 if halo > SUBLANES else 1024, seq // 2)
    tc = c if full_width else min(512, c)
    assert seq % tm == 0 and tm % halo == 0 and c % tc == 0 and t % seq == 0
    return halo, tm, tc, min(64 if halo > SUBLANES else 128, tm), min(LANES, tc)


def _pre(kind, a, b):
    if kind == "glu":
        return a * _sigmoid(b)
    if kind == "mul":
        return a * b
    return a


def _taps(k):
    return sorted((s % SUBLANES, s // SUBLANES, s) for s in range(k))


def _conv_fwd(name, seq, c, w, x1, c1, x2=None, c2=0, pre=None, bias=None, post=None, cpost=0, live=None):
    t = x1.shape[0]
    k = w.shape[0]
    halo, tm, tc, sr, sl = _conv_geom(t, seq, c, k, live is not None)
    nb, cps = tm // halo, seq // tm
    two = x2 is not None
    has_bias, has_post = bias is not None, post is not None

    def body(*refs):
        it = iter(refs)
        x1c, x1h = next(it), next(it)
        x2c, x2h = (next(it), next(it)) if two else (None, None)
        w_ref = next(it)
        b_ref = next(it) if has_bias else None
        p_ref = next(it) if has_post else None
        o_ref = next(it)
        y_ref = next(it) if has_post else None
        xs = next(it)
        first = (pl.program_id(1) % cps) == 0
        hv = _pre(pre, x1h[...].astype(F32), x2h[...].astype(F32) if two else None)
        xs[0:halo, :] = jnp.where(first, 0.0, hv)
        xs[halo:halo + tm, :] = _pre(pre, x1c[...].astype(F32), x2c[...].astype(F32) if two else None)
        for l0 in range(0, tc, sl):
            ls = slice(l0, l0 + sl)
            for r0 in range(0, tm, sr):
                win = xs[r0:r0 + sr + halo, ls]
                acc = jnp.zeros((sr, sl), F32)
                rolled = {}
                for r, q, s in _taps(k if live is None else live[l0 // sl]):
                    if r not in rolled:
                        rolled[r] = win if r == 0 else pltpu.roll(win, r, 0)
                    lo = halo - SUBLANES * q
                    acc = acc + w_ref[k - 1 - s:k - s, ls] * rolled[r][lo:lo + sr]
                if has_bias:
                    acc = acc + b_ref[:, ls]
                o_ref[r0:r0 + sr, ls] = acc.astype(o_ref.dtype)
                if has_post:
                    y_ref[r0:r0 + sr, ls] = (acc * p_ref[r0:r0 + sr, ls].astype(F32)).astype(y_ref.dtype)

    def cur(off):
        return pl.BlockSpec((tm, tc), lambda j, i: (i, off // tc + j))

    def prev(off):
        return pl.BlockSpec((halo, tc), lambda j, i: (jnp.maximum(i * nb - 1, 0), off // tc + j))

    ins, specs = [x1, x1], [cur(c1), prev(c1)]
    if two:
        ins += [x2, x2]
        specs += [cur(c2), prev(c2)]
    ins.append(w)
    specs.append(pl.BlockSpec((k, tc), lambda j, i: (0, j)))
    if has_bias:
        ins.append(bias)
        specs.append(pl.BlockSpec((1, tc), lambda j, i: (0, j)))
    if has_post:
        ins.append(post)
        specs.append(cur(cpost))
    o_spec = pl.BlockSpec((tm, tc), lambda j, i: (i, j))
    shp = jax.ShapeDtypeStruct((t, c), BF16)
    return pl.pallas_call(
        body, name=name, grid=(c // tc, t // tm), in_specs=specs,
        out_specs=[o_spec, o_spec] if has_post else o_spec,
        out_shape=[shp, shp] if has_post else shp,
        scratch_shapes=[pltpu.VMEM((halo + tm, tc), F32)],
        compiler_params=_cp("parallel", "parallel"))(*ins)


def _conv_bwd(name, seq, c, w, d1, cd1, d2=None, cd2=0, dpre=None,
              x1=None, c1=0, x2=None, c2=0, pre=None, live=None, dep=None):
    t = d1.shape[0]
    k = w.shape[0]
    assert live is None or x1 is None
    halo, tm, tc, sr, sl = _conv_geom(t, seq, c, k, live is not None)
    nb, cps = tm // halo, seq // tm
    nchunks = t // tm
    dtwo, xtwo, has_x = d2 is not None, x2 is not None, x1 is not None

    def body(*refs):
        it = iter(refs)
        d1c, d1n = next(it), next(it)
        d2c, d2n = (next(it), next(it)) if dtwo else (None, None)
        x1c, x1h = (next(it), next(it)) if has_x else (None, None)
        x2c, x2h = (next(it), next(it)) if xtwo else (None, None)
        w_ref = next(it)
        if dep is not None:
            next(it)
        dx_ref = next(it)
        dw_ref = next(it) if has_x else None
        ds = next(it)
        xs = next(it) if has_x else None
        i = pl.program_id(1)
        last = (i % cps) == cps - 1
        ds[0:tm, :] = _pre(dpre, d1c[...].astype(F32), d2c[...].astype(F32) if dtwo else None)
        nv = _pre(dpre, d1n[...].astype(F32), d2n[...].astype(F32) if dtwo else None)
        ds[tm:tm + halo, :] = jnp.where(last, 0.0, nv)
        if has_x:
            first = (i % cps) == 0
            hv = _pre(pre, x1h[...].astype(F32), x2h[...].astype(F32) if xtwo else None)
            xs[0:halo, :] = jnp.where(first, 0.0, hv)
            xs[halo:halo + tm, :] = _pre(pre, x1c[...].astype(F32), x2c[...].astype(F32) if xtwo else None)

            @pl.when(i == 0)
            def _():
                dw_ref[...] = jnp.zeros_like(dw_ref)

        for l0 in range(0, tc, sl):
            ls = slice(l0, l0 + sl)
            for r0 in range(0, tm, sr):
                win = ds[r0:r0 + sr + halo, ls]
                nrow = sr + halo
                acc = jnp.zeros((sr, sl), F32)
                rolled = {}
                for r, q, s in _taps(k if live is None else live[l0 // sl]):
                    if r not in rolled:
                        rolled[r] = win if r == 0 else pltpu.roll(win, nrow - r, 0)
                    lo = SUBLANES * q
                    acc = acc + w_ref[k - 1 - s:k - s, ls] * rolled[r][lo:lo + sr]
                dx_ref[r0:r0 + sr, ls] = acc.astype(dx_ref.dtype)
                if has_x:
                    dcur = win[0:sr]
                    xwin = xs[r0:r0 + sr + halo, ls]
                    xrolled = {}
                    for r, q, s in _taps(k):
                        if r not in xrolled:
                            xrolled[r] = xwin if r == 0 else pltpu.roll(xwin, r, 0)
                        lo = halo - SUBLANES * q
                        part = jnp.sum(dcur * xrolled[r][lo:lo + sr], axis=0, keepdims=True)
                        dw_ref[k - 1 - s:k - s, ls] += part

    def cur(off):
        return pl.BlockSpec((tm, tc), lambda j, i: (i, off // tc + j))

    def prev(off):
        return pl.BlockSpec((halo, tc), lambda j, i: (jnp.maximum(i * nb - 1, 0), off // tc + j))

    def nxt(off):
        return pl.BlockSpec((halo, tc),
                            lambda j, i: (jnp.minimum((i + 1) * nb, nchunks * nb - 1), off // tc + j))

    ins, specs = [d1, d1], [cur(cd1), nxt(cd1)]
    if dtwo:
        ins += [d2, d2]
        specs += [cur(cd2), nxt(cd2)]
    if has_x:
        ins += [x1, x1]
        specs += [cur(c1), prev(c1)]
    if xtwo:
        ins += [x2, x2]
        specs += [cur(c2), prev(c2)]
    ins.append(w)
    specs.append(pl.BlockSpec((k, tc), lambda j, i: (0, j)))
    if dep is not None:
        ins.append(dep)
        specs.append(ANY)
    o_specs = [pl.BlockSpec((tm, tc), lambda j, i: (i, j))]
    o_shapes = [jax.ShapeDtypeStruct((t, c), BF16)]
    scratch = [pltpu.VMEM((tm + halo, tc), F32)]
    if has_x:
        o_specs.append(pl.BlockSpec((k, tc), lambda j, i: (0, j)))
        o_shapes.append(jax.ShapeDtypeStruct((k, c), F32))
        scratch.append(pltpu.VMEM((halo + tm, tc), F32))
    out = pl.pallas_call(
        body, name=name, grid=(c // tc, t // tm), in_specs=specs, out_specs=o_specs,
        out_shape=o_shapes, scratch_shapes=scratch,
        compiler_params=_cp("parallel", "arbitrary"))(*ins)
    return out if has_x else out[0]


def _pool_taps(c):
    kmax = max(POOL_WINDOWS)
    grp = c // len(POOL_WINDOWS)
    cols = []
    for wdw in POOL_WINDOWS:
        col = jnp.concatenate([jnp.zeros((kmax - wdw,), F32), jnp.ones((wdw,), F32)])
        cols.append(jnp.tile(col[:, None], (1, grp)))
    return jnp.concatenate(cols, axis=1)


def _pool_live(c):
    grp, sl = c // len(POOL_WINDOWS), min(LANES, c)
    return tuple(max(POOL_WINDOWS[g] for g in range(l0 // grp, (l0 + sl - 1) // grp + 1))
                 for l0 in range(0, c, sl))


def _counts(i, tr, seq, grp):
    pos = (i * tr + lax.broadcasted_iota(jnp.int32, (tr, 1), 0)) % seq + 1
    return [1.0 / jnp.minimum(pos, wdw).astype(F32) for wdw in POOL_WINDOWS]


def _ln_stats(a2):
    mu = jnp.mean(a2, axis=-1, keepdims=True)
    xc = a2 - mu
    rstd = lax.rsqrt(jnp.mean(xc * xc, axis=-1, keepdims=True) + EPS)
    return xc * rstd, rstd


def _even_fwd(name, seq, a2, ws, u, ln_g, ln_b, w_pool, scale):
    t, c = a2.shape
    ng = len(POOL_WINDOWS)
    grp = c // ng
    tr = _tile(t, 256, SUBLANES)

    def body(a_ref, ws_ref, b_ref, g_ref, bb_ref, wp_ref, sc_ref, z_ref, pm_ref):
        xhat, _ = _ln_stats(a_ref[...].astype(F32))
        l = xhat * g_ref[...] + bb_ref[...]
        z_ref[:, 0:c] = (l * _sigmoid(l)).astype(BF16)
        inv = _counts(pl.program_id(0), tr, seq, grp)
        for g in range(ng):
            gs = slice(g * grp, (g + 1) * grp)
            pm = (ws_ref[:, gs].astype(F32) * inv[g] - b_ref[:, gs].astype(F32)).astype(BF16)
            pm_ref[:, gs] = pm
            q = jnp.dot(pm, wp_ref[g], preferred_element_type=F32)
            z_ref[:, c + g * grp:c + (g + 1) * grp] = (q * sc_ref[:, gs]).astype(BF16)

    row = pl.BlockSpec((tr, c), lambda i: (i, 0))
    vec = pl.BlockSpec((1, c), lambda i: (0, 0))
    return pl.pallas_call(
        body, name=name, grid=(t // tr,),
        in_specs=[row, row, pl.BlockSpec((tr, c), lambda i: (i, 2)), vec, vec,
                  pl.BlockSpec((ng, grp, grp), lambda i: (0, 0, 0)), vec],
        out_specs=[pl.BlockSpec((tr, 2 * c), lambda i: (i, 0)), row],
        out_shape=[jax.ShapeDtypeStruct((t, 2 * c), BF16), jax.ShapeDtypeStruct((t, c), BF16)],
        compiler_params=_cp("parallel"))(a2, ws, u, ln_g, ln_b, w_pool, scale)


def _even_bwd(name, seq, dz, a2, pm, ln_g, ln_b, w_pool, scale):
    t, c = a2.shape
    ng = len(POOL_WINDOWS)
    grp = c // ng
    tr = _tile(t, 256, SUBLANES)

    def body(dz_ref, a_ref, pm_ref, g_ref, bb_ref, wp_ref, sc_ref,
             da_ref, dws_ref, dpm_ref, vec_ref, dwp_ref):
        i = pl.program_id(0)

        @pl.when(i == 0)
        def _():
            vec_ref[...] = jnp.zeros_like(vec_ref)
            dwp_ref[...] = jnp.zeros_like(dwp_ref)

        xhat, rstd = _ln_stats(a_ref[...].astype(F32))
        gv = g_ref[...]
        l = xhat * gv + bb_ref[...]
        sg = _sigmoid(l)
        dl = dz_ref[:, 0:c].astype(F32) * (sg * (1.0 + l * (1.0 - sg)))
        dxh = dl * gv
        da2 = rstd * (dxh - jnp.mean(dxh, axis=-1, keepdims=True)
                      - xhat * jnp.mean(dxh * xhat, axis=-1, keepdims=True))
        da_ref[...] = da2.astype(BF16)
        vec_ref[0:1, :] += jnp.sum(dl * xhat, axis=0, keepdims=True)
        vec_ref[1:2, :] += jnp.sum(dl, axis=0, keepdims=True)
        vec_ref[2:3, :] += jnp.sum(da2, axis=0, keepdims=True)
        inv = _counts(i, tr, seq, grp)
        for g in range(ng):
            gs = slice(g * grp, (g + 1) * grp)
            pmv = pm_ref[:, gs]
            wp = wp_ref[g]
            dp = dz_ref[:, c + g * grp:c + (g + 1) * grp].astype(F32)
            q = jnp.dot(pmv, wp, preferred_element_type=F32)
            vec_ref[3:4, gs] += jnp.sum(dp * q, axis=0, keepdims=True)
            dq = (dp * sc_ref[:, gs]).astype(BF16)
            dpm = lax.dot_general(dq, wp, NT, preferred_element_type=F32)
            dwp_ref[g] += lax.dot_general(pmv, dq, TN, preferred_element_type=F32)
            dpm_ref[:, gs] = dpm.astype(BF16)
            dws_ref[:, gs] = (dpm * inv[g]).astype(BF16)

    row = pl.BlockSpec((tr, c), lambda i: (i, 0))
    vec = pl.BlockSpec((1, c), lambda i: (0, 0))
    rshape = jax.ShapeDtypeStruct((t, c), BF16)
    return pl.pallas_call(
        body, name=name, grid=(t // tr,),
        in_specs=[pl.BlockSpec((tr, 2 * c), lambda i: (i, 0)), row, row, vec, vec,
                  pl.BlockSpec((ng, grp, grp), lambda i: (0, 0, 0)), vec],
        out_specs=[row, row, row, pl.BlockSpec((SUBLANES, c), lambda i: (0, 0)),
                   pl.BlockSpec((ng, grp, grp), lambda i: (0, 0, 0))],
        out_shape=[rshape, rshape, rshape, jax.ShapeDtypeStruct((SUBLANES, c), F32),
                   jax.ShapeDtypeStruct((ng, grp, grp), F32)],
        compiler_params=_cp("arbitrary"))(dz, a2, pm, ln_g, ln_b, w_pool, scale)


def _even_du(name, u, da1, dbp, dpm):
    t, c = da1.shape
    tr = _tile(t, 256, SUBLANES)

    def body(u_ref, da_ref, dbp_ref, dpm_ref, du_ref):
        val = u_ref[:, 0:c].astype(F32)
        sg = _sigmoid(u_ref[:, c:2 * c].astype(F32))
        da = da_ref[...].astype(F32)
        du_ref[:, 0:c] = (da * sg).astype(BF16)
        du_ref[:, c:2 * c] = (da * val * sg * (1.0 - sg)).astype(BF16)
        du_ref[:, 2 * c:3 * c] = (dbp_ref[...].astype(F32) - dpm_ref[...].astype(F32)).astype(BF16)

    row = pl.BlockSpec((tr, c), lambda i: (i, 0))
    wide = pl.BlockSpec((tr, 3 * c), lambda i: (i, 0))
    return pl.pallas_call(
        body, name=name, grid=(t // tr,), in_specs=[wide, row, row, row], out_specs=wide,
        out_shape=jax.ShapeDtypeStruct((t, 3 * c), BF16),
        compiler_params=_cp("parallel"))(u, da1, dbp, dpm)


def _odd_du(name, u, dy, co, dxc):
    t, c = dy.shape
    tr = _tile(t, 256, SUBLANES)

    def body(u_ref, dy_ref, co_ref, dx_ref, du_ref):
        dx = dx_ref[...].astype(F32)
        du_ref[:, 0:c] = (dy_ref[...].astype(F32) * co_ref[...].astype(F32)).astype(BF16)
        du_ref[:, c:2 * c] = (dx * u_ref[:, 2 * c:3 * c].astype(F32)).astype(BF16)
        du_ref[:, 2 * c:3 * c] = (dx * u_ref[:, c:2 * c].astype(F32)).astype(BF16)

    row = pl.BlockSpec((tr, c), lambda i: (i, 0))
    wide = pl.BlockSpec((tr, 3 * c), lambda i: (i, 0))
    return pl.pallas_call(
        body, name=name, grid=(t // tr,), in_specs=[wide, row, row, row], out_specs=wide,
        out_shape=jax.ShapeDtypeStruct((t, 3 * c), BF16),
        compiler_params=_cp("parallel"))(u, dy, co, dxc)


def _local_step(x, tgt, seq, small, get_w, put_g, sync):
    t, d = x.shape
    c = d // 2
    cw_e, cw_o = small["conv_w_e"], small["conv_w_o"]
    wp = small["w_pool_e"].astype(BF16)
    ptaps = _pool_taps(c)
    row = lambda v: v.reshape(1, -1)

    we = {"w_in": get_w("in_e", x)[0]}
    n0 = _rms_fwd("rms_fwd_mix0", x, row(small["mix_norm_e"]))
    u0 = _mm_nn("mm_in_e", n0, we["w_in"], BF16)
    sync("fwd_a", u0)
    a2 = _conv_fwd("conv_e_fwd", seq, c, cw_e, u0, 0, u0, c, "glu", bias=row(small["conv_b_e"]))
    ws = _conv_fwd("pool_fwd", seq, c, ptaps, u0, 2 * c, live=_pool_live(c))
    z0, pm = _even_fwd("even_fwd", seq, a2, ws, u0, row(small["ln_g_e"]), row(small["ln_b_e"]),
                       wp, row(small["pool_scale_e"]))
    we["w_out"] = get_w("out_e", z0)[0]
    h1 = _mm_nn("mm_out_e", z0, we["w_out"], F32, res=x)
    sync("fwd_b", h1)
    n1 = _rms_fwd("rms_fwd_ffn0", h1, row(small["ffn_norm"][0]))
    wf0 = dict(zip(("w_gate", "w_up"), get_w("gu0", n1)))
    act0, ds0, s0 = _ffn_fwd("ffn0_fwd", n1, wf0["w_gate"], wf0["w_up"])
    dep = sync("fwd_c", act0)
    wf0["w_down"] = get_w("down0", act0)[0]
    h2 = _mm_nn("mm_down0", act0, wf0["w_down"], F32, res=h1, dep=dep)
    dep = sync("fwd_d", h2)
    n2 = _rms_fwd("rms_fwd_mix1", h2, row(small["mix_norm_o"]))
    wo = {"w_in": get_w("in_o", n2)[0]}
    u1 = _mm_nn("mm_in_o", n2, wo["w_in"], BF16, dep=dep)
    co, y1 = _conv_fwd("conv_o_fwd", seq, d, cw_o, u1, d, u1, 2 * d, "mul", post=u1, cpost=0)
    dep = sync("fwd_e", y1)
    wo["w_out"] = get_w("out_o", y1)[0]
    h3 = _mm_nn("mm_out_o", y1, wo["w_out"], F32, res=h2, dep=dep)
    dep = sync("fwd_f", h3)
    n3 = _rms_fwd("rms_fwd_ffn1", h3, row(small["ffn_norm"][1]))
    wf1 = dict(zip(("w_gate", "w_up"), get_w("gu1", n3)))
    act1, ds1, s1 = _ffn_fwd("ffn1_fwd", n3, wf1["w_gate"], wf1["w_up"], dep=dep)
    wf1["w_down"] = get_w("down1", act1)[0]
    h4 = _mm_nn("mm_down1", act1, wf1["w_down"], F32, res=h3)

    dh4, dh4b, d_final, lsum = _loss_head("loss_head", h4, row(small["final_norm"]), tgt)

    def ffn_bwd(tag, dh, dhb, h_in, gain, n, dsilu, silu, act, w, dep):
        dg, dup = _ffn_bwd_act("ffn%s_bwd_act" % tag, dhb, w["w_down"], dsilu, silu, dep=dep)
        dwd = _mm_tn("mm_dwd%s" % tag, act, dhb, BF16)
        dwg = _mm_tn("mm_dwg%s" % tag, dg, n, BF16, dep=sync("bwd_ffn" + tag, dwd))
        dwu = _mm_tn("mm_dwu%s" % tag, dup, n, BF16)
        dn = _mm_nn("mm_ffn_dn%s" % tag, [dg, dup], [w["w_gate"], w["w_up"]], BF16)
        dh_in, dhb_in, dgain = _rms_bwd("rms_bwd_ffn%s" % tag, h_in, gain, dn, dh)
        dep = put_g("ffn" + tag, {"w_gate": dwg, "w_up": dwu, "w_down": dwd})
        return dh_in, dhb_in, dgain, dep

    dh3, dh3b, d_ffn1, dep = ffn_bwd("1", dh4, dh4b, h3, row(small["ffn_norm"][1]), n3, ds1, s1,
                                     act1, wf1, None)
    dw_out_o = _mm_tn("mm_dw_out_o", y1, dh3b, BF16, dep=dep)
    dy1 = _mm_nt("mm_dy_o", dh3b, wo["w_out"], BF16, dep=sync("bwd_mix_o", dw_out_o))
    dxc, dcw_o = _conv_bwd("conv_o_bwd", seq, d, cw_o, dy1, 0, u1, 0, "mul",
                           x1=u1, c1=d, x2=u1, c2=2 * d, pre="mul")
    du1 = _odd_du("odd_du", u1, dy1, co, dxc)
    dw_in_o = _mm_tn("mm_dw_in_o", n2, du1, BF16)
    dn2 = _mm_nt("mm_dn_o", du1, wo["w_in"], BF16)
    dh2, dh2b, d_mix_o = _rms_bwd("rms_bwd_mix1", h2, row(small["mix_norm_o"]), dn2, dh3)
    dep = put_g("mix_o", {"w_in": dw_in_o, "w_out": dw_out_o})

    dh1, dh1b, d_ffn0, dep = ffn_bwd("0", dh2, dh2b, h1, row(small["ffn_norm"][0]), n1, ds0, s0,
                                     act0, wf0, dep)
    dw_out_e = _mm_tn("mm_dw_out_e", z0, dh1b, BF16, dep=dep)
    dz0 = _mm_nt("mm_dz_e", dh1b, we["w_out"], BF16, dep=sync("bwd_mix_e", dw_out_e))
    da2, dws, dpm, vecs, dwp = _even_bwd("even_bwd", seq, dz0, a2, pm, row(small["ln_g_e"]),
                                         row(small["ln_b_e"]), wp, row(small["pool_scale_e"]))
    dep = put_g("small", {"conv_b_e": vecs[2], "ln_g_e": vecs[0], "ln_b_e": vecs[1],
                          "w_pool_e": dwp, "pool_scale_e": vecs[3], "mix_norm_o": d_mix_o[0],
                          "conv_w_o": dcw_o, "ffn_norm": jnp.concatenate([d_ffn0, d_ffn1], axis=0),
                          "final_norm": d_final[0], "loss": lsum})
    da1, dcw_e = _conv_bwd("conv_e_bwd", seq, c, cw_e, da2, 0, x1=u0, c1=0, x2=u0, c2=c, pre="glu", dep=dep)
    dbp = _conv_bwd("pool_bwd", seq, c, ptaps, dws, 0, live=_pool_live(c))
    du0 = _even_du("even_du", u0, da1, dbp, dpm)
    dw_in_e = _mm_tn("mm_dw_in_e", n0, du0, BF16)
    dep = put_g("mix_e", {"w_in": dw_in_e, "w_out": dw_out_e})
    dn0 = _mm_nt("mm_dn_e", du0, we["w_in"], BF16, dep=dep)
    dx, _, d_mix_e = _rms_bwd("rms_bwd_mix0", x, row(small["mix_norm_e"]), dn0, dh1, bf16_too=False)
    return dx, {"conv_w_e": dcw_e, "mix_norm_e": d_mix_e[0]}


def _place():
    x, y, c = (lax.axis_index(a) for a in MESH_AXES)
    return x, y, c


def _index(p):
    return 4 * p[0] + 2 * p[1] + p[2]


def _slab(ref, kind, d, n):
    if kind == "blk":
        return ref.at[d]
    return ref.at[:, pl.ds(pl.multiple_of(d * n, LANES), n)]


HBM = pl.BlockSpec(memory_space=pltpu.HBM)
SEM = pl.BlockSpec(memory_space=pltpu.SEMAPHORE)
EFFECT = pltpu.SideEffectType.DATAFLOW_SIDE_EFFECTING
NCHIPS = 4


def _in_hbm(a):
    return pltpu.with_memory_space_constraint(a, pltpu.HBM)


def _gathered_shape(s, kind):
    m, n = s.shape
    return (NDEV, m, n) if kind == "blk" else (m, NDEV * n)


def _first_targets():
    x, y, c = _place()
    return [(x, y, 1 - c), (1 - x, y, c), (x, 1 - y, c), (1 - x, 1 - y, c)]


def _gather_start(name, shards, kinds, after):
    na = len(shards)

    def body(*refs):
        x_refs, land_refs = refs[:na], refs[na:2 * na]
        send_sems, recv_sems = refs[2 * na + 1], refs[2 * na + 2]
        token = refs[-1]
        me = _index(_place())
        for a in range(na):
            for k, to in enumerate(_first_targets()):
                pltpu.make_async_remote_copy(
                    src_ref=x_refs[a], dst_ref=_slab(land_refs[a], kinds[a], me, shards[a].shape[1]),
                    send_sem=send_sems.at[4 * a + k], recv_sem=recv_sems.at[4 * a + k],
                    device_id=to, device_id_type=MESH).start()
        token[...] = jnp.zeros_like(token)

    lands = [lax.empty(_gathered_shape(s, k), s.dtype) for s, k in zip(shards, kinds)]
    outs = pl.pallas_call(
        body, name=name,
        out_shape=(pltpu.SemaphoreType.DMA((4 * na,)), pltpu.SemaphoreType.DMA((4 * na,)),
                   *[pltpu.HBM(s.shape, s.dtype) for s in shards],
                   *[pltpu.HBM(l.shape, l.dtype) for l in lands],
                   jax.ShapeDtypeStruct((SUBLANES, LANES), F32)),
        in_specs=[HBM] * (2 * na) + [ANY],
        out_specs=(SEM, SEM, *[HBM] * (2 * na), pl.BlockSpec(memory_space=pltpu.VMEM)),
        input_output_aliases={i: 2 + i for i in range(2 * na)},
        compiler_params=pltpu.CompilerParams(has_side_effects=EFFECT),
    )(*[_in_hbm(s) for s in shards], *[_in_hbm(l) for l in lands], after)
    return outs[0], outs[1], outs[2:2 + na], outs[2 + na:2 + 2 * na], outs[-1]


def _gather_wait(name, started, kinds, after):
    send_sems, recv_sems, shards, lands, _ = started
    na = len(shards)

    def body(*refs):
        x_refs, land_refs = refs[:na], refs[na:2 * na]
        s_sems, r_sems = refs[2 * na], refs[2 * na + 1]
        for a in range(na):
            for k, frm in enumerate(_first_targets()):
                cp = pltpu.make_async_remote_copy(
                    src_ref=x_refs[a],
                    dst_ref=_slab(land_refs[a], kinds[a], _index(frm), shards[a].shape[1]),
                    send_sem=s_sems.at[4 * a + k], recv_sem=r_sems.at[4 * a + k],
                    device_id=frm, device_id_type=MESH)
                cp.wait_send()
                cp.wait_recv()

    outs = pl.pallas_call(
        body, name=name,
        out_shape=(*[pltpu.HBM(s.shape, s.dtype) for s in shards],
                   *[pltpu.HBM(l.shape, l.dtype) for l in lands]),
        in_specs=[HBM] * (2 * na) + [SEM, SEM, ANY], out_specs=[HBM] * (2 * na),
        input_output_aliases={i: i for i in range(2 * na)},
        compiler_params=pltpu.CompilerParams(has_side_effects=EFFECT),
    )(*shards, *lands, send_sems, recv_sems, after)
    return outs[:na], outs[na:]


def _split_start(name, bufs, ncopies, plan, after):
    nb = len(bufs)

    def body(*refs):
        send_sems, recv_sems, token = refs[nb + 1], refs[nb + 2], refs[-1]
        for k, (src, dst, to, _) in enumerate(plan(refs[:nb])):
            pltpu.make_async_remote_copy(src_ref=src, dst_ref=dst, send_sem=send_sems.at[k],
                                         recv_sem=recv_sems.at[k], device_id=to, device_id_type=MESH).start()
        token[...] = jnp.zeros_like(token)

    outs = pl.pallas_call(
        body, name=name,
        out_shape=(pltpu.SemaphoreType.DMA((ncopies,)), pltpu.SemaphoreType.DMA((ncopies,)),
                   *[pltpu.HBM(b.shape, b.dtype) for b in bufs],
                   jax.ShapeDtypeStruct((SUBLANES, LANES), F32)),
        in_specs=[HBM] * nb + [ANY],
        out_specs=(SEM, SEM, *[HBM] * nb, pl.BlockSpec(memory_space=pltpu.VMEM)),
        input_output_aliases={i: 2 + i for i in range(nb)},
        compiler_params=pltpu.CompilerParams(has_side_effects=EFFECT),
    )(*[_in_hbm(b) for b in bufs], after)
    return outs[0], outs[1], list(outs[2:2 + nb]), outs[-1]


def _split_wait(name, started, plan, after):
    send_sems, recv_sems, bufs, _ = started
    nb = len(bufs)

    def body(*refs):
        s_sems, r_sems = refs[nb], refs[nb + 1]
        for k, (src, _, to, landed) in enumerate(plan(refs[:nb])):
            cp = pltpu.make_async_remote_copy(src_ref=src, dst_ref=landed, send_sem=s_sems.at[k],
                                              recv_sem=r_sems.at[k], device_id=to, device_id_type=MESH)
            cp.wait_send()
            cp.wait_recv()

    outs = pl.pallas_call(
        body, name=name, out_shape=tuple(pltpu.HBM(b.shape, b.dtype) for b in bufs),
        in_specs=[HBM] * nb + [SEM, SEM, ANY], out_specs=[HBM] * nb,
        input_output_aliases={i: i for i in range(nb)},
        compiler_params=pltpu.CompilerParams(has_side_effects=EFFECT),
    )(*bufs, send_sems, recv_sems, after)
    return list(outs)


def _forward_plan(kinds, nloc):
    def plan(lands):
        x, y, c = _place()
        out = []
        for a, land in enumerate(lands):
            for chip in [(1 - x, y), (x, 1 - y), (1 - x, 1 - y)]:
                mine = _slab(land, kinds[a], _index((*chip, c)), nloc[a])
                out.append((mine, mine, (x, y, 1 - c), _slab(land, kinds[a], _index((*chip, 1 - c)), nloc[a])))
        return out
    return plan


def _own_copy(name, shard, land, kind, me):
    m, n = shard.shape
    tr = _tile(m, max(SUBLANES, 1048576 // n), SUBLANES)

    def body(s_ref, x_ref, land_ref, o_ref):
        o_ref[...] = x_ref[...]

    if kind == "blk":
        o_spec = pl.BlockSpec((None, tr, n), lambda i, s: (s[0], i, 0))
    else:
        o_spec = pl.BlockSpec((tr, n), lambda i, s: (i, s[0]))
    return pl.pallas_call(
        body, name=name,
        grid_spec=pltpu.PrefetchScalarGridSpec(
            num_scalar_prefetch=1, grid=(m // tr,),
            in_specs=[pl.BlockSpec((tr, n), lambda i, s: (i, 0)), ANY], out_specs=o_spec),
        out_shape=jax.ShapeDtypeStruct(land.shape, land.dtype),
        input_output_aliases={2: 0}, compiler_params=_cp("parallel"))(me, shard, land)


def _everyone_plan(refs):
    x, y, c = _place()
    out = []
    for dx, dy, dc in [(a, b, e) for a in (0, 1) for b in (0, 1) for e in (0, 1)][1:]:
        peer = (x ^ dx, y ^ dy, c ^ dc)
        out.append((refs[0], refs[1].at[_index((x, y, c))], peer, refs[1].at[_index(peer)]))
    return out


def _pair_plan(kinds, nloc):
    na = len(kinds)

    def plan(refs):
        x, y, c = _place()
        out = []
        for a in range(na):
            for j in range(NCHIPS):
                dst = refs[na + a].at[j]
                out.append((_slab(refs[a], kinds[a], 2 * j + (1 - c), nloc[a]), dst, (x, y, 1 - c), dst))
        return out
    return plan


def _chip_sum(name, fulls, kind, n, from_sibs, place):
    na = len(fulls)
    _, m, _ = from_sibs[0].shape
    tr = _tile(m, max(SUBLANES, 1048576 // (n * min(na, 2))), SUBLANES)

    def body(s_ref, *refs):
        for a in range(na):
            mine_ref, sib_ref = refs[a], refs[na + a]
            csum_ref, land_ref = refs[2 * na + a], refs[3 * na + a]
            csum_ref[...] = (mine_ref[...].astype(F32) + sib_ref[...].astype(F32)).astype(csum_ref.dtype)

            @pl.when(pl.program_id(1) == s_ref[1])
            def _(csum_ref=csum_ref, land_ref=land_ref):
                land_ref[...] = csum_ref[...]

    if kind == "blk":
        mine_spec = pl.BlockSpec((None, tr, n), lambda i, j, s: (2 * j + s[0], i, 0))
    else:
        mine_spec = pl.BlockSpec((tr, n), lambda i, j, s: (i, 2 * j + s[0]))
    slot = pl.BlockSpec((None, tr, n), lambda i, j, s: (j, i, 0))
    own = pl.BlockSpec((None, tr, n), lambda i, j, s: (s[1], i, 0))
    shp = jax.ShapeDtypeStruct((NCHIPS, m, n), from_sibs[0].dtype)
    outs = pl.pallas_call(
        body, name=name,
        grid_spec=pltpu.PrefetchScalarGridSpec(
            num_scalar_prefetch=1, grid=(m // tr, NCHIPS), in_specs=[mine_spec] * na + [slot] * na,
            out_specs=[slot] * na + [own] * na),
        out_shape=[shp] * (2 * na), compiler_params=_cp("parallel", "arbitrary"))(place, *fulls, *from_sibs)
    return [(outs[a], outs[na + a]) for a in range(na)]


def _other_chips():
    x, y, c = _place()
    return [(1 - x, y, c), (x, 1 - y, c), (1 - x, 1 - y, c)]


def _scatter_start(name, csums, lands, after):
    na = len(csums)

    def body(*refs):
        c_refs, land_refs = refs[:na], refs[na:2 * na]
        send_sems, recv_sems = refs[2 * na + 1], refs[2 * na + 2]
        token = refs[-1]
        x, y, _ = _place()
        for a in range(na):
            for k, to in enumerate(_other_chips()):
                pltpu.make_async_remote_copy(
                    src_ref=c_refs[a].at[2 * to[0] + to[1]], dst_ref=land_refs[a].at[2 * x + y],
                    send_sem=send_sems.at[3 * a + k], recv_sem=recv_sems.at[3 * a + k],
                    device_id=to, device_id_type=MESH).start()
        token[...] = jnp.zeros_like(token)

    outs = pl.pallas_call(
        body, name=name,
        out_shape=(pltpu.SemaphoreType.DMA((3 * na,)), pltpu.SemaphoreType.DMA((3 * na,)),
                   *[pltpu.HBM(s.shape, s.dtype) for s in csums],
                   *[pltpu.HBM(l.shape, l.dtype) for l in lands],
                   jax.ShapeDtypeStruct((SUBLANES, LANES), F32)),
        in_specs=[HBM] * (2 * na) + [ANY],
        out_specs=(SEM, SEM, *[HBM] * (2 * na), pl.BlockSpec(memory_space=pltpu.VMEM)),
        input_output_aliases={i: 2 + i for i in range(2 * na)},
        compiler_params=pltpu.CompilerParams(has_side_effects=EFFECT),
    )(*[_in_hbm(s) for s in csums], *[_in_hbm(l) for l in lands], after)
    return outs[0], outs[1], outs[2:2 + na], outs[2 + na:2 + 2 * na], outs[-1]


def _scatter_wait(name, started, after):
    send_sems, recv_sems, csums, lands, _ = started
    na = len(csums)

    def body(*refs):
        c_refs, land_refs = refs[:na], refs[na:2 * na]
        s_sems, r_sems = refs[2 * na], refs[2 * na + 1]
        for a in range(na):
            for k, frm in enumerate(_other_chips()):
                cp = pltpu.make_async_remote_copy(
                    src_ref=c_refs[a].at[2 * frm[0] + frm[1]], dst_ref=land_refs[a].at[2 * frm[0] + frm[1]],
                    send_sem=s_sems.at[3 * a + k], recv_sem=r_sems.at[3 * a + k],
                    device_id=frm, device_id_type=MESH)
                cp.wait_send()
                cp.wait_recv()

    outs = pl.pallas_call(
        body, name=name,
        out_shape=(*[pltpu.HBM(s.shape, s.dtype) for s in csums],
                   *[pltpu.HBM(l.shape, l.dtype) for l in lands]),
        in_specs=[HBM] * (2 * na) + [SEM, SEM, ANY], out_specs=[HBM] * (2 * na),
        input_output_aliases={i: i for i in range(2 * na)},
        compiler_params=pltpu.CompilerParams(has_side_effects=EFFECT),
    )(*csums, *lands, send_sems, recv_sems, after)
    return outs[na:]


def _adam_math(w, g, m, v):
    m = ADAM_B1 * m + (1.0 - ADAM_B1) * g
    v = ADAM_B2 * v + (1.0 - ADAM_B2) * (g * g)
    m_hat = m / (1.0 - ADAM_B1 ** ADAM_STEP)
    v_hat = v / (1.0 - ADAM_B2 ** ADAM_STEP)
    delta = -ADAM_LR * (m_hat / (jnp.sqrt(v_hat) + ADAM_EPS) + ADAM_WD * w)
    return delta, m, v


def _sum_adamw(name, parts, w, m, v, layer, prev=None, dep=None):
    nl, r, c = w.shape
    nparts = parts.shape[0]
    tr = _tile(r, max(SUBLANES, 360448 // c), SUBLANES)

    def body(p_ref, w_ref, m_ref, v_ref, *rest):
        g_ref, d_ref, mo_ref, vo_ref = rest[-4:]
        g = p_ref[0].astype(F32)
        for s in range(1, nparts):
            g = g + p_ref[s].astype(F32)
        delta, mn, vn = _adam_math(w_ref[...], g, m_ref[...], v_ref[...])
        g_ref[...] = g
        d_ref[...] = delta
        mo_ref[...] = mn
        vo_ref[...] = vn

    row = pl.BlockSpec((None, tr, c), lambda i: (layer, i, 0))
    shp = jax.ShapeDtypeStruct((nl, r, c), F32)
    extra = ([] if prev is None else list(prev)) + ([] if dep is None else [dep])
    return pl.pallas_call(
        body, name=name, grid=(r // tr,),
        in_specs=[pl.BlockSpec((nparts, tr, c), lambda i: (0, i, 0)), row, row, row] + [ANY] * len(extra),
        out_specs=[row, row, row, row], out_shape=[shp, shp, shp, shp],
        input_output_aliases={} if prev is None else {4 + i: i for i in range(4)},
        compiler_params=_cp("parallel"))(parts, w, m, v, *extra)


def _sum_parts(name, parts):
    _, r, c = parts.shape

    def body(p_ref, o_ref):
        g = p_ref[0]
        for s in range(1, NDEV):
            g = g + p_ref[s]
        o_ref[...] = g

    return pl.pallas_call(
        body, name=name, grid=(1,),
        in_specs=[pl.BlockSpec((NDEV, r, c), lambda i: (0, 0, 0))],
        out_specs=pl.BlockSpec((r, c), lambda i: (0, 0)),
        out_shape=jax.ShapeDtypeStruct((r, c), F32), compiler_params=_cp("arbitrary"))(parts)


def _adamw(name, w, g, m, v):
    r, c = w.shape

    def body(w_ref, g_ref, m_ref, v_ref, d_ref, mo_ref, vo_ref):
        delta, mn, vn = _adam_math(w_ref[...], g_ref[...], m_ref[...], v_ref[...])
        d_ref[...] = delta
        mo_ref[...] = mn
        vo_ref[...] = vn

    full = pl.BlockSpec((r, c), lambda i: (0, 0))
    shp = jax.ShapeDtypeStruct((r, c), F32)
    return pl.pallas_call(
        body, name=name, grid=(1,), in_specs=[full] * 4, out_specs=[full] * 3,
        out_shape=[shp] * 3, compiler_params=_cp("arbitrary"))(w, g, m, v)


def _pack(arrays):
    flat = jnp.concatenate([a.reshape(-1) for a in arrays])
    unit = SUBLANES * LANES
    pad = (-flat.shape[0]) % unit
    return jnp.pad(flat, (0, pad)).reshape(-1, LANES)


def _unpack(buf, shapes):
    flat = buf.reshape(-1)
    out, off = [], 0
    for shp in shapes:
        size = 1
        for s in shp:
            size *= s
        out.append(flat[off:off + size].reshape(shp))
        off += size
    return out


WEIGHTS = ["mix_norm_e", "w_in_e", "conv_w_e", "conv_b_e", "ln_g_e", "ln_b_e", "w_pool_e",
           "pool_scale_e", "w_out_e", "mix_norm_o", "w_in_o", "conv_w_o", "w_out_o", "ffn_norm",
           "w_gate", "w_up", "w_down", "final_norm"]
BIG = ["w_in_e", "w_out_e", "w_in_o", "w_out_o", "w_gate", "w_up", "w_down"]
SHARDED_SMALL = {"conv_w_e": 1, "w_pool_e": 1, "mix_norm_o": 0, "conv_w_o": 1}
SMALL = [n for n in WEIGHTS if n not in BIG]


def kernel(x, mix_norm_e, w_in_e, conv_w_e, conv_b_e, ln_g_e, ln_b_e, w_pool_e, pool_scale_e, w_out_e, mix_norm_o, w_in_o, conv_w_o, w_out_o, ffn_norm, w_gate, w_up, w_down, final_norm, loss_target, m_mix_norm_e, m_w_in_e, m_conv_w_e, m_conv_b_e, m_ln_g_e, m_ln_b_e, m_w_pool_e, m_pool_scale_e, m_w_out_e, m_mix_norm_o, m_w_in_o, m_conv_w_o, m_w_out_o, m_ffn_norm, m_w_gate, m_w_up, m_w_down, m_final_norm, v_mix_norm_e, v_w_in_e, v_conv_w_e, v_conv_b_e, v_ln_g_e, v_ln_b_e, v_w_pool_e, v_pool_scale_e, v_w_out_e, v_mix_norm_o, v_w_in_o, v_conv_w_o, v_w_out_o, v_ffn_norm, v_w_gate, v_w_up, v_w_down, v_final_norm):
    wts = dict(zip(WEIGHTS, [mix_norm_e, w_in_e, conv_w_e, conv_b_e, ln_g_e, ln_b_e, w_pool_e, pool_scale_e, w_out_e, mix_norm_o, w_in_o, conv_w_o, w_out_o, ffn_norm, w_gate, w_up, w_down, final_norm]))
    mom = dict(zip(WEIGHTS, [m_mix_norm_e, m_w_in_e, m_conv_w_e, m_conv_b_e, m_ln_g_e, m_ln_b_e, m_w_pool_e, m_pool_scale_e, m_w_out_e, m_mix_norm_o, m_w_in_o, m_conv_w_o, m_w_out_o, m_ffn_norm, m_w_gate, m_w_up, m_w_down, m_final_norm]))
    var = dict(zip(WEIGHTS, [v_mix_norm_e, v_w_in_e, v_conv_w_e, v_conv_b_e, v_ln_g_e, v_ln_b_e, v_w_pool_e, v_pool_scale_e, v_w_out_e, v_mix_norm_o, v_w_in_o, v_conv_w_o, v_w_out_o, v_ffn_norm, v_w_gate, v_w_up, v_w_down, v_final_norm]))
    bsz, seq, d = x.shape
    t = bsz * seq
    me = _index(_place())
    me_arr = jnp.reshape(me, (1,)).astype(jnp.int32)

    sh_names = list(SHARDED_SMALL)
    sh_local = [wts[n][0] for n in sh_names]
    packed = _pack(sh_local)
    params_st = _split_start("small_params_start", [packed, lax.empty((NDEV,) + packed.shape, F32)], NDEV - 1,
                             _everyone_plan, x)

    for state in (wts, mom, var):
        for n in ("w_gate", "w_up"):
            state[n] = jnp.swapaxes(state[n], 1, 2)
    bf = lambda a: a.astype(BF16)
    mix_kinds, ffn_kinds = ["col", "blk"], ["blk", "blk", "blk"]
    ffn_names = ("w_gate", "w_up", "w_down")
    groups = {
        "mix_e": ([w_in_e.shape[2], d], mix_kinds, [("w_in_e", 0), ("w_out_e", 0)]),
        "ffn0": ([d, d, d], ffn_kinds, [(n, 0) for n in ffn_names]),
        "mix_o": ([w_in_o.shape[2], d], mix_kinds, [("w_in_o", 0), ("w_out_o", 0)]),
        "ffn1": ([d, d, d], ffn_kinds, [(n, 1) for n in ffn_names]),
    }
    gathers = {
        "in_e": ([bf(w_in_e[0])], ["col"]), "out_e": ([bf(w_out_e[0])], ["blk"]),
        "gu0": ([bf(wts["w_gate"][0]), bf(wts["w_up"][0])], ["blk", "blk"]), "down0": ([bf(w_down[0])], ["blk"]),
        "in_o": ([bf(w_in_o[0])], ["col"]), "out_o": ([bf(w_out_o[0])], ["blk"]),
        "gu1": ([bf(wts["w_gate"][1]), bf(wts["w_up"][1])], ["blk", "blk"]), "down1": ([bf(w_down[1])], ["blk"]),
    }
    started, prev = {}, params_st[3]
    for grp, (shards, kinds) in gathers.items():
        started[grp] = _gather_start("gather_start_" + grp, shards, kinds, prev)
        prev = started[grp][4]
    all_started = prev[0, 0:1]

    bufs = _split_wait("small_params_wait", params_st, _everyone_plan, prev)
    gathered = _own_copy("small_params_own", bufs[0], bufs[1], "blk", me_arr)
    small = {n: wts[n][0] for n in SMALL if n not in SHARDED_SMALL and n not in ("ffn_norm", "final_norm")}
    small["ffn_norm"], small["final_norm"] = ffn_norm, final_norm
    flat, off = gathered.reshape(NDEV, -1), 0
    for n, a in zip(sh_names, sh_local):
        ax, shp = SHARDED_SMALL[n], a.shape
        blocks = jnp.moveaxis(flat[:, off:off + a.size].reshape((NDEV,) + shp), 0, ax)
        small[n] = blocks.reshape(shp[:ax] + (NDEV * shp[ax],) + shp[ax + 1:])
        off += a.size

    passing, shards_of = {}, {}

    def pass_on(grp, after):
        shards, kinds = gathers[grp]
        shards_of[grp], lands = _gather_wait("gather_wait_" + grp, started[grp], kinds, after)
        plan = _forward_plan(kinds, [s.shape[1] for s in shards])
        passing[grp] = (_split_start("forward_start_" + grp, lands, 3 * len(lands), plan, after), plan)
        return passing[grp][0][3]

    def get_w(grp, after):
        if grp not in passing:
            after = pass_on(grp, after)
        st, plan = passing[grp]
        lands = _split_wait("forward_wait_" + grp, st, plan, after)
        full = [_own_copy("own_copy_%s%d" % (grp, a), shards_of[grp][a], lands[a], gathers[grp][1][a], me_arr)
                for a in range(len(lands))]
        return [f.reshape(-1, d) if kind == "blk" else f for f, kind in zip(full, gathers[grp][1])]

    cx, cy, cc = _place()
    place = jnp.stack([cc, 2 * cx + cy]).astype(jnp.int32)
    bwd_order = ["ffn1", "mix_o", "ffn0", "mix_e"]
    pairing, pending, results = {}, {}, {}

    late_names = ["conv_w_e", "mix_norm_e"]
    early_names = [n for n in SMALL if n not in late_names] + ["loss"]
    small_sent = {}

    def send_small(tag, arrays, after):
        mine = _pack(arrays)
        small_sent[tag] = _split_start(tag + "_start", [mine, lax.empty((NDEV,) + mine.shape, F32)], NDEV - 1,
                                       _everyone_plan, after)
        return small_sent[tag][3]

    def summed_small(tag, shapes, after):
        bufs = _split_wait(tag + "_wait", small_sent[tag], _everyone_plan, after)
        parts = _own_copy(tag + "_own", bufs[0], bufs[1], "blk", me_arr)
        return _unpack(_sum_parts(tag + "_sum", parts), shapes)

    def put_g(grp, grads):
        if grp == "small":
            small_sent["shapes"] = [grads[n].shape for n in early_names]
            return send_small("small_grads", [grads[n] for n in early_names], place)
        nloc, kinds, _ = groups[grp]
        if len(kinds) == 2:
            fulls = [grads["w_in"], grads["w_out"].reshape(NDEV, -1, d)]
        else:
            fulls = [grads[n].reshape(NDEV, -1, d) for n in ffn_names]
        empties = []
        for g, kind, n in zip(fulls, kinds, nloc):
            empties.append(lax.empty((NCHIPS, g.shape[1] if kind == "blk" else g.shape[0], n), g.dtype))
        plan = _pair_plan(kinds, nloc)
        pairing[grp] = (_split_start("pair_start_" + grp, fulls + empties, NCHIPS * len(fulls), plan, place),
                        plan, kinds, nloc)
        token = pairing[grp][0][3]
        return send_sums(grp, token) if grp == bwd_order[-1] else token

    def send_sums(grp, after):
        st, plan, kinds, nloc = pairing[grp]
        bufs = _split_wait("pair_wait_" + grp, st, plan, after)
        na = len(kinds)
        if len(set(kinds)) == 1:
            sums = _chip_sum("chip_sum_" + grp, bufs[:na], kinds[0], nloc[0], bufs[na:], place)
        else:
            sums = [_chip_sum("chip_sum_%s%d" % (grp, a), [bufs[a]], kinds[a], nloc[a], [bufs[na + a]], place)[0]
                    for a in range(na)]
        pending[grp] = _scatter_start("scatter_start_" + grp, [s[0] for s in sums], [s[1] for s in sums], after)
        return pending[grp][4]

    def finish(grp, after):
        lands = _scatter_wait("scatter_wait_" + grp, pending[grp], after)
        dep = None
        for (n, l), parts in zip(groups[grp][2], lands):
            results[n] = _sum_adamw("adamw_%s%d" % (n, l), parts, wts[n], mom[n], var[n], l, results.get(n), dep)
            dep = results[n][1]
        return dep

    fwd_sync = {"fwd_a": ["out_e"], "fwd_b": ["gu0"], "fwd_c": ["down0", "in_o"], "fwd_d": ["out_o"],
                "fwd_e": ["gu1"], "fwd_f": ["down1"]}

    def sync(tag, after):
        if tag in fwd_sync:
            for grp in fwd_sync[tag]:
                after = pass_on(grp, after)
            return after
        if tag == "bwd_mix_o":
            return send_sums("ffn1", after)
        if tag == "bwd_ffn0":
            return finish("ffn1", send_sums("mix_o", after))
        if tag == "bwd_mix_e":
            return finish("mix_o", send_sums("ffn0", after))
        return None

    small["mix_norm_e"] = small["mix_norm_e"] + all_started
    dx, late = _local_step(x.reshape(t, d), loss_target.reshape(t, d), seq, small, get_w, put_g, sync)

    out_g, out_d, out_m, out_v = {}, {}, {}, {}

    dep = finish("ffn0", send_small("last_grads", [late[n] for n in late_names], dx))
    sums = dict(zip(early_names, summed_small("small_grads", small_sent["shapes"], dep)))
    sums.update(zip(late_names, summed_small("last_grads", [late[n].shape for n in late_names], dep)))
    loss = jnp.sum(sums["loss"])
    gs_sum = [sums[n] for n in SMALL]
    local_g = []
    for n, g in zip(SMALL, gs_sum):
        if n in SHARDED_SMALL:
            ax = SHARDED_SMALL[n]
            size = wts[n].shape[ax + 1]
            g = lax.dynamic_slice_in_dim(g, me * size, size, axis=ax)
        local_g.append(g.reshape(wts[n].shape))
    shapes = [wts[n].shape for n in SMALL]
    upd = _adamw("adamw_small", _pack([wts[n] for n in SMALL]), _pack(local_g),
                 _pack([mom[n] for n in SMALL]), _pack([var[n] for n in SMALL]))
    for i, outd in enumerate((out_d, out_m, out_v)):
        for n, a in zip(SMALL, _unpack(upd[i], shapes)):
            outd[n] = a
    for n, g in zip(SMALL, local_g):
        out_g[n] = g

    finish("mix_e", upd[0])
    for n in BIG:
        res = [jnp.swapaxes(a, 1, 2) for a in results[n]] if n in ("w_gate", "w_up") else results[n]
        out_g[n], out_d[n], out_m[n], out_v[n] = res

    return (loss, dx.reshape(bsz, seq, d), *[out_g[n] for n in WEIGHTS], *[out_d[n] for n in WEIGHTS],
            *[out_m[n] for n in WEIGHTS], *[out_v[n] for n in WEIGHTS])
```
